```python
import jax, jax.numpy as jnp
from jax import lax
import numpy as np

D_MODEL = 4096
BATCH = 8
SEQ = 4096
DEPTH = 1

N_META = 16
HEAD_DIM = 64
N_Q_HEADS = 32
N_KV_HEADS = 4
GROUP = N_Q_HEADS // N_KV_HEADS
WINDOW = 128
BLOCK = 128
ROPE_THETA = 10000.0
Q_DIM = N_Q_HEADS * HEAD_DIM
KV_DIM = N_KV_HEADS * HEAD_DIM
CONV_DIM = D_MODEL // 2
CONV_WIDTH = 31
FFN_DIM = ((8 * D_MODEL + 3 * 256 - 1) // (3 * 256)) * 256
IN_DIM = Q_DIM + 2 * KV_DIM + 2 * CONV_DIM + 2 * D_MODEL
EPS = 1e-6

kernel_name = "hybrid_swa_sink_conformer_conv_gated_block"


def rms_norm(x, g):
    xf = x.astype(jnp.float32)
    y = xf * lax.rsqrt(jnp.mean(xf * xf, axis=-1, keepdims=True) + EPS)
    return (y * g.astype(jnp.float32)).astype(x.dtype)


def layer_norm(x, g, b):
    xf = x.astype(jnp.float32)
    mu = jnp.mean(xf, axis=-1, keepdims=True)
    xc = xf - mu
    y = xc * lax.rsqrt(jnp.mean(xc * xc, axis=-1, keepdims=True) + EPS)
    return (y * g.astype(jnp.float32) + b.astype(jnp.float32)).astype(x.dtype)


def rope_tables(length):
    pos = jnp.arange(length, dtype=jnp.float32)
    inv_freq = ROPE_THETA ** (-jnp.arange(0, HEAD_DIM, 2, dtype=jnp.float32) / HEAD_DIM)
    ang = pos[:, None] * inv_freq[None, :]
    return jnp.cos(ang), jnp.sin(ang)


def apply_rope(x, cos, sin):
    xf = x.astype(jnp.float32)
    x1, x2 = jnp.split(xf, 2, axis=-1)
    c = cos[None, :, None, :]
    s = sin[None, :, None, :]
    return jnp.concatenate([x1 * c - x2 * s, x2 * c + x1 * s], axis=-1).astype(x.dtype)


def sliding_window_attention(q, k, v, sinks):
    b, length = q.shape[0], q.shape[1]
    pad = BLOCK - N_META
    padded = length + pad
    nb = padded // BLOCK
    padw = ((0, 0), (pad, 0), (0, 0), (0, 0))
    qb = jnp.pad(q, padw).reshape(b, nb, BLOCK, N_KV_HEADS, GROUP, HEAD_DIM)
    kb = jnp.pad(k, padw).reshape(b, nb, BLOCK, N_KV_HEADS, HEAD_DIM)
    vb = jnp.pad(v, padw).reshape(b, nb, BLOCK, N_KV_HEADS, HEAD_DIM)
    k_prev = jnp.concatenate([jnp.zeros_like(kb[:, :1]), kb[:, :-1]], axis=1)
    v_prev = jnp.concatenate([jnp.zeros_like(vb[:, :1]), vb[:, :-1]], axis=1)
    k_band = jnp.concatenate([k_prev, kb], axis=2)
    v_band = jnp.concatenate([v_prev, vb], axis=2)
    k_meta = k[:, :N_META]
    v_meta = v[:, :N_META]
    scale = HEAD_DIM ** -0.5

    s_band = jnp.einsum('bnqhgd,bnkhd->bnhgqk', qb, k_band).astype(jnp.float32) * scale
    s_meta = jnp.einsum('bnqhgd,bmhd->bnhgqm', qb, k_meta).astype(jnp.float32) * scale

    blk = jnp.arange(nb)[:, None]
    q_pos = blk * BLOCK + jnp.arange(BLOCK)[None, :] - pad
    k_pos = (blk - 1) * BLOCK + jnp.arange(2 * BLOCK)[None, :] - pad
    qp = q_pos[:, :, None]
    kp = k_pos[:, None, :]
    band_mask = (kp >= N_META) & (kp <= qp) & (qp - kp < WINDOW)
    meta_mask = jnp.arange(N_META)[None, None, :] <= qp

    neg = jnp.float32(-jnp.inf)
    s_band = jnp.where(band_mask[None, :, None, None], s_band, neg)
    s_meta = jnp.where(meta_mask[None, :, None, None], s_meta, neg)
    sink = jnp.broadcast_to(
        sinks.astype(jnp.float32).reshape(N_KV_HEADS, GROUP)[None, None, :, :, None, None],
        s_band.shape[:-1] + (1,))
    probs = jax.nn.softmax(jnp.concatenate([s_band, s_meta, sink], axis=-1), axis=-1)
    p_band = probs[..., :2 * BLOCK].astype(v.dtype)
    p_meta = probs[..., 2 * BLOCK:2 * BLOCK + N_META].astype(v.dtype)
    o = (jnp.einsum('bnhgqk,bnkhd->bnqhgd', p_band, v_band)
         + jnp.einsum('bnhgqm,bmhd->bnqhgd', p_meta, v_meta))
    return o.reshape(b, padded, Q_DIM)[:, pad:]


def conformer_conv(c_in, conv_w, conv_b, ln_g, ln_b, w_co, b_co):
    a, g = jnp.split(c_in, 2, axis=-1)
    c = a * jax.nn.sigmoid(g)
    c = lax.conv_general_dilated(
        c, conv_w.astype(c.dtype), window_strides=(1,), padding=[(CONV_WIDTH - 1, 0)],
        dimension_numbers=('NWC', 'WIO', 'NWC'), feature_group_count=CONV_DIM) + conv_b
    c = layer_norm(c, ln_g, ln_b)
    c = c * jax.nn.sigmoid(c)
    return c @ w_co + b_co


def mixer_block(u, cos, sin, w_in, b_in, sinks, conv_w, conv_b, ln_g, ln_b,
                w_ao, w_co, b_co, w_out):
    b, length, _ = u.shape
    z = u @ w_in + b_in
    idx = np.cumsum([Q_DIM, KV_DIM, KV_DIM, 2 * CONV_DIM, D_MODEL])
    q, k, v, c_in, gate_a, gate_b = jnp.split(z, idx, axis=-1)
    q = apply_rope(q.reshape(b, length, N_Q_HEADS, HEAD_DIM), cos, sin)
    k = apply_rope(k.reshape(b, length, N_KV_HEADS, HEAD_DIM), cos, sin)
    v = v.reshape(b, length, N_KV_HEADS, HEAD_DIM)
    branch_a = sliding_window_attention(q, k, v, sinks) @ w_ao
    branch_b = conformer_conv(c_in, conv_w, conv_b, ln_g, ln_b, w_co, b_co)
    merged = jax.nn.sigmoid(gate_a) * branch_a + jax.nn.sigmoid(gate_b) * branch_b
    return merged @ w_out


def swiglu(u, w_gate_up, w_down):
    gu = u @ w_gate_up
    g, up = jnp.split(gu, 2, axis=-1)
    return (jax.nn.silu(g) * up) @ w_down


def _fwd_setup_inputs(seed: int = 0) -> dict:
    key = jax.random.key(seed)
    ks = jax.random.split(key, 20)
    f32 = jnp.float32
    nrm = lambda k, shape, s: jax.random.normal(k, shape, f32) * s
    return {
        "x": nrm(ks[0], (BATCH, SEQ, D_MODEL), 1.0),
        "meta_tokens": nrm(ks[1], (N_META, D_MODEL), 1.0),
        "mix_norm_g": 1.0 + nrm(ks[2], (DEPTH, D_MODEL), 0.02),
        "w_in": nrm(ks[3], (DEPTH, D_MODEL, IN_DIM), D_MODEL ** -0.5),
        "b_in": nrm(ks[4], (DEPTH, IN_DIM), 0.02),
        "attn_sinks": nrm(ks[5], (DEPTH, N_Q_HEADS), 1.0),
        "conv_w": nrm(ks[6], (DEPTH, CONV_WIDTH, 1, CONV_DIM), CONV_WIDTH ** -0.5),
        "conv_b": nrm(ks[7], (DEPTH, CONV_DIM), 0.02),
        "conv_ln_g": 1.0 + nrm(ks[8], (DEPTH, CONV_DIM), 0.02),
        "conv_ln_b": nrm(ks[9], (DEPTH, CONV_DIM), 0.02),
        "w_attn_o": nrm(ks[10], (DEPTH, Q_DIM, D_MODEL), Q_DIM ** -0.5),
        "w_conv_o": nrm(ks[11], (DEPTH, CONV_DIM, D_MODEL), CONV_DIM ** -0.5),
        "b_conv_o": nrm(ks[12], (DEPTH, D_MODEL), 0.02),
        "w_out": nrm(ks[13], (DEPTH, D_MODEL, D_MODEL), D_MODEL ** -0.5),
        "ffn_norm_g": 1.0 + nrm(ks[14], (DEPTH, D_MODEL), 0.02),
        "w_gate_up": nrm(ks[15], (DEPTH, D_MODEL, 2 * FFN_DIM), D_MODEL ** -0.5),
        "w_down": nrm(ks[16], (DEPTH, FFN_DIM, D_MODEL), FFN_DIM ** -0.5),
        "final_norm_g": 1.0 + nrm(ks[17], (D_MODEL,), 0.02),
    }


def _fwd_reference(x, meta_tokens, mix_norm_g, w_in, b_in, attn_sinks, conv_w, conv_b,
              conv_ln_g, conv_ln_b, w_attn_o, w_conv_o, b_conv_o, w_out,
              ffn_norm_g, w_gate_up, w_down, final_norm_g):
    b = x.shape[0]
    meta = jnp.broadcast_to(meta_tokens[None].astype(x.dtype), (b, N_META, D_MODEL))
    h = jnp.concatenate([meta, x], axis=1)
    cos, sin = rope_tables(h.shape[1])
    for layer in range(DEPTH):
        u = rms_norm(h, mix_norm_g[layer])
        h = h + mixer_block(u, cos, sin, w_in[layer], b_in[layer], attn_sinks[layer],
                            conv_w[layer], conv_b[layer], conv_ln_g[layer], conv_ln_b[layer],
                            w_attn_o[layer], w_conv_o[layer], b_conv_o[layer], w_out[layer])
        h = h + swiglu(rms_norm(h, ffn_norm_g[layer]), w_gate_up[layer], w_down[layer])
    y = rms_norm(h, final_norm_g)
    return y[:, N_META:]


import jax as _jax
import jax.numpy as _jnp

TWIN_FORMAT = 'train_step'
FWD_PARAMS = ['x', 'meta_tokens', 'mix_norm_g', 'w_in', 'b_in', 'attn_sinks', 'conv_w', 'conv_b', 'conv_ln_g', 'conv_ln_b', 'w_attn_o', 'w_conv_o', 'b_conv_o', 'w_out', 'ffn_norm_g', 'w_gate_up', 'w_down', 'final_norm_g']
TWIN_WEIGHTS = ['meta_tokens', 'mix_norm_g', 'w_in', 'b_in', 'attn_sinks', 'conv_w', 'conv_b', 'conv_ln_g', 'conv_ln_b', 'w_attn_o', 'w_conv_o', 'b_conv_o', 'w_out', 'ffn_norm_g', 'w_gate_up', 'w_down', 'final_norm_g']
TWIN_DIFF_INPUT = 'x'
TWIN_INPUTS = ['x', 'meta_tokens', 'mix_norm_g', 'w_in', 'b_in', 'attn_sinks', 'conv_w', 'conv_b', 'conv_ln_g', 'conv_ln_b', 'w_attn_o', 'w_conv_o', 'b_conv_o', 'w_out', 'ffn_norm_g', 'w_gate_up', 'w_down', 'final_norm_g', 'loss_target', 'm_meta_tokens', 'm_mix_norm_g', 'm_w_in', 'm_b_in', 'm_attn_sinks', 'm_conv_w', 'm_conv_b', 'm_conv_ln_g', 'm_conv_ln_b', 'm_w_attn_o', 'm_w_conv_o', 'm_b_conv_o', 'm_w_out', 'm_ffn_norm_g', 'm_w_gate_up', 'm_w_down', 'm_final_norm_g', 'v_meta_tokens', 'v_mix_norm_g', 'v_w_in', 'v_b_in', 'v_attn_sinks', 'v_conv_w', 'v_conv_b', 'v_conv_ln_g', 'v_conv_ln_b', 'v_w_attn_o', 'v_w_conv_o', 'v_b_conv_o', 'v_w_out', 'v_ffn_norm_g', 'v_w_gate_up', 'v_w_down', 'v_final_norm_g']
TWIN_OUTPUTS = ['loss', 'grad_x', 'grad_meta_tokens', 'grad_mix_norm_g', 'grad_w_in', 'grad_b_in', 'grad_attn_sinks', 'grad_conv_w', 'grad_conv_b', 'grad_conv_ln_g', 'grad_conv_ln_b', 'grad_w_attn_o', 'grad_w_conv_o', 'grad_b_conv_o', 'grad_w_out', 'grad_ffn_norm_g', 'grad_w_gate_up', 'grad_w_down', 'grad_final_norm_g', 'delta_meta_tokens', 'delta_mix_norm_g', 'delta_w_in', 'delta_b_in', 'delta_attn_sinks', 'delta_conv_w', 'delta_conv_b', 'delta_conv_ln_g', 'delta_conv_ln_b', 'delta_w_attn_o', 'delta_w_conv_o', 'delta_b_conv_o', 'delta_w_out', 'delta_ffn_norm_g', 'delta_w_gate_up', 'delta_w_down', 'delta_final_norm_g', 'new_m_meta_tokens', 'new_m_mix_norm_g', 'new_m_w_in', 'new_m_b_in', 'new_m_attn_sinks', 'new_m_conv_w', 'new_m_conv_b', 'new_m_conv_ln_g', 'new_m_conv_ln_b', 'new_m_w_attn_o', 'new_m_w_conv_o', 'new_m_b_conv_o', 'new_m_w_out', 'new_m_ffn_norm_g', 'new_m_w_gate_up', 'new_m_w_down', 'new_m_final_norm_g', 'new_v_meta_tokens', 'new_v_mix_norm_g', 'new_v_w_in', 'new_v_b_in', 'new_v_attn_sinks', 'new_v_conv_w', 'new_v_conv_b', 'new_v_conv_ln_g', 'new_v_conv_ln_b', 'new_v_w_attn_o', 'new_v_w_conv_o', 'new_v_b_conv_o', 'new_v_w_out', 'new_v_ffn_norm_g', 'new_v_w_gate_up', 'new_v_w_down', 'new_v_final_norm_g']
TWIN_LEAF_KINDS = {'loss': 'loss', 'grad_x': 'grad_x', 'grad_meta_tokens': 'grad_w', 'grad_mix_norm_g': 'grad_w', 'grad_w_in': 'grad_w', 'grad_b_in': 'grad_w', 'grad_attn_sinks': 'grad_w', 'grad_conv_w': 'grad_w', 'grad_conv_b': 'grad_w', 'grad_conv_ln_g': 'grad_w', 'grad_conv_ln_b': 'grad_w', 'grad_w_attn_o': 'grad_w', 'grad_w_conv_o': 'grad_w', 'grad_b_conv_o': 'grad_w', 'grad_w_out': 'grad_w', 'grad_ffn_norm_g': 'grad_w', 'grad_w_gate_up': 'grad_w', 'grad_w_down': 'grad_w', 'grad_final_norm_g': 'grad_w', 'delta_meta_tokens': 'delta_w', 'delta_mix_norm_g': 'delta_w', 'delta_w_in': 'delta_w', 'delta_b_in': 'delta_w', 'delta_attn_sinks': 'delta_w', 'delta_conv_w': 'delta_w', 'delta_conv_b': 'delta_w', 'delta_conv_ln_g': 'delta_w', 'delta_conv_ln_b': 'delta_w', 'delta_w_attn_o': 'delta_w', 'delta_w_conv_o': 'delta_w', 'delta_b_conv_o': 'delta_w', 'delta_w_out': 'delta_w', 'delta_ffn_norm_g': 'delta_w', 'delta_w_gate_up': 'delta_w', 'delta_w_down': 'delta_w', 'delta_final_norm_g': 'delta_w', 'new_m_meta_tokens': 'new_m', 'new_m_mix_norm_g': 'new_m', 'new_m_w_in': 'new_m', 'new_m_b_in': 'new_m', 'new_m_attn_sinks': 'new_m', 'new_m_conv_w': 'new_m', 'new_m_conv_b': 'new_m', 'new_m_conv_ln_g': 'new_m', 'new_m_conv_ln_b': 'new_m', 'new_m_w_attn_o': 'new_m', 'new_m_w_conv_o': 'new_m', 'new_m_b_conv_o': 'new_m', 'new_m_w_out': 'new_m', 'new_m_ffn_norm_g': 'new_m', 'new_m_w_gate_up': 'new_m', 'new_m_w_down': 'new_m', 'new_m_final_norm_g': 'new_m', 'new_v_meta_tokens': 'new_v', 'new_v_mix_norm_g': 'new_v', 'new_v_w_in': 'new_v', 'new_v_b_in': 'new_v', 'new_v_attn_sinks': 'new_v', 'new_v_conv_w': 'new_v', 'new_v_conv_b': 'new_v', 'new_v_conv_ln_g': 'new_v', 'new_v_conv_ln_b': 'new_v', 'new_v_w_attn_o': 'new_v', 'new_v_w_conv_o': 'new_v', 'new_v_b_conv_o': 'new_v', 'new_v_w_out': 'new_v', 'new_v_ffn_norm_g': 'new_v', 'new_v_w_gate_up': 'new_v', 'new_v_w_down': 'new_v', 'new_v_final_norm_g': 'new_v'}


def _forward(args):
    return _fwd_reference(*[args[k] for k in FWD_PARAMS])


def _output_shape():
    out = _jax.eval_shape(lambda: _forward(_fwd_setup_inputs(0)))
    return out.shape, out.dtype

N_MICROBATCH = 1
ADAM_LR = 0.001
ADAM_B1 = 0.9
ADAM_B2 = 0.999
ADAM_EPS = 1e-08
ADAM_WD = 0.01
ADAM_STEP = 10
PER_EXAMPLE_BATCH_AXIS = {'x': 0, 'loss_target': 0}
SHARED_INPUTS = []
_WEIGHT_DTYPES = {'meta_tokens': _jnp.float32, 'mix_norm_g': _jnp.float32, 'w_in': _jnp.float32, 'b_in': _jnp.float32, 'attn_sinks': _jnp.float32, 'conv_w': _jnp.float32, 'conv_b': _jnp.float32, 'conv_ln_g': _jnp.float32, 'conv_ln_b': _jnp.float32, 'w_attn_o': _jnp.float32, 'w_conv_o': _jnp.float32, 'b_conv_o': _jnp.float32, 'w_out': _jnp.float32, 'ffn_norm_g': _jnp.float32, 'w_gate_up': _jnp.float32, 'w_down': _jnp.float32, 'final_norm_g': _jnp.float32}
MOMENT_SCALE = {'meta_tokens': 1.260885e-03, 'mix_norm_g': 1.945311e-02, 'w_in': 1.020649e-02, 'b_in': 1.837143e-02, 'attn_sinks': 1.580226e-03, 'conv_w': 2.259543e-02, 'conv_b': 4.145401e-02, 'conv_ln_g': 2.671026e-02, 'conv_ln_b': 2.271702e-02, 'w_attn_o': 4.388237e-03, 'w_conv_o': 1.543835e-02, 'b_conv_o': 2.647436e-02, 'w_out': 1.590722e-02, 'ffn_norm_g': 3.158709e-02, 'w_gate_up': 1.355439e-02, 'w_down': 2.182426e-02, 'final_norm_g': 7.983693e+00}


def _to_microbatches(a, axis):
    t = _jnp.moveaxis(a, axis, 0)
    t = t.reshape((N_MICROBATCH, t.shape[0] // N_MICROBATCH) + t.shape[1:])
    return _jnp.moveaxis(t, 1, axis + 1)


def setup_inputs(seed: int = 0) -> dict:
    inp = _fwd_setup_inputs(seed)
    key = _jax.random.fold_in(_jax.random.key(seed), 7919)
    shape, _ = _output_shape()
    out = dict(inp)
    out["loss_target"] = _jax.random.normal(_jax.random.fold_in(key, 0), shape, _jnp.float32)
    for i, name in enumerate(TWIN_WEIGHTS):
        w = inp[name].astype(_jnp.float32)
        if MOMENT_SCALE is None:
            s = _jnp.sqrt(_jnp.mean(_jnp.square(w)) + 1e-30)
        else:
            s = MOMENT_SCALE[name]
        km, kv = _jax.random.split(_jax.random.fold_in(key, i + 1))
        out[name] = w
        out["m_" + name] = s * _jax.random.normal(km, w.shape, _jnp.float32)
        out["v_" + name] = (s * s) * _jax.random.uniform(kv, w.shape, _jnp.float32, 0.5, 1.5)
    if N_MICROBATCH > 1:
        for name, axis in PER_EXAMPLE_BATCH_AXIS.items():
            out[name] = _to_microbatches(out[name], axis)
    return {'x': out['x'], 'meta_tokens': out['meta_tokens'], 'mix_norm_g': out['mix_norm_g'], 'w_in': out['w_in'], 'b_in': out['b_in'], 'attn_sinks': out['attn_sinks'], 'conv_w': out['conv_w'], 'conv_b': out['conv_b'], 'conv_ln_g': out['conv_ln_g'], 'conv_ln_b': out['conv_ln_b'], 'w_attn_o': out['w_attn_o'], 'w_conv_o': out['w_conv_o'], 'b_conv_o': out['b_conv_o'], 'w_out': out['w_out'], 'ffn_norm_g': out['ffn_norm_g'], 'w_gate_up': out['w_gate_up'], 'w_down': out['w_down'], 'final_norm_g': out['final_norm_g'], 'loss_target': out['loss_target'], 'm_meta_tokens': out['m_meta_tokens'], 'm_mix_norm_g': out['m_mix_norm_g'], 'm_w_in': out['m_w_in'], 'm_b_in': out['m_b_in'], 'm_attn_sinks': out['m_attn_sinks'], 'm_conv_w': out['m_conv_w'], 'm_conv_b': out['m_conv_b'], 'm_conv_ln_g': out['m_conv_ln_g'], 'm_conv_ln_b': out['m_conv_ln_b'], 'm_w_attn_o': out['m_w_attn_o'], 'm_w_conv_o': out['m_w_conv_o'], 'm_b_conv_o': out['m_b_conv_o'], 'm_w_out': out['m_w_out'], 'm_ffn_norm_g': out['m_ffn_norm_g'], 'm_w_gate_up': out['m_w_gate_up'], 'm_w_down': out['m_w_down'], 'm_final_norm_g': out['m_final_norm_g'], 'v_meta_tokens': out['v_meta_tokens'], 'v_mix_norm_g': out['v_mix_norm_g'], 'v_w_in': out['v_w_in'], 'v_b_in': out['v_b_in'], 'v_attn_sinks': out['v_attn_sinks'], 'v_conv_w': out['v_conv_w'], 'v_conv_b': out['v_conv_b'], 'v_conv_ln_g': out['v_conv_ln_g'], 'v_conv_ln_b': out['v_conv_ln_b'], 'v_w_attn_o': out['v_w_attn_o'], 'v_w_conv_o': out['v_w_conv_o'], 'v_b_conv_o': out['v_b_conv_o'], 'v_w_out': out['v_w_out'], 'v_ffn_norm_g': out['v_ffn_norm_g'], 'v_w_gate_up': out['v_w_gate_up'], 'v_w_down': out['v_w_down'], 'v_final_norm_g': out['v_final_norm_g']}


def _loss(weights, diff, rest, loss_target):
    with _jax.named_scope("forward"):
        args = {**rest, TWIN_DIFF_INPUT: diff, **{k: w.astype(_WEIGHT_DTYPES[k]) for k, w in weights.items()}}
        y = _forward(args)
    with _jax.named_scope("loss_head"):
        err = _jnp.square(y.astype(_jnp.float32) - loss_target)
        return 0.5 * _jnp.sum(_jnp.mean(err, axis=-1)) if err.ndim else 0.5 * err


def _adamw(w, g, m, v):
    m = ADAM_B1 * m + (1.0 - ADAM_B1) * g
    v = ADAM_B2 * v + (1.0 - ADAM_B2) * _jnp.square(g)
    m_hat = m / (1.0 - ADAM_B1 ** ADAM_STEP)
    v_hat = v / (1.0 - ADAM_B2 ** ADAM_STEP)
    delta = -ADAM_LR * (m_hat / (_jnp.sqrt(v_hat) + ADAM_EPS) + ADAM_WD * w)
    return delta, m, v


def reference(x, meta_tokens, mix_norm_g, w_in, b_in, attn_sinks, conv_w, conv_b, conv_ln_g, conv_ln_b, w_attn_o, w_conv_o, b_conv_o, w_out, ffn_norm_g, w_gate_up, w_down, final_norm_g, loss_target, m_meta_tokens, m_mix_norm_g, m_w_in, m_b_in, m_attn_sinks, m_conv_w, m_conv_b, m_conv_ln_g, m_conv_ln_b, m_w_attn_o, m_w_conv_o, m_b_conv_o, m_w_out, m_ffn_norm_g, m_w_gate_up, m_w_down, m_final_norm_g, v_meta_tokens, v_mix_norm_g, v_w_in, v_b_in, v_attn_sinks, v_conv_w, v_conv_b, v_conv_ln_g, v_conv_ln_b, v_w_attn_o, v_w_conv_o, v_b_conv_o, v_w_out, v_ffn_norm_g, v_w_gate_up, v_w_down, v_final_norm_g):
    given = dict(x=x, meta_tokens=meta_tokens, mix_norm_g=mix_norm_g, w_in=w_in, b_in=b_in, attn_sinks=attn_sinks, conv_w=conv_w, conv_b=conv_b, conv_ln_g=conv_ln_g, conv_ln_b=conv_ln_b, w_attn_o=w_attn_o, w_conv_o=w_conv_o, b_conv_o=b_conv_o, w_out=w_out, ffn_norm_g=ffn_norm_g, w_gate_up=w_gate_up, w_down=w_down, final_norm_g=final_norm_g, loss_target=loss_target, m_meta_tokens=m_meta_tokens, m_mix_norm_g=m_mix_norm_g, m_w_in=m_w_in, m_b_in=m_b_in, m_attn_sinks=m_attn_sinks, m_conv_w=m_conv_w, m_conv_b=m_conv_b, m_conv_ln_g=m_conv_ln_g, m_conv_ln_b=m_conv_ln_b, m_w_attn_o=m_w_attn_o, m_w_conv_o=m_w_conv_o, m_b_conv_o=m_b_conv_o, m_w_out=m_w_out, m_ffn_norm_g=m_ffn_norm_g, m_w_gate_up=m_w_gate_up, m_w_down=m_w_down, m_final_norm_g=m_final_norm_g, v_meta_tokens=v_meta_tokens, v_mix_norm_g=v_mix_norm_g, v_w_in=v_w_in, v_b_in=v_b_in, v_attn_sinks=v_attn_sinks, v_conv_w=v_conv_w, v_conv_b=v_conv_b, v_conv_ln_g=v_conv_ln_g, v_conv_ln_b=v_conv_ln_b, v_w_attn_o=v_w_attn_o, v_w_conv_o=v_w_conv_o, v_b_conv_o=v_b_conv_o, v_w_out=v_w_out, v_ffn_norm_g=v_ffn_norm_g, v_w_gate_up=v_w_gate_up, v_w_down=v_w_down, v_final_norm_g=v_final_norm_g)
    weights = {n: given[n] for n in TWIN_WEIGHTS}
    shared = {n: given[n] for n in SHARED_INPUTS}
    per_example = {n: given[n] for n in ['x']}
    grad_fn = _jax.value_and_grad(_loss, argnums=(0, 1))

    def one_microbatch(ex, loss_target):
        ex = dict(ex)
        diff = ex.pop(TWIN_DIFF_INPUT)
        return grad_fn(weights, diff, {**shared, **ex}, loss_target)

    if N_MICROBATCH == 1:
        loss, (grad_w, grad_x) = one_microbatch(per_example, given["loss_target"])
    else:
        def body(carry, xs):
            loss_sum, grad_sum = carry
            l_k, (gw_k, gx_k) = one_microbatch(xs[0], xs[1])
            with _jax.named_scope("update"):
                return (loss_sum + l_k, _jax.tree.map(_jnp.add, grad_sum, gw_k)), gx_k

        init = (_jnp.zeros((), _jnp.float32), _jax.tree.map(_jnp.zeros_like, weights))
        (loss, grad_w), grad_x = _jax.lax.scan(body, init, (per_example, given["loss_target"]))
    with _jax.named_scope("update"):
        delta_w, new_m, new_v = {}, {}, {}
        for n in TWIN_WEIGHTS:
            delta_w[n], new_m[n], new_v[n] = _adamw(weights[n], grad_w[n], given["m_" + n], given["v_" + n])
    return (loss, grad_x, *[grad_w[n] for n in TWIN_WEIGHTS], *[delta_w[n] for n in TWIN_WEIGHTS],
            *[new_m[n] for n in TWIN_WEIGHTS], *[new_v[n] for n in TWIN_WEIGHTS])
```

```python
import functools

import jax
import jax.numpy as jnp
from jax import lax
from jax.experimental import pallas as pl
from jax.experimental.pallas import tpu as pltpu

F32 = jnp.float32
BF16 = jnp.bfloat16

N_DEV = 8
BLOCK = 128
N_META = 16
PAD_ROWS = BLOCK - N_META
HEAD_DIM = 64
N_Q_HEADS = 32
N_KV_HEADS = 4
GROUP = N_Q_HEADS // N_KV_HEADS
Q_DIM = N_Q_HEADS * HEAD_DIM
KV_DIM = N_KV_HEADS * HEAD_DIM
WINDOW = 128
CONV_WIDTH = 31
CONV_ROWS = 32
ROPE_THETA = 10000.0
EPS = 1e-6
ATTN_SCALE = HEAD_DIM ** -0.5
NEG = -1e30

ADAM_LR = 0.001
ADAM_B1 = 0.9
ADAM_B2 = 0.999
ADAM_EPS = 1e-08
ADAM_WD = 0.01
ADAM_STEP = 10

VMEM_LIMIT_BYTES = 56 * 1024 * 1024
LANES = 128
ELEMENTWISE_BLOCK_BYTES = 2 * 1024 * 1024
MESH = pl.DeviceIdType.MESH
CHIPS = ((0, 0), (0, 1), (1, 0), (1, 1))


def _pcall(body, **kw):
    return pl.pallas_call(body, **kw)


def _params(semantics=None):
    if semantics is None:
        return pltpu.CompilerParams(vmem_limit_bytes=VMEM_LIMIT_BYTES)
    return pltpu.CompilerParams(dimension_semantics=semantics, vmem_limit_bytes=VMEM_LIMIT_BYTES)


def _pick(dim, pref, align):
    best = None
    t = align
    while t <= min(dim, pref):
        if dim % t == 0:
            best = t
        t += align
    return dim if best is None else best


def _sigmoid(x):
    return 1.0 / (1.0 + jnp.exp(-x))


def _matmul(a, b, *, mode, name, out_dtype, tm, tn, tk, bias=None, residual=None, b_inner=True,
            b_row_off=0, b_rows=None):
    if mode == "nn":
        m, k = a.shape
        n = b.shape[1]
    elif mode == "nt":
        m, k = a.shape
        n = b.shape[0] if b_rows is None else b_rows
    else:
        k, m = a.shape
        n = b.shape[1]
    tm = _pick(m, tm, 16)
    tn = _pick(n, tn, LANES)
    tk = _pick(k, tk, LANES if mode != "tn" else 16)
    nm, nn, nk = m // tm, n // tn, k // tk
    if mode == "nt":
        assert b_row_off % tn == 0
    off = b_row_off // tn if mode == "nt" else 0

    if b_inner:
        grid = (nm, nn, nk)
        ij = lambda g0, g1: (g0, g1)
    else:
        grid = (nn, nm, nk)
        ij = lambda g0, g1: (g1, g0)

    if mode == "tn":
        a_spec = pl.BlockSpec((tk, tm), lambda g0, g1, kk: (kk, ij(g0, g1)[0]))
    else:
        a_spec = pl.BlockSpec((tm, tk), lambda g0, g1, kk: (ij(g0, g1)[0], kk))
    if mode == "nt":
        b_spec = pl.BlockSpec((tn, tk), lambda g0, g1, kk: (ij(g0, g1)[1] + off, kk))
    else:
        b_spec = pl.BlockSpec((tk, tn), lambda g0, g1, kk: (kk, ij(g0, g1)[1]))
    o_spec = pl.BlockSpec((tm, tn), lambda g0, g1, kk: ij(g0, g1))
    in_specs = [a_spec, b_spec]
    args = [a, b]
    if bias is not None:
        in_specs.append(pl.BlockSpec((1, tn), lambda g0, g1, kk: (0, ij(g0, g1)[1])))
        args.append(bias)
    if residual is not None:
        in_specs.append(o_spec)
        args.append(residual)
    dims = {"nn": (((1,), (0,)), ((), ())), "nt": (((1,), (1,)), ((), ())), "tn": (((0,), (0,)), ((), ()))}[mode]
    has_bias, has_res = bias is not None, residual is not None

    def body(*refs):
        a_ref, b_ref = refs[0], refs[1]
        pos = 2
        bias_ref = res_ref = None
        if has_bias:
            bias_ref = refs[pos]
            pos += 1
        if has_res:
            res_ref = refs[pos]
            pos += 1
        o_ref = refs[pos]
        acc_ref = refs[pos + 1] if nk > 1 else None

        def finish(acc):
            if has_bias:
                acc = acc + bias_ref[...]
            if has_res:
                acc = acc + res_ref[...]
            o_ref[...] = acc.astype(out_dtype)

        p = lax.dot_general(a_ref[...], b_ref[...], dims, preferred_element_type=F32)
        if nk == 1:
            finish(p)
        else:
            kk = pl.program_id(2)

            @pl.when(kk == 0)
            def _():
                acc_ref[...] = p

            @pl.when(kk > 0)
            def _():
                acc_ref[...] += p

            @pl.when(kk == nk - 1)
            def _():
                finish(acc_ref[...])

    return _pcall(
        body, name=name, grid=grid, in_specs=in_specs, out_specs=o_spec,
        out_shape=jax.ShapeDtypeStruct((m, n), out_dtype),
        scratch_shapes=[pltpu.VMEM((tm, tn), F32)] if nk > 1 else [],
        compiler_params=_params(("parallel", "parallel", "arbitrary")),
    )(*args)


def _row_spec(width, col=0):
    return pl.BlockSpec((BLOCK, width), lambda i: (i, col))


def _const_spec(shape):
    nd = len(shape)
    return pl.BlockSpec(shape, lambda i: (0,) * nd)


def _prep(x, meta_full, g):
    s, d = x.shape
    lp = s + BLOCK
    nb = lp // BLOCK

    def body(x_ref, meta_ref, g_ref, h_ref, u_ref):
        i = pl.program_id(0)

        @pl.when(i == 0)
        def _():
            h_ref[0:PAD_ROWS, :] = jnp.zeros((PAD_ROWS, d), F32)
            h_ref[PAD_ROWS:BLOCK, :] = meta_ref[...]

        @pl.when(i > 0)
        def _():
            h_ref[...] = x_ref[...]

        h = h_ref[...]
        r = lax.rsqrt(jnp.mean(h * h, axis=-1, keepdims=True) + EPS)
        u_ref[...] = (h * r * g_ref[...]).astype(BF16)

    return _pcall(
        body, name="prep_rmsnorm", grid=(nb,),
        in_specs=[pl.BlockSpec((BLOCK, d), lambda i: (jnp.maximum(i - 1, 0), 0)), _const_spec((N_META, d)), _const_spec((1, d))],
        out_specs=[_row_spec(d), _row_spec(d)],
        out_shape=[jax.ShapeDtypeStruct((lp, d), F32), jax.ShapeDtypeStruct((lp, d), BF16)],
        compiler_params=_params(("arbitrary",)),
    )(x, meta_full, g)


def _rmsnorm_fwd(h, g, name):
    lp, d = h.shape

    def body(h_ref, g_ref, u_ref):
        x = h_ref[...]
        r = lax.rsqrt(jnp.mean(x * x, axis=-1, keepdims=True) + EPS)
        u_ref[...] = (x * r * g_ref[...]).astype(BF16)

    return _pcall(
        body, name=name, grid=(lp // BLOCK,), in_specs=[_row_spec(d), _const_spec((1, d))], out_specs=_row_spec(d),
        out_shape=jax.ShapeDtypeStruct((lp, d), BF16), compiler_params=_params(("parallel",)),
    )(h, g)


def _rms_bwd_core(dy, x, g):
    r = lax.rsqrt(jnp.mean(x * x, axis=-1, keepdims=True) + EPS)
    xhat = x * r
    dxhat = dy * g
    dx = r * (dxhat - xhat * jnp.mean(dxhat * xhat, axis=-1, keepdims=True))
    return dx, jnp.sum(dy * xhat, axis=0, keepdims=True)


def _rmsnorm_bwd(dy, h, g, dres, name):
    lp, d = h.shape

    def body(dy_ref, h_ref, g_ref, dres_ref, dh_ref, dhb_ref, dg_ref):
        i = pl.program_id(0)
        dx, dg = _rms_bwd_core(dy_ref[...], h_ref[...], g_ref[...])
        dh = dres_ref[...] + dx
        dh_ref[...] = dh
        dhb_ref[...] = dh.astype(BF16)

        @pl.when(i == 0)
        def _():
            dg_ref[...] = jnp.zeros_like(dg_ref)

        dg_ref[...] += dg

    return _pcall(
        body, name=name, grid=(lp // BLOCK,),
        in_specs=[_row_spec(d), _row_spec(d), _const_spec((1, d)), _row_spec(d)],
        out_specs=[_row_spec(d), _row_spec(d), _const_spec((1, d))],
        out_shape=[jax.ShapeDtypeStruct((lp, d), F32), jax.ShapeDtypeStruct((lp, d), BF16), jax.ShapeDtypeStruct((1, d), F32)],
        compiler_params=_params(("arbitrary",)),
    )(dy, h, g, dres)


def _rmsnorm_bwd_first(dy, h, g, dres):
    lp, d = h.shape
    s = lp - BLOCK

    def body(dy_ref, h_ref, g_ref, dres_ref, gx_ref, dmeta_ref, dg_ref):
        i = pl.program_id(0)
        dx, dg = _rms_bwd_core(dy_ref[...], h_ref[...], g_ref[...])
        dh = dres_ref[...] + dx
        gx_ref[...] = dh

        @pl.when(i == 0)
        def _():
            dmeta_ref[...] = dh[PAD_ROWS:BLOCK, :]
            dg_ref[...] = jnp.zeros_like(dg_ref)

        dg_ref[...] += dg

    return _pcall(
        body, name="rmsnorm_bwd_first", grid=(lp // BLOCK,),
        in_specs=[_row_spec(d), _row_spec(d), _const_spec((1, d)), _row_spec(d)],
        out_specs=[pl.BlockSpec((BLOCK, d), lambda i: (jnp.maximum(i - 1, 0), 0)), _const_spec((N_META, d)), _const_spec((1, d))],
        out_shape=[jax.ShapeDtypeStruct((s, d), F32), jax.ShapeDtypeStruct((N_META, d), F32), jax.ShapeDtypeStruct((1, d), F32)],
        compiler_params=_params(("arbitrary",)),
    )(dy, h, g, dres)


def _final(h2, tgt, g):
    lp, d = h2.shape

    def body(h_ref, t_ref, g_ref, dh_ref, dhb_ref, loss_ref, dg_ref):
        i = pl.program_id(0)
        x = h_ref[...]
        gg = g_ref[...]
        r = lax.rsqrt(jnp.mean(x * x, axis=-1, keepdims=True) + EPS)
        xhat = x * r
        y = xhat * gg
        live = (i > 0).astype(F32)
        err = (y - t_ref[...]) * live
        dy = err * (1.0 / d)
        dxhat = dy * gg
        dh = r * (dxhat - xhat * jnp.mean(dxhat * xhat, axis=-1, keepdims=True))
        dh_ref[...] = dh
        dhb_ref[...] = dh.astype(BF16)

        @pl.when(i == 0)
        def _():
            loss_ref[...] = jnp.zeros_like(loss_ref)
            dg_ref[...] = jnp.zeros_like(dg_ref)

        row_loss = jnp.mean(err * err, axis=-1, keepdims=True)
        loss_ref[...] += 0.5 * jnp.sum(row_loss, axis=0, keepdims=True)
        dg_ref[...] += jnp.sum(dy * xhat, axis=0, keepdims=True)

    return _pcall(
        body, name="final_norm_loss", grid=(lp // BLOCK,),
        in_specs=[_row_spec(d), pl.BlockSpec((BLOCK, d), lambda i: (jnp.maximum(i - 1, 0), 0)), _const_spec((1, d))],
        out_specs=[_row_spec(d), _row_spec(d), _const_spec((1, LANES)), _const_spec((1, d))],
        out_shape=[jax.ShapeDtypeStruct((lp, d), F32), jax.ShapeDtypeStruct((lp, d), BF16),
                   jax.ShapeDtypeStruct((1, LANES), F32), jax.ShapeDtypeStruct((1, d), F32)],
        compiler_params=_params(("arbitrary",)),
    )(h2, tgt, g)


def _swap_halves(x):
    w = x.shape[1]
    lane = lax.broadcasted_iota(jnp.int32, x.shape, 1)
    first = (lane & (HEAD_DIM - 1)) < (HEAD_DIM // 2)
    return jnp.where(first, pltpu.roll(x, w - HEAD_DIM // 2, 1), pltpu.roll(x, HEAD_DIM // 2, 1))


def _rope_tables(lp):
    pos = jnp.maximum(jnp.arange(lp, dtype=jnp.int32) - PAD_ROWS, 0).astype(F32)
    inv_freq = ROPE_THETA ** (-jnp.arange(0, HEAD_DIM, 2, dtype=F32) / HEAD_DIM)
    ang = pos[:, None] * inv_freq[None, :]
    c, s = jnp.cos(ang), jnp.sin(ang)
    reps = LANES // HEAD_DIM
    return jnp.tile(jnp.concatenate([c, c], axis=1), (1, reps)), jnp.tile(jnp.concatenate([-s, s], axis=1), (1, reps))


def _rope_fwd(zq, zkv, ctab, stab):
    lp = zq.shape[0]
    nb = lp // BLOCK
    back = lambda s: (jnp.maximum(s - 1, 0), 0)

    def body(zq_ref, zkv_ref, c_ref, s_ref, q_ref, k_ref, v_ref):
        step = pl.program_id(0)
        c128, s128 = c_ref[...], s_ref[...]

        def rope(x):
            reps = x.shape[1] // LANES
            return x * jnp.tile(c128, (1, reps)) + _swap_halves(x) * jnp.tile(s128, (1, reps))

        q_ref[...] = (rope(zq_ref[...]) * ATTN_SCALE).astype(BF16)
        kv = zkv_ref[...]
        k = rope(kv[:, :KV_DIM])
        v = kv[:, KV_DIM:]

        @pl.when(step == 0)
        def _():
            k_ref[...] = jnp.zeros_like(k_ref)
            v_ref[...] = jnp.zeros_like(v_ref)

        @pl.when(step > 0)
        def _():
            for h in range(N_KV_HEADS):
                k_ref[h] = k[:, h * HEAD_DIM:(h + 1) * HEAD_DIM].astype(BF16)
                v_ref[h] = v[:, h * HEAD_DIM:(h + 1) * HEAD_DIM].astype(BF16)

    kv_spec = pl.BlockSpec((N_KV_HEADS, BLOCK, HEAD_DIM), lambda s: (0, s, 0))
    return _pcall(
        body, name="rope_fwd", grid=(nb + 1,),
        in_specs=[pl.BlockSpec((BLOCK, Q_DIM), back), pl.BlockSpec((BLOCK, 2 * KV_DIM), back),
                  pl.BlockSpec((BLOCK, LANES), back), pl.BlockSpec((BLOCK, LANES), back)],
        out_specs=[pl.BlockSpec((BLOCK, Q_DIM), back), kv_spec, kv_spec],
        out_shape=[jax.ShapeDtypeStruct((lp, Q_DIM), BF16),
                   jax.ShapeDtypeStruct((N_KV_HEADS, lp + BLOCK, HEAD_DIM), BF16),
                   jax.ShapeDtypeStruct((N_KV_HEADS, lp + BLOCK, HEAD_DIM), BF16)],
        compiler_params=_params(("arbitrary",)),
    )(zq, zkv, ctab, stab)


def _rope_bwd(dq, dk, dv, dkm, dvm, ctab, stab):
    lp = dq.shape[0]
    width = Q_DIM + 2 * KV_DIM
    head_spec = pl.BlockSpec((N_KV_HEADS, BLOCK, HEAD_DIM), lambda i: (0, i, 0))
    meta_spec = _const_spec((N_KV_HEADS, BLOCK, HEAD_DIM))

    def body(dq_ref, dk_ref, dv_ref, dkm_ref, dvm_ref, c_ref, s_ref, dz_ref, sum_ref, kbuf, vbuf):
        i = pl.program_id(0)
        c128, s128 = c_ref[...], s_ref[...]
        first = (i == 0).astype(F32)

        def rope_t(x):
            reps = x.shape[1] // LANES
            return x * jnp.tile(c128, (1, reps)) + _swap_halves(x * jnp.tile(s128, (1, reps)))

        for h in range(N_KV_HEADS):
            kbuf[:, h * HEAD_DIM:(h + 1) * HEAD_DIM] = dk_ref[h] + first * dkm_ref[h]
            vbuf[:, h * HEAD_DIM:(h + 1) * HEAD_DIM] = dv_ref[h] + first * dvm_ref[h]
        dzq = rope_t(dq_ref[...] * ATTN_SCALE)
        dzk = rope_t(kbuf[...])
        dzv = vbuf[...]
        dz_ref[:, 0:Q_DIM] = dzq.astype(BF16)
        dz_ref[:, Q_DIM:Q_DIM + KV_DIM] = dzk.astype(BF16)
        dz_ref[:, Q_DIM + KV_DIM:width] = dzv.astype(BF16)

        @pl.when(i == 0)
        def _():
            sum_ref[...] = jnp.zeros_like(sum_ref)

        sum_ref[:, 0:Q_DIM] += jnp.sum(dzq, axis=0, keepdims=True)
        sum_ref[:, Q_DIM:Q_DIM + KV_DIM] += jnp.sum(dzk, axis=0, keepdims=True)
        sum_ref[:, Q_DIM + KV_DIM:width] += jnp.sum(dzv, axis=0, keepdims=True)

    return _pcall(
        body, name="rope_bwd", grid=(lp // BLOCK,),
        in_specs=[_row_spec(Q_DIM), head_spec, head_spec, meta_spec, meta_spec, _row_spec(LANES), _row_spec(LANES)],
        out_specs=[_row_spec(width), _const_spec((1, width))],
        out_shape=[jax.ShapeDtypeStruct((lp, width), BF16), jax.ShapeDtypeStruct((1, width), F32)],
        scratch_shapes=[pltpu.VMEM((BLOCK, KV_DIM), F32), pltpu.VMEM((BLOCK, KV_DIM), F32)],
        compiler_params=_params(("arbitrary",)),
    )(dq, dk, dv, dkm, dvm, ctab, stab)


def _attn_bias(i):
    r = lax.broadcasted_iota(jnp.int32, (BLOCK, 3 * BLOCK), 0)
    c = lax.broadcasted_iota(jnp.int32, (BLOCK, 3 * BLOCK), 1)
    qp = i * BLOCK + r - PAD_ROWS
    kp = (i - 1) * BLOCK + c - PAD_ROWS
    band = (c < 2 * BLOCK) & (kp >= N_META) & (kp <= qp) & (qp - kp < WINDOW)
    mp = c - 2 * BLOCK - PAD_ROWS
    meta = (c >= 2 * BLOCK) & (mp >= 0) & (mp <= qp)
    return jnp.where(band | meta, 0.0, NEG).astype(F32)


def _stack_heads(ref, h):
    return jnp.concatenate(
        [ref[:, (h * GROUP + g) * HEAD_DIM:(h * GROUP + g + 1) * HEAD_DIM] for g in range(GROUP)], axis=0)


def _attn_probs(qs, k3, bias8, sink):
    s = lax.dot_general(qs, k3, (((1,), (1,)), ((), ())), preferred_element_type=F32) + bias8
    m = jnp.maximum(jnp.max(s, axis=1, keepdims=True), sink)
    p = jnp.exp(s - m)
    ps = jnp.exp(sink - m)
    inv = 1.0 / (jnp.sum(p, axis=1, keepdims=True) + ps)
    return p * inv, ps * inv


def _sink_column(sink_ref, h):
    return jnp.concatenate(
        [jnp.broadcast_to(sink_ref[0:1, h * GROUP + g:h * GROUP + g + 1], (BLOCK, 1)) for g in range(GROUP)], axis=0)


def _attn_fwd(q, k_sh, v_sh, sinks):
    lp = q.shape[0]
    nb = lp // BLOCK
    kv = lambda f: pl.BlockSpec((N_KV_HEADS, BLOCK, HEAD_DIM), f)

    def body(q_ref, kp_ref, kc_ref, km_ref, vp_ref, vc_ref, vm_ref, sink_ref, o_ref):
        i = pl.program_id(0)
        bias8 = jnp.tile(_attn_bias(i), (GROUP, 1))
        for h in range(N_KV_HEADS):
            k3 = jnp.concatenate([kp_ref[h], kc_ref[h], km_ref[h]], axis=0)
            v3 = jnp.concatenate([vp_ref[h], vc_ref[h], vm_ref[h]], axis=0)
            qs = _stack_heads(q_ref, h)
            p, _ = _attn_probs(qs, k3, bias8, _sink_column(sink_ref, h))
            o = jnp.dot(p.astype(BF16), v3, preferred_element_type=F32)
            for g in range(GROUP):
                n = h * GROUP + g
                o_ref[:, n * HEAD_DIM:(n + 1) * HEAD_DIM] = o[g * BLOCK:(g + 1) * BLOCK].astype(BF16)

    prev, cur, meta = (lambda i: (0, i, 0)), (lambda i: (0, i + 1, 0)), (lambda i: (0, 1, 0))
    return _pcall(
        body, name="attn_fwd", grid=(nb,),
        in_specs=[_row_spec(Q_DIM), kv(prev), kv(cur), kv(meta), kv(prev), kv(cur), kv(meta), _const_spec((1, N_Q_HEADS))],
        out_specs=_row_spec(Q_DIM), out_shape=jax.ShapeDtypeStruct((lp, Q_DIM), BF16),
        compiler_params=_params(("parallel",)),
    )(q, k_sh, k_sh, k_sh, v_sh, v_sh, v_sh, sinks)


def _attn_bwd(q, k_sh, v_sh, sinks, do):
    lp = q.shape[0]
    nb = lp // BLOCK
    kv = lambda f: pl.BlockSpec((N_KV_HEADS, BLOCK, HEAD_DIM), f)
    cl = lambda s: jnp.minimum(s, nb - 1)

    def body(q_ref, do_ref, kp_ref, kc_ref, km_ref, vp_ref, vc_ref, vm_ref, sink_ref,
             dq_ref, dk_ref, dv_ref, dkm_ref, dvm_ref, dsink_ref, carry_k, carry_v):
        step = pl.program_id(0)

        @pl.when(step == 0)
        def _():
            carry_k[...] = jnp.zeros_like(carry_k)
            carry_v[...] = jnp.zeros_like(carry_v)
            dkm_ref[...] = jnp.zeros_like(dkm_ref)
            dvm_ref[...] = jnp.zeros_like(dvm_ref)
            dsink_ref[...] = jnp.zeros_like(dsink_ref)

        @pl.when(step < nb)
        def _():
            bias8 = jnp.tile(_attn_bias(step), (GROUP, 1))
            lane = lax.broadcasted_iota(jnp.int32, (1, LANES), 1)
            dsink = jnp.zeros((1, LANES), F32)
            for h in range(N_KV_HEADS):
                k3 = jnp.concatenate([kp_ref[h], kc_ref[h], km_ref[h]], axis=0)
                v3 = jnp.concatenate([vp_ref[h], vc_ref[h], vm_ref[h]], axis=0)
                qs = _stack_heads(q_ref, h)
                dos = _stack_heads(do_ref, h)
                p, psink = _attn_probs(qs, k3, bias8, _sink_column(sink_ref, h))
                dp = lax.dot_general(dos, v3, (((1,), (1,)), ((), ())), preferred_element_type=F32)
                delta = jnp.sum(p * dp, axis=1, keepdims=True)
                ds = (p * (dp - delta)).astype(BF16)
                dsk = -psink * delta
                for g in range(GROUP):
                    val = jnp.sum(dsk[g * BLOCK:(g + 1) * BLOCK], axis=0, keepdims=True)
                    dsink = dsink + jnp.where(lane == h * GROUP + g, val, 0.0)
                dqs = jnp.dot(ds, k3, preferred_element_type=F32)
                for g in range(GROUP):
                    n = h * GROUP + g
                    dq_ref[:, n * HEAD_DIM:(n + 1) * HEAD_DIM] = dqs[g * BLOCK:(g + 1) * BLOCK]
                dk3 = lax.dot_general(ds, qs, (((0,), (0,)), ((), ())), preferred_element_type=F32)
                dv3 = lax.dot_general(p.astype(BF16), dos, (((0,), (0,)), ((), ())), preferred_element_type=F32)
                dk_ref[h] = carry_k[h] + dk3[0:BLOCK]
                dv_ref[h] = carry_v[h] + dv3[0:BLOCK]
                carry_k[h] = dk3[BLOCK:2 * BLOCK]
                carry_v[h] = dv3[BLOCK:2 * BLOCK]
                dkm_ref[h] += dk3[2 * BLOCK:3 * BLOCK]
                dvm_ref[h] += dv3[2 * BLOCK:3 * BLOCK]
            dsink_ref[...] += dsink

        @pl.when(step == nb)
        def _():
            dk_ref[...] = carry_k[...]
            dv_ref[...] = carry_v[...]

    prev, cur, meta = (lambda s: (0, cl(s), 0)), (lambda s: (0, cl(s) + 1, 0)), (lambda s: (0, 1, 0))
    lag = lambda s: (0, jnp.maximum(s - 1, 0), 0)
    head_shape = jax.ShapeDtypeStruct((N_KV_HEADS, lp, HEAD_DIM), F32)
    meta_shape = jax.ShapeDtypeStruct((N_KV_HEADS, BLOCK, HEAD_DIM), F32)
    return _pcall(
        body, name="attn_bwd", grid=(nb + 1,),
        in_specs=[pl.BlockSpec((BLOCK, Q_DIM), lambda s: (cl(s), 0)), pl.BlockSpec((BLOCK, Q_DIM), lambda s: (cl(s), 0)),
                  kv(prev), kv(cur), kv(meta), kv(prev), kv(cur), kv(meta), _const_spec((1, N_Q_HEADS))],
        out_specs=[pl.BlockSpec((BLOCK, Q_DIM), lambda s: (cl(s), 0)), kv(lag), kv(lag),
                   _const_spec((N_KV_HEADS, BLOCK, HEAD_DIM)), _const_spec((N_KV_HEADS, BLOCK, HEAD_DIM)), _const_spec((1, LANES))],
        out_shape=[jax.ShapeDtypeStruct((lp, Q_DIM), F32), head_shape, head_shape, meta_shape, meta_shape,
                   jax.ShapeDtypeStruct((1, LANES), F32)],
        scratch_shapes=[pltpu.VMEM((N_KV_HEADS, BLOCK, HEAD_DIM), F32), pltpu.VMEM((N_KV_HEADS, BLOCK, HEAD_DIM), F32)],
        compiler_params=_params(("arbitrary",)),
    )(q, do, k_sh, k_sh, k_sh, v_sh, v_sh, v_sh, sinks)


CONV_CHUNK = 256


def _glu_masked(a_ref, g_ref, base):
    rows = base + lax.broadcasted_iota(jnp.int32, (BLOCK, 1), 0)
    return jnp.where(rows >= PAD_ROWS, a_ref[...] * _sigmoid(g_ref[...]), 0.0)


def _conv_fwd(zc, conv_w, conv_b, ln_g, ln_b):
    lp = zc.shape[0]
    cd = zc.shape[1] // 2
    nb = lp // BLOCK
    chunk = min(CONV_CHUNK, cd)
    back = lambda col: (lambda i: (jnp.maximum(i - 1, 0), col))
    lo = BLOCK - (CONV_WIDTH - 1)

    def body(ap_ref, gp_ref, ac_ref, gc_ref, w_ref, b_ref, lg_ref, lb_ref, co_ref, c2_ref, ext):
        i = pl.program_id(0)
        ext[0:BLOCK, :] = _glu_masked(ap_ref, gp_ref, (i - 1) * BLOCK)
        ext[BLOCK:2 * BLOCK, :] = _glu_masked(ac_ref, gc_ref, i * BLOCK)
        for c0 in range(0, cd, chunk):
            acc = jnp.zeros((BLOCK, chunk), F32)
            for k in range(CONV_WIDTH):
                acc = acc + ext[lo + k:lo + k + BLOCK, c0:c0 + chunk] * w_ref[k:k + 1, c0:c0 + chunk]
            co_ref[:, c0:c0 + chunk] = acc + b_ref[:, c0:c0 + chunk]
        x = co_ref[...]
        mu = jnp.mean(x, axis=-1, keepdims=True)
        xc = x - mu
        r = lax.rsqrt(jnp.mean(xc * xc, axis=-1, keepdims=True) + EPS)
        y = xc * r * lg_ref[...] + lb_ref[...]
        c2_ref[...] = (y * _sigmoid(y)).astype(BF16)

    return _pcall(
        body, name="conv_fwd", grid=(nb,),
        in_specs=[pl.BlockSpec((BLOCK, cd), back(0)), pl.BlockSpec((BLOCK, cd), back(1)), _row_spec(cd, 0), _row_spec(cd, 1),
                  _const_spec((CONV_ROWS, cd)), _const_spec((1, cd)), _const_spec((1, cd)), _const_spec((1, cd))],
        out_specs=[_row_spec(cd), _row_spec(cd)],
        out_shape=[jax.ShapeDtypeStruct((lp, cd), F32), jax.ShapeDtypeStruct((lp, cd), BF16)],
        scratch_shapes=[pltpu.VMEM((2 * BLOCK, cd), F32)],
        compiler_params=_params(("arbitrary",)),
    )(zc, zc, zc, zc, conv_w, conv_b, ln_g, ln_b)


def _conv_bwd_norm(dc2, conv_out, ln_g, ln_b):
    lp, cd = conv_out.shape

    def body(d_ref, x_ref, lg_ref, lb_ref, dco_ref, dlg_ref, dlb_ref, dcb_ref):
        i = pl.program_id(0)
        x = x_ref[...]
        g = lg_ref[...]
        mu = jnp.mean(x, axis=-1, keepdims=True)
        xc = x - mu
        r = lax.rsqrt(jnp.mean(xc * xc, axis=-1, keepdims=True) + EPS)
        xhat = xc * r
        y = xhat * g + lb_ref[...]
        sg = _sigmoid(y)
        dy = d_ref[...] * (sg * (1.0 + y * (1.0 - sg)))
        dxhat = dy * g
        dx = r * (dxhat - jnp.mean(dxhat, axis=-1, keepdims=True) - xhat * jnp.mean(dxhat * xhat, axis=-1, keepdims=True))
        dco_ref[...] = dx

        @pl.when(i == 0)
        def _():
            dlg_ref[...] = jnp.zeros_like(dlg_ref)
            dlb_ref[...] = jnp.zeros_like(dlb_ref)
            dcb_ref[...] = jnp.zeros_like(dcb_ref)

        dlg_ref[...] += jnp.sum(dy * xhat, axis=0, keepdims=True)
        dlb_ref[...] += jnp.sum(dy, axis=0, keepdims=True)
        dcb_ref[...] += jnp.sum(dx, axis=0, keepdims=True)

    vec = jax.ShapeDtypeStruct((1, cd), F32)
    return _pcall(
        body, name="conv_bwd_norm", grid=(lp // BLOCK,),
        in_specs=[_row_spec(cd), _row_spec(cd), _const_spec((1, cd)), _const_spec((1, cd))],
        out_specs=[_row_spec(cd), _const_spec((1, cd)), _const_spec((1, cd)), _const_spec((1, cd))],
        out_shape=[jax.ShapeDtypeStruct((lp, cd), F32), vec, vec, vec],
        compiler_params=_params(("arbitrary",)),
    )(dc2, conv_out, ln_g, ln_b)


def _conv_bwd_taps(dco, zc, conv_w):
    lp, cd = dco.shape
    nb = lp // BLOCK
    chunk = min(CONV_CHUNK, cd)
    back = lambda col: (lambda i: (jnp.maximum(i - 1, 0), col))
    fwd = lambda i: (jnp.minimum(i + 1, nb - 1), 0)
    lo = BLOCK - (CONV_WIDTH - 1)

    def body(dc_ref, dn_ref, ap_ref, gp_ref, ac_ref, gc_ref, w_ref, dz_ref, sum_ref, dw_ref, ext, dext, dcb):
        i = pl.program_id(0)
        ext[0:BLOCK, :] = _glu_masked(ap_ref, gp_ref, (i - 1) * BLOCK)
        ext[BLOCK:2 * BLOCK, :] = _glu_masked(ac_ref, gc_ref, i * BLOCK)
        dext[0:BLOCK, :] = dc_ref[...]
        dext[BLOCK:2 * BLOCK, :] = dn_ref[...] * (i < nb - 1).astype(F32)

        @pl.when(i == 0)
        def _():
            dw_ref[...] = jnp.zeros_like(dw_ref)
            sum_ref[...] = jnp.zeros_like(sum_ref)

        for c0 in range(0, cd, chunk):
            cols = slice(c0, c0 + chunk)
            dcur = dext[0:BLOCK, cols]
            acc = jnp.zeros((BLOCK, chunk), F32)
            for k in range(CONV_WIDTH):
                s = CONV_WIDTH - 1 - k
                acc = acc + dext[s:s + BLOCK, cols] * w_ref[k:k + 1, cols]
                dw_ref[k:k + 1, cols] += jnp.sum(dcur * ext[lo + k:lo + k + BLOCK, cols], axis=0, keepdims=True)
            dcb[:, cols] = acc
        rows = i * BLOCK + lax.broadcasted_iota(jnp.int32, (BLOCK, 1), 0)
        dc = jnp.where(rows >= PAD_ROWS, dcb[...], 0.0)
        a = ac_ref[...]
        sg = _sigmoid(gc_ref[...])
        da = dc * sg
        dg = dc * a * sg * (1.0 - sg)
        dz_ref[:, 0:cd] = da.astype(BF16)
        dz_ref[:, cd:2 * cd] = dg.astype(BF16)
        sum_ref[:, 0:cd] += jnp.sum(da, axis=0, keepdims=True)
        sum_ref[:, cd:2 * cd] += jnp.sum(dg, axis=0, keepdims=True)

    return _pcall(
        body, name="conv_bwd_taps", grid=(nb,),
        in_specs=[_row_spec(cd), pl.BlockSpec((BLOCK, cd), fwd),
                  pl.BlockSpec((BLOCK, cd), back(0)), pl.BlockSpec((BLOCK, cd), back(1)), _row_spec(cd, 0), _row_spec(cd, 1),
                  _const_spec((CONV_ROWS, cd))],
        out_specs=[_row_spec(2 * cd), _const_spec((1, 2 * cd)), _const_spec((CONV_ROWS, cd))],
        out_shape=[jax.ShapeDtypeStruct((lp, 2 * cd), BF16), jax.ShapeDtypeStruct((1, 2 * cd), F32),
                   jax.ShapeDtypeStruct((CONV_ROWS, cd), F32)],
        scratch_shapes=[pltpu.VMEM((2 * BLOCK, cd), F32), pltpu.VMEM((2 * BLOCK, cd), F32), pltpu.VMEM((BLOCK, cd), F32)],
        compiler_params=_params(("arbitrary",)),
    )(dco, dco, zc, zc, zc, zc, conv_w)


def _gate_fwd(a, b, zg):
    lp, d = a.shape

    def body(a_ref, b_ref, ga_ref, gb_ref, m_ref):
        m_ref[...] = (_sigmoid(ga_ref[...]) * a_ref[...] + _sigmoid(gb_ref[...]) * b_ref[...]).astype(BF16)

    return _pcall(
        body, name="gate_fwd", grid=(lp // BLOCK,),
        in_specs=[_row_spec(d), _row_spec(d), _row_spec(d, 0), _row_spec(d, 1)], out_specs=_row_spec(d),
        out_shape=jax.ShapeDtypeStruct((lp, d), BF16), compiler_params=_params(("parallel",)),
    )(a, b, zg, zg)


def _gate_bwd(dm, a, b, zg):
    lp, d = a.shape

    def body(dm_ref, a_ref, b_ref, ga_ref, gb_ref, da_ref, db_ref, dz_ref, sum_ref, dbias_ref):
        i = pl.program_id(0)
        dm_ = dm_ref[...]
        sa = _sigmoid(ga_ref[...])
        sb = _sigmoid(gb_ref[...])
        db = dm_ * sb
        dga = dm_ * a_ref[...] * sa * (1.0 - sa)
        dgb = dm_ * b_ref[...] * sb * (1.0 - sb)
        da_ref[...] = (dm_ * sa).astype(BF16)
        db_ref[...] = db.astype(BF16)
        dz_ref[:, 0:d] = dga.astype(BF16)
        dz_ref[:, d:2 * d] = dgb.astype(BF16)

        @pl.when(i == 0)
        def _():
            sum_ref[...] = jnp.zeros_like(sum_ref)
            dbias_ref[...] = jnp.zeros_like(dbias_ref)

        sum_ref[:, 0:d] += jnp.sum(dga, axis=0, keepdims=True)
        sum_ref[:, d:2 * d] += jnp.sum(dgb, axis=0, keepdims=True)
        dbias_ref[...] += jnp.sum(db, axis=0, keepdims=True)

    return _pcall(
        body, name="gate_bwd", grid=(lp // BLOCK,),
        in_specs=[_row_spec(d), _row_spec(d), _row_spec(d), _row_spec(d, 0), _row_spec(d, 1)],
        out_specs=[_row_spec(d), _row_spec(d), _row_spec(2 * d), _const_spec((1, 2 * d)), _const_spec((1, d))],
        out_shape=[jax.ShapeDtypeStruct((lp, d), BF16), jax.ShapeDtypeStruct((lp, d), BF16), jax.ShapeDtypeStruct((lp, 2 * d), BF16),
                   jax.ShapeDtypeStruct((1, 2 * d), F32), jax.ShapeDtypeStruct((1, d), F32)],
        compiler_params=_params(("arbitrary",)),
    )(dm, a, b, zg, zg)


def _swiglu_fwd(gu):
    lp = gu.shape[0]
    f = gu.shape[1] // 2

    def body(g_ref, u_ref, o_ref):
        g = g_ref[...]
        o_ref[...] = (g * _sigmoid(g) * u_ref[...]).astype(BF16)

    return _pcall(
        body, name="swiglu_fwd", grid=(lp // BLOCK,), in_specs=[_row_spec(f, 0), _row_spec(f, 1)], out_specs=_row_spec(f),
        out_shape=jax.ShapeDtypeStruct((lp, f), BF16), compiler_params=_params(("parallel",)),
    )(gu, gu)


def _swiglu_bwd(dact, gu):
    lp, f = dact.shape

    def body(d_ref, g_ref, u_ref, o_ref):
        g = g_ref[...]
        d = d_ref[...]
        sg = _sigmoid(g)
        o_ref[:, 0:f] = (d * u_ref[...] * (sg * (1.0 + g * (1.0 - sg)))).astype(BF16)
        o_ref[:, f:2 * f] = (d * g * sg).astype(BF16)

    return _pcall(
        body, name="swiglu_bwd", grid=(lp // BLOCK,), in_specs=[_row_spec(f), _row_spec(f, 0), _row_spec(f, 1)],
        out_specs=_row_spec(2 * f), out_shape=jax.ShapeDtypeStruct((lp, 2 * f), BF16), compiler_params=_params(("parallel",)),
    )(dact, gu, gu)


ANY = pl.BlockSpec(memory_space=pl.ANY)


def _all_gather_rows(x, name):
    r, c = x.shape

    def body(x_ref, out_ref, send_sems, recv_sems, local_sem):
        mx, my, mc = lax.axis_index("x"), lax.axis_index("y"), lax.axis_index("c")
        me, sibling = (mx, my, mc), (mx, my, 1 - mc)
        chips = [(1 - mx, my), (mx, 1 - my), (1 - mx, 1 - my)]

        def rows(px, py, pc):
            return out_ref.at[pl.ds((4 * px + 2 * py + pc) * r, r), :]

        def copy(k, block, to, src=None):
            return pltpu.make_async_remote_copy(
                src_ref=rows(*block) if src is None else src, dst_ref=rows(*block),
                send_sem=send_sems.at[k], recv_sem=recv_sems.at[k], device_id=to, device_id_type=MESH)

        mine = pltpu.make_async_copy(x_ref, rows(*me), local_sem)
        mine.start()
        first = [copy(0, me, sibling, src=x_ref)]
        first += [copy(1 + j, me, (*chip, mc), src=x_ref) for j, chip in enumerate(chips)]
        for cp in first:
            cp.start()
        passed = [copy(4 + j, (*chip, mc), sibling) for j, chip in enumerate(chips)]
        for j, chip in enumerate(chips):
            copy(1 + j, (*chip, mc), me).wait_recv()
            passed[j].start()
        copy(0, sibling, me).wait_recv()
        for j, chip in enumerate(chips):
            copy(4 + j, (*chip, 1 - mc), me).wait_recv()
        for cp in first + passed:
            cp.wait_send()
        mine.wait()

    return _pcall(
        body, name=name, in_specs=[ANY], out_specs=ANY, out_shape=jax.ShapeDtypeStruct((N_DEV * r, c), x.dtype),
        scratch_shapes=[pltpu.SemaphoreType.DMA((7,)), pltpu.SemaphoreType.DMA((7,)), pltpu.SemaphoreType.DMA(())],
    )(x)


def _pair_exchange(g, name):
    r = g.shape[0] // N_DEV
    c = g.shape[1]

    def body(g_ref, mine_ref, theirs_ref, send_sems, recv_sems, local_sems):
        mx, my, mc = lax.axis_index("x"), lax.axis_index("y"), lax.axis_index("c")
        sibling = (mx, my, 1 - mc)
        copies = []
        for j, (px, py) in enumerate(CHIPS):
            keep = g_ref.at[pl.ds((4 * px + 2 * py + mc) * r, r), :]
            give = g_ref.at[pl.ds((4 * px + 2 * py + 1 - mc) * r, r), :]
            lc = pltpu.make_async_copy(keep, mine_ref.at[j], local_sems.at[j])
            rc = pltpu.make_async_remote_copy(
                src_ref=give, dst_ref=theirs_ref.at[j], send_sem=send_sems.at[j], recv_sem=recv_sems.at[j],
                device_id=sibling, device_id_type=MESH)
            lc.start()
            rc.start()
            copies += [lc, rc]
        for cp in copies:
            cp.wait()

    blk = jax.ShapeDtypeStruct((len(CHIPS), r, c), g.dtype)
    return _pcall(
        body, name=name, in_specs=[ANY], out_specs=[ANY, ANY], out_shape=[blk, blk],
        scratch_shapes=[pltpu.SemaphoreType.DMA((4,)), pltpu.SemaphoreType.DMA((4,)), pltpu.SemaphoreType.DMA((4,))],
    )(g)


def _chip_exchange(ps, name):
    _, r, c = ps.shape

    def body(ps_ref, rx_ref, send_sems, recv_sems, local_sem):
        mx, my, mc = lax.axis_index("x"), lax.axis_index("y"), lax.axis_index("c")
        my_chip = 2 * mx + my
        chips = [(1 - mx, my), (mx, 1 - my), (1 - mx, 1 - my)]
        own = pltpu.make_async_copy(ps_ref.at[my_chip], rx_ref.at[my_chip], local_sem)
        own.start()
        sends = []
        for k, (px, py) in enumerate(chips):
            cp = pltpu.make_async_remote_copy(
                src_ref=ps_ref.at[2 * px + py], dst_ref=rx_ref.at[my_chip], send_sem=send_sems.at[k], recv_sem=recv_sems.at[k],
                device_id=(px, py, mc), device_id_type=MESH)
            cp.start()
            sends.append(cp)
        for k, (px, py) in enumerate(chips):
            pltpu.make_async_remote_copy(
                src_ref=ps_ref.at[my_chip], dst_ref=rx_ref.at[2 * px + py], send_sem=send_sems.at[k], recv_sem=recv_sems.at[k],
                device_id=(px, py, mc), device_id_type=MESH).wait_recv()
        for cp in sends:
            cp.wait_send()
        own.wait()

    return _pcall(
        body, name=name, in_specs=[ANY], out_specs=ANY, out_shape=jax.ShapeDtypeStruct(ps.shape, ps.dtype),
        scratch_shapes=[pltpu.SemaphoreType.DMA((3,)), pltpu.SemaphoreType.DMA((3,)), pltpu.SemaphoreType.DMA(())],
    )(ps)


def _pair_sum(mine, theirs, name):
    nch, r, c = mine.shape
    tr = _pick(r, max(16, ELEMENTWISE_BLOCK_BYTES // (2 * c)), 16)

    def body(a_ref, b_ref, o_ref):
        o_ref[...] = (a_ref[...].astype(F32) + b_ref[...].astype(F32)).astype(o_ref.dtype)

    spec = pl.BlockSpec((1, tr, c), lambda j, i: (j, i, 0))
    return _pcall(
        body, name=name, grid=(nch, r // tr), in_specs=[spec, spec], out_specs=spec,
        out_shape=jax.ShapeDtypeStruct(mine.shape, mine.dtype), compiler_params=_params(("parallel", "parallel")),
    )(mine, theirs)


def _sum_blocks(rx, name):
    n, r, c = rx.shape
    tr = _pick(r, max(8, ELEMENTWISE_BLOCK_BYTES // (4 * n * c)), 8)

    def body(x_ref, o_ref):
        acc = x_ref[0].astype(F32)
        for j in range(1, n):
            acc = acc + x_ref[j].astype(F32)
        o_ref[...] = acc

    return _pcall(
        body, name=name, grid=(r // tr,), in_specs=[pl.BlockSpec((n, tr, c), lambda i: (0, i, 0))],
        out_specs=pl.BlockSpec((tr, c), lambda i: (i, 0)), out_shape=jax.ShapeDtypeStruct((r, c), F32),
        compiler_params=_params(("parallel",)),
    )(rx)


def _reduce_scatter_rows(g, name):
    mine, theirs = _pair_exchange(g, name + "_pair_exchange")
    ps = _pair_sum(mine, theirs, name + "_pair_sum")
    rx = _chip_exchange(ps, name + "_chip_exchange")
    return _sum_blocks(rx, name + "_sum")


def _adamw(w, g, m, v, name):
    r, c = w.shape
    tr = _pick(r, max(8, ELEMENTWISE_BLOCK_BYTES // (4 * c)), 8)
    c1 = 1.0 - ADAM_B1 ** ADAM_STEP
    c2 = 1.0 - ADAM_B2 ** ADAM_STEP

    def body(w_ref, g_ref, m_ref, v_ref, d_ref, nm_ref, nv_ref):
        gg = g_ref[...]
        nm = ADAM_B1 * m_ref[...] + (1.0 - ADAM_B1) * gg
        nv = ADAM_B2 * v_ref[...] + (1.0 - ADAM_B2) * (gg * gg)
        d_ref[...] = -ADAM_LR * ((nm / c1) / (jnp.sqrt(nv / c2) + ADAM_EPS) + ADAM_WD * w_ref[...])
        nm_ref[...] = nm
        nv_ref[...] = nv

    spec = pl.BlockSpec((tr, c), lambda i: (i, 0))
    shp = jax.ShapeDtypeStruct((r, c), F32)
    return _pcall(
        body, name=name, grid=(r // tr,), in_specs=[spec] * 4, out_specs=[spec] * 3, out_shape=[shp] * 3,
        compiler_params=_params(("parallel",)),
    )(w, g, m, v)


def _pack(parts):
    flat, layout, row = [], [], 0
    for p in parts:
        n = p.size
        rows = -(-n // LANES)
        flat.append(jnp.pad(p.reshape(-1).astype(F32), (0, rows * LANES - n)))
        layout.append((row, n, p.shape))
        row += rows
    total = -(-row // 8) * 8
    if total > row:
        flat.append(jnp.zeros(((total - row) * LANES,), F32))
    return jnp.concatenate(flat).reshape(total, LANES), layout


def _unpack(slab, layout):
    flat = slab.reshape(-1)
    return [flat[row * LANES:row * LANES + n].reshape(shape) for row, n, shape in layout]


def kernel(x, meta_tokens, mix_norm_g, w_in, b_in, attn_sinks, conv_w, conv_b, conv_ln_g, conv_ln_b, w_attn_o, w_conv_o, b_conv_o, w_out, ffn_norm_g, w_gate_up, w_down, final_norm_g, loss_target, m_meta_tokens, m_mix_norm_g, m_w_in, m_b_in, m_attn_sinks, m_conv_w, m_conv_b, m_conv_ln_g, m_conv_ln_b, m_w_attn_o, m_w_conv_o, m_b_conv_o, m_w_out, m_ffn_norm_g, m_w_gate_up, m_w_down, m_final_norm_g, v_meta_tokens, v_mix_norm_g, v_w_in, v_b_in, v_attn_sinks, v_conv_w, v_conv_b, v_conv_ln_g, v_conv_ln_b, v_w_attn_o, v_w_conv_o, v_b_conv_o, v_w_out, v_ffn_norm_g, v_w_gate_up, v_w_down, v_final_norm_g):
    xs = x[0]
    tgt = loss_target[0]
    s, d = xs.shape
    lp = s + BLOCK
    cd = conv_b.shape[1]
    ffn = w_down.shape[1] * N_DEV
    dev = 4 * lax.axis_index("x") + 2 * lax.axis_index("y") + lax.axis_index("c")
    cw_cols = conv_w.shape[3]
    meta_cols = meta_tokens.shape[1]

    win_t = _all_gather_rows(w_in[0].T.astype(BF16), "gather_w_in")
    wao_t = _all_gather_rows(w_attn_o[0].T.astype(BF16), "gather_w_attn_o")
    wco_t = _all_gather_rows(w_conv_o[0].T.astype(BF16), "gather_w_conv_o")
    wout = _all_gather_rows(w_out[0].astype(BF16), "gather_w_out")
    wgu_t = _all_gather_rows(w_gate_up[0].T.astype(BF16), "gather_w_gate_up")
    wdown = _all_gather_rows(w_down[0].astype(BF16), "gather_w_down")
    small, small_layout = _pack([meta_tokens, jnp.pad(conv_w[0, :, 0, :], ((0, CONV_ROWS - CONV_WIDTH), (0, 0)))])
    small_all = _all_gather_rows(small, "gather_small").reshape(N_DEV, *small.shape)
    meta_parts, cw_parts = zip(*[_unpack(small_all[j], small_layout) for j in range(N_DEV)])
    meta_full = jnp.concatenate(meta_parts, axis=1)
    conv_w_full = jnp.concatenate(cw_parts, axis=1)

    ctab, stab = _rope_tables(lp)
    mm = functools.partial(_matmul, tm=1056, tn=512)

    h0, u = _prep(xs, meta_full, mix_norm_g)
    bq, bkv, bc, bg = b_in[:, :Q_DIM], b_in[:, Q_DIM:Q_DIM + 2 * KV_DIM], b_in[:, Q_DIM + 2 * KV_DIM:Q_DIM + 2 * KV_DIM + 2 * cd], b_in[:, Q_DIM + 2 * KV_DIM + 2 * cd:]
    o_kv, o_c, o_g = Q_DIM, Q_DIM + 2 * KV_DIM, Q_DIM + 2 * KV_DIM + 2 * cd
    zq = mm(u, win_t, mode="nt", name="in_proj_q", out_dtype=F32, tk=d, bias=bq, b_row_off=0, b_rows=Q_DIM)
    zkv = mm(u, win_t, mode="nt", name="in_proj_kv", out_dtype=F32, tk=d, bias=bkv, b_row_off=o_kv, b_rows=2 * KV_DIM)
    zc = mm(u, win_t, mode="nt", name="in_proj_conv", out_dtype=F32, tk=d, bias=bc, b_row_off=o_c, b_rows=2 * cd)
    zg = mm(u, win_t, mode="nt", name="in_proj_gates", out_dtype=F32, tk=d, bias=bg, b_row_off=o_g, b_rows=2 * d)
    q_rot, k_sh, v_sh = _rope_fwd(zq, zkv, ctab, stab)
    o = _attn_fwd(q_rot, k_sh, v_sh, attn_sinks)
    br_a = mm(o, wao_t, mode="nt", name="attn_out_proj", out_dtype=F32, tk=Q_DIM)
    conv_out, c2 = _conv_fwd(zc, conv_w_full, conv_b, conv_ln_g, conv_ln_b)
    br_b = mm(c2, wco_t, mode="nt", name="conv_out_proj", out_dtype=F32, tk=cd, bias=b_conv_o)
    merged = _gate_fwd(br_a, br_b, zg)
    h1 = mm(merged, wout, mode="nn", name="mix_out_proj", out_dtype=F32, tk=d, residual=h0)
    u2 = _rmsnorm_fwd(h1, ffn_norm_g, "ffn_rmsnorm")
    gu = _matmul(u2, wgu_t, mode="nt", name="ffn_gate_up", out_dtype=F32, tm=1056, tn=256, tk=d)
    act = _swiglu_fwd(gu)
    h2 = mm(act, wdown, mode="nn", name="ffn_down", out_dtype=F32, tk=ffn // 2, residual=h1)
    dh2, dh2_b, loss_part, d_final_g = _final(h2, tgt, final_norm_g.reshape(1, d))

    wgrad = functools.partial(_matmul, mode="tn", out_dtype=BF16, tk=lp, tn=1024, b_inner=False)
    dact = _matmul(dh2_b, wdown, mode="nt", name="ffn_down_dx", out_dtype=F32, tm=1056, tn=256, tk=d)
    g_wdown = wgrad(act, dh2_b, name="ffn_down_dw", tm=256)
    dgu = _swiglu_bwd(dact, gu)
    du2 = mm(dgu, wgu_t, mode="nn", name="ffn_gate_up_dx", out_dtype=F32, tk=ffn // 2)
    g_wgu_t = wgrad(dgu, u2, name="ffn_gate_up_dw", tm=256)
    dh1, dh1_b, d_ffn_g = _rmsnorm_bwd(du2, h1, ffn_norm_g, dh2, "ffn_rmsnorm_bwd")
    dmerged = mm(dh1_b, wout, mode="nt", name="mix_out_dx", out_dtype=F32, tk=d)
    g_wout = wgrad(merged, dh1_b, name="mix_out_dw", tm=512)
    d_a, d_b, dz_g, sum_g, d_bco = _gate_bwd(dmerged, br_a, br_b, zg)
    do = mm(d_a, wao_t, mode="nn", name="attn_out_dx", out_dtype=BF16, tk=d)
    g_wao_t = wgrad(d_a, o, name="attn_out_dw", tm=512)
    dc2 = mm(d_b, wco_t, mode="nn", name="conv_out_dx", out_dtype=F32, tk=d)
    g_wco_t = wgrad(d_b, c2, name="conv_out_dw", tm=512)
    dq, dk, dv, dkm, dvm, d_sinks = _attn_bwd(q_rot, k_sh, v_sh, attn_sinks, do)
    dz_qkv, sum_qkv = _rope_bwd(dq, dk, dv, dkm, dvm, ctab, stab)
    dco, d_ln_g, d_ln_b, d_conv_b = _conv_bwd_norm(dc2, conv_out, conv_ln_g, conv_ln_b)
    dz_c, sum_c, d_conv_w = _conv_bwd_taps(dco, zc, conv_w_full)
    dz = jnp.concatenate([dz_qkv, dz_c, dz_g], axis=1)
    d_b_in = jnp.concatenate([sum_qkv, sum_c, sum_g], axis=1)
    in_dim = dz.shape[1]
    du = mm(dz, win_t, mode="nn", name="in_proj_dx", out_dtype=F32, tk=in_dim // 4)
    g_win_t = wgrad(dz, u, name="in_proj_dw", tm=512)
    grad_x, d_meta, d_mix_g = _rmsnorm_bwd_first(du, h0, mix_norm_g, dh1)

    slab, slab_layout = _pack([loss_part[:, :1], d_mix_g, d_b_in, d_sinks[:, :N_Q_HEADS], d_conv_b, d_ln_g, d_ln_b, d_bco,
                               d_ffn_g, d_final_g, d_conv_w, d_meta])
    slab_all = _all_gather_rows(slab, "gather_small_grads").reshape(N_DEV, *slab.shape)
    (loss, g_mix_g, g_b_in, g_sinks, g_conv_b, g_ln_g, g_ln_b, g_bco, g_ffn_g, g_final_g, g_conv_w_full, g_meta_full
     ) = _unpack(_sum_blocks(slab_all, "sum_small_grads"), slab_layout)
    g_conv_w = lax.dynamic_slice(g_conv_w_full, (0, dev * cw_cols), (CONV_WIDTH, cw_cols)).reshape(conv_w.shape)
    g_meta = lax.dynamic_slice(g_meta_full, (0, dev * meta_cols), (N_META, meta_cols))
    g_final_g = g_final_g.reshape(final_norm_g.shape)

    g_w_in = _reduce_scatter_rows(g_win_t, "rs_w_in").T[None]
    g_w_attn_o = _reduce_scatter_rows(g_wao_t, "rs_w_attn_o").T[None]
    g_w_conv_o = _reduce_scatter_rows(g_wco_t, "rs_w_conv_o").T[None]
    g_w_out = _reduce_scatter_rows(g_wout, "rs_w_out")[None]
    g_w_gate_up = _reduce_scatter_rows(g_wgu_t, "rs_w_gate_up").T[None]
    g_w_down = _reduce_scatter_rows(g_wdown, "rs_w_down")[None]

    grads = dict(meta_tokens=g_meta, mix_norm_g=g_mix_g, w_in=g_w_in, b_in=g_b_in, attn_sinks=g_sinks, conv_w=g_conv_w,
                 conv_b=g_conv_b, conv_ln_g=g_ln_g, conv_ln_b=g_ln_b, w_attn_o=g_w_attn_o, w_conv_o=g_w_conv_o, b_conv_o=g_bco,
                 w_out=g_w_out, ffn_norm_g=g_ffn_g, w_gate_up=g_w_gate_up, w_down=g_w_down, final_norm_g=g_final_g)
    weights = dict(meta_tokens=meta_tokens, mix_norm_g=mix_norm_g, w_in=w_in, b_in=b_in, attn_sinks=attn_sinks, conv_w=conv_w,
                   conv_b=conv_b, conv_ln_g=conv_ln_g, conv_ln_b=conv_ln_b, w_attn_o=w_attn_o, w_conv_o=w_conv_o, b_conv_o=b_conv_o,
                   w_out=w_out, ffn_norm_g=ffn_norm_g, w_gate_up=w_gate_up, w_down=w_down, final_norm_g=final_norm_g)
    m_in = dict(meta_tokens=m_meta_tokens, mix_norm_g=m_mix_norm_g, w_in=m_w_in, b_in=m_b_in, attn_sinks=m_attn_sinks, conv_w=m_conv_w,
                conv_b=m_conv_b, conv_ln_g=m_conv_ln_g, conv_ln_b=m_conv_ln_b, w_attn_o=m_w_attn_o, w_conv_o=m_w_conv_o,
                b_conv_o=m_b_conv_o, w_out=m_w_out, ffn_norm_g=m_ffn_norm_g, w_gate_up=m_w_gate_up, w_down=m_w_down,
                final_norm_g=m_final_norm_g)
    v_in = dict(meta_tokens=v_meta_tokens, mix_norm_g=v_mix_norm_g, w_in=v_w_in, b_in=v_b_in, attn_sinks=v_attn_sinks, conv_w=v_conv_w,
                conv_b=v_conv_b, conv_ln_g=v_conv_ln_g, conv_ln_b=v_conv_ln_b, w_attn_o=v_w_attn_o, w_conv_o=v_w_conv_o,
                b_conv_o=v_b_conv_o, w_out=v_w_out, ffn_norm_g=v_ffn_norm_g, w_gate_up=v_w_gate_up, w_down=v_w_down,
                final_norm_g=v_final_norm_g)
    names = list(weights)
    big = ("w_in", "w_attn_o", "w_conv_o", "w_out", "w_gate_up", "w_down")
    delta, new_m, new_v = {}, {}, {}
    for n in big:
        shape = weights[n].shape
        two_d = (shape[1], shape[2])
        dl, nm, nv = _adamw(weights[n].reshape(two_d), grads[n].reshape(two_d), m_in[n].reshape(two_d), v_in[n].reshape(two_d),
                            "adamw_" + n)
        delta[n], new_m[n], new_v[n] = dl.reshape(shape), nm.reshape(shape), nv.reshape(shape)
    rest = [n for n in names if n not in big]
    w_slab, rest_layout = _pack([weights[n] for n in rest])
    g_slab, _ = _pack([grads[n] for n in rest])
    m_slab, _ = _pack([m_in[n] for n in rest])
    v_slab, _ = _pack([v_in[n] for n in rest])
    dl, nm, nv = _adamw(w_slab, g_slab, m_slab, v_slab, "adamw_small")
    for n, a, b, c in zip(rest, _unpack(dl, rest_layout), _unpack(nm, rest_layout), _unpack(nv, rest_layout)):
        delta[n], new_m[n], new_v[n] = a, b, c

    return (loss.reshape(()), grad_x[None], *[grads[n] for n in names], *[delta[n] for n in names],
            *[new_m[n] for n in names], *[new_v[n] for n in names])
```

```python
import functools

import jax
import jax.numpy as jnp
from jax import lax
from jax.experimental import pallas as pl
from jax.experimental.pallas import tpu as pltpu

F32 = jnp.float32
BF16 = jnp.bfloat16

N_DEV = 8
BLOCK = 128
N_META = 16
PAD_ROWS = BLOCK - N_META
HEAD_DIM = 64
N_Q_HEADS = 32
N_KV_HEADS = 4
GROUP = N_Q_HEADS // N_KV_HEADS
Q_DIM = N_Q_HEADS * HEAD_DIM
KV_DIM = N_KV_HEADS * HEAD_DIM
WINDOW = 128
CONV_WIDTH = 31
CONV_ROWS = 32
ROPE_THETA = 10000.0
EPS = 1e-6
ATTN_SCALE = HEAD_DIM ** -0.5
NEG = -1e30

ADAM_LR = 0.001
ADAM_B1 = 0.9
ADAM_B2 = 0.999
ADAM_EPS = 1e-08
ADAM_WD = 0.01
ADAM_STEP = 10

VMEM_LIMIT_BYTES = 56 * 1024 * 1024
LANES = 128
ELEMENTWISE_BLOCK_BYTES = 2 * 1024 * 1024
MESH = pl.DeviceIdType.MESH
CHIPS = ((0, 0), (0, 1), (1, 0), (1, 1))


def _pcall(body, after=None, **kw):
    if after is None:
        return pl.pallas_call(body, **kw)
    in_specs = list(kw.pop("in_specs"))
    n_in = len(in_specs)

    def ordered_body(*refs):
        return body(*refs[:n_in], *refs[n_in + 1:])

    call = pl.pallas_call(ordered_body, in_specs=in_specs + [pl.BlockSpec(memory_space=pl.ANY)], **kw)
    return lambda *args: call(*args, after)


def _params(semantics=None):
    if semantics is None:
        return pltpu.CompilerParams(vmem_limit_bytes=VMEM_LIMIT_BYTES)
    return pltpu.CompilerParams(dimension_semantics=semantics, vmem_limit_bytes=VMEM_LIMIT_BYTES)


def _pick(dim, pref, align):
    best = None
    t = align
    while t <= min(dim, pref):
        if dim % t == 0:
            best = t
        t += align
    return dim if best is None else best


def _sigmoid(x):
    return 1.0 / (1.0 + jnp.exp(-x))


def _matmul(a, b, *, mode, name, out_dtype, tm, tn, tk, bias=None, residual=None, b_inner=True,
            b_row_off=0, b_rows=None, after=None):
    if mode == "nn":
        m, k = a.shape
        n = b.shape[1]
    elif mode == "nt":
        m, k = a.shape
        n = b.shape[0] if b_rows is None else b_rows
    else:
        k, m = a.shape
        n = b.shape[1]
    tm = _pick(m, tm, 16)
    tn = _pick(n, tn, LANES)
    tk = _pick(k, tk, LANES if mode != "tn" else 16)
    nm, nn, nk = m // tm, n // tn, k // tk
    if mode == "nt":
        assert b_row_off % tn == 0
    off = b_row_off // tn if mode == "nt" else 0

    if b_inner:
        grid = (nm, nn, nk)
        ij = lambda g0, g1: (g0, g1)
    else:
        grid = (nn, nm, nk)
        ij = lambda g0, g1: (g1, g0)

    if mode == "tn":
        a_spec = pl.BlockSpec((tk, tm), lambda g0, g1, kk: (kk, ij(g0, g1)[0]))
    else:
        a_spec = pl.BlockSpec((tm, tk), lambda g0, g1, kk: (ij(g0, g1)[0], kk))
    if mode == "nt":
        b_spec = pl.BlockSpec((tn, tk), lambda g0, g1, kk: (ij(g0, g1)[1] + off, kk))
    else:
        b_spec = pl.BlockSpec((tk, tn), lambda g0, g1, kk: (kk, ij(g0, g1)[1]))
    o_spec = pl.BlockSpec((tm, tn), lambda g0, g1, kk: ij(g0, g1))
    in_specs = [a_spec, b_spec]
    args = [a, b]
    if bias is not None:
        in_specs.append(pl.BlockSpec((1, tn), lambda g0, g1, kk: (0, ij(g0, g1)[1])))
        args.append(bias)
    if residual is not None:
        in_specs.append(o_spec)
        args.append(residual)
    dims = {"nn": (((1,), (0,)), ((), ())), "nt": (((1,), (1,)), ((), ())), "tn": (((0,), (0,)), ((), ()))}[mode]
    has_bias, has_res = bias is not None, residual is not None

    def body(*refs):
        a_ref, b_ref = refs[0], refs[1]
        pos = 2
        bias_ref = res_ref = None
        if has_bias:
            bias_ref = refs[pos]
            pos += 1
        if has_res:
            res_ref = refs[pos]
            pos += 1
        o_ref = refs[pos]
        acc_ref = refs[pos + 1] if nk > 1 else None

        def finish(acc):
            if has_bias:
                acc = acc + bias_ref[...]
            if has_res:
                acc = acc + res_ref[...]
            o_ref[...] = acc.astype(out_dtype)

        p = lax.dot_general(a_ref[...], b_ref[...], dims, preferred_element_type=F32)
        if nk == 1:
            finish(p)
        else:
            kk = pl.program_id(2)

            @pl.when(kk == 0)
            def _():
                acc_ref[...] = p

            @pl.when(kk > 0)
            def _():
                acc_ref[...] += p

            @pl.when(kk == nk - 1)
            def _():
                finish(acc_ref[...])

    return _pcall(
        body, after=after, name=name, grid=grid, in_specs=in_specs, out_specs=o_spec,
        out_shape=jax.ShapeDtypeStruct((m, n), out_dtype),
        scratch_shapes=[pltpu.VMEM((tm, tn), F32)] if nk > 1 else [],
        compiler_params=_params(("parallel", "parallel", "arbitrary")),
    )(*args)


def _row_spec(width, col=0):
    return pl.BlockSpec((BLOCK, width), lambda i: (i, col))


def _const_spec(shape):
    nd = len(shape)
    return pl.BlockSpec(shape, lambda i: (0,) * nd)


def _prep(x, meta_full, g):
    s, d = x.shape
    lp = s + BLOCK
    nb = lp // BLOCK

    def body(x_ref, meta_ref, g_ref, h_ref, u_ref):
        i = pl.program_id(0)

        @pl.when(i == 0)
        def _():
            h_ref[0:PAD_ROWS, :] = jnp.zeros((PAD_ROWS, d), F32)
            h_ref[PAD_ROWS:BLOCK, :] = meta_ref[...]

        @pl.when(i > 0)
        def _():
            h_ref[...] = x_ref[...]

        h = h_ref[...]
        r = lax.rsqrt(jnp.mean(h * h, axis=-1, keepdims=True) + EPS)
        u_ref[...] = (h * r * g_ref[...]).astype(BF16)

    return _pcall(
        body, name="prep_rmsnorm", grid=(nb,),
        in_specs=[pl.BlockSpec((BLOCK, d), lambda i: (jnp.maximum(i - 1, 0), 0)), _const_spec((N_META, d)), _const_spec((1, d))],
        out_specs=[_row_spec(d), _row_spec(d)],
        out_shape=[jax.ShapeDtypeStruct((lp, d), F32), jax.ShapeDtypeStruct((lp, d), BF16)],
        compiler_params=_params(("arbitrary",)),
    )(x, meta_full, g)


def _rmsnorm_fwd(h, g, name):
    lp, d = h.shape

    def body(h_ref, g_ref, u_ref):
        x = h_ref[...]
        r = lax.rsqrt(jnp.mean(x * x, axis=-1, keepdims=True) + EPS)
        u_ref[...] = (x * r * g_ref[...]).astype(BF16)

    return _pcall(
        body, name=name, grid=(lp // BLOCK,), in_specs=[_row_spec(d), _const_spec((1, d))], out_specs=_row_spec(d),
        out_shape=jax.ShapeDtypeStruct((lp, d), BF16), compiler_params=_params(("parallel",)),
    )(h, g)


def _rms_bwd_core(dy, x, g):
    r = lax.rsqrt(jnp.mean(x * x, axis=-1, keepdims=True) + EPS)
    xhat = x * r
    dxhat = dy * g
    dx = r * (dxhat - xhat * jnp.mean(dxhat * xhat, axis=-1, keepdims=True))
    return dx, jnp.sum(dy * xhat, axis=0, keepdims=True)


def _rmsnorm_bwd(dy, h, g, dres, name):
    lp, d = h.shape

    def body(dy_ref, h_ref, g_ref, dres_ref, dh_ref, dhb_ref, dg_ref):
        i = pl.program_id(0)
        dx, dg = _rms_bwd_core(dy_ref[...], h_ref[...], g_ref[...])
        dh = dres_ref[...] + dx
        dh_ref[...] = dh
        dhb_ref[...] = dh.astype(BF16)

        @pl.when(i == 0)
        def _():
            dg_ref[...] = jnp.zeros_like(dg_ref)

        dg_ref[...] += dg

    return _pcall(
        body, name=name, grid=(lp // BLOCK,),
        in_specs=[_row_spec(d), _row_spec(d), _const_spec((1, d)), _row_spec(d)],
        out_specs=[_row_spec(d), _row_spec(d), _const_spec((1, d))],
        out_shape=[jax.ShapeDtypeStruct((lp, d), F32), jax.ShapeDtypeStruct((lp, d), BF16), jax.ShapeDtypeStruct((1, d), F32)],
        compiler_params=_params(("arbitrary",)),
    )(dy, h, g, dres)


def _rmsnorm_bwd_first(dy, h, g, dres):
    lp, d = h.shape
    s = lp - BLOCK

    def body(dy_ref, h_ref, g_ref, dres_ref, gx_ref, dmeta_ref, dg_ref):
        i = pl.program_id(0)
        dx, dg = _rms_bwd_core(dy_ref[...], h_ref[...], g_ref[...])
        dh = dres_ref[...] + dx
        gx_ref[...] = dh

        @pl.when(i == 0)
        def _():
            dmeta_ref[...] = dh[PAD_ROWS:BLOCK, :]
            dg_ref[...] = jnp.zeros_like(dg_ref)

        dg_ref[...] += dg

    return _pcall(
        body, name="rmsnorm_bwd_first", grid=(lp // BLOCK,),
        in_specs=[_row_spec(d), _row_spec(d), _const_spec((1, d)), _row_spec(d)],
        out_specs=[pl.BlockSpec((BLOCK, d), lambda i: (jnp.maximum(i - 1, 0), 0)), _const_spec((N_META, d)), _const_spec((1, d))],
        out_shape=[jax.ShapeDtypeStruct((s, d), F32), jax.ShapeDtypeStruct((N_META, d), F32), jax.ShapeDtypeStruct((1, d), F32)],
        compiler_params=_params(("arbitrary",)),
    )(dy, h, g, dres)


def _final(h2, tgt, g):
    lp, d = h2.shape

    def body(h_ref, t_ref, g_ref, dh_ref, dhb_ref, loss_ref, dg_ref):
        i = pl.program_id(0)
        x = h_ref[...]
        gg = g_ref[...]
        r = lax.rsqrt(jnp.mean(x * x, axis=-1, keepdims=True) + EPS)
        xhat = x * r
        y = xhat * gg
        live = (i > 0).astype(F32)
        err = (y - t_ref[...]) * live
        dy = err * (1.0 / d)
        dxhat = dy * gg
        dh = r * (dxhat - xhat * jnp.mean(dxhat * xhat, axis=-1, keepdims=True))
        dh_ref[...] = dh
        dhb_ref[...] = dh.astype(BF16)

        @pl.when(i == 0)
        def _():
            loss_ref[...] = jnp.zeros_like(loss_ref)
            dg_ref[...] = jnp.zeros_like(dg_ref)

        row_loss = jnp.mean(err * err, axis=-1, keepdims=True)
        loss_ref[...] += 0.5 * jnp.sum(row_loss, axis=0, keepdims=True)
        dg_ref[...] += jnp.sum(dy * xhat, axis=0, keepdims=True)

    return _pcall(
        body, name="final_norm_loss", grid=(lp // BLOCK,),
        in_specs=[_row_spec(d), pl.BlockSpec((BLOCK, d), lambda i: (jnp.maximum(i - 1, 0), 0)), _const_spec((1, d))],
        out_specs=[_row_spec(d), _row_spec(d), _const_spec((1, LANES)), _const_spec((1, d))],
        out_shape=[jax.ShapeDtypeStruct((lp, d), F32), jax.ShapeDtypeStruct((lp, d), BF16),
                   jax.ShapeDtypeStruct((1, LANES), F32), jax.ShapeDtypeStruct((1, d), F32)],
        compiler_params=_params(("arbitrary",)),
    )(h2, tgt, g)


def _swap_halves(x):
    w = x.shape[1]
    lane = lax.broadcasted_iota(jnp.int32, x.shape, 1)
    first = (lane & (HEAD_DIM - 1)) < (HEAD_DIM // 2)
    return jnp.where(first, pltpu.roll(x, w - HEAD_DIM // 2, 1), pltpu.roll(x, HEAD_DIM // 2, 1))


def _rope_tables(lp):
    pos = jnp.maximum(jnp.arange(lp, dtype=jnp.int32) - PAD_ROWS, 0).astype(F32)
    inv_freq = ROPE_THETA ** (-jnp.arange(0, HEAD_DIM, 2, dtype=F32) / HEAD_DIM)
    ang = pos[:, None] * inv_freq[None, :]
    c, s = jnp.cos(ang), jnp.sin(ang)
    reps = LANES // HEAD_DIM
    return jnp.tile(jnp.concatenate([c, c], axis=1), (1, reps)), jnp.tile(jnp.concatenate([-s, s], axis=1), (1, reps))


def _rope_fwd(zq, zkv, ctab, stab):
    lp = zq.shape[0]
    nb = lp // BLOCK
    back = lambda s: (jnp.maximum(s - 1, 0), 0)

    def body(zq_ref, zkv_ref, c_ref, s_ref, q_ref, k_ref, v_ref):
        step = pl.program_id(0)
        c128, s128 = c_ref[...], s_ref[...]

        def rope(x):
            reps = x.shape[1] // LANES
            return x * jnp.tile(c128, (1, reps)) + _swap_halves(x) * jnp.tile(s128, (1, reps))

        q_ref[...] = (rope(zq_ref[...]) * ATTN_SCALE).astype(BF16)
        kv = zkv_ref[...]
        k = rope(kv[:, :KV_DIM])
        v = kv[:, KV_DIM:]

        @pl.when(step == 0)
        def _():
            k_ref[...] = jnp.zeros_like(k_ref)
            v_ref[...] = jnp.zeros_like(v_ref)

        @pl.when(step > 0)
        def _():
            for h in range(N_KV_HEADS):
                k_ref[h] = k[:, h * HEAD_DIM:(h + 1) * HEAD_DIM].astype(BF16)
                v_ref[h] = v[:, h * HEAD_DIM:(h + 1) * HEAD_DIM].astype(BF16)

    kv_spec = pl.BlockSpec((N_KV_HEADS, BLOCK, HEAD_DIM), lambda s: (0, s, 0))
    return _pcall(
        body, name="rope_fwd", grid=(nb + 1,),
        in_specs=[pl.BlockSpec((BLOCK, Q_DIM), back), pl.BlockSpec((BLOCK, 2 * KV_DIM), back),
                  pl.BlockSpec((BLOCK, LANES), back), pl.BlockSpec((BLOCK, LANES), back)],
        out_specs=[pl.BlockSpec((BLOCK, Q_DIM), back), kv_spec, kv_spec],
        out_shape=[jax.ShapeDtypeStruct((lp, Q_DIM), BF16),
                   jax.ShapeDtypeStruct((N_KV_HEADS, lp + BLOCK, HEAD_DIM), BF16),
                   jax.ShapeDtypeStruct((N_KV_HEADS, lp + BLOCK, HEAD_DIM), BF16)],
        compiler_params=_params(("arbitrary",)),
    )(zq, zkv, ctab, stab)


def _rope_bwd(dq, dk, dv, dkm, dvm, ctab, stab):
    lp = dq.shape[0]
    width = Q_DIM + 2 * KV_DIM
    head_spec = pl.BlockSpec((N_KV_HEADS, BLOCK, HEAD_DIM), lambda i: (0, i, 0))
    meta_spec = _const_spec((N_KV_HEADS, BLOCK, HEAD_DIM))

    def body(dq_ref, dk_ref, dv_ref, dkm_ref, dvm_ref, c_ref, s_ref, dz_ref, sum_ref, kbuf, vbuf):
        i = pl.program_id(0)
        c128, s128 = c_ref[...], s_ref[...]
        first = (i == 0).astype(F32)

        def rope_t(x):
            reps = x.shape[1] // LANES
            return x * jnp.tile(c128, (1, reps)) + _swap_halves(x * jnp.tile(s128, (1, reps)))

        for h in range(N_KV_HEADS):
            kbuf[:, h * HEAD_DIM:(h + 1) * HEAD_DIM] = dk_ref[h] + first * dkm_ref[h]
            vbuf[:, h * HEAD_DIM:(h + 1) * HEAD_DIM] = dv_ref[h] + first * dvm_ref[h]
        dzq = rope_t(dq_ref[...] * ATTN_SCALE)
        dzk = rope_t(kbuf[...])
        dzv = vbuf[...]
        dz_ref[:, 0:Q_DIM] = dzq.astype(BF16)
        dz_ref[:, Q_DIM:Q_DIM + KV_DIM] = dzk.astype(BF16)
        dz_ref[:, Q_DIM + KV_DIM:width] = dzv.astype(BF16)

        @pl.when(i == 0)
        def _():
            sum_ref[...] = jnp.zeros_like(sum_ref)

        sum_ref[:, 0:Q_DIM] += jnp.sum(dzq, axis=0, keepdims=True)
        sum_ref[:, Q_DIM:Q_DIM + KV_DIM] += jnp.sum(dzk, axis=0, keepdims=True)
        sum_ref[:, Q_DIM + KV_DIM:width] += jnp.sum(dzv, axis=0, keepdims=True)

    return _pcall(
        body, name="rope_bwd", grid=(lp // BLOCK,),
        in_specs=[_row_spec(Q_DIM), head_spec, head_spec, meta_spec, meta_spec, _row_spec(LANES), _row_spec(LANES)],
        out_specs=[_row_spec(width), _const_spec((1, width))],
        out_shape=[jax.ShapeDtypeStruct((lp, width), BF16), jax.ShapeDtypeStruct((1, width), F32)],
        scratch_shapes=[pltpu.VMEM((BLOCK, KV_DIM), F32), pltpu.VMEM((BLOCK, KV_DIM), F32)],
        compiler_params=_params(("arbitrary",)),
    )(dq, dk, dv, dkm, dvm, ctab, stab)


def _attn_bias(i):
    r = lax.broadcasted_iota(jnp.int32, (BLOCK, 3 * BLOCK), 0)
    c = lax.broadcasted_iota(jnp.int32, (BLOCK, 3 * BLOCK), 1)
    qp = i * BLOCK + r - PAD_ROWS
    kp = (i - 1) * BLOCK + c - PAD_ROWS
    band = (c < 2 * BLOCK) & (kp >= N_META) & (kp <= qp) & (qp - kp < WINDOW)
    mp = c - 2 * BLOCK - PAD_ROWS
    meta = (c >= 2 * BLOCK) & (mp >= 0) & (mp <= qp)
    return jnp.where(band | meta, 0.0, NEG).astype(F32)


def _stack_heads(ref, h):
    return jnp.concatenate(
        [ref[:, (h * GROUP + g) * HEAD_DIM:(h * GROUP + g + 1) * HEAD_DIM] for g in range(GROUP)], axis=0)


def _attn_probs(qs, k3, bias8, sink):
    s = lax.dot_general(qs, k3, (((1,), (1,)), ((), ())), preferred_element_type=F32) + bias8
    m = jnp.maximum(jnp.max(s, axis=1, keepdims=True), sink)
    p = jnp.exp(s - m)
    ps = jnp.exp(sink - m)
    inv = 1.0 / (jnp.sum(p, axis=1, keepdims=True) + ps)
    return p * inv, ps * inv


def _sink_column(sink_ref, h):
    return jnp.concatenate(
        [jnp.broadcast_to(sink_ref[0:1, h * GROUP + g:h * GROUP + g + 1], (BLOCK, 1)) for g in range(GROUP)], axis=0)


def _attn_fwd(q, k_sh, v_sh, sinks):
    lp = q.shape[0]
    nb = lp // BLOCK
    kv = lambda f: pl.BlockSpec((N_KV_HEADS, BLOCK, HEAD_DIM), f)

    def body(q_ref, kp_ref, kc_ref, km_ref, vp_ref, vc_ref, vm_ref, sink_ref, o_ref):
        i = pl.program_id(0)
        bias8 = jnp.tile(_attn_bias(i), (GROUP, 1))
        for h in range(N_KV_HEADS):
            k3 = jnp.concatenate([kp_ref[h], kc_ref[h], km_ref[h]], axis=0)
            v3 = jnp.concatenate([vp_ref[h], vc_ref[h], vm_ref[h]], axis=0)
            qs = _stack_heads(q_ref, h)
            p, _ = _attn_probs(qs, k3, bias8, _sink_column(sink_ref, h))
            o = jnp.dot(p.astype(BF16), v3, preferred_element_type=F32)
            for g in range(GROUP):
                n = h * GROUP + g
                o_ref[:, n * HEAD_DIM:(n + 1) * HEAD_DIM] = o[g * BLOCK:(g + 1) * BLOCK].astype(BF16)

    prev, cur, meta = (lambda i: (0, i, 0)), (lambda i: (0, i + 1, 0)), (lambda i: (0, 1, 0))
    return _pcall(
        body, name="attn_fwd", grid=(nb,),
        in_specs=[_row_spec(Q_DIM), kv(prev), kv(cur), kv(meta), kv(prev), kv(cur), kv(meta), _const_spec((1, N_Q_HEADS))],
        out_specs=_row_spec(Q_DIM), out_shape=jax.ShapeDtypeStruct((lp, Q_DIM), BF16),
        compiler_params=_params(("parallel",)),
    )(q, k_sh, k_sh, k_sh, v_sh, v_sh, v_sh, sinks)


def _attn_bwd(q, k_sh, v_sh, sinks, do):
    lp = q.shape[0]
    nb = lp // BLOCK
    kv = lambda f: pl.BlockSpec((N_KV_HEADS, BLOCK, HEAD_DIM), f)
    cl = lambda s: jnp.minimum(s, nb - 1)

    def body(q_ref, do_ref, kp_ref, kc_ref, km_ref, vp_ref, vc_ref, vm_ref, sink_ref,
             dq_ref, dk_ref, dv_ref, dkm_ref, dvm_ref, dsink_ref, carry_k, carry_v):
        step = pl.program_id(0)

        @pl.when(step == 0)
        def _():
            carry_k[...] = jnp.zeros_like(carry_k)
            carry_v[...] = jnp.zeros_like(carry_v)
            dkm_ref[...] = jnp.zeros_like(dkm_ref)
            dvm_ref[...] = jnp.zeros_like(dvm_ref)
            dsink_ref[...] = jnp.zeros_like(dsink_ref)

        @pl.when(step < nb)
        def _():
            bias8 = jnp.tile(_attn_bias(step), (GROUP, 1))
            lane = lax.broadcasted_iota(jnp.int32, (1, LANES), 1)
            dsink = jnp.zeros((1, LANES), F32)
            for h in range(N_KV_HEADS):
                k3 = jnp.concatenate([kp_ref[h], kc_ref[h], km_ref[h]], axis=0)
                v3 = jnp.concatenate([vp_ref[h], vc_ref[h], vm_ref[h]], axis=0)
                qs = _stack_heads(q_ref, h)
                dos = _stack_heads(do_ref, h)
                p, psink = _attn_probs(qs, k3, bias8, _sink_column(sink_ref, h))
                dp = lax.dot_general(dos, v3, (((1,), (1,)), ((), ())), preferred_element_type=F32)
                delta = jnp.sum(p * dp, axis=1, keepdims=True)
                ds = (p * (dp - delta)).astype(BF16)
                dsk = -psink * delta
                for g in range(GROUP):
                    val = jnp.sum(dsk[g * BLOCK:(g + 1) * BLOCK], axis=0, keepdims=True)
                    dsink = dsink + jnp.where(lane == h * GROUP + g, val, 0.0)
                dqs = jnp.dot(ds, k3, preferred_element_type=F32)
                for g in range(GROUP):
                    n = h * GROUP + g
                    dq_ref[:, n * HEAD_DIM:(n + 1) * HEAD_DIM] = dqs[g * BLOCK:(g + 1) * BLOCK]
                dk3 = lax.dot_general(ds, qs, (((0,), (0,)), ((), ())), preferred_element_type=F32)
                dv3 = lax.dot_general(p.astype(BF16), dos, (((0,), (0,)), ((), ())), preferred_element_type=F32)
                dk_ref[h] = carry_k[h] + dk3[0:BLOCK]
                dv_ref[h] = carry_v[h] + dv3[0:BLOCK]
                carry_k[h] = dk3[BLOCK:2 * BLOCK]
                carry_v[h] = dv3[BLOCK:2 * BLOCK]
                dkm_ref[h] += dk3[2 * BLOCK:3 * BLOCK]
                dvm_ref[h] += dv3[2 * BLOCK:3 * BLOCK]
            dsink_ref[...] += dsink

        @pl.when(step == nb)
        def _():
            dk_ref[...] = carry_k[...]
            dv_ref[...] = carry_v[...]

    prev, cur, meta = (lambda s: (0, cl(s), 0)), (lambda s: (0, cl(s) + 1, 0)), (lambda s: (0, 1, 0))
    lag = lambda s: (0, jnp.maximum(s - 1, 0), 0)
    head_shape = jax.ShapeDtypeStruct((N_KV_HEADS, lp, HEAD_DIM), F32)
    meta_shape = jax.ShapeDtypeStruct((N_KV_HEADS, BLOCK, HEAD_DIM), F32)
    return _pcall(
        body, name="attn_bwd", grid=(nb + 1,),
        in_specs=[pl.BlockSpec((BLOCK, Q_DIM), lambda s: (cl(s), 0)), pl.BlockSpec((BLOCK, Q_DIM), lambda s: (cl(s), 0)),
                  kv(prev), kv(cur), kv(meta), kv(prev), kv(cur), kv(meta), _const_spec((1, N_Q_HEADS))],
        out_specs=[pl.BlockSpec((BLOCK, Q_DIM), lambda s: (cl(s), 0)), kv(lag), kv(lag),
                   _const_spec((N_KV_HEADS, BLOCK, HEAD_DIM)), _const_spec((N_KV_HEADS, BLOCK, HEAD_DIM)), _const_spec((1, LANES))],
        out_shape=[jax.ShapeDtypeStruct((lp, Q_DIM), F32), head_shape, head_shape, meta_shape, meta_shape,
                   jax.ShapeDtypeStruct((1, LANES), F32)],
        scratch_shapes=[pltpu.VMEM((N_KV_HEADS, BLOCK, HEAD_DIM), F32), pltpu.VMEM((N_KV_HEADS, BLOCK, HEAD_DIM), F32)],
        compiler_params=_params(("arbitrary",)),
    )(q, do, k_sh, k_sh, k_sh, v_sh, v_sh, v_sh, sinks)


CONV_CHUNK = 256


def _glu_masked(a_ref, g_ref, base):
    rows = base + lax.broadcasted_iota(jnp.int32, (BLOCK, 1), 0)
    return jnp.where(rows >= PAD_ROWS, a_ref[...] * _sigmoid(g_ref[...]), 0.0)


def _conv_fwd(zc, conv_w, conv_b, ln_g, ln_b):
    lp = zc.shape[0]
    cd = zc.shape[1] // 2
    nb = lp // BLOCK
    chunk = min(CONV_CHUNK, cd)
    back = lambda col: (lambda i: (jnp.maximum(i - 1, 0), col))
    lo = BLOCK - (CONV_WIDTH - 1)

    def body(ap_ref, gp_ref, ac_ref, gc_ref, w_ref, b_ref, lg_ref, lb_ref, co_ref, c2_ref, ext):
        i = pl.program_id(0)
        ext[0:BLOCK, :] = _glu_masked(ap_ref, gp_ref, (i - 1) * BLOCK)
        ext[BLOCK:2 * BLOCK, :] = _glu_masked(ac_ref, gc_ref, i * BLOCK)
        for c0 in range(0, cd, chunk):
            acc = jnp.zeros((BLOCK, chunk), F32)
            for k in range(CONV_WIDTH):
                acc = acc + ext[lo + k:lo + k + BLOCK, c0:c0 + chunk] * w_ref[k:k + 1, c0:c0 + chunk]
            co_ref[:, c0:c0 + chunk] = acc + b_ref[:, c0:c0 + chunk]
        x = co_ref[...]
        mu = jnp.mean(x, axis=-1, keepdims=True)
        xc = x - mu
        r = lax.rsqrt(jnp.mean(xc * xc, axis=-1, keepdims=True) + EPS)
        y = xc * r * lg_ref[...] + lb_ref[...]
        c2_ref[...] = (y * _sigmoid(y)).astype(BF16)

    return _pcall(
        body, name="conv_fwd", grid=(nb,),
        in_specs=[pl.BlockSpec((BLOCK, cd), back(0)), pl.BlockSpec((BLOCK, cd), back(1)), _row_spec(cd, 0), _row_spec(cd, 1),
                  _const_spec((CONV_ROWS, cd)), _const_spec((1, cd)), _const_spec((1, cd)), _const_spec((1, cd))],
        out_specs=[_row_spec(cd), _row_spec(cd)],
        out_shape=[jax.ShapeDtypeStruct((lp, cd), F32), jax.ShapeDtypeStruct((lp, cd), BF16)],
        scratch_shapes=[pltpu.VMEM((2 * BLOCK, cd), F32)],
        compiler_params=_params(("arbitrary",)),
    )(zc, zc, zc, zc, conv_w, conv_b, ln_g, ln_b)


def _conv_bwd_norm(dc2, conv_out, ln_g, ln_b):
    lp, cd = conv_out.shape

    def body(d_ref, x_ref, lg_ref, lb_ref, dco_ref, dlg_ref, dlb_ref, dcb_ref):
        i = pl.program_id(0)
        x = x_ref[...]
        g = lg_ref[...]
        mu = jnp.mean(x, axis=-1, keepdims=True)
        xc = x - mu
        r = lax.rsqrt(jnp.mean(xc * xc, axis=-1, keepdims=True) + EPS)
        xhat = xc * r
        y = xhat * g + lb_ref[...]
        sg = _sigmoid(y)
        dy = d_ref[...] * (sg * (1.0 + y * (1.0 - sg)))
        dxhat = dy * g
        dx = r * (dxhat - jnp.mean(dxhat, axis=-1, keepdims=True) - xhat * jnp.mean(dxhat * xhat, axis=-1, keepdims=True))
        dco_ref[...] = dx

        @pl.when(i == 0)
        def _():
            dlg_ref[...] = jnp.zeros_like(dlg_ref)
            dlb_ref[...] = jnp.zeros_like(dlb_ref)
            dcb_ref[...] = jnp.zeros_like(dcb_ref)

        dlg_ref[...] += jnp.sum(dy * xhat, axis=0, keepdims=True)
        dlb_ref[...] += jnp.sum(dy, axis=0, keepdims=True)
        dcb_ref[...] += jnp.sum(dx, axis=0, keepdims=True)

    vec = jax.ShapeDtypeStruct((1, cd), F32)
    return _pcall(
        body, name="conv_bwd_norm", grid=(lp // BLOCK,),
        in_specs=[_row_spec(cd), _row_spec(cd), _const_spec((1, cd)), _const_spec((1, cd))],
        out_specs=[_row_spec(cd), _const_spec((1, cd)), _const_spec((1, cd)), _const_spec((1, cd))],
        out_shape=[jax.ShapeDtypeStruct((lp, cd), F32), vec, vec, vec],
        compiler_params=_params(("arbitrary",)),
    )(dc2, conv_out, ln_g, ln_b)


def _conv_bwd_taps(dco, zc, conv_w):
    lp, cd = dco.shape
    nb = lp // BLOCK
    chunk = min(CONV_CHUNK, cd)
    back = lambda col: (lambda i: (jnp.maximum(i - 1, 0), col))
    fwd = lambda i: (jnp.minimum(i + 1, nb - 1), 0)
    lo = BLOCK - (CONV_WIDTH - 1)

    def body(dc_ref, dn_ref, ap_ref, gp_ref, ac_ref, gc_ref, w_ref, dz_ref, sum_ref, dw_ref, ext, dext, dcb):
        i = pl.program_id(0)
        ext[0:BLOCK, :] = _glu_masked(ap_ref, gp_ref, (i - 1) * BLOCK)
        ext[BLOCK:2 * BLOCK, :] = _glu_masked(ac_ref, gc_ref, i * BLOCK)
        dext[0:BLOCK, :] = dc_ref[...]
        dext[BLOCK:2 * BLOCK, :] = dn_ref[...] * (i < nb - 1).astype(F32)

        @pl.when(i == 0)
        def _():
            dw_ref[...] = jnp.zeros_like(dw_ref)
            sum_ref[...] = jnp.zeros_like(sum_ref)

        for c0 in range(0, cd, chunk):
            cols = slice(c0, c0 + chunk)
            dcur = dext[0:BLOCK, cols]
            acc = jnp.zeros((BLOCK, chunk), F32)
            for k in range(CONV_WIDTH):
                s = CONV_WIDTH - 1 - k
                acc = acc + dext[s:s + BLOCK, cols] * w_ref[k:k + 1, cols]
                dw_ref[k:k + 1, cols] += jnp.sum(dcur * ext[lo + k:lo + k + BLOCK, cols], axis=0, keepdims=True)
            dcb[:, cols] = acc
        rows = i * BLOCK + lax.broadcasted_iota(jnp.int32, (BLOCK, 1), 0)
        dc = jnp.where(rows >= PAD_ROWS, dcb[...], 0.0)
        a = ac_ref[...]
        sg = _sigmoid(gc_ref[...])
        da = dc * sg
        dg = dc * a * sg * (1.0 - sg)
        dz_ref[:, 0:cd] = da.astype(BF16)
        dz_ref[:, cd:2 * cd] = dg.astype(BF16)
        sum_ref[:, 0:cd] += jnp.sum(da, axis=0, keepdims=True)
        sum_ref[:, cd:2 * cd] += jnp.sum(dg, axis=0, keepdims=True)

    return _pcall(
        body, name="conv_bwd_taps", grid=(nb,),
        in_specs=[_row_spec(cd), pl.BlockSpec((BLOCK, cd), fwd),
                  pl.BlockSpec((BLOCK, cd), back(0)), pl.BlockSpec((BLOCK, cd), back(1)), _row_spec(cd, 0), _row_spec(cd, 1),
                  _const_spec((CONV_ROWS, cd))],
        out_specs=[_row_spec(2 * cd), _const_spec((1, 2 * cd)), _const_spec((CONV_ROWS, cd))],
        out_shape=[jax.ShapeDtypeStruct((lp, 2 * cd), BF16), jax.ShapeDtypeStruct((1, 2 * cd), F32),
                   jax.ShapeDtypeStruct((CONV_ROWS, cd), F32)],
        scratch_shapes=[pltpu.VMEM((2 * BLOCK, cd), F32), pltpu.VMEM((2 * BLOCK, cd), F32), pltpu.VMEM((BLOCK, cd), F32)],
        compiler_params=_params(("arbitrary",)),
    )(dco, dco, zc, zc, zc, zc, conv_w)


def _gate_fwd(a, b, zg):
    lp, d = a.shape

    def body(a_ref, b_ref, ga_ref, gb_ref, m_ref):
        m_ref[...] = (_sigmoid(ga_ref[...]) * a_ref[...] + _sigmoid(gb_ref[...]) * b_ref[...]).astype(BF16)

    return _pcall(
        body, name="gate_fwd", grid=(lp // BLOCK,),
        in_specs=[_row_spec(d), _row_spec(d), _row_spec(d, 0), _row_spec(d, 1)], out_specs=_row_spec(d),
        out_shape=jax.ShapeDtypeStruct((lp, d), BF16), compiler_params=_params(("parallel",)),
    )(a, b, zg, zg)


def _gate_bwd(dm, a, b, zg):
    lp, d = a.shape

    def body(dm_ref, a_ref, b_ref, ga_ref, gb_ref, da_ref, db_ref, dz_ref, sum_ref, dbias_ref):
        i = pl.program_id(0)
        dm_ = dm_ref[...]
        sa = _sigmoid(ga_ref[...])
        sb = _sigmoid(gb_ref[...])
        db = dm_ * sb
        dga = dm_ * a_ref[...] * sa * (1.0 - sa)
        dgb = dm_ * b_ref[...] * sb * (1.0 - sb)
        da_ref[...] = (dm_ * sa).astype(BF16)
        db_ref[...] = db.astype(BF16)
        dz_ref[:, 0:d] = dga.astype(BF16)
        dz_ref[:, d:2 * d] = dgb.astype(BF16)

        @pl.when(i == 0)
        def _():
            sum_ref[...] = jnp.zeros_like(sum_ref)
            dbias_ref[...] = jnp.zeros_like(dbias_ref)

        sum_ref[:, 0:d] += jnp.sum(dga, axis=0, keepdims=True)
        sum_ref[:, d:2 * d] += jnp.sum(dgb, axis=0, keepdims=True)
        dbias_ref[...] += jnp.sum(db, axis=0, keepdims=True)

    return _pcall(
        body, name="gate_bwd", grid=(lp // BLOCK,),
        in_specs=[_row_spec(d), _row_spec(d), _row_spec(d), _row_spec(d, 0), _row_spec(d, 1)],
        out_specs=[_row_spec(d), _row_spec(d), _row_spec(2 * d), _const_spec((1, 2 * d)), _const_spec((1, d))],
        out_shape=[jax.ShapeDtypeStruct((lp, d), BF16), jax.ShapeDtypeStruct((lp, d), BF16), jax.ShapeDtypeStruct((lp, 2 * d), BF16),
                   jax.ShapeDtypeStruct((1, 2 * d), F32), jax.ShapeDtypeStruct((1, d), F32)],
        compiler_params=_params(("arbitrary",)),
    )(dm, a, b, zg, zg)


def _swiglu_fwd(gu):
    lp = gu.shape[0]
    f = gu.shape[1] // 2

    def body(g_ref, u_ref, o_ref):
        g = g_ref[...]
        o_ref[...] = (g * _sigmoid(g) * u_ref[...]).astype(BF16)

    return _pcall(
        body, name="swiglu_fwd", grid=(lp // BLOCK,), in_specs=[_row_spec(f, 0), _row_spec(f, 1)], out_specs=_row_spec(f),
        out_shape=jax.ShapeDtypeStruct((lp, f), BF16), compiler_params=_params(("parallel",)),
    )(gu, gu)


def _swiglu_bwd(dact, gu):
    lp, f = dact.shape

    def body(d_ref, g_ref, u_ref, o_ref):
        g = g_ref[...]
        d = d_ref[...]
        sg = _sigmoid(g)
        o_ref[:, 0:f] = (d * u_ref[...] * (sg * (1.0 + g * (1.0 - sg)))).astype(BF16)
        o_ref[:, f:2 * f] = (d * g * sg).astype(BF16)

    return _pcall(
        body, name="swiglu_bwd", grid=(lp // BLOCK,), in_specs=[_row_spec(f), _row_spec(f, 0), _row_spec(f, 1)],
        out_specs=_row_spec(2 * f), out_shape=jax.ShapeDtypeStruct((lp, 2 * f), BF16), compiler_params=_params(("parallel",)),
    )(dact, gu, gu)


ANY = pl.BlockSpec(memory_space=pl.ANY)


def _all_gather_rows(x, name, after=None):
    r, c = x.shape

    def body(x_ref, out_ref, send_sems, recv_sems, local_sem):
        mx, my, mc = lax.axis_index("x"), lax.axis_index("y"), lax.axis_index("c")
        me, sibling = (mx, my, mc), (mx, my, 1 - mc)
        chips = [(1 - mx, my), (mx, 1 - my), (1 - mx, 1 - my)]

        def rows(px, py, pc):
            return out_ref.at[pl.ds((4 * px + 2 * py + pc) * r, r), :]

        def copy(k, block, to, src=None):
            return pltpu.make_async_remote_copy(
                src_ref=rows(*block) if src is None else src, dst_ref=rows(*block),
                send_sem=send_sems.at[k], recv_sem=recv_sems.at[k], device_id=to, device_id_type=MESH)

        mine = pltpu.make_async_copy(x_ref, rows(*me), local_sem)
        mine.start()
        first = [copy(0, me, sibling, src=x_ref)]
        first += [copy(1 + j, me, (*chip, mc), src=x_ref) for j, chip in enumerate(chips)]
        for cp in first:
            cp.start()
        passed = [copy(4 + j, (*chip, mc), sibling) for j, chip in enumerate(chips)]
        for j, chip in enumerate(chips):
            copy(1 + j, (*chip, mc), me).wait_recv()
            passed[j].start()
        copy(0, sibling, me).wait_recv()
        for j, chip in enumerate(chips):
            copy(4 + j, (*chip, 1 - mc), me).wait_recv()
        for cp in first + passed:
            cp.wait_send()
        mine.wait()

    return _pcall(
        body, after=after, name=name, in_specs=[ANY], out_specs=ANY, out_shape=jax.ShapeDtypeStruct((N_DEV * r, c), x.dtype),
        scratch_shapes=[pltpu.SemaphoreType.DMA((7,)), pltpu.SemaphoreType.DMA((7,)), pltpu.SemaphoreType.DMA(())],
    )(x)


HBM = pl.BlockSpec(memory_space=pltpu.HBM)
SEM = pl.BlockSpec(memory_space=pltpu.SEMAPHORE)
IN_FLIGHT = pltpu.CompilerParams(has_side_effects=pltpu.SideEffectType.DATAFLOW_SIDE_EFFECTING)
N_PEERS = 4


def _place_rows(shard, name):
    r, c = shard.shape
    tr = _pick(r, max(16, ELEMENTWISE_BLOCK_BYTES // (4 * c)), 16)
    steps = r // tr
    dev = (4 * lax.axis_index("x") + 2 * lax.axis_index("y") + lax.axis_index("c")).astype(jnp.int32).reshape(1)

    def body(dev_ref, x_ref, o_ref):
        o_ref[...] = x_ref[...].astype(BF16)

    return _pcall(
        body, name=name,
        grid_spec=pltpu.PrefetchScalarGridSpec(
            num_scalar_prefetch=1, grid=(steps,),
            in_specs=[pl.BlockSpec((tr, c), lambda i, dev_ref: (i, 0))],
            out_specs=pl.BlockSpec((tr, c), lambda i, dev_ref: (dev_ref[0] * steps + i, 0))),
        out_shape=jax.ShapeDtypeStruct((N_DEV * r, c), BF16), compiler_params=_params(("parallel",)),
    )(dev, shard)


def _gather_start(fulls, after):
    n = len(fulls)

    def body(*refs):
        full_refs = refs[:n]
        send_sems, recv_sems = refs[n + 1:2 * n + 1], refs[2 * n + 1:3 * n + 1]
        mx, my, mc = lax.axis_index("x"), lax.axis_index("y"), lax.axis_index("c")
        peers = [(mx, my, 1 - mc), (1 - mx, my, mc), (mx, 1 - my, mc), (1 - mx, 1 - my, mc)]
        for w in range(n):
            r = full_refs[w].shape[0] // N_DEV
            mine = full_refs[w].at[pl.ds((4 * mx + 2 * my + mc) * r, r), :]
            for k, peer in enumerate(peers):
                pltpu.make_async_remote_copy(
                    src_ref=mine, dst_ref=mine, send_sem=send_sems[w].at[k], recv_sem=recv_sems[w].at[k],
                    device_id=peer, device_id_type=MESH).start()

    sems = tuple(pltpu.SemaphoreType.DMA((N_PEERS,)) for _ in range(2 * n))
    outs = pl.pallas_call(
        body, name="gather_start", in_specs=[HBM] * (n + 1),
        out_specs=(SEM,) * (2 * n) + (HBM,) * n,
        out_shape=sems + tuple(pltpu.HBM(f.shape, f.dtype) for f in fulls),
        input_output_aliases={w: 2 * n + w for w in range(n)}, compiler_params=IN_FLIGHT,
    )(*[pltpu.with_memory_space_constraint(f, pltpu.HBM) for f in fulls], after)
    return outs[:n], outs[n:2 * n], outs[2 * n:]


def _gather_wait(full, send_sem, recv_sem, after, name):
    r = full.shape[0] // N_DEV

    def body(full_ref, send_ref, recv_ref, after_ref, out_ref):
        mx, my, mc = lax.axis_index("x"), lax.axis_index("y"), lax.axis_index("c")
        block = full_ref.at[pl.ds(0, r), :]
        for k in range(N_PEERS):
            cp = pltpu.make_async_remote_copy(
                src_ref=block, dst_ref=block, send_sem=send_ref.at[k], recv_sem=recv_ref.at[k],
                device_id=(mx, my, mc), device_id_type=MESH)
            cp.wait_send()
            cp.wait_recv()

    return pl.pallas_call(
        body, name=name, in_specs=[HBM, SEM, SEM, pl.BlockSpec(memory_space=pl.ANY)], out_specs=HBM,
        out_shape=pltpu.HBM(full.shape, full.dtype), input_output_aliases={0: 0}, compiler_params=IN_FLIGHT,
    )(full, send_sem, recv_sem, after)


def _gather_forward(full, name):
    r = full.shape[0] // N_DEV

    def body(full_ref, out_ref, send_sems, recv_sems):
        mx, my, mc = lax.axis_index("x"), lax.axis_index("y"), lax.axis_index("c")
        chips = [(1 - mx, my), (mx, 1 - my), (1 - mx, 1 - my)]
        copies = []
        for k, (px, py) in enumerate(chips):
            rows = full_ref.at[pl.ds((4 * px + 2 * py + mc) * r, r), :]
            cp = pltpu.make_async_remote_copy(
                src_ref=rows, dst_ref=rows, send_sem=send_sems.at[k], recv_sem=recv_sems.at[k],
                device_id=(mx, my, 1 - mc), device_id_type=MESH)
            cp.start()
            copies.append(cp)
        for cp in copies:
            cp.wait()

    return _pcall(
        body, name=name, in_specs=[ANY], out_specs=ANY, out_shape=jax.ShapeDtypeStruct(full.shape, full.dtype),
        input_output_aliases={0: 0},
        scratch_shapes=[pltpu.SemaphoreType.DMA((3,)), pltpu.SemaphoreType.DMA((3,))],
    )(full)


def _chip_exchange_start(ps, after, name):
    def body(ps_ref, rx_ref, after_ref, send_sems, recv_sems, ps_out, rx_out):
        mx, my, mc = lax.axis_index("x"), lax.axis_index("y"), lax.axis_index("c")
        chips = [(1 - mx, my), (mx, 1 - my), (1 - mx, 1 - my)]
        for k, (px, py) in enumerate(chips):
            pltpu.make_async_remote_copy(
                src_ref=ps_ref.at[2 * px + py], dst_ref=rx_ref.at[2 * mx + my], send_sem=send_sems.at[k], recv_sem=recv_sems.at[k],
                device_id=(px, py, mc), device_id_type=MESH).start()

    return pl.pallas_call(
        body, name=name, in_specs=[HBM, HBM, pl.BlockSpec(memory_space=pl.ANY)], out_specs=(SEM, SEM, HBM, HBM),
        out_shape=(pltpu.SemaphoreType.DMA((3,)), pltpu.SemaphoreType.DMA((3,)), pltpu.HBM(ps.shape, ps.dtype), pltpu.HBM(ps.shape, ps.dtype)),
        input_output_aliases={0: 2, 1: 3}, compiler_params=IN_FLIGHT,
    )(pltpu.with_memory_space_constraint(ps, pltpu.HBM), pltpu.with_memory_space_constraint(lax.empty(ps.shape, ps.dtype), pltpu.HBM), after)


def _chip_exchange_wait(send_sem, recv_sem, ps, rx, after, name):
    def body(ps_ref, rx_ref, send_ref, recv_ref, after_ref, ps_out, rx_out):
        mx, my, mc = lax.axis_index("x"), lax.axis_index("y"), lax.axis_index("c")
        for k in range(3):
            cp = pltpu.make_async_remote_copy(
                src_ref=ps_ref.at[0], dst_ref=rx_ref.at[0], send_sem=send_ref.at[k], recv_sem=recv_ref.at[k],
                device_id=(mx, my, mc), device_id_type=MESH)
            cp.wait_send()
            cp.wait_recv()

    return pl.pallas_call(
        body, name=name, in_specs=[HBM, HBM, SEM, SEM, pl.BlockSpec(memory_space=pl.ANY)], out_specs=(HBM, HBM),
        out_shape=(pltpu.HBM(ps.shape, ps.dtype), pltpu.HBM(rx.shape, rx.dtype)), input_output_aliases={0: 0, 1: 1},
        compiler_params=IN_FLIGHT,
    )(ps, rx, send_sem, recv_sem, after)


def _sum_chips(ps, rx, name):
    n, r, c = rx.shape
    tr = _pick(r, max(8, ELEMENTWISE_BLOCK_BYTES // (4 * n * c)), 8)
    chip = (2 * lax.axis_index("x") + lax.axis_index("y")).astype(jnp.int32).reshape(1)

    def body(chip_ref, own_ref, x_ref, o_ref):
        me = chip_ref[0]
        own = own_ref[0].astype(F32)
        acc = jnp.where(me == 0, own, x_ref[0].astype(F32))
        for j in range(1, n):
            acc = acc + jnp.where(me == j, own, x_ref[j].astype(F32))
        o_ref[...] = acc

    return _pcall(
        body, name=name,
        grid_spec=pltpu.PrefetchScalarGridSpec(
            num_scalar_prefetch=1, grid=(r // tr,),
            in_specs=[pl.BlockSpec((1, tr, c), lambda i, chip_ref: (chip_ref[0], i, 0)), pl.BlockSpec((n, tr, c), lambda i, chip_ref: (0, i, 0))],
            out_specs=pl.BlockSpec((tr, c), lambda i, chip_ref: (i, 0))),
        out_shape=jax.ShapeDtypeStruct((r, c), F32), compiler_params=_params(("parallel",)),
    )(chip, ps, rx)


def _pair_exchange(g, name):
    r = g.shape[0] // N_DEV
    c = g.shape[1]

    def body(g_ref, theirs_ref, send_sems, recv_sems):
        mx, my, mc = lax.axis_index("x"), lax.axis_index("y"), lax.axis_index("c")
        sibling = (mx, my, 1 - mc)
        copies = []
        for j, (px, py) in enumerate(CHIPS):
            give = g_ref.at[pl.ds((4 * px + 2 * py + 1 - mc) * r, r), :]
            rc = pltpu.make_async_remote_copy(
                src_ref=give, dst_ref=theirs_ref.at[j], send_sem=send_sems.at[j], recv_sem=recv_sems.at[j],
                device_id=sibling, device_id_type=MESH)
            rc.start()
            copies.append(rc)
        for cp in copies:
            cp.wait()

    return _pcall(
        body, name=name, in_specs=[ANY], out_specs=ANY, out_shape=jax.ShapeDtypeStruct((len(CHIPS), r, c), g.dtype),
        scratch_shapes=[pltpu.SemaphoreType.DMA((4,)), pltpu.SemaphoreType.DMA((4,))],
    )(g)


def _pair_sum(g, theirs, name):
    nch, r, c = theirs.shape
    tr = _pick(r, max(16, ELEMENTWISE_BLOCK_BYTES // (2 * c)), 16)
    core = lax.axis_index("c").astype(jnp.int32).reshape(1)

    def body(core_ref, a_ref, b_ref, o_ref):
        o_ref[...] = (a_ref[...].astype(F32) + b_ref[...].astype(F32)).astype(o_ref.dtype)

    spec = pl.BlockSpec((1, tr, c), lambda j, i, core_ref: (j, i, 0))
    own = pl.BlockSpec((1, tr, c), lambda j, i, core_ref: (2 * j + core_ref[0], i, 0))
    return _pcall(
        body, name=name,
        grid_spec=pltpu.PrefetchScalarGridSpec(num_scalar_prefetch=1, grid=(nch, r // tr), in_specs=[own, spec], out_specs=spec),
        out_shape=jax.ShapeDtypeStruct(theirs.shape, theirs.dtype), compiler_params=_params(("parallel", "parallel")),
    )(core, g.reshape(N_DEV, r, c), theirs)


def _sum_blocks(rx, name):
    n, r, c = rx.shape
    tr = _pick(r, max(8, ELEMENTWISE_BLOCK_BYTES // (4 * n * c)), 8)

    def body(x_ref, o_ref):
        acc = x_ref[0].astype(F32)
        for j in range(1, n):
            acc = acc + x_ref[j].astype(F32)
        o_ref[...] = acc

    return _pcall(
        body, name=name, grid=(r // tr,), in_specs=[pl.BlockSpec((n, tr, c), lambda i: (0, i, 0))],
        out_specs=pl.BlockSpec((tr, c), lambda i: (i, 0)), out_shape=jax.ShapeDtypeStruct((r, c), F32),
        compiler_params=_params(("parallel",)),
    )(rx)


def _adamw(w, g, m, v, name):
    r, c = w.shape
    tr = _pick(r, max(8, ELEMENTWISE_BLOCK_BYTES // (4 * c)), 8)
    c1 = 1.0 - ADAM_B1 ** ADAM_STEP
    c2 = 1.0 - ADAM_B2 ** ADAM_STEP

    def body(w_ref, g_ref, m_ref, v_ref, d_ref, nm_ref, nv_ref):
        gg = g_ref[...]
        nm = ADAM_B1 * m_ref[...] + (1.0 - ADAM_B1) * gg
        nv = ADAM_B2 * v_ref[...] + (1.0 - ADAM_B2) * (gg * gg)
        d_ref[...] = -ADAM_LR * ((nm / c1) / (jnp.sqrt(nv / c2) + ADAM_EPS) + ADAM_WD * w_ref[...])
        nm_ref[...] = nm
        nv_ref[...] = nv

    spec = pl.BlockSpec((tr, c), lambda i: (i, 0))
    shp = jax.ShapeDtypeStruct((r, c), F32)
    return _pcall(
        body, name=name, grid=(r // tr,), in_specs=[spec] * 4, out_specs=[spec] * 3, out_shape=[shp] * 3,
        compiler_params=_params(("parallel",)),
    )(w, g, m, v)


def _pack(parts):
    flat, layout, row = [], [], 0
    for p in parts:
        n = p.size
        rows = -(-n // LANES)
        flat.append(jnp.pad(p.reshape(-1).astype(F32), (0, rows * LANES - n)))
        layout.append((row, n, p.shape))
        row += rows
    total = -(-row // 8) * 8
    if total > row:
        flat.append(jnp.zeros(((total - row) * LANES,), F32))
    return jnp.concatenate(flat).reshape(total, LANES), layout


def _unpack(slab, layout):
    flat = slab.reshape(-1)
    return [flat[row * LANES:row * LANES + n].reshape(shape) for row, n, shape in layout]


def kernel(x, meta_tokens, mix_norm_g, w_in, b_in, attn_sinks, conv_w, conv_b, conv_ln_g, conv_ln_b, w_attn_o, w_conv_o, b_conv_o, w_out, ffn_norm_g, w_gate_up, w_down, final_norm_g, loss_target, m_meta_tokens, m_mix_norm_g, m_w_in, m_b_in, m_attn_sinks, m_conv_w, m_conv_b, m_conv_ln_g, m_conv_ln_b, m_w_attn_o, m_w_conv_o, m_b_conv_o, m_w_out, m_ffn_norm_g, m_w_gate_up, m_w_down, m_final_norm_g, v_meta_tokens, v_mix_norm_g, v_w_in, v_b_in, v_attn_sinks, v_conv_w, v_conv_b, v_conv_ln_g, v_conv_ln_b, v_w_attn_o, v_w_conv_o, v_b_conv_o, v_w_out, v_ffn_norm_g, v_w_gate_up, v_w_down, v_final_norm_g):
    xs = x[0]
    tgt = loss_target[0]
    s, d = xs.shape
    lp = s + BLOCK
    cd = conv_b.shape[1]
    ffn = w_down.shape[1] * N_DEV
    dev = 4 * lax.axis_index("x") + 2 * lax.axis_index("y") + lax.axis_index("c")
    cw_cols = conv_w.shape[3]
    meta_cols = meta_tokens.shape[1]

    small, small_layout = _pack([meta_tokens, jnp.pad(conv_w[0, :, 0, :], ((0, CONV_ROWS - CONV_WIDTH), (0, 0)))])
    small_flat = _all_gather_rows(small, "gather_small")
    small_all = small_flat.reshape(N_DEV, *small.shape)
    meta_parts, cw_parts = zip(*[_unpack(small_all[j], small_layout) for j in range(N_DEV)])
    meta_full = jnp.concatenate(meta_parts, axis=1)
    conv_w_full = jnp.concatenate(cw_parts, axis=1)
    placed = [_place_rows(w_in[0].T, "place_w_in"), _place_rows(w_attn_o[0].T, "place_w_attn_o"),
              _place_rows(w_conv_o[0].T, "place_w_conv_o"), _place_rows(w_out[0], "place_w_out"),
              _place_rows(w_gate_up[0].T, "place_w_gate_up"), _place_rows(w_down[0], "place_w_down")]
    g_send, g_recv, g_full = _gather_start(placed, small_flat)

    def gathered(w, after, name):
        return _gather_forward(_gather_wait(g_full[w], g_send[w], g_recv[w], after, "gather_wait_" + name), "gather_forward_" + name)

    ctab, stab = _rope_tables(lp)
    mm = functools.partial(_matmul, tm=1056, tn=512)

    h0, u = _prep(xs, meta_full, mix_norm_g)
    win_t = gathered(0, u, "w_in")
    bq, bkv, bc, bg = b_in[:, :Q_DIM], b_in[:, Q_DIM:Q_DIM + 2 * KV_DIM], b_in[:, Q_DIM + 2 * KV_DIM:Q_DIM + 2 * KV_DIM + 2 * cd], b_in[:, Q_DIM + 2 * KV_DIM + 2 * cd:]
    o_kv, o_c, o_g = Q_DIM, Q_DIM + 2 * KV_DIM, Q_DIM + 2 * KV_DIM + 2 * cd
    zq = mm(u, win_t, mode="nt", name="in_proj_q", out_dtype=F32, tk=d, bias=bq, b_row_off=0, b_rows=Q_DIM)
    zkv = mm(u, win_t, mode="nt", name="in_proj_kv", out_dtype=F32, tk=d, bias=bkv, b_row_off=o_kv, b_rows=2 * KV_DIM)
    zc = mm(u, win_t, mode="nt", name="in_proj_conv", out_dtype=F32, tk=d, bias=bc, b_row_off=o_c, b_rows=2 * cd)
    zg = mm(u, win_t, mode="nt", name="in_proj_gates", out_dtype=F32, tk=d, bias=bg, b_row_off=o_g, b_rows=2 * d)
    q_rot, k_sh, v_sh = _rope_fwd(zq, zkv, ctab, stab)
    o = _attn_fwd(q_rot, k_sh, v_sh, attn_sinks)
    wao_t = gathered(1, o, "w_attn_o")
    br_a = mm(o, wao_t, mode="nt", name="attn_out_proj", out_dtype=F32, tk=Q_DIM)
    conv_out, c2 = _conv_fwd(zc, conv_w_full, conv_b, conv_ln_g, conv_ln_b)
    wco_t = gathered(2, c2, "w_conv_o")
    br_b = mm(c2, wco_t, mode="nt", name="conv_out_proj", out_dtype=F32, tk=cd, bias=b_conv_o)
    merged = _gate_fwd(br_a, br_b, zg)
    wout = gathered(3, merged, "w_out")
    h1 = mm(merged, wout, mode="nn", name="mix_out_proj", out_dtype=F32, tk=d, residual=h0)
    u2 = _rmsnorm_fwd(h1, ffn_norm_g, "ffn_rmsnorm")
    wgu_t = gathered(4, u2, "w_gate_up")
    gu = _matmul(u2, wgu_t, mode="nt", name="ffn_gate_up", out_dtype=F32, tm=1056, tn=256, tk=d)
    act = _swiglu_fwd(gu)
    wdown = gathered(5, act, "w_down")
    h2 = mm(act, wdown, mode="nn", name="ffn_down", out_dtype=F32, tk=ffn // 2, residual=h1)
    dh2, dh2_b, loss_part, d_final_g = _final(h2, tgt, final_norm_g.reshape(1, d))

    wgrad = functools.partial(_matmul, mode="tn", out_dtype=BF16, tk=lp, tn=1024, b_inner=False)
    in_flight = {}

    def scatter_start(g, name):
        theirs = _pair_exchange(g, "rs_" + name + "_pair_exchange")
        ps = _pair_sum(g, theirs, "rs_" + name + "_pair_sum")
        in_flight[name] = _chip_exchange_start(ps, theirs, "rs_" + name + "_chip_start")
        return in_flight[name][2]

    g_wdown = wgrad(act, dh2_b, name="ffn_down_dw", tm=256)
    tok = scatter_start(g_wdown, "w_down")
    dact = _matmul(dh2_b, wdown, mode="nt", name="ffn_down_dx", out_dtype=F32, tm=1056, tn=256, tk=d, after=tok)
    dgu = _swiglu_bwd(dact, gu)
    g_wgu_t = wgrad(dgu, u2, name="ffn_gate_up_dw", tm=256)
    tok = scatter_start(g_wgu_t, "w_gate_up")
    du2 = mm(dgu, wgu_t, mode="nn", name="ffn_gate_up_dx", out_dtype=F32, tk=ffn // 2, after=tok)
    dh1, dh1_b, d_ffn_g = _rmsnorm_bwd(du2, h1, ffn_norm_g, dh2, "ffn_rmsnorm_bwd")
    g_wout = wgrad(merged, dh1_b, name="mix_out_dw", tm=512)
    tok = scatter_start(g_wout, "w_out")
    dmerged = mm(dh1_b, wout, mode="nt", name="mix_out_dx", out_dtype=F32, tk=d, after=tok)
    d_a, d_b, dz_g, sum_g, d_bco = _gate_bwd(dmerged, br_a, br_b, zg)
    g_wao_t = wgrad(d_a, o, name="attn_out_dw", tm=512)
    tok = scatter_start(g_wao_t, "w_attn_o")
    do = mm(d_a, wao_t, mode="nn", name="attn_out_dx", out_dtype=BF16, tk=d, after=tok)
    g_wco_t = wgrad(d_b, c2, name="conv_out_dw", tm=512)
    tok = scatter_start(g_wco_t, "w_conv_o")
    dc2 = mm(d_b, wco_t, mode="nn", name="conv_out_dx", out_dtype=F32, tk=d, after=tok)
    dq, dk, dv, dkm, dvm, d_sinks = _attn_bwd(q_rot, k_sh, v_sh, attn_sinks, do)
    dz_qkv, sum_qkv = _rope_bwd(dq, dk, dv, dkm, dvm, ctab, stab)
    dco, d_ln_g, d_ln_b, d_conv_b = _conv_bwd_norm(dc2, conv_out, conv_ln_g, conv_ln_b)
    dz_c, sum_c, d_conv_w = _conv_bwd_taps(dco, zc, conv_w_full)
    dz = jnp.concatenate([dz_qkv, dz_c, dz_g], axis=1)
    d_b_in = jnp.concatenate([sum_qkv, sum_c, sum_g], axis=1)
    in_dim = dz.shape[1]
    g_win_t = wgrad(dz, u, name="in_proj_dw", tm=512)
    tok = scatter_start(g_win_t, "w_in")
    du = mm(dz, win_t, mode="nn", name="in_proj_dx", out_dtype=F32, tk=in_dim // 4, after=tok)
    grad_x, d_meta, d_mix_g = _rmsnorm_bwd_first(du, h0, mix_norm_g, dh1)

    weights = dict(meta_tokens=meta_tokens, mix_norm_g=mix_norm_g, w_in=w_in, b_in=b_in, attn_sinks=attn_sinks, conv_w=conv_w,
                   conv_b=conv_b, conv_ln_g=conv_ln_g, conv_ln_b=conv_ln_b, w_attn_o=w_attn_o, w_conv_o=w_conv_o, b_conv_o=b_conv_o,
                   w_out=w_out, ffn_norm_g=ffn_norm_g, w_gate_up=w_gate_up, w_down=w_down, final_norm_g=final_norm_g)
    m_in = dict(meta_tokens=m_meta_tokens, mix_norm_g=m_mix_norm_g, w_in=m_w_in, b_in=m_b_in, attn_sinks=m_attn_sinks, conv_w=m_conv_w,
                conv_b=m_conv_b, conv_ln_g=m_conv_ln_g, conv_ln_b=m_conv_ln_b, w_attn_o=m_w_attn_o, w_conv_o=m_w_conv_o,
                b_conv_o=m_b_conv_o, w_out=m_w_out, ffn_norm_g=m_ffn_norm_g, w_gate_up=m_w_gate_up, w_down=m_w_down,
                final_norm_g=m_final_norm_g)
    v_in = dict(meta_tokens=v_meta_tokens, mix_norm_g=v_mix_norm_g, w_in=v_w_in, b_in=v_b_in, attn_sinks=v_attn_sinks, conv_w=v_conv_w,
                conv_b=v_conv_b, conv_ln_g=v_conv_ln_g, conv_ln_b=v_conv_ln_b, w_attn_o=v_w_attn_o, w_conv_o=v_w_conv_o,
                b_conv_o=v_b_conv_o, w_out=v_w_out, ffn_norm_g=v_ffn_norm_g, w_gate_up=v_w_gate_up, w_down=v_w_down,
                final_norm_g=v_final_norm_g)
    names = list(weights)
    grads, delta, new_m, new_v = {}, {}, {}, {}
    transposed = ("w_in", "w_attn_o", "w_conv_o", "w_gate_up")
    tok = grad_x
    for n in ("w_down", "w_gate_up", "w_out", "w_attn_o", "w_conv_o", "w_in"):
        send_sem, recv_sem, ps, rx = in_flight[n]
        ps, rx = _chip_exchange_wait(send_sem, recv_sem, ps, rx, tok, "rs_" + n + "_chip_wait")
        g = _sum_chips(ps, rx, "rs_" + n + "_sum")
        g = g.T if n in transposed else g
        shape = weights[n].shape
        dl, nm, nv = _adamw(weights[n].reshape(g.shape), g, m_in[n].reshape(g.shape), v_in[n].reshape(g.shape), "adamw_" + n)
        grads[n], delta[n], new_m[n], new_v[n] = g.reshape(shape), dl.reshape(shape), nm.reshape(shape), nv.reshape(shape)
        tok = dl

    slab, slab_layout = _pack([loss_part[:, :1], d_mix_g, d_b_in, d_sinks[:, :N_Q_HEADS], d_conv_b, d_ln_g, d_ln_b, d_bco,
                               d_ffn_g, d_final_g, d_conv_w, d_meta])
    slab_all = _all_gather_rows(slab, "gather_small_grads", after=tok).reshape(N_DEV, *slab.shape)
    (loss, g_mix_g, g_b_in, g_sinks, g_conv_b, g_ln_g, g_ln_b, g_bco, g_ffn_g, g_final_g, g_conv_w_full, g_meta_full
     ) = _unpack(_sum_blocks(slab_all, "sum_small_grads"), slab_layout)
    g_conv_w = lax.dynamic_slice(g_conv_w_full, (0, dev * cw_cols), (CONV_WIDTH, cw_cols)).reshape(conv_w.shape)
    g_meta = lax.dynamic_slice(g_meta_full, (0, dev * meta_cols), (N_META, meta_cols))
    g_final_g = g_final_g.reshape(final_norm_g.shape)
    grads.update(meta_tokens=g_meta, mix_norm_g=g_mix_g, b_in=g_b_in, attn_sinks=g_sinks, conv_w=g_conv_w, conv_b=g_conv_b,
                 conv_ln_g=g_ln_g, conv_ln_b=g_ln_b, b_conv_o=g_bco, ffn_norm_g=g_ffn_g, final_norm_g=g_final_g)
    rest = [n for n in names if n not in delta]
    w_slab, rest_layout = _pack([weights[n] for n in rest])
    g_slab, _ = _pack([grads[n] for n in rest])
    m_slab, _ = _pack([m_in[n] for n in rest])
    v_slab, _ = _pack([v_in[n] for n in rest])
    dl, nm, nv = _adamw(w_slab, g_slab, m_slab, v_slab, "adamw_small")
    for n, a, b, c in zip(rest, _unpack(dl, rest_layout), _unpack(nm, rest_layout), _unpack(nv, rest_layout)):
        delta[n], new_m[n], new_v[n] = a, b, c

    return (loss.reshape(()), grad_x[None], *[grads[n] for n in names], *[delta[n] for n in names],
            *[new_m[n] for n in names], *[new_v[n] for n in names])
```

```python
import functools
import math

import jax
import jax.numpy as jnp
from jax import lax
from jax.experimental import pallas as pl
from jax.experimental.pallas import tpu as pltpu

F32 = jnp.float32
BF16 = jnp.bfloat16

N_DEV = 8
BLOCK = 128
N_META = 16
PAD_ROWS = BLOCK - N_META
HEAD_DIM = 64
N_Q_HEADS = 32
N_KV_HEADS = 4
GROUP = N_Q_HEADS // N_KV_HEADS
Q_DIM = N_Q_HEADS * HEAD_DIM
KV_DIM = N_KV_HEADS * HEAD_DIM
WINDOW = 128
CONV_WIDTH = 31
CONV_ROWS = 32
ROPE_THETA = 10000.0
EPS = 1e-6
ATTN_SCALE = HEAD_DIM ** -0.5
NEG = -1e30

ADAM_LR = 0.001
ADAM_B1 = 0.9
ADAM_B2 = 0.999
ADAM_EPS = 1e-08
ADAM_WD = 0.01
ADAM_STEP = 10

VMEM_LIMIT_BYTES = 56 * 1024 * 1024
LANES = 128
ELEMENTWISE_BLOCK_BYTES = 2 * 1024 * 1024
MESH = pl.DeviceIdType.MESH
CHIPS = ((0, 0), (0, 1), (1, 0), (1, 1))


def _pcall(body, after=None, **kw):
    if after is None:
        return pl.pallas_call(body, **kw)
    in_specs = list(kw.pop("in_specs"))
    n_in = len(in_specs)

    def ordered_body(*refs):
        return body(*refs[:n_in], *refs[n_in + 1:])

    call = pl.pallas_call(ordered_body, in_specs=in_specs + [pl.BlockSpec(memory_space=pl.ANY)], **kw)
    return lambda *args: call(*args, after)


def _params(semantics=None):
    if semantics is None:
        return pltpu.CompilerParams(vmem_limit_bytes=VMEM_LIMIT_BYTES)
    return pltpu.CompilerParams(dimension_semantics=semantics, vmem_limit_bytes=VMEM_LIMIT_BYTES)


def _pick(dim, pref, align):
    best = None
    t = align
    while t <= min(dim, pref):
        if dim % t == 0:
            best = t
        t += align
    return dim if best is None else best


def _sigmoid(x):
    return 1.0 / (1.0 + jnp.exp(-x))


def _matmul(a, b, *, mode, name, out_dtype, tm, tn, tk, bias=None, residual=None, b_inner=True,
            b_row_off=0, b_rows=None, after=None):
    if mode == "nn":
        m, k = a.shape
        n = b.shape[1]
    elif mode == "nt":
        m, k = a.shape
        n = b.shape[0] if b_rows is None else b_rows
    else:
        k, m = a.shape
        n = b.shape[1]
    tm = _pick(m, tm, 16)
    tn = _pick(math.gcd(n, b_row_off) if mode == "nt" and b_row_off else n, tn, LANES)
    tk = _pick(k, tk, LANES if mode != "tn" else 16)
    nm, nn, nk = m // tm, n // tn, k // tk
    if mode == "nt":
        assert b_row_off % tn == 0
    off = b_row_off // tn if mode == "nt" else 0

    if b_inner:
        grid = (nm, nn, nk)
        ij = lambda g0, g1: (g0, g1)
    else:
        grid = (nn, nm, nk)
        ij = lambda g0, g1: (g1, g0)

    if mode == "tn":
        a_spec = pl.BlockSpec((tk, tm), lambda g0, g1, kk: (kk, ij(g0, g1)[0]))
    else:
        a_spec = pl.BlockSpec((tm, tk), lambda g0, g1, kk: (ij(g0, g1)[0], kk))
    if mode == "nt":
        b_spec = pl.BlockSpec((tn, tk), lambda g0, g1, kk: (ij(g0, g1)[1] + off, kk))
    else:
        b_spec = pl.BlockSpec((tk, tn), lambda g0, g1, kk: (kk, ij(g0, g1)[1]))
    o_spec = pl.BlockSpec((tm, tn), lambda g0, g1, kk: ij(g0, g1))
    in_specs = [a_spec, b_spec]
    args = [a, b]
    if bias is not None:
        in_specs.append(pl.BlockSpec((1, tn), lambda g0, g1, kk: (0, ij(g0, g1)[1])))
        args.append(bias)
    if residual is not None:
        in_specs.append(o_spec)
        args.append(residual)
    dims = {"nn": (((1,), (0,)), ((), ())), "nt": (((1,), (1,)), ((), ())), "tn": (((0,), (0,)), ((), ()))}[mode]
    has_bias, has_res = bias is not None, residual is not None

    def body(*refs):
        a_ref, b_ref = refs[0], refs[1]
        pos = 2
        bias_ref = res_ref = None
        if has_bias:
            bias_ref = refs[pos]
            pos += 1
        if has_res:
            res_ref = refs[pos]
            pos += 1
        o_ref = refs[pos]
        acc_ref = refs[pos + 1] if nk > 1 else None

        def finish(acc):
            if has_bias:
                acc = acc + bias_ref[...]
            if has_res:
                acc = acc + res_ref[...]
            o_ref[...] = acc.astype(out_dtype)

        p = lax.dot_general(a_ref[...], b_ref[...], dims, preferred_element_type=F32)
        if nk == 1:
            finish(p)
        else:
            kk = pl.program_id(2)

            @pl.when(kk == 0)
            def _():
                acc_ref[...] = p

            @pl.when(kk > 0)
            def _():
                acc_ref[...] += p

            @pl.when(kk == nk - 1)
            def _():
                finish(acc_ref[...])

    return _pcall(
        body, after=after, name=name, grid=grid, in_specs=in_specs, out_specs=o_spec,
        out_shape=jax.ShapeDtypeStruct((m, n), out_dtype),
        scratch_shapes=[pltpu.VMEM((tm, tn), F32)] if nk > 1 else [],
        compiler_params=_params(("parallel", "parallel", "arbitrary")),
    )(*args)


def _row_spec(width, col=0):
    return pl.BlockSpec((BLOCK, width), lambda i: (i, col))


def _const_spec(shape):
    nd = len(shape)
    return pl.BlockSpec(shape, lambda i: (0,) * nd)


def _prep(x, meta_full, g, after=None):
    s, d = x.shape
    lp = s + BLOCK
    nb = lp // BLOCK

    def body(x_ref, meta_ref, g_ref, h_ref, u_ref):
        i = pl.program_id(0)

        @pl.when(i == 0)
        def _():
            h_ref[0:PAD_ROWS, :] = jnp.zeros((PAD_ROWS, d), F32)
            h_ref[PAD_ROWS:BLOCK, :] = meta_ref[...]

        @pl.when(i > 0)
        def _():
            h_ref[...] = x_ref[...]

        h = h_ref[...]
        r = lax.rsqrt(jnp.mean(h * h, axis=-1, keepdims=True) + EPS)
        u_ref[...] = (h * r * g_ref[...]).astype(BF16)

    return _pcall(
        body, after=after, name="prep_rmsnorm", grid=(nb,),
        in_specs=[pl.BlockSpec((BLOCK, d), lambda i: (jnp.maximum(i - 1, 0), 0)), _const_spec((N_META, d)), _const_spec((1, d))],
        out_specs=[_row_spec(d), _row_spec(d)],
        out_shape=[jax.ShapeDtypeStruct((lp, d), F32), jax.ShapeDtypeStruct((lp, d), BF16)],
        compiler_params=_params(("arbitrary",)),
    )(x, meta_full, g)


def _rmsnorm_fwd(h, g, name):
    lp, d = h.shape

    def body(h_ref, g_ref, u_ref):
        x = h_ref[...]
        r = lax.rsqrt(jnp.mean(x * x, axis=-1, keepdims=True) + EPS)
        u_ref[...] = (x * r * g_ref[...]).astype(BF16)

    return _pcall(
        body, name=name, grid=(lp // BLOCK,), in_specs=[_row_spec(d), _const_spec((1, d))], out_specs=_row_spec(d),
        out_shape=jax.ShapeDtypeStruct((lp, d), BF16), compiler_params=_params(("parallel",)),
    )(h, g)


def _rms_bwd_core(dy, x, g):
    r = lax.rsqrt(jnp.mean(x * x, axis=-1, keepdims=True) + EPS)
    xhat = x * r
    dxhat = dy * g
    dx = r * (dxhat - xhat * jnp.mean(dxhat * xhat, axis=-1, keepdims=True))
    return dx, jnp.sum(dy * xhat, axis=0, keepdims=True)


def _rmsnorm_bwd(dy, h, g, dres, name):
    lp, d = h.shape

    def body(dy_ref, h_ref, g_ref, dres_ref, dh_ref, dhb_ref, dg_ref):
        i = pl.program_id(0)
        dx, dg = _rms_bwd_core(dy_ref[...], h_ref[...], g_ref[...])
        dh = dres_ref[...] + dx
        dh_ref[...] = dh
        dhb_ref[...] = dh.astype(BF16)

        @pl.when(i == 0)
        def _():
            dg_ref[...] = jnp.zeros_like(dg_ref)

        dg_ref[...] += dg

    return _pcall(
        body, name=name, grid=(lp // BLOCK,),
        in_specs=[_row_spec(d), _row_spec(d), _const_spec((1, d)), _row_spec(d)],
        out_specs=[_row_spec(d), _row_spec(d), _const_spec((1, d))],
        out_shape=[jax.ShapeDtypeStruct((lp, d), F32), jax.ShapeDtypeStruct((lp, d), BF16), jax.ShapeDtypeStruct((1, d), F32)],
        compiler_params=_params(("arbitrary",)),
    )(dy, h, g, dres)


def _rmsnorm_bwd_first(dy, h, g, dres):
    lp, d = h.shape
    s = lp - BLOCK

    def body(dy_ref, h_ref, g_ref, dres_ref, gx_ref, dmeta_ref, dg_ref):
        i = pl.program_id(0)
        dx, dg = _rms_bwd_core(dy_ref[...], h_ref[...], g_ref[...])
        dh = dres_ref[...] + dx
        gx_ref[...] = dh

        @pl.when(i == 0)
        def _():
            dmeta_ref[...] = dh[PAD_ROWS:BLOCK, :]
            dg_ref[...] = jnp.zeros_like(dg_ref)

        dg_ref[...] += dg

    return _pcall(
        body, name="rmsnorm_bwd_first", grid=(lp // BLOCK,),
        in_specs=[_row_spec(d), _row_spec(d), _const_spec((1, d)), _row_spec(d)],
        out_specs=[pl.BlockSpec((BLOCK, d), lambda i: (jnp.maximum(i - 1, 0), 0)), _const_spec((N_META, d)), _const_spec((1, d))],
        out_shape=[jax.ShapeDtypeStruct((s, d), F32), jax.ShapeDtypeStruct((N_META, d), F32), jax.ShapeDtypeStruct((1, d), F32)],
        compiler_params=_params(("arbitrary",)),
    )(dy, h, g, dres)


def _final(h2, tgt, g):
    lp, d = h2.shape

    def body(h_ref, t_ref, g_ref, dh_ref, dhb_ref, loss_ref, dg_ref):
        i = pl.program_id(0)
        x = h_ref[...]
        gg = g_ref[...]
        r = lax.rsqrt(jnp.mean(x * x, axis=-1, keepdims=True) + EPS)
        xhat = x * r
        y = xhat * gg
        live = (i > 0).astype(F32)
        err = (y - t_ref[...]) * live
        dy = err * (1.0 / d)
        dxhat = dy * gg
        dh = r * (dxhat - xhat * jnp.mean(dxhat * xhat, axis=-1, keepdims=True))
        dh_ref[...] = dh
        dhb_ref[...] = dh.astype(BF16)

        @pl.when(i == 0)
        def _():
            loss_ref[...] = jnp.zeros_like(loss_ref)
            dg_ref[...] = jnp.zeros_like(dg_ref)

        row_loss = jnp.mean(err * err, axis=-1, keepdims=True)
        loss_ref[...] += 0.5 * jnp.sum(row_loss, axis=0, keepdims=True)
        dg_ref[...] += jnp.sum(dy * xhat, axis=0, keepdims=True)

    return _pcall(
        body, name="final_norm_loss", grid=(lp // BLOCK,),
        in_specs=[_row_spec(d), pl.BlockSpec((BLOCK, d), lambda i: (jnp.maximum(i - 1, 0), 0)), _const_spec((1, d))],
        out_specs=[_row_spec(d), _row_spec(d), _const_spec((1, LANES)), _const_spec((1, d))],
        out_shape=[jax.ShapeDtypeStruct((lp, d), F32), jax.ShapeDtypeStruct((lp, d), BF16),
                   jax.ShapeDtypeStruct((1, LANES), F32), jax.ShapeDtypeStruct((1, d), F32)],
        compiler_params=_params(("arbitrary",)),
    )(h2, tgt, g)


def _swap_halves(x):
    w = x.shape[1]
    lane = lax.broadcasted_iota(jnp.int32, x.shape, 1)
    first = (lane & (HEAD_DIM - 1)) < (HEAD_DIM // 2)
    return jnp.where(first, pltpu.roll(x, w - HEAD_DIM // 2, 1), pltpu.roll(x, HEAD_DIM // 2, 1))


def _rope_tables(lp):
    pos = jnp.maximum(jnp.arange(lp, dtype=jnp.int32) - PAD_ROWS, 0).astype(F32)
    inv_freq = ROPE_THETA ** (-jnp.arange(0, HEAD_DIM, 2, dtype=F32) / HEAD_DIM)
    ang = pos[:, None] * inv_freq[None, :]
    c, s = jnp.cos(ang), jnp.sin(ang)
    reps = LANES // HEAD_DIM
    return jnp.tile(jnp.concatenate([c, c], axis=1), (1, reps)), jnp.tile(jnp.concatenate([-s, s], axis=1), (1, reps))


def _rope_fwd(zq, zkv, ctab, stab, after=None):
    lp = zq.shape[0]
    nb = lp // BLOCK
    back = lambda s: (jnp.maximum(s - 1, 0), 0)

    def body(zq_ref, zkv_ref, c_ref, s_ref, q_ref, k_ref, v_ref):
        step = pl.program_id(0)
        c128, s128 = c_ref[...], s_ref[...]

        def rope(x):
            reps = x.shape[1] // LANES
            return x * jnp.tile(c128, (1, reps)) + _swap_halves(x) * jnp.tile(s128, (1, reps))

        q_ref[...] = (rope(zq_ref[...]) * ATTN_SCALE).astype(BF16)
        kv = zkv_ref[...]
        k = rope(kv[:, :KV_DIM])
        v = kv[:, KV_DIM:]

        @pl.when(step == 0)
        def _():
            k_ref[...] = jnp.zeros_like(k_ref)
            v_ref[...] = jnp.zeros_like(v_ref)

        @pl.when(step > 0)
        def _():
            for h in range(N_KV_HEADS):
                k_ref[h] = k[:, h * HEAD_DIM:(h + 1) * HEAD_DIM].astype(BF16)
                v_ref[h] = v[:, h * HEAD_DIM:(h + 1) * HEAD_DIM].astype(BF16)

    kv_spec = pl.BlockSpec((N_KV_HEADS, BLOCK, HEAD_DIM), lambda s: (0, s, 0))
    return _pcall(
        body, after=after, name="rope_fwd", grid=(nb + 1,),
        in_specs=[pl.BlockSpec((BLOCK, Q_DIM), back), pl.BlockSpec((BLOCK, 2 * KV_DIM), back),
                  pl.BlockSpec((BLOCK, LANES), back), pl.BlockSpec((BLOCK, LANES), back)],
        out_specs=[pl.BlockSpec((BLOCK, Q_DIM), back), kv_spec, kv_spec],
        out_shape=[jax.ShapeDtypeStruct((lp, Q_DIM), BF16),
                   jax.ShapeDtypeStruct((N_KV_HEADS, lp + BLOCK, HEAD_DIM), BF16),
                   jax.ShapeDtypeStruct((N_KV_HEADS, lp + BLOCK, HEAD_DIM), BF16)],
        compiler_params=_params(("arbitrary",)),
    )(zq, zkv, ctab, stab)


def _rope_bwd(dq, dk, dv, dkm, dvm, ctab, stab):
    lp = dq.shape[0]
    width = Q_DIM + 2 * KV_DIM
    head_spec = pl.BlockSpec((N_KV_HEADS, BLOCK, HEAD_DIM), lambda i: (0, i, 0))
    meta_spec = _const_spec((N_KV_HEADS, BLOCK, HEAD_DIM))

    def body(dq_ref, dk_ref, dv_ref, dkm_ref, dvm_ref, c_ref, s_ref, dz_ref, sum_ref, kbuf, vbuf):
        i = pl.program_id(0)
        c128, s128 = c_ref[...], s_ref[...]
        first = (i == 0).astype(F32)

        def rope_t(x):
            reps = x.shape[1] // LANES
            return x * jnp.tile(c128, (1, reps)) + _swap_halves(x * jnp.tile(s128, (1, reps)))

        for h in range(N_KV_HEADS):
            kbuf[:, h * HEAD_DIM:(h + 1) * HEAD_DIM] = dk_ref[h] + first * dkm_ref[h]
            vbuf[:, h * HEAD_DIM:(h + 1) * HEAD_DIM] = dv_ref[h] + first * dvm_ref[h]
        dzq = rope_t(dq_ref[...] * ATTN_SCALE)
        dzk = rope_t(kbuf[...])
        dzv = vbuf[...]
        dz_ref[:, 0:Q_DIM] = dzq.astype(BF16)
        dz_ref[:, Q_DIM:Q_DIM + KV_DIM] = dzk.astype(BF16)
        dz_ref[:, Q_DIM + KV_DIM:width] = dzv.astype(BF16)

        @pl.when(i == 0)
        def _():
            sum_ref[...] = jnp.zeros_like(sum_ref)

        sum_ref[:, 0:Q_DIM] += jnp.sum(dzq, axis=0, keepdims=True)
        sum_ref[:, Q_DIM:Q_DIM + KV_DIM] += jnp.sum(dzk, axis=0, keepdims=True)
        sum_ref[:, Q_DIM + KV_DIM:width] += jnp.sum(dzv, axis=0, keepdims=True)

    return _pcall(
        body, name="rope_bwd", grid=(lp // BLOCK,),
        in_specs=[_row_spec(Q_DIM), head_spec, head_spec, meta_spec, meta_spec, _row_spec(LANES), _row_spec(LANES)],
        out_specs=[_row_spec(width), _const_spec((1, width))],
        out_shape=[jax.ShapeDtypeStruct((lp, width), BF16), jax.ShapeDtypeStruct((1, width), F32)],
        scratch_shapes=[pltpu.VMEM((BLOCK, KV_DIM), F32), pltpu.VMEM((BLOCK, KV_DIM), F32)],
        compiler_params=_params(("arbitrary",)),
    )(dq, dk, dv, dkm, dvm, ctab, stab)


def _attn_bias(i):
    r = lax.broadcasted_iota(jnp.int32, (BLOCK, 3 * BLOCK), 0)
    c = lax.broadcasted_iota(jnp.int32, (BLOCK, 3 * BLOCK), 1)
    qp = i * BLOCK + r - PAD_ROWS
    kp = (i - 1) * BLOCK + c - PAD_ROWS
    band = (c < 2 * BLOCK) & (kp >= N_META) & (kp <= qp) & (qp - kp < WINDOW)
    mp = c - 2 * BLOCK - PAD_ROWS
    meta = (c >= 2 * BLOCK) & (mp >= 0) & (mp <= qp)
    return jnp.where(band | meta, 0.0, NEG).astype(F32)


def _stack_heads(ref, h):
    return jnp.concatenate(
        [ref[:, (h * GROUP + g) * HEAD_DIM:(h * GROUP + g + 1) * HEAD_DIM] for g in range(GROUP)], axis=0)


def _attn_probs(qs, k3, bias8, sink):
    s = lax.dot_general(qs, k3, (((1,), (1,)), ((), ())), preferred_element_type=F32) + bias8
    m = jnp.maximum(jnp.max(s, axis=1, keepdims=True), sink)
    p = jnp.exp(s - m)
    ps = jnp.exp(sink - m)
    inv = 1.0 / (jnp.sum(p, axis=1, keepdims=True) + ps)
    return p * inv, ps * inv


def _sink_column(sink_ref, h):
    return jnp.concatenate(
        [jnp.broadcast_to(sink_ref[0:1, h * GROUP + g:h * GROUP + g + 1], (BLOCK, 1)) for g in range(GROUP)], axis=0)


def _attn_fwd(q, k_sh, v_sh, sinks):
    lp = q.shape[0]
    nb = lp // BLOCK
    kv = lambda f: pl.BlockSpec((N_KV_HEADS, BLOCK, HEAD_DIM), f)

    def body(q_ref, kp_ref, kc_ref, km_ref, vp_ref, vc_ref, vm_ref, sink_ref, o_ref):
        i = pl.program_id(0)
        bias8 = jnp.tile(_attn_bias(i), (GROUP, 1))
        for h in range(N_KV_HEADS):
            k3 = jnp.concatenate([kp_ref[h], kc_ref[h], km_ref[h]], axis=0)
            v3 = jnp.concatenate([vp_ref[h], vc_ref[h], vm_ref[h]], axis=0)
            qs = _stack_heads(q_ref, h)
            p, _ = _attn_probs(qs, k3, bias8, _sink_column(sink_ref, h))
            o = jnp.dot(p.astype(BF16), v3, preferred_element_type=F32)
            for g in range(GROUP):
                n = h * GROUP + g
                o_ref[:, n * HEAD_DIM:(n + 1) * HEAD_DIM] = o[g * BLOCK:(g + 1) * BLOCK].astype(BF16)

    prev, cur, meta = (lambda i: (0, i, 0)), (lambda i: (0, i + 1, 0)), (lambda i: (0, 1, 0))
    return _pcall(
        body, name="attn_fwd", grid=(nb,),
        in_specs=[_row_spec(Q_DIM), kv(prev), kv(cur), kv(meta), kv(prev), kv(cur), kv(meta), _const_spec((1, N_Q_HEADS))],
        out_specs=_row_spec(Q_DIM), out_shape=jax.ShapeDtypeStruct((lp, Q_DIM), BF16),
        compiler_params=_params(("parallel",)),
    )(q, k_sh, k_sh, k_sh, v_sh, v_sh, v_sh, sinks)


def _attn_bwd(q, k_sh, v_sh, sinks, do):
    lp = q.shape[0]
    nb = lp // BLOCK
    kv = lambda f: pl.BlockSpec((N_KV_HEADS, BLOCK, HEAD_DIM), f)
    cl = lambda s: jnp.minimum(s, nb - 1)

    def body(q_ref, do_ref, kp_ref, kc_ref, km_ref, vp_ref, vc_ref, vm_ref, sink_ref,
             dq_ref, dk_ref, dv_ref, dkm_ref, dvm_ref, dsink_ref, carry_k, carry_v):
        step = pl.program_id(0)

        @pl.when(step == 0)
        def _():
            carry_k[...] = jnp.zeros_like(carry_k)
            carry_v[...] = jnp.zeros_like(carry_v)
            dkm_ref[...] = jnp.zeros_like(dkm_ref)
            dvm_ref[...] = jnp.zeros_like(dvm_ref)
            dsink_ref[...] = jnp.zeros_like(dsink_ref)

        @pl.when(step < nb)
        def _():
            bias8 = jnp.tile(_attn_bias(step), (GROUP, 1))
            lane = lax.broadcasted_iota(jnp.int32, (1, LANES), 1)
            dsink = jnp.zeros((1, LANES), F32)
            for h in range(N_KV_HEADS):
                k3 = jnp.concatenate([kp_ref[h], kc_ref[h], km_ref[h]], axis=0)
                v3 = jnp.concatenate([vp_ref[h], vc_ref[h], vm_ref[h]], axis=0)
                qs = _stack_heads(q_ref, h)
                dos = _stack_heads(do_ref, h)
                p, psink = _attn_probs(qs, k3, bias8, _sink_column(sink_ref, h))
                dp = lax.dot_general(dos, v3, (((1,), (1,)), ((), ())), preferred_element_type=F32)
                delta = jnp.sum(p * dp, axis=1, keepdims=True)
                ds = (p * (dp - delta)).astype(BF16)
                dsk = -psink * delta
                for g in range(GROUP):
                    val = jnp.sum(dsk[g * BLOCK:(g + 1) * BLOCK], axis=0, keepdims=True)
                    dsink = dsink + jnp.where(lane == h * GROUP + g, val, 0.0)
                dqs = jnp.dot(ds, k3, preferred_element_type=F32)
                for g in range(GROUP):
                    n = h * GROUP + g
                    dq_ref[:, n * HEAD_DIM:(n + 1) * HEAD_DIM] = dqs[g * BLOCK:(g + 1) * BLOCK]
                dk3 = lax.dot_general(ds, qs, (((0,), (0,)), ((), ())), preferred_element_type=F32)
                dv3 = lax.dot_general(p.astype(BF16), dos, (((0,), (0,)), ((), ())), preferred_element_type=F32)
                dk_ref[h] = carry_k[h] + dk3[0:BLOCK]
                dv_ref[h] = carry_v[h] + dv3[0:BLOCK]
                carry_k[h] = dk3[BLOCK:2 * BLOCK]
                carry_v[h] = dv3[BLOCK:2 * BLOCK]
                dkm_ref[h] += dk3[2 * BLOCK:3 * BLOCK]
                dvm_ref[h] += dv3[2 * BLOCK:3 * BLOCK]
            dsink_ref[...] += dsink

        @pl.when(step == nb)
        def _():
            dk_ref[...] = carry_k[...]
            dv_ref[...] = carry_v[...]

    prev, cur, meta = (lambda s: (0, cl(s), 0)), (lambda s: (0, cl(s) + 1, 0)), (lambda s: (0, 1, 0))
    lag = lambda s: (0, jnp.maximum(s - 1, 0), 0)
    head_shape = jax.ShapeDtypeStruct((N_KV_HEADS, lp, HEAD_DIM), F32)
    meta_shape = jax.ShapeDtypeStruct((N_KV_HEADS, BLOCK, HEAD_DIM), F32)
    return _pcall(
        body, name="attn_bwd", grid=(nb + 1,),
        in_specs=[pl.BlockSpec((BLOCK, Q_DIM), lambda s: (cl(s), 0)), pl.BlockSpec((BLOCK, Q_DIM), lambda s: (cl(s), 0)),
                  kv(prev), kv(cur), kv(meta), kv(prev), kv(cur), kv(meta), _const_spec((1, N_Q_HEADS))],
        out_specs=[pl.BlockSpec((BLOCK, Q_DIM), lambda s: (cl(s), 0)), kv(lag), kv(lag),
                   _const_spec((N_KV_HEADS, BLOCK, HEAD_DIM)), _const_spec((N_KV_HEADS, BLOCK, HEAD_DIM)), _const_spec((1, LANES))],
        out_shape=[jax.ShapeDtypeStruct((lp, Q_DIM), F32), head_shape, head_shape, meta_shape, meta_shape,
                   jax.ShapeDtypeStruct((1, LANES), F32)],
        scratch_shapes=[pltpu.VMEM((N_KV_HEADS, BLOCK, HEAD_DIM), F32), pltpu.VMEM((N_KV_HEADS, BLOCK, HEAD_DIM), F32)],
        compiler_params=_params(("arbitrary",)),
    )(q, do, k_sh, k_sh, k_sh, v_sh, v_sh, v_sh, sinks)


CONV_CHUNK = 256


def _glu_masked(a_ref, g_ref, base):
    rows = base + lax.broadcasted_iota(jnp.int32, (BLOCK, 1), 0)
    return jnp.where(rows >= PAD_ROWS, a_ref[...] * _sigmoid(g_ref[...]), 0.0)


def _conv_fwd(zc, conv_w, conv_b, ln_g, ln_b, after=None):
    lp = zc.shape[0]
    cd = zc.shape[1] // 2
    nb = lp // BLOCK
    chunk = min(CONV_CHUNK, cd)
    back = lambda col: (lambda i: (jnp.maximum(i - 1, 0), col))
    lo = BLOCK - (CONV_WIDTH - 1)

    def body(ap_ref, gp_ref, ac_ref, gc_ref, w_ref, b_ref, lg_ref, lb_ref, co_ref, c2_ref, ext):
        i = pl.program_id(0)
        ext[0:BLOCK, :] = _glu_masked(ap_ref, gp_ref, (i - 1) * BLOCK)
        ext[BLOCK:2 * BLOCK, :] = _glu_masked(ac_ref, gc_ref, i * BLOCK)
        for c0 in range(0, cd, chunk):
            acc = jnp.zeros((BLOCK, chunk), F32)
            for k in range(CONV_WIDTH):
                acc = acc + ext[lo + k:lo + k + BLOCK, c0:c0 + chunk] * w_ref[k:k + 1, c0:c0 + chunk]
            co_ref[:, c0:c0 + chunk] = acc + b_ref[:, c0:c0 + chunk]
        x = co_ref[...]
        mu = jnp.mean(x, axis=-1, keepdims=True)
        xc = x - mu
        r = lax.rsqrt(jnp.mean(xc * xc, axis=-1, keepdims=True) + EPS)
        y = xc * r * lg_ref[...] + lb_ref[...]
        c2_ref[...] = (y * _sigmoid(y)).astype(BF16)

    return _pcall(
        body, after=after, name="conv_fwd", grid=(nb,),
        in_specs=[pl.BlockSpec((BLOCK, cd), back(0)), pl.BlockSpec((BLOCK, cd), back(1)), _row_spec(cd, 0), _row_spec(cd, 1),
                  _const_spec((CONV_ROWS, cd)), _const_spec((1, cd)), _const_spec((1, cd)), _const_spec((1, cd))],
        out_specs=[_row_spec(cd), _row_spec(cd)],
        out_shape=[jax.ShapeDtypeStruct((lp, cd), F32), jax.ShapeDtypeStruct((lp, cd), BF16)],
        scratch_shapes=[pltpu.VMEM((2 * BLOCK, cd), F32)],
        compiler_params=_params(("arbitrary",)),
    )(zc, zc, zc, zc, conv_w, conv_b, ln_g, ln_b)


def _conv_bwd_norm(dc2, conv_out, ln_g, ln_b):
    lp, cd = conv_out.shape

    def body(d_ref, x_ref, lg_ref, lb_ref, dco_ref, dlg_ref, dlb_ref, dcb_ref):
        i = pl.program_id(0)
        x = x_ref[...]
        g = lg_ref[...]
        mu = jnp.mean(x, axis=-1, keepdims=True)
        xc = x - mu
        r = lax.rsqrt(jnp.mean(xc * xc, axis=-1, keepdims=True) + EPS)
        xhat = xc * r
        y = xhat * g + lb_ref[...]
        sg = _sigmoid(y)
        dy = d_ref[...] * (sg * (1.0 + y * (1.0 - sg)))
        dxhat = dy * g
        dx = r * (dxhat - jnp.mean(dxhat, axis=-1, keepdims=True) - xhat * jnp.mean(dxhat * xhat, axis=-1, keepdims=True))
        dco_ref[...] = dx

        @pl.when(i == 0)
        def _():
            dlg_ref[...] = jnp.zeros_like(dlg_ref)
            dlb_ref[...] = jnp.zeros_like(dlb_ref)
            dcb_ref[...] = jnp.zeros_like(dcb_ref)

        dlg_ref[...] += jnp.sum(dy * xhat, axis=0, keepdims=True)
        dlb_ref[...] += jnp.sum(dy, axis=0, keepdims=True)
        dcb_ref[...] += jnp.sum(dx, axis=0, keepdims=True)

    vec = jax.ShapeDtypeStruct((1, cd), F32)
    return _pcall(
        body, name="conv_bwd_norm", grid=(lp // BLOCK,),
        in_specs=[_row_spec(cd), _row_spec(cd), _const_spec((1, cd)), _const_spec((1, cd))],
        out_specs=[_row_spec(cd), _const_spec((1, cd)), _const_spec((1, cd)), _const_spec((1, cd))],
        out_shape=[jax.ShapeDtypeStruct((lp, cd), F32), vec, vec, vec],
        compiler_params=_params(("arbitrary",)),
    )(dc2, conv_out, ln_g, ln_b)


def _conv_bwd_taps(dco, zc, conv_w):
    lp, cd = dco.shape
    nb = lp // BLOCK
    chunk = min(CONV_CHUNK, cd)
    back = lambda col: (lambda i: (jnp.maximum(i - 1, 0), col))
    fwd = lambda i: (jnp.minimum(i + 1, nb - 1), 0)
    lo = BLOCK - (CONV_WIDTH - 1)

    def body(dc_ref, dn_ref, ap_ref, gp_ref, ac_ref, gc_ref, w_ref, dz_ref, sum_ref, dw_ref, ext, dext, dcb):
        i = pl.program_id(0)
        ext[0:BLOCK, :] = _glu_masked(ap_ref, gp_ref, (i - 1) * BLOCK)
        ext[BLOCK:2 * BLOCK, :] = _glu_masked(ac_ref, gc_ref, i * BLOCK)
        dext[0:BLOCK, :] = dc_ref[...]
        dext[BLOCK:2 * BLOCK, :] = dn_ref[...] * (i < nb - 1).astype(F32)

        @pl.when(i == 0)
        def _():
            dw_ref[...] = jnp.zeros_like(dw_ref)
            sum_ref[...] = jnp.zeros_like(sum_ref)

        for c0 in range(0, cd, chunk):
            cols = slice(c0, c0 + chunk)
            dcur = dext[0:BLOCK, cols]
            acc = jnp.zeros((BLOCK, chunk), F32)
            for k in range(CONV_WIDTH):
                s = CONV_WIDTH - 1 - k
                acc = acc + dext[s:s + BLOCK, cols] * w_ref[k:k + 1, cols]
                dw_ref[k:k + 1, cols] += jnp.sum(dcur * ext[lo + k:lo + k + BLOCK, cols], axis=0, keepdims=True)
            dcb[:, cols] = acc
        rows = i * BLOCK + lax.broadcasted_iota(jnp.int32, (BLOCK, 1), 0)
        dc = jnp.where(rows >= PAD_ROWS, dcb[...], 0.0)
        a = ac_ref[...]
        sg = _sigmoid(gc_ref[...])
        da = dc * sg
        dg = dc * a * sg * (1.0 - sg)
        dz_ref[:, 0:cd] = da.astype(BF16)
        dz_ref[:, cd:2 * cd] = dg.astype(BF16)
        sum_ref[:, 0:cd] += jnp.sum(da, axis=0, keepdims=True)
        sum_ref[:, cd:2 * cd] += jnp.sum(dg, axis=0, keepdims=True)

    return _pcall(
        body, name="conv_bwd_taps", grid=(nb,),
        in_specs=[_row_spec(cd), pl.BlockSpec((BLOCK, cd), fwd),
                  pl.BlockSpec((BLOCK, cd), back(0)), pl.BlockSpec((BLOCK, cd), back(1)), _row_spec(cd, 0), _row_spec(cd, 1),
                  _const_spec((CONV_ROWS, cd))],
        out_specs=[_row_spec(2 * cd), _const_spec((1, 2 * cd)), _const_spec((CONV_ROWS, cd))],
        out_shape=[jax.ShapeDtypeStruct((lp, 2 * cd), BF16), jax.ShapeDtypeStruct((1, 2 * cd), F32),
                   jax.ShapeDtypeStruct((CONV_ROWS, cd), F32)],
        scratch_shapes=[pltpu.VMEM((2 * BLOCK, cd), F32), pltpu.VMEM((2 * BLOCK, cd), F32), pltpu.VMEM((BLOCK, cd), F32)],
        compiler_params=_params(("arbitrary",)),
    )(dco, dco, zc, zc, zc, zc, conv_w)


def _gate_fwd(a, b, zg, after=None):
    lp, d = a.shape

    def body(a_ref, b_ref, ga_ref, gb_ref, m_ref):
        m_ref[...] = (_sigmoid(ga_ref[...]) * a_ref[...] + _sigmoid(gb_ref[...]) * b_ref[...]).astype(BF16)

    return _pcall(
        body, after=after, name="gate_fwd", grid=(lp // BLOCK,),
        in_specs=[_row_spec(d), _row_spec(d), _row_spec(d, 0), _row_spec(d, 1)], out_specs=_row_spec(d),
        out_shape=jax.ShapeDtypeStruct((lp, d), BF16), compiler_params=_params(("parallel",)),
    )(a, b, zg, zg)


def _gate_bwd(dm, a, b, zg):
    lp, d = a.shape

    def body(dm_ref, a_ref, b_ref, ga_ref, gb_ref, da_ref, db_ref, dz_ref, sum_ref, dbias_ref):
        i = pl.program_id(0)
        dm_ = dm_ref[...]
        sa = _sigmoid(ga_ref[...])
        sb = _sigmoid(gb_ref[...])
        db = dm_ * sb
        dga = dm_ * a_ref[...] * sa * (1.0 - sa)
        dgb = dm_ * b_ref[...] * sb * (1.0 - sb)
        da_ref[...] = (dm_ * sa).astype(BF16)
        db_ref[...] = db.astype(BF16)
        dz_ref[:, 0:d] = dga.astype(BF16)
        dz_ref[:, d:2 * d] = dgb.astype(BF16)

        @pl.when(i == 0)
        def _():
            sum_ref[...] = jnp.zeros_like(sum_ref)
            dbias_ref[...] = jnp.zeros_like(dbias_ref)

        sum_ref[:, 0:d] += jnp.sum(dga, axis=0, keepdims=True)
        sum_ref[:, d:2 * d] += jnp.sum(dgb, axis=0, keepdims=True)
        dbias_ref[...] += jnp.sum(db, axis=0, keepdims=True)

    return _pcall(
        body, name="gate_bwd", grid=(lp // BLOCK,),
        in_specs=[_row_spec(d), _row_spec(d), _row_spec(d), _row_spec(d, 0), _row_spec(d, 1)],
        out_specs=[_row_spec(d), _row_spec(d), _row_spec(2 * d), _const_spec((1, 2 * d)), _const_spec((1, d))],
        out_shape=[jax.ShapeDtypeStruct((lp, d), BF16), jax.ShapeDtypeStruct((lp, d), BF16), jax.ShapeDtypeStruct((lp, 2 * d), BF16),
                   jax.ShapeDtypeStruct((1, 2 * d), F32), jax.ShapeDtypeStruct((1, d), F32)],
        compiler_params=_params(("arbitrary",)),
    )(dm, a, b, zg, zg)


def _swiglu_fwd(gu, after=None):
    lp = gu.shape[0]
    f = gu.shape[1] // 2

    def body(g_ref, u_ref, o_ref):
        g = g_ref[...]
        o_ref[...] = (g * _sigmoid(g) * u_ref[...]).astype(BF16)

    return _pcall(
        body, after=after, name="swiglu_fwd", grid=(lp // BLOCK,), in_specs=[_row_spec(f, 0), _row_spec(f, 1)], out_specs=_row_spec(f),
        out_shape=jax.ShapeDtypeStruct((lp, f), BF16), compiler_params=_params(("parallel",)),
    )(gu, gu)


def _swiglu_bwd(dact, gu):
    lp, f = dact.shape

    def body(d_ref, g_ref, u_ref, o_ref):
        g = g_ref[...]
        d = d_ref[...]
        sg = _sigmoid(g)
        o_ref[:, 0:f] = (d * u_ref[...] * (sg * (1.0 + g * (1.0 - sg)))).astype(BF16)
        o_ref[:, f:2 * f] = (d * g * sg).astype(BF16)

    return _pcall(
        body, name="swiglu_bwd", grid=(lp // BLOCK,), in_specs=[_row_spec(f), _row_spec(f, 0), _row_spec(f, 1)],
        out_specs=_row_spec(2 * f), out_shape=jax.ShapeDtypeStruct((lp, 2 * f), BF16), compiler_params=_params(("parallel",)),
    )(dact, gu, gu)


ANY = pl.BlockSpec(memory_space=pl.ANY)


def _all_gather_rows(x, name, after=None):
    r, c = x.shape

    def body(x_ref, out_ref, send_sems, recv_sems, local_sem):
        mx, my, mc = lax.axis_index("x"), lax.axis_index("y"), lax.axis_index("c")
        me, sibling = (mx, my, mc), (mx, my, 1 - mc)
        chips = [(1 - mx, my), (mx, 1 - my), (1 - mx, 1 - my)]

        def rows(px, py, pc):
            return out_ref.at[pl.ds((4 * px + 2 * py + pc) * r, r), :]

        def copy(k, block, to, src=None):
            return pltpu.make_async_remote_copy(
                src_ref=rows(*block) if src is None else src, dst_ref=rows(*block),
                send_sem=send_sems.at[k], recv_sem=recv_sems.at[k], device_id=to, device_id_type=MESH)

        mine = pltpu.make_async_copy(x_ref, rows(*me), local_sem)
        mine.start()
        first = [copy(0, me, sibling, src=x_ref)]
        first += [copy(1 + j, me, (*chip, mc), src=x_ref) for j, chip in enumerate(chips)]
        for cp in first:
            cp.start()
        passed = [copy(4 + j, (*chip, mc), sibling) for j, chip in enumerate(chips)]
        for j, chip in enumerate(chips):
            copy(1 + j, (*chip, mc), me).wait_recv()
            passed[j].start()
        copy(0, sibling, me).wait_recv()
        for j, chip in enumerate(chips):
            copy(4 + j, (*chip, 1 - mc), me).wait_recv()
        for cp in first + passed:
            cp.wait_send()
        mine.wait()

    return _pcall(
        body, after=after, name=name, in_specs=[ANY], out_specs=ANY, out_shape=jax.ShapeDtypeStruct((N_DEV * r, c), x.dtype),
        scratch_shapes=[pltpu.SemaphoreType.DMA((7,)), pltpu.SemaphoreType.DMA((7,)), pltpu.SemaphoreType.DMA(())],
    )(x)


HBM = pl.BlockSpec(memory_space=pltpu.HBM)
SEM = pl.BlockSpec(memory_space=pltpu.SEMAPHORE)
IN_FLIGHT = pltpu.CompilerParams(has_side_effects=pltpu.SideEffectType.DATAFLOW_SIDE_EFFECTING)
N_PEERS = 4


def _place_rows(shard, name):
    r, c = shard.shape
    tr = _pick(r, max(16, ELEMENTWISE_BLOCK_BYTES // (4 * c)), 16)
    steps = r // tr
    dev = (4 * lax.axis_index("x") + 2 * lax.axis_index("y") + lax.axis_index("c")).astype(jnp.int32).reshape(1)

    def body(dev_ref, x_ref, o_ref):
        o_ref[...] = x_ref[...].astype(BF16)

    return _pcall(
        body, name=name,
        grid_spec=pltpu.PrefetchScalarGridSpec(
            num_scalar_prefetch=1, grid=(steps,),
            in_specs=[pl.BlockSpec((tr, c), lambda i, dev_ref: (i, 0))],
            out_specs=pl.BlockSpec((tr, c), lambda i, dev_ref: (dev_ref[0] * steps + i, 0))),
        out_shape=jax.ShapeDtypeStruct((N_DEV * r, c), BF16), compiler_params=_params(("parallel",)),
    )(dev, shard)


def _gather_start(fulls, after):
    n = len(fulls)

    def body(*refs):
        full_refs = refs[:n]
        send_sems, recv_sems = refs[n + 1:2 * n + 1], refs[2 * n + 1:3 * n + 1]
        mx, my, mc = lax.axis_index("x"), lax.axis_index("y"), lax.axis_index("c")
        peers = [(mx, my, 1 - mc), (1 - mx, my, mc), (mx, 1 - my, mc), (1 - mx, 1 - my, mc)]
        for w in range(n):
            r = full_refs[w].shape[0] // N_DEV
            mine = full_refs[w].at[pl.ds((4 * mx + 2 * my + mc) * r, r), :]
            for k, peer in enumerate(peers):
                pltpu.make_async_remote_copy(
                    src_ref=mine, dst_ref=mine, send_sem=send_sems[w].at[k], recv_sem=recv_sems[w].at[k],
                    device_id=peer, device_id_type=MESH).start()

    sems = tuple(pltpu.SemaphoreType.DMA((N_PEERS,)) for _ in range(2 * n))
    outs = pl.pallas_call(
        body, name="gather_start", in_specs=[HBM] * (n + 1),
        out_specs=(SEM,) * (2 * n) + (HBM,) * n,
        out_shape=sems + tuple(pltpu.HBM(f.shape, f.dtype) for f in fulls),
        input_output_aliases={w: 2 * n + w for w in range(n)}, compiler_params=IN_FLIGHT,
    )(*[pltpu.with_memory_space_constraint(f, pltpu.HBM) for f in fulls], after)
    return outs[:n], outs[n:2 * n], outs[2 * n:]


def _gather_wait(full, send_sem, recv_sem, after, name):
    r = full.shape[0] // N_DEV

    def body(full_ref, send_ref, recv_ref, after_ref, out_ref):
        mx, my, mc = lax.axis_index("x"), lax.axis_index("y"), lax.axis_index("c")
        block = full_ref.at[pl.ds(0, r), :]
        for k in range(N_PEERS):
            cp = pltpu.make_async_remote_copy(
                src_ref=block, dst_ref=block, send_sem=send_ref.at[k], recv_sem=recv_ref.at[k],
                device_id=(mx, my, mc), device_id_type=MESH)
            cp.wait_send()
            cp.wait_recv()

    return pl.pallas_call(
        body, name=name, in_specs=[HBM, SEM, SEM, pl.BlockSpec(memory_space=pl.ANY)], out_specs=HBM,
        out_shape=pltpu.HBM(full.shape, full.dtype), input_output_aliases={0: 0}, compiler_params=IN_FLIGHT,
    )(full, send_sem, recv_sem, after)


def _gather_forward_start(full, name):
    r = full.shape[0] // N_DEV

    def body(full_ref, send_sems, recv_sems, out_ref):
        mx, my, mc = lax.axis_index("x"), lax.axis_index("y"), lax.axis_index("c")
        for k, (px, py) in enumerate([(1 - mx, my), (mx, 1 - my), (1 - mx, 1 - my)]):
            rows = full_ref.at[pl.ds((4 * px + 2 * py + mc) * r, r), :]
            pltpu.make_async_remote_copy(
                src_ref=rows, dst_ref=rows, send_sem=send_sems.at[k], recv_sem=recv_sems.at[k],
                device_id=(mx, my, 1 - mc), device_id_type=MESH).start()

    return pl.pallas_call(
        body, name=name, in_specs=[HBM], out_specs=(SEM, SEM, HBM),
        out_shape=(pltpu.SemaphoreType.DMA((3,)), pltpu.SemaphoreType.DMA((3,)), pltpu.HBM(full.shape, full.dtype)),
        input_output_aliases={0: 2}, compiler_params=IN_FLIGHT,
    )(full)


def _gather_forward_wait(full, send_sem, recv_sem, after, name):
    r = full.shape[0] // N_DEV

    def body(full_ref, send_ref, recv_ref, after_ref, out_ref):
        mx, my, mc = lax.axis_index("x"), lax.axis_index("y"), lax.axis_index("c")
        block = full_ref.at[pl.ds(0, r), :]
        for k in range(3):
            cp = pltpu.make_async_remote_copy(
                src_ref=block, dst_ref=block, send_sem=send_ref.at[k], recv_sem=recv_ref.at[k],
                device_id=(mx, my, mc), device_id_type=MESH)
            cp.wait_send()
            cp.wait_recv()

    return pl.pallas_call(
        body, name=name, in_specs=[HBM, SEM, SEM, pl.BlockSpec(memory_space=pl.ANY)], out_specs=HBM,
        out_shape=pltpu.HBM(full.shape, full.dtype), input_output_aliases={0: 0}, compiler_params=IN_FLIGHT,
    )(full, send_sem, recv_sem, after)


def _pair_exchange_start(g, name):
    r = g.shape[0] // N_DEV
    c = g.shape[1]
    land = (len(CHIPS), r, c)

    def body(g_ref, land_ref, send_sems, recv_sems, g_out, land_out):
        mx, my, mc = lax.axis_index("x"), lax.axis_index("y"), lax.axis_index("c")
        for j, (px, py) in enumerate(CHIPS):
            pltpu.make_async_remote_copy(
                src_ref=g_ref.at[pl.ds((4 * px + 2 * py + 1 - mc) * r, r), :], dst_ref=land_ref.at[j],
                send_sem=send_sems.at[j], recv_sem=recv_sems.at[j], device_id=(mx, my, 1 - mc), device_id_type=MESH).start()

    return pl.pallas_call(
        body, name=name, in_specs=[HBM, HBM], out_specs=(SEM, SEM, HBM, HBM),
        out_shape=(pltpu.SemaphoreType.DMA((4,)), pltpu.SemaphoreType.DMA((4,)), pltpu.HBM(g.shape, g.dtype), pltpu.HBM(land, g.dtype)),
        input_output_aliases={0: 2, 1: 3}, compiler_params=IN_FLIGHT,
    )(pltpu.with_memory_space_constraint(g, pltpu.HBM), pltpu.with_memory_space_constraint(lax.empty(land, g.dtype), pltpu.HBM))


def _pair_exchange_wait(send_sem, recv_sem, g, land, after, name):
    def body(g_ref, land_ref, send_ref, recv_ref, after_ref, g_out, land_out):
        mx, my, mc = lax.axis_index("x"), lax.axis_index("y"), lax.axis_index("c")
        for j in range(len(CHIPS)):
            cp = pltpu.make_async_remote_copy(
                src_ref=land_ref.at[0], dst_ref=land_ref.at[0], send_sem=send_ref.at[j], recv_sem=recv_ref.at[j],
                device_id=(mx, my, mc), device_id_type=MESH)
            cp.wait_send()
            cp.wait_recv()

    return pl.pallas_call(
        body, name=name, in_specs=[HBM, HBM, SEM, SEM, pl.BlockSpec(memory_space=pl.ANY)], out_specs=(HBM, HBM),
        out_shape=(pltpu.HBM(g.shape, g.dtype), pltpu.HBM(land.shape, land.dtype)), input_output_aliases={0: 0, 1: 1},
        compiler_params=IN_FLIGHT,
    )(g, land, send_sem, recv_sem, after)


def _chip_exchange_start(ps, after, name):
    def body(ps_ref, rx_ref, after_ref, send_sems, recv_sems, ps_out, rx_out):
        mx, my, mc = lax.axis_index("x"), lax.axis_index("y"), lax.axis_index("c")
        chips = [(1 - mx, my), (mx, 1 - my), (1 - mx, 1 - my)]
        for k, (px, py) in enumerate(chips):
            pltpu.make_async_remote_copy(
                src_ref=ps_ref.at[2 * px + py], dst_ref=rx_ref.at[2 * mx + my], send_sem=send_sems.at[k], recv_sem=recv_sems.at[k],
                device_id=(px, py, mc), device_id_type=MESH).start()

    return pl.pallas_call(
        body, name=name, in_specs=[HBM, HBM, pl.BlockSpec(memory_space=pl.ANY)], out_specs=(SEM, SEM, HBM, HBM),
        out_shape=(pltpu.SemaphoreType.DMA((3,)), pltpu.SemaphoreType.DMA((3,)), pltpu.HBM(ps.shape, ps.dtype), pltpu.HBM(ps.shape, ps.dtype)),
        input_output_aliases={0: 2, 1: 3}, compiler_params=IN_FLIGHT,
    )(pltpu.with_memory_space_constraint(ps, pltpu.HBM), pltpu.with_memory_space_constraint(lax.empty(ps.shape, ps.dtype), pltpu.HBM), after)


def _chip_exchange_wait(send_sem, recv_sem, ps, rx, after, name):
    def body(ps_ref, rx_ref, send_ref, recv_ref, after_ref, ps_out, rx_out):
        mx, my, mc = lax.axis_index("x"), lax.axis_index("y"), lax.axis_index("c")
        for k in range(3):
            cp = pltpu.make_async_remote_copy(
                src_ref=ps_ref.at[0], dst_ref=rx_ref.at[0], send_sem=send_ref.at[k], recv_sem=recv_ref.at[k],
                device_id=(mx, my, mc), device_id_type=MESH)
            cp.wait_send()
            cp.wait_recv()

    return pl.pallas_call(
        body, name=name, in_specs=[HBM, HBM, SEM, SEM, pl.BlockSpec(memory_space=pl.ANY)], out_specs=(HBM, HBM),
        out_shape=(pltpu.HBM(ps.shape, ps.dtype), pltpu.HBM(rx.shape, rx.dtype)), input_output_aliases={0: 0, 1: 1},
        compiler_params=IN_FLIGHT,
    )(ps, rx, send_sem, recv_sem, after)


def _sum_chips(ps, rx, name):
    n, r, c = rx.shape
    tr = _pick(r, max(8, ELEMENTWISE_BLOCK_BYTES // (4 * n * c)), 8)
    chip = (2 * lax.axis_index("x") + lax.axis_index("y")).astype(jnp.int32).reshape(1)

    def body(chip_ref, own_ref, x_ref, o_ref):
        me = chip_ref[0]
        own = own_ref[0].astype(F32)
        acc = jnp.where(me == 0, own, x_ref[0].astype(F32))
        for j in range(1, n):
            acc = acc + jnp.where(me == j, own, x_ref[j].astype(F32))
        o_ref[...] = acc

    return _pcall(
        body, name=name,
        grid_spec=pltpu.PrefetchScalarGridSpec(
            num_scalar_prefetch=1, grid=(r // tr,),
            in_specs=[pl.BlockSpec((1, tr, c), lambda i, chip_ref: (chip_ref[0], i, 0)), pl.BlockSpec((n, tr, c), lambda i, chip_ref: (0, i, 0))],
            out_specs=pl.BlockSpec((tr, c), lambda i, chip_ref: (i, 0))),
        out_shape=jax.ShapeDtypeStruct((r, c), F32), compiler_params=_params(("parallel",)),
    )(chip, ps, rx)


def _pair_exchange(g, name):
    r = g.shape[0] // N_DEV
    c = g.shape[1]

    def body(g_ref, theirs_ref, send_sems, recv_sems):
        mx, my, mc = lax.axis_index("x"), lax.axis_index("y"), lax.axis_index("c")
        sibling = (mx, my, 1 - mc)
        copies = []
        for j, (px, py) in enumerate(CHIPS):
            give = g_ref.at[pl.ds((4 * px + 2 * py + 1 - mc) * r, r), :]
            rc = pltpu.make_async_remote_copy(
                src_ref=give, dst_ref=theirs_ref.at[j], send_sem=send_sems.at[j], recv_sem=recv_sems.at[j],
                device_id=sibling, device_id_type=MESH)
            rc.start()
            copies.append(rc)
        for cp in copies:
            cp.wait()

    return _pcall(
        body, name=name, in_specs=[ANY], out_specs=ANY, out_shape=jax.ShapeDtypeStruct((len(CHIPS), r, c), g.dtype),
        scratch_shapes=[pltpu.SemaphoreType.DMA((4,)), pltpu.SemaphoreType.DMA((4,))],
    )(g)


def _pair_sum(g, theirs, name):
    nch, r, c = theirs.shape
    tr = _pick(r, max(16, ELEMENTWISE_BLOCK_BYTES // (2 * c)), 16)
    core = lax.axis_index("c").astype(jnp.int32).reshape(1)

    def body(core_ref, a_ref, b_ref, o_ref):
        o_ref[...] = (a_ref[...].astype(F32) + b_ref[...].astype(F32)).astype(o_ref.dtype)

    spec = pl.BlockSpec((1, tr, c), lambda j, i, core_ref: (j, i, 0))
    own = pl.BlockSpec((1, tr, c), lambda j, i, core_ref: (2 * j + core_ref[0], i, 0))
    return _pcall(
        body, name=name,
        grid_spec=pltpu.PrefetchScalarGridSpec(num_scalar_prefetch=1, grid=(nch, r // tr), in_specs=[own, spec], out_specs=spec),
        out_shape=jax.ShapeDtypeStruct(theirs.shape, theirs.dtype), compiler_params=_params(("parallel", "parallel")),
    )(core, g.reshape(N_DEV, r, c), theirs)


def _sum_blocks(rx, name):
    n, r, c = rx.shape
    tr = _pick(r, max(8, ELEMENTWISE_BLOCK_BYTES // (4 * n * c)), 8)

    def body(x_ref, o_ref):
        acc = x_ref[0].astype(F32)
        for j in range(1, n):
            acc = acc + x_ref[j].astype(F32)
        o_ref[...] = acc

    return _pcall(
        body, name=name, grid=(r // tr,), in_specs=[pl.BlockSpec((n, tr, c), lambda i: (0, i, 0))],
        out_specs=pl.BlockSpec((tr, c), lambda i: (i, 0)), out_shape=jax.ShapeDtypeStruct((r, c), F32),
        compiler_params=_params(("parallel",)),
    )(rx)


def _adamw(w, g, m, v, name):
    r, c = w.shape
    tr = _pick(r, max(8, ELEMENTWISE_BLOCK_BYTES // (4 * c)), 8)
    c1 = 1.0 - ADAM_B1 ** ADAM_STEP
    c2 = 1.0 - ADAM_B2 ** ADAM_STEP

    def body(w_ref, g_ref, m_ref, v_ref, d_ref, nm_ref, nv_ref):
        gg = g_ref[...]
        nm = ADAM_B1 * m_ref[...] + (1.0 - ADAM_B1) * gg
        nv = ADAM_B2 * v_ref[...] + (1.0 - ADAM_B2) * (gg * gg)
        d_ref[...] = -ADAM_LR * ((nm / c1) / (jnp.sqrt(nv / c2) + ADAM_EPS) + ADAM_WD * w_ref[...])
        nm_ref[...] = nm
        nv_ref[...] = nv

    spec = pl.BlockSpec((tr, c), lambda i: (i, 0))
    shp = jax.ShapeDtypeStruct((r, c), F32)
    return _pcall(
        body, name=name, grid=(r // tr,), in_specs=[spec] * 4, out_specs=[spec] * 3, out_shape=[shp] * 3,
        compiler_params=_params(("parallel",)),
    )(w, g, m, v)


def _pack(parts):
    flat, layout, row = [], [], 0
    for p in parts:
        n = p.size
        rows = -(-n // LANES)
        flat.append(jnp.pad(p.reshape(-1).astype(F32), (0, rows * LANES - n)))
        layout.append((row, n, p.shape))
        row += rows
    total = -(-row // 8) * 8
    if total > row:
        flat.append(jnp.zeros(((total - row) * LANES,), F32))
    return jnp.concatenate(flat).reshape(total, LANES), layout


def _unpack(slab, layout):
    flat = slab.reshape(-1)
    return [flat[row * LANES:row * LANES + n].reshape(shape) for row, n, shape in layout]


def kernel(x, meta_tokens, mix_norm_g, w_in, b_in, attn_sinks, conv_w, conv_b, conv_ln_g, conv_ln_b, w_attn_o, w_conv_o, b_conv_o, w_out, ffn_norm_g, w_gate_up, w_down, final_norm_g, loss_target, m_meta_tokens, m_mix_norm_g, m_w_in, m_b_in, m_attn_sinks, m_conv_w, m_conv_b, m_conv_ln_g, m_conv_ln_b, m_w_attn_o, m_w_conv_o, m_b_conv_o, m_w_out, m_ffn_norm_g, m_w_gate_up, m_w_down, m_final_norm_g, v_meta_tokens, v_mix_norm_g, v_w_in, v_b_in, v_attn_sinks, v_conv_w, v_conv_b, v_conv_ln_g, v_conv_ln_b, v_w_attn_o, v_w_conv_o, v_b_conv_o, v_w_out, v_ffn_norm_g, v_w_gate_up, v_w_down, v_final_norm_g):
    xs = x[0]
    tgt = loss_target[0]
    s, d = xs.shape
    lp = s + BLOCK
    cd = conv_b.shape[1]
    ffn = w_down.shape[1] * N_DEV
    dev = 4 * lax.axis_index("x") + 2 * lax.axis_index("y") + lax.axis_index("c")
    cw_cols = conv_w.shape[3]
    meta_cols = meta_tokens.shape[1]

    small, small_layout = _pack([meta_tokens, jnp.pad(conv_w[0, :, 0, :], ((0, CONV_ROWS - CONV_WIDTH), (0, 0)))])
    small_flat = _all_gather_rows(small, "gather_small")
    small_all = small_flat.reshape(N_DEV, *small.shape)
    meta_parts, cw_parts = zip(*[_unpack(small_all[j], small_layout) for j in range(N_DEV)])
    meta_full = jnp.concatenate(meta_parts, axis=1)
    conv_w_full = jnp.concatenate(cw_parts, axis=1)
    placed = [_place_rows(w_in[0].T, "place_w_in"), _place_rows(w_attn_o[0].T, "place_w_attn_o"),
              _place_rows(w_conv_o[0].T, "place_w_conv_o"), _place_rows(w_out[0], "place_w_out"),
              _place_rows(w_gate_up[0].T, "place_w_gate_up"), _place_rows(w_down[0], "place_w_down")]
    g_send, g_recv, g_full = _gather_start(placed, small_flat)

    def arrived(w, after, name):
        full = _gather_wait(g_full[w], g_send[w], g_recv[w], after, "gather_wait_" + name)
        return _gather_forward_start(full, "gather_forward_start_" + name)

    def whole(passing, after, name):
        return _gather_forward_wait(passing[2], passing[0], passing[1], after, "gather_forward_wait_" + name)

    ctab, stab = _rope_tables(lp)
    mm = functools.partial(_matmul, tm=1056, tn=1024)

    passing = arrived(0, small_flat, "w_in")
    h0, u = _prep(xs, meta_full, mix_norm_g, after=passing[2])
    win_t = whole(passing, u, "w_in")
    bq, bkv, bc, bg = b_in[:, :Q_DIM], b_in[:, Q_DIM:Q_DIM + 2 * KV_DIM], b_in[:, Q_DIM + 2 * KV_DIM:Q_DIM + 2 * KV_DIM + 2 * cd], b_in[:, Q_DIM + 2 * KV_DIM + 2 * cd:]
    o_kv, o_c, o_g = Q_DIM, Q_DIM + 2 * KV_DIM, Q_DIM + 2 * KV_DIM + 2 * cd
    zq = mm(u, win_t, mode="nt", name="in_proj_q", out_dtype=F32, tk=d, bias=bq, b_row_off=0, b_rows=Q_DIM)
    zkv = mm(u, win_t, mode="nt", name="in_proj_kv", out_dtype=F32, tk=d, bias=bkv, b_row_off=o_kv, b_rows=2 * KV_DIM)
    zc = mm(u, win_t, mode="nt", name="in_proj_conv", out_dtype=F32, tk=d, bias=bc, b_row_off=o_c, b_rows=2 * cd)
    zg = mm(u, win_t, mode="nt", name="in_proj_gates", out_dtype=F32, tk=d, bias=bg, b_row_off=o_g, b_rows=2 * d)
    passing = arrived(1, zg, "w_attn_o")
    q_rot, k_sh, v_sh = _rope_fwd(zq, zkv, ctab, stab, after=passing[2])
    o = _attn_fwd(q_rot, k_sh, v_sh, attn_sinks)
    wao_t = whole(passing, o, "w_attn_o")
    br_a = mm(o, wao_t, mode="nt", name="attn_out_proj", out_dtype=F32, tk=Q_DIM)
    passing = arrived(2, br_a, "w_conv_o")
    conv_out, c2 = _conv_fwd(zc, conv_w_full, conv_b, conv_ln_g, conv_ln_b, after=passing[2])
    wco_t = whole(passing, c2, "w_conv_o")
    br_b = mm(c2, wco_t, mode="nt", name="conv_out_proj", out_dtype=F32, tk=cd, bias=b_conv_o)
    passing = arrived(3, br_b, "w_out")
    merged = _gate_fwd(br_a, br_b, zg, after=passing[2])
    wout = whole(passing, merged, "w_out")
    passing = arrived(4, wout, "w_gate_up")
    h1 = mm(merged, wout, mode="nn", name="mix_out_proj", out_dtype=F32, tn=512, tk=d, residual=h0, after=passing[2])
    u2 = _rmsnorm_fwd(h1, ffn_norm_g, "ffn_rmsnorm")
    wgu_t = whole(passing, u2, "w_gate_up")
    gu = _matmul(u2, wgu_t, mode="nt", name="ffn_gate_up", out_dtype=F32, tm=1056, tn=512, tk=d)
    passing = arrived(5, gu, "w_down")
    act = _swiglu_fwd(gu, after=passing[2])
    wdown = whole(passing, act, "w_down")
    h2 = mm(act, wdown, mode="nn", name="ffn_down", out_dtype=F32, tn=512, tk=ffn // 2, residual=h1)
    dh2, dh2_b, loss_part, d_final_g = _final(h2, tgt, final_norm_g.reshape(1, d))

    wgrad = functools.partial(_matmul, mode="tn", out_dtype=BF16, tk=lp, tn=2048, b_inner=False)
    in_flight = {}

    def scatter_begin(g, name):
        return _pair_exchange_start(g, "rs_" + name + "_pair_start")

    def scatter_go_on(pair, after, name):
        g, theirs = _pair_exchange_wait(pair[0], pair[1], pair[2], pair[3], after, "rs_" + name + "_pair_wait")
        ps = _pair_sum(g, theirs, "rs_" + name + "_pair_sum")
        in_flight[name] = _chip_exchange_start(ps, theirs, "rs_" + name + "_chip_start")
        return in_flight[name][2]

    g_wdown = wgrad(act, dh2_b, name="ffn_down_dw", tm=256)
    pair = scatter_begin(g_wdown, "w_down")
    dact = _matmul(dh2_b, wdown, mode="nt", name="ffn_down_dx", out_dtype=F32, tm=2112, tn=256, tk=d, after=pair[2])
    scatter_go_on(pair, dact, "w_down")
    dgu = _swiglu_bwd(dact, gu)
    g_wgu_t = wgrad(dgu, u2, name="ffn_gate_up_dw", tm=512)
    pair = scatter_begin(g_wgu_t, "w_gate_up")
    du2 = mm(dgu, wgu_t, mode="nn", name="ffn_gate_up_dx", out_dtype=F32, tn=512, tk=ffn // 2, after=pair[2])
    scatter_go_on(pair, du2, "w_gate_up")
    dh1, dh1_b, d_ffn_g = _rmsnorm_bwd(du2, h1, ffn_norm_g, dh2, "ffn_rmsnorm_bwd")
    g_wout = wgrad(merged, dh1_b, name="mix_out_dw", tm=512)
    pair = scatter_begin(g_wout, "w_out")
    dmerged = mm(dh1_b, wout, mode="nt", name="mix_out_dx", out_dtype=F32, tk=d, after=pair[2])
    scatter_go_on(pair, dmerged, "w_out")
    d_a, d_b, dz_g, sum_g, d_bco = _gate_bwd(dmerged, br_a, br_b, zg)
    g_wao_t = wgrad(d_a, o, name="attn_out_dw", tm=512)
    pair = scatter_begin(g_wao_t, "w_attn_o")
    do = mm(d_a, wao_t, mode="nn", name="attn_out_dx", out_dtype=BF16, tk=d, after=pair[2])
    scatter_go_on(pair, do, "w_attn_o")
    g_wco_t = wgrad(d_b, c2, name="conv_out_dw", tm=512)
    pair = scatter_begin(g_wco_t, "w_conv_o")
    dc2 = mm(d_b, wco_t, mode="nn", name="conv_out_dx", out_dtype=F32, tk=d, after=pair[2])
    scatter_go_on(pair, dc2, "w_conv_o")
    dq, dk, dv, dkm, dvm, d_sinks = _attn_bwd(q_rot, k_sh, v_sh, attn_sinks, do)
    dz_qkv, sum_qkv = _rope_bwd(dq, dk, dv, dkm, dvm, ctab, stab)
    dco, d_ln_g, d_ln_b, d_conv_b = _conv_bwd_norm(dc2, conv_out, conv_ln_g, conv_ln_b)
    dz_c, sum_c, d_conv_w = _conv_bwd_taps(dco, zc, conv_w_full)
    dz = jnp.concatenate([dz_qkv, dz_c, dz_g], axis=1)
    d_b_in = jnp.concatenate([sum_qkv, sum_c, sum_g], axis=1)
    in_dim = dz.shape[1]
    g_win_t = wgrad(dz, u, name="in_proj_dw", tm=512)
    theirs = _pair_exchange(g_win_t, "rs_w_in_pair_exchange")
    in_flight["w_in"] = _chip_exchange_start(_pair_sum(g_win_t, theirs, "rs_w_in_pair_sum"), theirs, "rs_w_in_chip_start")
    du = mm(dz, win_t, mode="nn", name="in_proj_dx", out_dtype=F32, tk=in_dim // 4, after=in_flight["w_in"][2])
    grad_x, d_meta, d_mix_g = _rmsnorm_bwd_first(du, h0, mix_norm_g, dh1)

    weights = dict(meta_tokens=meta_tokens, mix_norm_g=mix_norm_g, w_in=w_in, b_in=b_in, attn_sinks=attn_sinks, conv_w=conv_w,
                   conv_b=conv_b, conv_ln_g=conv_ln_g, conv_ln_b=conv_ln_b, w_attn_o=w_attn_o, w_conv_o=w_conv_o, b_conv_o=b_conv_o,
                   w_out=w_out, ffn_norm_g=ffn_norm_g, w_gate_up=w_gate_up, w_down=w_down, final_norm_g=final_norm_g)
    m_in = dict(meta_tokens=m_meta_tokens, mix_norm_g=m_mix_norm_g, w_in=m_w_in, b_in=m_b_in, attn_sinks=m_attn_sinks, conv_w=m_conv_w,
                conv_b=m_conv_b, conv_ln_g=m_conv_ln_g, conv_ln_b=m_conv_ln_b, w_attn_o=m_w_attn_o, w_conv_o=m_w_conv_o,
                b_conv_o=m_b_conv_o, w_out=m_w_out, ffn_norm_g=m_ffn_norm_g, w_gate_up=m_w_gate_up, w_down=m_w_down,
                final_norm_g=m_final_norm_g)
    v_in = dict(meta_tokens=v_meta_tokens, mix_norm_g=v_mix_norm_g, w_in=v_w_in, b_in=v_b_in, attn_sinks=v_attn_sinks, conv_w=v_conv_w,
                conv_b=v_conv_b, conv_ln_g=v_conv_ln_g, conv_ln_b=v_conv_ln_b, w_attn_o=v_w_attn_o, w_conv_o=v_w_conv_o,
                b_conv_o=v_b_conv_o, w_out=v_w_out, ffn_norm_g=v_ffn_norm_g, w_gate_up=v_w_gate_up, w_down=v_w_down,
                final_norm_g=v_final_norm_g)
    names = list(weights)
    grads, delta, new_m, new_v = {}, {}, {}, {}
    transposed = ("w_in", "w_attn_o", "w_conv_o", "w_gate_up")
    tok = grad_x
    for n in ("w_down", "w_gate_up", "w_out", "w_attn_o", "w_conv_o", "w_in"):
        send_sem, recv_sem, ps, rx = in_flight[n]
        ps, rx = _chip_exchange_wait(send_sem, recv_sem, ps, rx, tok, "rs_" + n + "_chip_wait")
        g = _sum_chips(ps, rx, "rs_" + n + "_sum")
        g = g.T if n in transposed else g
        shape = weights[n].shape
        dl, nm, nv = _adamw(weights[n].reshape(g.shape), g, m_in[n].reshape(g.shape), v_in[n].reshape(g.shape), "adamw_" + n)
        grads[n], delta[n], new_m[n], new_v[n] = g.reshape(shape), dl.reshape(shape), nm.reshape(shape), nv.reshape(shape)
        tok = dl

    slab, slab_layout = _pack([loss_part[:, :1], d_mix_g, d_b_in, d_sinks[:, :N_Q_HEADS], d_conv_b, d_ln_g, d_ln_b, d_bco,
                               d_ffn_g, d_final_g, d_conv_w, d_meta])
    slab_all = _all_gather_rows(slab, "gather_small_grads", after=tok).reshape(N_DEV, *slab.shape)
    (loss, g_mix_g, g_b_in, g_sinks, g_conv_b, g_ln_g, g_ln_b, g_bco, g_ffn_g, g_final_g, g_conv_w_full, g_meta_full
     ) = _unpack(_sum_blocks(slab_all, "sum_small_grads"), slab_layout)
    g_conv_w = lax.dynamic_slice(g_conv_w_full, (0, dev * cw_cols), (CONV_WIDTH, cw_cols)).reshape(conv_w.shape)
    g_meta = lax.dynamic_slice(g_meta_full, (0, dev * meta_cols), (N_META, meta_cols))
    g_final_g = g_final_g.reshape(final_norm_g.shape)
    grads.update(meta_tokens=g_meta, mix_norm_g=g_mix_g, b_in=g_b_in, attn_sinks=g_sinks, conv_w=g_conv_w, conv_b=g_conv_b,
                 conv_ln_g=g_ln_g, conv_ln_b=g_ln_b, b_conv_o=g_bco, ffn_norm_g=g_ffn_g, final_norm_g=g_final_g)
    rest = [n for n in names if n not in delta]
    w_slab, rest_layout = _pack([weights[n] for n in rest])
    g_slab, _ = _pack([grads[n] for n in rest])
    m_slab, _ = _pack([m_in[n] for n in rest])
    v_slab, _ = _pack([v_in[n] for n in rest])
    dl, nm, nv = _adamw(w_slab, g_slab, m_slab, v_slab, "adamw_small")
    for n, a, b, c in zip(rest, _unpack(dl, rest_layout), _unpack(nm, rest_layout), _unpack(nv, rest_layout)):
        delta[n], new_m[n], new_v[n] = a, b, c

    return (loss.reshape(()), grad_x[None], *[grads[n] for n in names], *[delta[n] for n in names],
            *[new_m[n] for n in names], *[new_v[n] for n in names])
```

```python
import functools
import math

import jax
import jax.numpy as jnp
from jax import lax
from jax.experimental import pallas as pl
from jax.experimental.pallas import tpu as pltpu

F32 = jnp.float32
BF16 = jnp.bfloat16

N_DEV = 8
BLOCK = 128
N_META = 16
PAD_ROWS = BLOCK - N_META
HEAD_DIM = 64
N_Q_HEADS = 32
N_KV_HEADS = 4
GROUP = N_Q_HEADS // N_KV_HEADS
Q_DIM = N_Q_HEADS * HEAD_DIM
KV_DIM = N_KV_HEADS * HEAD_DIM
WINDOW = 128
CONV_WIDTH = 31
CONV_ROWS = 32
ROPE_THETA = 10000.0
EPS = 1e-6
ATTN_SCALE = HEAD_DIM ** -0.5
NEG = -1e30

ADAM_LR = 0.001
ADAM_B1 = 0.9
ADAM_B2 = 0.999
ADAM_EPS = 1e-08
ADAM_WD = 0.01
ADAM_STEP = 10

VMEM_LIMIT_BYTES = 56 * 1024 * 1024
LANES = 128
ELEMENTWISE_BLOCK_BYTES = 2 * 1024 * 1024
MESH = pl.DeviceIdType.MESH
CHIPS = ((0, 0), (0, 1), (1, 0), (1, 1))


def _pcall(body, after=None, **kw):
    if after is None:
        return pl.pallas_call(body, **kw)
    in_specs = list(kw.pop("in_specs"))
    n_in = len(in_specs)

    def ordered_body(*refs):
        return body(*refs[:n_in], *refs[n_in + 1:])

    call = pl.pallas_call(ordered_body, in_specs=in_specs + [pl.BlockSpec(memory_space=pl.ANY)], **kw)
    return lambda *args: call(*args, after)


def _params(semantics=None):
    if semantics is None:
        return pltpu.CompilerParams(vmem_limit_bytes=VMEM_LIMIT_BYTES)
    return pltpu.CompilerParams(dimension_semantics=semantics, vmem_limit_bytes=VMEM_LIMIT_BYTES)


def _pick(dim, pref, align):
    best = None
    t = align
    while t <= min(dim, pref):
        if dim % t == 0:
            best = t
        t += align
    return dim if best is None else best


def _sigmoid(x):
    return 1.0 / (1.0 + jnp.exp(-x))


def _matmul(a, b, *, mode, name, out_dtype, tm, tn, tk, bias=None, residual=None, b_inner=True,
            b_row_off=0, b_rows=None, after=None):
    if mode == "nn":
        m, k = a.shape
        n = b.shape[1]
    elif mode == "nt":
        m, k = a.shape
        n = b.shape[0] if b_rows is None else b_rows
    else:
        k, m = a.shape
        n = b.shape[1]
    tm = _pick(m, tm, 16)
    tn = _pick(math.gcd(n, b_row_off) if mode == "nt" and b_row_off else n, tn, LANES)
    tk = _pick(k, tk, LANES if mode != "tn" else 16)
    nm, nn, nk = m // tm, n // tn, k // tk
    if mode == "nt":
        assert b_row_off % tn == 0
    off = b_row_off // tn if mode == "nt" else 0

    if b_inner:
        grid = (nm, nn, nk)
        ij = lambda g0, g1: (g0, g1)
    else:
        grid = (nn, nm, nk)
        ij = lambda g0, g1: (g1, g0)

    if mode == "tn":
        a_spec = pl.BlockSpec((tk, tm), lambda g0, g1, kk: (kk, ij(g0, g1)[0]))
    else:
        a_spec = pl.BlockSpec((tm, tk), lambda g0, g1, kk: (ij(g0, g1)[0], kk))
    if mode == "nt":
        b_spec = pl.BlockSpec((tn, tk), lambda g0, g1, kk: (ij(g0, g1)[1] + off, kk))
    else:
        b_spec = pl.BlockSpec((tk, tn), lambda g0, g1, kk: (kk, ij(g0, g1)[1]))
    o_spec = pl.BlockSpec((tm, tn), lambda g0, g1, kk: ij(g0, g1))
    in_specs = [a_spec, b_spec]
    args = [a, b]
    if bias is not None:
        in_specs.append(pl.BlockSpec((1, tn), lambda g0, g1, kk: (0, ij(g0, g1)[1])))
        args.append(bias)
    if residual is not None:
        in_specs.append(o_spec)
        args.append(residual)
    dims = {"nn": (((1,), (0,)), ((), ())), "nt": (((1,), (1,)), ((), ())), "tn": (((0,), (0,)), ((), ()))}[mode]
    has_bias, has_res = bias is not None, residual is not None

    def body(*refs):
        a_ref, b_ref = refs[0], refs[1]
        pos = 2
        bias_ref = res_ref = None
        if has_bias:
            bias_ref = refs[pos]
            pos += 1
        if has_res:
            res_ref = refs[pos]
            pos += 1
        o_ref = refs[pos]
        acc_ref = refs[pos + 1] if nk > 1 else None

        def finish(acc):
            if has_bias:
                acc = acc + bias_ref[...]
            if has_res:
                acc = acc + res_ref[...]
            o_ref[...] = acc.astype(out_dtype)

        p = lax.dot_general(a_ref[...], b_ref[...], dims, preferred_element_type=F32)
        if nk == 1:
            finish(p)
        else:
            kk = pl.program_id(2)

            @pl.when(kk == 0)
            def _():
                acc_ref[...] = p

            @pl.when(kk > 0)
            def _():
                acc_ref[...] += p

            @pl.when(kk == nk - 1)
            def _():
                finish(acc_ref[...])

    return _pcall(
        body, after=after, name=name, grid=grid, in_specs=in_specs, out_specs=o_spec,
        out_shape=jax.ShapeDtypeStruct((m, n), out_dtype),
        scratch_shapes=[pltpu.VMEM((tm, tn), F32)] if nk > 1 else [],
        compiler_params=_params(("parallel", "parallel", "arbitrary")),
    )(*args)


def _row_spec(width, col=0):
    return pl.BlockSpec((BLOCK, width), lambda i: (i, col))


def _const_spec(shape):
    nd = len(shape)
    return pl.BlockSpec(shape, lambda i: (0,) * nd)


def _prep(x, meta_full, g, after=None):
    s, d = x.shape
    lp = s + BLOCK
    nb = lp // BLOCK

    def body(x_ref, meta_ref, g_ref, h_ref, u_ref):
        i = pl.program_id(0)

        @pl.when(i == 0)
        def _():
            h_ref[0:PAD_ROWS, :] = jnp.zeros((PAD_ROWS, d), F32)
            h_ref[PAD_ROWS:BLOCK, :] = meta_ref[...]

        @pl.when(i > 0)
        def _():
            h_ref[...] = x_ref[...]

        h = h_ref[...]
        r = lax.rsqrt(jnp.mean(h * h, axis=-1, keepdims=True) + EPS)
        u_ref[...] = (h * r * g_ref[...]).astype(BF16)

    return _pcall(
        body, after=after, name="prep_rmsnorm", grid=(nb,),
        in_specs=[pl.BlockSpec((BLOCK, d), lambda i: (jnp.maximum(i - 1, 0), 0)), _const_spec((N_META, d)), _const_spec((1, d))],
        out_specs=[_row_spec(d), _row_spec(d)],
        out_shape=[jax.ShapeDtypeStruct((lp, d), F32), jax.ShapeDtypeStruct((lp, d), BF16)],
        compiler_params=_params(("arbitrary",)),
    )(x, meta_full, g)


def _rmsnorm_fwd(h, g, name):
    lp, d = h.shape

    def body(h_ref, g_ref, u_ref):
        x = h_ref[...]
        r = lax.rsqrt(jnp.mean(x * x, axis=-1, keepdims=True) + EPS)
        u_ref[...] = (x * r * g_ref[...]).astype(BF16)

    return _pcall(
        body, name=name, grid=(lp // BLOCK,), in_specs=[_row_spec(d), _const_spec((1, d))], out_specs=_row_spec(d),
        out_shape=jax.ShapeDtypeStruct((lp, d), BF16), compiler_params=_params(("parallel",)),
    )(h, g)


def _rms_bwd_core(dy, x, g):
    r = lax.rsqrt(jnp.mean(x * x, axis=-1, keepdims=True) + EPS)
    xhat = x * r
    dxhat = dy * g
    dx = r * (dxhat - xhat * jnp.mean(dxhat * xhat, axis=-1, keepdims=True))
    return dx, jnp.sum(dy * xhat, axis=0, keepdims=True)


def _rmsnorm_bwd(dy, h, g, dres, name):
    lp, d = h.shape

    def body(dy_ref, h_ref, g_ref, dres_ref, dh_ref, dhb_ref, dg_ref):
        i = pl.program_id(0)
        dx, dg = _rms_bwd_core(dy_ref[...], h_ref[...], g_ref[...])
        dh = dres_ref[...] + dx
        dh_ref[...] = dh
        dhb_ref[...] = dh.astype(BF16)

        @pl.when(i == 0)
        def _():
            dg_ref[...] = jnp.zeros_like(dg_ref)

        dg_ref[...] += dg

    return _pcall(
        body, name=name, grid=(lp // BLOCK,),
        in_specs=[_row_spec(d), _row_spec(d), _const_spec((1, d)), _row_spec(d)],
        out_specs=[_row_spec(d), _row_spec(d), _const_spec((1, d))],
        out_shape=[jax.ShapeDtypeStruct((lp, d), F32), jax.ShapeDtypeStruct((lp, d), BF16), jax.ShapeDtypeStruct((1, d), F32)],
        compiler_params=_params(("arbitrary",)),
    )(dy, h, g, dres)


def _rmsnorm_bwd_first(dy, h, g, dres):
    lp, d = h.shape
    s = lp - BLOCK

    def body(dy_ref, h_ref, g_ref, dres_ref, gx_ref, dmeta_ref, dg_ref):
        i = pl.program_id(0)
        dx, dg = _rms_bwd_core(dy_ref[...], h_ref[...], g_ref[...])
        dh = dres_ref[...] + dx
        gx_ref[...] = dh

        @pl.when(i == 0)
        def _():
            dmeta_ref[...] = dh[PAD_ROWS:BLOCK, :]
            dg_ref[...] = jnp.zeros_like(dg_ref)

        dg_ref[...] += dg

    return _pcall(
        body, name="rmsnorm_bwd_first", grid=(lp // BLOCK,),
        in_specs=[_row_spec(d), _row_spec(d), _const_spec((1, d)), _row_spec(d)],
        out_specs=[pl.BlockSpec((BLOCK, d), lambda i: (jnp.maximum(i - 1, 0), 0)), _const_spec((N_META, d)), _const_spec((1, d))],
        out_shape=[jax.ShapeDtypeStruct((s, d), F32), jax.ShapeDtypeStruct((N_META, d), F32), jax.ShapeDtypeStruct((1, d), F32)],
        compiler_params=_params(("arbitrary",)),
    )(dy, h, g, dres)


def _final(h2, tgt, g):
    lp, d = h2.shape

    def body(h_ref, t_ref, g_ref, dh_ref, dhb_ref, loss_ref, dg_ref):
        i = pl.program_id(0)
        x = h_ref[...]
        gg = g_ref[...]
        r = lax.rsqrt(jnp.mean(x * x, axis=-1, keepdims=True) + EPS)
        xhat = x * r
        y = xhat * gg
        live = (i > 0).astype(F32)
        err = (y - t_ref[...]) * live
        dy = err * (1.0 / d)
        dxhat = dy * gg
        dh = r * (dxhat - xhat * jnp.mean(dxhat * xhat, axis=-1, keepdims=True))
        dh_ref[...] = dh
        dhb_ref[...] = dh.astype(BF16)

        @pl.when(i == 0)
        def _():
            loss_ref[...] = jnp.zeros_like(loss_ref)
            dg_ref[...] = jnp.zeros_like(dg_ref)

        row_loss = jnp.mean(err * err, axis=-1, keepdims=True)
        loss_ref[...] += 0.5 * jnp.sum(row_loss, axis=0, keepdims=True)
        dg_ref[...] += jnp.sum(dy * xhat, axis=0, keepdims=True)

    return _pcall(
        body, name="final_norm_loss", grid=(lp // BLOCK,),
        in_specs=[_row_spec(d), pl.BlockSpec((BLOCK, d), lambda i: (jnp.maximum(i - 1, 0), 0)), _const_spec((1, d))],
        out_specs=[_row_spec(d), _row_spec(d), _const_spec((1, LANES)), _const_spec((1, d))],
        out_shape=[jax.ShapeDtypeStruct((lp, d), F32), jax.ShapeDtypeStruct((lp, d), BF16),
                   jax.ShapeDtypeStruct((1, LANES), F32), jax.ShapeDtypeStruct((1, d), F32)],
        compiler_params=_params(("arbitrary",)),
    )(h2, tgt, g)


def _swap_halves(x):
    w = x.shape[1]
    lane = lax.broadcasted_iota(jnp.int32, x.shape, 1)
    first = (lane & (HEAD_DIM - 1)) < (HEAD_DIM // 2)
    return jnp.where(first, pltpu.roll(x, w - HEAD_DIM // 2, 1), pltpu.roll(x, HEAD_DIM // 2, 1))


def _rope_tables(lp):
    pos = jnp.maximum(jnp.arange(lp, dtype=jnp.int32) - PAD_ROWS, 0).astype(F32)
    inv_freq = ROPE_THETA ** (-jnp.arange(0, HEAD_DIM, 2, dtype=F32) / HEAD_DIM)
    ang = pos[:, None] * inv_freq[None, :]
    c, s = jnp.cos(ang), jnp.sin(ang)
    reps = LANES // HEAD_DIM
    return jnp.tile(jnp.concatenate([c, c], axis=1), (1, reps)), jnp.tile(jnp.concatenate([-s, s], axis=1), (1, reps))


def _rope_fwd(zq, zkv, ctab, stab, after=None):
    lp = zq.shape[0]
    nb = lp // BLOCK
    back = lambda s: (jnp.maximum(s - 1, 0), 0)

    def body(zq_ref, zkv_ref, c_ref, s_ref, q_ref, k_ref, v_ref):
        step = pl.program_id(0)
        c128, s128 = c_ref[...], s_ref[...]

        def rope(x):
            reps = x.shape[1] // LANES
            return x * jnp.tile(c128, (1, reps)) + _swap_halves(x) * jnp.tile(s128, (1, reps))

        q_ref[...] = (rope(zq_ref[...]) * ATTN_SCALE).astype(BF16)
        kv = zkv_ref[...]
        k = rope(kv[:, :KV_DIM])
        v = kv[:, KV_DIM:]

        @pl.when(step == 0)
        def _():
            k_ref[...] = jnp.zeros_like(k_ref)
            v_ref[...] = jnp.zeros_like(v_ref)

        @pl.when(step > 0)
        def _():
            for h in range(N_KV_HEADS):
                k_ref[h] = k[:, h * HEAD_DIM:(h + 1) * HEAD_DIM].astype(BF16)
                v_ref[h] = v[:, h * HEAD_DIM:(h + 1) * HEAD_DIM].astype(BF16)

    kv_spec = pl.BlockSpec((N_KV_HEADS, BLOCK, HEAD_DIM), lambda s: (0, s, 0))
    return _pcall(
        body, after=after, name="rope_fwd", grid=(nb + 1,),
        in_specs=[pl.BlockSpec((BLOCK, Q_DIM), back), pl.BlockSpec((BLOCK, 2 * KV_DIM), back),
                  pl.BlockSpec((BLOCK, LANES), back), pl.BlockSpec((BLOCK, LANES), back)],
        out_specs=[pl.BlockSpec((BLOCK, Q_DIM), back), kv_spec, kv_spec],
        out_shape=[jax.ShapeDtypeStruct((lp, Q_DIM), BF16),
                   jax.ShapeDtypeStruct((N_KV_HEADS, lp + BLOCK, HEAD_DIM), BF16),
                   jax.ShapeDtypeStruct((N_KV_HEADS, lp + BLOCK, HEAD_DIM), BF16)],
        compiler_params=_params(("arbitrary",)),
    )(zq, zkv, ctab, stab)


def _rope_bwd(dq, dk, dv, dkm, dvm, ctab, stab):
    lp = dq.shape[0]
    width = Q_DIM + 2 * KV_DIM
    head_spec = pl.BlockSpec((N_KV_HEADS, BLOCK, HEAD_DIM), lambda i: (0, i, 0))
    meta_spec = _const_spec((N_KV_HEADS, BLOCK, HEAD_DIM))

    def body(dq_ref, dk_ref, dv_ref, dkm_ref, dvm_ref, c_ref, s_ref, dz_ref, sum_ref, kbuf, vbuf):
        i = pl.program_id(0)
        c128, s128 = c_ref[...], s_ref[...]
        first = (i == 0).astype(F32)

        def rope_t(x):
            reps = x.shape[1] // LANES
            return x * jnp.tile(c128, (1, reps)) + _swap_halves(x * jnp.tile(s128, (1, reps)))

        for h in range(N_KV_HEADS):
            kbuf[:, h * HEAD_DIM:(h + 1) * HEAD_DIM] = dk_ref[h] + first * dkm_ref[h]
            vbuf[:, h * HEAD_DIM:(h + 1) * HEAD_DIM] = dv_ref[h] + first * dvm_ref[h]
        dzq = rope_t(dq_ref[...] * ATTN_SCALE)
        dzk = rope_t(kbuf[...])
        dzv = vbuf[...]
        dz_ref[:, 0:Q_DIM] = dzq.astype(BF16)
        dz_ref[:, Q_DIM:Q_DIM + KV_DIM] = dzk.astype(BF16)
        dz_ref[:, Q_DIM + KV_DIM:width] = dzv.astype(BF16)

        @pl.when(i == 0)
        def _():
            sum_ref[...] = jnp.zeros_like(sum_ref)

        sum_ref[:, 0:Q_DIM] += jnp.sum(dzq, axis=0, keepdims=True)
        sum_ref[:, Q_DIM:Q_DIM + KV_DIM] += jnp.sum(dzk, axis=0, keepdims=True)
        sum_ref[:, Q_DIM + KV_DIM:width] += jnp.sum(dzv, axis=0, keepdims=True)

    return _pcall(
        body, name="rope_bwd", grid=(lp // BLOCK,),
        in_specs=[_row_spec(Q_DIM), head_spec, head_spec, meta_spec, meta_spec, _row_spec(LANES), _row_spec(LANES)],
        out_specs=[_row_spec(width), _const_spec((1, width))],
        out_shape=[jax.ShapeDtypeStruct((lp, width), BF16), jax.ShapeDtypeStruct((1, width), F32)],
        scratch_shapes=[pltpu.VMEM((BLOCK, KV_DIM), F32), pltpu.VMEM((BLOCK, KV_DIM), F32)],
        compiler_params=_params(("arbitrary",)),
    )(dq, dk, dv, dkm, dvm, ctab, stab)


def _attn_bias(i):
    r = lax.broadcasted_iota(jnp.int32, (BLOCK, 3 * BLOCK), 0)
    c = lax.broadcasted_iota(jnp.int32, (BLOCK, 3 * BLOCK), 1)
    qp = i * BLOCK + r - PAD_ROWS
    kp = (i - 1) * BLOCK + c - PAD_ROWS
    band = (c < 2 * BLOCK) & (kp >= N_META) & (kp <= qp) & (qp - kp < WINDOW)
    mp = c - 2 * BLOCK - PAD_ROWS
    meta = (c >= 2 * BLOCK) & (mp >= 0) & (mp <= qp)
    return jnp.where(band | meta, 0.0, NEG).astype(F32)


def _stack_heads(ref, h):
    return jnp.concatenate(
        [ref[:, (h * GROUP + g) * HEAD_DIM:(h * GROUP + g + 1) * HEAD_DIM] for g in range(GROUP)], axis=0)


def _attn_probs(qs, k3, bias8, sink):
    s = lax.dot_general(qs, k3, (((1,), (1,)), ((), ())), preferred_element_type=F32) + bias8
    m = jnp.maximum(jnp.max(s, axis=1, keepdims=True), sink)
    p = jnp.exp(s - m)
    ps = jnp.exp(sink - m)
    inv = 1.0 / (jnp.sum(p, axis=1, keepdims=True) + ps)
    return p * inv, ps * inv


def _sink_column(sink_ref, h):
    return jnp.concatenate(
        [jnp.broadcast_to(sink_ref[0:1, h * GROUP + g:h * GROUP + g + 1], (BLOCK, 1)) for g in range(GROUP)], axis=0)


def _attn_fwd(q, k_sh, v_sh, sinks):
    lp = q.shape[0]
    nb = lp // BLOCK
    kv = lambda f: pl.BlockSpec((N_KV_HEADS, BLOCK, HEAD_DIM), f)

    def body(q_ref, kp_ref, kc_ref, km_ref, vp_ref, vc_ref, vm_ref, sink_ref, o_ref):
        i = pl.program_id(0)
        bias8 = jnp.tile(_attn_bias(i), (GROUP, 1))
        for h in range(N_KV_HEADS):
            k3 = jnp.concatenate([kp_ref[h], kc_ref[h], km_ref[h]], axis=0)
            v3 = jnp.concatenate([vp_ref[h], vc_ref[h], vm_ref[h]], axis=0)
            qs = _stack_heads(q_ref, h)
            p, _ = _attn_probs(qs, k3, bias8, _sink_column(sink_ref, h))
            o = jnp.dot(p.astype(BF16), v3, preferred_element_type=F32)
            for g in range(GROUP):
                n = h * GROUP + g
                o_ref[:, n * HEAD_DIM:(n + 1) * HEAD_DIM] = o[g * BLOCK:(g + 1) * BLOCK].astype(BF16)

    prev, cur, meta = (lambda i: (0, i, 0)), (lambda i: (0, i + 1, 0)), (lambda i: (0, 1, 0))
    return _pcall(
        body, name="attn_fwd", grid=(nb,),
        in_specs=[_row_spec(Q_DIM), kv(prev), kv(cur), kv(meta), kv(prev), kv(cur), kv(meta), _const_spec((1, N_Q_HEADS))],
        out_specs=_row_spec(Q_DIM), out_shape=jax.ShapeDtypeStruct((lp, Q_DIM), BF16),
        compiler_params=_params(("parallel",)),
    )(q, k_sh, k_sh, k_sh, v_sh, v_sh, v_sh, sinks)


def _attn_bwd(q, k_sh, v_sh, sinks, do):
    lp = q.shape[0]
    nb = lp // BLOCK
    kv = lambda f: pl.BlockSpec((N_KV_HEADS, BLOCK, HEAD_DIM), f)
    cl = lambda s: jnp.minimum(s, nb - 1)

    def body(q_ref, do_ref, kp_ref, kc_ref, km_ref, vp_ref, vc_ref, vm_ref, sink_ref,
             dq_ref, dk_ref, dv_ref, dkm_ref, dvm_ref, dsink_ref, carry_k, carry_v):
        step = pl.program_id(0)

        @pl.when(step == 0)
        def _():
            carry_k[...] = jnp.zeros_like(carry_k)
            carry_v[...] = jnp.zeros_like(carry_v)
            dkm_ref[...] = jnp.zeros_like(dkm_ref)
            dvm_ref[...] = jnp.zeros_like(dvm_ref)
            dsink_ref[...] = jnp.zeros_like(dsink_ref)

        @pl.when(step < nb)
        def _():
            bias8 = jnp.tile(_attn_bias(step), (GROUP, 1))
            lane = lax.broadcasted_iota(jnp.int32, (1, LANES), 1)
            dsink = jnp.zeros((1, LANES), F32)
            for h in range(N_KV_HEADS):
                k3 = jnp.concatenate([kp_ref[h], kc_ref[h], km_ref[h]], axis=0)
                v3 = jnp.concatenate([vp_ref[h], vc_ref[h], vm_ref[h]], axis=0)
                qs = _stack_heads(q_ref, h)
                dos = _stack_heads(do_ref, h)
                p, psink = _attn_probs(qs, k3, bias8, _sink_column(sink_ref, h))
                dp = lax.dot_general(dos, v3, (((1,), (1,)), ((), ())), preferred_element_type=F32)
                delta = jnp.sum(p * dp, axis=1, keepdims=True)
                ds = (p * (dp - delta)).astype(BF16)
                dsk = -psink * delta
                for g in range(GROUP):
                    val = jnp.sum(dsk[g * BLOCK:(g + 1) * BLOCK], axis=0, keepdims=True)
                    dsink = dsink + jnp.where(lane == h * GROUP + g, val, 0.0)
                dqs = jnp.dot(ds, k3, preferred_element_type=F32)
                for g in range(GROUP):
                    n = h * GROUP + g
                    dq_ref[:, n * HEAD_DIM:(n + 1) * HEAD_DIM] = dqs[g * BLOCK:(g + 1) * BLOCK]
                dk3 = lax.dot_general(ds, qs, (((0,), (0,)), ((), ())), preferred_element_type=F32)
                dv3 = lax.dot_general(p.astype(BF16), dos, (((0,), (0,)), ((), ())), preferred_element_type=F32)
                dk_ref[h] = carry_k[h] + dk3[0:BLOCK]
                dv_ref[h] = carry_v[h] + dv3[0:BLOCK]
                carry_k[h] = dk3[BLOCK:2 * BLOCK]
                carry_v[h] = dv3[BLOCK:2 * BLOCK]
                dkm_ref[h] += dk3[2 * BLOCK:3 * BLOCK]
                dvm_ref[h] += dv3[2 * BLOCK:3 * BLOCK]
            dsink_ref[...] += dsink

        @pl.when(step == nb)
        def _():
            dk_ref[...] = carry_k[...]
            dv_ref[...] = carry_v[...]

    prev, cur, meta = (lambda s: (0, cl(s), 0)), (lambda s: (0, cl(s) + 1, 0)), (lambda s: (0, 1, 0))
    lag = lambda s: (0, jnp.maximum(s - 1, 0), 0)
    head_shape = jax.ShapeDtypeStruct((N_KV_HEADS, lp, HEAD_DIM), F32)
    meta_shape = jax.ShapeDtypeStruct((N_KV_HEADS, BLOCK, HEAD_DIM), F32)
    return _pcall(
        body, name="attn_bwd", grid=(nb + 1,),
        in_specs=[pl.BlockSpec((BLOCK, Q_DIM), lambda s: (cl(s), 0)), pl.BlockSpec((BLOCK, Q_DIM), lambda s: (cl(s), 0)),
                  kv(prev), kv(cur), kv(meta), kv(prev), kv(cur), kv(meta), _const_spec((1, N_Q_HEADS))],
        out_specs=[pl.BlockSpec((BLOCK, Q_DIM), lambda s: (cl(s), 0)), kv(lag), kv(lag),
                   _const_spec((N_KV_HEADS, BLOCK, HEAD_DIM)), _const_spec((N_KV_HEADS, BLOCK, HEAD_DIM)), _const_spec((1, LANES))],
        out_shape=[jax.ShapeDtypeStruct((lp, Q_DIM), F32), head_shape, head_shape, meta_shape, meta_shape,
                   jax.ShapeDtypeStruct((1, LANES), F32)],
        scratch_shapes=[pltpu.VMEM((N_KV_HEADS, BLOCK, HEAD_DIM), F32), pltpu.VMEM((N_KV_HEADS, BLOCK, HEAD_DIM), F32)],
        compiler_params=_params(("arbitrary",)),
    )(q, do, k_sh, k_sh, k_sh, v_sh, v_sh, v_sh, sinks)


CONV_CHUNK = 256


def _glu_masked(a_ref, g_ref, base):
    rows = base + lax.broadcasted_iota(jnp.int32, (BLOCK, 1), 0)
    return jnp.where(rows >= PAD_ROWS, a_ref[...] * _sigmoid(g_ref[...]), 0.0)


def _conv_fwd(zc, conv_w, conv_b, ln_g, ln_b, after=None):
    lp = zc.shape[0]
    cd = zc.shape[1] // 2
    nb = lp // BLOCK
    chunk = min(CONV_CHUNK, cd)
    back = lambda col: (lambda i: (jnp.maximum(i - 1, 0), col))
    lo = BLOCK - (CONV_WIDTH - 1)

    def body(ap_ref, gp_ref, ac_ref, gc_ref, w_ref, b_ref, lg_ref, lb_ref, co_ref, c2_ref, ext):
        i = pl.program_id(0)
        ext[0:BLOCK, :] = _glu_masked(ap_ref, gp_ref, (i - 1) * BLOCK)
        ext[BLOCK:2 * BLOCK, :] = _glu_masked(ac_ref, gc_ref, i * BLOCK)
        for c0 in range(0, cd, chunk):
            acc = jnp.zeros((BLOCK, chunk), F32)
            for k in range(CONV_WIDTH):
                acc = acc + ext[lo + k:lo + k + BLOCK, c0:c0 + chunk] * w_ref[k:k + 1, c0:c0 + chunk]
            co_ref[:, c0:c0 + chunk] = acc + b_ref[:, c0:c0 + chunk]
        x = co_ref[...]
        mu = jnp.mean(x, axis=-1, keepdims=True)
        xc = x - mu
        r = lax.rsqrt(jnp.mean(xc * xc, axis=-1, keepdims=True) + EPS)
        y = xc * r * lg_ref[...] + lb_ref[...]
        c2_ref[...] = (y * _sigmoid(y)).astype(BF16)

    return _pcall(
        body, after=after, name="conv_fwd", grid=(nb,),
        in_specs=[pl.BlockSpec((BLOCK, cd), back(0)), pl.BlockSpec((BLOCK, cd), back(1)), _row_spec(cd, 0), _row_spec(cd, 1),
                  _const_spec((CONV_ROWS, cd)), _const_spec((1, cd)), _const_spec((1, cd)), _const_spec((1, cd))],
        out_specs=[_row_spec(cd), _row_spec(cd)],
        out_shape=[jax.ShapeDtypeStruct((lp, cd), F32), jax.ShapeDtypeStruct((lp, cd), BF16)],
        scratch_shapes=[pltpu.VMEM((2 * BLOCK, cd), F32)],
        compiler_params=_params(("arbitrary",)),
    )(zc, zc, zc, zc, conv_w, conv_b, ln_g, ln_b)


def _conv_bwd_norm(dc2, conv_out, ln_g, ln_b):
    lp, cd = conv_out.shape

    def body(d_ref, x_ref, lg_ref, lb_ref, dco_ref, dlg_ref, dlb_ref, dcb_ref):
        i = pl.program_id(0)
        x = x_ref[...]
        g = lg_ref[...]
        mu = jnp.mean(x, axis=-1, keepdims=True)
        xc = x - mu
        r = lax.rsqrt(jnp.mean(xc * xc, axis=-1, keepdims=True) + EPS)
        xhat = xc * r
        y = xhat * g + lb_ref[...]
        sg = _sigmoid(y)
        dy = d_ref[...] * (sg * (1.0 + y * (1.0 - sg)))
        dxhat = dy * g
        dx = r * (dxhat - jnp.mean(dxhat, axis=-1, keepdims=True) - xhat * jnp.mean(dxhat * xhat, axis=-1, keepdims=True))
        dco_ref[...] = dx

        @pl.when(i == 0)
        def _():
            dlg_ref[...] = jnp.zeros_like(dlg_ref)
            dlb_ref[...] = jnp.zeros_like(dlb_ref)
            dcb_ref[...] = jnp.zeros_like(dcb_ref)

        dlg_ref[...] += jnp.sum(dy * xhat, axis=0, keepdims=True)
        dlb_ref[...] += jnp.sum(dy, axis=0, keepdims=True)
        dcb_ref[...] += jnp.sum(dx, axis=0, keepdims=True)

    vec = jax.ShapeDtypeStruct((1, cd), F32)
    return _pcall(
        body, name="conv_bwd_norm", grid=(lp // BLOCK,),
        in_specs=[_row_spec(cd), _row_spec(cd), _const_spec((1, cd)), _const_spec((1, cd))],
        out_specs=[_row_spec(cd), _const_spec((1, cd)), _const_spec((1, cd)), _const_spec((1, cd))],
        out_shape=[jax.ShapeDtypeStruct((lp, cd), F32), vec, vec, vec],
        compiler_params=_params(("arbitrary",)),
    )(dc2, conv_out, ln_g, ln_b)


def _conv_bwd_taps(dco, zc, conv_w):
    lp, cd = dco.shape
    nb = lp // BLOCK
    chunk = min(CONV_CHUNK, cd)
    back = lambda col: (lambda i: (jnp.maximum(i - 1, 0), col))
    fwd = lambda i: (jnp.minimum(i + 1, nb - 1), 0)
    lo = BLOCK - (CONV_WIDTH - 1)

    def body(dc_ref, dn_ref, ap_ref, gp_ref, ac_ref, gc_ref, w_ref, dz_ref, sum_ref, dw_ref, ext, dext, dcb):
        i = pl.program_id(0)
        ext[0:BLOCK, :] = _glu_masked(ap_ref, gp_ref, (i - 1) * BLOCK)
        ext[BLOCK:2 * BLOCK, :] = _glu_masked(ac_ref, gc_ref, i * BLOCK)
        dext[0:BLOCK, :] = dc_ref[...]
        dext[BLOCK:2 * BLOCK, :] = dn_ref[...] * (i < nb - 1).astype(F32)

        @pl.when(i == 0)
        def _():
            dw_ref[...] = jnp.zeros_like(dw_ref)
            sum_ref[...] = jnp.zeros_like(sum_ref)

        for c0 in range(0, cd, chunk):
            cols = slice(c0, c0 + chunk)
            dcur = dext[0:BLOCK, cols]
            acc = jnp.zeros((BLOCK, chunk), F32)
            for k in range(CONV_WIDTH):
                s = CONV_WIDTH - 1 - k
                acc = acc + dext[s:s + BLOCK, cols] * w_ref[k:k + 1, cols]
                dw_ref[k:k + 1, cols] += jnp.sum(dcur * ext[lo + k:lo + k + BLOCK, cols], axis=0, keepdims=True)
            dcb[:, cols] = acc
        rows = i * BLOCK + lax.broadcasted_iota(jnp.int32, (BLOCK, 1), 0)
        dc = jnp.where(rows >= PAD_ROWS, dcb[...], 0.0)
        a = ac_ref[...]
        sg = _sigmoid(gc_ref[...])
        da = dc * sg
        dg = dc * a * sg * (1.0 - sg)
        dz_ref[:, 0:cd] = da.astype(BF16)
        dz_ref[:, cd:2 * cd] = dg.astype(BF16)
        sum_ref[:, 0:cd] += jnp.sum(da, axis=0, keepdims=True)
        sum_ref[:, cd:2 * cd] += jnp.sum(dg, axis=0, keepdims=True)

    return _pcall(
        body, name="conv_bwd_taps", grid=(nb,),
        in_specs=[_row_spec(cd), pl.BlockSpec((BLOCK, cd), fwd),
                  pl.BlockSpec((BLOCK, cd), back(0)), pl.BlockSpec((BLOCK, cd), back(1)), _row_spec(cd, 0), _row_spec(cd, 1),
                  _const_spec((CONV_ROWS, cd))],
        out_specs=[_row_spec(2 * cd), _const_spec((1, 2 * cd)), _const_spec((CONV_ROWS, cd))],
        out_shape=[jax.ShapeDtypeStruct((lp, 2 * cd), BF16), jax.ShapeDtypeStruct((1, 2 * cd), F32),
                   jax.ShapeDtypeStruct((CONV_ROWS, cd), F32)],
        scratch_shapes=[pltpu.VMEM((2 * BLOCK, cd), F32), pltpu.VMEM((2 * BLOCK, cd), F32), pltpu.VMEM((BLOCK, cd), F32)],
        compiler_params=_params(("arbitrary",)),
    )(dco, dco, zc, zc, zc, zc, conv_w)


def _gate_fwd(a, b, zg, after=None):
    lp, d = a.shape

    def body(a_ref, b_ref, ga_ref, gb_ref, m_ref):
        m_ref[...] = (_sigmoid(ga_ref[...]) * a_ref[...] + _sigmoid(gb_ref[...]) * b_ref[...]).astype(BF16)

    return _pcall(
        body, after=after, name="gate_fwd", grid=(lp // BLOCK,),
        in_specs=[_row_spec(d), _row_spec(d), _row_spec(d, 0), _row_spec(d, 1)], out_specs=_row_spec(d),
        out_shape=jax.ShapeDtypeStruct((lp, d), BF16), compiler_params=_params(("parallel",)),
    )(a, b, zg, zg)


def _gate_bwd(dm, a, b, zg):
    lp, d = a.shape

    def body(dm_ref, a_ref, b_ref, ga_ref, gb_ref, da_ref, db_ref, dz_ref, sum_ref, dbias_ref):
        i = pl.program_id(0)
        dm_ = dm_ref[...]
        sa = _sigmoid(ga_ref[...])
        sb = _sigmoid(gb_ref[...])
        db = dm_ * sb
        dga = dm_ * a_ref[...] * sa * (1.0 - sa)
        dgb = dm_ * b_ref[...] * sb * (1.0 - sb)
        da_ref[...] = (dm_ * sa).astype(BF16)
        db_ref[...] = db.astype(BF16)
        dz_ref[:, 0:d] = dga.astype(BF16)
        dz_ref[:, d:2 * d] = dgb.astype(BF16)

        @pl.when(i == 0)
        def _():
            sum_ref[...] = jnp.zeros_like(sum_ref)
            dbias_ref[...] = jnp.zeros_like(dbias_ref)

        sum_ref[:, 0:d] += jnp.sum(dga, axis=0, keepdims=True)
        sum_ref[:, d:2 * d] += jnp.sum(dgb, axis=0, keepdims=True)
        dbias_ref[...] += jnp.sum(db, axis=0, keepdims=True)

    return _pcall(
        body, name="gate_bwd", grid=(lp // BLOCK,),
        in_specs=[_row_spec(d), _row_spec(d), _row_spec(d), _row_spec(d, 0), _row_spec(d, 1)],
        out_specs=[_row_spec(d), _row_spec(d), _row_spec(2 * d), _const_spec((1, 2 * d)), _const_spec((1, d))],
        out_shape=[jax.ShapeDtypeStruct((lp, d), BF16), jax.ShapeDtypeStruct((lp, d), BF16), jax.ShapeDtypeStruct((lp, 2 * d), BF16),
                   jax.ShapeDtypeStruct((1, 2 * d), F32), jax.ShapeDtypeStruct((1, d), F32)],
        compiler_params=_params(("arbitrary",)),
    )(dm, a, b, zg, zg)


def _swiglu_fwd(gu, after=None):
    lp = gu.shape[0]
    f = gu.shape[1] // 2

    def body(g_ref, u_ref, o_ref):
        g = g_ref[...]
        o_ref[...] = (g * _sigmoid(g) * u_ref[...]).astype(BF16)

    return _pcall(
        body, after=after, name="swiglu_fwd", grid=(lp // BLOCK,), in_specs=[_row_spec(f, 0), _row_spec(f, 1)], out_specs=_row_spec(f),
        out_shape=jax.ShapeDtypeStruct((lp, f), BF16), compiler_params=_params(("parallel",)),
    )(gu, gu)


def _swiglu_bwd(dact, gu):
    lp, f = dact.shape

    def body(d_ref, g_ref, u_ref, o_ref):
        g = g_ref[...]
        d = d_ref[...]
        sg = _sigmoid(g)
        o_ref[:, 0:f] = (d * u_ref[...] * (sg * (1.0 + g * (1.0 - sg)))).astype(BF16)
        o_ref[:, f:2 * f] = (d * g * sg).astype(BF16)

    return _pcall(
        body, name="swiglu_bwd", grid=(lp // BLOCK,), in_specs=[_row_spec(f), _row_spec(f, 0), _row_spec(f, 1)],
        out_specs=_row_spec(2 * f), out_shape=jax.ShapeDtypeStruct((lp, 2 * f), BF16), compiler_params=_params(("parallel",)),
    )(dact, gu, gu)


ANY = pl.BlockSpec(memory_space=pl.ANY)


def _all_gather_rows(x, name, after=None):
    r, c = x.shape

    def body(x_ref, out_ref, send_sems, recv_sems, local_sem):
        mx, my, mc = lax.axis_index("x"), lax.axis_index("y"), lax.axis_index("c")
        me, sibling = (mx, my, mc), (mx, my, 1 - mc)
        chips = [(1 - mx, my), (mx, 1 - my), (1 - mx, 1 - my)]

        def rows(px, py, pc):
            return out_ref.at[pl.ds((4 * px + 2 * py + pc) * r, r), :]

        def copy(k, block, to, src=None):
            return pltpu.make_async_remote_copy(
                src_ref=rows(*block) if src is None else src, dst_ref=rows(*block),
                send_sem=send_sems.at[k], recv_sem=recv_sems.at[k], device_id=to, device_id_type=MESH)

        mine = pltpu.make_async_copy(x_ref, rows(*me), local_sem)
        mine.start()
        first = [copy(0, me, sibling, src=x_ref)]
        first += [copy(1 + j, me, (*chip, mc), src=x_ref) for j, chip in enumerate(chips)]
        for cp in first:
            cp.start()
        passed = [copy(4 + j, (*chip, mc), sibling) for j, chip in enumerate(chips)]
        for j, chip in enumerate(chips):
            copy(1 + j, (*chip, mc), me).wait_recv()
            passed[j].start()
        copy(0, sibling, me).wait_recv()
        for j, chip in enumerate(chips):
            copy(4 + j, (*chip, 1 - mc), me).wait_recv()
        for cp in first + passed:
            cp.wait_send()
        mine.wait()

    return _pcall(
        body, after=after, name=name, in_specs=[ANY], out_specs=ANY, out_shape=jax.ShapeDtypeStruct((N_DEV * r, c), x.dtype),
        scratch_shapes=[pltpu.SemaphoreType.DMA((7,)), pltpu.SemaphoreType.DMA((7,)), pltpu.SemaphoreType.DMA(())],
    )(x)


HBM = pl.BlockSpec(memory_space=pltpu.HBM)
SEM = pl.BlockSpec(memory_space=pltpu.SEMAPHORE)
IN_FLIGHT = pltpu.CompilerParams(has_side_effects=pltpu.SideEffectType.DATAFLOW_SIDE_EFFECTING)
N_PEERS = 4


def _place_rows(shard, name):
    r, c = shard.shape
    tr = _pick(r, max(16, ELEMENTWISE_BLOCK_BYTES // (4 * c)), 16)
    steps = r // tr
    dev = (4 * lax.axis_index("x") + 2 * lax.axis_index("y") + lax.axis_index("c")).astype(jnp.int32).reshape(1)

    def body(dev_ref, x_ref, o_ref):
        o_ref[...] = x_ref[...].astype(BF16)

    return _pcall(
        body, name=name,
        grid_spec=pltpu.PrefetchScalarGridSpec(
            num_scalar_prefetch=1, grid=(steps,),
            in_specs=[pl.BlockSpec((tr, c), lambda i, dev_ref: (i, 0))],
            out_specs=pl.BlockSpec((tr, c), lambda i, dev_ref: (dev_ref[0] * steps + i, 0))),
        out_shape=jax.ShapeDtypeStruct((N_DEV * r, c), BF16), compiler_params=_params(("parallel",)),
    )(dev, shard)


def _gather_start(fulls, after):
    n = len(fulls)

    def body(*refs):
        full_refs = refs[:n]
        send_sems, recv_sems = refs[n + 1:2 * n + 1], refs[2 * n + 1:3 * n + 1]
        mx, my, mc = lax.axis_index("x"), lax.axis_index("y"), lax.axis_index("c")
        peers = [(mx, my, 1 - mc), (1 - mx, my, mc), (mx, 1 - my, mc), (1 - mx, 1 - my, mc)]
        for w in range(n):
            r = full_refs[w].shape[0] // N_DEV
            mine = full_refs[w].at[pl.ds((4 * mx + 2 * my + mc) * r, r), :]
            for k, peer in enumerate(peers):
                pltpu.make_async_remote_copy(
                    src_ref=mine, dst_ref=mine, send_sem=send_sems[w].at[k], recv_sem=recv_sems[w].at[k],
                    device_id=peer, device_id_type=MESH).start()

    sems = tuple(pltpu.SemaphoreType.DMA((N_PEERS,)) for _ in range(2 * n))
    outs = pl.pallas_call(
        body, name="gather_start", in_specs=[HBM] * (n + 1),
        out_specs=(SEM,) * (2 * n) + (HBM,) * n,
        out_shape=sems + tuple(pltpu.HBM(f.shape, f.dtype) for f in fulls),
        input_output_aliases={w: 2 * n + w for w in range(n)}, compiler_params=IN_FLIGHT,
    )(*[pltpu.with_memory_space_constraint(f, pltpu.HBM) for f in fulls], after)
    return outs[:n], outs[n:2 * n], outs[2 * n:]


def _gather_wait(full, send_sem, recv_sem, after, name):
    r = full.shape[0] // N_DEV

    def body(full_ref, send_ref, recv_ref, after_ref, out_ref):
        mx, my, mc = lax.axis_index("x"), lax.axis_index("y"), lax.axis_index("c")
        block = full_ref.at[pl.ds(0, r), :]
        for k in range(N_PEERS):
            cp = pltpu.make_async_remote_copy(
                src_ref=block, dst_ref=block, send_sem=send_ref.at[k], recv_sem=recv_ref.at[k],
                device_id=(mx, my, mc), device_id_type=MESH)
            cp.wait_send()
            cp.wait_recv()

    return pl.pallas_call(
        body, name=name, in_specs=[HBM, SEM, SEM, pl.BlockSpec(memory_space=pl.ANY)], out_specs=HBM,
        out_shape=pltpu.HBM(full.shape, full.dtype), input_output_aliases={0: 0}, compiler_params=IN_FLIGHT,
    )(full, send_sem, recv_sem, after)


def _gather_forward_start(full, name):
    r = full.shape[0] // N_DEV

    def body(full_ref, send_sems, recv_sems, out_ref):
        mx, my, mc = lax.axis_index("x"), lax.axis_index("y"), lax.axis_index("c")
        for k, (px, py) in enumerate([(1 - mx, my), (mx, 1 - my), (1 - mx, 1 - my)]):
            rows = full_ref.at[pl.ds((4 * px + 2 * py + mc) * r, r), :]
            pltpu.make_async_remote_copy(
                src_ref=rows, dst_ref=rows, send_sem=send_sems.at[k], recv_sem=recv_sems.at[k],
                device_id=(mx, my, 1 - mc), device_id_type=MESH).start()

    return pl.pallas_call(
        body, name=name, in_specs=[HBM], out_specs=(SEM, SEM, HBM),
        out_shape=(pltpu.SemaphoreType.DMA((3,)), pltpu.SemaphoreType.DMA((3,)), pltpu.HBM(full.shape, full.dtype)),
        input_output_aliases={0: 2}, compiler_params=IN_FLIGHT,
    )(full)


def _gather_forward_wait(full, send_sem, recv_sem, after, name):
    r = full.shape[0] // N_DEV

    def body(full_ref, send_ref, recv_ref, after_ref, out_ref):
        mx, my, mc = lax.axis_index("x"), lax.axis_index("y"), lax.axis_index("c")
        block = full_ref.at[pl.ds(0, r), :]
        for k in range(3):
            cp = pltpu.make_async_remote_copy(
                src_ref=block, dst_ref=block, send_sem=send_ref.at[k], recv_sem=recv_ref.at[k],
                device_id=(mx, my, mc), device_id_type=MESH)
            cp.wait_send()
            cp.wait_recv()

    return pl.pallas_call(
        body, name=name, in_specs=[HBM, SEM, SEM, pl.BlockSpec(memory_space=pl.ANY)], out_specs=HBM,
        out_shape=pltpu.HBM(full.shape, full.dtype), input_output_aliases={0: 0}, compiler_params=IN_FLIGHT,
    )(full, send_sem, recv_sem, after)


def _pair_exchange_start(g, name):
    r = g.shape[0] // N_DEV
    c = g.shape[1]
    land = (len(CHIPS), r, c)

    def body(g_ref, land_ref, send_sems, recv_sems, g_out, land_out):
        mx, my, mc = lax.axis_index("x"), lax.axis_index("y"), lax.axis_index("c")
        for j, (px, py) in enumerate(CHIPS):
            pltpu.make_async_remote_copy(
                src_ref=g_ref.at[pl.ds((4 * px + 2 * py + 1 - mc) * r, r), :], dst_ref=land_ref.at[j],
                send_sem=send_sems.at[j], recv_sem=recv_sems.at[j], device_id=(mx, my, 1 - mc), device_id_type=MESH).start()

    return pl.pallas_call(
        body, name=name, in_specs=[HBM, HBM], out_specs=(SEM, SEM, HBM, HBM),
        out_shape=(pltpu.SemaphoreType.DMA((4,)), pltpu.SemaphoreType.DMA((4,)), pltpu.HBM(g.shape, g.dtype), pltpu.HBM(land, g.dtype)),
        input_output_aliases={0: 2, 1: 3}, compiler_params=IN_FLIGHT,
    )(pltpu.with_memory_space_constraint(g, pltpu.HBM), pltpu.with_memory_space_constraint(lax.empty(land, g.dtype), pltpu.HBM))


def _pair_exchange_wait(send_sem, recv_sem, g, land, after, name):
    def body(g_ref, land_ref, send_ref, recv_ref, after_ref, g_out, land_out):
        mx, my, mc = lax.axis_index("x"), lax.axis_index("y"), lax.axis_index("c")
        for j in range(len(CHIPS)):
            cp = pltpu.make_async_remote_copy(
                src_ref=land_ref.at[0], dst_ref=land_ref.at[0], send_sem=send_ref.at[j], recv_sem=recv_ref.at[j],
                device_id=(mx, my, mc), device_id_type=MESH)
            cp.wait_send()
            cp.wait_recv()

    return pl.pallas_call(
        body, name=name, in_specs=[HBM, HBM, SEM, SEM, pl.BlockSpec(memory_space=pl.ANY)], out_specs=(HBM, HBM),
        out_shape=(pltpu.HBM(g.shape, g.dtype), pltpu.HBM(land.shape, land.dtype)), input_output_aliases={0: 0, 1: 1},
        compiler_params=IN_FLIGHT,
    )(g, land, send_sem, recv_sem, after)


def _chip_exchange_start(ps, after, name):
    def body(ps_ref, rx_ref, after_ref, send_sems, recv_sems, ps_out, rx_out):
        mx, my, mc = lax.axis_index("x"), lax.axis_index("y"), lax.axis_index("c")
        chips = [(1 - mx, my), (mx, 1 - my), (1 - mx, 1 - my)]
        for k, (px, py) in enumerate(chips):
            pltpu.make_async_remote_copy(
                src_ref=ps_ref.at[2 * px + py], dst_ref=rx_ref.at[2 * mx + my], send_sem=send_sems.at[k], recv_sem=recv_sems.at[k],
                device_id=(px, py, mc), device_id_type=MESH).start()

    return pl.pallas_call(
        body, name=name, in_specs=[HBM, HBM, pl.BlockSpec(memory_space=pl.ANY)], out_specs=(SEM, SEM, HBM, HBM),
        out_shape=(pltpu.SemaphoreType.DMA((3,)), pltpu.SemaphoreType.DMA((3,)), pltpu.HBM(ps.shape, ps.dtype), pltpu.HBM(ps.shape, ps.dtype)),
        input_output_aliases={0: 2, 1: 3}, compiler_params=IN_FLIGHT,
    )(pltpu.with_memory_space_constraint(ps, pltpu.HBM), pltpu.with_memory_space_constraint(lax.empty(ps.shape, ps.dtype), pltpu.HBM), after)


def _chip_exchange_wait(send_sem, recv_sem, ps, rx, after, name):
    def body(ps_ref, rx_ref, send_ref, recv_ref, after_ref, ps_out, rx_out):
        mx, my, mc = lax.axis_index("x"), lax.axis_index("y"), lax.axis_index("c")
        for k in range(3):
            cp = pltpu.make_async_remote_copy(
                src_ref=ps_ref.at[0], dst_ref=rx_ref.at[0], send_sem=send_ref.at[k], recv_sem=recv_ref.at[k],
                device_id=(mx, my, mc), device_id_type=MESH)
            cp.wait_send()
            cp.wait_recv()

    return pl.pallas_call(
        body, name=name, in_specs=[HBM, HBM, SEM, SEM, pl.BlockSpec(memory_space=pl.ANY)], out_specs=(HBM, HBM),
        out_shape=(pltpu.HBM(ps.shape, ps.dtype), pltpu.HBM(rx.shape, rx.dtype)), input_output_aliases={0: 0, 1: 1},
        compiler_params=IN_FLIGHT,
    )(ps, rx, send_sem, recv_sem, after)


def _sum_chips(ps, rx, name):
    n, r, c = rx.shape
    tr = _pick(r, max(8, ELEMENTWISE_BLOCK_BYTES // (4 * n * c)), 8)
    chip = (2 * lax.axis_index("x") + lax.axis_index("y")).astype(jnp.int32).reshape(1)

    def body(chip_ref, own_ref, x_ref, o_ref):
        me = chip_ref[0]
        own = own_ref[0].astype(F32)
        acc = jnp.where(me == 0, own, x_ref[0].astype(F32))
        for j in range(1, n):
            acc = acc + jnp.where(me == j, own, x_ref[j].astype(F32))
        o_ref[...] = acc

    return _pcall(
        body, name=name,
        grid_spec=pltpu.PrefetchScalarGridSpec(
            num_scalar_prefetch=1, grid=(r // tr,),
            in_specs=[pl.BlockSpec((1, tr, c), lambda i, chip_ref: (chip_ref[0], i, 0)), pl.BlockSpec((n, tr, c), lambda i, chip_ref: (0, i, 0))],
            out_specs=pl.BlockSpec((tr, c), lambda i, chip_ref: (i, 0))),
        out_shape=jax.ShapeDtypeStruct((r, c), F32), compiler_params=_params(("parallel",)),
    )(chip, ps, rx)


def _pair_exchange(g, name):
    r = g.shape[0] // N_DEV
    c = g.shape[1]

    def body(g_ref, theirs_ref, send_sems, recv_sems):
        mx, my, mc = lax.axis_index("x"), lax.axis_index("y"), lax.axis_index("c")
        sibling = (mx, my, 1 - mc)
        copies = []
        for j, (px, py) in enumerate(CHIPS):
            give = g_ref.at[pl.ds((4 * px + 2 * py + 1 - mc) * r, r), :]
            rc = pltpu.make_async_remote_copy(
                src_ref=give, dst_ref=theirs_ref.at[j], send_sem=send_sems.at[j], recv_sem=recv_sems.at[j],
                device_id=sibling, device_id_type=MESH)
            rc.start()
            copies.append(rc)
        for cp in copies:
            cp.wait()

    return _pcall(
        body, name=name, in_specs=[ANY], out_specs=ANY, out_shape=jax.ShapeDtypeStruct((len(CHIPS), r, c), g.dtype),
        scratch_shapes=[pltpu.SemaphoreType.DMA((4,)), pltpu.SemaphoreType.DMA((4,))],
    )(g)


def _pair_sum(g, theirs, name):
    nch, r, c = theirs.shape
    tr = _pick(r, max(16, ELEMENTWISE_BLOCK_BYTES // (2 * c)), 16)
    core = lax.axis_index("c").astype(jnp.int32).reshape(1)

    def body(core_ref, a_ref, b_ref, o_ref):
        o_ref[...] = (a_ref[...].astype(F32) + b_ref[...].astype(F32)).astype(o_ref.dtype)

    spec = pl.BlockSpec((1, tr, c), lambda j, i, core_ref: (j, i, 0))
    own = pl.BlockSpec((1, tr, c), lambda j, i, core_ref: (2 * j + core_ref[0], i, 0))
    return _pcall(
        body, name=name,
        grid_spec=pltpu.PrefetchScalarGridSpec(num_scalar_prefetch=1, grid=(nch, r // tr), in_specs=[own, spec], out_specs=spec),
        out_shape=jax.ShapeDtypeStruct(theirs.shape, theirs.dtype), compiler_params=_params(("parallel", "parallel")),
    )(core, g.reshape(N_DEV, r, c), theirs)


def _sum_blocks(rx, name):
    n, r, c = rx.shape
    tr = _pick(r, max(8, ELEMENTWISE_BLOCK_BYTES // (4 * n * c)), 8)

    def body(x_ref, o_ref):
        acc = x_ref[0].astype(F32)
        for j in range(1, n):
            acc = acc + x_ref[j].astype(F32)
        o_ref[...] = acc

    return _pcall(
        body, name=name, grid=(r // tr,), in_specs=[pl.BlockSpec((n, tr, c), lambda i: (0, i, 0))],
        out_specs=pl.BlockSpec((tr, c), lambda i: (i, 0)), out_shape=jax.ShapeDtypeStruct((r, c), F32),
        compiler_params=_params(("parallel",)),
    )(rx)


def _adamw(w, g, m, v, name):
    r, c = w.shape
    tr = _pick(r, max(8, ELEMENTWISE_BLOCK_BYTES // (4 * c)), 8)
    c1 = 1.0 - ADAM_B1 ** ADAM_STEP
    c2 = 1.0 - ADAM_B2 ** ADAM_STEP

    def body(w_ref, g_ref, m_ref, v_ref, d_ref, nm_ref, nv_ref):
        gg = g_ref[...]
        nm = ADAM_B1 * m_ref[...] + (1.0 - ADAM_B1) * gg
        nv = ADAM_B2 * v_ref[...] + (1.0 - ADAM_B2) * (gg * gg)
        d_ref[...] = -ADAM_LR * ((nm / c1) / (jnp.sqrt(nv / c2) + ADAM_EPS) + ADAM_WD * w_ref[...])
        nm_ref[...] = nm
        nv_ref[...] = nv

    spec = pl.BlockSpec((tr, c), lambda i: (i, 0))
    shp = jax.ShapeDtypeStruct((r, c), F32)
    return _pcall(
        body, name=name, grid=(r // tr,), in_specs=[spec] * 4, out_specs=[spec] * 3, out_shape=[shp] * 3,
        compiler_params=_params(("parallel",)),
    )(w, g, m, v)


def _pack(parts):
    flat, layout, row = [], [], 0
    for p in parts:
        n = p.size
        rows = -(-n // LANES)
        flat.append(jnp.pad(p.reshape(-1).astype(F32), (0, rows * LANES - n)))
        layout.append((row, n, p.shape))
        row += rows
    total = -(-row // 8) * 8
    if total > row:
        flat.append(jnp.zeros(((total - row) * LANES,), F32))
    return jnp.concatenate(flat).reshape(total, LANES), layout


def _unpack(slab, layout):
    flat = slab.reshape(-1)
    return [flat[row * LANES:row * LANES + n].reshape(shape) for row, n, shape in layout]


def kernel(x, meta_tokens, mix_norm_g, w_in, b_in, attn_sinks, conv_w, conv_b, conv_ln_g, conv_ln_b, w_attn_o, w_conv_o, b_conv_o, w_out, ffn_norm_g, w_gate_up, w_down, final_norm_g, loss_target, m_meta_tokens, m_mix_norm_g, m_w_in, m_b_in, m_attn_sinks, m_conv_w, m_conv_b, m_conv_ln_g, m_conv_ln_b, m_w_attn_o, m_w_conv_o, m_b_conv_o, m_w_out, m_ffn_norm_g, m_w_gate_up, m_w_down, m_final_norm_g, v_meta_tokens, v_mix_norm_g, v_w_in, v_b_in, v_attn_sinks, v_conv_w, v_conv_b, v_conv_ln_g, v_conv_ln_b, v_w_attn_o, v_w_conv_o, v_b_conv_o, v_w_out, v_ffn_norm_g, v_w_gate_up, v_w_down, v_final_norm_g):
    xs = x[0]
    tgt = loss_target[0]
    s, d = xs.shape
    lp = s + BLOCK
    cd = conv_b.shape[1]
    ffn = w_down.shape[1] * N_DEV
    dev = 4 * lax.axis_index("x") + 2 * lax.axis_index("y") + lax.axis_index("c")
    cw_cols = conv_w.shape[3]
    meta_cols = meta_tokens.shape[1]

    small, small_layout = _pack([meta_tokens, jnp.pad(conv_w[0, :, 0, :], ((0, CONV_ROWS - CONV_WIDTH), (0, 0)))])
    small_flat = _all_gather_rows(small, "gather_small")
    small_all = small_flat.reshape(N_DEV, *small.shape)
    meta_parts, cw_parts = zip(*[_unpack(small_all[j], small_layout) for j in range(N_DEV)])
    meta_full = jnp.concatenate(meta_parts, axis=1)
    conv_w_full = jnp.concatenate(cw_parts, axis=1)
    placed = [_place_rows(w_in[0].T, "place_w_in"), _place_rows(w_attn_o[0].T, "place_w_attn_o"),
              _place_rows(w_conv_o[0].T, "place_w_conv_o"), _place_rows(w_out[0], "place_w_out"),
              _place_rows(w_gate_up[0].T, "place_w_gate_up"), _place_rows(w_down[0], "place_w_down")]
    g_send, g_recv, g_full = _gather_start(placed, small_flat)

    def arrived(w, after, name):
        full = _gather_wait(g_full[w], g_send[w], g_recv[w], after, "gather_wait_" + name)
        return _gather_forward_start(full, "gather_forward_start_" + name)

    def whole(passing, after, name):
        return _gather_forward_wait(passing[2], passing[0], passing[1], after, "gather_forward_wait_" + name)

    ctab, stab = _rope_tables(lp)
    mm = functools.partial(_matmul, tm=1056, tn=1024)

    passing = arrived(0, small_flat, "w_in")
    h0, u = _prep(xs, meta_full, mix_norm_g, after=passing[2])
    win_t = whole(passing, u, "w_in")
    bq, bkv, bc, bg = b_in[:, :Q_DIM], b_in[:, Q_DIM:Q_DIM + 2 * KV_DIM], b_in[:, Q_DIM + 2 * KV_DIM:Q_DIM + 2 * KV_DIM + 2 * cd], b_in[:, Q_DIM + 2 * KV_DIM + 2 * cd:]
    o_kv, o_c, o_g = Q_DIM, Q_DIM + 2 * KV_DIM, Q_DIM + 2 * KV_DIM + 2 * cd
    zq = mm(u, win_t, mode="nt", name="in_proj_q", out_dtype=F32, tk=d, bias=bq, b_row_off=0, b_rows=Q_DIM)
    zkv = mm(u, win_t, mode="nt", name="in_proj_kv", out_dtype=F32, tk=d, bias=bkv, b_row_off=o_kv, b_rows=2 * KV_DIM)
    zc = mm(u, win_t, mode="nt", name="in_proj_conv", out_dtype=F32, tk=d, bias=bc, b_row_off=o_c, b_rows=2 * cd)
    zg = mm(u, win_t, mode="nt", name="in_proj_gates", out_dtype=F32, tk=d, bias=bg, b_row_off=o_g, b_rows=2 * d)
    passing = arrived(1, zg, "w_attn_o")
    q_rot, k_sh, v_sh = _rope_fwd(zq, zkv, ctab, stab, after=passing[2])
    o = _attn_fwd(q_rot, k_sh, v_sh, attn_sinks)
    wao_t = whole(passing, o, "w_attn_o")
    br_a = mm(o, wao_t, mode="nt", name="attn_out_proj", out_dtype=F32, tk=Q_DIM)
    passing = arrived(2, br_a, "w_conv_o")
    conv_out, c2 = _conv_fwd(zc, conv_w_full, conv_b, conv_ln_g, conv_ln_b, after=passing[2])
    wco_t = whole(passing, c2, "w_conv_o")
    br_b = mm(c2, wco_t, mode="nt", name="conv_out_proj", out_dtype=F32, tk=cd, bias=b_conv_o)
    passing = arrived(3, br_b, "w_out")
    merged = _gate_fwd(br_a, br_b, zg, after=passing[2])
    wout = whole(passing, merged, "w_out")
    passing = arrived(4, wout, "w_gate_up")
    h1 = mm(merged, wout, mode="nn", name="mix_out_proj", out_dtype=F32, tn=512, tk=d, residual=h0, after=passing[2])
    u2 = _rmsnorm_fwd(h1, ffn_norm_g, "ffn_rmsnorm")
    wgu_t = whole(passing, u2, "w_gate_up")
    gu = _matmul(u2, wgu_t, mode="nt", name="ffn_gate_up", out_dtype=F32, tm=1056, tn=512, tk=d)
    passing = arrived(5, gu, "w_down")
    act = _swiglu_fwd(gu, after=passing[2])
    wdown = whole(passing, act, "w_down")
    h2 = mm(act, wdown, mode="nn", name="ffn_down", out_dtype=F32, tn=512, tk=ffn // 2, residual=h1)
    dh2, dh2_b, loss_part, d_final_g = _final(h2, tgt, final_norm_g.reshape(1, d))

    wgrad = functools.partial(_matmul, mode="tn", out_dtype=BF16, tk=lp, tn=2048, b_inner=False)
    in_flight = {}

    def scatter_begin(g, name):
        return _pair_exchange_start(g, "rs_" + name + "_pair_start")

    def scatter_go_on(pair, after, name):
        g, theirs = _pair_exchange_wait(pair[0], pair[1], pair[2], pair[3], after, "rs_" + name + "_pair_wait")
        ps = _pair_sum(g, theirs, "rs_" + name + "_pair_sum")
        in_flight[name] = _chip_exchange_start(ps, theirs, "rs_" + name + "_chip_start")
        return in_flight[name][2]

    g_wdown = wgrad(act, dh2_b, name="ffn_down_dw", tm=256)
    pair = scatter_begin(g_wdown, "w_down")
    dact = _matmul(dh2_b, wdown, mode="nt", name="ffn_down_dx", out_dtype=F32, tm=2112, tn=256, tk=d, after=pair[2])
    tok = scatter_go_on(pair, dact, "w_down")
    dgu = _swiglu_bwd(dact, gu)
    g_wgu_t = wgrad(dgu, u2, name="ffn_gate_up_dw", tm=512, after=tok)
    pair = scatter_begin(g_wgu_t, "w_gate_up")
    du2 = mm(dgu, wgu_t, mode="nn", name="ffn_gate_up_dx", out_dtype=F32, tn=512, tk=ffn // 2, after=pair[2])
    tok = scatter_go_on(pair, du2, "w_gate_up")
    dh1, dh1_b, d_ffn_g = _rmsnorm_bwd(du2, h1, ffn_norm_g, dh2, "ffn_rmsnorm_bwd")
    g_wout = wgrad(merged, dh1_b, name="mix_out_dw", tm=512, after=tok)
    pair = scatter_begin(g_wout, "w_out")
    dmerged = mm(dh1_b, wout, mode="nt", name="mix_out_dx", out_dtype=F32, tk=d, after=pair[2])
    tok = scatter_go_on(pair, dmerged, "w_out")
    d_a, d_b, dz_g, sum_g, d_bco = _gate_bwd(dmerged, br_a, br_b, zg)
    g_wao_t = wgrad(d_a, o, name="attn_out_dw", tm=512, after=tok)
    pair = scatter_begin(g_wao_t, "w_attn_o")
    do = mm(d_a, wao_t, mode="nn", name="attn_out_dx", out_dtype=BF16, tk=d, after=pair[2])
    tok = scatter_go_on(pair, do, "w_attn_o")
    g_wco_t = wgrad(d_b, c2, name="conv_out_dw", tm=512, after=tok)
    pair = scatter_begin(g_wco_t, "w_conv_o")
    dc2 = mm(d_b, wco_t, mode="nn", name="conv_out_dx", out_dtype=F32, tk=d, after=pair[2])
    tok = scatter_go_on(pair, dc2, "w_conv_o")
    dq, dk, dv, dkm, dvm, d_sinks = _attn_bwd(q_rot, k_sh, v_sh, attn_sinks, do)
    dz_qkv, sum_qkv = _rope_bwd(dq, dk, dv, dkm, dvm, ctab, stab)
    dco, d_ln_g, d_ln_b, d_conv_b = _conv_bwd_norm(dc2, conv_out, conv_ln_g, conv_ln_b)
    dz_c, sum_c, d_conv_w = _conv_bwd_taps(dco, zc, conv_w_full)
    dz = jnp.concatenate([dz_qkv, dz_c, dz_g], axis=1)
    d_b_in = jnp.concatenate([sum_qkv, sum_c, sum_g], axis=1)
    in_dim = dz.shape[1]
    g_win_t = wgrad(dz, u, name="in_proj_dw", tm=512, after=tok)
    theirs = _pair_exchange(g_win_t, "rs_w_in_pair_exchange")
    in_flight["w_in"] = _chip_exchange_start(_pair_sum(g_win_t, theirs, "rs_w_in_pair_sum"), theirs, "rs_w_in_chip_start")
    du = mm(dz, win_t, mode="nn", name="in_proj_dx", out_dtype=F32, tk=in_dim // 4, after=in_flight["w_in"][2])
    grad_x, d_meta, d_mix_g = _rmsnorm_bwd_first(du, h0, mix_norm_g, dh1)

    weights = dict(meta_tokens=meta_tokens, mix_norm_g=mix_norm_g, w_in=w_in, b_in=b_in, attn_sinks=attn_sinks, conv_w=conv_w,
                   conv_b=conv_b, conv_ln_g=conv_ln_g, conv_ln_b=conv_ln_b, w_attn_o=w_attn_o, w_conv_o=w_conv_o, b_conv_o=b_conv_o,
                   w_out=w_out, ffn_norm_g=ffn_norm_g, w_gate_up=w_gate_up, w_down=w_down, final_norm_g=final_norm_g)
    m_in = dict(meta_tokens=m_meta_tokens, mix_norm_g=m_mix_norm_g, w_in=m_w_in, b_in=m_b_in, attn_sinks=m_attn_sinks, conv_w=m_conv_w,
                conv_b=m_conv_b, conv_ln_g=m_conv_ln_g, conv_ln_b=m_conv_ln_b, w_attn_o=m_w_attn_o, w_conv_o=m_w_conv_o,
                b_conv_o=m_b_conv_o, w_out=m_w_out, ffn_norm_g=m_ffn_norm_g, w_gate_up=m_w_gate_up, w_down=m_w_down,
                final_norm_g=m_final_norm_g)
    v_in = dict(meta_tokens=v_meta_tokens, mix_norm_g=v_mix_norm_g, w_in=v_w_in, b_in=v_b_in, attn_sinks=v_attn_sinks, conv_w=v_conv_w,
                conv_b=v_conv_b, conv_ln_g=v_conv_ln_g, conv_ln_b=v_conv_ln_b, w_attn_o=v_w_attn_o, w_conv_o=v_w_conv_o,
                b_conv_o=v_b_conv_o, w_out=v_w_out, ffn_norm_g=v_ffn_norm_g, w_gate_up=v_w_gate_up, w_down=v_w_down,
                final_norm_g=v_final_norm_g)
    names = list(weights)
    grads, delta, new_m, new_v = {}, {}, {}, {}
    transposed = ("w_in", "w_attn_o", "w_conv_o", "w_gate_up")
    tok = grad_x
    for n in ("w_down", "w_gate_up", "w_out", "w_attn_o", "w_conv_o", "w_in"):
        send_sem, recv_sem, ps, rx = in_flight[n]
        ps, rx = _chip_exchange_wait(send_sem, recv_sem, ps, rx, tok, "rs_" + n + "_chip_wait")
        g = _sum_chips(ps, rx, "rs_" + n + "_sum")
        g = g.T if n in transposed else g
        shape = weights[n].shape
        dl, nm, nv = _adamw(weights[n].reshape(g.shape), g, m_in[n].reshape(g.shape), v_in[n].reshape(g.shape), "adamw_" + n)
        grads[n], delta[n], new_m[n], new_v[n] = g.reshape(shape), dl.reshape(shape), nm.reshape(shape), nv.reshape(shape)
        tok = dl

    slab, slab_layout = _pack([loss_part[:, :1], d_mix_g, d_b_in, d_sinks[:, :N_Q_HEADS], d_conv_b, d_ln_g, d_ln_b, d_bco,
                               d_ffn_g, d_final_g, d_conv_w, d_meta])
    slab_all = _all_gather_rows(slab, "gather_small_grads", after=tok).reshape(N_DEV, *slab.shape)
    (loss, g_mix_g, g_b_in, g_sinks, g_conv_b, g_ln_g, g_ln_b, g_bco, g_ffn_g, g_final_g, g_conv_w_full, g_meta_full
     ) = _unpack(_sum_blocks(slab_all, "sum_small_grads"), slab_layout)
    g_conv_w = lax.dynamic_slice(g_conv_w_full, (0, dev * cw_cols), (CONV_WIDTH, cw_cols)).reshape(conv_w.shape)
    g_meta = lax.dynamic_slice(g_meta_full, (0, dev * meta_cols), (N_META, meta_cols))
    g_final_g = g_final_g.reshape(final_norm_g.shape)
    grads.update(meta_tokens=g_meta, mix_norm_g=g_mix_g, b_in=g_b_in, attn_sinks=g_sinks, conv_w=g_conv_w, conv_b=g_conv_b,
                 conv_ln_g=g_ln_g, conv_ln_b=g_ln_b, b_conv_o=g_bco, ffn_norm_g=g_ffn_g, final_norm_g=g_final_g)
    rest = [n for n in names if n not in delta]
    w_slab, rest_layout = _pack([weights[n] for n in rest])
    g_slab, _ = _pack([grads[n] for n in rest])
    m_slab, _ = _pack([m_in[n] for n in rest])
    v_slab, _ = _pack([v_in[n] for n in rest])
    dl, nm, nv = _adamw(w_slab, g_slab, m_slab, v_slab, "adamw_small")
    for n, a, b, c in zip(rest, _unpack(dl, rest_layout), _unpack(nm, rest_layout), _unpack(nv, rest_layout)):
        delta[n], new_m[n], new_v[n] = a, b, c

    return (loss.reshape(()), grad_x[None], *[grads[n] for n in names], *[delta[n] for n in names],
            *[new_m[n] for n in names], *[new_v[n] for n in names])
```

```python
import functools
import math

import jax
import jax.numpy as jnp
from jax import lax
from jax.experimental import pallas as pl
from jax.experimental.pallas import tpu as pltpu

F32 = jnp.float32
BF16 = jnp.bfloat16

N_DEV = 8
BLOCK = 128
N_META = 16
PAD_ROWS = BLOCK - N_META
HEAD_DIM = 64
N_Q_HEADS = 32
N_KV_HEADS = 4
GROUP = N_Q_HEADS // N_KV_HEADS
Q_DIM = N_Q_HEADS * HEAD_DIM
KV_DIM = N_KV_HEADS * HEAD_DIM
WINDOW = 128
CONV_WIDTH = 31
CONV_ROWS = 32
ROPE_THETA = 10000.0
EPS = 1e-6
ATTN_SCALE = HEAD_DIM ** -0.5
NEG = -1e30

ADAM_LR = 0.001
ADAM_B1 = 0.9
ADAM_B2 = 0.999
ADAM_EPS = 1e-08
ADAM_WD = 0.01
ADAM_STEP = 10

VMEM_LIMIT_BYTES = 56 * 1024 * 1024
LANES = 128
ELEMENTWISE_BLOCK_BYTES = 2 * 1024 * 1024
MESH = pl.DeviceIdType.MESH
CHIPS = ((0, 0), (0, 1), (1, 0), (1, 1))


def _pcall(body, after=None, **kw):
    if after is None:
        return pl.pallas_call(body, **kw)
    in_specs = list(kw.pop("in_specs"))
    n_in = len(in_specs)

    def ordered_body(*refs):
        return body(*refs[:n_in], *refs[n_in + 1:])

    call = pl.pallas_call(ordered_body, in_specs=in_specs + [pl.BlockSpec(memory_space=pl.ANY)], **kw)
    return lambda *args: call(*args, after)


def _params(semantics=None):
    if semantics is None:
        return pltpu.CompilerParams(vmem_limit_bytes=VMEM_LIMIT_BYTES)
    return pltpu.CompilerParams(dimension_semantics=semantics, vmem_limit_bytes=VMEM_LIMIT_BYTES)


def _pick(dim, pref, align):
    best = None
    t = align
    while t <= min(dim, pref):
        if dim % t == 0:
            best = t
        t += align
    return dim if best is None else best


def _sigmoid(x):
    return 1.0 / (1.0 + jnp.exp(-x))


def _matmul(a, b, *, mode, name, out_dtype, tm, tn, tk, bias=None, residual=None, b_inner=True,
            b_row_off=0, b_rows=None, after=None):
    if mode == "nn":
        m, k = a.shape
        n = b.shape[1]
    elif mode == "nt":
        m, k = a.shape
        n = b.shape[0] if b_rows is None else b_rows
    else:
        k, m = a.shape
        n = b.shape[1]
    tm = _pick(m, tm, 16)
    tn = _pick(math.gcd(n, b_row_off) if mode == "nt" and b_row_off else n, tn, LANES)
    tk = _pick(k, tk, LANES if mode != "tn" else 16)
    nm, nn, nk = m // tm, n // tn, k // tk
    if mode == "nt":
        assert b_row_off % tn == 0
    off = b_row_off // tn if mode == "nt" else 0

    if b_inner:
        grid = (nm, nn, nk)
        ij = lambda g0, g1: (g0, g1)
    else:
        grid = (nn, nm, nk)
        ij = lambda g0, g1: (g1, g0)

    if mode == "tn":
        a_spec = pl.BlockSpec((tk, tm), lambda g0, g1, kk: (kk, ij(g0, g1)[0]))
    else:
        a_spec = pl.BlockSpec((tm, tk), lambda g0, g1, kk: (ij(g0, g1)[0], kk))
    if mode == "nt":
        b_spec = pl.BlockSpec((tn, tk), lambda g0, g1, kk: (ij(g0, g1)[1] + off, kk))
    else:
        b_spec = pl.BlockSpec((tk, tn), lambda g0, g1, kk: (kk, ij(g0, g1)[1]))
    o_spec = pl.BlockSpec((tm, tn), lambda g0, g1, kk: ij(g0, g1))
    in_specs = [a_spec, b_spec]
    args = [a, b]
    if bias is not None:
        in_specs.append(pl.BlockSpec((1, tn), lambda g0, g1, kk: (0, ij(g0, g1)[1])))
        args.append(bias)
    if residual is not None:
        in_specs.append(o_spec)
        args.append(residual)
    dims = {"nn": (((1,), (0,)), ((), ())), "nt": (((1,), (1,)), ((), ())), "tn": (((0,), (0,)), ((), ()))}[mode]
    has_bias, has_res = bias is not None, residual is not None

    def body(*refs):
        a_ref, b_ref = refs[0], refs[1]
        pos = 2
        bias_ref = res_ref = None
        if has_bias:
            bias_ref = refs[pos]
            pos += 1
        if has_res:
            res_ref = refs[pos]
            pos += 1
        o_ref = refs[pos]
        acc_ref = refs[pos + 1] if nk > 1 else None

        def finish(acc):
            if has_bias:
                acc = acc + bias_ref[...]
            if has_res:
                acc = acc + res_ref[...]
            o_ref[...] = acc.astype(out_dtype)

        p = lax.dot_general(a_ref[...], b_ref[...], dims, preferred_element_type=F32)
        if nk == 1:
            finish(p)
        else:
            kk = pl.program_id(2)

            @pl.when(kk == 0)
            def _():
                acc_ref[...] = p

            @pl.when(kk > 0)
            def _():
                acc_ref[...] += p

            @pl.when(kk == nk - 1)
            def _():
                finish(acc_ref[...])

    return _pcall(
        body, after=after, name=name, grid=grid, in_specs=in_specs, out_specs=o_spec,
        out_shape=jax.ShapeDtypeStruct((m, n), out_dtype),
        scratch_shapes=[pltpu.VMEM((tm, tn), F32)] if nk > 1 else [],
        compiler_params=_params(("parallel", "parallel", "arbitrary")),
    )(*args)


def _row_spec(width, col=0):
    return pl.BlockSpec((BLOCK, width), lambda i: (i, col))


def _const_spec(shape):
    nd = len(shape)
    return pl.BlockSpec(shape, lambda i: (0,) * nd)


def _prep(x, meta_full, g, after=None):
    s, d = x.shape
    lp = s + BLOCK
    nb = lp // BLOCK

    def body(x_ref, meta_ref, g_ref, h_ref, u_ref):
        i = pl.program_id(0)

        @pl.when(i == 0)
        def _():
            h_ref[0:PAD_ROWS, :] = jnp.zeros((PAD_ROWS, d), F32)
            h_ref[PAD_ROWS:BLOCK, :] = meta_ref[...]

        @pl.when(i > 0)
        def _():
            h_ref[...] = x_ref[...]

        h = h_ref[...]
        r = lax.rsqrt(jnp.mean(h * h, axis=-1, keepdims=True) + EPS)
        u_ref[...] = (h * r * g_ref[...]).astype(BF16)

    return _pcall(
        body, after=after, name="prep_rmsnorm", grid=(nb,),
        in_specs=[pl.BlockSpec((BLOCK, d), lambda i: (jnp.maximum(i - 1, 0), 0)), _const_spec((N_META, d)), _const_spec((1, d))],
        out_specs=[_row_spec(d), _row_spec(d)],
        out_shape=[jax.ShapeDtypeStruct((lp, d), F32), jax.ShapeDtypeStruct((lp, d), BF16)],
        compiler_params=_params(("arbitrary",)),
    )(x, meta_full, g)


def _rmsnorm_fwd(h, g, name):
    lp, d = h.shape

    def body(h_ref, g_ref, u_ref):
        x = h_ref[...]
        r = lax.rsqrt(jnp.mean(x * x, axis=-1, keepdims=True) + EPS)
        u_ref[...] = (x * r * g_ref[...]).astype(BF16)

    return _pcall(
        body, name=name, grid=(lp // BLOCK,), in_specs=[_row_spec(d), _const_spec((1, d))], out_specs=_row_spec(d),
        out_shape=jax.ShapeDtypeStruct((lp, d), BF16), compiler_params=_params(("parallel",)),
    )(h, g)


def _rms_bwd_core(dy, x, g):
    r = lax.rsqrt(jnp.mean(x * x, axis=-1, keepdims=True) + EPS)
    xhat = x * r
    dxhat = dy * g
    dx = r * (dxhat - xhat * jnp.mean(dxhat * xhat, axis=-1, keepdims=True))
    return dx, jnp.sum(dy * xhat, axis=0, keepdims=True)


def _rmsnorm_bwd(dy, h, g, dres, name):
    lp, d = h.shape

    def body(dy_ref, h_ref, g_ref, dres_ref, dh_ref, dhb_ref, dg_ref):
        i = pl.program_id(0)
        dx, dg = _rms_bwd_core(dy_ref[...], h_ref[...], g_ref[...])
        dh = dres_ref[...] + dx
        dh_ref[...] = dh
        dhb_ref[...] = dh.astype(BF16)

        @pl.when(i == 0)
        def _():
            dg_ref[...] = jnp.zeros_like(dg_ref)

        dg_ref[...] += dg

    return _pcall(
        body, name=name, grid=(lp // BLOCK,),
        in_specs=[_row_spec(d), _row_spec(d), _const_spec((1, d)), _row_spec(d)],
        out_specs=[_row_spec(d), _row_spec(d), _const_spec((1, d))],
        out_shape=[jax.ShapeDtypeStruct((lp, d), F32), jax.ShapeDtypeStruct((lp, d), BF16), jax.ShapeDtypeStruct((1, d), F32)],
        compiler_params=_params(("arbitrary",)),
    )(dy, h, g, dres)


def _rmsnorm_bwd_first(dy, h, g, dres):
    lp, d = h.shape
    s = lp - BLOCK

    def body(dy_ref, h_ref, g_ref, dres_ref, gx_ref, dmeta_ref, dg_ref):
        i = pl.program_id(0)
        dx, dg = _rms_bwd_core(dy_ref[...], h_ref[...], g_ref[...])
        dh = dres_ref[...] + dx
        gx_ref[...] = dh

        @pl.when(i == 0)
        def _():
            dmeta_ref[...] = dh[PAD_ROWS:BLOCK, :]
            dg_ref[...] = jnp.zeros_like(dg_ref)

        dg_ref[...] += dg

    return _pcall(
        body, name="rmsnorm_bwd_first", grid=(lp // BLOCK,),
        in_specs=[_row_spec(d), _row_spec(d), _const_spec((1, d)), _row_spec(d)],
        out_specs=[pl.BlockSpec((BLOCK, d), lambda i: (jnp.maximum(i - 1, 0), 0)), _const_spec((N_META, d)), _const_spec((1, d))],
        out_shape=[jax.ShapeDtypeStruct((s, d), F32), jax.ShapeDtypeStruct((N_META, d), F32), jax.ShapeDtypeStruct((1, d), F32)],
        compiler_params=_params(("arbitrary",)),
    )(dy, h, g, dres)


def _final(h2, tgt, g):
    lp, d = h2.shape

    def body(h_ref, t_ref, g_ref, dh_ref, dhb_ref, loss_ref, dg_ref):
        i = pl.program_id(0)
        x = h_ref[...]
        gg = g_ref[...]
        r = lax.rsqrt(jnp.mean(x * x, axis=-1, keepdims=True) + EPS)
        xhat = x * r
        y = xhat * gg
        live = (i > 0).astype(F32)
        err = (y - t_ref[...]) * live
        dy = err * (1.0 / d)
        dxhat = dy * gg
        dh = r * (dxhat - xhat * jnp.mean(dxhat * xhat, axis=-1, keepdims=True))
        dh_ref[...] = dh
        dhb_ref[...] = dh.astype(BF16)

        @pl.when(i == 0)
        def _():
            loss_ref[...] = jnp.zeros_like(loss_ref)
            dg_ref[...] = jnp.zeros_like(dg_ref)

        row_loss = jnp.mean(err * err, axis=-1, keepdims=True)
        loss_ref[...] += 0.5 * jnp.sum(row_loss, axis=0, keepdims=True)
        dg_ref[...] += jnp.sum(dy * xhat, axis=0, keepdims=True)

    return _pcall(
        body, name="final_norm_loss", grid=(lp // BLOCK,),
        in_specs=[_row_spec(d), pl.BlockSpec((BLOCK, d), lambda i: (jnp.maximum(i - 1, 0), 0)), _const_spec((1, d))],
        out_specs=[_row_spec(d), _row_spec(d), _const_spec((1, LANES)), _const_spec((1, d))],
        out_shape=[jax.ShapeDtypeStruct((lp, d), F32), jax.ShapeDtypeStruct((lp, d), BF16),
                   jax.ShapeDtypeStruct((1, LANES), F32), jax.ShapeDtypeStruct((1, d), F32)],
        compiler_params=_params(("arbitrary",)),
    )(h2, tgt, g)


def _swap_halves(x):
    w = x.shape[1]
    lane = lax.broadcasted_iota(jnp.int32, x.shape, 1)
    first = (lane & (HEAD_DIM - 1)) < (HEAD_DIM // 2)
    return jnp.where(first, pltpu.roll(x, w - HEAD_DIM // 2, 1), pltpu.roll(x, HEAD_DIM // 2, 1))


def _rope_tables(lp):
    pos = jnp.maximum(jnp.arange(lp, dtype=jnp.int32) - PAD_ROWS, 0).astype(F32)
    inv_freq = ROPE_THETA ** (-jnp.arange(0, HEAD_DIM, 2, dtype=F32) / HEAD_DIM)
    ang = pos[:, None] * inv_freq[None, :]
    c, s = jnp.cos(ang), jnp.sin(ang)
    reps = LANES // HEAD_DIM
    return jnp.tile(jnp.concatenate([c, c], axis=1), (1, reps)), jnp.tile(jnp.concatenate([-s, s], axis=1), (1, reps))


def _rope_fwd(zq, zkv, ctab, stab, after=None):
    lp = zq.shape[0]
    nb = lp // BLOCK
    back = lambda s: (jnp.maximum(s - 1, 0), 0)

    def body(zq_ref, zkv_ref, c_ref, s_ref, q_ref, k_ref, v_ref):
        step = pl.program_id(0)
        c128, s128 = c_ref[...], s_ref[...]

        def rope(x):
            reps = x.shape[1] // LANES
            return x * jnp.tile(c128, (1, reps)) + _swap_halves(x) * jnp.tile(s128, (1, reps))

        q_ref[...] = (rope(zq_ref[...]) * ATTN_SCALE).astype(BF16)
        kv = zkv_ref[...]
        k = rope(kv[:, :KV_DIM])
        v = kv[:, KV_DIM:]

        @pl.when(step == 0)
        def _():
            k_ref[...] = jnp.zeros_like(k_ref)
            v_ref[...] = jnp.zeros_like(v_ref)

        @pl.when(step > 0)
        def _():
            for h in range(N_KV_HEADS):
                k_ref[h] = k[:, h * HEAD_DIM:(h + 1) * HEAD_DIM].astype(BF16)
                v_ref[h] = v[:, h * HEAD_DIM:(h + 1) * HEAD_DIM].astype(BF16)

    kv_spec = pl.BlockSpec((N_KV_HEADS, BLOCK, HEAD_DIM), lambda s: (0, s, 0))
    return _pcall(
        body, after=after, name="rope_fwd", grid=(nb + 1,),
        in_specs=[pl.BlockSpec((BLOCK, Q_DIM), back), pl.BlockSpec((BLOCK, 2 * KV_DIM), back),
                  pl.BlockSpec((BLOCK, LANES), back), pl.BlockSpec((BLOCK, LANES), back)],
        out_specs=[pl.BlockSpec((BLOCK, Q_DIM), back), kv_spec, kv_spec],
        out_shape=[jax.ShapeDtypeStruct((lp, Q_DIM), BF16),
                   jax.ShapeDtypeStruct((N_KV_HEADS, lp + BLOCK, HEAD_DIM), BF16),
                   jax.ShapeDtypeStruct((N_KV_HEADS, lp + BLOCK, HEAD_DIM), BF16)],
        compiler_params=_params(("arbitrary",)),
    )(zq, zkv, ctab, stab)


def _rope_bwd(dq, dk, dv, dkm, dvm, ctab, stab):
    lp = dq.shape[0]
    width = Q_DIM + 2 * KV_DIM
    head_spec = pl.BlockSpec((N_KV_HEADS, BLOCK, HEAD_DIM), lambda i: (0, i, 0))
    meta_spec = _const_spec((N_KV_HEADS, BLOCK, HEAD_DIM))

    def body(dq_ref, dk_ref, dv_ref, dkm_ref, dvm_ref, c_ref, s_ref, dz_ref, sum_ref, kbuf, vbuf):
        i = pl.program_id(0)
        c128, s128 = c_ref[...], s_ref[...]
        first = (i == 0).astype(F32)

        def rope_t(x):
            reps = x.shape[1] // LANES
            return x * jnp.tile(c128, (1, reps)) + _swap_halves(x * jnp.tile(s128, (1, reps)))

        for h in range(N_KV_HEADS):
            kbuf[:, h * HEAD_DIM:(h + 1) * HEAD_DIM] = dk_ref[h] + first * dkm_ref[h]
            vbuf[:, h * HEAD_DIM:(h + 1) * HEAD_DIM] = dv_ref[h] + first * dvm_ref[h]
        dzq = rope_t(dq_ref[...] * ATTN_SCALE)
        dzk = rope_t(kbuf[...])
        dzv = vbuf[...]
        dz_ref[:, 0:Q_DIM] = dzq.astype(BF16)
        dz_ref[:, Q_DIM:Q_DIM + KV_DIM] = dzk.astype(BF16)
        dz_ref[:, Q_DIM + KV_DIM:width] = dzv.astype(BF16)

        @pl.when(i == 0)
        def _():
            sum_ref[...] = jnp.zeros_like(sum_ref)

        sum_ref[:, 0:Q_DIM] += jnp.sum(dzq, axis=0, keepdims=True)
        sum_ref[:, Q_DIM:Q_DIM + KV_DIM] += jnp.sum(dzk, axis=0, keepdims=True)
        sum_ref[:, Q_DIM + KV_DIM:width] += jnp.sum(dzv, axis=0, keepdims=True)

    return _pcall(
        body, name="rope_bwd", grid=(lp // BLOCK,),
        in_specs=[_row_spec(Q_DIM), head_spec, head_spec, meta_spec, meta_spec, _row_spec(LANES), _row_spec(LANES)],
        out_specs=[_row_spec(width), _const_spec((1, width))],
        out_shape=[jax.ShapeDtypeStruct((lp, width), BF16), jax.ShapeDtypeStruct((1, width), F32)],
        scratch_shapes=[pltpu.VMEM((BLOCK, KV_DIM), F32), pltpu.VMEM((BLOCK, KV_DIM), F32)],
        compiler_params=_params(("arbitrary",)),
    )(dq, dk, dv, dkm, dvm, ctab, stab)


def _attn_bias(i):
    r = lax.broadcasted_iota(jnp.int32, (BLOCK, 3 * BLOCK), 0)
    c = lax.broadcasted_iota(jnp.int32, (BLOCK, 3 * BLOCK), 1)
    qp = i * BLOCK + r - PAD_ROWS
    kp = (i - 1) * BLOCK + c - PAD_ROWS
    band = (c < 2 * BLOCK) & (kp >= N_META) & (kp <= qp) & (qp - kp < WINDOW)
    mp = c - 2 * BLOCK - PAD_ROWS
    meta = (c >= 2 * BLOCK) & (mp >= 0) & (mp <= qp)
    return jnp.where(band | meta, 0.0, NEG).astype(F32)


def _stack_heads(ref, h):
    return jnp.concatenate(
        [ref[:, (h * GROUP + g) * HEAD_DIM:(h * GROUP + g + 1) * HEAD_DIM] for g in range(GROUP)], axis=0)


def _attn_probs(qs, k3, bias8, sink):
    s = lax.dot_general(qs, k3, (((1,), (1,)), ((), ())), preferred_element_type=F32) + bias8
    m = jnp.maximum(jnp.max(s, axis=1, keepdims=True), sink)
    p = jnp.exp(s - m)
    ps = jnp.exp(sink - m)
    inv = 1.0 / (jnp.sum(p, axis=1, keepdims=True) + ps)
    return p * inv, ps * inv


def _sink_column(sink_ref, h):
    return jnp.concatenate(
        [jnp.broadcast_to(sink_ref[0:1, h * GROUP + g:h * GROUP + g + 1], (BLOCK, 1)) for g in range(GROUP)], axis=0)


def _attn_fwd(q, k_sh, v_sh, sinks):
    lp = q.shape[0]
    nb = lp // BLOCK
    kv = lambda f: pl.BlockSpec((N_KV_HEADS, BLOCK, HEAD_DIM), f)

    def body(q_ref, kp_ref, kc_ref, km_ref, vp_ref, vc_ref, vm_ref, sink_ref, o_ref):
        i = pl.program_id(0)
        bias8 = jnp.tile(_attn_bias(i), (GROUP, 1))
        for h in range(N_KV_HEADS):
            k3 = jnp.concatenate([kp_ref[h], kc_ref[h], km_ref[h]], axis=0)
            v3 = jnp.concatenate([vp_ref[h], vc_ref[h], vm_ref[h]], axis=0)
            qs = _stack_heads(q_ref, h)
            p, _ = _attn_probs(qs, k3, bias8, _sink_column(sink_ref, h))
            o = jnp.dot(p.astype(BF16), v3, preferred_element_type=F32)
            for g in range(GROUP):
                n = h * GROUP + g
                o_ref[:, n * HEAD_DIM:(n + 1) * HEAD_DIM] = o[g * BLOCK:(g + 1) * BLOCK].astype(BF16)

    prev, cur, meta = (lambda i: (0, i, 0)), (lambda i: (0, i + 1, 0)), (lambda i: (0, 1, 0))
    return _pcall(
        body, name="attn_fwd", grid=(nb,),
        in_specs=[_row_spec(Q_DIM), kv(prev), kv(cur), kv(meta), kv(prev), kv(cur), kv(meta), _const_spec((1, N_Q_HEADS))],
        out_specs=_row_spec(Q_DIM), out_shape=jax.ShapeDtypeStruct((lp, Q_DIM), BF16),
        compiler_params=_params(("parallel",)),
    )(q, k_sh, k_sh, k_sh, v_sh, v_sh, v_sh, sinks)


def _attn_bwd(q, k_sh, v_sh, sinks, do):
    lp = q.shape[0]
    nb = lp // BLOCK
    kv = lambda f: pl.BlockSpec((N_KV_HEADS, BLOCK, HEAD_DIM), f)
    cl = lambda s: jnp.minimum(s, nb - 1)

    def body(q_ref, do_ref, kp_ref, kc_ref, km_ref, vp_ref, vc_ref, vm_ref, sink_ref,
             dq_ref, dk_ref, dv_ref, dkm_ref, dvm_ref, dsink_ref, carry_k, carry_v):
        step = pl.program_id(0)

        @pl.when(step == 0)
        def _():
            carry_k[...] = jnp.zeros_like(carry_k)
            carry_v[...] = jnp.zeros_like(carry_v)
            dkm_ref[...] = jnp.zeros_like(dkm_ref)
            dvm_ref[...] = jnp.zeros_like(dvm_ref)
            dsink_ref[...] = jnp.zeros_like(dsink_ref)

        @pl.when(step < nb)
        def _():
            bias8 = jnp.tile(_attn_bias(step), (GROUP, 1))
            lane = lax.broadcasted_iota(jnp.int32, (1, LANES), 1)
            dsink = jnp.zeros((1, LANES), F32)
            for h in range(N_KV_HEADS):
                k3 = jnp.concatenate([kp_ref[h], kc_ref[h], km_ref[h]], axis=0)
                v3 = jnp.concatenate([vp_ref[h], vc_ref[h], vm_ref[h]], axis=0)
                qs = _stack_heads(q_ref, h)
                dos = _stack_heads(do_ref, h)
                p, psink = _attn_probs(qs, k3, bias8, _sink_column(sink_ref, h))
                dp = lax.dot_general(dos, v3, (((1,), (1,)), ((), ())), preferred_element_type=F32)
                delta = jnp.sum(p * dp, axis=1, keepdims=True)
                ds = (p * (dp - delta)).astype(BF16)
                dsk = -psink * delta
                for g in range(GROUP):
                    val = jnp.sum(dsk[g * BLOCK:(g + 1) * BLOCK], axis=0, keepdims=True)
                    dsink = dsink + jnp.where(lane == h * GROUP + g, val, 0.0)
                dqs = jnp.dot(ds, k3, preferred_element_type=F32)
                for g in range(GROUP):
                    n = h * GROUP + g
                    dq_ref[:, n * HEAD_DIM:(n + 1) * HEAD_DIM] = dqs[g * BLOCK:(g + 1) * BLOCK]
                dk3 = lax.dot_general(ds, qs, (((0,), (0,)), ((), ())), preferred_element_type=F32)
                dv3 = lax.dot_general(p.astype(BF16), dos, (((0,), (0,)), ((), ())), preferred_element_type=F32)
                dk_ref[h] = carry_k[h] + dk3[0:BLOCK]
                dv_ref[h] = carry_v[h] + dv3[0:BLOCK]
                carry_k[h] = dk3[BLOCK:2 * BLOCK]
                carry_v[h] = dv3[BLOCK:2 * BLOCK]
                dkm_ref[h] += dk3[2 * BLOCK:3 * BLOCK]
                dvm_ref[h] += dv3[2 * BLOCK:3 * BLOCK]
            dsink_ref[...] += dsink

        @pl.when(step == nb)
        def _():
            dk_ref[...] = carry_k[...]
            dv_ref[...] = carry_v[...]

    prev, cur, meta = (lambda s: (0, cl(s), 0)), (lambda s: (0, cl(s) + 1, 0)), (lambda s: (0, 1, 0))
    lag = lambda s: (0, jnp.maximum(s - 1, 0), 0)
    head_shape = jax.ShapeDtypeStruct((N_KV_HEADS, lp, HEAD_DIM), F32)
    meta_shape = jax.ShapeDtypeStruct((N_KV_HEADS, BLOCK, HEAD_DIM), F32)
    return _pcall(
        body, name="attn_bwd", grid=(nb + 1,),
        in_specs=[pl.BlockSpec((BLOCK, Q_DIM), lambda s: (cl(s), 0)), pl.BlockSpec((BLOCK, Q_DIM), lambda s: (cl(s), 0)),
                  kv(prev), kv(cur), kv(meta), kv(prev), kv(cur), kv(meta), _const_spec((1, N_Q_HEADS))],
        out_specs=[pl.BlockSpec((BLOCK, Q_DIM), lambda s: (cl(s), 0)), kv(lag), kv(lag),
                   _const_spec((N_KV_HEADS, BLOCK, HEAD_DIM)), _const_spec((N_KV_HEADS, BLOCK, HEAD_DIM)), _const_spec((1, LANES))],
        out_shape=[jax.ShapeDtypeStruct((lp, Q_DIM), F32), head_shape, head_shape, meta_shape, meta_shape,
                   jax.ShapeDtypeStruct((1, LANES), F32)],
        scratch_shapes=[pltpu.VMEM((N_KV_HEADS, BLOCK, HEAD_DIM), F32), pltpu.VMEM((N_KV_HEADS, BLOCK, HEAD_DIM), F32)],
        compiler_params=_params(("arbitrary",)),
    )(q, do, k_sh, k_sh, k_sh, v_sh, v_sh, v_sh, sinks)


CONV_CHUNK = 256


def _glu_masked(a_ref, g_ref, base):
    rows = base + lax.broadcasted_iota(jnp.int32, (BLOCK, 1), 0)
    return jnp.where(rows >= PAD_ROWS, a_ref[...] * _sigmoid(g_ref[...]), 0.0)


def _conv_fwd(zc, conv_w, conv_b, ln_g, ln_b, after=None):
    lp = zc.shape[0]
    cd = zc.shape[1] // 2
    nb = lp // BLOCK
    chunk = min(CONV_CHUNK, cd)
    back = lambda col: (lambda i: (jnp.maximum(i - 1, 0), col))
    lo = BLOCK - (CONV_WIDTH - 1)

    def body(ap_ref, gp_ref, ac_ref, gc_ref, w_ref, b_ref, lg_ref, lb_ref, co_ref, c2_ref, ext):
        i = pl.program_id(0)
        ext[0:BLOCK, :] = _glu_masked(ap_ref, gp_ref, (i - 1) * BLOCK)
        ext[BLOCK:2 * BLOCK, :] = _glu_masked(ac_ref, gc_ref, i * BLOCK)
        for c0 in range(0, cd, chunk):
            acc = jnp.zeros((BLOCK, chunk), F32)
            for k in range(CONV_WIDTH):
                acc = acc + ext[lo + k:lo + k + BLOCK, c0:c0 + chunk] * w_ref[k:k + 1, c0:c0 + chunk]
            co_ref[:, c0:c0 + chunk] = acc + b_ref[:, c0:c0 + chunk]
        x = co_ref[...]
        mu = jnp.mean(x, axis=-1, keepdims=True)
        xc = x - mu
        r = lax.rsqrt(jnp.mean(xc * xc, axis=-1, keepdims=True) + EPS)
        y = xc * r * lg_ref[...] + lb_ref[...]
        c2_ref[...] = (y * _sigmoid(y)).astype(BF16)

    return _pcall(
        body, after=after, name="conv_fwd", grid=(nb,),
        in_specs=[pl.BlockSpec((BLOCK, cd), back(0)), pl.BlockSpec((BLOCK, cd), back(1)), _row_spec(cd, 0), _row_spec(cd, 1),
                  _const_spec((CONV_ROWS, cd)), _const_spec((1, cd)), _const_spec((1, cd)), _const_spec((1, cd))],
        out_specs=[_row_spec(cd), _row_spec(cd)],
        out_shape=[jax.ShapeDtypeStruct((lp, cd), F32), jax.ShapeDtypeStruct((lp, cd), BF16)],
        scratch_shapes=[pltpu.VMEM((2 * BLOCK, cd), F32)],
        compiler_params=_params(("arbitrary",)),
    )(zc, zc, zc, zc, conv_w, conv_b, ln_g, ln_b)


def _conv_bwd_norm(dc2, conv_out, ln_g, ln_b):
    lp, cd = conv_out.shape

    def body(d_ref, x_ref, lg_ref, lb_ref, dco_ref, dlg_ref, dlb_ref, dcb_ref):
        i = pl.program_id(0)
        x = x_ref[...]
        g = lg_ref[...]
        mu = jnp.mean(x, axis=-1, keepdims=True)
        xc = x - mu
        r = lax.rsqrt(jnp.mean(xc * xc, axis=-1, keepdims=True) + EPS)
        xhat = xc * r
        y = xhat * g + lb_ref[...]
        sg = _sigmoid(y)
        dy = d_ref[...] * (sg * (1.0 + y * (1.0 - sg)))
        dxhat = dy * g
        dx = r * (dxhat - jnp.mean(dxhat, axis=-1, keepdims=True) - xhat * jnp.mean(dxhat * xhat, axis=-1, keepdims=True))
        dco_ref[...] = dx

        @pl.when(i == 0)
        def _():
            dlg_ref[...] = jnp.zeros_like(dlg_ref)
            dlb_ref[...] = jnp.zeros_like(dlb_ref)
            dcb_ref[...] = jnp.zeros_like(dcb_ref)

        dlg_ref[...] += jnp.sum(dy * xhat, axis=0, keepdims=True)
        dlb_ref[...] += jnp.sum(dy, axis=0, keepdims=True)
        dcb_ref[...] += jnp.sum(dx, axis=0, keepdims=True)

    vec = jax.ShapeDtypeStruct((1, cd), F32)
    return _pcall(
        body, name="conv_bwd_norm", grid=(lp // BLOCK,),
        in_specs=[_row_spec(cd), _row_spec(cd), _const_spec((1, cd)), _const_spec((1, cd))],
        out_specs=[_row_spec(cd), _const_spec((1, cd)), _const_spec((1, cd)), _const_spec((1, cd))],
        out_shape=[jax.ShapeDtypeStruct((lp, cd), F32), vec, vec, vec],
        compiler_params=_params(("arbitrary",)),
    )(dc2, conv_out, ln_g, ln_b)


def _conv_bwd_taps(dco, zc, conv_w):
    lp, cd = dco.shape
    nb = lp // BLOCK
    chunk = min(CONV_CHUNK, cd)
    back = lambda col: (lambda i: (jnp.maximum(i - 1, 0), col))
    fwd = lambda i: (jnp.minimum(i + 1, nb - 1), 0)
    lo = BLOCK - (CONV_WIDTH - 1)

    def body(dc_ref, dn_ref, ap_ref, gp_ref, ac_ref, gc_ref, w_ref, dz_ref, sum_ref, dw_ref, ext, dext, dcb):
        i = pl.program_id(0)
        ext[0:BLOCK, :] = _glu_masked(ap_ref, gp_ref, (i - 1) * BLOCK)
        ext[BLOCK:2 * BLOCK, :] = _glu_masked(ac_ref, gc_ref, i * BLOCK)
        dext[0:BLOCK, :] = dc_ref[...]
        dext[BLOCK:2 * BLOCK, :] = dn_ref[...] * (i < nb - 1).astype(F32)

        @pl.when(i == 0)
        def _():
            dw_ref[...] = jnp.zeros_like(dw_ref)
            sum_ref[...] = jnp.zeros_like(sum_ref)

        for c0 in range(0, cd, chunk):
            cols = slice(c0, c0 + chunk)
            dcur = dext[0:BLOCK, cols]
            acc = jnp.zeros((BLOCK, chunk), F32)
            for k in range(CONV_WIDTH):
                s = CONV_WIDTH - 1 - k
                acc = acc + dext[s:s + BLOCK, cols] * w_ref[k:k + 1, cols]
                dw_ref[k:k + 1, cols] += jnp.sum(dcur * ext[lo + k:lo + k + BLOCK, cols], axis=0, keepdims=True)
            dcb[:, cols] = acc
        rows = i * BLOCK + lax.broadcasted_iota(jnp.int32, (BLOCK, 1), 0)
        dc = jnp.where(rows >= PAD_ROWS, dcb[...], 0.0)
        a = ac_ref[...]
        sg = _sigmoid(gc_ref[...])
        da = dc * sg
        dg = dc * a * sg * (1.0 - sg)
        dz_ref[:, 0:cd] = da.astype(BF16)
        dz_ref[:, cd:2 * cd] = dg.astype(BF16)
        sum_ref[:, 0:cd] += jnp.sum(da, axis=0, keepdims=True)
        sum_ref[:, cd:2 * cd] += jnp.sum(dg, axis=0, keepdims=True)

    return _pcall(
        body, name="conv_bwd_taps", grid=(nb,),
        in_specs=[_row_spec(cd), pl.BlockSpec((BLOCK, cd), fwd),
                  pl.BlockSpec((BLOCK, cd), back(0)), pl.BlockSpec((BLOCK, cd), back(1)), _row_spec(cd, 0), _row_spec(cd, 1),
                  _const_spec((CONV_ROWS, cd))],
        out_specs=[_row_spec(2 * cd), _const_spec((1, 2 * cd)), _const_spec((CONV_ROWS, cd))],
        out_shape=[jax.ShapeDtypeStruct((lp, 2 * cd), BF16), jax.ShapeDtypeStruct((1, 2 * cd), F32),
                   jax.ShapeDtypeStruct((CONV_ROWS, cd), F32)],
        scratch_shapes=[pltpu.VMEM((2 * BLOCK, cd), F32), pltpu.VMEM((2 * BLOCK, cd), F32), pltpu.VMEM((BLOCK, cd), F32)],
        compiler_params=_params(("arbitrary",)),
    )(dco, dco, zc, zc, zc, zc, conv_w)


def _gate_fwd(a, b, zg, after=None):
    lp, d = a.shape

    def body(a_ref, b_ref, ga_ref, gb_ref, m_ref):
        m_ref[...] = (_sigmoid(ga_ref[...]) * a_ref[...] + _sigmoid(gb_ref[...]) * b_ref[...]).astype(BF16)

    return _pcall(
        body, after=after, name="gate_fwd", grid=(lp // BLOCK,),
        in_specs=[_row_spec(d), _row_spec(d), _row_spec(d, 0), _row_spec(d, 1)], out_specs=_row_spec(d),
        out_shape=jax.ShapeDtypeStruct((lp, d), BF16), compiler_params=_params(("parallel",)),
    )(a, b, zg, zg)


def _gate_bwd(dm, a, b, zg):
    lp, d = a.shape

    def body(dm_ref, a_ref, b_ref, ga_ref, gb_ref, da_ref, db_ref, dz_ref, sum_ref, dbias_ref):
        i = pl.program_id(0)
        dm_ = dm_ref[...]
        sa = _sigmoid(ga_ref[...])
        sb = _sigmoid(gb_ref[...])
        db = dm_ * sb
        dga = dm_ * a_ref[...] * sa * (1.0 - sa)
        dgb = dm_ * b_ref[...] * sb * (1.0 - sb)
        da_ref[...] = (dm_ * sa).astype(BF16)
        db_ref[...] = db.astype(BF16)
        dz_ref[:, 0:d] = dga.astype(BF16)
        dz_ref[:, d:2 * d] = dgb.astype(BF16)

        @pl.when(i == 0)
        def _():
            sum_ref[...] = jnp.zeros_like(sum_ref)
            dbias_ref[...] = jnp.zeros_like(dbias_ref)

        sum_ref[:, 0:d] += jnp.sum(dga, axis=0, keepdims=True)
        sum_ref[:, d:2 * d] += jnp.sum(dgb, axis=0, keepdims=True)
        dbias_ref[...] += jnp.sum(db, axis=0, keepdims=True)

    return _pcall(
        body, name="gate_bwd", grid=(lp // BLOCK,),
        in_specs=[_row_spec(d), _row_spec(d), _row_spec(d), _row_spec(d, 0), _row_spec(d, 1)],
        out_specs=[_row_spec(d), _row_spec(d), _row_spec(2 * d), _const_spec((1, 2 * d)), _const_spec((1, d))],
        out_shape=[jax.ShapeDtypeStruct((lp, d), BF16), jax.ShapeDtypeStruct((lp, d), BF16), jax.ShapeDtypeStruct((lp, 2 * d), BF16),
                   jax.ShapeDtypeStruct((1, 2 * d), F32), jax.ShapeDtypeStruct((1, d), F32)],
        compiler_params=_params(("arbitrary",)),
    )(dm, a, b, zg, zg)


def _swiglu_fwd(gu, after=None):
    lp = gu.shape[0]
    f = gu.shape[1] // 2

    def body(g_ref, u_ref, o_ref):
        g = g_ref[...]
        o_ref[...] = (g * _sigmoid(g) * u_ref[...]).astype(BF16)

    return _pcall(
        body, after=after, name="swiglu_fwd", grid=(lp // BLOCK,), in_specs=[_row_spec(f, 0), _row_spec(f, 1)], out_specs=_row_spec(f),
        out_shape=jax.ShapeDtypeStruct((lp, f), BF16), compiler_params=_params(("parallel",)),
    )(gu, gu)


def _swiglu_bwd(dact, gu):
    lp, f = dact.shape

    def body(d_ref, g_ref, u_ref, o_ref):
        g = g_ref[...]
        d = d_ref[...]
        sg = _sigmoid(g)
        o_ref[:, 0:f] = (d * u_ref[...] * (sg * (1.0 + g * (1.0 - sg)))).astype(BF16)
        o_ref[:, f:2 * f] = (d * g * sg).astype(BF16)

    return _pcall(
        body, name="swiglu_bwd", grid=(lp // BLOCK,), in_specs=[_row_spec(f), _row_spec(f, 0), _row_spec(f, 1)],
        out_specs=_row_spec(2 * f), out_shape=jax.ShapeDtypeStruct((lp, 2 * f), BF16), compiler_params=_params(("parallel",)),
    )(dact, gu, gu)


ANY = pl.BlockSpec(memory_space=pl.ANY)


def _all_gather_rows(x, name, after=None):
    r, c = x.shape

    def body(x_ref, out_ref, send_sems, recv_sems, local_sem):
        mx, my, mc = lax.axis_index("x"), lax.axis_index("y"), lax.axis_index("c")
        me, sibling = (mx, my, mc), (mx, my, 1 - mc)
        chips = [(1 - mx, my), (mx, 1 - my), (1 - mx, 1 - my)]

        def rows(px, py, pc):
            return out_ref.at[pl.ds((4 * px + 2 * py + pc) * r, r), :]

        def copy(k, block, to, src=None):
            return pltpu.make_async_remote_copy(
                src_ref=rows(*block) if src is None else src, dst_ref=rows(*block),
                send_sem=send_sems.at[k], recv_sem=recv_sems.at[k], device_id=to, device_id_type=MESH)

        mine = pltpu.make_async_copy(x_ref, rows(*me), local_sem)
        mine.start()
        first = [copy(0, me, sibling, src=x_ref)]
        first += [copy(1 + j, me, (*chip, mc), src=x_ref) for j, chip in enumerate(chips)]
        for cp in first:
            cp.start()
        passed = [copy(4 + j, (*chip, mc), sibling) for j, chip in enumerate(chips)]
        for j, chip in enumerate(chips):
            copy(1 + j, (*chip, mc), me).wait_recv()
            passed[j].start()
        copy(0, sibling, me).wait_recv()
        for j, chip in enumerate(chips):
            copy(4 + j, (*chip, 1 - mc), me).wait_recv()
        for cp in first + passed:
            cp.wait_send()
        mine.wait()

    return _pcall(
        body, after=after, name=name, in_specs=[ANY], out_specs=ANY, out_shape=jax.ShapeDtypeStruct((N_DEV * r, c), x.dtype),
        scratch_shapes=[pltpu.SemaphoreType.DMA((7,)), pltpu.SemaphoreType.DMA((7,)), pltpu.SemaphoreType.DMA(())],
    )(x)


HBM = pl.BlockSpec(memory_space=pltpu.HBM)
SEM = pl.BlockSpec(memory_space=pltpu.SEMAPHORE)
IN_FLIGHT = pltpu.CompilerParams(has_side_effects=pltpu.SideEffectType.DATAFLOW_SIDE_EFFECTING)
N_PEERS = 4


def _place_rows(shard, after, name):
    r, c = shard.shape
    tr = _pick(r, max(16, ELEMENTWISE_BLOCK_BYTES // (4 * c)), 16)
    steps = r // tr
    dev = (4 * lax.axis_index("x") + 2 * lax.axis_index("y") + lax.axis_index("c")).astype(jnp.int32).reshape(1)

    def body(dev_ref, x_ref, after_ref, o_ref):
        o_ref[...] = x_ref[...].astype(BF16)

    return _pcall(
        body, name=name,
        grid_spec=pltpu.PrefetchScalarGridSpec(
            num_scalar_prefetch=1, grid=(steps,),
            in_specs=[pl.BlockSpec((tr, c), lambda i, dev_ref: (i, 0)), pl.BlockSpec(memory_space=pl.ANY)],
            out_specs=pl.BlockSpec((tr, c), lambda i, dev_ref: (dev_ref[0] * steps + i, 0))),
        out_shape=jax.ShapeDtypeStruct((N_DEV * r, c), BF16), compiler_params=_params(("parallel",)),
    )(dev, shard, after)


def _gather_start(full, name):
    r = full.shape[0] // N_DEV

    def body(full_ref, send_sems, recv_sems, out_ref):
        mx, my, mc = lax.axis_index("x"), lax.axis_index("y"), lax.axis_index("c")
        mine = full_ref.at[pl.ds((4 * mx + 2 * my + mc) * r, r), :]
        for k, peer in enumerate([(mx, my, 1 - mc), (1 - mx, my, mc), (mx, 1 - my, mc), (1 - mx, 1 - my, mc)]):
            pltpu.make_async_remote_copy(
                src_ref=mine, dst_ref=mine, send_sem=send_sems.at[k], recv_sem=recv_sems.at[k],
                device_id=peer, device_id_type=MESH).start()

    return pl.pallas_call(
        body, name=name, in_specs=[HBM], out_specs=(SEM, SEM, HBM),
        out_shape=(pltpu.SemaphoreType.DMA((N_PEERS,)), pltpu.SemaphoreType.DMA((N_PEERS,)), pltpu.HBM(full.shape, full.dtype)),
        input_output_aliases={0: 2}, compiler_params=IN_FLIGHT,
    )(pltpu.with_memory_space_constraint(full, pltpu.HBM))


def _gather_wait(full, send_sem, recv_sem, after, name):
    r = full.shape[0] // N_DEV

    def body(full_ref, send_ref, recv_ref, after_ref, out_ref):
        mx, my, mc = lax.axis_index("x"), lax.axis_index("y"), lax.axis_index("c")
        block = full_ref.at[pl.ds(0, r), :]
        for k in range(N_PEERS):
            cp = pltpu.make_async_remote_copy(
                src_ref=block, dst_ref=block, send_sem=send_ref.at[k], recv_sem=recv_ref.at[k],
                device_id=(mx, my, mc), device_id_type=MESH)
            cp.wait_send()
            cp.wait_recv()

    return pl.pallas_call(
        body, name=name, in_specs=[HBM, SEM, SEM, pl.BlockSpec(memory_space=pl.ANY)], out_specs=HBM,
        out_shape=pltpu.HBM(full.shape, full.dtype), input_output_aliases={0: 0}, compiler_params=IN_FLIGHT,
    )(full, send_sem, recv_sem, after)


def _gather_forward_start(full, name):
    r = full.shape[0] // N_DEV

    def body(full_ref, send_sems, recv_sems, out_ref):
        mx, my, mc = lax.axis_index("x"), lax.axis_index("y"), lax.axis_index("c")
        for k, (px, py) in enumerate([(1 - mx, my), (mx, 1 - my), (1 - mx, 1 - my)]):
            rows = full_ref.at[pl.ds((4 * px + 2 * py + mc) * r, r), :]
            pltpu.make_async_remote_copy(
                src_ref=rows, dst_ref=rows, send_sem=send_sems.at[k], recv_sem=recv_sems.at[k],
                device_id=(mx, my, 1 - mc), device_id_type=MESH).start()

    return pl.pallas_call(
        body, name=name, in_specs=[HBM], out_specs=(SEM, SEM, HBM),
        out_shape=(pltpu.SemaphoreType.DMA((3,)), pltpu.SemaphoreType.DMA((3,)), pltpu.HBM(full.shape, full.dtype)),
        input_output_aliases={0: 2}, compiler_params=IN_FLIGHT,
    )(full)


def _gather_forward_wait(full, send_sem, recv_sem, after, name):
    r = full.shape[0] // N_DEV

    def body(full_ref, send_ref, recv_ref, after_ref, out_ref):
        mx, my, mc = lax.axis_index("x"), lax.axis_index("y"), lax.axis_index("c")
        block = full_ref.at[pl.ds(0, r), :]
        for k in range(3):
            cp = pltpu.make_async_remote_copy(
                src_ref=block, dst_ref=block, send_sem=send_ref.at[k], recv_sem=recv_ref.at[k],
                device_id=(mx, my, mc), device_id_type=MESH)
            cp.wait_send()
            cp.wait_recv()

    return pl.pallas_call(
        body, name=name, in_specs=[HBM, SEM, SEM, pl.BlockSpec(memory_space=pl.ANY)], out_specs=HBM,
        out_shape=pltpu.HBM(full.shape, full.dtype), input_output_aliases={0: 0}, compiler_params=IN_FLIGHT,
    )(full, send_sem, recv_sem, after)


def _pair_exchange_start(g, name):
    r = g.shape[0] // N_DEV
    c = g.shape[1]
    land = (len(CHIPS), r, c)

    def body(g_ref, land_ref, send_sems, recv_sems, g_out, land_out):
        mx, my, mc = lax.axis_index("x"), lax.axis_index("y"), lax.axis_index("c")
        for j, (px, py) in enumerate(CHIPS):
            pltpu.make_async_remote_copy(
                src_ref=g_ref.at[pl.ds((4 * px + 2 * py + 1 - mc) * r, r), :], dst_ref=land_ref.at[j],
                send_sem=send_sems.at[j], recv_sem=recv_sems.at[j], device_id=(mx, my, 1 - mc), device_id_type=MESH).start()

    return pl.pallas_call(
        body, name=name, in_specs=[HBM, HBM], out_specs=(SEM, SEM, HBM, HBM),
        out_shape=(pltpu.SemaphoreType.DMA((4,)), pltpu.SemaphoreType.DMA((4,)), pltpu.HBM(g.shape, g.dtype), pltpu.HBM(land, g.dtype)),
        input_output_aliases={0: 2, 1: 3}, compiler_params=IN_FLIGHT,
    )(pltpu.with_memory_space_constraint(g, pltpu.HBM), pltpu.with_memory_space_constraint(lax.empty(land, g.dtype), pltpu.HBM))


def _pair_exchange_wait(send_sem, recv_sem, g, land, after, name):
    def body(g_ref, land_ref, send_ref, recv_ref, after_ref, g_out, land_out):
        mx, my, mc = lax.axis_index("x"), lax.axis_index("y"), lax.axis_index("c")
        for j in range(len(CHIPS)):
            cp = pltpu.make_async_remote_copy(
                src_ref=land_ref.at[0], dst_ref=land_ref.at[0], send_sem=send_ref.at[j], recv_sem=recv_ref.at[j],
                device_id=(mx, my, mc), device_id_type=MESH)
            cp.wait_send()
            cp.wait_recv()

    return pl.pallas_call(
        body, name=name, in_specs=[HBM, HBM, SEM, SEM, pl.BlockSpec(memory_space=pl.ANY)], out_specs=(HBM, HBM),
        out_shape=(pltpu.HBM(g.shape, g.dtype), pltpu.HBM(land.shape, land.dtype)), input_output_aliases={0: 0, 1: 1},
        compiler_params=IN_FLIGHT,
    )(g, land, send_sem, recv_sem, after)


def _chip_exchange_start(ps, after, name):
    def body(ps_ref, rx_ref, after_ref, send_sems, recv_sems, ps_out, rx_out):
        mx, my, mc = lax.axis_index("x"), lax.axis_index("y"), lax.axis_index("c")
        chips = [(1 - mx, my), (mx, 1 - my), (1 - mx, 1 - my)]
        for k, (px, py) in enumerate(chips):
            pltpu.make_async_remote_copy(
                src_ref=ps_ref.at[2 * px + py], dst_ref=rx_ref.at[2 * mx + my], send_sem=send_sems.at[k], recv_sem=recv_sems.at[k],
                device_id=(px, py, mc), device_id_type=MESH).start()

    return pl.pallas_call(
        body, name=name, in_specs=[HBM, HBM, pl.BlockSpec(memory_space=pl.ANY)], out_specs=(SEM, SEM, HBM, HBM),
        out_shape=(pltpu.SemaphoreType.DMA((3,)), pltpu.SemaphoreType.DMA((3,)), pltpu.HBM(ps.shape, ps.dtype), pltpu.HBM(ps.shape, ps.dtype)),
        input_output_aliases={0: 2, 1: 3}, compiler_params=IN_FLIGHT,
    )(pltpu.with_memory_space_constraint(ps, pltpu.HBM), pltpu.with_memory_space_constraint(lax.empty(ps.shape, ps.dtype), pltpu.HBM), after)


def _chip_exchange_wait(send_sem, recv_sem, ps, rx, after, name):
    def body(ps_ref, rx_ref, send_ref, recv_ref, after_ref, ps_out, rx_out):
        mx, my, mc = lax.axis_index("x"), lax.axis_index("y"), lax.axis_index("c")
        for k in range(3):
            cp = pltpu.make_async_remote_copy(
                src_ref=ps_ref.at[0], dst_ref=rx_ref.at[0], send_sem=send_ref.at[k], recv_sem=recv_ref.at[k],
                device_id=(mx, my, mc), device_id_type=MESH)
            cp.wait_send()
            cp.wait_recv()

    return pl.pallas_call(
        body, name=name, in_specs=[HBM, HBM, SEM, SEM, pl.BlockSpec(memory_space=pl.ANY)], out_specs=(HBM, HBM),
        out_shape=(pltpu.HBM(ps.shape, ps.dtype), pltpu.HBM(rx.shape, rx.dtype)), input_output_aliases={0: 0, 1: 1},
        compiler_params=IN_FLIGHT,
    )(ps, rx, send_sem, recv_sem, after)


def _sum_chips(ps, rx, name):
    n, r, c = rx.shape
    tr = _pick(r, max(8, ELEMENTWISE_BLOCK_BYTES // (4 * n * c)), 8)
    chip = (2 * lax.axis_index("x") + lax.axis_index("y")).astype(jnp.int32).reshape(1)

    def body(chip_ref, own_ref, x_ref, o_ref):
        me = chip_ref[0]
        own = own_ref[0].astype(F32)
        acc = jnp.where(me == 0, own, x_ref[0].astype(F32))
        for j in range(1, n):
            acc = acc + jnp.where(me == j, own, x_ref[j].astype(F32))
        o_ref[...] = acc

    return _pcall(
        body, name=name,
        grid_spec=pltpu.PrefetchScalarGridSpec(
            num_scalar_prefetch=1, grid=(r // tr,),
            in_specs=[pl.BlockSpec((1, tr, c), lambda i, chip_ref: (chip_ref[0], i, 0)), pl.BlockSpec((n, tr, c), lambda i, chip_ref: (0, i, 0))],
            out_specs=pl.BlockSpec((tr, c), lambda i, chip_ref: (i, 0))),
        out_shape=jax.ShapeDtypeStruct((r, c), F32), compiler_params=_params(("parallel",)),
    )(chip, ps, rx)


def _pair_exchange(g, name):
    r = g.shape[0] // N_DEV
    c = g.shape[1]

    def body(g_ref, theirs_ref, send_sems, recv_sems):
        mx, my, mc = lax.axis_index("x"), lax.axis_index("y"), lax.axis_index("c")
        sibling = (mx, my, 1 - mc)
        copies = []
        for j, (px, py) in enumerate(CHIPS):
            give = g_ref.at[pl.ds((4 * px + 2 * py + 1 - mc) * r, r), :]
            rc = pltpu.make_async_remote_copy(
                src_ref=give, dst_ref=theirs_ref.at[j], send_sem=send_sems.at[j], recv_sem=recv_sems.at[j],
                device_id=sibling, device_id_type=MESH)
            rc.start()
            copies.append(rc)
        for cp in copies:
            cp.wait()

    return _pcall(
        body, name=name, in_specs=[ANY], out_specs=ANY, out_shape=jax.ShapeDtypeStruct((len(CHIPS), r, c), g.dtype),
        scratch_shapes=[pltpu.SemaphoreType.DMA((4,)), pltpu.SemaphoreType.DMA((4,))],
    )(g)


def _pair_sum(g, theirs, name):
    nch, r, c = theirs.shape
    tr = _pick(r, max(16, ELEMENTWISE_BLOCK_BYTES // (2 * c)), 16)
    core = lax.axis_index("c").astype(jnp.int32).reshape(1)

    def body(core_ref, a_ref, b_ref, o_ref):
        o_ref[...] = (a_ref[...].astype(F32) + b_ref[...].astype(F32)).astype(o_ref.dtype)

    spec = pl.BlockSpec((1, tr, c), lambda j, i, core_ref: (j, i, 0))
    own = pl.BlockSpec((1, tr, c), lambda j, i, core_ref: (2 * j + core_ref[0], i, 0))
    return _pcall(
        body, name=name,
        grid_spec=pltpu.PrefetchScalarGridSpec(num_scalar_prefetch=1, grid=(nch, r // tr), in_specs=[own, spec], out_specs=spec),
        out_shape=jax.ShapeDtypeStruct(theirs.shape, theirs.dtype), compiler_params=_params(("parallel", "parallel")),
    )(core, g.reshape(N_DEV, r, c), theirs)


def _sum_blocks(rx, name):
    n, r, c = rx.shape
    tr = _pick(r, max(8, ELEMENTWISE_BLOCK_BYTES // (4 * n * c)), 8)

    def body(x_ref, o_ref):
        acc = x_ref[0].astype(F32)
        for j in range(1, n):
            acc = acc + x_ref[j].astype(F32)
        o_ref[...] = acc

    return _pcall(
        body, name=name, grid=(r // tr,), in_specs=[pl.BlockSpec((n, tr, c), lambda i: (0, i, 0))],
        out_specs=pl.BlockSpec((tr, c), lambda i: (i, 0)), out_shape=jax.ShapeDtypeStruct((r, c), F32),
        compiler_params=_params(("parallel",)),
    )(rx)


def _adamw(w, g, m, v, name):
    r, c = w.shape
    tr = _pick(r, max(8, ELEMENTWISE_BLOCK_BYTES // (4 * c)), 8)
    c1 = 1.0 - ADAM_B1 ** ADAM_STEP
    c2 = 1.0 - ADAM_B2 ** ADAM_STEP

    def body(w_ref, g_ref, m_ref, v_ref, d_ref, nm_ref, nv_ref):
        gg = g_ref[...]
        nm = ADAM_B1 * m_ref[...] + (1.0 - ADAM_B1) * gg
        nv = ADAM_B2 * v_ref[...] + (1.0 - ADAM_B2) * (gg * gg)
        d_ref[...] = -ADAM_LR * ((nm / c1) / (jnp.sqrt(nv / c2) + ADAM_EPS) + ADAM_WD * w_ref[...])
        nm_ref[...] = nm
        nv_ref[...] = nv

    spec = pl.BlockSpec((tr, c), lambda i: (i, 0))
    shp = jax.ShapeDtypeStruct((r, c), F32)
    return _pcall(
        body, name=name, grid=(r // tr,), in_specs=[spec] * 4, out_specs=[spec] * 3, out_shape=[shp] * 3,
        compiler_params=_params(("parallel",)),
    )(w, g, m, v)


def _pack(parts):
    flat, layout, row = [], [], 0
    for p in parts:
        n = p.size
        rows = -(-n // LANES)
        flat.append(jnp.pad(p.reshape(-1).astype(F32), (0, rows * LANES - n)))
        layout.append((row, n, p.shape))
        row += rows
    total = -(-row // 8) * 8
    if total > row:
        flat.append(jnp.zeros(((total - row) * LANES,), F32))
    return jnp.concatenate(flat).reshape(total, LANES), layout


def _unpack(slab, layout):
    flat = slab.reshape(-1)
    return [flat[row * LANES:row * LANES + n].reshape(shape) for row, n, shape in layout]


def kernel(x, meta_tokens, mix_norm_g, w_in, b_in, attn_sinks, conv_w, conv_b, conv_ln_g, conv_ln_b, w_attn_o, w_conv_o, b_conv_o, w_out, ffn_norm_g, w_gate_up, w_down, final_norm_g, loss_target, m_meta_tokens, m_mix_norm_g, m_w_in, m_b_in, m_attn_sinks, m_conv_w, m_conv_b, m_conv_ln_g, m_conv_ln_b, m_w_attn_o, m_w_conv_o, m_b_conv_o, m_w_out, m_ffn_norm_g, m_w_gate_up, m_w_down, m_final_norm_g, v_meta_tokens, v_mix_norm_g, v_w_in, v_b_in, v_attn_sinks, v_conv_w, v_conv_b, v_conv_ln_g, v_conv_ln_b, v_w_attn_o, v_w_conv_o, v_b_conv_o, v_w_out, v_ffn_norm_g, v_w_gate_up, v_w_down, v_final_norm_g):
    xs = x[0]
    tgt = loss_target[0]
    s, d = xs.shape
    lp = s + BLOCK
    cd = conv_b.shape[1]
    ffn = w_down.shape[1] * N_DEV
    dev = 4 * lax.axis_index("x") + 2 * lax.axis_index("y") + lax.axis_index("c")
    cw_cols = conv_w.shape[3]
    meta_cols = meta_tokens.shape[1]

    small, small_layout = _pack([meta_tokens, jnp.pad(conv_w[0, :, 0, :], ((0, CONV_ROWS - CONV_WIDTH), (0, 0)))])
    small_flat = _all_gather_rows(small, "gather_small")
    small_all = small_flat.reshape(N_DEV, *small.shape)
    meta_parts, cw_parts = zip(*[_unpack(small_all[j], small_layout) for j in range(N_DEV)])
    meta_full = jnp.concatenate(meta_parts, axis=1)
    conv_w_full = jnp.concatenate(cw_parts, axis=1)
    g_send, g_recv, g_full = [], [], []
    tok = small_flat
    for shard, name in ((w_in[0].T, "w_in"), (w_attn_o[0].T, "w_attn_o"), (w_conv_o[0].T, "w_conv_o"), (w_out[0], "w_out"),
                        (w_gate_up[0].T, "w_gate_up"), (w_down[0], "w_down")):
        send_sem, recv_sem, tok = _gather_start(_place_rows(shard, tok, "place_" + name), "gather_start_" + name)
        g_send.append(send_sem)
        g_recv.append(recv_sem)
        g_full.append(tok)

    def arrived(w, after, name):
        full = _gather_wait(g_full[w], g_send[w], g_recv[w], after, "gather_wait_" + name)
        return _gather_forward_start(full, "gather_forward_start_" + name)

    def whole(passing, after, name):
        return _gather_forward_wait(passing[2], passing[0], passing[1], after, "gather_forward_wait_" + name)

    ctab, stab = _rope_tables(lp)
    mm = functools.partial(_matmul, tm=1056, tn=1024)

    passing = arrived(0, tok, "w_in")
    h0, u = _prep(xs, meta_full, mix_norm_g, after=passing[2])
    win_t = whole(passing, u, "w_in")
    bq, bkv, bc, bg = b_in[:, :Q_DIM], b_in[:, Q_DIM:Q_DIM + 2 * KV_DIM], b_in[:, Q_DIM + 2 * KV_DIM:Q_DIM + 2 * KV_DIM + 2 * cd], b_in[:, Q_DIM + 2 * KV_DIM + 2 * cd:]
    o_kv, o_c, o_g = Q_DIM, Q_DIM + 2 * KV_DIM, Q_DIM + 2 * KV_DIM + 2 * cd
    zq = mm(u, win_t, mode="nt", name="in_proj_q", out_dtype=F32, tk=d, bias=bq, b_row_off=0, b_rows=Q_DIM)
    zkv = mm(u, win_t, mode="nt", name="in_proj_kv", out_dtype=F32, tk=d, bias=bkv, b_row_off=o_kv, b_rows=2 * KV_DIM)
    zc = mm(u, win_t, mode="nt", name="in_proj_conv", out_dtype=F32, tk=d, bias=bc, b_row_off=o_c, b_rows=2 * cd)
    zg = mm(u, win_t, mode="nt", name="in_proj_gates", out_dtype=F32, tk=d, bias=bg, b_row_off=o_g, b_rows=2 * d)
    passing = arrived(1, zg, "w_attn_o")
    q_rot, k_sh, v_sh = _rope_fwd(zq, zkv, ctab, stab, after=passing[2])
    o = _attn_fwd(q_rot, k_sh, v_sh, attn_sinks)
    wao_t = whole(passing, o, "w_attn_o")
    br_a = mm(o, wao_t, mode="nt", name="attn_out_proj", out_dtype=F32, tk=Q_DIM)
    passing = arrived(2, br_a, "w_conv_o")
    conv_out, c2 = _conv_fwd(zc, conv_w_full, conv_b, conv_ln_g, conv_ln_b, after=passing[2])
    wco_t = whole(passing, c2, "w_conv_o")
    br_b = mm(c2, wco_t, mode="nt", name="conv_out_proj", out_dtype=F32, tk=cd, bias=b_conv_o)
    passing = arrived(3, br_b, "w_out")
    merged = _gate_fwd(br_a, br_b, zg, after=passing[2])
    wout = whole(passing, merged, "w_out")
    passing = arrived(4, wout, "w_gate_up")
    h1 = mm(merged, wout, mode="nn", name="mix_out_proj", out_dtype=F32, tn=512, tk=d, residual=h0, after=passing[2])
    u2 = _rmsnorm_fwd(h1, ffn_norm_g, "ffn_rmsnorm")
    wgu_t = whole(passing, u2, "w_gate_up")
    gu = _matmul(u2, wgu_t, mode="nt", name="ffn_gate_up", out_dtype=F32, tm=1056, tn=512, tk=d)
    passing = arrived(5, gu, "w_down")
    act = _swiglu_fwd(gu, after=passing[2])
    wdown = whole(passing, act, "w_down")
    h2 = mm(act, wdown, mode="nn", name="ffn_down", out_dtype=F32, tn=512, tk=ffn // 2, residual=h1)
    dh2, dh2_b, loss_part, d_final_g = _final(h2, tgt, final_norm_g.reshape(1, d))

    wgrad = functools.partial(_matmul, mode="tn", out_dtype=BF16, tk=lp, tn=2048, b_inner=False)
    in_flight = {}

    def scatter_begin(g, name):
        return _pair_exchange_start(g, "rs_" + name + "_pair_start")

    def scatter_go_on(pair, after, name):
        g, theirs = _pair_exchange_wait(pair[0], pair[1], pair[2], pair[3], after, "rs_" + name + "_pair_wait")
        ps = _pair_sum(g, theirs, "rs_" + name + "_pair_sum")
        in_flight[name] = _chip_exchange_start(ps, theirs, "rs_" + name + "_chip_start")
        return in_flight[name][2]

    g_wdown = wgrad(act, dh2_b, name="ffn_down_dw", tm=256)
    pair = scatter_begin(g_wdown, "w_down")
    dact = _matmul(dh2_b, wdown, mode="nt", name="ffn_down_dx", out_dtype=F32, tm=2112, tn=256, tk=d, after=pair[2])
    tok = scatter_go_on(pair, dact, "w_down")
    dgu = _swiglu_bwd(dact, gu)
    g_wgu_t = wgrad(dgu, u2, name="ffn_gate_up_dw", tm=512, after=tok)
    pair = scatter_begin(g_wgu_t, "w_gate_up")
    du2 = mm(dgu, wgu_t, mode="nn", name="ffn_gate_up_dx", out_dtype=F32, tn=512, tk=ffn // 2, after=pair[2])
    tok = scatter_go_on(pair, du2, "w_gate_up")
    dh1, dh1_b, d_ffn_g = _rmsnorm_bwd(du2, h1, ffn_norm_g, dh2, "ffn_rmsnorm_bwd")
    g_wout = wgrad(merged, dh1_b, name="mix_out_dw", tm=512, after=tok)
    pair = scatter_begin(g_wout, "w_out")
    dmerged = mm(dh1_b, wout, mode="nt", name="mix_out_dx", out_dtype=F32, tk=d, after=pair[2])
    tok = scatter_go_on(pair, dmerged, "w_out")
    d_a, d_b, dz_g, sum_g, d_bco = _gate_bwd(dmerged, br_a, br_b, zg)
    g_wao_t = wgrad(d_a, o, name="attn_out_dw", tm=512, after=tok)
    pair = scatter_begin(g_wao_t, "w_attn_o")
    do = mm(d_a, wao_t, mode="nn", name="attn_out_dx", out_dtype=BF16, tk=d, after=pair[2])
    tok = scatter_go_on(pair, do, "w_attn_o")
    g_wco_t = wgrad(d_b, c2, name="conv_out_dw", tm=512, after=tok)
    pair = scatter_begin(g_wco_t, "w_conv_o")
    dc2 = mm(d_b, wco_t, mode="nn", name="conv_out_dx", out_dtype=F32, tk=d, after=pair[2])
    tok = scatter_go_on(pair, dc2, "w_conv_o")
    dq, dk, dv, dkm, dvm, d_sinks = _attn_bwd(q_rot, k_sh, v_sh, attn_sinks, do)
    dz_qkv, sum_qkv = _rope_bwd(dq, dk, dv, dkm, dvm, ctab, stab)
    dco, d_ln_g, d_ln_b, d_conv_b = _conv_bwd_norm(dc2, conv_out, conv_ln_g, conv_ln_b)
    dz_c, sum_c, d_conv_w = _conv_bwd_taps(dco, zc, conv_w_full)
    dz = jnp.concatenate([dz_qkv, dz_c, dz_g], axis=1)
    d_b_in = jnp.concatenate([sum_qkv, sum_c, sum_g], axis=1)
    in_dim = dz.shape[1]
    g_win_t = wgrad(dz, u, name="in_proj_dw", tm=512, after=tok)
    theirs = _pair_exchange(g_win_t, "rs_w_in_pair_exchange")
    in_flight["w_in"] = _chip_exchange_start(_pair_sum(g_win_t, theirs, "rs_w_in_pair_sum"), theirs, "rs_w_in_chip_start")
    du = mm(dz, win_t, mode="nn", name="in_proj_dx", out_dtype=F32, tk=in_dim // 4, after=in_flight["w_in"][2])
    grad_x, d_meta, d_mix_g = _rmsnorm_bwd_first(du, h0, mix_norm_g, dh1)

    weights = dict(meta_tokens=meta_tokens, mix_norm_g=mix_norm_g, w_in=w_in, b_in=b_in, attn_sinks=attn_sinks, conv_w=conv_w,
                   conv_b=conv_b, conv_ln_g=conv_ln_g, conv_ln_b=conv_ln_b, w_attn_o=w_attn_o, w_conv_o=w_conv_o, b_conv_o=b_conv_o,
                   w_out=w_out, ffn_norm_g=ffn_norm_g, w_gate_up=w_gate_up, w_down=w_down, final_norm_g=final_norm_g)
    m_in = dict(meta_tokens=m_meta_tokens, mix_norm_g=m_mix_norm_g, w_in=m_w_in, b_in=m_b_in, attn_sinks=m_attn_sinks, conv_w=m_conv_w,
                conv_b=m_conv_b, conv_ln_g=m_conv_ln_g, conv_ln_b=m_conv_ln_b, w_attn_o=m_w_attn_o, w_conv_o=m_w_conv_o,
                b_conv_o=m_b_conv_o, w_out=m_w_out, ffn_norm_g=m_ffn_norm_g, w_gate_up=m_w_gate_up, w_down=m_w_down,
                final_norm_g=m_final_norm_g)
    v_in = dict(meta_tokens=v_meta_tokens, mix_norm_g=v_mix_norm_g, w_in=v_w_in, b_in=v_b_in, attn_sinks=v_attn_sinks, conv_w=v_conv_w,
                conv_b=v_conv_b, conv_ln_g=v_conv_ln_g, conv_ln_b=v_conv_ln_b, w_attn_o=v_w_attn_o, w_conv_o=v_w_conv_o,
                b_conv_o=v_b_conv_o, w_out=v_w_out, ffn_norm_g=v_ffn_norm_g, w_gate_up=v_w_gate_up, w_down=v_w_down,
                final_norm_g=v_final_norm_g)
    names = list(weights)
    grads, delta, new_m, new_v = {}, {}, {}, {}
    transposed = ("w_in", "w_attn_o", "w_conv_o", "w_gate_up")
    tok = grad_x
    for n in ("w_down", "w_gate_up", "w_out", "w_attn_o", "w_conv_o", "w_in"):
        send_sem, recv_sem, ps, rx = in_flight[n]
        ps, rx = _chip_exchange_wait(send_sem, recv_sem, ps, rx, tok, "rs_" + n + "_chip_wait")
        g = _sum_chips(ps, rx, "rs_" + n + "_sum")
        g = g.T if n in transposed else g
        shape = weights[n].shape
        dl, nm, nv = _adamw(weights[n].reshape(g.shape), g, m_in[n].reshape(g.shape), v_in[n].reshape(g.shape), "adamw_" + n)
        grads[n], delta[n], new_m[n], new_v[n] = g.reshape(shape), dl.reshape(shape), nm.reshape(shape), nv.reshape(shape)
        tok = dl

    slab, slab_layout = _pack([loss_part[:, :1], d_mix_g, d_b_in, d_sinks[:, :N_Q_HEADS], d_conv_b, d_ln_g, d_ln_b, d_bco,
                               d_ffn_g, d_final_g, d_conv_w, d_meta])
    slab_all = _all_gather_rows(slab, "gather_small_grads", after=tok).reshape(N_DEV, *slab.shape)
    (loss, g_mix_g, g_b_in, g_sinks, g_conv_b, g_ln_g, g_ln_b, g_bco, g_ffn_g, g_final_g, g_conv_w_full, g_meta_full
     ) = _unpack(_sum_blocks(slab_all, "sum_small_grads"), slab_layout)
    g_conv_w = lax.dynamic_slice(g_conv_w_full, (0, dev * cw_cols), (CONV_WIDTH, cw_cols)).reshape(conv_w.shape)
    g_meta = lax.dynamic_slice(g_meta_full, (0, dev * meta_cols), (N_META, meta_cols))
    g_final_g = g_final_g.reshape(final_norm_g.shape)
    grads.update(meta_tokens=g_meta, mix_norm_g=g_mix_g, b_in=g_b_in, attn_sinks=g_sinks, conv_w=g_conv_w, conv_b=g_conv_b,
                 conv_ln_g=g_ln_g, conv_ln_b=g_ln_b, b_conv_o=g_bco, ffn_norm_g=g_ffn_g, final_norm_g=g_final_g)
    rest = [n for n in names if n not in delta]
    w_slab, rest_layout = _pack([weights[n] for n in rest])
    g_slab, _ = _pack([grads[n] for n in rest])
    m_slab, _ = _pack([m_in[n] for n in rest])
    v_slab, _ = _pack([v_in[n] for n in rest])
    dl, nm, nv = _adamw(w_slab, g_slab, m_slab, v_slab, "adamw_small")
    for n, a, b, c in zip(rest, _unpack(dl, rest_layout), _unpack(nm, rest_layout), _unpack(nv, rest_layout)):
        delta[n], new_m[n], new_v[n] = a, b, c

    return (loss.reshape(()), grad_x[None], *[grads[n] for n in names], *[delta[n] for n in names],
            *[new_m[n] for n in names], *[new_v[n] for n in names])
```

```python
import functools
import math

import jax
import jax.numpy as jnp
from jax import lax
from jax.experimental import pallas as pl
from jax.experimental.pallas import tpu as pltpu

F32 = jnp.float32
BF16 = jnp.bfloat16

N_DEV = 8
BLOCK = 128
N_META = 16
PAD_ROWS = BLOCK - N_META
HEAD_DIM = 64
N_Q_HEADS = 32
N_KV_HEADS = 4
GROUP = N_Q_HEADS // N_KV_HEADS
Q_DIM = N_Q_HEADS * HEAD_DIM
KV_DIM = N_KV_HEADS * HEAD_DIM
WINDOW = 128
CONV_WIDTH = 31
CONV_ROWS = 32
ROPE_THETA = 10000.0
EPS = 1e-6
ATTN_SCALE = HEAD_DIM ** -0.5
NEG = -1e30

ADAM_LR = 0.001
ADAM_B1 = 0.9
ADAM_B2 = 0.999
ADAM_EPS = 1e-08
ADAM_WD = 0.01
ADAM_STEP = 10

VMEM_LIMIT_BYTES = 56 * 1024 * 1024
LANES = 128
ELEMENTWISE_BLOCK_BYTES = 2 * 1024 * 1024
MESH = pl.DeviceIdType.MESH
CHIPS = ((0, 0), (0, 1), (1, 0), (1, 1))


def _pcall(body, after=None, **kw):
    if after is None:
        return pl.pallas_call(body, **kw)
    in_specs = list(kw.pop("in_specs"))
    n_in = len(in_specs)

    def ordered_body(*refs):
        return body(*refs[:n_in], *refs[n_in + 1:])

    call = pl.pallas_call(ordered_body, in_specs=in_specs + [pl.BlockSpec(memory_space=pl.ANY)], **kw)
    return lambda *args: call(*args, after)


def _params(semantics=None):
    if semantics is None:
        return pltpu.CompilerParams(vmem_limit_bytes=VMEM_LIMIT_BYTES)
    return pltpu.CompilerParams(dimension_semantics=semantics, vmem_limit_bytes=VMEM_LIMIT_BYTES)


def _pick(dim, pref, align):
    best = None
    t = align
    while t <= min(dim, pref):
        if dim % t == 0:
            best = t
        t += align
    return dim if best is None else best


def _sigmoid(x):
    return 1.0 / (1.0 + jnp.exp(-x))


def _matmul(a, b, *, mode, name, out_dtype, tm, tn, tk, bias=None, residual=None, b_inner=True,
            b_row_off=0, b_rows=None, after=None):
    if mode == "nn":
        m, k = a.shape
        n = b.shape[1]
    elif mode == "nt":
        m, k = a.shape
        n = b.shape[0] if b_rows is None else b_rows
    else:
        k, m = a.shape
        n = b.shape[1]
    tm = _pick(m, tm, 16)
    tn = _pick(math.gcd(n, b_row_off) if mode == "nt" and b_row_off else n, tn, LANES)
    tk = _pick(k, tk, LANES if mode != "tn" else 16)
    nm, nn, nk = m // tm, n // tn, k // tk
    if mode == "nt":
        assert b_row_off % tn == 0
    off = b_row_off // tn if mode == "nt" else 0

    if b_inner:
        grid = (nm, nn, nk)
        ij = lambda g0, g1: (g0, g1)
    else:
        grid = (nn, nm, nk)
        ij = lambda g0, g1: (g1, g0)

    if mode == "tn":
        a_spec = pl.BlockSpec((tk, tm), lambda g0, g1, kk: (kk, ij(g0, g1)[0]))
    else:
        a_spec = pl.BlockSpec((tm, tk), lambda g0, g1, kk: (ij(g0, g1)[0], kk))
    if mode == "nt":
        b_spec = pl.BlockSpec((tn, tk), lambda g0, g1, kk: (ij(g0, g1)[1] + off, kk))
    else:
        b_spec = pl.BlockSpec((tk, tn), lambda g0, g1, kk: (kk, ij(g0, g1)[1]))
    o_spec = pl.BlockSpec((tm, tn), lambda g0, g1, kk: ij(g0, g1))
    in_specs = [a_spec, b_spec]
    args = [a, b]
    if bias is not None:
        in_specs.append(pl.BlockSpec((1, tn), lambda g0, g1, kk: (0, ij(g0, g1)[1])))
        args.append(bias)
    if residual is not None:
        in_specs.append(o_spec)
        args.append(residual)
    dims = {"nn": (((1,), (0,)), ((), ())), "nt": (((1,), (1,)), ((), ())), "tn": (((0,), (0,)), ((), ()))}[mode]
    has_bias, has_res = bias is not None, residual is not None

    def body(*refs):
        a_ref, b_ref = refs[0], refs[1]
        pos = 2
        bias_ref = res_ref = None
        if has_bias:
            bias_ref = refs[pos]
            pos += 1
        if has_res:
            res_ref = refs[pos]
            pos += 1
        o_ref = refs[pos]
        acc_ref = refs[pos + 1] if nk > 1 else None

        def finish(acc):
            if has_bias:
                acc = acc + bias_ref[...]
            if has_res:
                acc = acc + res_ref[...]
            o_ref[...] = acc.astype(out_dtype)

        p = lax.dot_general(a_ref[...], b_ref[...], dims, preferred_element_type=F32)
        if nk == 1:
            finish(p)
        else:
            kk = pl.program_id(2)

            @pl.when(kk == 0)
            def _():
                acc_ref[...] = p

            @pl.when(kk > 0)
            def _():
                acc_ref[...] += p

            @pl.when(kk == nk - 1)
            def _():
                finish(acc_ref[...])

    return _pcall(
        body, after=after, name=name, grid=grid, in_specs=in_specs, out_specs=o_spec,
        out_shape=jax.ShapeDtypeStruct((m, n), out_dtype),
        scratch_shapes=[pltpu.VMEM((tm, tn), F32)] if nk > 1 else [],
        compiler_params=_params(("parallel", "parallel", "arbitrary")),
    )(*args)


def _row_spec(width, col=0):
    return pl.BlockSpec((BLOCK, width), lambda i: (i, col))


def _const_spec(shape):
    nd = len(shape)
    return pl.BlockSpec(shape, lambda i: (0,) * nd)


def _prep(x, meta_full, g, after=None):
    s, d = x.shape
    lp = s + BLOCK
    nb = lp // BLOCK

    def body(x_ref, meta_ref, g_ref, h_ref, u_ref):
        i = pl.program_id(0)

        @pl.when(i == 0)
        def _():
            h_ref[0:PAD_ROWS, :] = jnp.zeros((PAD_ROWS, d), F32)
            h_ref[PAD_ROWS:BLOCK, :] = meta_ref[...]

        @pl.when(i > 0)
        def _():
            h_ref[...] = x_ref[...]

        h = h_ref[...]
        r = lax.rsqrt(jnp.mean(h * h, axis=-1, keepdims=True) + EPS)
        u_ref[...] = (h * r * g_ref[...]).astype(BF16)

    return _pcall(
        body, after=after, name="prep_rmsnorm", grid=(nb,),
        in_specs=[pl.BlockSpec((BLOCK, d), lambda i: (jnp.maximum(i - 1, 0), 0)), _const_spec((N_META, d)), _const_spec((1, d))],
        out_specs=[_row_spec(d), _row_spec(d)],
        out_shape=[jax.ShapeDtypeStruct((lp, d), F32), jax.ShapeDtypeStruct((lp, d), BF16)],
        compiler_params=_params(("arbitrary",)),
    )(x, meta_full, g)


def _rmsnorm_fwd(h, g, name):
    lp, d = h.shape

    def body(h_ref, g_ref, u_ref):
        x = h_ref[...]
        r = lax.rsqrt(jnp.mean(x * x, axis=-1, keepdims=True) + EPS)
        u_ref[...] = (x * r * g_ref[...]).astype(BF16)

    return _pcall(
        body, name=name, grid=(lp // BLOCK,), in_specs=[_row_spec(d), _const_spec((1, d))], out_specs=_row_spec(d),
        out_shape=jax.ShapeDtypeStruct((lp, d), BF16), compiler_params=_params(("parallel",)),
    )(h, g)


def _rms_bwd_core(dy, x, g):
    r = lax.rsqrt(jnp.mean(x * x, axis=-1, keepdims=True) + EPS)
    xhat = x * r
    dxhat = dy * g
    dx = r * (dxhat - xhat * jnp.mean(dxhat * xhat, axis=-1, keepdims=True))
    return dx, jnp.sum(dy * xhat, axis=0, keepdims=True)


def _rmsnorm_bwd(dy, h, g, dres, name):
    lp, d = h.shape

    def body(dy_ref, h_ref, g_ref, dres_ref, dh_ref, dhb_ref, dg_ref):
        i = pl.program_id(0)
        dx, dg = _rms_bwd_core(dy_ref[...], h_ref[...], g_ref[...])
        dh = dres_ref[...] + dx
        dh_ref[...] = dh
        dhb_ref[...] = dh.astype(BF16)

        @pl.when(i == 0)
        def _():
            dg_ref[...] = jnp.zeros_like(dg_ref)

        dg_ref[...] += dg

    return _pcall(
        body, name=name, grid=(lp // BLOCK,),
        in_specs=[_row_spec(d), _row_spec(d), _const_spec((1, d)), _row_spec(d)],
        out_specs=[_row_spec(d), _row_spec(d), _const_spec((1, d))],
        out_shape=[jax.ShapeDtypeStruct((lp, d), F32), jax.ShapeDtypeStruct((lp, d), BF16), jax.ShapeDtypeStruct((1, d), F32)],
        compiler_params=_params(("arbitrary",)),
    )(dy, h, g, dres)


def _rmsnorm_bwd_first(dy, h, g, dres):
    lp, d = h.shape
    s = lp - BLOCK

    def body(dy_ref, h_ref, g_ref, dres_ref, gx_ref, dmeta_ref, dg_ref):
        i = pl.program_id(0)
        dx, dg = _rms_bwd_core(dy_ref[...], h_ref[...], g_ref[...])
        dh = dres_ref[...] + dx
        gx_ref[...] = dh

        @pl.when(i == 0)
        def _():
            dmeta_ref[...] = dh[PAD_ROWS:BLOCK, :]
            dg_ref[...] = jnp.zeros_like(dg_ref)

        dg_ref[...] += dg

    return _pcall(
        body, name="rmsnorm_bwd_first", grid=(lp // BLOCK,),
        in_specs=[_row_spec(d), _row_spec(d), _const_spec((1, d)), _row_spec(d)],
        out_specs=[pl.BlockSpec((BLOCK, d), lambda i: (jnp.maximum(i - 1, 0), 0)), _const_spec((N_META, d)), _const_spec((1, d))],
        out_shape=[jax.ShapeDtypeStruct((s, d), F32), jax.ShapeDtypeStruct((N_META, d), F32), jax.ShapeDtypeStruct((1, d), F32)],
        compiler_params=_params(("arbitrary",)),
    )(dy, h, g, dres)


def _final(h2, tgt, g):
    lp, d = h2.shape

    def body(h_ref, t_ref, g_ref, dh_ref, dhb_ref, loss_ref, dg_ref):
        i = pl.program_id(0)
        x = h_ref[...]
        gg = g_ref[...]
        r = lax.rsqrt(jnp.mean(x * x, axis=-1, keepdims=True) + EPS)
        xhat = x * r
        y = xhat * gg
        live = (i > 0).astype(F32)
        err = (y - t_ref[...]) * live
        dy = err * (1.0 / d)
        dxhat = dy * gg
        dh = r * (dxhat - xhat * jnp.mean(dxhat * xhat, axis=-1, keepdims=True))
        dh_ref[...] = dh
        dhb_ref[...] = dh.astype(BF16)

        @pl.when(i == 0)
        def _():
            loss_ref[...] = jnp.zeros_like(loss_ref)
            dg_ref[...] = jnp.zeros_like(dg_ref)

        row_loss = jnp.mean(err * err, axis=-1, keepdims=True)
        loss_ref[...] += 0.5 * jnp.sum(row_loss, axis=0, keepdims=True)
        dg_ref[...] += jnp.sum(dy * xhat, axis=0, keepdims=True)

    return _pcall(
        body, name="final_norm_loss", grid=(lp // BLOCK,),
        in_specs=[_row_spec(d), pl.BlockSpec((BLOCK, d), lambda i: (jnp.maximum(i - 1, 0), 0)), _const_spec((1, d))],
        out_specs=[_row_spec(d), _row_spec(d), _const_spec((1, LANES)), _const_spec((1, d))],
        out_shape=[jax.ShapeDtypeStruct((lp, d), F32), jax.ShapeDtypeStruct((lp, d), BF16),
                   jax.ShapeDtypeStruct((1, LANES), F32), jax.ShapeDtypeStruct((1, d), F32)],
        compiler_params=_params(("arbitrary",)),
    )(h2, tgt, g)


def _swap_halves(x):
    w = x.shape[1]
    lane = lax.broadcasted_iota(jnp.int32, x.shape, 1)
    first = (lane & (HEAD_DIM - 1)) < (HEAD_DIM // 2)
    return jnp.where(first, pltpu.roll(x, w - HEAD_DIM // 2, 1), pltpu.roll(x, HEAD_DIM // 2, 1))


def _rope_tables(lp):
    pos = jnp.maximum(jnp.arange(lp, dtype=jnp.int32) - PAD_ROWS, 0).astype(F32)
    inv_freq = ROPE_THETA ** (-jnp.arange(0, HEAD_DIM, 2, dtype=F32) / HEAD_DIM)
    ang = pos[:, None] * inv_freq[None, :]
    c, s = jnp.cos(ang), jnp.sin(ang)
    reps = LANES // HEAD_DIM
    return jnp.tile(jnp.concatenate([c, c], axis=1), (1, reps)), jnp.tile(jnp.concatenate([-s, s], axis=1), (1, reps))


def _rope_fwd(zq, zkv, ctab, stab, after=None):
    lp = zq.shape[0]
    nb = lp // BLOCK
    back = lambda s: (jnp.maximum(s - 1, 0), 0)

    def body(zq_ref, zkv_ref, c_ref, s_ref, q_ref, k_ref, v_ref):
        step = pl.program_id(0)
        c128, s128 = c_ref[...], s_ref[...]

        def rope(x):
            reps = x.shape[1] // LANES
            return x * jnp.tile(c128, (1, reps)) + _swap_halves(x) * jnp.tile(s128, (1, reps))

        q_ref[...] = (rope(zq_ref[...].astype(F32)) * ATTN_SCALE).astype(BF16)
        kv = zkv_ref[...].astype(F32)
        k = rope(kv[:, :KV_DIM])
        v = kv[:, KV_DIM:]

        @pl.when(step == 0)
        def _():
            k_ref[...] = jnp.zeros_like(k_ref)
            v_ref[...] = jnp.zeros_like(v_ref)

        @pl.when(step > 0)
        def _():
            for h in range(N_KV_HEADS):
                k_ref[h] = k[:, h * HEAD_DIM:(h + 1) * HEAD_DIM].astype(BF16)
                v_ref[h] = v[:, h * HEAD_DIM:(h + 1) * HEAD_DIM].astype(BF16)

    kv_spec = pl.BlockSpec((N_KV_HEADS, BLOCK, HEAD_DIM), lambda s: (0, s, 0))
    return _pcall(
        body, after=after, name="rope_fwd", grid=(nb + 1,),
        in_specs=[pl.BlockSpec((BLOCK, Q_DIM), back), pl.BlockSpec((BLOCK, 2 * KV_DIM), back),
                  pl.BlockSpec((BLOCK, LANES), back), pl.BlockSpec((BLOCK, LANES), back)],
        out_specs=[pl.BlockSpec((BLOCK, Q_DIM), back), kv_spec, kv_spec],
        out_shape=[jax.ShapeDtypeStruct((lp, Q_DIM), BF16),
                   jax.ShapeDtypeStruct((N_KV_HEADS, lp + BLOCK, HEAD_DIM), BF16),
                   jax.ShapeDtypeStruct((N_KV_HEADS, lp + BLOCK, HEAD_DIM), BF16)],
        compiler_params=_params(("arbitrary",)),
    )(zq, zkv, ctab, stab)


def _rope_bwd(dq, dk, dv, dkm, dvm, ctab, stab):
    lp = dq.shape[0]
    width = Q_DIM + 2 * KV_DIM
    head_spec = pl.BlockSpec((N_KV_HEADS, BLOCK, HEAD_DIM), lambda i: (0, i, 0))
    meta_spec = _const_spec((N_KV_HEADS, BLOCK, HEAD_DIM))

    def body(dq_ref, dk_ref, dv_ref, dkm_ref, dvm_ref, c_ref, s_ref, dz_ref, sum_ref, kbuf, vbuf):
        i = pl.program_id(0)
        c128, s128 = c_ref[...], s_ref[...]
        first = (i == 0).astype(F32)

        def rope_t(x):
            reps = x.shape[1] // LANES
            return x * jnp.tile(c128, (1, reps)) + _swap_halves(x * jnp.tile(s128, (1, reps)))

        for h in range(N_KV_HEADS):
            kbuf[:, h * HEAD_DIM:(h + 1) * HEAD_DIM] = dk_ref[h] + first * dkm_ref[h]
            vbuf[:, h * HEAD_DIM:(h + 1) * HEAD_DIM] = dv_ref[h] + first * dvm_ref[h]
        dzq = rope_t(dq_ref[...] * ATTN_SCALE)
        dzk = rope_t(kbuf[...])
        dzv = vbuf[...]
        dz_ref[:, 0:Q_DIM] = dzq.astype(BF16)
        dz_ref[:, Q_DIM:Q_DIM + KV_DIM] = dzk.astype(BF16)
        dz_ref[:, Q_DIM + KV_DIM:width] = dzv.astype(BF16)

        @pl.when(i == 0)
        def _():
            sum_ref[...] = jnp.zeros_like(sum_ref)

        sum_ref[:, 0:Q_DIM] += jnp.sum(dzq, axis=0, keepdims=True)
        sum_ref[:, Q_DIM:Q_DIM + KV_DIM] += jnp.sum(dzk, axis=0, keepdims=True)
        sum_ref[:, Q_DIM + KV_DIM:width] += jnp.sum(dzv, axis=0, keepdims=True)

    return _pcall(
        body, name="rope_bwd", grid=(lp // BLOCK,),
        in_specs=[_row_spec(Q_DIM), head_spec, head_spec, meta_spec, meta_spec, _row_spec(LANES), _row_spec(LANES)],
        out_specs=[_row_spec(width), _const_spec((1, width))],
        out_shape=[jax.ShapeDtypeStruct((lp, width), BF16), jax.ShapeDtypeStruct((1, width), F32)],
        scratch_shapes=[pltpu.VMEM((BLOCK, KV_DIM), F32), pltpu.VMEM((BLOCK, KV_DIM), F32)],
        compiler_params=_params(("arbitrary",)),
    )(dq, dk, dv, dkm, dvm, ctab, stab)


def _attn_bias(i):
    r = lax.broadcasted_iota(jnp.int32, (BLOCK, 3 * BLOCK), 0)
    c = lax.broadcasted_iota(jnp.int32, (BLOCK, 3 * BLOCK), 1)
    qp = i * BLOCK + r - PAD_ROWS
    kp = (i - 1) * BLOCK + c - PAD_ROWS
    band = (c < 2 * BLOCK) & (kp >= N_META) & (kp <= qp) & (qp - kp < WINDOW)
    mp = c - 2 * BLOCK - PAD_ROWS
    meta = (c >= 2 * BLOCK) & (mp >= 0) & (mp <= qp)
    return jnp.where(band | meta, 0.0, NEG).astype(F32)


def _stack_heads(ref, h):
    return jnp.concatenate(
        [ref[:, (h * GROUP + g) * HEAD_DIM:(h * GROUP + g + 1) * HEAD_DIM] for g in range(GROUP)], axis=0)


def _attn_probs(qs, k3, bias8, sink):
    s = lax.dot_general(qs, k3, (((1,), (1,)), ((), ())), preferred_element_type=F32) + bias8
    m = jnp.maximum(jnp.max(s, axis=1, keepdims=True), sink)
    p = jnp.exp(s - m)
    ps = jnp.exp(sink - m)
    inv = 1.0 / (jnp.sum(p, axis=1, keepdims=True) + ps)
    return p * inv, ps * inv


def _sink_column(sink_ref, h):
    return jnp.concatenate(
        [jnp.broadcast_to(sink_ref[0:1, h * GROUP + g:h * GROUP + g + 1], (BLOCK, 1)) for g in range(GROUP)], axis=0)


def _attn_fwd(q, k_sh, v_sh, sinks):
    lp = q.shape[0]
    nb = lp // BLOCK
    kv = lambda f: pl.BlockSpec((N_KV_HEADS, BLOCK, HEAD_DIM), f)

    def body(q_ref, kp_ref, kc_ref, km_ref, vp_ref, vc_ref, vm_ref, sink_ref, o_ref):
        i = pl.program_id(0)
        bias8 = jnp.tile(_attn_bias(i), (GROUP, 1))
        for h in range(N_KV_HEADS):
            k3 = jnp.concatenate([kp_ref[h], kc_ref[h], km_ref[h]], axis=0)
            v3 = jnp.concatenate([vp_ref[h], vc_ref[h], vm_ref[h]], axis=0)
            qs = _stack_heads(q_ref, h)
            p, _ = _attn_probs(qs, k3, bias8, _sink_column(sink_ref, h))
            o = jnp.dot(p.astype(BF16), v3, preferred_element_type=F32)
            for g in range(GROUP):
                n = h * GROUP + g
                o_ref[:, n * HEAD_DIM:(n + 1) * HEAD_DIM] = o[g * BLOCK:(g + 1) * BLOCK].astype(BF16)

    prev, cur, meta = (lambda i: (0, i, 0)), (lambda i: (0, i + 1, 0)), (lambda i: (0, 1, 0))
    return _pcall(
        body, name="attn_fwd", grid=(nb,),
        in_specs=[_row_spec(Q_DIM), kv(prev), kv(cur), kv(meta), kv(prev), kv(cur), kv(meta), _const_spec((1, N_Q_HEADS))],
        out_specs=_row_spec(Q_DIM), out_shape=jax.ShapeDtypeStruct((lp, Q_DIM), BF16),
        compiler_params=_params(("parallel",)),
    )(q, k_sh, k_sh, k_sh, v_sh, v_sh, v_sh, sinks)


def _attn_bwd(q, k_sh, v_sh, sinks, do):
    lp = q.shape[0]
    nb = lp // BLOCK
    kv = lambda f: pl.BlockSpec((N_KV_HEADS, BLOCK, HEAD_DIM), f)
    cl = lambda s: jnp.minimum(s, nb - 1)

    def body(q_ref, do_ref, kp_ref, kc_ref, km_ref, vp_ref, vc_ref, vm_ref, sink_ref,
             dq_ref, dk_ref, dv_ref, dkm_ref, dvm_ref, dsink_ref, carry_k, carry_v):
        step = pl.program_id(0)

        @pl.when(step == 0)
        def _():
            carry_k[...] = jnp.zeros_like(carry_k)
            carry_v[...] = jnp.zeros_like(carry_v)
            dkm_ref[...] = jnp.zeros_like(dkm_ref)
            dvm_ref[...] = jnp.zeros_like(dvm_ref)
            dsink_ref[...] = jnp.zeros_like(dsink_ref)

        @pl.when(step < nb)
        def _():
            bias8 = jnp.tile(_attn_bias(step), (GROUP, 1))
            lane = lax.broadcasted_iota(jnp.int32, (1, LANES), 1)
            dsink = jnp.zeros((1, LANES), F32)
            for h in range(N_KV_HEADS):
                k3 = jnp.concatenate([kp_ref[h], kc_ref[h], km_ref[h]], axis=0)
                v3 = jnp.concatenate([vp_ref[h], vc_ref[h], vm_ref[h]], axis=0)
                qs = _stack_heads(q_ref, h)
                dos = _stack_heads(do_ref, h)
                p, psink = _attn_probs(qs, k3, bias8, _sink_column(sink_ref, h))
                dp = lax.dot_general(dos, v3, (((1,), (1,)), ((), ())), preferred_element_type=F32)
                delta = jnp.sum(p * dp, axis=1, keepdims=True)
                ds = (p * (dp - delta)).astype(BF16)
                dsk = -psink * delta
                for g in range(GROUP):
                    val = jnp.sum(dsk[g * BLOCK:(g + 1) * BLOCK], axis=0, keepdims=True)
                    dsink = dsink + jnp.where(lane == h * GROUP + g, val, 0.0)
                dqs = jnp.dot(ds, k3, preferred_element_type=F32)
                for g in range(GROUP):
                    n = h * GROUP + g
                    dq_ref[:, n * HEAD_DIM:(n + 1) * HEAD_DIM] = dqs[g * BLOCK:(g + 1) * BLOCK]
                dk3 = lax.dot_general(ds, qs, (((0,), (0,)), ((), ())), preferred_element_type=F32)
                dv3 = lax.dot_general(p.astype(BF16), dos, (((0,), (0,)), ((), ())), preferred_element_type=F32)
                dk_ref[h] = carry_k[h] + dk3[0:BLOCK]
                dv_ref[h] = carry_v[h] + dv3[0:BLOCK]
                carry_k[h] = dk3[BLOCK:2 * BLOCK]
                carry_v[h] = dv3[BLOCK:2 * BLOCK]
                dkm_ref[h] += dk3[2 * BLOCK:3 * BLOCK]
                dvm_ref[h] += dv3[2 * BLOCK:3 * BLOCK]
            dsink_ref[...] += dsink

        @pl.when(step == nb)
        def _():
            dk_ref[...] = carry_k[...]
            dv_ref[...] = carry_v[...]

    prev, cur, meta = (lambda s: (0, cl(s), 0)), (lambda s: (0, cl(s) + 1, 0)), (lambda s: (0, 1, 0))
    lag = lambda s: (0, jnp.maximum(s - 1, 0), 0)
    head_shape = jax.ShapeDtypeStruct((N_KV_HEADS, lp, HEAD_DIM), F32)
    meta_shape = jax.ShapeDtypeStruct((N_KV_HEADS, BLOCK, HEAD_DIM), F32)
    return _pcall(
        body, name="attn_bwd", grid=(nb + 1,),
        in_specs=[pl.BlockSpec((BLOCK, Q_DIM), lambda s: (cl(s), 0)), pl.BlockSpec((BLOCK, Q_DIM), lambda s: (cl(s), 0)),
                  kv(prev), kv(cur), kv(meta), kv(prev), kv(cur), kv(meta), _const_spec((1, N_Q_HEADS))],
        out_specs=[pl.BlockSpec((BLOCK, Q_DIM), lambda s: (cl(s), 0)), kv(lag), kv(lag),
                   _const_spec((N_KV_HEADS, BLOCK, HEAD_DIM)), _const_spec((N_KV_HEADS, BLOCK, HEAD_DIM)), _const_spec((1, LANES))],
        out_shape=[jax.ShapeDtypeStruct((lp, Q_DIM), F32), head_shape, head_shape, meta_shape, meta_shape,
                   jax.ShapeDtypeStruct((1, LANES), F32)],
        scratch_shapes=[pltpu.VMEM((N_KV_HEADS, BLOCK, HEAD_DIM), F32), pltpu.VMEM((N_KV_HEADS, BLOCK, HEAD_DIM), F32)],
        compiler_params=_params(("arbitrary",)),
    )(q, do, k_sh, k_sh, k_sh, v_sh, v_sh, v_sh, sinks)


CONV_CHUNK = 256


def _glu_masked(a_ref, g_ref, base):
    rows = base + lax.broadcasted_iota(jnp.int32, (BLOCK, 1), 0)
    return jnp.where(rows >= PAD_ROWS, a_ref[...].astype(F32) * _sigmoid(g_ref[...].astype(F32)), 0.0)


def _conv_fwd(zc, conv_w, conv_b, ln_g, ln_b, after=None):
    lp = zc.shape[0]
    cd = zc.shape[1] // 2
    nb = lp // BLOCK
    chunk = min(CONV_CHUNK, cd)
    back = lambda col: (lambda i: (jnp.maximum(i - 1, 0), col))
    lo = BLOCK - (CONV_WIDTH - 1)

    def body(ap_ref, gp_ref, ac_ref, gc_ref, w_ref, b_ref, lg_ref, lb_ref, co_ref, c2_ref, ext):
        i = pl.program_id(0)
        ext[0:BLOCK, :] = _glu_masked(ap_ref, gp_ref, (i - 1) * BLOCK)
        ext[BLOCK:2 * BLOCK, :] = _glu_masked(ac_ref, gc_ref, i * BLOCK)
        for c0 in range(0, cd, chunk):
            acc = jnp.zeros((BLOCK, chunk), F32)
            for k in range(CONV_WIDTH):
                acc = acc + ext[lo + k:lo + k + BLOCK, c0:c0 + chunk] * w_ref[k:k + 1, c0:c0 + chunk]
            co_ref[:, c0:c0 + chunk] = acc + b_ref[:, c0:c0 + chunk]
        x = co_ref[...]
        mu = jnp.mean(x, axis=-1, keepdims=True)
        xc = x - mu
        r = lax.rsqrt(jnp.mean(xc * xc, axis=-1, keepdims=True) + EPS)
        y = xc * r * lg_ref[...] + lb_ref[...]
        c2_ref[...] = (y * _sigmoid(y)).astype(BF16)

    return _pcall(
        body, after=after, name="conv_fwd", grid=(nb,),
        in_specs=[pl.BlockSpec((BLOCK, cd), back(0)), pl.BlockSpec((BLOCK, cd), back(1)), _row_spec(cd, 0), _row_spec(cd, 1),
                  _const_spec((CONV_ROWS, cd)), _const_spec((1, cd)), _const_spec((1, cd)), _const_spec((1, cd))],
        out_specs=[_row_spec(cd), _row_spec(cd)],
        out_shape=[jax.ShapeDtypeStruct((lp, cd), F32), jax.ShapeDtypeStruct((lp, cd), BF16)],
        scratch_shapes=[pltpu.VMEM((2 * BLOCK, cd), F32)],
        compiler_params=_params(("arbitrary",)),
    )(zc, zc, zc, zc, conv_w, conv_b, ln_g, ln_b)


def _conv_bwd_norm(dc2, conv_out, ln_g, ln_b):
    lp, cd = conv_out.shape

    def body(d_ref, x_ref, lg_ref, lb_ref, dco_ref, dlg_ref, dlb_ref, dcb_ref):
        i = pl.program_id(0)
        x = x_ref[...]
        g = lg_ref[...]
        mu = jnp.mean(x, axis=-1, keepdims=True)
        xc = x - mu
        r = lax.rsqrt(jnp.mean(xc * xc, axis=-1, keepdims=True) + EPS)
        xhat = xc * r
        y = xhat * g + lb_ref[...]
        sg = _sigmoid(y)
        dy = d_ref[...] * (sg * (1.0 + y * (1.0 - sg)))
        dxhat = dy * g
        dx = r * (dxhat - jnp.mean(dxhat, axis=-1, keepdims=True) - xhat * jnp.mean(dxhat * xhat, axis=-1, keepdims=True))
        dco_ref[...] = dx

        @pl.when(i == 0)
        def _():
            dlg_ref[...] = jnp.zeros_like(dlg_ref)
            dlb_ref[...] = jnp.zeros_like(dlb_ref)
            dcb_ref[...] = jnp.zeros_like(dcb_ref)

        dlg_ref[...] += jnp.sum(dy * xhat, axis=0, keepdims=True)
        dlb_ref[...] += jnp.sum(dy, axis=0, keepdims=True)
        dcb_ref[...] += jnp.sum(dx, axis=0, keepdims=True)

    vec = jax.ShapeDtypeStruct((1, cd), F32)
    return _pcall(
        body, name="conv_bwd_norm", grid=(lp // BLOCK,),
        in_specs=[_row_spec(cd), _row_spec(cd), _const_spec((1, cd)), _const_spec((1, cd))],
        out_specs=[_row_spec(cd), _const_spec((1, cd)), _const_spec((1, cd)), _const_spec((1, cd))],
        out_shape=[jax.ShapeDtypeStruct((lp, cd), F32), vec, vec, vec],
        compiler_params=_params(("arbitrary",)),
    )(dc2, conv_out, ln_g, ln_b)


def _conv_bwd_taps(dco, zc, conv_w):
    lp, cd = dco.shape
    nb = lp // BLOCK
    chunk = min(CONV_CHUNK, cd)
    back = lambda col: (lambda i: (jnp.maximum(i - 1, 0), col))
    fwd = lambda i: (jnp.minimum(i + 1, nb - 1), 0)
    lo = BLOCK - (CONV_WIDTH - 1)

    def body(dc_ref, dn_ref, ap_ref, gp_ref, ac_ref, gc_ref, w_ref, dz_ref, sum_ref, dw_ref, ext, dext, dcb, dwacc):
        i = pl.program_id(0)
        ext[0:BLOCK, :] = _glu_masked(ap_ref, gp_ref, (i - 1) * BLOCK)
        ext[BLOCK:2 * BLOCK, :] = _glu_masked(ac_ref, gc_ref, i * BLOCK)
        dext[0:BLOCK, :] = dc_ref[...]
        dext[BLOCK:2 * BLOCK, :] = dn_ref[...] * (i < nb - 1).astype(F32)

        @pl.when(i == 0)
        def _():
            dwacc[...] = jnp.zeros_like(dwacc)
            sum_ref[...] = jnp.zeros_like(sum_ref)

        for c0 in range(0, cd, chunk):
            cols = slice(c0, c0 + chunk)
            dcur = dext[0:BLOCK, cols]
            acc = jnp.zeros((BLOCK, chunk), F32)
            for k in range(CONV_WIDTH):
                s = CONV_WIDTH - 1 - k
                acc = acc + dext[s:s + BLOCK, cols] * w_ref[k:k + 1, cols]
                prod = dcur * ext[lo + k:lo + k + BLOCK, cols]
                dwacc[8 * k:8 * k + 8, cols] += jnp.sum(prod.reshape(BLOCK // 8, 8, chunk), axis=0)
            dcb[:, cols] = acc

        @pl.when(i == nb - 1)
        def _():
            for k in range(CONV_ROWS):
                dw_ref[k:k + 1, :] = jnp.sum(dwacc[8 * k:8 * k + 8, :], axis=0, keepdims=True)
        rows = i * BLOCK + lax.broadcasted_iota(jnp.int32, (BLOCK, 1), 0)
        dc = jnp.where(rows >= PAD_ROWS, dcb[...], 0.0)
        a = ac_ref[...].astype(F32)
        sg = _sigmoid(gc_ref[...].astype(F32))
        da = dc * sg
        dg = dc * a * sg * (1.0 - sg)
        dz_ref[:, 0:cd] = da.astype(BF16)
        dz_ref[:, cd:2 * cd] = dg.astype(BF16)
        sum_ref[:, 0:cd] += jnp.sum(da, axis=0, keepdims=True)
        sum_ref[:, cd:2 * cd] += jnp.sum(dg, axis=0, keepdims=True)

    return _pcall(
        body, name="conv_bwd_taps", grid=(nb,),
        in_specs=[_row_spec(cd), pl.BlockSpec((BLOCK, cd), fwd),
                  pl.BlockSpec((BLOCK, cd), back(0)), pl.BlockSpec((BLOCK, cd), back(1)), _row_spec(cd, 0), _row_spec(cd, 1),
                  _const_spec((CONV_ROWS, cd))],
        out_specs=[_row_spec(2 * cd), _const_spec((1, 2 * cd)), _const_spec((CONV_ROWS, cd))],
        out_shape=[jax.ShapeDtypeStruct((lp, 2 * cd), BF16), jax.ShapeDtypeStruct((1, 2 * cd), F32),
                   jax.ShapeDtypeStruct((CONV_ROWS, cd), F32)],
        scratch_shapes=[pltpu.VMEM((2 * BLOCK, cd), F32), pltpu.VMEM((2 * BLOCK, cd), F32), pltpu.VMEM((BLOCK, cd), F32),
                        pltpu.VMEM((8 * CONV_ROWS, cd), F32)],
        compiler_params=_params(("arbitrary",)),
    )(dco, dco, zc, zc, zc, zc, conv_w)


def _gate_fwd(a, b, zg, after=None):
    lp, d = a.shape

    def body(a_ref, b_ref, ga_ref, gb_ref, m_ref):
        ga, gb = ga_ref[...].astype(F32), gb_ref[...].astype(F32)
        m_ref[...] = (_sigmoid(ga) * a_ref[...].astype(F32) + _sigmoid(gb) * b_ref[...].astype(F32)).astype(BF16)

    return _pcall(
        body, after=after, name="gate_fwd", grid=(lp // BLOCK,),
        in_specs=[_row_spec(d), _row_spec(d), _row_spec(d, 0), _row_spec(d, 1)], out_specs=_row_spec(d),
        out_shape=jax.ShapeDtypeStruct((lp, d), BF16), compiler_params=_params(("parallel",)),
    )(a, b, zg, zg)


def _gate_bwd(dm, a, b, zg):
    lp, d = a.shape

    def body(dm_ref, a_ref, b_ref, ga_ref, gb_ref, da_ref, db_ref, dz_ref, sum_ref, dbias_ref):
        i = pl.program_id(0)
        dm_ = dm_ref[...].astype(F32)
        sa = _sigmoid(ga_ref[...].astype(F32))
        sb = _sigmoid(gb_ref[...].astype(F32))
        db = dm_ * sb
        dga = dm_ * a_ref[...].astype(F32) * sa * (1.0 - sa)
        dgb = dm_ * b_ref[...].astype(F32) * sb * (1.0 - sb)
        da_ref[...] = (dm_ * sa).astype(BF16)
        db_ref[...] = db.astype(BF16)
        dz_ref[:, 0:d] = dga.astype(BF16)
        dz_ref[:, d:2 * d] = dgb.astype(BF16)

        @pl.when(i == 0)
        def _():
            sum_ref[...] = jnp.zeros_like(sum_ref)
            dbias_ref[...] = jnp.zeros_like(dbias_ref)

        sum_ref[:, 0:d] += jnp.sum(dga, axis=0, keepdims=True)
        sum_ref[:, d:2 * d] += jnp.sum(dgb, axis=0, keepdims=True)
        dbias_ref[...] += jnp.sum(db, axis=0, keepdims=True)

    return _pcall(
        body, name="gate_bwd", grid=(lp // BLOCK,),
        in_specs=[_row_spec(d), _row_spec(d), _row_spec(d), _row_spec(d, 0), _row_spec(d, 1)],
        out_specs=[_row_spec(d), _row_spec(d), _row_spec(2 * d), _const_spec((1, 2 * d)), _const_spec((1, d))],
        out_shape=[jax.ShapeDtypeStruct((lp, d), BF16), jax.ShapeDtypeStruct((lp, d), BF16), jax.ShapeDtypeStruct((lp, 2 * d), BF16),
                   jax.ShapeDtypeStruct((1, 2 * d), F32), jax.ShapeDtypeStruct((1, d), F32)],
        compiler_params=_params(("arbitrary",)),
    )(dm, a, b, zg, zg)


def _swiglu_fwd(gu, after=None):
    lp = gu.shape[0]
    f = gu.shape[1] // 2

    def body(g_ref, u_ref, o_ref):
        g = g_ref[...].astype(F32)
        o_ref[...] = (g * _sigmoid(g) * u_ref[...].astype(F32)).astype(BF16)

    return _pcall(
        body, after=after, name="swiglu_fwd", grid=(lp // BLOCK,), in_specs=[_row_spec(f, 0), _row_spec(f, 1)], out_specs=_row_spec(f),
        out_shape=jax.ShapeDtypeStruct((lp, f), BF16), compiler_params=_params(("parallel",)),
    )(gu, gu)


def _swiglu_bwd(dact, gu):
    lp, f = dact.shape

    def body(d_ref, g_ref, u_ref, o_ref):
        g = g_ref[...].astype(F32)
        d = d_ref[...].astype(F32)
        sg = _sigmoid(g)
        o_ref[:, 0:f] = (d * u_ref[...].astype(F32) * (sg * (1.0 + g * (1.0 - sg)))).astype(BF16)
        o_ref[:, f:2 * f] = (d * g * sg).astype(BF16)

    return _pcall(
        body, name="swiglu_bwd", grid=(lp // BLOCK,), in_specs=[_row_spec(f), _row_spec(f, 0), _row_spec(f, 1)],
        out_specs=_row_spec(2 * f), out_shape=jax.ShapeDtypeStruct((lp, 2 * f), BF16), compiler_params=_params(("parallel",)),
    )(dact, gu, gu)


ANY = pl.BlockSpec(memory_space=pl.ANY)


def _all_gather_rows(x, name, after=None):
    r, c = x.shape

    def body(x_ref, out_ref, send_sems, recv_sems, local_sem):
        mx, my, mc = lax.axis_index("x"), lax.axis_index("y"), lax.axis_index("c")
        me, sibling = (mx, my, mc), (mx, my, 1 - mc)
        chips = [(1 - mx, my), (mx, 1 - my), (1 - mx, 1 - my)]

        def rows(px, py, pc):
            return out_ref.at[pl.ds((4 * px + 2 * py + pc) * r, r), :]

        def copy(k, block, to, src=None):
            return pltpu.make_async_remote_copy(
                src_ref=rows(*block) if src is None else src, dst_ref=rows(*block),
                send_sem=send_sems.at[k], recv_sem=recv_sems.at[k], device_id=to, device_id_type=MESH)

        mine = pltpu.make_async_copy(x_ref, rows(*me), local_sem)
        mine.start()
        first = [copy(0, me, sibling, src=x_ref)]
        first += [copy(1 + j, me, (*chip, mc), src=x_ref) for j, chip in enumerate(chips)]
        for cp in first:
            cp.start()
        passed = [copy(4 + j, (*chip, mc), sibling) for j, chip in enumerate(chips)]
        for j, chip in enumerate(chips):
            copy(1 + j, (*chip, mc), me).wait_recv()
            passed[j].start()
        copy(0, sibling, me).wait_recv()
        for j, chip in enumerate(chips):
            copy(4 + j, (*chip, 1 - mc), me).wait_recv()
        for cp in first + passed:
            cp.wait_send()
        mine.wait()

    return _pcall(
        body, after=after, name=name, in_specs=[ANY], out_specs=ANY, out_shape=jax.ShapeDtypeStruct((N_DEV * r, c), x.dtype),
        scratch_shapes=[pltpu.SemaphoreType.DMA((7,)), pltpu.SemaphoreType.DMA((7,)), pltpu.SemaphoreType.DMA(())],
    )(x)


HBM = pl.BlockSpec(memory_space=pltpu.HBM)
SEM = pl.BlockSpec(memory_space=pltpu.SEMAPHORE)
IN_FLIGHT = pltpu.CompilerParams(has_side_effects=pltpu.SideEffectType.DATAFLOW_SIDE_EFFECTING)
N_PEERS = 4


def _place_rows(shard, after, name):
    r, c = shard.shape
    tr = _pick(r, max(16, ELEMENTWISE_BLOCK_BYTES // (4 * c)), 16)
    steps = r // tr
    dev = (4 * lax.axis_index("x") + 2 * lax.axis_index("y") + lax.axis_index("c")).astype(jnp.int32).reshape(1)

    def body(dev_ref, x_ref, after_ref, o_ref):
        o_ref[...] = x_ref[...].astype(BF16)

    return _pcall(
        body, name=name,
        grid_spec=pltpu.PrefetchScalarGridSpec(
            num_scalar_prefetch=1, grid=(steps,),
            in_specs=[pl.BlockSpec((tr, c), lambda i, dev_ref: (i, 0)), pl.BlockSpec(memory_space=pl.ANY)],
            out_specs=pl.BlockSpec((tr, c), lambda i, dev_ref: (dev_ref[0] * steps + i, 0))),
        out_shape=jax.ShapeDtypeStruct((N_DEV * r, c), BF16), compiler_params=_params(("parallel",)),
    )(dev, shard, after)


def _gather_start(full, name):
    r = full.shape[0] // N_DEV

    def body(full_ref, send_sems, recv_sems, out_ref):
        mx, my, mc = lax.axis_index("x"), lax.axis_index("y"), lax.axis_index("c")
        mine = full_ref.at[pl.ds((4 * mx + 2 * my + mc) * r, r), :]
        for k, peer in enumerate([(mx, my, 1 - mc), (1 - mx, my, mc), (mx, 1 - my, mc), (1 - mx, 1 - my, mc)]):
            pltpu.make_async_remote_copy(
                src_ref=mine, dst_ref=mine, send_sem=send_sems.at[k], recv_sem=recv_sems.at[k],
                device_id=peer, device_id_type=MESH).start()

    return pl.pallas_call(
        body, name=name, in_specs=[HBM], out_specs=(SEM, SEM, HBM),
        out_shape=(pltpu.SemaphoreType.DMA((N_PEERS,)), pltpu.SemaphoreType.DMA((N_PEERS,)), pltpu.HBM(full.shape, full.dtype)),
        input_output_aliases={0: 2}, compiler_params=IN_FLIGHT,
    )(pltpu.with_memory_space_constraint(full, pltpu.HBM))


def _gather_wait(full, send_sem, recv_sem, after, name):
    r = full.shape[0] // N_DEV

    def body(full_ref, send_ref, recv_ref, after_ref, out_ref):
        mx, my, mc = lax.axis_index("x"), lax.axis_index("y"), lax.axis_index("c")
        block = full_ref.at[pl.ds(0, r), :]
        for k in range(N_PEERS):
            cp = pltpu.make_async_remote_copy(
                src_ref=block, dst_ref=block, send_sem=send_ref.at[k], recv_sem=recv_ref.at[k],
                device_id=(mx, my, mc), device_id_type=MESH)
            cp.wait_send()
            cp.wait_recv()

    return pl.pallas_call(
        body, name=name, in_specs=[HBM, SEM, SEM, pl.BlockSpec(memory_space=pl.ANY)], out_specs=HBM,
        out_shape=pltpu.HBM(full.shape, full.dtype), input_output_aliases={0: 0}, compiler_params=IN_FLIGHT,
    )(full, send_sem, recv_sem, after)


def _gather_forward_start(full, name):
    r = full.shape[0] // N_DEV

    def body(full_ref, send_sems, recv_sems, out_ref):
        mx, my, mc = lax.axis_index("x"), lax.axis_index("y"), lax.axis_index("c")
        for k, (px, py) in enumerate([(1 - mx, my), (mx, 1 - my), (1 - mx, 1 - my)]):
            rows = full_ref.at[pl.ds((4 * px + 2 * py + mc) * r, r), :]
            pltpu.make_async_remote_copy(
                src_ref=rows, dst_ref=rows, send_sem=send_sems.at[k], recv_sem=recv_sems.at[k],
                device_id=(mx, my, 1 - mc), device_id_type=MESH).start()

    return pl.pallas_call(
        body, name=name, in_specs=[HBM], out_specs=(SEM, SEM, HBM),
        out_shape=(pltpu.SemaphoreType.DMA((3,)), pltpu.SemaphoreType.DMA((3,)), pltpu.HBM(full.shape, full.dtype)),
        input_output_aliases={0: 2}, compiler_params=IN_FLIGHT,
    )(full)


def _gather_forward_wait(full, send_sem, recv_sem, after, name):
    r = full.shape[0] // N_DEV

    def body(full_ref, send_ref, recv_ref, after_ref, out_ref):
        mx, my, mc = lax.axis_index("x"), lax.axis_index("y"), lax.axis_index("c")
        block = full_ref.at[pl.ds(0, r), :]
        for k in range(3):
            cp = pltpu.make_async_remote_copy(
                src_ref=block, dst_ref=block, send_sem=send_ref.at[k], recv_sem=recv_ref.at[k],
                device_id=(mx, my, mc), device_id_type=MESH)
            cp.wait_send()
            cp.wait_recv()

    return pl.pallas_call(
        body, name=name, in_specs=[HBM, SEM, SEM, pl.BlockSpec(memory_space=pl.ANY)], out_specs=HBM,
        out_shape=pltpu.HBM(full.shape, full.dtype), input_output_aliases={0: 0}, compiler_params=IN_FLIGHT,
    )(full, send_sem, recv_sem, after)


def _pair_exchange_start(g, name):
    r = g.shape[0] // N_DEV
    c = g.shape[1]
    land = (len(CHIPS), r, c)

    def body(g_ref, land_ref, send_sems, recv_sems, g_out, land_out):
        mx, my, mc = lax.axis_index("x"), lax.axis_index("y"), lax.axis_index("c")
        for j, (px, py) in enumerate(CHIPS):
            pltpu.make_async_remote_copy(
                src_ref=g_ref.at[pl.ds((4 * px + 2 * py + 1 - mc) * r, r), :], dst_ref=land_ref.at[j],
                send_sem=send_sems.at[j], recv_sem=recv_sems.at[j], device_id=(mx, my, 1 - mc), device_id_type=MESH).start()

    return pl.pallas_call(
        body, name=name, in_specs=[HBM, HBM], out_specs=(SEM, SEM, HBM, HBM),
        out_shape=(pltpu.SemaphoreType.DMA((4,)), pltpu.SemaphoreType.DMA((4,)), pltpu.HBM(g.shape, g.dtype), pltpu.HBM(land, g.dtype)),
        input_output_aliases={0: 2, 1: 3}, compiler_params=IN_FLIGHT,
    )(pltpu.with_memory_space_constraint(g, pltpu.HBM), pltpu.with_memory_space_constraint(lax.empty(land, g.dtype), pltpu.HBM))


def _pair_exchange_wait(send_sem, recv_sem, g, land, after, name):
    def body(g_ref, land_ref, send_ref, recv_ref, after_ref, g_out, land_out):
        mx, my, mc = lax.axis_index("x"), lax.axis_index("y"), lax.axis_index("c")
        for j in range(len(CHIPS)):
            cp = pltpu.make_async_remote_copy(
                src_ref=land_ref.at[0], dst_ref=land_ref.at[0], send_sem=send_ref.at[j], recv_sem=recv_ref.at[j],
                device_id=(mx, my, mc), device_id_type=MESH)
            cp.wait_send()
            cp.wait_recv()

    return pl.pallas_call(
        body, name=name, in_specs=[HBM, HBM, SEM, SEM, pl.BlockSpec(memory_space=pl.ANY)], out_specs=(HBM, HBM),
        out_shape=(pltpu.HBM(g.shape, g.dtype), pltpu.HBM(land.shape, land.dtype)), input_output_aliases={0: 0, 1: 1},
        compiler_params=IN_FLIGHT,
    )(g, land, send_sem, recv_sem, after)


def _chip_exchange_start(ps, after, name):
    def body(ps_ref, rx_ref, after_ref, send_sems, recv_sems, ps_out, rx_out):
        mx, my, mc = lax.axis_index("x"), lax.axis_index("y"), lax.axis_index("c")
        chips = [(1 - mx, my), (mx, 1 - my), (1 - mx, 1 - my)]
        for k, (px, py) in enumerate(chips):
            pltpu.make_async_remote_copy(
                src_ref=ps_ref.at[2 * px + py], dst_ref=rx_ref.at[2 * mx + my], send_sem=send_sems.at[k], recv_sem=recv_sems.at[k],
                device_id=(px, py, mc), device_id_type=MESH).start()

    return pl.pallas_call(
        body, name=name, in_specs=[HBM, HBM, pl.BlockSpec(memory_space=pl.ANY)], out_specs=(SEM, SEM, HBM, HBM),
        out_shape=(pltpu.SemaphoreType.DMA((3,)), pltpu.SemaphoreType.DMA((3,)), pltpu.HBM(ps.shape, ps.dtype), pltpu.HBM(ps.shape, ps.dtype)),
        input_output_aliases={0: 2, 1: 3}, compiler_params=IN_FLIGHT,
    )(pltpu.with_memory_space_constraint(ps, pltpu.HBM), pltpu.with_memory_space_constraint(lax.empty(ps.shape, ps.dtype), pltpu.HBM), after)


def _chip_exchange_wait(send_sem, recv_sem, ps, rx, after, name):
    def body(ps_ref, rx_ref, send_ref, recv_ref, after_ref, ps_out, rx_out):
        mx, my, mc = lax.axis_index("x"), lax.axis_index("y"), lax.axis_index("c")
        for k in range(3):
            cp = pltpu.make_async_remote_copy(
                src_ref=ps_ref.at[0], dst_ref=rx_ref.at[0], send_sem=send_ref.at[k], recv_sem=recv_ref.at[k],
                device_id=(mx, my, mc), device_id_type=MESH)
            cp.wait_send()
            cp.wait_recv()

    return pl.pallas_call(
        body, name=name, in_specs=[HBM, HBM, SEM, SEM, pl.BlockSpec(memory_space=pl.ANY)], out_specs=(HBM, HBM),
        out_shape=(pltpu.HBM(ps.shape, ps.dtype), pltpu.HBM(rx.shape, rx.dtype)), input_output_aliases={0: 0, 1: 1},
        compiler_params=IN_FLIGHT,
    )(ps, rx, send_sem, recv_sem, after)


def _sum_chips(ps, rx, name):
    n, r, c = rx.shape
    tr = _pick(r, max(8, ELEMENTWISE_BLOCK_BYTES // (4 * n * c)), 8)
    chip = (2 * lax.axis_index("x") + lax.axis_index("y")).astype(jnp.int32).reshape(1)

    def body(chip_ref, own_ref, x_ref, o_ref):
        me = chip_ref[0]
        own = own_ref[0].astype(F32)
        acc = jnp.where(me == 0, own, x_ref[0].astype(F32))
        for j in range(1, n):
            acc = acc + jnp.where(me == j, own, x_ref[j].astype(F32))
        o_ref[...] = acc

    return _pcall(
        body, name=name,
        grid_spec=pltpu.PrefetchScalarGridSpec(
            num_scalar_prefetch=1, grid=(r // tr,),
            in_specs=[pl.BlockSpec((1, tr, c), lambda i, chip_ref: (chip_ref[0], i, 0)), pl.BlockSpec((n, tr, c), lambda i, chip_ref: (0, i, 0))],
            out_specs=pl.BlockSpec((tr, c), lambda i, chip_ref: (i, 0))),
        out_shape=jax.ShapeDtypeStruct((r, c), F32), compiler_params=_params(("parallel",)),
    )(chip, ps, rx)


def _pair_exchange(g, name):
    r = g.shape[0] // N_DEV
    c = g.shape[1]

    def body(g_ref, theirs_ref, send_sems, recv_sems):
        mx, my, mc = lax.axis_index("x"), lax.axis_index("y"), lax.axis_index("c")
        sibling = (mx, my, 1 - mc)
        copies = []
        for j, (px, py) in enumerate(CHIPS):
            give = g_ref.at[pl.ds((4 * px + 2 * py + 1 - mc) * r, r), :]
            rc = pltpu.make_async_remote_copy(
                src_ref=give, dst_ref=theirs_ref.at[j], send_sem=send_sems.at[j], recv_sem=recv_sems.at[j],
                device_id=sibling, device_id_type=MESH)
            rc.start()
            copies.append(rc)
        for cp in copies:
            cp.wait()

    return _pcall(
        body, name=name, in_specs=[ANY], out_specs=ANY, out_shape=jax.ShapeDtypeStruct((len(CHIPS), r, c), g.dtype),
        scratch_shapes=[pltpu.SemaphoreType.DMA((4,)), pltpu.SemaphoreType.DMA((4,))],
    )(g)


def _pair_sum(g, theirs, name):
    nch, r, c = theirs.shape
    tr = _pick(r, max(16, ELEMENTWISE_BLOCK_BYTES // (2 * c)), 16)
    core = lax.axis_index("c").astype(jnp.int32).reshape(1)

    def body(core_ref, a_ref, b_ref, o_ref):
        o_ref[...] = (a_ref[...].astype(F32) + b_ref[...].astype(F32)).astype(o_ref.dtype)

    spec = pl.BlockSpec((1, tr, c), lambda j, i, core_ref: (j, i, 0))
    own = pl.BlockSpec((1, tr, c), lambda j, i, core_ref: (2 * j + core_ref[0], i, 0))
    return _pcall(
        body, name=name,
        grid_spec=pltpu.PrefetchScalarGridSpec(num_scalar_prefetch=1, grid=(nch, r // tr), in_specs=[own, spec], out_specs=spec),
        out_shape=jax.ShapeDtypeStruct(theirs.shape, theirs.dtype), compiler_params=_params(("parallel", "parallel")),
    )(core, g.reshape(N_DEV, r, c), theirs)


def _sum_blocks(rx, name):
    n, r, c = rx.shape
    tr = _pick(r, max(8, ELEMENTWISE_BLOCK_BYTES // (4 * n * c)), 8)

    def body(x_ref, o_ref):
        acc = x_ref[0].astype(F32)
        for j in range(1, n):
            acc = acc + x_ref[j].astype(F32)
        o_ref[...] = acc

    return _pcall(
        body, name=name, grid=(r // tr,), in_specs=[pl.BlockSpec((n, tr, c), lambda i: (0, i, 0))],
        out_specs=pl.BlockSpec((tr, c), lambda i: (i, 0)), out_shape=jax.ShapeDtypeStruct((r, c), F32),
        compiler_params=_params(("parallel",)),
    )(rx)


def _adamw(w, g, m, v, name):
    r, c = w.shape
    tr = _pick(r, max(8, ELEMENTWISE_BLOCK_BYTES // (4 * c)), 8)
    c1 = 1.0 - ADAM_B1 ** ADAM_STEP
    c2 = 1.0 - ADAM_B2 ** ADAM_STEP

    def body(w_ref, g_ref, m_ref, v_ref, d_ref, nm_ref, nv_ref):
        gg = g_ref[...]
        nm = ADAM_B1 * m_ref[...] + (1.0 - ADAM_B1) * gg
        nv = ADAM_B2 * v_ref[...] + (1.0 - ADAM_B2) * (gg * gg)
        d_ref[...] = -ADAM_LR * ((nm / c1) / (jnp.sqrt(nv / c2) + ADAM_EPS) + ADAM_WD * w_ref[...])
        nm_ref[...] = nm
        nv_ref[...] = nv

    spec = pl.BlockSpec((tr, c), lambda i: (i, 0))
    shp = jax.ShapeDtypeStruct((r, c), F32)
    return _pcall(
        body, name=name, grid=(r // tr,), in_specs=[spec] * 4, out_specs=[spec] * 3, out_shape=[shp] * 3,
        compiler_params=_params(("parallel",)),
    )(w, g, m, v)


def _pack(parts):
    flat, layout, row = [], [], 0
    for p in parts:
        n = p.size
        rows = -(-n // LANES)
        flat.append(jnp.pad(p.reshape(-1).astype(F32), (0, rows * LANES - n)))
        layout.append((row, n, p.shape))
        row += rows
    total = -(-row // 8) * 8
    if total > row:
        flat.append(jnp.zeros(((total - row) * LANES,), F32))
    return jnp.concatenate(flat).reshape(total, LANES), layout


def _unpack(slab, layout):
    flat = slab.reshape(-1)
    return [flat[row * LANES:row * LANES + n].reshape(shape) for row, n, shape in layout]


def kernel(x, meta_tokens, mix_norm_g, w_in, b_in, attn_sinks, conv_w, conv_b, conv_ln_g, conv_ln_b, w_attn_o, w_conv_o, b_conv_o, w_out, ffn_norm_g, w_gate_up, w_down, final_norm_g, loss_target, m_meta_tokens, m_mix_norm_g, m_w_in, m_b_in, m_attn_sinks, m_conv_w, m_conv_b, m_conv_ln_g, m_conv_ln_b, m_w_attn_o, m_w_conv_o, m_b_conv_o, m_w_out, m_ffn_norm_g, m_w_gate_up, m_w_down, m_final_norm_g, v_meta_tokens, v_mix_norm_g, v_w_in, v_b_in, v_attn_sinks, v_conv_w, v_conv_b, v_conv_ln_g, v_conv_ln_b, v_w_attn_o, v_w_conv_o, v_b_conv_o, v_w_out, v_ffn_norm_g, v_w_gate_up, v_w_down, v_final_norm_g):
    xs = x[0]
    tgt = loss_target[0]
    s, d = xs.shape
    lp = s + BLOCK
    cd = conv_b.shape[1]
    ffn = w_down.shape[1] * N_DEV
    dev = 4 * lax.axis_index("x") + 2 * lax.axis_index("y") + lax.axis_index("c")
    cw_cols = conv_w.shape[3]
    meta_cols = meta_tokens.shape[1]

    small, small_layout = _pack([meta_tokens, jnp.pad(conv_w[0, :, 0, :], ((0, CONV_ROWS - CONV_WIDTH), (0, 0)))])
    small_flat = _all_gather_rows(small, "gather_small")
    small_all = small_flat.reshape(N_DEV, *small.shape)
    meta_parts, cw_parts = zip(*[_unpack(small_all[j], small_layout) for j in range(N_DEV)])
    meta_full = jnp.concatenate(meta_parts, axis=1)
    conv_w_full = jnp.concatenate(cw_parts, axis=1)
    g_send, g_recv, g_full = [], [], []
    tok = small_flat
    for shard, name in ((w_in[0].T, "w_in"), (w_attn_o[0].T, "w_attn_o"), (w_conv_o[0].T, "w_conv_o"), (w_out[0], "w_out"),
                        (w_gate_up[0].T, "w_gate_up"), (w_down[0], "w_down")):
        send_sem, recv_sem, tok = _gather_start(_place_rows(shard, tok, "place_" + name), "gather_start_" + name)
        g_send.append(send_sem)
        g_recv.append(recv_sem)
        g_full.append(tok)

    def arrived(w, after, name):
        full = _gather_wait(g_full[w], g_send[w], g_recv[w], after, "gather_wait_" + name)
        return _gather_forward_start(full, "gather_forward_start_" + name)

    def whole(passing, after, name):
        return _gather_forward_wait(passing[2], passing[0], passing[1], after, "gather_forward_wait_" + name)

    ctab, stab = _rope_tables(lp)
    mm = functools.partial(_matmul, tm=1056, tn=1024)

    passing = arrived(0, tok, "w_in")
    h0, u = _prep(xs, meta_full, mix_norm_g, after=passing[2])
    win_t = whole(passing, u, "w_in")
    bq, bkv, bc, bg = b_in[:, :Q_DIM], b_in[:, Q_DIM:Q_DIM + 2 * KV_DIM], b_in[:, Q_DIM + 2 * KV_DIM:Q_DIM + 2 * KV_DIM + 2 * cd], b_in[:, Q_DIM + 2 * KV_DIM + 2 * cd:]
    o_kv, o_c, o_g = Q_DIM, Q_DIM + 2 * KV_DIM, Q_DIM + 2 * KV_DIM + 2 * cd
    zq = mm(u, win_t, mode="nt", name="in_proj_q", out_dtype=BF16, tk=d, bias=bq, b_row_off=0, b_rows=Q_DIM)
    zkv = mm(u, win_t, mode="nt", name="in_proj_kv", out_dtype=BF16, tk=d, bias=bkv, b_row_off=o_kv, b_rows=2 * KV_DIM)
    zc = mm(u, win_t, mode="nt", name="in_proj_conv", out_dtype=BF16, tk=d, bias=bc, b_row_off=o_c, b_rows=2 * cd)
    zg = mm(u, win_t, mode="nt", name="in_proj_gates", out_dtype=BF16, tk=d, bias=bg, b_row_off=o_g, b_rows=2 * d)
    passing = arrived(1, zg, "w_attn_o")
    q_rot, k_sh, v_sh = _rope_fwd(zq, zkv, ctab, stab, after=passing[2])
    o = _attn_fwd(q_rot, k_sh, v_sh, attn_sinks)
    wao_t = whole(passing, o, "w_attn_o")
    br_a = mm(o, wao_t, mode="nt", name="attn_out_proj", out_dtype=BF16, tk=Q_DIM)
    passing = arrived(2, br_a, "w_conv_o")
    conv_out, c2 = _conv_fwd(zc, conv_w_full, conv_b, conv_ln_g, conv_ln_b, after=passing[2])
    wco_t = whole(passing, c2, "w_conv_o")
    br_b = mm(c2, wco_t, mode="nt", name="conv_out_proj", out_dtype=BF16, tk=cd, bias=b_conv_o)
    passing = arrived(3, br_b, "w_out")
    merged = _gate_fwd(br_a, br_b, zg, after=passing[2])
    wout = whole(passing, merged, "w_out")
    passing = arrived(4, wout, "w_gate_up")
    h1 = mm(merged, wout, mode="nn", name="mix_out_proj", out_dtype=F32, tn=512, tk=d, residual=h0, after=passing[2])
    u2 = _rmsnorm_fwd(h1, ffn_norm_g, "ffn_rmsnorm")
    wgu_t = whole(passing, u2, "w_gate_up")
    gu = _matmul(u2, wgu_t, mode="nt", name="ffn_gate_up", out_dtype=BF16, tm=1056, tn=512, tk=d)
    passing = arrived(5, gu, "w_down")
    act = _swiglu_fwd(gu, after=passing[2])
    wdown = whole(passing, act, "w_down")
    h2 = mm(act, wdown, mode="nn", name="ffn_down", out_dtype=F32, tn=512, tk=ffn // 2, residual=h1)
    dh2, dh2_b, loss_part, d_final_g = _final(h2, tgt, final_norm_g.reshape(1, d))

    wgrad = functools.partial(_matmul, mode="tn", out_dtype=BF16, tk=lp, tn=2048, b_inner=False)
    in_flight = {}

    def scatter_begin(g, name):
        return _pair_exchange_start(g, "rs_" + name + "_pair_start")

    def scatter_go_on(pair, after, name):
        g, theirs = _pair_exchange_wait(pair[0], pair[1], pair[2], pair[3], after, "rs_" + name + "_pair_wait")
        ps = _pair_sum(g, theirs, "rs_" + name + "_pair_sum")
        in_flight[name] = _chip_exchange_start(ps, theirs, "rs_" + name + "_chip_start")
        return in_flight[name][2]

    g_wdown = wgrad(act, dh2_b, name="ffn_down_dw", tm=256)
    pair = scatter_begin(g_wdown, "w_down")
    dact = _matmul(dh2_b, wdown, mode="nt", name="ffn_down_dx", out_dtype=BF16, tm=2112, tn=256, tk=d, after=pair[2])
    tok = scatter_go_on(pair, dact, "w_down")
    dgu = _swiglu_bwd(dact, gu)
    g_wgu_t = wgrad(dgu, u2, name="ffn_gate_up_dw", tm=512, after=tok)
    pair = scatter_begin(g_wgu_t, "w_gate_up")
    du2 = mm(dgu, wgu_t, mode="nn", name="ffn_gate_up_dx", out_dtype=F32, tn=512, tk=ffn // 2, after=pair[2])
    tok = scatter_go_on(pair, du2, "w_gate_up")
    dh1, dh1_b, d_ffn_g = _rmsnorm_bwd(du2, h1, ffn_norm_g, dh2, "ffn_rmsnorm_bwd")
    g_wout = wgrad(merged, dh1_b, name="mix_out_dw", tm=512, after=tok)
    pair = scatter_begin(g_wout, "w_out")
    dmerged = mm(dh1_b, wout, mode="nt", name="mix_out_dx", out_dtype=BF16, tk=d, after=pair[2])
    tok = scatter_go_on(pair, dmerged, "w_out")
    d_a, d_b, dz_g, sum_g, d_bco = _gate_bwd(dmerged, br_a, br_b, zg)
    g_wao_t = wgrad(d_a, o, name="attn_out_dw", tm=512, after=tok)
    pair = scatter_begin(g_wao_t, "w_attn_o")
    do = mm(d_a, wao_t, mode="nn", name="attn_out_dx", out_dtype=BF16, tk=d, after=pair[2])
    tok = scatter_go_on(pair, do, "w_attn_o")
    g_wco_t = wgrad(d_b, c2, name="conv_out_dw", tm=512, after=tok)
    pair = scatter_begin(g_wco_t, "w_conv_o")
    dc2 = mm(d_b, wco_t, mode="nn", name="conv_out_dx", out_dtype=F32, tk=d, after=pair[2])
    tok = scatter_go_on(pair, dc2, "w_conv_o")
    dq, dk, dv, dkm, dvm, d_sinks = _attn_bwd(q_rot, k_sh, v_sh, attn_sinks, do)
    dz_qkv, sum_qkv = _rope_bwd(dq, dk, dv, dkm, dvm, ctab, stab)
    dco, d_ln_g, d_ln_b, d_conv_b = _conv_bwd_norm(dc2, conv_out, conv_ln_g, conv_ln_b)
    dz_c, sum_c, d_conv_w = _conv_bwd_taps(dco, zc, conv_w_full)
    dz = jnp.concatenate([dz_qkv, dz_c, dz_g], axis=1)
    d_b_in = jnp.concatenate([sum_qkv, sum_c, sum_g], axis=1)
    in_dim = dz.shape[1]
    g_win_t = wgrad(dz, u, name="in_proj_dw", tm=512, after=tok)
    theirs = _pair_exchange(g_win_t, "rs_w_in_pair_exchange")
    in_flight["w_in"] = _chip_exchange_start(_pair_sum(g_win_t, theirs, "rs_w_in_pair_sum"), theirs, "rs_w_in_chip_start")
    du = mm(dz, win_t, mode="nn", name="in_proj_dx", out_dtype=F32, tk=in_dim // 4, after=in_flight["w_in"][2])
    grad_x, d_meta, d_mix_g = _rmsnorm_bwd_first(du, h0, mix_norm_g, dh1)

    weights = dict(meta_tokens=meta_tokens, mix_norm_g=mix_norm_g, w_in=w_in, b_in=b_in, attn_sinks=attn_sinks, conv_w=conv_w,
                   conv_b=conv_b, conv_ln_g=conv_ln_g, conv_ln_b=conv_ln_b, w_attn_o=w_attn_o, w_conv_o=w_conv_o, b_conv_o=b_conv_o,
                   w_out=w_out, ffn_norm_g=ffn_norm_g, w_gate_up=w_gate_up, w_down=w_down, final_norm_g=final_norm_g)
    m_in = dict(meta_tokens=m_meta_tokens, mix_norm_g=m_mix_norm_g, w_in=m_w_in, b_in=m_b_in, attn_sinks=m_attn_sinks, conv_w=m_conv_w,
                conv_b=m_conv_b, conv_ln_g=m_conv_ln_g, conv_ln_b=m_conv_ln_b, w_attn_o=m_w_attn_o, w_conv_o=m_w_conv_o,
                b_conv_o=m_b_conv_o, w_out=m_w_out, ffn_norm_g=m_ffn_norm_g, w_gate_up=m_w_gate_up, w_down=m_w_down,
                final_norm_g=m_final_norm_g)
    v_in = dict(meta_tokens=v_meta_tokens, mix_norm_g=v_mix_norm_g, w_in=v_w_in, b_in=v_b_in, attn_sinks=v_attn_sinks, conv_w=v_conv_w,
                conv_b=v_conv_b, conv_ln_g=v_conv_ln_g, conv_ln_b=v_conv_ln_b, w_attn_o=v_w_attn_o, w_conv_o=v_w_conv_o,
                b_conv_o=v_b_conv_o, w_out=v_w_out, ffn_norm_g=v_ffn_norm_g, w_gate_up=v_w_gate_up, w_down=v_w_down,
                final_norm_g=v_final_norm_g)
    names = list(weights)
    grads, delta, new_m, new_v = {}, {}, {}, {}
    transposed = ("w_in", "w_attn_o", "w_conv_o", "w_gate_up")
    tok = grad_x
    for n in ("w_down", "w_gate_up", "w_out", "w_attn_o", "w_conv_o", "w_in"):
        send_sem, recv_sem, ps, rx = in_flight[n]
        ps, rx = _chip_exchange_wait(send_sem, recv_sem, ps, rx, tok, "rs_" + n + "_chip_wait")
        g = _sum_chips(ps, rx, "rs_" + n + "_sum")
        g = g.T if n in transposed else g
        shape = weights[n].shape
        dl, nm, nv = _adamw(weights[n].reshape(g.shape), g, m_in[n].reshape(g.shape), v_in[n].reshape(g.shape), "adamw_" + n)
        grads[n], delta[n], new_m[n], new_v[n] = g.reshape(shape), dl.reshape(shape), nm.reshape(shape), nv.reshape(shape)
        tok = dl

    slab, slab_layout = _pack([loss_part[:, :1], d_mix_g, d_b_in, d_sinks[:, :N_Q_HEADS], d_conv_b, d_ln_g, d_ln_b, d_bco,
                               d_ffn_g, d_final_g, d_conv_w, d_meta])
    slab_all = _all_gather_rows(slab, "gather_small_grads", after=tok).reshape(N_DEV, *slab.shape)
    (loss, g_mix_g, g_b_in, g_sinks, g_conv_b, g_ln_g, g_ln_b, g_bco, g_ffn_g, g_final_g, g_conv_w_full, g_meta_full
     ) = _unpack(_sum_blocks(slab_all, "sum_small_grads"), slab_layout)
    g_conv_w = lax.dynamic_slice(g_conv_w_full, (0, dev * cw_cols), (CONV_WIDTH, cw_cols)).reshape(conv_w.shape)
    g_meta = lax.dynamic_slice(g_meta_full, (0, dev * meta_cols), (N_META, meta_cols))
    g_final_g = g_final_g.reshape(final_norm_g.shape)
    grads.update(meta_tokens=g_meta, mix_norm_g=g_mix_g, b_in=g_b_in, attn_sinks=g_sinks, conv_w=g_conv_w, conv_b=g_conv_b,
                 conv_ln_g=g_ln_g, conv_ln_b=g_ln_b, b_conv_o=g_bco, ffn_norm_g=g_ffn_g, final_norm_g=g_final_g)
    rest = [n for n in names if n not in delta]
    w_slab, rest_layout = _pack([weights[n] for n in rest])
    g_slab, _ = _pack([grads[n] for n in rest])
    m_slab, _ = _pack([m_in[n] for n in rest])
    v_slab, _ = _pack([v_in[n] for n in rest])
    dl, nm, nv = _adamw(w_slab, g_slab, m_slab, v_slab, "adamw_small")
    for n, a, b, c in zip(rest, _unpack(dl, rest_layout), _unpack(nm, rest_layout), _unpack(nv, rest_layout)):
        delta[n], new_m[n], new_v[n] = a, b, c

    return (loss.reshape(()), grad_x[None], *[grads[n] for n in names], *[delta[n] for n in names],
            *[new_m[n] for n in names], *[new_v[n] for n in names])
```

```python
import functools
import math

import jax
import jax.numpy as jnp
from jax import lax
from jax.experimental import pallas as pl
from jax.experimental.pallas import tpu as pltpu

F32 = jnp.float32
BF16 = jnp.bfloat16

N_DEV = 8
BLOCK = 128
N_META = 16
PAD_ROWS = BLOCK - N_META
HEAD_DIM = 64
N_Q_HEADS = 32
N_KV_HEADS = 4
GROUP = N_Q_HEADS // N_KV_HEADS
Q_DIM = N_Q_HEADS * HEAD_DIM
KV_DIM = N_KV_HEADS * HEAD_DIM
WINDOW = 128
CONV_WIDTH = 31
CONV_ROWS = 32
ROPE_THETA = 10000.0
EPS = 1e-6
ATTN_SCALE = HEAD_DIM ** -0.5
NEG = -1e30

ADAM_LR = 0.001
ADAM_B1 = 0.9
ADAM_B2 = 0.999
ADAM_EPS = 1e-08
ADAM_WD = 0.01
ADAM_STEP = 10

VMEM_LIMIT_BYTES = 56 * 1024 * 1024
LANES = 128
ELEMENTWISE_BLOCK_BYTES = 2 * 1024 * 1024
MESH = pl.DeviceIdType.MESH
CHIPS = ((0, 0), (0, 1), (1, 0), (1, 1))


def _pcall(body, after=None, **kw):
    if after is None:
        return pl.pallas_call(body, **kw)
    in_specs = list(kw.pop("in_specs"))
    n_in = len(in_specs)

    def ordered_body(*refs):
        return body(*refs[:n_in], *refs[n_in + 1:])

    call = pl.pallas_call(ordered_body, in_specs=in_specs + [pl.BlockSpec(memory_space=pl.ANY)], **kw)
    return lambda *args: call(*args, after)


def _params(semantics=None):
    if semantics is None:
        return pltpu.CompilerParams(vmem_limit_bytes=VMEM_LIMIT_BYTES)
    return pltpu.CompilerParams(dimension_semantics=semantics, vmem_limit_bytes=VMEM_LIMIT_BYTES)


def _pick(dim, pref, align):
    best = None
    t = align
    while t <= min(dim, pref):
        if dim % t == 0:
            best = t
        t += align
    return dim if best is None else best


def _sigmoid(x):
    return 1.0 / (1.0 + jnp.exp(-x))


def _matmul(a, b, *, mode, name, out_dtype, tm, tn, tk, bias=None, residual=None, b_inner=True,
            b_row_off=0, b_rows=None, after=None):
    if mode == "nn":
        m, k = a.shape
        n = b.shape[1]
    elif mode == "nt":
        m, k = a.shape
        n = b.shape[0] if b_rows is None else b_rows
    else:
        k, m = a.shape
        n = b.shape[1]
    tm = _pick(m, tm, 16)
    tn = _pick(math.gcd(n, b_row_off) if mode == "nt" and b_row_off else n, tn, LANES)
    tk = _pick(k, tk, LANES if mode != "tn" else 16)
    nm, nn, nk = m // tm, n // tn, k // tk
    if mode == "nt":
        assert b_row_off % tn == 0
    off = b_row_off // tn if mode == "nt" else 0

    if b_inner:
        grid = (nm, nn, nk)
        ij = lambda g0, g1: (g0, g1)
    else:
        grid = (nn, nm, nk)
        ij = lambda g0, g1: (g1, g0)

    if mode == "tn":
        a_spec = pl.BlockSpec((tk, tm), lambda g0, g1, kk: (kk, ij(g0, g1)[0]))
    else:
        a_spec = pl.BlockSpec((tm, tk), lambda g0, g1, kk: (ij(g0, g1)[0], kk))
    if mode == "nt":
        b_spec = pl.BlockSpec((tn, tk), lambda g0, g1, kk: (ij(g0, g1)[1] + off, kk))
    else:
        b_spec = pl.BlockSpec((tk, tn), lambda g0, g1, kk: (kk, ij(g0, g1)[1]))
    o_spec = pl.BlockSpec((tm, tn), lambda g0, g1, kk: ij(g0, g1))
    in_specs = [a_spec, b_spec]
    args = [a, b]
    if bias is not None:
        in_specs.append(pl.BlockSpec((1, tn), lambda g0, g1, kk: (0, ij(g0, g1)[1])))
        args.append(bias)
    if residual is not None:
        in_specs.append(o_spec)
        args.append(residual)
    dims = {"nn": (((1,), (0,)), ((), ())), "nt": (((1,), (1,)), ((), ())), "tn": (((0,), (0,)), ((), ()))}[mode]
    has_bias, has_res = bias is not None, residual is not None

    def body(*refs):
        a_ref, b_ref = refs[0], refs[1]
        pos = 2
        bias_ref = res_ref = None
        if has_bias:
            bias_ref = refs[pos]
            pos += 1
        if has_res:
            res_ref = refs[pos]
            pos += 1
        o_ref = refs[pos]
        acc_ref = refs[pos + 1] if nk > 1 else None

        def finish(acc):
            if has_bias:
                acc = acc + bias_ref[...]
            if has_res:
                acc = acc + res_ref[...]
            o_ref[...] = acc.astype(out_dtype)

        p = lax.dot_general(a_ref[...], b_ref[...], dims, preferred_element_type=F32)
        if nk == 1:
            finish(p)
        else:
            kk = pl.program_id(2)

            @pl.when(kk == 0)
            def _():
                acc_ref[...] = p

            @pl.when(kk > 0)
            def _():
                acc_ref[...] += p

            @pl.when(kk == nk - 1)
            def _():
                finish(acc_ref[...])

    return _pcall(
        body, after=after, name=name, grid=grid, in_specs=in_specs, out_specs=o_spec,
        out_shape=jax.ShapeDtypeStruct((m, n), out_dtype),
        scratch_shapes=[pltpu.VMEM((tm, tn), F32)] if nk > 1 else [],
        compiler_params=_params(("parallel", "parallel", "arbitrary")),
    )(*args)


def _row_spec(width, col=0):
    return pl.BlockSpec((BLOCK, width), lambda i: (i, col))


def _const_spec(shape):
    nd = len(shape)
    return pl.BlockSpec(shape, lambda i: (0,) * nd)


def _prep(x, meta_full, g, after=None):
    s, d = x.shape
    lp = s + BLOCK
    nb = lp // BLOCK

    def body(x_ref, meta_ref, g_ref, h_ref, u_ref):
        i = pl.program_id(0)

        @pl.when(i == 0)
        def _():
            h_ref[0:PAD_ROWS, :] = jnp.zeros((PAD_ROWS, d), F32)
            h_ref[PAD_ROWS:BLOCK, :] = meta_ref[...]

        @pl.when(i > 0)
        def _():
            h_ref[...] = x_ref[...]

        h = h_ref[...]
        r = lax.rsqrt(jnp.mean(h * h, axis=-1, keepdims=True) + EPS)
        u_ref[...] = (h * r * g_ref[...]).astype(BF16)

    return _pcall(
        body, after=after, name="prep_rmsnorm", grid=(nb,),
        in_specs=[pl.BlockSpec((BLOCK, d), lambda i: (jnp.maximum(i - 1, 0), 0)), _const_spec((N_META, d)), _const_spec((1, d))],
        out_specs=[_row_spec(d), _row_spec(d)],
        out_shape=[jax.ShapeDtypeStruct((lp, d), F32), jax.ShapeDtypeStruct((lp, d), BF16)],
        compiler_params=_params(("arbitrary",)),
    )(x, meta_full, g)


def _rmsnorm_fwd(h, g, name):
    lp, d = h.shape

    def body(h_ref, g_ref, u_ref):
        x = h_ref[...]
        r = lax.rsqrt(jnp.mean(x * x, axis=-1, keepdims=True) + EPS)
        u_ref[...] = (x * r * g_ref[...]).astype(BF16)

    return _pcall(
        body, name=name, grid=(lp // BLOCK,), in_specs=[_row_spec(d), _const_spec((1, d))], out_specs=_row_spec(d),
        out_shape=jax.ShapeDtypeStruct((lp, d), BF16), compiler_params=_params(("parallel",)),
    )(h, g)


def _rms_bwd_core(dy, x, g):
    r = lax.rsqrt(jnp.mean(x * x, axis=-1, keepdims=True) + EPS)
    xhat = x * r
    dxhat = dy * g
    dx = r * (dxhat - xhat * jnp.mean(dxhat * xhat, axis=-1, keepdims=True))
    return dx, jnp.sum(dy * xhat, axis=0, keepdims=True)


def _rmsnorm_bwd(dy, h, g, dres, name):
    lp, d = h.shape

    def body(dy_ref, h_ref, g_ref, dres_ref, dh_ref, dhb_ref, dg_ref):
        i = pl.program_id(0)
        dx, dg = _rms_bwd_core(dy_ref[...], h_ref[...], g_ref[...])
        dh = dres_ref[...] + dx
        dh_ref[...] = dh
        dhb_ref[...] = dh.astype(BF16)

        @pl.when(i == 0)
        def _():
            dg_ref[...] = jnp.zeros_like(dg_ref)

        dg_ref[...] += dg

    return _pcall(
        body, name=name, grid=(lp // BLOCK,),
        in_specs=[_row_spec(d), _row_spec(d), _const_spec((1, d)), _row_spec(d)],
        out_specs=[_row_spec(d), _row_spec(d), _const_spec((1, d))],
        out_shape=[jax.ShapeDtypeStruct((lp, d), F32), jax.ShapeDtypeStruct((lp, d), BF16), jax.ShapeDtypeStruct((1, d), F32)],
        compiler_params=_params(("arbitrary",)),
    )(dy, h, g, dres)


def _rmsnorm_bwd_first(dy, h, g, dres):
    lp, d = h.shape
    s = lp - BLOCK

    def body(dy_ref, h_ref, g_ref, dres_ref, gx_ref, dmeta_ref, dg_ref):
        i = pl.program_id(0)
        dx, dg = _rms_bwd_core(dy_ref[...], h_ref[...], g_ref[...])
        dh = dres_ref[...] + dx
        gx_ref[...] = dh

        @pl.when(i == 0)
        def _():
            dmeta_ref[...] = dh[PAD_ROWS:BLOCK, :]
            dg_ref[...] = jnp.zeros_like(dg_ref)

        dg_ref[...] += dg

    return _pcall(
        body, name="rmsnorm_bwd_first", grid=(lp // BLOCK,),
        in_specs=[_row_spec(d), _row_spec(d), _const_spec((1, d)), _row_spec(d)],
        out_specs=[pl.BlockSpec((BLOCK, d), lambda i: (jnp.maximum(i - 1, 0), 0)), _const_spec((N_META, d)), _const_spec((1, d))],
        out_shape=[jax.ShapeDtypeStruct((s, d), F32), jax.ShapeDtypeStruct((N_META, d), F32), jax.ShapeDtypeStruct((1, d), F32)],
        compiler_params=_params(("arbitrary",)),
    )(dy, h, g, dres)


def _final(h2, tgt, g):
    lp, d = h2.shape

    def body(h_ref, t_ref, g_ref, dh_ref, dhb_ref, loss_ref, dg_ref):
        i = pl.program_id(0)
        x = h_ref[...]
        gg = g_ref[...]
        r = lax.rsqrt(jnp.mean(x * x, axis=-1, keepdims=True) + EPS)
        xhat = x * r
        y = xhat * gg
        live = (i > 0).astype(F32)
        err = (y - t_ref[...]) * live
        dy = err * (1.0 / d)
        dxhat = dy * gg
        dh = r * (dxhat - xhat * jnp.mean(dxhat * xhat, axis=-1, keepdims=True))
        dh_ref[...] = dh
        dhb_ref[...] = dh.astype(BF16)

        @pl.when(i == 0)
        def _():
            loss_ref[...] = jnp.zeros_like(loss_ref)
            dg_ref[...] = jnp.zeros_like(dg_ref)

        row_loss = jnp.mean(err * err, axis=-1, keepdims=True)
        loss_ref[...] += 0.5 * jnp.sum(row_loss, axis=0, keepdims=True)
        dg_ref[...] += jnp.sum(dy * xhat, axis=0, keepdims=True)

    return _pcall(
        body, name="final_norm_loss", grid=(lp // BLOCK,),
        in_specs=[_row_spec(d), pl.BlockSpec((BLOCK, d), lambda i: (jnp.maximum(i - 1, 0), 0)), _const_spec((1, d))],
        out_specs=[_row_spec(d), _row_spec(d), _const_spec((1, LANES)), _const_spec((1, d))],
        out_shape=[jax.ShapeDtypeStruct((lp, d), F32), jax.ShapeDtypeStruct((lp, d), BF16),
                   jax.ShapeDtypeStruct((1, LANES), F32), jax.ShapeDtypeStruct((1, d), F32)],
        compiler_params=_params(("arbitrary",)),
    )(h2, tgt, g)


def _swap_halves(x):
    w = x.shape[1]
    lane = lax.broadcasted_iota(jnp.int32, x.shape, 1)
    first = (lane & (HEAD_DIM - 1)) < (HEAD_DIM // 2)
    return jnp.where(first, pltpu.roll(x, w - HEAD_DIM // 2, 1), pltpu.roll(x, HEAD_DIM // 2, 1))


def _rope_tables(lp):
    pos = jnp.maximum(jnp.arange(lp, dtype=jnp.int32) - PAD_ROWS, 0).astype(F32)
    inv_freq = ROPE_THETA ** (-jnp.arange(0, HEAD_DIM, 2, dtype=F32) / HEAD_DIM)
    ang = pos[:, None] * inv_freq[None, :]
    c, s = jnp.cos(ang), jnp.sin(ang)
    reps = LANES // HEAD_DIM
    return jnp.tile(jnp.concatenate([c, c], axis=1), (1, reps)), jnp.tile(jnp.concatenate([-s, s], axis=1), (1, reps))


def _rope_fwd(zq, zkv, ctab, stab, after=None):
    lp = zq.shape[0]
    nb = lp // BLOCK
    back = lambda s: (jnp.maximum(s - 1, 0), 0)

    def body(zq_ref, zkv_ref, c_ref, s_ref, q_ref, k_ref, v_ref):
        step = pl.program_id(0)
        c128, s128 = c_ref[...], s_ref[...]

        def rope(x):
            reps = x.shape[1] // LANES
            return x * jnp.tile(c128, (1, reps)) + _swap_halves(x) * jnp.tile(s128, (1, reps))

        q_ref[...] = (rope(zq_ref[...].astype(F32)) * ATTN_SCALE).astype(BF16)
        kv = zkv_ref[...].astype(F32)
        k = rope(kv[:, :KV_DIM])
        v = kv[:, KV_DIM:]

        @pl.when(step == 0)
        def _():
            k_ref[...] = jnp.zeros_like(k_ref)
            v_ref[...] = jnp.zeros_like(v_ref)

        @pl.when(step > 0)
        def _():
            for h in range(N_KV_HEADS):
                k_ref[h] = k[:, h * HEAD_DIM:(h + 1) * HEAD_DIM].astype(BF16)
                v_ref[h] = v[:, h * HEAD_DIM:(h + 1) * HEAD_DIM].astype(BF16)

    kv_spec = pl.BlockSpec((N_KV_HEADS, BLOCK, HEAD_DIM), lambda s: (0, s, 0))
    return _pcall(
        body, after=after, name="rope_fwd", grid=(nb + 1,),
        in_specs=[pl.BlockSpec((BLOCK, Q_DIM), back), pl.BlockSpec((BLOCK, 2 * KV_DIM), back),
                  pl.BlockSpec((BLOCK, LANES), back), pl.BlockSpec((BLOCK, LANES), back)],
        out_specs=[pl.BlockSpec((BLOCK, Q_DIM), back), kv_spec, kv_spec],
        out_shape=[jax.ShapeDtypeStruct((lp, Q_DIM), BF16),
                   jax.ShapeDtypeStruct((N_KV_HEADS, lp + BLOCK, HEAD_DIM), BF16),
                   jax.ShapeDtypeStruct((N_KV_HEADS, lp + BLOCK, HEAD_DIM), BF16)],
        compiler_params=_params(("arbitrary",)),
    )(zq, zkv, ctab, stab)


def _rope_bwd(dq, dk, dv, dkm, dvm, ctab, stab):
    lp = dq.shape[0]
    width = Q_DIM + 2 * KV_DIM
    head_spec = pl.BlockSpec((N_KV_HEADS, BLOCK, HEAD_DIM), lambda i: (0, i, 0))
    meta_spec = _const_spec((N_KV_HEADS, BLOCK, HEAD_DIM))

    def body(dq_ref, dk_ref, dv_ref, dkm_ref, dvm_ref, c_ref, s_ref, dz_ref, sum_ref, kbuf, vbuf):
        i = pl.program_id(0)
        c128, s128 = c_ref[...], s_ref[...]
        first = (i == 0).astype(F32)

        def rope_t(x):
            reps = x.shape[1] // LANES
            return x * jnp.tile(c128, (1, reps)) + _swap_halves(x * jnp.tile(s128, (1, reps)))

        for h in range(N_KV_HEADS):
            kbuf[:, h * HEAD_DIM:(h + 1) * HEAD_DIM] = dk_ref[h] + first * dkm_ref[h]
            vbuf[:, h * HEAD_DIM:(h + 1) * HEAD_DIM] = dv_ref[h] + first * dvm_ref[h]
        dzq = rope_t(dq_ref[...] * ATTN_SCALE)
        dzk = rope_t(kbuf[...])
        dzv = vbuf[...]
        dz_ref[:, 0:Q_DIM] = dzq.astype(BF16)
        dz_ref[:, Q_DIM:Q_DIM + KV_DIM] = dzk.astype(BF16)
        dz_ref[:, Q_DIM + KV_DIM:width] = dzv.astype(BF16)

        @pl.when(i == 0)
        def _():
            sum_ref[...] = jnp.zeros_like(sum_ref)

        sum_ref[:, 0:Q_DIM] += jnp.sum(dzq, axis=0, keepdims=True)
        sum_ref[:, Q_DIM:Q_DIM + KV_DIM] += jnp.sum(dzk, axis=0, keepdims=True)
        sum_ref[:, Q_DIM + KV_DIM:width] += jnp.sum(dzv, axis=0, keepdims=True)

    return _pcall(
        body, name="rope_bwd", grid=(lp // BLOCK,),
        in_specs=[_row_spec(Q_DIM), head_spec, head_spec, meta_spec, meta_spec, _row_spec(LANES), _row_spec(LANES)],
        out_specs=[_row_spec(width), _const_spec((1, width))],
        out_shape=[jax.ShapeDtypeStruct((lp, width), BF16), jax.ShapeDtypeStruct((1, width), F32)],
        scratch_shapes=[pltpu.VMEM((BLOCK, KV_DIM), F32), pltpu.VMEM((BLOCK, KV_DIM), F32)],
        compiler_params=_params(("arbitrary",)),
    )(dq, dk, dv, dkm, dvm, ctab, stab)


def _attn_bias(i):
    r = lax.broadcasted_iota(jnp.int32, (BLOCK, 3 * BLOCK), 0)
    c = lax.broadcasted_iota(jnp.int32, (BLOCK, 3 * BLOCK), 1)
    qp = i * BLOCK + r - PAD_ROWS
    kp = (i - 1) * BLOCK + c - PAD_ROWS
    band = (c < 2 * BLOCK) & (kp >= N_META) & (kp <= qp) & (qp - kp < WINDOW)
    mp = c - 2 * BLOCK - PAD_ROWS
    meta = (c >= 2 * BLOCK) & (mp >= 0) & (mp <= qp)
    return jnp.where(band | meta, 0.0, NEG).astype(F32)


def _stack_heads(ref, h):
    return jnp.concatenate(
        [ref[:, (h * GROUP + g) * HEAD_DIM:(h * GROUP + g + 1) * HEAD_DIM] for g in range(GROUP)], axis=0)


def _attn_probs(qs, k3, bias8, sink):
    s = lax.dot_general(qs, k3, (((1,), (1,)), ((), ())), preferred_element_type=F32) + bias8
    m = jnp.maximum(jnp.max(s, axis=1, keepdims=True), sink)
    p = jnp.exp(s - m)
    ps = jnp.exp(sink - m)
    inv = 1.0 / (jnp.sum(p, axis=1, keepdims=True) + ps)
    return p * inv, ps * inv


def _sink_column(sink_ref, h):
    return jnp.concatenate(
        [jnp.broadcast_to(sink_ref[0:1, h * GROUP + g:h * GROUP + g + 1], (BLOCK, 1)) for g in range(GROUP)], axis=0)


def _attn_fwd(q, k_sh, v_sh, sinks):
    lp = q.shape[0]
    nb = lp // BLOCK
    kv = lambda f: pl.BlockSpec((N_KV_HEADS, BLOCK, HEAD_DIM), f)

    def body(q_ref, kp_ref, kc_ref, km_ref, vp_ref, vc_ref, vm_ref, sink_ref, o_ref):
        i = pl.program_id(0)
        bias8 = jnp.tile(_attn_bias(i), (GROUP, 1))
        for h in range(N_KV_HEADS):
            k3 = jnp.concatenate([kp_ref[h], kc_ref[h], km_ref[h]], axis=0)
            v3 = jnp.concatenate([vp_ref[h], vc_ref[h], vm_ref[h]], axis=0)
            qs = _stack_heads(q_ref, h)
            p, _ = _attn_probs(qs, k3, bias8, _sink_column(sink_ref, h))
            o = jnp.dot(p.astype(BF16), v3, preferred_element_type=F32)
            for g in range(GROUP):
                n = h * GROUP + g
                o_ref[:, n * HEAD_DIM:(n + 1) * HEAD_DIM] = o[g * BLOCK:(g + 1) * BLOCK].astype(BF16)

    prev, cur, meta = (lambda i: (0, i, 0)), (lambda i: (0, i + 1, 0)), (lambda i: (0, 1, 0))
    return _pcall(
        body, name="attn_fwd", grid=(nb,),
        in_specs=[_row_spec(Q_DIM), kv(prev), kv(cur), kv(meta), kv(prev), kv(cur), kv(meta), _const_spec((1, N_Q_HEADS))],
        out_specs=_row_spec(Q_DIM), out_shape=jax.ShapeDtypeStruct((lp, Q_DIM), BF16),
        compiler_params=_params(("parallel",)),
    )(q, k_sh, k_sh, k_sh, v_sh, v_sh, v_sh, sinks)


def _attn_bwd(q, k_sh, v_sh, sinks, do):
    lp = q.shape[0]
    nb = lp // BLOCK
    kv = lambda f: pl.BlockSpec((N_KV_HEADS, BLOCK, HEAD_DIM), f)
    cl = lambda s: jnp.minimum(s, nb - 1)

    def body(q_ref, do_ref, kp_ref, kc_ref, km_ref, vp_ref, vc_ref, vm_ref, sink_ref,
             dq_ref, dk_ref, dv_ref, dkm_ref, dvm_ref, dsink_ref, carry_k, carry_v):
        step = pl.program_id(0)

        @pl.when(step == 0)
        def _():
            carry_k[...] = jnp.zeros_like(carry_k)
            carry_v[...] = jnp.zeros_like(carry_v)
            dkm_ref[...] = jnp.zeros_like(dkm_ref)
            dvm_ref[...] = jnp.zeros_like(dvm_ref)
            dsink_ref[...] = jnp.zeros_like(dsink_ref)

        @pl.when(step < nb)
        def _():
            bias8 = jnp.tile(_attn_bias(step), (GROUP, 1))
            lane = lax.broadcasted_iota(jnp.int32, (1, LANES), 1)
            dsink = jnp.zeros((1, LANES), F32)
            for h in range(N_KV_HEADS):
                k3 = jnp.concatenate([kp_ref[h], kc_ref[h], km_ref[h]], axis=0)
                v3 = jnp.concatenate([vp_ref[h], vc_ref[h], vm_ref[h]], axis=0)
                qs = _stack_heads(q_ref, h)
                dos = _stack_heads(do_ref, h)
                p, psink = _attn_probs(qs, k3, bias8, _sink_column(sink_ref, h))
                dp = lax.dot_general(dos, v3, (((1,), (1,)), ((), ())), preferred_element_type=F32)
                delta = jnp.sum(p * dp, axis=1, keepdims=True)
                ds = (p * (dp - delta)).astype(BF16)
                dsk = -psink * delta
                for g in range(GROUP):
                    val = jnp.sum(dsk[g * BLOCK:(g + 1) * BLOCK], axis=0, keepdims=True)
                    dsink = dsink + jnp.where(lane == h * GROUP + g, val, 0.0)
                dqs = jnp.dot(ds, k3, preferred_element_type=F32)
                for g in range(GROUP):
                    n = h * GROUP + g
                    dq_ref[:, n * HEAD_DIM:(n + 1) * HEAD_DIM] = dqs[g * BLOCK:(g + 1) * BLOCK]
                dk3 = lax.dot_general(ds, qs, (((0,), (0,)), ((), ())), preferred_element_type=F32)
                dv3 = lax.dot_general(p.astype(BF16), dos, (((0,), (0,)), ((), ())), preferred_element_type=F32)
                dk_ref[h] = carry_k[h] + dk3[0:BLOCK]
                dv_ref[h] = carry_v[h] + dv3[0:BLOCK]
                carry_k[h] = dk3[BLOCK:2 * BLOCK]
                carry_v[h] = dv3[BLOCK:2 * BLOCK]
                dkm_ref[h] += dk3[2 * BLOCK:3 * BLOCK]
                dvm_ref[h] += dv3[2 * BLOCK:3 * BLOCK]
            dsink_ref[...] += dsink

        @pl.when(step == nb)
        def _():
            dk_ref[...] = carry_k[...]
            dv_ref[...] = carry_v[...]

    prev, cur, meta = (lambda s: (0, cl(s), 0)), (lambda s: (0, cl(s) + 1, 0)), (lambda s: (0, 1, 0))
    lag = lambda s: (0, jnp.maximum(s - 1, 0), 0)
    head_shape = jax.ShapeDtypeStruct((N_KV_HEADS, lp, HEAD_DIM), F32)
    meta_shape = jax.ShapeDtypeStruct((N_KV_HEADS, BLOCK, HEAD_DIM), F32)
    return _pcall(
        body, name="attn_bwd", grid=(nb + 1,),
        in_specs=[pl.BlockSpec((BLOCK, Q_DIM), lambda s: (cl(s), 0)), pl.BlockSpec((BLOCK, Q_DIM), lambda s: (cl(s), 0)),
                  kv(prev), kv(cur), kv(meta), kv(prev), kv(cur), kv(meta), _const_spec((1, N_Q_HEADS))],
        out_specs=[pl.BlockSpec((BLOCK, Q_DIM), lambda s: (cl(s), 0)), kv(lag), kv(lag),
                   _const_spec((N_KV_HEADS, BLOCK, HEAD_DIM)), _const_spec((N_KV_HEADS, BLOCK, HEAD_DIM)), _const_spec((1, LANES))],
        out_shape=[jax.ShapeDtypeStruct((lp, Q_DIM), F32), head_shape, head_shape, meta_shape, meta_shape,
                   jax.ShapeDtypeStruct((1, LANES), F32)],
        scratch_shapes=[pltpu.VMEM((N_KV_HEADS, BLOCK, HEAD_DIM), F32), pltpu.VMEM((N_KV_HEADS, BLOCK, HEAD_DIM), F32)],
        compiler_params=_params(("arbitrary",)),
    )(q, do, k_sh, k_sh, k_sh, v_sh, v_sh, v_sh, sinks)


CONV_CHUNK = 256


SUBLANES = 8
SH_BASE = BLOCK - 4 * SUBLANES
SH_ROWS = BLOCK + 3 * SUBLANES
DSH_ROWS = SH_ROWS


def _shifted_windows(src, sh, base, rows):
    for b in range(1, SUBLANES):
        sh[b] = src[base + b:base + b + rows, :]


def _window(src, sh, base, start, cols):
    a, b = divmod(start - base, SUBLANES)
    if b == 0:
        return src[start:start + BLOCK, cols]
    return sh[b, SUBLANES * a:SUBLANES * a + BLOCK, cols]


def _glu_masked(a_ref, g_ref, base):
    rows = base + lax.broadcasted_iota(jnp.int32, (BLOCK, 1), 0)
    return jnp.where(rows >= PAD_ROWS, a_ref[...].astype(F32) * _sigmoid(g_ref[...].astype(F32)), 0.0)


def _conv_fwd(zc, conv_w, conv_b, ln_g, ln_b, after=None):
    lp = zc.shape[0]
    cd = zc.shape[1] // 2
    nb = lp // BLOCK
    chunk = min(CONV_CHUNK, cd)
    back = lambda col: (lambda i: (jnp.maximum(i - 1, 0), col))
    lo = BLOCK - (CONV_WIDTH - 1)

    def body(ap_ref, gp_ref, ac_ref, gc_ref, w_ref, b_ref, lg_ref, lb_ref, co_ref, c2_ref, ext, sh):
        i = pl.program_id(0)
        ext[0:BLOCK, :] = _glu_masked(ap_ref, gp_ref, (i - 1) * BLOCK)
        ext[BLOCK:2 * BLOCK, :] = _glu_masked(ac_ref, gc_ref, i * BLOCK)
        _shifted_windows(ext, sh, SH_BASE, SH_ROWS)
        for c0 in range(0, cd, chunk):
            cols = slice(c0, c0 + chunk)
            acc = jnp.zeros((BLOCK, chunk), F32)
            for k in range(CONV_WIDTH):
                acc = acc + _window(ext, sh, SH_BASE, lo + k, cols) * w_ref[k:k + 1, cols]
            co_ref[:, cols] = acc + b_ref[:, cols]
        x = co_ref[...]
        mu = jnp.mean(x, axis=-1, keepdims=True)
        xc = x - mu
        r = lax.rsqrt(jnp.mean(xc * xc, axis=-1, keepdims=True) + EPS)
        y = xc * r * lg_ref[...] + lb_ref[...]
        c2_ref[...] = (y * _sigmoid(y)).astype(BF16)

    return _pcall(
        body, after=after, name="conv_fwd", grid=(nb,),
        in_specs=[pl.BlockSpec((BLOCK, cd), back(0)), pl.BlockSpec((BLOCK, cd), back(1)), _row_spec(cd, 0), _row_spec(cd, 1),
                  _const_spec((CONV_ROWS, cd)), _const_spec((1, cd)), _const_spec((1, cd)), _const_spec((1, cd))],
        out_specs=[_row_spec(cd), _row_spec(cd)],
        out_shape=[jax.ShapeDtypeStruct((lp, cd), F32), jax.ShapeDtypeStruct((lp, cd), BF16)],
        scratch_shapes=[pltpu.VMEM((2 * BLOCK, cd), F32), pltpu.VMEM((SUBLANES, SH_ROWS, cd), F32)],
        compiler_params=_params(("arbitrary",)),
    )(zc, zc, zc, zc, conv_w, conv_b, ln_g, ln_b)


def _conv_bwd_norm(dc2, conv_out, ln_g, ln_b):
    lp, cd = conv_out.shape

    def body(d_ref, x_ref, lg_ref, lb_ref, dco_ref, dlg_ref, dlb_ref, dcb_ref):
        i = pl.program_id(0)
        x = x_ref[...]
        g = lg_ref[...]
        mu = jnp.mean(x, axis=-1, keepdims=True)
        xc = x - mu
        r = lax.rsqrt(jnp.mean(xc * xc, axis=-1, keepdims=True) + EPS)
        xhat = xc * r
        y = xhat * g + lb_ref[...]
        sg = _sigmoid(y)
        dy = d_ref[...] * (sg * (1.0 + y * (1.0 - sg)))
        dxhat = dy * g
        dx = r * (dxhat - jnp.mean(dxhat, axis=-1, keepdims=True) - xhat * jnp.mean(dxhat * xhat, axis=-1, keepdims=True))
        dco_ref[...] = dx

        @pl.when(i == 0)
        def _():
            dlg_ref[...] = jnp.zeros_like(dlg_ref)
            dlb_ref[...] = jnp.zeros_like(dlb_ref)
            dcb_ref[...] = jnp.zeros_like(dcb_ref)

        dlg_ref[...] += jnp.sum(dy * xhat, axis=0, keepdims=True)
        dlb_ref[...] += jnp.sum(dy, axis=0, keepdims=True)
        dcb_ref[...] += jnp.sum(dx, axis=0, keepdims=True)

    vec = jax.ShapeDtypeStruct((1, cd), F32)
    return _pcall(
        body, name="conv_bwd_norm", grid=(lp // BLOCK,),
        in_specs=[_row_spec(cd), _row_spec(cd), _const_spec((1, cd)), _const_spec((1, cd))],
        out_specs=[_row_spec(cd), _const_spec((1, cd)), _const_spec((1, cd)), _const_spec((1, cd))],
        out_shape=[jax.ShapeDtypeStruct((lp, cd), F32), vec, vec, vec],
        compiler_params=_params(("arbitrary",)),
    )(dc2, conv_out, ln_g, ln_b)


def _conv_bwd_taps(dco, zc, conv_w):
    lp, cd = dco.shape
    nb = lp // BLOCK
    chunk = min(CONV_CHUNK, cd)
    back = lambda col: (lambda i: (jnp.maximum(i - 1, 0), col))
    fwd = lambda i: (jnp.minimum(i + 1, nb - 1), 0)
    lo = BLOCK - (CONV_WIDTH - 1)

    def body(dc_ref, dn_ref, ap_ref, gp_ref, ac_ref, gc_ref, w_ref, dz_ref, sum_ref, dw_ref, ext, dext, dcb, sh, dsh):
        i = pl.program_id(0)
        ext[0:BLOCK, :] = _glu_masked(ap_ref, gp_ref, (i - 1) * BLOCK)
        ext[BLOCK:2 * BLOCK, :] = _glu_masked(ac_ref, gc_ref, i * BLOCK)
        dext[0:BLOCK, :] = dc_ref[...]
        dext[BLOCK:2 * BLOCK, :] = dn_ref[...] * (i < nb - 1).astype(F32)
        _shifted_windows(ext, sh, SH_BASE, SH_ROWS)
        _shifted_windows(dext, dsh, 0, DSH_ROWS)

        @pl.when(i == 0)
        def _():
            dw_ref[...] = jnp.zeros_like(dw_ref)
            sum_ref[...] = jnp.zeros_like(sum_ref)

        for c0 in range(0, cd, chunk):
            cols = slice(c0, c0 + chunk)
            dcur = dext[0:BLOCK, cols]
            acc = jnp.zeros((BLOCK, chunk), F32)
            for k in range(CONV_WIDTH):
                s = CONV_WIDTH - 1 - k
                acc = acc + _window(dext, dsh, 0, s, cols) * w_ref[k:k + 1, cols]
                dw_ref[k:k + 1, cols] += jnp.sum(dcur * _window(ext, sh, SH_BASE, lo + k, cols), axis=0, keepdims=True)
            dcb[:, cols] = acc
        rows = i * BLOCK + lax.broadcasted_iota(jnp.int32, (BLOCK, 1), 0)
        dc = jnp.where(rows >= PAD_ROWS, dcb[...], 0.0)
        a = ac_ref[...].astype(F32)
        sg = _sigmoid(gc_ref[...].astype(F32))
        da = dc * sg
        dg = dc * a * sg * (1.0 - sg)
        dz_ref[:, 0:cd] = da.astype(BF16)
        dz_ref[:, cd:2 * cd] = dg.astype(BF16)
        sum_ref[:, 0:cd] += jnp.sum(da, axis=0, keepdims=True)
        sum_ref[:, cd:2 * cd] += jnp.sum(dg, axis=0, keepdims=True)

    return _pcall(
        body, name="conv_bwd_taps", grid=(nb,),
        in_specs=[_row_spec(cd), pl.BlockSpec((BLOCK, cd), fwd),
                  pl.BlockSpec((BLOCK, cd), back(0)), pl.BlockSpec((BLOCK, cd), back(1)), _row_spec(cd, 0), _row_spec(cd, 1),
                  _const_spec((CONV_ROWS, cd))],
        out_specs=[_row_spec(2 * cd), _const_spec((1, 2 * cd)), _const_spec((CONV_ROWS, cd))],
        out_shape=[jax.ShapeDtypeStruct((lp, 2 * cd), BF16), jax.ShapeDtypeStruct((1, 2 * cd), F32),
                   jax.ShapeDtypeStruct((CONV_ROWS, cd), F32)],
        scratch_shapes=[pltpu.VMEM((2 * BLOCK, cd), F32), pltpu.VMEM((2 * BLOCK, cd), F32), pltpu.VMEM((BLOCK, cd), F32),
                        pltpu.VMEM((SUBLANES, SH_ROWS, cd), F32), pltpu.VMEM((SUBLANES, DSH_ROWS, cd), F32)],
        compiler_params=_params(("arbitrary",)),
    )(dco, dco, zc, zc, zc, zc, conv_w)


def _gate_fwd(a, b, zg, after=None):
    lp, d = a.shape

    def body(a_ref, b_ref, ga_ref, gb_ref, m_ref):
        ga, gb = ga_ref[...].astype(F32), gb_ref[...].astype(F32)
        m_ref[...] = (_sigmoid(ga) * a_ref[...].astype(F32) + _sigmoid(gb) * b_ref[...].astype(F32)).astype(BF16)

    return _pcall(
        body, after=after, name="gate_fwd", grid=(lp // BLOCK,),
        in_specs=[_row_spec(d), _row_spec(d), _row_spec(d, 0), _row_spec(d, 1)], out_specs=_row_spec(d),
        out_shape=jax.ShapeDtypeStruct((lp, d), BF16), compiler_params=_params(("parallel",)),
    )(a, b, zg, zg)


def _gate_bwd(dm, a, b, zg):
    lp, d = a.shape

    def body(dm_ref, a_ref, b_ref, ga_ref, gb_ref, da_ref, db_ref, dz_ref, sum_ref, dbias_ref):
        i = pl.program_id(0)
        dm_ = dm_ref[...].astype(F32)
        sa = _sigmoid(ga_ref[...].astype(F32))
        sb = _sigmoid(gb_ref[...].astype(F32))
        db = dm_ * sb
        dga = dm_ * a_ref[...].astype(F32) * sa * (1.0 - sa)
        dgb = dm_ * b_ref[...].astype(F32) * sb * (1.0 - sb)
        da_ref[...] = (dm_ * sa).astype(BF16)
        db_ref[...] = db.astype(BF16)
        dz_ref[:, 0:d] = dga.astype(BF16)
        dz_ref[:, d:2 * d] = dgb.astype(BF16)

        @pl.when(i == 0)
        def _():
            sum_ref[...] = jnp.zeros_like(sum_ref)
            dbias_ref[...] = jnp.zeros_like(dbias_ref)

        sum_ref[:, 0:d] += jnp.sum(dga, axis=0, keepdims=True)
        sum_ref[:, d:2 * d] += jnp.sum(dgb, axis=0, keepdims=True)
        dbias_ref[...] += jnp.sum(db, axis=0, keepdims=True)

    return _pcall(
        body, name="gate_bwd", grid=(lp // BLOCK,),
        in_specs=[_row_spec(d), _row_spec(d), _row_spec(d), _row_spec(d, 0), _row_spec(d, 1)],
        out_specs=[_row_spec(d), _row_spec(d), _row_spec(2 * d), _const_spec((1, 2 * d)), _const_spec((1, d))],
        out_shape=[jax.ShapeDtypeStruct((lp, d), BF16), jax.ShapeDtypeStruct((lp, d), BF16), jax.ShapeDtypeStruct((lp, 2 * d), BF16),
                   jax.ShapeDtypeStruct((1, 2 * d), F32), jax.ShapeDtypeStruct((1, d), F32)],
        compiler_params=_params(("arbitrary",)),
    )(dm, a, b, zg, zg)


def _swiglu_fwd(gu, after=None):
    lp = gu.shape[0]
    f = gu.shape[1] // 2

    def body(g_ref, u_ref, o_ref):
        g = g_ref[...].astype(F32)
        o_ref[...] = (g * _sigmoid(g) * u_ref[...].astype(F32)).astype(BF16)

    return _pcall(
        body, after=after, name="swiglu_fwd", grid=(lp // BLOCK,), in_specs=[_row_spec(f, 0), _row_spec(f, 1)], out_specs=_row_spec(f),
        out_shape=jax.ShapeDtypeStruct((lp, f), BF16), compiler_params=_params(("parallel",)),
    )(gu, gu)


def _swiglu_bwd(dact, gu):
    lp, f = dact.shape

    def body(d_ref, g_ref, u_ref, o_ref):
        g = g_ref[...].astype(F32)
        d = d_ref[...].astype(F32)
        sg = _sigmoid(g)
        o_ref[:, 0:f] = (d * u_ref[...].astype(F32) * (sg * (1.0 + g * (1.0 - sg)))).astype(BF16)
        o_ref[:, f:2 * f] = (d * g * sg).astype(BF16)

    return _pcall(
        body, name="swiglu_bwd", grid=(lp // BLOCK,), in_specs=[_row_spec(f), _row_spec(f, 0), _row_spec(f, 1)],
        out_specs=_row_spec(2 * f), out_shape=jax.ShapeDtypeStruct((lp, 2 * f), BF16), compiler_params=_params(("parallel",)),
    )(dact, gu, gu)


ANY = pl.BlockSpec(memory_space=pl.ANY)


def _all_gather_rows(x, name, after=None):
    r, c = x.shape

    def body(x_ref, out_ref, send_sems, recv_sems, local_sem):
        mx, my, mc = lax.axis_index("x"), lax.axis_index("y"), lax.axis_index("c")
        me, sibling = (mx, my, mc), (mx, my, 1 - mc)
        chips = [(1 - mx, my), (mx, 1 - my), (1 - mx, 1 - my)]

        def rows(px, py, pc):
            return out_ref.at[pl.ds((4 * px + 2 * py + pc) * r, r), :]

        def copy(k, block, to, src=None):
            return pltpu.make_async_remote_copy(
                src_ref=rows(*block) if src is None else src, dst_ref=rows(*block),
                send_sem=send_sems.at[k], recv_sem=recv_sems.at[k], device_id=to, device_id_type=MESH)

        mine = pltpu.make_async_copy(x_ref, rows(*me), local_sem)
        mine.start()
        first = [copy(0, me, sibling, src=x_ref)]
        first += [copy(1 + j, me, (*chip, mc), src=x_ref) for j, chip in enumerate(chips)]
        for cp in first:
            cp.start()
        passed = [copy(4 + j, (*chip, mc), sibling) for j, chip in enumerate(chips)]
        for j, chip in enumerate(chips):
            copy(1 + j, (*chip, mc), me).wait_recv()
            passed[j].start()
        copy(0, sibling, me).wait_recv()
        for j, chip in enumerate(chips):
            copy(4 + j, (*chip, 1 - mc), me).wait_recv()
        for cp in first + passed:
            cp.wait_send()
        mine.wait()

    return _pcall(
        body, after=after, name=name, in_specs=[ANY], out_specs=ANY, out_shape=jax.ShapeDtypeStruct((N_DEV * r, c), x.dtype),
        scratch_shapes=[pltpu.SemaphoreType.DMA((7,)), pltpu.SemaphoreType.DMA((7,)), pltpu.SemaphoreType.DMA(())],
    )(x)


HBM = pl.BlockSpec(memory_space=pltpu.HBM)
SEM = pl.BlockSpec(memory_space=pltpu.SEMAPHORE)
IN_FLIGHT = pltpu.CompilerParams(has_side_effects=pltpu.SideEffectType.DATAFLOW_SIDE_EFFECTING)
N_PEERS = 4


def _place_rows(shard, after, name):
    r, c = shard.shape
    tr = _pick(r, max(16, ELEMENTWISE_BLOCK_BYTES // (4 * c)), 16)
    steps = r // tr
    dev = (4 * lax.axis_index("x") + 2 * lax.axis_index("y") + lax.axis_index("c")).astype(jnp.int32).reshape(1)

    def body(dev_ref, x_ref, after_ref, o_ref):
        o_ref[...] = x_ref[...].astype(BF16)

    return _pcall(
        body, name=name,
        grid_spec=pltpu.PrefetchScalarGridSpec(
            num_scalar_prefetch=1, grid=(steps,),
            in_specs=[pl.BlockSpec((tr, c), lambda i, dev_ref: (i, 0)), pl.BlockSpec(memory_space=pl.ANY)],
            out_specs=pl.BlockSpec((tr, c), lambda i, dev_ref: (dev_ref[0] * steps + i, 0))),
        out_shape=jax.ShapeDtypeStruct((N_DEV * r, c), BF16), compiler_params=_params(("parallel",)),
    )(dev, shard, after)


def _gather_start(full, name):
    r = full.shape[0] // N_DEV

    def body(full_ref, send_sems, recv_sems, out_ref):
        mx, my, mc = lax.axis_index("x"), lax.axis_index("y"), lax.axis_index("c")
        mine = full_ref.at[pl.ds((4 * mx + 2 * my + mc) * r, r), :]
        for k, peer in enumerate([(mx, my, 1 - mc), (1 - mx, my, mc), (mx, 1 - my, mc), (1 - mx, 1 - my, mc)]):
            pltpu.make_async_remote_copy(
                src_ref=mine, dst_ref=mine, send_sem=send_sems.at[k], recv_sem=recv_sems.at[k],
                device_id=peer, device_id_type=MESH).start()

    return pl.pallas_call(
        body, name=name, in_specs=[HBM], out_specs=(SEM, SEM, HBM),
        out_shape=(pltpu.SemaphoreType.DMA((N_PEERS,)), pltpu.SemaphoreType.DMA((N_PEERS,)), pltpu.HBM(full.shape, full.dtype)),
        input_output_aliases={0: 2}, compiler_params=IN_FLIGHT,
    )(pltpu.with_memory_space_constraint(full, pltpu.HBM))


def _gather_wait(full, send_sem, recv_sem, after, name):
    r = full.shape[0] // N_DEV

    def body(full_ref, send_ref, recv_ref, after_ref, out_ref):
        mx, my, mc = lax.axis_index("x"), lax.axis_index("y"), lax.axis_index("c")
        block = full_ref.at[pl.ds(0, r), :]
        for k in range(N_PEERS):
            cp = pltpu.make_async_remote_copy(
                src_ref=block, dst_ref=block, send_sem=send_ref.at[k], recv_sem=recv_ref.at[k],
                device_id=(mx, my, mc), device_id_type=MESH)
            cp.wait_send()
            cp.wait_recv()

    return pl.pallas_call(
        body, name=name, in_specs=[HBM, SEM, SEM, pl.BlockSpec(memory_space=pl.ANY)], out_specs=HBM,
        out_shape=pltpu.HBM(full.shape, full.dtype), input_output_aliases={0: 0}, compiler_params=IN_FLIGHT,
    )(full, send_sem, recv_sem, after)


def _gather_forward_start(full, name):
    r = full.shape[0] // N_DEV

    def body(full_ref, send_sems, recv_sems, out_ref):
        mx, my, mc = lax.axis_index("x"), lax.axis_index("y"), lax.axis_index("c")
        for k, (px, py) in enumerate([(1 - mx, my), (mx, 1 - my), (1 - mx, 1 - my)]):
            rows = full_ref.at[pl.ds((4 * px + 2 * py + mc) * r, r), :]
            pltpu.make_async_remote_copy(
                src_ref=rows, dst_ref=rows, send_sem=send_sems.at[k], recv_sem=recv_sems.at[k],
                device_id=(mx, my, 1 - mc), device_id_type=MESH).start()

    return pl.pallas_call(
        body, name=name, in_specs=[HBM], out_specs=(SEM, SEM, HBM),
        out_shape=(pltpu.SemaphoreType.DMA((3,)), pltpu.SemaphoreType.DMA((3,)), pltpu.HBM(full.shape, full.dtype)),
        input_output_aliases={0: 2}, compiler_params=IN_FLIGHT,
    )(full)


def _gather_forward_wait(full, send_sem, recv_sem, after, name):
    r = full.shape[0] // N_DEV

    def body(full_ref, send_ref, recv_ref, after_ref, out_ref):
        mx, my, mc = lax.axis_index("x"), lax.axis_index("y"), lax.axis_index("c")
        block = full_ref.at[pl.ds(0, r), :]
        for k in range(3):
            cp = pltpu.make_async_remote_copy(
                src_ref=block, dst_ref=block, send_sem=send_ref.at[k], recv_sem=recv_ref.at[k],
                device_id=(mx, my, mc), device_id_type=MESH)
            cp.wait_send()
            cp.wait_recv()

    return pl.pallas_call(
        body, name=name, in_specs=[HBM, SEM, SEM, pl.BlockSpec(memory_space=pl.ANY)], out_specs=HBM,
        out_shape=pltpu.HBM(full.shape, full.dtype), input_output_aliases={0: 0}, compiler_params=IN_FLIGHT,
    )(full, send_sem, recv_sem, after)


def _pair_exchange_start(g, name):
    r = g.shape[0] // N_DEV
    c = g.shape[1]
    land = (len(CHIPS), r, c)

    def body(g_ref, land_ref, send_sems, recv_sems, g_out, land_out):
        mx, my, mc = lax.axis_index("x"), lax.axis_index("y"), lax.axis_index("c")
        for j, (px, py) in enumerate(CHIPS):
            pltpu.make_async_remote_copy(
                src_ref=g_ref.at[pl.ds((4 * px + 2 * py + 1 - mc) * r, r), :], dst_ref=land_ref.at[j],
                send_sem=send_sems.at[j], recv_sem=recv_sems.at[j], device_id=(mx, my, 1 - mc), device_id_type=MESH).start()

    return pl.pallas_call(
        body, name=name, in_specs=[HBM, HBM], out_specs=(SEM, SEM, HBM, HBM),
        out_shape=(pltpu.SemaphoreType.DMA((4,)), pltpu.SemaphoreType.DMA((4,)), pltpu.HBM(g.shape, g.dtype), pltpu.HBM(land, g.dtype)),
        input_output_aliases={0: 2, 1: 3}, compiler_params=IN_FLIGHT,
    )(pltpu.with_memory_space_constraint(g, pltpu.HBM), pltpu.with_memory_space_constraint(lax.empty(land, g.dtype), pltpu.HBM))


def _pair_exchange_wait(send_sem, recv_sem, g, land, after, name):
    def body(g_ref, land_ref, send_ref, recv_ref, after_ref, g_out, land_out):
        mx, my, mc = lax.axis_index("x"), lax.axis_index("y"), lax.axis_index("c")
        for j in range(len(CHIPS)):
            cp = pltpu.make_async_remote_copy(
                src_ref=land_ref.at[0], dst_ref=land_ref.at[0], send_sem=send_ref.at[j], recv_sem=recv_ref.at[j],
                device_id=(mx, my, mc), device_id_type=MESH)
            cp.wait_send()
            cp.wait_recv()

    return pl.pallas_call(
        body, name=name, in_specs=[HBM, HBM, SEM, SEM, pl.BlockSpec(memory_space=pl.ANY)], out_specs=(HBM, HBM),
        out_shape=(pltpu.HBM(g.shape, g.dtype), pltpu.HBM(land.shape, land.dtype)), input_output_aliases={0: 0, 1: 1},
        compiler_params=IN_FLIGHT,
    )(g, land, send_sem, recv_sem, after)


def _chip_exchange_start(ps, after, name):
    def body(ps_ref, rx_ref, after_ref, send_sems, recv_sems, ps_out, rx_out):
        mx, my, mc = lax.axis_index("x"), lax.axis_index("y"), lax.axis_index("c")
        chips = [(1 - mx, my), (mx, 1 - my), (1 - mx, 1 - my)]
        for k, (px, py) in enumerate(chips):
            pltpu.make_async_remote_copy(
                src_ref=ps_ref.at[2 * px + py], dst_ref=rx_ref.at[2 * mx + my], send_sem=send_sems.at[k], recv_sem=recv_sems.at[k],
                device_id=(px, py, mc), device_id_type=MESH).start()

    return pl.pallas_call(
        body, name=name, in_specs=[HBM, HBM, pl.BlockSpec(memory_space=pl.ANY)], out_specs=(SEM, SEM, HBM, HBM),
        out_shape=(pltpu.SemaphoreType.DMA((3,)), pltpu.SemaphoreType.DMA((3,)), pltpu.HBM(ps.shape, ps.dtype), pltpu.HBM(ps.shape, ps.dtype)),
        input_output_aliases={0: 2, 1: 3}, compiler_params=IN_FLIGHT,
    )(pltpu.with_memory_space_constraint(ps, pltpu.HBM), pltpu.with_memory_space_constraint(lax.empty(ps.shape, ps.dtype), pltpu.HBM), after)


def _chip_exchange_wait(send_sem, recv_sem, ps, rx, after, name):
    def body(ps_ref, rx_ref, send_ref, recv_ref, after_ref, ps_out, rx_out):
        mx, my, mc = lax.axis_index("x"), lax.axis_index("y"), lax.axis_index("c")
        for k in range(3):
            cp = pltpu.make_async_remote_copy(
                src_ref=ps_ref.at[0], dst_ref=rx_ref.at[0], send_sem=send_ref.at[k], recv_sem=recv_ref.at[k],
                device_id=(mx, my, mc), device_id_type=MESH)
            cp.wait_send()
            cp.wait_recv()

    return pl.pallas_call(
        body, name=name, in_specs=[HBM, HBM, SEM, SEM, pl.BlockSpec(memory_space=pl.ANY)], out_specs=(HBM, HBM),
        out_shape=(pltpu.HBM(ps.shape, ps.dtype), pltpu.HBM(rx.shape, rx.dtype)), input_output_aliases={0: 0, 1: 1},
        compiler_params=IN_FLIGHT,
    )(ps, rx, send_sem, recv_sem, after)


def _sum_chips(ps, rx, name):
    n, r, c = rx.shape
    tr = _pick(r, max(8, ELEMENTWISE_BLOCK_BYTES // (4 * n * c)), 8)
    chip = (2 * lax.axis_index("x") + lax.axis_index("y")).astype(jnp.int32).reshape(1)

    def body(chip_ref, own_ref, x_ref, o_ref):
        me = chip_ref[0]
        own = own_ref[0].astype(F32)
        acc = jnp.where(me == 0, own, x_ref[0].astype(F32))
        for j in range(1, n):
            acc = acc + jnp.where(me == j, own, x_ref[j].astype(F32))
        o_ref[...] = acc

    return _pcall(
        body, name=name,
        grid_spec=pltpu.PrefetchScalarGridSpec(
            num_scalar_prefetch=1, grid=(r // tr,),
            in_specs=[pl.BlockSpec((1, tr, c), lambda i, chip_ref: (chip_ref[0], i, 0)), pl.BlockSpec((n, tr, c), lambda i, chip_ref: (0, i, 0))],
            out_specs=pl.BlockSpec((tr, c), lambda i, chip_ref: (i, 0))),
        out_shape=jax.ShapeDtypeStruct((r, c), F32), compiler_params=_params(("parallel",)),
    )(chip, ps, rx)


def _pair_exchange(g, name):
    r = g.shape[0] // N_DEV
    c = g.shape[1]

    def body(g_ref, theirs_ref, send_sems, recv_sems):
        mx, my, mc = lax.axis_index("x"), lax.axis_index("y"), lax.axis_index("c")
        sibling = (mx, my, 1 - mc)
        copies = []
        for j, (px, py) in enumerate(CHIPS):
            give = g_ref.at[pl.ds((4 * px + 2 * py + 1 - mc) * r, r), :]
            rc = pltpu.make_async_remote_copy(
                src_ref=give, dst_ref=theirs_ref.at[j], send_sem=send_sems.at[j], recv_sem=recv_sems.at[j],
                device_id=sibling, device_id_type=MESH)
            rc.start()
            copies.append(rc)
        for cp in copies:
            cp.wait()

    return _pcall(
        body, name=name, in_specs=[ANY], out_specs=ANY, out_shape=jax.ShapeDtypeStruct((len(CHIPS), r, c), g.dtype),
        scratch_shapes=[pltpu.SemaphoreType.DMA((4,)), pltpu.SemaphoreType.DMA((4,))],
    )(g)


def _pair_sum(g, theirs, name):
    nch, r, c = theirs.shape
    tr = _pick(r, max(16, 2 * ELEMENTWISE_BLOCK_BYTES // (2 * c)), 16)
    core = lax.axis_index("c").astype(jnp.int32).reshape(1)

    def body(core_ref, a_ref, b_ref, o_ref):
        o_ref[...] = (a_ref[...].astype(F32) + b_ref[...].astype(F32)).astype(o_ref.dtype)

    spec = pl.BlockSpec((1, tr, c), lambda j, i, core_ref: (j, i, 0))
    own = pl.BlockSpec((1, tr, c), lambda j, i, core_ref: (2 * j + core_ref[0], i, 0))
    return _pcall(
        body, name=name,
        grid_spec=pltpu.PrefetchScalarGridSpec(num_scalar_prefetch=1, grid=(nch, r // tr), in_specs=[own, spec], out_specs=spec),
        out_shape=jax.ShapeDtypeStruct(theirs.shape, theirs.dtype), compiler_params=_params(("parallel", "parallel")),
    )(core, g.reshape(N_DEV, r, c), theirs)


def _sum_blocks(rx, name):
    n, r, c = rx.shape
    tr = _pick(r, max(8, ELEMENTWISE_BLOCK_BYTES // (4 * n * c)), 8)

    def body(x_ref, o_ref):
        acc = x_ref[0].astype(F32)
        for j in range(1, n):
            acc = acc + x_ref[j].astype(F32)
        o_ref[...] = acc

    return _pcall(
        body, name=name, grid=(r // tr,), in_specs=[pl.BlockSpec((n, tr, c), lambda i: (0, i, 0))],
        out_specs=pl.BlockSpec((tr, c), lambda i: (i, 0)), out_shape=jax.ShapeDtypeStruct((r, c), F32),
        compiler_params=_params(("parallel",)),
    )(rx)


def _adamw(w, g, m, v, name):
    r, c = w.shape
    tr = _pick(r, max(8, ELEMENTWISE_BLOCK_BYTES // (4 * c)), 8)
    c1 = 1.0 - ADAM_B1 ** ADAM_STEP
    c2 = 1.0 - ADAM_B2 ** ADAM_STEP

    def body(w_ref, g_ref, m_ref, v_ref, d_ref, nm_ref, nv_ref):
        gg = g_ref[...]
        nm = ADAM_B1 * m_ref[...] + (1.0 - ADAM_B1) * gg
        nv = ADAM_B2 * v_ref[...] + (1.0 - ADAM_B2) * (gg * gg)
        d_ref[...] = -ADAM_LR * ((nm / c1) / (jnp.sqrt(nv / c2) + ADAM_EPS) + ADAM_WD * w_ref[...])
        nm_ref[...] = nm
        nv_ref[...] = nv

    spec = pl.BlockSpec((tr, c), lambda i: (i, 0))
    shp = jax.ShapeDtypeStruct((r, c), F32)
    return _pcall(
        body, name=name, grid=(r // tr,), in_specs=[spec] * 4, out_specs=[spec] * 3, out_shape=[shp] * 3,
        compiler_params=_params(("parallel",)),
    )(w, g, m, v)


def _pack(parts):
    flat, layout, row = [], [], 0
    for p in parts:
        n = p.size
        rows = -(-n // LANES)
        flat.append(jnp.pad(p.reshape(-1).astype(F32), (0, rows * LANES - n)))
        layout.append((row, n, p.shape))
        row += rows
    total = -(-row // 8) * 8
    if total > row:
        flat.append(jnp.zeros(((total - row) * LANES,), F32))
    return jnp.concatenate(flat).reshape(total, LANES), layout


def _unpack(slab, layout):
    flat = slab.reshape(-1)
    return [flat[row * LANES:row * LANES + n].reshape(shape) for row, n, shape in layout]


def kernel(x, meta_tokens, mix_norm_g, w_in, b_in, attn_sinks, conv_w, conv_b, conv_ln_g, conv_ln_b, w_attn_o, w_conv_o, b_conv_o, w_out, ffn_norm_g, w_gate_up, w_down, final_norm_g, loss_target, m_meta_tokens, m_mix_norm_g, m_w_in, m_b_in, m_attn_sinks, m_conv_w, m_conv_b, m_conv_ln_g, m_conv_ln_b, m_w_attn_o, m_w_conv_o, m_b_conv_o, m_w_out, m_ffn_norm_g, m_w_gate_up, m_w_down, m_final_norm_g, v_meta_tokens, v_mix_norm_g, v_w_in, v_b_in, v_attn_sinks, v_conv_w, v_conv_b, v_conv_ln_g, v_conv_ln_b, v_w_attn_o, v_w_conv_o, v_b_conv_o, v_w_out, v_ffn_norm_g, v_w_gate_up, v_w_down, v_final_norm_g):
    xs = x[0]
    tgt = loss_target[0]
    s, d = xs.shape
    lp = s + BLOCK
    cd = conv_b.shape[1]
    ffn = w_down.shape[1] * N_DEV
    dev = 4 * lax.axis_index("x") + 2 * lax.axis_index("y") + lax.axis_index("c")
    cw_cols = conv_w.shape[3]
    meta_cols = meta_tokens.shape[1]

    small, small_layout = _pack([meta_tokens, jnp.pad(conv_w[0, :, 0, :], ((0, CONV_ROWS - CONV_WIDTH), (0, 0)))])
    small_flat = _all_gather_rows(small, "gather_small")
    small_all = small_flat.reshape(N_DEV, *small.shape)
    meta_parts, cw_parts = zip(*[_unpack(small_all[j], small_layout) for j in range(N_DEV)])
    meta_full = jnp.concatenate(meta_parts, axis=1)
    conv_w_full = jnp.concatenate(cw_parts, axis=1)
    g_send, g_recv, g_full = [], [], []
    tok = small_flat
    for shard, name in ((w_in[0].T, "w_in"), (w_attn_o[0].T, "w_attn_o"), (w_conv_o[0].T, "w_conv_o"), (w_out[0], "w_out"),
                        (w_gate_up[0].T, "w_gate_up"), (w_down[0], "w_down")):
        send_sem, recv_sem, tok = _gather_start(_place_rows(shard, tok, "place_" + name), "gather_start_" + name)
        g_send.append(send_sem)
        g_recv.append(recv_sem)
        g_full.append(tok)

    def arrived(w, after, name):
        full = _gather_wait(g_full[w], g_send[w], g_recv[w], after, "gather_wait_" + name)
        return _gather_forward_start(full, "gather_forward_start_" + name)

    def whole(passing, after, name):
        return _gather_forward_wait(passing[2], passing[0], passing[1], after, "gather_forward_wait_" + name)

    ctab, stab = _rope_tables(lp)
    mm = functools.partial(_matmul, tm=1056, tn=1024)

    passing = arrived(0, tok, "w_in")
    h0, u = _prep(xs, meta_full, mix_norm_g, after=passing[2])
    win_t = whole(passing, u, "w_in")
    bq, bkv, bc, bg = b_in[:, :Q_DIM], b_in[:, Q_DIM:Q_DIM + 2 * KV_DIM], b_in[:, Q_DIM + 2 * KV_DIM:Q_DIM + 2 * KV_DIM + 2 * cd], b_in[:, Q_DIM + 2 * KV_DIM + 2 * cd:]
    o_kv, o_c, o_g = Q_DIM, Q_DIM + 2 * KV_DIM, Q_DIM + 2 * KV_DIM + 2 * cd
    in_proj = functools.partial(_matmul, u, win_t, mode="nt", out_dtype=BF16, tm=2112, tn=512, tk=d)
    zq = in_proj(name="in_proj_q", bias=bq, b_row_off=0, b_rows=Q_DIM)
    zkv = in_proj(name="in_proj_kv", bias=bkv, b_row_off=o_kv, b_rows=2 * KV_DIM)
    zc = in_proj(name="in_proj_conv", bias=bc, b_row_off=o_c, b_rows=2 * cd)
    zg = in_proj(name="in_proj_gates", bias=bg, b_row_off=o_g, b_rows=2 * d)
    passing = arrived(1, zg, "w_attn_o")
    q_rot, k_sh, v_sh = _rope_fwd(zq, zkv, ctab, stab, after=passing[2])
    o = _attn_fwd(q_rot, k_sh, v_sh, attn_sinks)
    wao_t = whole(passing, o, "w_attn_o")
    br_a = mm(o, wao_t, mode="nt", name="attn_out_proj", out_dtype=BF16, tk=Q_DIM)
    passing = arrived(2, br_a, "w_conv_o")
    conv_out, c2 = _conv_fwd(zc, conv_w_full, conv_b, conv_ln_g, conv_ln_b, after=passing[2])
    wco_t = whole(passing, c2, "w_conv_o")
    br_b = mm(c2, wco_t, mode="nt", name="conv_out_proj", out_dtype=BF16, tk=cd, bias=b_conv_o)
    passing = arrived(3, br_b, "w_out")
    merged = _gate_fwd(br_a, br_b, zg, after=passing[2])
    wout = whole(passing, merged, "w_out")
    passing = arrived(4, wout, "w_gate_up")
    h1 = mm(merged, wout, mode="nn", name="mix_out_proj", out_dtype=F32, tn=512, tk=d, residual=h0, after=passing[2])
    u2 = _rmsnorm_fwd(h1, ffn_norm_g, "ffn_rmsnorm")
    wgu_t = whole(passing, u2, "w_gate_up")
    gu = _matmul(u2, wgu_t, mode="nt", name="ffn_gate_up", out_dtype=BF16, tm=2112, tn=512, tk=d)
    passing = arrived(5, gu, "w_down")
    act = _swiglu_fwd(gu, after=passing[2])
    wdown = whole(passing, act, "w_down")
    h2 = mm(act, wdown, mode="nn", name="ffn_down", out_dtype=F32, tn=512, tk=ffn // 2, residual=h1)
    dh2, dh2_b, loss_part, d_final_g = _final(h2, tgt, final_norm_g.reshape(1, d))

    wgrad = functools.partial(_matmul, mode="tn", out_dtype=BF16, tk=lp, tn=2048, b_inner=False)
    in_flight = {}

    def scatter_begin(g, name):
        return _pair_exchange_start(g, "rs_" + name + "_pair_start")

    def scatter_go_on(pair, after, name):
        g, theirs = _pair_exchange_wait(pair[0], pair[1], pair[2], pair[3], after, "rs_" + name + "_pair_wait")
        ps = _pair_sum(g, theirs, "rs_" + name + "_pair_sum")
        in_flight[name] = _chip_exchange_start(ps, theirs, "rs_" + name + "_chip_start")
        return in_flight[name][2]

    g_wdown = wgrad(act, dh2_b, name="ffn_down_dw", tm=256)
    pair = scatter_begin(g_wdown, "w_down")
    dact = _matmul(dh2_b, wdown, mode="nt", name="ffn_down_dx", out_dtype=BF16, tm=2112, tn=256, tk=d, after=pair[2])
    tok = scatter_go_on(pair, dact, "w_down")
    dgu = _swiglu_bwd(dact, gu)
    g_wgu_t = wgrad(dgu, u2, name="ffn_gate_up_dw", tm=512, after=tok)
    pair = scatter_begin(g_wgu_t, "w_gate_up")
    du2 = mm(dgu, wgu_t, mode="nn", name="ffn_gate_up_dx", out_dtype=F32, tn=512, tk=ffn // 2, after=pair[2])
    tok = scatter_go_on(pair, du2, "w_gate_up")
    dh1, dh1_b, d_ffn_g = _rmsnorm_bwd(du2, h1, ffn_norm_g, dh2, "ffn_rmsnorm_bwd")
    g_wout = wgrad(merged, dh1_b, name="mix_out_dw", tm=512, after=tok)
    pair = scatter_begin(g_wout, "w_out")
    dmerged = mm(dh1_b, wout, mode="nt", name="mix_out_dx", out_dtype=BF16, tk=d, after=pair[2])
    tok = scatter_go_on(pair, dmerged, "w_out")
    d_a, d_b, dz_g, sum_g, d_bco = _gate_bwd(dmerged, br_a, br_b, zg)
    g_wao_t = wgrad(d_a, o, name="attn_out_dw", tm=512, after=tok)
    pair = scatter_begin(g_wao_t, "w_attn_o")
    do = mm(d_a, wao_t, mode="nn", name="attn_out_dx", out_dtype=BF16, tk=d, after=pair[2])
    tok = scatter_go_on(pair, do, "w_attn_o")
    g_wco_t = wgrad(d_b, c2, name="conv_out_dw", tm=512, after=tok)
    pair = scatter_begin(g_wco_t, "w_conv_o")
    dc2 = mm(d_b, wco_t, mode="nn", name="conv_out_dx", out_dtype=F32, tk=d, after=pair[2])
    tok = scatter_go_on(pair, dc2, "w_conv_o")
    dq, dk, dv, dkm, dvm, d_sinks = _attn_bwd(q_rot, k_sh, v_sh, attn_sinks, do)
    dz_qkv, sum_qkv = _rope_bwd(dq, dk, dv, dkm, dvm, ctab, stab)
    dco, d_ln_g, d_ln_b, d_conv_b = _conv_bwd_norm(dc2, conv_out, conv_ln_g, conv_ln_b)
    dz_c, sum_c, d_conv_w = _conv_bwd_taps(dco, zc, conv_w_full)
    dz = jnp.concatenate([dz_qkv, dz_c, dz_g], axis=1)
    d_b_in = jnp.concatenate([sum_qkv, sum_c, sum_g], axis=1)
    in_dim = dz.shape[1]
    g_win_t = wgrad(dz, u, name="in_proj_dw", tm=512, after=tok)
    theirs = _pair_exchange(g_win_t, "rs_w_in_pair_exchange")
    in_flight["w_in"] = _chip_exchange_start(_pair_sum(g_win_t, theirs, "rs_w_in_pair_sum"), theirs, "rs_w_in_chip_start")
    du = mm(dz, win_t, mode="nn", name="in_proj_dx", out_dtype=F32, tk=in_dim // 4, after=in_flight["w_in"][2])
    grad_x, d_meta, d_mix_g = _rmsnorm_bwd_first(du, h0, mix_norm_g, dh1)

    weights = dict(meta_tokens=meta_tokens, mix_norm_g=mix_norm_g, w_in=w_in, b_in=b_in, attn_sinks=attn_sinks, conv_w=conv_w,
                   conv_b=conv_b, conv_ln_g=conv_ln_g, conv_ln_b=conv_ln_b, w_attn_o=w_attn_o, w_conv_o=w_conv_o, b_conv_o=b_conv_o,
                   w_out=w_out, ffn_norm_g=ffn_norm_g, w_gate_up=w_gate_up, w_down=w_down, final_norm_g=final_norm_g)
    m_in = dict(meta_tokens=m_meta_tokens, mix_norm_g=m_mix_norm_g, w_in=m_w_in, b_in=m_b_in, attn_sinks=m_attn_sinks, conv_w=m_conv_w,
                conv_b=m_conv_b, conv_ln_g=m_conv_ln_g, conv_ln_b=m_conv_ln_b, w_attn_o=m_w_attn_o, w_conv_o=m_w_conv_o,
                b_conv_o=m_b_conv_o, w_out=m_w_out, ffn_norm_g=m_ffn_norm_g, w_gate_up=m_w_gate_up, w_down=m_w_down,
                final_norm_g=m_final_norm_g)
    v_in = dict(meta_tokens=v_meta_tokens, mix_norm_g=v_mix_norm_g, w_in=v_w_in, b_in=v_b_in, attn_sinks=v_attn_sinks, conv_w=v_conv_w,
                conv_b=v_conv_b, conv_ln_g=v_conv_ln_g, conv_ln_b=v_conv_ln_b, w_attn_o=v_w_attn_o, w_conv_o=v_w_conv_o,
                b_conv_o=v_b_conv_o, w_out=v_w_out, ffn_norm_g=v_ffn_norm_g, w_gate_up=v_w_gate_up, w_down=v_w_down,
                final_norm_g=v_final_norm_g)
    names = list(weights)
    grads, delta, new_m, new_v = {}, {}, {}, {}
    transposed = ("w_in", "w_attn_o", "w_conv_o", "w_gate_up")
    tok = grad_x
    for n in ("w_down", "w_gate_up", "w_out", "w_attn_o", "w_conv_o", "w_in"):
        send_sem, recv_sem, ps, rx = in_flight[n]
        ps, rx = _chip_exchange_wait(send_sem, recv_sem, ps, rx, tok, "rs_" + n + "_chip_wait")
        g = _sum_chips(ps, rx, "rs_" + n + "_sum")
        g = g.T if n in transposed else g
        shape = weights[n].shape
        dl, nm, nv = _adamw(weights[n].reshape(g.shape), g, m_in[n].reshape(g.shape), v_in[n].reshape(g.shape), "adamw_" + n)
        grads[n], delta[n], new_m[n], new_v[n] = g.reshape(shape), dl.reshape(shape), nm.reshape(shape), nv.reshape(shape)
        tok = dl

    slab, slab_layout = _pack([loss_part[:, :1], d_mix_g, d_b_in, d_sinks[:, :N_Q_HEADS], d_conv_b, d_ln_g, d_ln_b, d_bco,
                               d_ffn_g, d_final_g, d_conv_w, d_meta])
    slab_all = _all_gather_rows(slab, "gather_small_grads", after=tok).reshape(N_DEV, *slab.shape)
    (loss, g_mix_g, g_b_in, g_sinks, g_conv_b, g_ln_g, g_ln_b, g_bco, g_ffn_g, g_final_g, g_conv_w_full, g_meta_full
     ) = _unpack(_sum_blocks(slab_all, "sum_small_grads"), slab_layout)
    g_conv_w = lax.dynamic_slice(g_conv_w_full, (0, dev * cw_cols), (CONV_WIDTH, cw_cols)).reshape(conv_w.shape)
    g_meta = lax.dynamic_slice(g_meta_full, (0, dev * meta_cols), (N_META, meta_cols))
    g_final_g = g_final_g.reshape(final_norm_g.shape)
    grads.update(meta_tokens=g_meta, mix_norm_g=g_mix_g, b_in=g_b_in, attn_sinks=g_sinks, conv_w=g_conv_w, conv_b=g_conv_b,
                 conv_ln_g=g_ln_g, conv_ln_b=g_ln_b, b_conv_o=g_bco, ffn_norm_g=g_ffn_g, final_norm_g=g_final_g)
    rest = [n for n in names if n not in delta]
    w_slab, rest_layout = _pack([weights[n] for n in rest])
    g_slab, _ = _pack([grads[n] for n in rest])
    m_slab, _ = _pack([m_in[n] for n in rest])
    v_slab, _ = _pack([v_in[n] for n in rest])
    dl, nm, nv = _adamw(w_slab, g_slab, m_slab, v_slab, "adamw_small")
    for n, a, b, c in zip(rest, _unpack(dl, rest_layout), _unpack(nm, rest_layout), _unpack(nv, rest_layout)):
        delta[n], new_m[n], new_v[n] = a, b, c

    return (loss.reshape(()), grad_x[None], *[grads[n] for n in names], *[delta[n] for n in names],
            *[new_m[n] for n in names], *[new_v[n] for n in names])
```

```python
import functools
import math

import jax
import jax.numpy as jnp
from jax import lax
from jax.experimental import pallas as pl
from jax.experimental.pallas import tpu as pltpu

F32 = jnp.float32
BF16 = jnp.bfloat16

N_DEV = 8
BLOCK = 128
N_META = 16
PAD_ROWS = BLOCK - N_META
HEAD_DIM = 64
N_Q_HEADS = 32
N_KV_HEADS = 4
GROUP = N_Q_HEADS // N_KV_HEADS
Q_DIM = N_Q_HEADS * HEAD_DIM
KV_DIM = N_KV_HEADS * HEAD_DIM
WINDOW = 128
CONV_WIDTH = 31
CONV_ROWS = 32
ROPE_THETA = 10000.0
EPS = 1e-6
ATTN_SCALE = HEAD_DIM ** -0.5
NEG = -1e30

ADAM_LR = 0.001
ADAM_B1 = 0.9
ADAM_B2 = 0.999
ADAM_EPS = 1e-08
ADAM_WD = 0.01
ADAM_STEP = 10

VMEM_LIMIT_BYTES = 56 * 1024 * 1024
LANES = 128
ELEMENTWISE_BLOCK_BYTES = 2 * 1024 * 1024
MESH = pl.DeviceIdType.MESH
CHIPS = ((0, 0), (0, 1), (1, 0), (1, 1))


def _pcall(body, after=None, **kw):
    if after is None:
        return pl.pallas_call(body, **kw)
    in_specs = list(kw.pop("in_specs"))
    n_in = len(in_specs)

    def ordered_body(*refs):
        return body(*refs[:n_in], *refs[n_in + 1:])

    call = pl.pallas_call(ordered_body, in_specs=in_specs + [pl.BlockSpec(memory_space=pl.ANY)], **kw)
    return lambda *args: call(*args, after)


def _params(semantics=None):
    if semantics is None:
        return pltpu.CompilerParams(vmem_limit_bytes=VMEM_LIMIT_BYTES)
    return pltpu.CompilerParams(dimension_semantics=semantics, vmem_limit_bytes=VMEM_LIMIT_BYTES)


def _pick(dim, pref, align):
    best = None
    t = align
    while t <= min(dim, pref):
        if dim % t == 0:
            best = t
        t += align
    return dim if best is None else best


def _sigmoid(x):
    return 1.0 / (1.0 + jnp.exp(-x))


def _matmul(a, b, *, mode, name, out_dtype, tm, tn, tk, bias=None, residual=None, b_inner=True,
            b_row_off=0, b_rows=None, after=None):
    if mode == "nn":
        m, k = a.shape
        n = b.shape[1]
    elif mode == "nt":
        m, k = a.shape
        n = b.shape[0] if b_rows is None else b_rows
    else:
        k, m = a.shape
        n = b.shape[1]
    tm = _pick(m, tm, 16)
    tn = _pick(math.gcd(n, b_row_off) if mode == "nt" and b_row_off else n, tn, LANES)
    tk = _pick(k, tk, LANES if mode != "tn" else 16)
    nm, nn, nk = m // tm, n // tn, k // tk
    if mode == "nt":
        assert b_row_off % tn == 0
    off = b_row_off // tn if mode == "nt" else 0

    if b_inner:
        grid = (nm, nn, nk)
        ij = lambda g0, g1: (g0, g1)
    else:
        grid = (nn, nm, nk)
        ij = lambda g0, g1: (g1, g0)

    if mode == "tn":
        a_spec = pl.BlockSpec((tk, tm), lambda g0, g1, kk: (kk, ij(g0, g1)[0]))
    else:
        a_spec = pl.BlockSpec((tm, tk), lambda g0, g1, kk: (ij(g0, g1)[0], kk))
    if mode == "nt":
        b_spec = pl.BlockSpec((tn, tk), lambda g0, g1, kk: (ij(g0, g1)[1] + off, kk))
    else:
        b_spec = pl.BlockSpec((tk, tn), lambda g0, g1, kk: (kk, ij(g0, g1)[1]))
    o_spec = pl.BlockSpec((tm, tn), lambda g0, g1, kk: ij(g0, g1))
    in_specs = [a_spec, b_spec]
    args = [a, b]
    if bias is not None:
        in_specs.append(pl.BlockSpec((1, tn), lambda g0, g1, kk: (0, ij(g0, g1)[1])))
        args.append(bias)
    if residual is not None:
        in_specs.append(o_spec)
        args.append(residual)
    dims = {"nn": (((1,), (0,)), ((), ())), "nt": (((1,), (1,)), ((), ())), "tn": (((0,), (0,)), ((), ()))}[mode]
    has_bias, has_res = bias is not None, residual is not None

    def body(*refs):
        a_ref, b_ref = refs[0], refs[1]
        pos = 2
        bias_ref = res_ref = None
        if has_bias:
            bias_ref = refs[pos]
            pos += 1
        if has_res:
            res_ref = refs[pos]
            pos += 1
        o_ref = refs[pos]
        acc_ref = refs[pos + 1] if nk > 1 else None

        def finish(acc):
            if has_bias:
                acc = acc + bias_ref[...]
            if has_res:
                acc = acc + res_ref[...]
            o_ref[...] = acc.astype(out_dtype)

        p = lax.dot_general(a_ref[...], b_ref[...], dims, preferred_element_type=F32)
        if nk == 1:
            finish(p)
        else:
            kk = pl.program_id(2)

            @pl.when(kk == 0)
            def _():
                acc_ref[...] = p

            @pl.when(kk > 0)
            def _():
                acc_ref[...] += p

            @pl.when(kk == nk - 1)
            def _():
                finish(acc_ref[...])

    return _pcall(
        body, after=after, name=name, grid=grid, in_specs=in_specs, out_specs=o_spec,
        out_shape=jax.ShapeDtypeStruct((m, n), out_dtype),
        scratch_shapes=[pltpu.VMEM((tm, tn), F32)] if nk > 1 else [],
        compiler_params=_params(("parallel", "parallel", "arbitrary")),
    )(*args)


def _row_spec(width, col=0):
    return pl.BlockSpec((BLOCK, width), lambda i: (i, col))


def _const_spec(shape):
    nd = len(shape)
    return pl.BlockSpec(shape, lambda i: (0,) * nd)


def _prep(x, meta_full, g, after=None):
    s, d = x.shape
    lp = s + BLOCK
    nb = lp // BLOCK

    def body(x_ref, meta_ref, g_ref, h_ref, u_ref):
        i = pl.program_id(0)

        @pl.when(i == 0)
        def _():
            h_ref[0:PAD_ROWS, :] = jnp.zeros((PAD_ROWS, d), F32)
            h_ref[PAD_ROWS:BLOCK, :] = meta_ref[...]

        @pl.when(i > 0)
        def _():
            h_ref[...] = x_ref[...]

        h = h_ref[...]
        r = lax.rsqrt(jnp.mean(h * h, axis=-1, keepdims=True) + EPS)
        u_ref[...] = (h * r * g_ref[...]).astype(BF16)

    return _pcall(
        body, after=after, name="prep_rmsnorm", grid=(nb,),
        in_specs=[pl.BlockSpec((BLOCK, d), lambda i: (jnp.maximum(i - 1, 0), 0)), _const_spec((N_META, d)), _const_spec((1, d))],
        out_specs=[_row_spec(d), _row_spec(d)],
        out_shape=[jax.ShapeDtypeStruct((lp, d), F32), jax.ShapeDtypeStruct((lp, d), BF16)],
        compiler_params=_params(("arbitrary",)),
    )(x, meta_full, g)


def _rmsnorm_fwd(h, g, name):
    lp, d = h.shape

    def body(h_ref, g_ref, u_ref):
        x = h_ref[...]
        r = lax.rsqrt(jnp.mean(x * x, axis=-1, keepdims=True) + EPS)
        u_ref[...] = (x * r * g_ref[...]).astype(BF16)

    return _pcall(
        body, name=name, grid=(lp // BLOCK,), in_specs=[_row_spec(d), _const_spec((1, d))], out_specs=_row_spec(d),
        out_shape=jax.ShapeDtypeStruct((lp, d), BF16), compiler_params=_params(("parallel",)),
    )(h, g)


def _rms_bwd_core(dy, x, g):
    r = lax.rsqrt(jnp.mean(x * x, axis=-1, keepdims=True) + EPS)
    xhat = x * r
    dxhat = dy * g
    dx = r * (dxhat - xhat * jnp.mean(dxhat * xhat, axis=-1, keepdims=True))
    return dx, jnp.sum(dy * xhat, axis=0, keepdims=True)


def _rmsnorm_bwd(dy, h, g, dres, name):
    lp, d = h.shape

    def body(dy_ref, h_ref, g_ref, dres_ref, dh_ref, dhb_ref, dg_ref):
        i = pl.program_id(0)
        dx, dg = _rms_bwd_core(dy_ref[...], h_ref[...], g_ref[...])
        dh = dres_ref[...] + dx
        dh_ref[...] = dh
        dhb_ref[...] = dh.astype(BF16)

        @pl.when(i == 0)
        def _():
            dg_ref[...] = jnp.zeros_like(dg_ref)

        dg_ref[...] += dg

    return _pcall(
        body, name=name, grid=(lp // BLOCK,),
        in_specs=[_row_spec(d), _row_spec(d), _const_spec((1, d)), _row_spec(d)],
        out_specs=[_row_spec(d), _row_spec(d), _const_spec((1, d))],
        out_shape=[jax.ShapeDtypeStruct((lp, d), F32), jax.ShapeDtypeStruct((lp, d), BF16), jax.ShapeDtypeStruct((1, d), F32)],
        compiler_params=_params(("arbitrary",)),
    )(dy, h, g, dres)


def _rmsnorm_bwd_first(dy, h, g, dres):
    lp, d = h.shape
    s = lp - BLOCK

    def body(dy_ref, h_ref, g_ref, dres_ref, gx_ref, dmeta_ref, dg_ref):
        i = pl.program_id(0)
        dx, dg = _rms_bwd_core(dy_ref[...], h_ref[...], g_ref[...])
        dh = dres_ref[...] + dx
        gx_ref[...] = dh

        @pl.when(i == 0)
        def _():
            dmeta_ref[...] = dh[PAD_ROWS:BLOCK, :]
            dg_ref[...] = jnp.zeros_like(dg_ref)

        dg_ref[...] += dg

    return _pcall(
        body, name="rmsnorm_bwd_first", grid=(lp // BLOCK,),
        in_specs=[_row_spec(d), _row_spec(d), _const_spec((1, d)), _row_spec(d)],
        out_specs=[pl.BlockSpec((BLOCK, d), lambda i: (jnp.maximum(i - 1, 0), 0)), _const_spec((N_META, d)), _const_spec((1, d))],
        out_shape=[jax.ShapeDtypeStruct((s, d), F32), jax.ShapeDtypeStruct((N_META, d), F32), jax.ShapeDtypeStruct((1, d), F32)],
        compiler_params=_params(("arbitrary",)),
    )(dy, h, g, dres)


def _final(h2, tgt, g):
    lp, d = h2.shape

    def body(h_ref, t_ref, g_ref, dh_ref, dhb_ref, loss_ref, dg_ref):
        i = pl.program_id(0)
        x = h_ref[...]
        gg = g_ref[...]
        r = lax.rsqrt(jnp.mean(x * x, axis=-1, keepdims=True) + EPS)
        xhat = x * r
        y = xhat * gg
        live = (i > 0).astype(F32)
        err = (y - t_ref[...]) * live
        dy = err * (1.0 / d)
        dxhat = dy * gg
        dh = r * (dxhat - xhat * jnp.mean(dxhat * xhat, axis=-1, keepdims=True))
        dh_ref[...] = dh
        dhb_ref[...] = dh.astype(BF16)

        @pl.when(i == 0)
        def _():
            loss_ref[...] = jnp.zeros_like(loss_ref)
            dg_ref[...] = jnp.zeros_like(dg_ref)

        row_loss = jnp.mean(err * err, axis=-1, keepdims=True)
        loss_ref[...] += 0.5 * jnp.sum(row_loss, axis=0, keepdims=True)
        dg_ref[...] += jnp.sum(dy * xhat, axis=0, keepdims=True)

    return _pcall(
        body, name="final_norm_loss", grid=(lp // BLOCK,),
        in_specs=[_row_spec(d), pl.BlockSpec((BLOCK, d), lambda i: (jnp.maximum(i - 1, 0), 0)), _const_spec((1, d))],
        out_specs=[_row_spec(d), _row_spec(d), _const_spec((1, LANES)), _const_spec((1, d))],
        out_shape=[jax.ShapeDtypeStruct((lp, d), F32), jax.ShapeDtypeStruct((lp, d), BF16),
                   jax.ShapeDtypeStruct((1, LANES), F32), jax.ShapeDtypeStruct((1, d), F32)],
        compiler_params=_params(("arbitrary",)),
    )(h2, tgt, g)


def _swap_halves(x):
    w = x.shape[1]
    lane = lax.broadcasted_iota(jnp.int32, x.shape, 1)
    first = (lane & (HEAD_DIM - 1)) < (HEAD_DIM // 2)
    return jnp.where(first, pltpu.roll(x, w - HEAD_DIM // 2, 1), pltpu.roll(x, HEAD_DIM // 2, 1))


def _rope_tables(lp):
    pos = jnp.maximum(jnp.arange(lp, dtype=jnp.int32) - PAD_ROWS, 0).astype(F32)
    inv_freq = ROPE_THETA ** (-jnp.arange(0, HEAD_DIM, 2, dtype=F32) / HEAD_DIM)
    ang = pos[:, None] * inv_freq[None, :]
    c, s = jnp.cos(ang), jnp.sin(ang)
    reps = LANES // HEAD_DIM
    return jnp.tile(jnp.concatenate([c, c], axis=1), (1, reps)), jnp.tile(jnp.concatenate([-s, s], axis=1), (1, reps))


def _rope_fwd(zq, zkv, ctab, stab, after=None):
    lp = zq.shape[0]
    nb = lp // BLOCK
    back = lambda s: (jnp.maximum(s - 1, 0), 0)

    def body(zq_ref, zkv_ref, c_ref, s_ref, q_ref, k_ref, v_ref):
        step = pl.program_id(0)
        c128, s128 = c_ref[...], s_ref[...]

        def rope(x):
            reps = x.shape[1] // LANES
            return x * jnp.tile(c128, (1, reps)) + _swap_halves(x) * jnp.tile(s128, (1, reps))

        q_ref[...] = (rope(zq_ref[...].astype(F32)) * ATTN_SCALE).astype(BF16)
        kv = zkv_ref[...].astype(F32)
        k = rope(kv[:, :KV_DIM])
        v = kv[:, KV_DIM:]

        @pl.when(step == 0)
        def _():
            k_ref[...] = jnp.zeros_like(k_ref)
            v_ref[...] = jnp.zeros_like(v_ref)

        @pl.when(step > 0)
        def _():
            for h in range(N_KV_HEADS):
                k_ref[h] = k[:, h * HEAD_DIM:(h + 1) * HEAD_DIM].astype(BF16)
                v_ref[h] = v[:, h * HEAD_DIM:(h + 1) * HEAD_DIM].astype(BF16)

    kv_spec = pl.BlockSpec((N_KV_HEADS, BLOCK, HEAD_DIM), lambda s: (0, s, 0))
    return _pcall(
        body, after=after, name="rope_fwd", grid=(nb + 1,),
        in_specs=[pl.BlockSpec((BLOCK, Q_DIM), back), pl.BlockSpec((BLOCK, 2 * KV_DIM), back),
                  pl.BlockSpec((BLOCK, LANES), back), pl.BlockSpec((BLOCK, LANES), back)],
        out_specs=[pl.BlockSpec((BLOCK, Q_DIM), back), kv_spec, kv_spec],
        out_shape=[jax.ShapeDtypeStruct((lp, Q_DIM), BF16),
                   jax.ShapeDtypeStruct((N_KV_HEADS, lp + BLOCK, HEAD_DIM), BF16),
                   jax.ShapeDtypeStruct((N_KV_HEADS, lp + BLOCK, HEAD_DIM), BF16)],
        compiler_params=_params(("arbitrary",)),
    )(zq, zkv, ctab, stab)


def _rope_bwd(dq, dk, dv, dkm, dvm, ctab, stab):
    lp = dq.shape[0]
    width = Q_DIM + 2 * KV_DIM
    head_spec = pl.BlockSpec((N_KV_HEADS, BLOCK, HEAD_DIM), lambda i: (0, i, 0))
    meta_spec = _const_spec((N_KV_HEADS, BLOCK, HEAD_DIM))

    def body(dq_ref, dk_ref, dv_ref, dkm_ref, dvm_ref, c_ref, s_ref, dz_ref, sum_ref, kbuf, vbuf):
        i = pl.program_id(0)
        c128, s128 = c_ref[...], s_ref[...]
        first = (i == 0).astype(F32)

        def rope_t(x):
            reps = x.shape[1] // LANES
            return x * jnp.tile(c128, (1, reps)) + _swap_halves(x * jnp.tile(s128, (1, reps)))

        for h in range(N_KV_HEADS):
            kbuf[:, h * HEAD_DIM:(h + 1) * HEAD_DIM] = dk_ref[h] + first * dkm_ref[h]
            vbuf[:, h * HEAD_DIM:(h + 1) * HEAD_DIM] = dv_ref[h] + first * dvm_ref[h]
        dzq = rope_t(dq_ref[...] * ATTN_SCALE)
        dzk = rope_t(kbuf[...])
        dzv = vbuf[...]
        dz_ref[:, 0:Q_DIM] = dzq.astype(BF16)
        dz_ref[:, Q_DIM:Q_DIM + KV_DIM] = dzk.astype(BF16)
        dz_ref[:, Q_DIM + KV_DIM:width] = dzv.astype(BF16)

        @pl.when(i == 0)
        def _():
            sum_ref[...] = jnp.zeros_like(sum_ref)

        sum_ref[:, 0:Q_DIM] += jnp.sum(dzq, axis=0, keepdims=True)
        sum_ref[:, Q_DIM:Q_DIM + KV_DIM] += jnp.sum(dzk, axis=0, keepdims=True)
        sum_ref[:, Q_DIM + KV_DIM:width] += jnp.sum(dzv, axis=0, keepdims=True)

    return _pcall(
        body, name="rope_bwd", grid=(lp // BLOCK,),
        in_specs=[_row_spec(Q_DIM), head_spec, head_spec, meta_spec, meta_spec, _row_spec(LANES), _row_spec(LANES)],
        out_specs=[_row_spec(width), _const_spec((1, width))],
        out_shape=[jax.ShapeDtypeStruct((lp, width), BF16), jax.ShapeDtypeStruct((1, width), F32)],
        scratch_shapes=[pltpu.VMEM((BLOCK, KV_DIM), F32), pltpu.VMEM((BLOCK, KV_DIM), F32)],
        compiler_params=_params(("arbitrary",)),
    )(dq, dk, dv, dkm, dvm, ctab, stab)


def _attn_bias(i):
    r = lax.broadcasted_iota(jnp.int32, (BLOCK, 3 * BLOCK), 0)
    c = lax.broadcasted_iota(jnp.int32, (BLOCK, 3 * BLOCK), 1)
    qp = i * BLOCK + r - PAD_ROWS
    kp = (i - 1) * BLOCK + c - PAD_ROWS
    band = (c < 2 * BLOCK) & (kp >= N_META) & (kp <= qp) & (qp - kp < WINDOW)
    mp = c - 2 * BLOCK - PAD_ROWS
    meta = (c >= 2 * BLOCK) & (mp >= 0) & (mp <= qp)
    return jnp.where(band | meta, 0.0, NEG).astype(F32)


HALF = BLOCK // 2
HALF_KEYS = 2 * BLOCK


def _half_keys(prev, own, meta, half):
    if half == 0:
        return jnp.concatenate([prev, own[0:HALF], meta[HALF:BLOCK]], axis=0)
    return jnp.concatenate([prev[HALF:BLOCK], own, meta[HALF:BLOCK]], axis=0)


def _half_bias(i, half):
    r = lax.broadcasted_iota(jnp.int32, (HALF, HALF_KEYS), 0) + half * HALF
    c = lax.broadcasted_iota(jnp.int32, (HALF, HALF_KEYS), 1)
    n_prev = BLOCK - half * HALF
    qp = i * BLOCK + r - PAD_ROWS
    kp = jnp.where(c < n_prev, (i - 1) * BLOCK + c + half * HALF, i * BLOCK + c - n_prev) - PAD_ROWS
    band = (c < HALF_KEYS - HALF) & (kp >= N_META) & (kp <= qp) & (qp - kp < WINDOW)
    mp = c - (HALF_KEYS - HALF) + HALF - PAD_ROWS
    meta = (c >= HALF_KEYS - HALF) & (mp >= 0) & (mp <= qp)
    return jnp.where(band | meta, 0.0, NEG).astype(F32)


def _half_rows(ref, heads, half):
    rows = slice(half * HALF, (half + 1) * HALF)
    return jnp.concatenate([ref[rows, n * HEAD_DIM:(n + 1) * HEAD_DIM] for n in heads], axis=0)


def _half_sinks(sink_ref, heads):
    return jnp.concatenate([jnp.broadcast_to(sink_ref[0:1, n:n + 1], (HALF, 1)) for n in heads], axis=0)


def _stack_heads(ref, h):
    return jnp.concatenate(
        [ref[:, (h * GROUP + g) * HEAD_DIM:(h * GROUP + g + 1) * HEAD_DIM] for g in range(GROUP)], axis=0)


def _attn_probs(qs, k3, bias8, sink):
    s = lax.dot_general(qs, k3, (((1,), (1,)), ((), ())), preferred_element_type=F32) + bias8
    m = jnp.maximum(jnp.max(s, axis=1, keepdims=True), sink)
    p = jnp.exp(s - m)
    ps = jnp.exp(sink - m)
    inv = 1.0 / (jnp.sum(p, axis=1, keepdims=True) + ps)
    return p * inv, ps * inv


def _sink_column(sink_ref, h):
    return jnp.concatenate(
        [jnp.broadcast_to(sink_ref[0:1, h * GROUP + g:h * GROUP + g + 1], (BLOCK, 1)) for g in range(GROUP)], axis=0)


FWD_CHAIN_HEADS = 8


def _attn_fwd(q, k_sh, v_sh, sinks):
    lp = q.shape[0]
    nb = lp // BLOCK
    kv = lambda f: pl.BlockSpec((N_KV_HEADS, BLOCK, HEAD_DIM), f)

    def body(q_ref, kp_ref, kc_ref, km_ref, vp_ref, vc_ref, vm_ref, sink_ref, o_ref):
        i = pl.program_id(0)
        for half in range(2):
            bias = jnp.tile(_half_bias(i, half), (FWD_CHAIN_HEADS, 1))
            rows = slice(half * HALF, (half + 1) * HALF)
            for h in range(N_KV_HEADS):
                keys = _half_keys(kp_ref[h], kc_ref[h], km_ref[h], half)
                vals = _half_keys(vp_ref[h], vc_ref[h], vm_ref[h], half)
                for g0 in range(0, GROUP, FWD_CHAIN_HEADS):
                    heads = range(h * GROUP + g0, h * GROUP + g0 + FWD_CHAIN_HEADS)
                    p, _ = _attn_probs(_half_rows(q_ref, heads, half), keys, bias, _half_sinks(sink_ref, heads))
                    o = jnp.dot(p.astype(BF16), vals, preferred_element_type=F32)
                    for j, n in enumerate(heads):
                        o_ref[rows, n * HEAD_DIM:(n + 1) * HEAD_DIM] = o[j * HALF:(j + 1) * HALF].astype(BF16)

    prev, cur, meta = (lambda i: (0, i, 0)), (lambda i: (0, i + 1, 0)), (lambda i: (0, 1, 0))
    return _pcall(
        body, name="attn_fwd", grid=(nb,),
        in_specs=[_row_spec(Q_DIM), kv(prev), kv(cur), kv(meta), kv(prev), kv(cur), kv(meta), _const_spec((1, N_Q_HEADS))],
        out_specs=_row_spec(Q_DIM), out_shape=jax.ShapeDtypeStruct((lp, Q_DIM), BF16),
        compiler_params=_params(("parallel",)),
    )(q, k_sh, k_sh, k_sh, v_sh, v_sh, v_sh, sinks)


def _attn_bwd(q, k_sh, v_sh, sinks, do):
    lp = q.shape[0]
    nb = lp // BLOCK
    kv = lambda f: pl.BlockSpec((N_KV_HEADS, BLOCK, HEAD_DIM), f)
    cl = lambda s: jnp.minimum(s, nb - 1)

    def body(q_ref, do_ref, kp_ref, kc_ref, km_ref, vp_ref, vc_ref, vm_ref, sink_ref,
             dq_ref, dk_ref, dv_ref, dkm_ref, dvm_ref, dsink_ref, carry_k, carry_v):
        step = pl.program_id(0)

        @pl.when(step == 0)
        def _():
            carry_k[...] = jnp.zeros_like(carry_k)
            carry_v[...] = jnp.zeros_like(carry_v)
            dkm_ref[...] = jnp.zeros_like(dkm_ref)
            dvm_ref[...] = jnp.zeros_like(dvm_ref)
            dsink_ref[...] = jnp.zeros_like(dsink_ref)

        @pl.when(step < nb)
        def _():
            bias8 = jnp.tile(_attn_bias(step), (GROUP, 1))
            lane = lax.broadcasted_iota(jnp.int32, (1, LANES), 1)
            dsink = jnp.zeros((1, LANES), F32)
            for h in range(N_KV_HEADS):
                k3 = jnp.concatenate([kp_ref[h], kc_ref[h], km_ref[h]], axis=0)
                v3 = jnp.concatenate([vp_ref[h], vc_ref[h], vm_ref[h]], axis=0)
                qs = _stack_heads(q_ref, h)
                dos = _stack_heads(do_ref, h)
                p, psink = _attn_probs(qs, k3, bias8, _sink_column(sink_ref, h))
                dp = lax.dot_general(dos, v3, (((1,), (1,)), ((), ())), preferred_element_type=F32)
                delta = jnp.sum(p * dp, axis=1, keepdims=True)
                ds = (p * (dp - delta)).astype(BF16)
                dsk = -psink * delta
                for g in range(GROUP):
                    val = jnp.sum(dsk[g * BLOCK:(g + 1) * BLOCK], axis=0, keepdims=True)
                    dsink = dsink + jnp.where(lane == h * GROUP + g, val, 0.0)
                dqs = jnp.dot(ds, k3, preferred_element_type=F32)
                for g in range(GROUP):
                    n = h * GROUP + g
                    dq_ref[:, n * HEAD_DIM:(n + 1) * HEAD_DIM] = dqs[g * BLOCK:(g + 1) * BLOCK]
                dk3 = lax.dot_general(ds, qs, (((0,), (0,)), ((), ())), preferred_element_type=F32)
                dv3 = lax.dot_general(p.astype(BF16), dos, (((0,), (0,)), ((), ())), preferred_element_type=F32)
                dk_ref[h] = carry_k[h] + dk3[0:BLOCK]
                dv_ref[h] = carry_v[h] + dv3[0:BLOCK]
                carry_k[h] = dk3[BLOCK:2 * BLOCK]
                carry_v[h] = dv3[BLOCK:2 * BLOCK]
                dkm_ref[h] += dk3[2 * BLOCK:3 * BLOCK]
                dvm_ref[h] += dv3[2 * BLOCK:3 * BLOCK]
            dsink_ref[...] += dsink

        @pl.when(step == nb)
        def _():
            dk_ref[...] = carry_k[...]
            dv_ref[...] = carry_v[...]

    prev, cur, meta = (lambda s: (0, cl(s), 0)), (lambda s: (0, cl(s) + 1, 0)), (lambda s: (0, 1, 0))
    lag = lambda s: (0, jnp.maximum(s - 1, 0), 0)
    head_shape = jax.ShapeDtypeStruct((N_KV_HEADS, lp, HEAD_DIM), F32)
    meta_shape = jax.ShapeDtypeStruct((N_KV_HEADS, BLOCK, HEAD_DIM), F32)
    return _pcall(
        body, name="attn_bwd", grid=(nb + 1,),
        in_specs=[pl.BlockSpec((BLOCK, Q_DIM), lambda s: (cl(s), 0)), pl.BlockSpec((BLOCK, Q_DIM), lambda s: (cl(s), 0)),
                  kv(prev), kv(cur), kv(meta), kv(prev), kv(cur), kv(meta), _const_spec((1, N_Q_HEADS))],
        out_specs=[pl.BlockSpec((BLOCK, Q_DIM), lambda s: (cl(s), 0)), kv(lag), kv(lag),
                   _const_spec((N_KV_HEADS, BLOCK, HEAD_DIM)), _const_spec((N_KV_HEADS, BLOCK, HEAD_DIM)), _const_spec((1, LANES))],
        out_shape=[jax.ShapeDtypeStruct((lp, Q_DIM), F32), head_shape, head_shape, meta_shape, meta_shape,
                   jax.ShapeDtypeStruct((1, LANES), F32)],
        scratch_shapes=[pltpu.VMEM((N_KV_HEADS, BLOCK, HEAD_DIM), F32), pltpu.VMEM((N_KV_HEADS, BLOCK, HEAD_DIM), F32)],
        compiler_params=_params(("arbitrary",)),
    )(q, do, k_sh, k_sh, k_sh, v_sh, v_sh, v_sh, sinks)


CONV_CHUNK = 256


SUBLANES = 8
SH_BASE = BLOCK - 4 * SUBLANES
SH_ROWS = BLOCK + 3 * SUBLANES
DSH_ROWS = SH_ROWS


def _shifted_windows(src, sh, base, rows):
    for b in range(1, SUBLANES):
        sh[b] = src[base + b:base + b + rows, :]


def _window(src, sh, base, start, cols):
    a, b = divmod(start - base, SUBLANES)
    if b == 0:
        return src[start:start + BLOCK, cols]
    return sh[b, SUBLANES * a:SUBLANES * a + BLOCK, cols]


def _glu_masked(a_ref, g_ref, base):
    rows = base + lax.broadcasted_iota(jnp.int32, (BLOCK, 1), 0)
    return jnp.where(rows >= PAD_ROWS, a_ref[...].astype(F32) * _sigmoid(g_ref[...].astype(F32)), 0.0)


def _conv_fwd(zc, conv_w, conv_b, ln_g, ln_b, after=None):
    lp = zc.shape[0]
    cd = zc.shape[1] // 2
    nb = lp // BLOCK
    chunk = min(CONV_CHUNK, cd)
    back = lambda col: (lambda i: (jnp.maximum(i - 1, 0), col))
    lo = BLOCK - (CONV_WIDTH - 1)

    def body(ap_ref, gp_ref, ac_ref, gc_ref, w_ref, b_ref, lg_ref, lb_ref, co_ref, c2_ref, ext, sh):
        i = pl.program_id(0)
        ext[0:BLOCK, :] = _glu_masked(ap_ref, gp_ref, (i - 1) * BLOCK)
        ext[BLOCK:2 * BLOCK, :] = _glu_masked(ac_ref, gc_ref, i * BLOCK)
        _shifted_windows(ext, sh, SH_BASE, SH_ROWS)
        for c0 in range(0, cd, chunk):
            cols = slice(c0, c0 + chunk)
            acc = jnp.zeros((BLOCK, chunk), F32)
            for k in range(CONV_WIDTH):
                acc = acc + _window(ext, sh, SH_BASE, lo + k, cols) * w_ref[k:k + 1, cols]
            co_ref[:, cols] = acc + b_ref[:, cols]
        x = co_ref[...]
        mu = jnp.mean(x, axis=-1, keepdims=True)
        xc = x - mu
        r = lax.rsqrt(jnp.mean(xc * xc, axis=-1, keepdims=True) + EPS)
        y = xc * r * lg_ref[...] + lb_ref[...]
        c2_ref[...] = (y * _sigmoid(y)).astype(BF16)

    return _pcall(
        body, after=after, name="conv_fwd", grid=(nb,),
        in_specs=[pl.BlockSpec((BLOCK, cd), back(0)), pl.BlockSpec((BLOCK, cd), back(1)), _row_spec(cd, 0), _row_spec(cd, 1),
                  _const_spec((CONV_ROWS, cd)), _const_spec((1, cd)), _const_spec((1, cd)), _const_spec((1, cd))],
        out_specs=[_row_spec(cd), _row_spec(cd)],
        out_shape=[jax.ShapeDtypeStruct((lp, cd), F32), jax.ShapeDtypeStruct((lp, cd), BF16)],
        scratch_shapes=[pltpu.VMEM((2 * BLOCK, cd), F32), pltpu.VMEM((SUBLANES, SH_ROWS, cd), F32)],
        compiler_params=_params(("arbitrary",)),
    )(zc, zc, zc, zc, conv_w, conv_b, ln_g, ln_b)


def _conv_bwd_norm(dc2, conv_out, ln_g, ln_b):
    lp, cd = conv_out.shape

    def body(d_ref, x_ref, lg_ref, lb_ref, dco_ref, dlg_ref, dlb_ref, dcb_ref):
        i = pl.program_id(0)
        x = x_ref[...]
        g = lg_ref[...]
        mu = jnp.mean(x, axis=-1, keepdims=True)
        xc = x - mu
        r = lax.rsqrt(jnp.mean(xc * xc, axis=-1, keepdims=True) + EPS)
        xhat = xc * r
        y = xhat * g + lb_ref[...]
        sg = _sigmoid(y)
        dy = d_ref[...] * (sg * (1.0 + y * (1.0 - sg)))
        dxhat = dy * g
        dx = r * (dxhat - jnp.mean(dxhat, axis=-1, keepdims=True) - xhat * jnp.mean(dxhat * xhat, axis=-1, keepdims=True))
        dco_ref[...] = dx

        @pl.when(i == 0)
        def _():
            dlg_ref[...] = jnp.zeros_like(dlg_ref)
            dlb_ref[...] = jnp.zeros_like(dlb_ref)
            dcb_ref[...] = jnp.zeros_like(dcb_ref)

        dlg_ref[...] += jnp.sum(dy * xhat, axis=0, keepdims=True)
        dlb_ref[...] += jnp.sum(dy, axis=0, keepdims=True)
        dcb_ref[...] += jnp.sum(dx, axis=0, keepdims=True)

    vec = jax.ShapeDtypeStruct((1, cd), F32)
    return _pcall(
        body, name="conv_bwd_norm", grid=(lp // BLOCK,),
        in_specs=[_row_spec(cd), _row_spec(cd), _const_spec((1, cd)), _const_spec((1, cd))],
        out_specs=[_row_spec(cd), _const_spec((1, cd)), _const_spec((1, cd)), _const_spec((1, cd))],
        out_shape=[jax.ShapeDtypeStruct((lp, cd), F32), vec, vec, vec],
        compiler_params=_params(("arbitrary",)),
    )(dc2, conv_out, ln_g, ln_b)


def _conv_bwd_taps(dco, zc, conv_w):
    lp, cd = dco.shape
    nb = lp // BLOCK
    chunk = min(CONV_CHUNK, cd)
    back = lambda col: (lambda i: (jnp.maximum(i - 1, 0), col))
    fwd = lambda i: (jnp.minimum(i + 1, nb - 1), 0)
    lo = BLOCK - (CONV_WIDTH - 1)

    def body(dc_ref, dn_ref, ap_ref, gp_ref, ac_ref, gc_ref, w_ref, dz_ref, sum_ref, dw_ref, ext, dext, dcb, sh, dsh):
        i = pl.program_id(0)
        ext[0:BLOCK, :] = _glu_masked(ap_ref, gp_ref, (i - 1) * BLOCK)
        ext[BLOCK:2 * BLOCK, :] = _glu_masked(ac_ref, gc_ref, i * BLOCK)
        dext[0:BLOCK, :] = dc_ref[...]
        dext[BLOCK:2 * BLOCK, :] = dn_ref[...] * (i < nb - 1).astype(F32)
        _shifted_windows(ext, sh, SH_BASE, SH_ROWS)
        _shifted_windows(dext, dsh, 0, DSH_ROWS)

        @pl.when(i == 0)
        def _():
            dw_ref[...] = jnp.zeros_like(dw_ref)
            sum_ref[...] = jnp.zeros_like(sum_ref)

        for c0 in range(0, cd, chunk):
            cols = slice(c0, c0 + chunk)
            dcur = dext[0:BLOCK, cols]
            acc = jnp.zeros((BLOCK, chunk), F32)
            for k in range(CONV_WIDTH):
                s = CONV_WIDTH - 1 - k
                acc = acc + _window(dext, dsh, 0, s, cols) * w_ref[k:k + 1, cols]
                dw_ref[k:k + 1, cols] += jnp.sum(dcur * _window(ext, sh, SH_BASE, lo + k, cols), axis=0, keepdims=True)
            dcb[:, cols] = acc
        rows = i * BLOCK + lax.broadcasted_iota(jnp.int32, (BLOCK, 1), 0)
        dc = jnp.where(rows >= PAD_ROWS, dcb[...], 0.0)
        a = ac_ref[...].astype(F32)
        sg = _sigmoid(gc_ref[...].astype(F32))
        da = dc * sg
        dg = dc * a * sg * (1.0 - sg)
        dz_ref[:, 0:cd] = da.astype(BF16)
        dz_ref[:, cd:2 * cd] = dg.astype(BF16)
        sum_ref[:, 0:cd] += jnp.sum(da, axis=0, keepdims=True)
        sum_ref[:, cd:2 * cd] += jnp.sum(dg, axis=0, keepdims=True)

    return _pcall(
        body, name="conv_bwd_taps", grid=(nb,),
        in_specs=[_row_spec(cd), pl.BlockSpec((BLOCK, cd), fwd),
                  pl.BlockSpec((BLOCK, cd), back(0)), pl.BlockSpec((BLOCK, cd), back(1)), _row_spec(cd, 0), _row_spec(cd, 1),
                  _const_spec((CONV_ROWS, cd))],
        out_specs=[_row_spec(2 * cd), _const_spec((1, 2 * cd)), _const_spec((CONV_ROWS, cd))],
        out_shape=[jax.ShapeDtypeStruct((lp, 2 * cd), BF16), jax.ShapeDtypeStruct((1, 2 * cd), F32),
                   jax.ShapeDtypeStruct((CONV_ROWS, cd), F32)],
        scratch_shapes=[pltpu.VMEM((2 * BLOCK, cd), F32), pltpu.VMEM((2 * BLOCK, cd), F32), pltpu.VMEM((BLOCK, cd), F32),
                        pltpu.VMEM((SUBLANES, SH_ROWS, cd), F32), pltpu.VMEM((SUBLANES, DSH_ROWS, cd), F32)],
        compiler_params=_params(("arbitrary",)),
    )(dco, dco, zc, zc, zc, zc, conv_w)


def _gate_fwd(a, b, zg, after=None):
    lp, d = a.shape

    def body(a_ref, b_ref, ga_ref, gb_ref, m_ref):
        ga, gb = ga_ref[...].astype(F32), gb_ref[...].astype(F32)
        m_ref[...] = (_sigmoid(ga) * a_ref[...].astype(F32) + _sigmoid(gb) * b_ref[...].astype(F32)).astype(BF16)

    return _pcall(
        body, after=after, name="gate_fwd", grid=(lp // BLOCK,),
        in_specs=[_row_spec(d), _row_spec(d), _row_spec(d, 0), _row_spec(d, 1)], out_specs=_row_spec(d),
        out_shape=jax.ShapeDtypeStruct((lp, d), BF16), compiler_params=_params(("parallel",)),
    )(a, b, zg, zg)


def _gate_bwd(dm, a, b, zg):
    lp, d = a.shape

    def body(dm_ref, a_ref, b_ref, ga_ref, gb_ref, da_ref, db_ref, dz_ref, sum_ref, dbias_ref):
        i = pl.program_id(0)
        dm_ = dm_ref[...].astype(F32)
        sa = _sigmoid(ga_ref[...].astype(F32))
        sb = _sigmoid(gb_ref[...].astype(F32))
        db = dm_ * sb
        dga = dm_ * a_ref[...].astype(F32) * sa * (1.0 - sa)
        dgb = dm_ * b_ref[...].astype(F32) * sb * (1.0 - sb)
        da_ref[...] = (dm_ * sa).astype(BF16)
        db_ref[...] = db.astype(BF16)
        dz_ref[:, 0:d] = dga.astype(BF16)
        dz_ref[:, d:2 * d] = dgb.astype(BF16)

        @pl.when(i == 0)
        def _():
            sum_ref[...] = jnp.zeros_like(sum_ref)
            dbias_ref[...] = jnp.zeros_like(dbias_ref)

        sum_ref[:, 0:d] += jnp.sum(dga, axis=0, keepdims=True)
        sum_ref[:, d:2 * d] += jnp.sum(dgb, axis=0, keepdims=True)
        dbias_ref[...] += jnp.sum(db, axis=0, keepdims=True)

    return _pcall(
        body, name="gate_bwd", grid=(lp // BLOCK,),
        in_specs=[_row_spec(d), _row_spec(d), _row_spec(d), _row_spec(d, 0), _row_spec(d, 1)],
        out_specs=[_row_spec(d), _row_spec(d), _row_spec(2 * d), _const_spec((1, 2 * d)), _const_spec((1, d))],
        out_shape=[jax.ShapeDtypeStruct((lp, d), BF16), jax.ShapeDtypeStruct((lp, d), BF16), jax.ShapeDtypeStruct((lp, 2 * d), BF16),
                   jax.ShapeDtypeStruct((1, 2 * d), F32), jax.ShapeDtypeStruct((1, d), F32)],
        compiler_params=_params(("arbitrary",)),
    )(dm, a, b, zg, zg)


def _swiglu_fwd(gu, after=None):
    lp = gu.shape[0]
    f = gu.shape[1] // 2

    def body(g_ref, u_ref, o_ref):
        g = g_ref[...].astype(F32)
        o_ref[...] = (g * _sigmoid(g) * u_ref[...].astype(F32)).astype(BF16)

    return _pcall(
        body, after=after, name="swiglu_fwd", grid=(lp // BLOCK,), in_specs=[_row_spec(f, 0), _row_spec(f, 1)], out_specs=_row_spec(f),
        out_shape=jax.ShapeDtypeStruct((lp, f), BF16), compiler_params=_params(("parallel",)),
    )(gu, gu)


def _swiglu_bwd(dact, gu):
    lp, f = dact.shape

    def body(d_ref, g_ref, u_ref, o_ref):
        g = g_ref[...].astype(F32)
        d = d_ref[...].astype(F32)
        sg = _sigmoid(g)
        o_ref[:, 0:f] = (d * u_ref[...].astype(F32) * (sg * (1.0 + g * (1.0 - sg)))).astype(BF16)
        o_ref[:, f:2 * f] = (d * g * sg).astype(BF16)

    return _pcall(
        body, name="swiglu_bwd", grid=(lp // BLOCK,), in_specs=[_row_spec(f), _row_spec(f, 0), _row_spec(f, 1)],
        out_specs=_row_spec(2 * f), out_shape=jax.ShapeDtypeStruct((lp, 2 * f), BF16), compiler_params=_params(("parallel",)),
    )(dact, gu, gu)


ANY = pl.BlockSpec(memory_space=pl.ANY)


def _all_gather_rows(x, name, after=None):
    r, c = x.shape

    def body(x_ref, out_ref, send_sems, recv_sems, local_sem):
        mx, my, mc = lax.axis_index("x"), lax.axis_index("y"), lax.axis_index("c")
        me, sibling = (mx, my, mc), (mx, my, 1 - mc)
        chips = [(1 - mx, my), (mx, 1 - my), (1 - mx, 1 - my)]

        def rows(px, py, pc):
            return out_ref.at[pl.ds((4 * px + 2 * py + pc) * r, r), :]

        def copy(k, block, to, src=None):
            return pltpu.make_async_remote_copy(
                src_ref=rows(*block) if src is None else src, dst_ref=rows(*block),
                send_sem=send_sems.at[k], recv_sem=recv_sems.at[k], device_id=to, device_id_type=MESH)

        mine = pltpu.make_async_copy(x_ref, rows(*me), local_sem)
        mine.start()
        first = [copy(0, me, sibling, src=x_ref)]
        first += [copy(1 + j, me, (*chip, mc), src=x_ref) for j, chip in enumerate(chips)]
        for cp in first:
            cp.start()
        passed = [copy(4 + j, (*chip, mc), sibling) for j, chip in enumerate(chips)]
        for j, chip in enumerate(chips):
            copy(1 + j, (*chip, mc), me).wait_recv()
            passed[j].start()
        copy(0, sibling, me).wait_recv()
        for j, chip in enumerate(chips):
            copy(4 + j, (*chip, 1 - mc), me).wait_recv()
        for cp in first + passed:
            cp.wait_send()
        mine.wait()

    return _pcall(
        body, after=after, name=name, in_specs=[ANY], out_specs=ANY, out_shape=jax.ShapeDtypeStruct((N_DEV * r, c), x.dtype),
        scratch_shapes=[pltpu.SemaphoreType.DMA((7,)), pltpu.SemaphoreType.DMA((7,)), pltpu.SemaphoreType.DMA(())],
    )(x)


HBM = pl.BlockSpec(memory_space=pltpu.HBM)
SEM = pl.BlockSpec(memory_space=pltpu.SEMAPHORE)
IN_FLIGHT = pltpu.CompilerParams(has_side_effects=pltpu.SideEffectType.DATAFLOW_SIDE_EFFECTING)
N_PEERS = 4


def _place_rows(shard, after, name):
    r, c = shard.shape
    tr = _pick(r, max(16, ELEMENTWISE_BLOCK_BYTES // (4 * c)), 16)
    steps = r // tr
    dev = (4 * lax.axis_index("x") + 2 * lax.axis_index("y") + lax.axis_index("c")).astype(jnp.int32).reshape(1)

    def body(dev_ref, x_ref, after_ref, o_ref):
        o_ref[...] = x_ref[...].astype(BF16)

    return _pcall(
        body, name=name,
        grid_spec=pltpu.PrefetchScalarGridSpec(
            num_scalar_prefetch=1, grid=(steps,),
            in_specs=[pl.BlockSpec((tr, c), lambda i, dev_ref: (i, 0)), pl.BlockSpec(memory_space=pl.ANY)],
            out_specs=pl.BlockSpec((tr, c), lambda i, dev_ref: (dev_ref[0] * steps + i, 0))),
        out_shape=jax.ShapeDtypeStruct((N_DEV * r, c), BF16), compiler_params=_params(("parallel",)),
    )(dev, shard, after)


def _gather_start(full, name):
    r = full.shape[0] // N_DEV

    def body(full_ref, send_sems, recv_sems, out_ref):
        mx, my, mc = lax.axis_index("x"), lax.axis_index("y"), lax.axis_index("c")
        mine = full_ref.at[pl.ds((4 * mx + 2 * my + mc) * r, r), :]
        for k, peer in enumerate([(mx, my, 1 - mc), (1 - mx, my, mc), (mx, 1 - my, mc), (1 - mx, 1 - my, mc)]):
            pltpu.make_async_remote_copy(
                src_ref=mine, dst_ref=mine, send_sem=send_sems.at[k], recv_sem=recv_sems.at[k],
                device_id=peer, device_id_type=MESH).start()

    return pl.pallas_call(
        body, name=name, in_specs=[HBM], out_specs=(SEM, SEM, HBM),
        out_shape=(pltpu.SemaphoreType.DMA((N_PEERS,)), pltpu.SemaphoreType.DMA((N_PEERS,)), pltpu.HBM(full.shape, full.dtype)),
        input_output_aliases={0: 2}, compiler_params=IN_FLIGHT,
    )(pltpu.with_memory_space_constraint(full, pltpu.HBM))


def _gather_wait(full, send_sem, recv_sem, after, name):
    r = full.shape[0] // N_DEV

    def body(full_ref, send_ref, recv_ref, after_ref, out_ref):
        mx, my, mc = lax.axis_index("x"), lax.axis_index("y"), lax.axis_index("c")
        block = full_ref.at[pl.ds(0, r), :]
        for k in range(N_PEERS):
            cp = pltpu.make_async_remote_copy(
                src_ref=block, dst_ref=block, send_sem=send_ref.at[k], recv_sem=recv_ref.at[k],
                device_id=(mx, my, mc), device_id_type=MESH)
            cp.wait_send()
            cp.wait_recv()

    return pl.pallas_call(
        body, name=name, in_specs=[HBM, SEM, SEM, pl.BlockSpec(memory_space=pl.ANY)], out_specs=HBM,
        out_shape=pltpu.HBM(full.shape, full.dtype), input_output_aliases={0: 0}, compiler_params=IN_FLIGHT,
    )(full, send_sem, recv_sem, after)


def _gather_forward_start(full, name):
    r = full.shape[0] // N_DEV

    def body(full_ref, send_sems, recv_sems, out_ref):
        mx, my, mc = lax.axis_index("x"), lax.axis_index("y"), lax.axis_index("c")
        for k, (px, py) in enumerate([(1 - mx, my), (mx, 1 - my), (1 - mx, 1 - my)]):
            rows = full_ref.at[pl.ds((4 * px + 2 * py + mc) * r, r), :]
            pltpu.make_async_remote_copy(
                src_ref=rows, dst_ref=rows, send_sem=send_sems.at[k], recv_sem=recv_sems.at[k],
                device_id=(mx, my, 1 - mc), device_id_type=MESH).start()

    return pl.pallas_call(
        body, name=name, in_specs=[HBM], out_specs=(SEM, SEM, HBM),
        out_shape=(pltpu.SemaphoreType.DMA((3,)), pltpu.SemaphoreType.DMA((3,)), pltpu.HBM(full.shape, full.dtype)),
        input_output_aliases={0: 2}, compiler_params=IN_FLIGHT,
    )(full)


def _gather_forward_wait(full, send_sem, recv_sem, after, name):
    r = full.shape[0] // N_DEV

    def body(full_ref, send_ref, recv_ref, after_ref, out_ref):
        mx, my, mc = lax.axis_index("x"), lax.axis_index("y"), lax.axis_index("c")
        block = full_ref.at[pl.ds(0, r), :]
        for k in range(3):
            cp = pltpu.make_async_remote_copy(
                src_ref=block, dst_ref=block, send_sem=send_ref.at[k], recv_sem=recv_ref.at[k],
                device_id=(mx, my, mc), device_id_type=MESH)
            cp.wait_send()
            cp.wait_recv()

    return pl.pallas_call(
        body, name=name, in_specs=[HBM, SEM, SEM, pl.BlockSpec(memory_space=pl.ANY)], out_specs=HBM,
        out_shape=pltpu.HBM(full.shape, full.dtype), input_output_aliases={0: 0}, compiler_params=IN_FLIGHT,
    )(full, send_sem, recv_sem, after)


def _pair_exchange_start(g, name):
    r = g.shape[0] // N_DEV
    c = g.shape[1]
    land = (len(CHIPS), r, c)

    def body(g_ref, land_ref, send_sems, recv_sems, g_out, land_out):
        mx, my, mc = lax.axis_index("x"), lax.axis_index("y"), lax.axis_index("c")
        for j, (px, py) in enumerate(CHIPS):
            pltpu.make_async_remote_copy(
                src_ref=g_ref.at[pl.ds((4 * px + 2 * py + 1 - mc) * r, r), :], dst_ref=land_ref.at[j],
                send_sem=send_sems.at[j], recv_sem=recv_sems.at[j], device_id=(mx, my, 1 - mc), device_id_type=MESH).start()

    return pl.pallas_call(
        body, name=name, in_specs=[HBM, HBM], out_specs=(SEM, SEM, HBM, HBM),
        out_shape=(pltpu.SemaphoreType.DMA((4,)), pltpu.SemaphoreType.DMA((4,)), pltpu.HBM(g.shape, g.dtype), pltpu.HBM(land, g.dtype)),
        input_output_aliases={0: 2, 1: 3}, compiler_params=IN_FLIGHT,
    )(pltpu.with_memory_space_constraint(g, pltpu.HBM), pltpu.with_memory_space_constraint(lax.empty(land, g.dtype), pltpu.HBM))


def _pair_exchange_wait(send_sem, recv_sem, g, land, after, name):
    def body(g_ref, land_ref, send_ref, recv_ref, after_ref, g_out, land_out):
        mx, my, mc = lax.axis_index("x"), lax.axis_index("y"), lax.axis_index("c")
        for j in range(len(CHIPS)):
            cp = pltpu.make_async_remote_copy(
                src_ref=land_ref.at[0], dst_ref=land_ref.at[0], send_sem=send_ref.at[j], recv_sem=recv_ref.at[j],
                device_id=(mx, my, mc), device_id_type=MESH)
            cp.wait_send()
            cp.wait_recv()

    return pl.pallas_call(
        body, name=name, in_specs=[HBM, HBM, SEM, SEM, pl.BlockSpec(memory_space=pl.ANY)], out_specs=(HBM, HBM),
        out_shape=(pltpu.HBM(g.shape, g.dtype), pltpu.HBM(land.shape, land.dtype)), input_output_aliases={0: 0, 1: 1},
        compiler_params=IN_FLIGHT,
    )(g, land, send_sem, recv_sem, after)


def _chip_exchange_start(ps, after, name):
    def body(ps_ref, rx_ref, after_ref, send_sems, recv_sems, ps_out, rx_out):
        mx, my, mc = lax.axis_index("x"), lax.axis_index("y"), lax.axis_index("c")
        chips = [(1 - mx, my), (mx, 1 - my), (1 - mx, 1 - my)]
        for k, (px, py) in enumerate(chips):
            pltpu.make_async_remote_copy(
                src_ref=ps_ref.at[2 * px + py], dst_ref=rx_ref.at[2 * mx + my], send_sem=send_sems.at[k], recv_sem=recv_sems.at[k],
                device_id=(px, py, mc), device_id_type=MESH).start()

    return pl.pallas_call(
        body, name=name, in_specs=[HBM, HBM, pl.BlockSpec(memory_space=pl.ANY)], out_specs=(SEM, SEM, HBM, HBM),
        out_shape=(pltpu.SemaphoreType.DMA((3,)), pltpu.SemaphoreType.DMA((3,)), pltpu.HBM(ps.shape, ps.dtype), pltpu.HBM(ps.shape, ps.dtype)),
        input_output_aliases={0: 2, 1: 3}, compiler_params=IN_FLIGHT,
    )(pltpu.with_memory_space_constraint(ps, pltpu.HBM), pltpu.with_memory_space_constraint(lax.empty(ps.shape, ps.dtype), pltpu.HBM), after)


def _chip_exchange_wait(send_sem, recv_sem, ps, rx, after, name):
    def body(ps_ref, rx_ref, send_ref, recv_ref, after_ref, ps_out, rx_out):
        mx, my, mc = lax.axis_index("x"), lax.axis_index("y"), lax.axis_index("c")
        for k in range(3):
            cp = pltpu.make_async_remote_copy(
                src_ref=ps_ref.at[0], dst_ref=rx_ref.at[0], send_sem=send_ref.at[k], recv_sem=recv_ref.at[k],
                device_id=(mx, my, mc), device_id_type=MESH)
            cp.wait_send()
            cp.wait_recv()

    return pl.pallas_call(
        body, name=name, in_specs=[HBM, HBM, SEM, SEM, pl.BlockSpec(memory_space=pl.ANY)], out_specs=(HBM, HBM),
        out_shape=(pltpu.HBM(ps.shape, ps.dtype), pltpu.HBM(rx.shape, rx.dtype)), input_output_aliases={0: 0, 1: 1},
        compiler_params=IN_FLIGHT,
    )(ps, rx, send_sem, recv_sem, after)


def _sum_chips(ps, rx, name):
    n, r, c = rx.shape
    tr = _pick(r, max(8, ELEMENTWISE_BLOCK_BYTES // (4 * n * c)), 8)
    chip = (2 * lax.axis_index("x") + lax.axis_index("y")).astype(jnp.int32).reshape(1)

    def body(chip_ref, own_ref, x_ref, o_ref):
        me = chip_ref[0]
        own = own_ref[0].astype(F32)
        acc = jnp.where(me == 0, own, x_ref[0].astype(F32))
        for j in range(1, n):
            acc = acc + jnp.where(me == j, own, x_ref[j].astype(F32))
        o_ref[...] = acc

    return _pcall(
        body, name=name,
        grid_spec=pltpu.PrefetchScalarGridSpec(
            num_scalar_prefetch=1, grid=(r // tr,),
            in_specs=[pl.BlockSpec((1, tr, c), lambda i, chip_ref: (chip_ref[0], i, 0)), pl.BlockSpec((n, tr, c), lambda i, chip_ref: (0, i, 0))],
            out_specs=pl.BlockSpec((tr, c), lambda i, chip_ref: (i, 0))),
        out_shape=jax.ShapeDtypeStruct((r, c), F32), compiler_params=_params(("parallel",)),
    )(chip, ps, rx)


def _pair_exchange(g, name):
    r = g.shape[0] // N_DEV
    c = g.shape[1]

    def body(g_ref, theirs_ref, send_sems, recv_sems):
        mx, my, mc = lax.axis_index("x"), lax.axis_index("y"), lax.axis_index("c")
        sibling = (mx, my, 1 - mc)
        copies = []
        for j, (px, py) in enumerate(CHIPS):
            give = g_ref.at[pl.ds((4 * px + 2 * py + 1 - mc) * r, r), :]
            rc = pltpu.make_async_remote_copy(
                src_ref=give, dst_ref=theirs_ref.at[j], send_sem=send_sems.at[j], recv_sem=recv_sems.at[j],
                device_id=sibling, device_id_type=MESH)
            rc.start()
            copies.append(rc)
        for cp in copies:
            cp.wait()

    return _pcall(
        body, name=name, in_specs=[ANY], out_specs=ANY, out_shape=jax.ShapeDtypeStruct((len(CHIPS), r, c), g.dtype),
        scratch_shapes=[pltpu.SemaphoreType.DMA((4,)), pltpu.SemaphoreType.DMA((4,))],
    )(g)


def _pair_sum(g, theirs, name):
    nch, r, c = theirs.shape
    tr = _pick(r, max(16, 2 * ELEMENTWISE_BLOCK_BYTES // (2 * c)), 16)
    core = lax.axis_index("c").astype(jnp.int32).reshape(1)

    def body(core_ref, a_ref, b_ref, o_ref):
        o_ref[...] = (a_ref[...].astype(F32) + b_ref[...].astype(F32)).astype(o_ref.dtype)

    spec = pl.BlockSpec((1, tr, c), lambda j, i, core_ref: (j, i, 0))
    own = pl.BlockSpec((1, tr, c), lambda j, i, core_ref: (2 * j + core_ref[0], i, 0))
    return _pcall(
        body, name=name,
        grid_spec=pltpu.PrefetchScalarGridSpec(num_scalar_prefetch=1, grid=(nch, r // tr), in_specs=[own, spec], out_specs=spec),
        out_shape=jax.ShapeDtypeStruct(theirs.shape, theirs.dtype), compiler_params=_params(("parallel", "parallel")),
    )(core, g.reshape(N_DEV, r, c), theirs)


def _sum_blocks(rx, name):
    n, r, c = rx.shape
    tr = _pick(r, max(8, ELEMENTWISE_BLOCK_BYTES // (4 * n * c)), 8)

    def body(x_ref, o_ref):
        acc = x_ref[0].astype(F32)
        for j in range(1, n):
            acc = acc + x_ref[j].astype(F32)
        o_ref[...] = acc

    return _pcall(
        body, name=name, grid=(r // tr,), in_specs=[pl.BlockSpec((n, tr, c), lambda i: (0, i, 0))],
        out_specs=pl.BlockSpec((tr, c), lambda i: (i, 0)), out_shape=jax.ShapeDtypeStruct((r, c), F32),
        compiler_params=_params(("parallel",)),
    )(rx)


def _adamw(w, g, m, v, name):
    r, c = w.shape
    tr = _pick(r, max(8, ELEMENTWISE_BLOCK_BYTES // (4 * c)), 8)
    c1 = 1.0 - ADAM_B1 ** ADAM_STEP
    c2 = 1.0 - ADAM_B2 ** ADAM_STEP

    def body(w_ref, g_ref, m_ref, v_ref, d_ref, nm_ref, nv_ref):
        gg = g_ref[...]
        nm = ADAM_B1 * m_ref[...] + (1.0 - ADAM_B1) * gg
        nv = ADAM_B2 * v_ref[...] + (1.0 - ADAM_B2) * (gg * gg)
        d_ref[...] = -ADAM_LR * ((nm / c1) / (jnp.sqrt(nv / c2) + ADAM_EPS) + ADAM_WD * w_ref[...])
        nm_ref[...] = nm
        nv_ref[...] = nv

    spec = pl.BlockSpec((tr, c), lambda i: (i, 0))
    shp = jax.ShapeDtypeStruct((r, c), F32)
    return _pcall(
        body, name=name, grid=(r // tr,), in_specs=[spec] * 4, out_specs=[spec] * 3, out_shape=[shp] * 3,
        compiler_params=_params(("parallel",)),
    )(w, g, m, v)


def _pack(parts):
    flat, layout, row = [], [], 0
    for p in parts:
        n = p.size
        rows = -(-n // LANES)
        flat.append(jnp.pad(p.reshape(-1).astype(F32), (0, rows * LANES - n)))
        layout.append((row, n, p.shape))
        row += rows
    total = -(-row // 8) * 8
    if total > row:
        flat.append(jnp.zeros(((total - row) * LANES,), F32))
    return jnp.concatenate(flat).reshape(total, LANES), layout


def _unpack(slab, layout):
    flat = slab.reshape(-1)
    return [flat[row * LANES:row * LANES + n].reshape(shape) for row, n, shape in layout]


def kernel(x, meta_tokens, mix_norm_g, w_in, b_in, attn_sinks, conv_w, conv_b, conv_ln_g, conv_ln_b, w_attn_o, w_conv_o, b_conv_o, w_out, ffn_norm_g, w_gate_up, w_down, final_norm_g, loss_target, m_meta_tokens, m_mix_norm_g, m_w_in, m_b_in, m_attn_sinks, m_conv_w, m_conv_b, m_conv_ln_g, m_conv_ln_b, m_w_attn_o, m_w_conv_o, m_b_conv_o, m_w_out, m_ffn_norm_g, m_w_gate_up, m_w_down, m_final_norm_g, v_meta_tokens, v_mix_norm_g, v_w_in, v_b_in, v_attn_sinks, v_conv_w, v_conv_b, v_conv_ln_g, v_conv_ln_b, v_w_attn_o, v_w_conv_o, v_b_conv_o, v_w_out, v_ffn_norm_g, v_w_gate_up, v_w_down, v_final_norm_g):
    xs = x[0]
    tgt = loss_target[0]
    s, d = xs.shape
    lp = s + BLOCK
    cd = conv_b.shape[1]
    ffn = w_down.shape[1] * N_DEV
    dev = 4 * lax.axis_index("x") + 2 * lax.axis_index("y") + lax.axis_index("c")
    cw_cols = conv_w.shape[3]
    meta_cols = meta_tokens.shape[1]

    small, small_layout = _pack([meta_tokens, jnp.pad(conv_w[0, :, 0, :], ((0, CONV_ROWS - CONV_WIDTH), (0, 0)))])
    small_flat = _all_gather_rows(small, "gather_small")
    small_all = small_flat.reshape(N_DEV, *small.shape)
    meta_parts, cw_parts = zip(*[_unpack(small_all[j], small_layout) for j in range(N_DEV)])
    meta_full = jnp.concatenate(meta_parts, axis=1)
    conv_w_full = jnp.concatenate(cw_parts, axis=1)
    g_send, g_recv, g_full = [], [], []
    tok = small_flat
    for shard, name in ((w_in[0].T, "w_in"), (w_attn_o[0].T, "w_attn_o"), (w_conv_o[0].T, "w_conv_o"), (w_out[0], "w_out"),
                        (w_gate_up[0].T, "w_gate_up"), (w_down[0], "w_down")):
        send_sem, recv_sem, tok = _gather_start(_place_rows(shard, tok, "place_" + name), "gather_start_" + name)
        g_send.append(send_sem)
        g_recv.append(recv_sem)
        g_full.append(tok)

    def arrived(w, after, name):
        full = _gather_wait(g_full[w], g_send[w], g_recv[w], after, "gather_wait_" + name)
        return _gather_forward_start(full, "gather_forward_start_" + name)

    def whole(passing, after, name):
        return _gather_forward_wait(passing[2], passing[0], passing[1], after, "gather_forward_wait_" + name)

    ctab, stab = _rope_tables(lp)
    mm = functools.partial(_matmul, tm=1056, tn=1024)

    passing = arrived(0, tok, "w_in")
    h0, u = _prep(xs, meta_full, mix_norm_g, after=passing[2])
    win_t = whole(passing, u, "w_in")
    bq, bkv, bc, bg = b_in[:, :Q_DIM], b_in[:, Q_DIM:Q_DIM + 2 * KV_DIM], b_in[:, Q_DIM + 2 * KV_DIM:Q_DIM + 2 * KV_DIM + 2 * cd], b_in[:, Q_DIM + 2 * KV_DIM + 2 * cd:]
    o_kv, o_c, o_g = Q_DIM, Q_DIM + 2 * KV_DIM, Q_DIM + 2 * KV_DIM + 2 * cd
    in_proj = functools.partial(_matmul, u, win_t, mode="nt", out_dtype=BF16, tm=2112, tn=512, tk=d)
    zq = in_proj(name="in_proj_q", bias=bq, b_row_off=0, b_rows=Q_DIM)
    zkv = in_proj(name="in_proj_kv", bias=bkv, b_row_off=o_kv, b_rows=2 * KV_DIM)
    zc = in_proj(name="in_proj_conv", bias=bc, b_row_off=o_c, b_rows=2 * cd)
    zg = in_proj(name="in_proj_gates", bias=bg, b_row_off=o_g, b_rows=2 * d)
    passing = arrived(1, zg, "w_attn_o")
    q_rot, k_sh, v_sh = _rope_fwd(zq, zkv, ctab, stab, after=passing[2])
    o = _attn_fwd(q_rot, k_sh, v_sh, attn_sinks)
    wao_t = whole(passing, o, "w_attn_o")
    br_a = mm(o, wao_t, mode="nt", name="attn_out_proj", out_dtype=BF16, tk=Q_DIM)
    passing = arrived(2, br_a, "w_conv_o")
    conv_out, c2 = _conv_fwd(zc, conv_w_full, conv_b, conv_ln_g, conv_ln_b, after=passing[2])
    wco_t = whole(passing, c2, "w_conv_o")
    br_b = mm(c2, wco_t, mode="nt", name="conv_out_proj", out_dtype=BF16, tk=cd, bias=b_conv_o)
    passing = arrived(3, br_b, "w_out")
    merged = _gate_fwd(br_a, br_b, zg, after=passing[2])
    wout = whole(passing, merged, "w_out")
    passing = arrived(4, wout, "w_gate_up")
    h1 = mm(merged, wout, mode="nn", name="mix_out_proj", out_dtype=F32, tn=512, tk=d, residual=h0, after=passing[2])
    u2 = _rmsnorm_fwd(h1, ffn_norm_g, "ffn_rmsnorm")
    wgu_t = whole(passing, u2, "w_gate_up")
    gu = _matmul(u2, wgu_t, mode="nt", name="ffn_gate_up", out_dtype=BF16, tm=2112, tn=512, tk=d)
    passing = arrived(5, gu, "w_down")
    act = _swiglu_fwd(gu, after=passing[2])
    wdown = whole(passing, act, "w_down")
    h2 = mm(act, wdown, mode="nn", name="ffn_down", out_dtype=F32, tn=512, tk=ffn // 2, residual=h1)
    dh2, dh2_b, loss_part, d_final_g = _final(h2, tgt, final_norm_g.reshape(1, d))

    wgrad = functools.partial(_matmul, mode="tn", out_dtype=BF16, tk=lp, tn=2048, b_inner=False)
    in_flight = {}

    def scatter_begin(g, name):
        return _pair_exchange_start(g, "rs_" + name + "_pair_start")

    def scatter_go_on(pair, after, name):
        g, theirs = _pair_exchange_wait(pair[0], pair[1], pair[2], pair[3], after, "rs_" + name + "_pair_wait")
        ps = _pair_sum(g, theirs, "rs_" + name + "_pair_sum")
        in_flight[name] = _chip_exchange_start(ps, theirs, "rs_" + name + "_chip_start")
        return in_flight[name][2]

    g_wdown = wgrad(act, dh2_b, name="ffn_down_dw", tm=256)
    pair = scatter_begin(g_wdown, "w_down")
    dact = _matmul(dh2_b, wdown, mode="nt", name="ffn_down_dx", out_dtype=BF16, tm=2112, tn=256, tk=d, after=pair[2])
    tok = scatter_go_on(pair, dact, "w_down")
    dgu = _swiglu_bwd(dact, gu)
    g_wgu_t = wgrad(dgu, u2, name="ffn_gate_up_dw", tm=512, after=tok)
    pair = scatter_begin(g_wgu_t, "w_gate_up")
    du2 = mm(dgu, wgu_t, mode="nn", name="ffn_gate_up_dx", out_dtype=F32, tn=512, tk=ffn // 2, after=pair[2])
    tok = scatter_go_on(pair, du2, "w_gate_up")
    dh1, dh1_b, d_ffn_g = _rmsnorm_bwd(du2, h1, ffn_norm_g, dh2, "ffn_rmsnorm_bwd")
    g_wout = wgrad(merged, dh1_b, name="mix_out_dw", tm=512, after=tok)
    pair = scatter_begin(g_wout, "w_out")
    dmerged = mm(dh1_b, wout, mode="nt", name="mix_out_dx", out_dtype=BF16, tk=d, after=pair[2])
    tok = scatter_go_on(pair, dmerged, "w_out")
    d_a, d_b, dz_g, sum_g, d_bco = _gate_bwd(dmerged, br_a, br_b, zg)
    g_wao_t = wgrad(d_a, o, name="attn_out_dw", tm=512, after=tok)
    pair = scatter_begin(g_wao_t, "w_attn_o")
    do = mm(d_a, wao_t, mode="nn", name="attn_out_dx", out_dtype=BF16, tk=d, after=pair[2])
    tok = scatter_go_on(pair, do, "w_attn_o")
    g_wco_t = wgrad(d_b, c2, name="conv_out_dw", tm=512, after=tok)
    pair = scatter_begin(g_wco_t, "w_conv_o")
    dc2 = mm(d_b, wco_t, mode="nn", name="conv_out_dx", out_dtype=F32, tk=d, after=pair[2])
    tok = scatter_go_on(pair, dc2, "w_conv_o")
    dq, dk, dv, dkm, dvm, d_sinks = _attn_bwd(q_rot, k_sh, v_sh, attn_sinks, do)
    dz_qkv, sum_qkv = _rope_bwd(dq, dk, dv, dkm, dvm, ctab, stab)
    dco, d_ln_g, d_ln_b, d_conv_b = _conv_bwd_norm(dc2, conv_out, conv_ln_g, conv_ln_b)
    dz_c, sum_c, d_conv_w = _conv_bwd_taps(dco, zc, conv_w_full)
    dz = jnp.concatenate([dz_qkv, dz_c, dz_g], axis=1)
    d_b_in = jnp.concatenate([sum_qkv, sum_c, sum_g], axis=1)
    in_dim = dz.shape[1]
    g_win_t = wgrad(dz, u, name="in_proj_dw", tm=512, after=tok)
    theirs = _pair_exchange(g_win_t, "rs_w_in_pair_exchange")
    in_flight["w_in"] = _chip_exchange_start(_pair_sum(g_win_t, theirs, "rs_w_in_pair_sum"), theirs, "rs_w_in_chip_start")
    du = mm(dz, win_t, mode="nn", name="in_proj_dx", out_dtype=F32, tk=in_dim // 4, after=in_flight["w_in"][2])
    grad_x, d_meta, d_mix_g = _rmsnorm_bwd_first(du, h0, mix_norm_g, dh1)

    weights = dict(meta_tokens=meta_tokens, mix_norm_g=mix_norm_g, w_in=w_in, b_in=b_in, attn_sinks=attn_sinks, conv_w=conv_w,
                   conv_b=conv_b, conv_ln_g=conv_ln_g, conv_ln_b=conv_ln_b, w_attn_o=w_attn_o, w_conv_o=w_conv_o, b_conv_o=b_conv_o,
                   w_out=w_out, ffn_norm_g=ffn_norm_g, w_gate_up=w_gate_up, w_down=w_down, final_norm_g=final_norm_g)
    m_in = dict(meta_tokens=m_meta_tokens, mix_norm_g=m_mix_norm_g, w_in=m_w_in, b_in=m_b_in, attn_sinks=m_attn_sinks, conv_w=m_conv_w,
                conv_b=m_conv_b, conv_ln_g=m_conv_ln_g, conv_ln_b=m_conv_ln_b, w_attn_o=m_w_attn_o, w_conv_o=m_w_conv_o,
                b_conv_o=m_b_conv_o, w_out=m_w_out, ffn_norm_g=m_ffn_norm_g, w_gate_up=m_w_gate_up, w_down=m_w_down,
                final_norm_g=m_final_norm_g)
    v_in = dict(meta_tokens=v_meta_tokens, mix_norm_g=v_mix_norm_g, w_in=v_w_in, b_in=v_b_in, attn_sinks=v_attn_sinks, conv_w=v_conv_w,
                conv_b=v_conv_b, conv_ln_g=v_conv_ln_g, conv_ln_b=v_conv_ln_b, w_attn_o=v_w_attn_o, w_conv_o=v_w_conv_o,
                b_conv_o=v_b_conv_o, w_out=v_w_out, ffn_norm_g=v_ffn_norm_g, w_gate_up=v_w_gate_up, w_down=v_w_down,
                final_norm_g=v_final_norm_g)
    names = list(weights)
    grads, delta, new_m, new_v = {}, {}, {}, {}
    transposed = ("w_in", "w_attn_o", "w_conv_o", "w_gate_up")
    tok = grad_x
    for n in ("w_down", "w_gate_up", "w_out", "w_attn_o", "w_conv_o", "w_in"):
        send_sem, recv_sem, ps, rx = in_flight[n]
        ps, rx = _chip_exchange_wait(send_sem, recv_sem, ps, rx, tok, "rs_" + n + "_chip_wait")
        g = _sum_chips(ps, rx, "rs_" + n + "_sum")
        g = g.T if n in transposed else g
        shape = weights[n].shape
        dl, nm, nv = _adamw(weights[n].reshape(g.shape), g, m_in[n].reshape(g.shape), v_in[n].reshape(g.shape), "adamw_" + n)
        grads[n], delta[n], new_m[n], new_v[n] = g.reshape(shape), dl.reshape(shape), nm.reshape(shape), nv.reshape(shape)
        tok = dl

    slab, slab_layout = _pack([loss_part[:, :1], d_mix_g, d_b_in, d_sinks[:, :N_Q_HEADS], d_conv_b, d_ln_g, d_ln_b, d_bco,
                               d_ffn_g, d_final_g, d_conv_w, d_meta])
    slab_all = _all_gather_rows(slab, "gather_small_grads", after=tok).reshape(N_DEV, *slab.shape)
    (loss, g_mix_g, g_b_in, g_sinks, g_conv_b, g_ln_g, g_ln_b, g_bco, g_ffn_g, g_final_g, g_conv_w_full, g_meta_full
     ) = _unpack(_sum_blocks(slab_all, "sum_small_grads"), slab_layout)
    g_conv_w = lax.dynamic_slice(g_conv_w_full, (0, dev * cw_cols), (CONV_WIDTH, cw_cols)).reshape(conv_w.shape)
    g_meta = lax.dynamic_slice(g_meta_full, (0, dev * meta_cols), (N_META, meta_cols))
    g_final_g = g_final_g.reshape(final_norm_g.shape)
    grads.update(meta_tokens=g_meta, mix_norm_g=g_mix_g, b_in=g_b_in, attn_sinks=g_sinks, conv_w=g_conv_w, conv_b=g_conv_b,
                 conv_ln_g=g_ln_g, conv_ln_b=g_ln_b, b_conv_o=g_bco, ffn_norm_g=g_ffn_g, final_norm_g=g_final_g)
    rest = [n for n in names if n not in delta]
    w_slab, rest_layout = _pack([weights[n] for n in rest])
    g_slab, _ = _pack([grads[n] for n in rest])
    m_slab, _ = _pack([m_in[n] for n in rest])
    v_slab, _ = _pack([v_in[n] for n in rest])
    dl, nm, nv = _adamw(w_slab, g_slab, m_slab, v_slab, "adamw_small")
    for n, a, b, c in zip(rest, _unpack(dl, rest_layout), _unpack(nm, rest_layout), _unpack(nv, rest_layout)):
        delta[n], new_m[n], new_v[n] = a, b, c

    return (loss.reshape(()), grad_x[None], *[grads[n] for n in names], *[delta[n] for n in names],
            *[new_m[n] for n in names], *[new_v[n] for n in names])
```

```python
import functools
import math

import jax
import jax.numpy as jnp
from jax import lax
from jax.experimental import pallas as pl
from jax.experimental.pallas import tpu as pltpu

F32 = jnp.float32
BF16 = jnp.bfloat16

N_DEV = 8
BLOCK = 128
N_META = 16
PAD_ROWS = BLOCK - N_META
HEAD_DIM = 64
N_Q_HEADS = 32
N_KV_HEADS = 4
GROUP = N_Q_HEADS // N_KV_HEADS
Q_DIM = N_Q_HEADS * HEAD_DIM
KV_DIM = N_KV_HEADS * HEAD_DIM
WINDOW = 128
CONV_WIDTH = 31
CONV_ROWS = 32
ROPE_THETA = 10000.0
EPS = 1e-6
ATTN_SCALE = HEAD_DIM ** -0.5
NEG = -1e30

ADAM_LR = 0.001
ADAM_B1 = 0.9
ADAM_B2 = 0.999
ADAM_EPS = 1e-08
ADAM_WD = 0.01
ADAM_STEP = 10

VMEM_LIMIT_BYTES = 56 * 1024 * 1024
LANES = 128
ELEMENTWISE_BLOCK_BYTES = 2 * 1024 * 1024
MESH = pl.DeviceIdType.MESH
CHIPS = ((0, 0), (0, 1), (1, 0), (1, 1))


def _pcall(body, after=None, **kw):
    if after is None:
        return pl.pallas_call(body, **kw)
    in_specs = list(kw.pop("in_specs"))
    n_in = len(in_specs)

    def ordered_body(*refs):
        return body(*refs[:n_in], *refs[n_in + 1:])

    call = pl.pallas_call(ordered_body, in_specs=in_specs + [pl.BlockSpec(memory_space=pl.ANY)], **kw)
    return lambda *args: call(*args, after)


def _params(semantics=None):
    if semantics is None:
        return pltpu.CompilerParams(vmem_limit_bytes=VMEM_LIMIT_BYTES)
    return pltpu.CompilerParams(dimension_semantics=semantics, vmem_limit_bytes=VMEM_LIMIT_BYTES)


def _pick(dim, pref, align):
    best = None
    t = align
    while t <= min(dim, pref):
        if dim % t == 0:
            best = t
        t += align
    return dim if best is None else best


def _sigmoid(x):
    return 1.0 / (1.0 + jnp.exp(-x))


def _matmul(a, b, *, mode, name, out_dtype, tm, tn, tk, bias=None, residual=None, b_inner=True,
            b_row_off=0, b_rows=None, after=None):
    if mode == "nn":
        m, k = a.shape
        n = b.shape[1]
    elif mode == "nt":
        m, k = a.shape
        n = b.shape[0] if b_rows is None else b_rows
    else:
        k, m = a.shape
        n = b.shape[1]
    tm = _pick(m, tm, 16)
    tn = _pick(math.gcd(n, b_row_off) if mode == "nt" and b_row_off else n, tn, LANES)
    tk = _pick(k, tk, LANES if mode != "tn" else 16)
    nm, nn, nk = m // tm, n // tn, k // tk
    if mode == "nt":
        assert b_row_off % tn == 0
    off = b_row_off // tn if mode == "nt" else 0

    if b_inner:
        grid = (nm, nn, nk)
        ij = lambda g0, g1: (g0, g1)
    else:
        grid = (nn, nm, nk)
        ij = lambda g0, g1: (g1, g0)

    if mode == "tn":
        a_spec = pl.BlockSpec((tk, tm), lambda g0, g1, kk: (kk, ij(g0, g1)[0]))
    else:
        a_spec = pl.BlockSpec((tm, tk), lambda g0, g1, kk: (ij(g0, g1)[0], kk))
    if mode == "nt":
        b_spec = pl.BlockSpec((tn, tk), lambda g0, g1, kk: (ij(g0, g1)[1] + off, kk))
    else:
        b_spec = pl.BlockSpec((tk, tn), lambda g0, g1, kk: (kk, ij(g0, g1)[1]))
    o_spec = pl.BlockSpec((tm, tn), lambda g0, g1, kk: ij(g0, g1))
    in_specs = [a_spec, b_spec]
    args = [a, b]
    if bias is not None:
        in_specs.append(pl.BlockSpec((1, tn), lambda g0, g1, kk: (0, ij(g0, g1)[1])))
        args.append(bias)
    if residual is not None:
        in_specs.append(o_spec)
        args.append(residual)
    dims = {"nn": (((1,), (0,)), ((), ())), "nt": (((1,), (1,)), ((), ())), "tn": (((0,), (0,)), ((), ()))}[mode]
    has_bias, has_res = bias is not None, residual is not None

    def body(*refs):
        a_ref, b_ref = refs[0], refs[1]
        pos = 2
        bias_ref = res_ref = None
        if has_bias:
            bias_ref = refs[pos]
            pos += 1
        if has_res:
            res_ref = refs[pos]
            pos += 1
        o_ref = refs[pos]
        acc_ref = refs[pos + 1] if nk > 1 else None

        def finish(acc):
            if has_bias:
                acc = acc + bias_ref[...]
            if has_res:
                acc = acc + res_ref[...]
            o_ref[...] = acc.astype(out_dtype)

        p = lax.dot_general(a_ref[...], b_ref[...], dims, preferred_element_type=F32)
        if nk == 1:
            finish(p)
        else:
            kk = pl.program_id(2)

            @pl.when(kk == 0)
            def _():
                acc_ref[...] = p

            @pl.when(kk > 0)
            def _():
                acc_ref[...] += p

            @pl.when(kk == nk - 1)
            def _():
                finish(acc_ref[...])

    return _pcall(
        body, after=after, name=name, grid=grid, in_specs=in_specs, out_specs=o_spec,
        out_shape=jax.ShapeDtypeStruct((m, n), out_dtype),
        scratch_shapes=[pltpu.VMEM((tm, tn), F32)] if nk > 1 else [],
        compiler_params=_params(("parallel", "parallel", "arbitrary")),
    )(*args)


def _row_spec(width, col=0):
    return pl.BlockSpec((BLOCK, width), lambda i: (i, col))


def _const_spec(shape):
    nd = len(shape)
    return pl.BlockSpec(shape, lambda i: (0,) * nd)


def _prep(x, meta_full, g, after=None):
    s, d = x.shape
    lp = s + BLOCK
    nb = lp // BLOCK

    def body(x_ref, meta_ref, g_ref, h_ref, u_ref):
        i = pl.program_id(0)

        @pl.when(i == 0)
        def _():
            h_ref[0:PAD_ROWS, :] = jnp.zeros((PAD_ROWS, d), F32)
            h_ref[PAD_ROWS:BLOCK, :] = meta_ref[...]

        @pl.when(i > 0)
        def _():
            h_ref[...] = x_ref[...]

        h = h_ref[...]
        r = lax.rsqrt(jnp.mean(h * h, axis=-1, keepdims=True) + EPS)
        u_ref[...] = (h * r * g_ref[...]).astype(BF16)

    return _pcall(
        body, after=after, name="prep_rmsnorm", grid=(nb,),
        in_specs=[pl.BlockSpec((BLOCK, d), lambda i: (jnp.maximum(i - 1, 0), 0)), _const_spec((N_META, d)), _const_spec((1, d))],
        out_specs=[_row_spec(d), _row_spec(d)],
        out_shape=[jax.ShapeDtypeStruct((lp, d), F32), jax.ShapeDtypeStruct((lp, d), BF16)],
        compiler_params=_params(("arbitrary",)),
    )(x, meta_full, g)


def _rmsnorm_fwd(h, g, name):
    lp, d = h.shape

    def body(h_ref, g_ref, u_ref):
        x = h_ref[...]
        r = lax.rsqrt(jnp.mean(x * x, axis=-1, keepdims=True) + EPS)
        u_ref[...] = (x * r * g_ref[...]).astype(BF16)

    return _pcall(
        body, name=name, grid=(lp // BLOCK,), in_specs=[_row_spec(d), _const_spec((1, d))], out_specs=_row_spec(d),
        out_shape=jax.ShapeDtypeStruct((lp, d), BF16), compiler_params=_params(("parallel",)),
    )(h, g)


def _rms_bwd_core(dy, x, g):
    r = lax.rsqrt(jnp.mean(x * x, axis=-1, keepdims=True) + EPS)
    xhat = x * r
    dxhat = dy * g
    dx = r * (dxhat - xhat * jnp.mean(dxhat * xhat, axis=-1, keepdims=True))
    return dx, jnp.sum(dy * xhat, axis=0, keepdims=True)


def _rmsnorm_bwd(dy, h, g, dres, name):
    lp, d = h.shape

    def body(dy_ref, h_ref, g_ref, dres_ref, dh_ref, dhb_ref, dg_ref):
        i = pl.program_id(0)
        dx, dg = _rms_bwd_core(dy_ref[...], h_ref[...], g_ref[...])
        dh = dres_ref[...] + dx
        dh_ref[...] = dh
        dhb_ref[...] = dh.astype(BF16)

        @pl.when(i == 0)
        def _():
            dg_ref[...] = jnp.zeros_like(dg_ref)

        dg_ref[...] += dg

    return _pcall(
        body, name=name, grid=(lp // BLOCK,),
        in_specs=[_row_spec(d), _row_spec(d), _const_spec((1, d)), _row_spec(d)],
        out_specs=[_row_spec(d), _row_spec(d), _const_spec((1, d))],
        out_shape=[jax.ShapeDtypeStruct((lp, d), F32), jax.ShapeDtypeStruct((lp, d), BF16), jax.ShapeDtypeStruct((1, d), F32)],
        compiler_params=_params(("arbitrary",)),
    )(dy, h, g, dres)


def _rmsnorm_bwd_first(dy, h, g, dres):
    lp, d = h.shape
    s = lp - BLOCK

    def body(dy_ref, h_ref, g_ref, dres_ref, gx_ref, dmeta_ref, dg_ref):
        i = pl.program_id(0)
        dx, dg = _rms_bwd_core(dy_ref[...], h_ref[...], g_ref[...])
        dh = dres_ref[...] + dx
        gx_ref[...] = dh

        @pl.when(i == 0)
        def _():
            dmeta_ref[...] = dh[PAD_ROWS:BLOCK, :]
            dg_ref[...] = jnp.zeros_like(dg_ref)

        dg_ref[...] += dg

    return _pcall(
        body, name="rmsnorm_bwd_first", grid=(lp // BLOCK,),
        in_specs=[_row_spec(d), _row_spec(d), _const_spec((1, d)), _row_spec(d)],
        out_specs=[pl.BlockSpec((BLOCK, d), lambda i: (jnp.maximum(i - 1, 0), 0)), _const_spec((N_META, d)), _const_spec((1, d))],
        out_shape=[jax.ShapeDtypeStruct((s, d), F32), jax.ShapeDtypeStruct((N_META, d), F32), jax.ShapeDtypeStruct((1, d), F32)],
        compiler_params=_params(("arbitrary",)),
    )(dy, h, g, dres)


def _final(h2, tgt, g):
    lp, d = h2.shape

    def body(h_ref, t_ref, g_ref, dh_ref, dhb_ref, loss_ref, dg_ref):
        i = pl.program_id(0)
        x = h_ref[...]
        gg = g_ref[...]
        r = lax.rsqrt(jnp.mean(x * x, axis=-1, keepdims=True) + EPS)
        xhat = x * r
        y = xhat * gg
        live = (i > 0).astype(F32)
        err = (y - t_ref[...]) * live
        dy = err * (1.0 / d)
        dxhat = dy * gg
        dh = r * (dxhat - xhat * jnp.mean(dxhat * xhat, axis=-1, keepdims=True))
        dh_ref[...] = dh
        dhb_ref[...] = dh.astype(BF16)

        @pl.when(i == 0)
        def _():
            loss_ref[...] = jnp.zeros_like(loss_ref)
            dg_ref[...] = jnp.zeros_like(dg_ref)

        row_loss = jnp.mean(err * err, axis=-1, keepdims=True)
        loss_ref[...] += 0.5 * jnp.sum(row_loss, axis=0, keepdims=True)
        dg_ref[...] += jnp.sum(dy * xhat, axis=0, keepdims=True)

    return _pcall(
        body, name="final_norm_loss", grid=(lp // BLOCK,),
        in_specs=[_row_spec(d), pl.BlockSpec((BLOCK, d), lambda i: (jnp.maximum(i - 1, 0), 0)), _const_spec((1, d))],
        out_specs=[_row_spec(d), _row_spec(d), _const_spec((1, LANES)), _const_spec((1, d))],
        out_shape=[jax.ShapeDtypeStruct((lp, d), F32), jax.ShapeDtypeStruct((lp, d), BF16),
                   jax.ShapeDtypeStruct((1, LANES), F32), jax.ShapeDtypeStruct((1, d), F32)],
        compiler_params=_params(("arbitrary",)),
    )(h2, tgt, g)


def _swap_halves(x):
    w = x.shape[1]
    lane = lax.broadcasted_iota(jnp.int32, x.shape, 1)
    first = (lane & (HEAD_DIM - 1)) < (HEAD_DIM // 2)
    return jnp.where(first, pltpu.roll(x, w - HEAD_DIM // 2, 1), pltpu.roll(x, HEAD_DIM // 2, 1))


def _rope_tables(lp):
    pos = jnp.maximum(jnp.arange(lp, dtype=jnp.int32) - PAD_ROWS, 0).astype(F32)
    inv_freq = ROPE_THETA ** (-jnp.arange(0, HEAD_DIM, 2, dtype=F32) / HEAD_DIM)
    ang = pos[:, None] * inv_freq[None, :]
    c, s = jnp.cos(ang), jnp.sin(ang)
    reps = LANES // HEAD_DIM
    return jnp.tile(jnp.concatenate([c, c], axis=1), (1, reps)), jnp.tile(jnp.concatenate([-s, s], axis=1), (1, reps))


def _rope_fwd(zq, zkv, ctab, stab, after=None):
    lp = zq.shape[0]
    nb = lp // BLOCK
    back = lambda s: (jnp.maximum(s - 1, 0), 0)

    def body(zq_ref, zkv_ref, c_ref, s_ref, q_ref, k_ref, v_ref):
        step = pl.program_id(0)
        c128, s128 = c_ref[...], s_ref[...]

        def rope(x):
            reps = x.shape[1] // LANES
            return x * jnp.tile(c128, (1, reps)) + _swap_halves(x) * jnp.tile(s128, (1, reps))

        q_ref[...] = (rope(zq_ref[...].astype(F32)) * ATTN_SCALE).astype(BF16)
        kv = zkv_ref[...].astype(F32)
        k = rope(kv[:, :KV_DIM])
        v = kv[:, KV_DIM:]

        @pl.when(step == 0)
        def _():
            k_ref[...] = jnp.zeros_like(k_ref)
            v_ref[...] = jnp.zeros_like(v_ref)

        @pl.when(step > 0)
        def _():
            for h in range(N_KV_HEADS):
                k_ref[h] = k[:, h * HEAD_DIM:(h + 1) * HEAD_DIM].astype(BF16)
                v_ref[h] = v[:, h * HEAD_DIM:(h + 1) * HEAD_DIM].astype(BF16)

    kv_spec = pl.BlockSpec((N_KV_HEADS, BLOCK, HEAD_DIM), lambda s: (0, s, 0))
    return _pcall(
        body, after=after, name="rope_fwd", grid=(nb + 1,),
        in_specs=[pl.BlockSpec((BLOCK, Q_DIM), back), pl.BlockSpec((BLOCK, 2 * KV_DIM), back),
                  pl.BlockSpec((BLOCK, LANES), back), pl.BlockSpec((BLOCK, LANES), back)],
        out_specs=[pl.BlockSpec((BLOCK, Q_DIM), back), kv_spec, kv_spec],
        out_shape=[jax.ShapeDtypeStruct((lp, Q_DIM), BF16),
                   jax.ShapeDtypeStruct((N_KV_HEADS, lp + BLOCK, HEAD_DIM), BF16),
                   jax.ShapeDtypeStruct((N_KV_HEADS, lp + BLOCK, HEAD_DIM), BF16)],
        compiler_params=_params(("arbitrary",)),
    )(zq, zkv, ctab, stab)


def _rope_bwd(dq, dk, dv, dkm, dvm, ctab, stab):
    lp = dq.shape[0]
    width = Q_DIM + 2 * KV_DIM
    head_spec = pl.BlockSpec((N_KV_HEADS, BLOCK, HEAD_DIM), lambda i: (0, i, 0))
    meta_spec = _const_spec((N_KV_HEADS, BLOCK, HEAD_DIM))

    def body(dq_ref, dk_ref, dv_ref, dkm_ref, dvm_ref, c_ref, s_ref, dz_ref, sum_ref, kbuf, vbuf):
        i = pl.program_id(0)
        c128, s128 = c_ref[...], s_ref[...]
        first = (i == 0).astype(F32)

        def rope_t(x):
            reps = x.shape[1] // LANES
            return x * jnp.tile(c128, (1, reps)) + _swap_halves(x * jnp.tile(s128, (1, reps)))

        for h in range(N_KV_HEADS):
            kbuf[:, h * HEAD_DIM:(h + 1) * HEAD_DIM] = dk_ref[h] + first * dkm_ref[h]
            vbuf[:, h * HEAD_DIM:(h + 1) * HEAD_DIM] = dv_ref[h] + first * dvm_ref[h]
        dzq = rope_t(dq_ref[...] * ATTN_SCALE)
        dzk = rope_t(kbuf[...])
        dzv = vbuf[...]
        dz_ref[:, 0:Q_DIM] = dzq.astype(BF16)
        dz_ref[:, Q_DIM:Q_DIM + KV_DIM] = dzk.astype(BF16)
        dz_ref[:, Q_DIM + KV_DIM:width] = dzv.astype(BF16)

        @pl.when(i == 0)
        def _():
            sum_ref[...] = jnp.zeros_like(sum_ref)

        sum_ref[:, 0:Q_DIM] += jnp.sum(dzq, axis=0, keepdims=True)
        sum_ref[:, Q_DIM:Q_DIM + KV_DIM] += jnp.sum(dzk, axis=0, keepdims=True)
        sum_ref[:, Q_DIM + KV_DIM:width] += jnp.sum(dzv, axis=0, keepdims=True)

    return _pcall(
        body, name="rope_bwd", grid=(lp // BLOCK,),
        in_specs=[_row_spec(Q_DIM), head_spec, head_spec, meta_spec, meta_spec, _row_spec(LANES), _row_spec(LANES)],
        out_specs=[_row_spec(width), _const_spec((1, width))],
        out_shape=[jax.ShapeDtypeStruct((lp, width), BF16), jax.ShapeDtypeStruct((1, width), F32)],
        scratch_shapes=[pltpu.VMEM((BLOCK, KV_DIM), F32), pltpu.VMEM((BLOCK, KV_DIM), F32)],
        compiler_params=_params(("arbitrary",)),
    )(dq, dk, dv, dkm, dvm, ctab, stab)


def _attn_bias(i):
    r = lax.broadcasted_iota(jnp.int32, (BLOCK, 3 * BLOCK), 0)
    c = lax.broadcasted_iota(jnp.int32, (BLOCK, 3 * BLOCK), 1)
    qp = i * BLOCK + r - PAD_ROWS
    kp = (i - 1) * BLOCK + c - PAD_ROWS
    band = (c < 2 * BLOCK) & (kp >= N_META) & (kp <= qp) & (qp - kp < WINDOW)
    mp = c - 2 * BLOCK - PAD_ROWS
    meta = (c >= 2 * BLOCK) & (mp >= 0) & (mp <= qp)
    return jnp.where(band | meta, 0.0, NEG).astype(F32)


HALF = BLOCK // 2
HALF_KEYS = 2 * BLOCK


def _half_keys(prev, own, meta, half):
    if half == 0:
        return jnp.concatenate([prev, own[0:HALF], meta[HALF:BLOCK]], axis=0)
    return jnp.concatenate([prev[HALF:BLOCK], own, meta[HALF:BLOCK]], axis=0)


def _half_bias(i, half):
    r = lax.broadcasted_iota(jnp.int32, (HALF, HALF_KEYS), 0) + half * HALF
    c = lax.broadcasted_iota(jnp.int32, (HALF, HALF_KEYS), 1)
    n_prev = BLOCK - half * HALF
    qp = i * BLOCK + r - PAD_ROWS
    kp = jnp.where(c < n_prev, (i - 1) * BLOCK + c + half * HALF, i * BLOCK + c - n_prev) - PAD_ROWS
    band = (c < HALF_KEYS - HALF) & (kp >= N_META) & (kp <= qp) & (qp - kp < WINDOW)
    mp = c - (HALF_KEYS - HALF) + HALF - PAD_ROWS
    meta = (c >= HALF_KEYS - HALF) & (mp >= 0) & (mp <= qp)
    return jnp.where(band | meta, 0.0, NEG).astype(F32)


def _half_rows(ref, heads, half):
    rows = slice(half * HALF, (half + 1) * HALF)
    return jnp.concatenate([ref[rows, n * HEAD_DIM:(n + 1) * HEAD_DIM] for n in heads], axis=0)


def _half_sinks(sink_ref, heads):
    return jnp.concatenate([jnp.broadcast_to(sink_ref[0:1, n:n + 1], (HALF, 1)) for n in heads], axis=0)


def _stack_heads(ref, h):
    return jnp.concatenate(
        [ref[:, (h * GROUP + g) * HEAD_DIM:(h * GROUP + g + 1) * HEAD_DIM] for g in range(GROUP)], axis=0)


def _attn_probs(qs, k3, bias8, sink):
    s = lax.dot_general(qs, k3, (((1,), (1,)), ((), ())), preferred_element_type=F32) + bias8
    m = jnp.maximum(jnp.max(s, axis=1, keepdims=True), sink)
    p = jnp.exp(s - m)
    ps = jnp.exp(sink - m)
    inv = 1.0 / (jnp.sum(p, axis=1, keepdims=True) + ps)
    return p * inv, ps * inv


def _sink_column(sink_ref, h):
    return jnp.concatenate(
        [jnp.broadcast_to(sink_ref[0:1, h * GROUP + g:h * GROUP + g + 1], (BLOCK, 1)) for g in range(GROUP)], axis=0)


def _attn_fwd(q, k_sh, v_sh, sinks):
    lp = q.shape[0]
    nb = lp // BLOCK
    kv = lambda f: pl.BlockSpec((N_KV_HEADS, BLOCK, HEAD_DIM), f)

    def body(q_ref, kp_ref, kc_ref, km_ref, vp_ref, vc_ref, vm_ref, sink_ref, o_ref):
        i = pl.program_id(0)
        for half in range(2):
            bias = jnp.tile(_half_bias(i, half), (GROUP, 1))
            rows = slice(half * HALF, (half + 1) * HALF)
            for h in range(N_KV_HEADS):
                heads = range(h * GROUP, (h + 1) * GROUP)
                keys = _half_keys(kp_ref[h], kc_ref[h], km_ref[h], half)
                vals = _half_keys(vp_ref[h], vc_ref[h], vm_ref[h], half)
                p, _ = _attn_probs(_half_rows(q_ref, heads, half), keys, bias, _half_sinks(sink_ref, heads))
                o = jnp.dot(p.astype(BF16), vals, preferred_element_type=F32)
                for j, n in enumerate(heads):
                    o_ref[rows, n * HEAD_DIM:(n + 1) * HEAD_DIM] = o[j * HALF:(j + 1) * HALF].astype(BF16)

    prev, cur, meta = (lambda i: (0, i, 0)), (lambda i: (0, i + 1, 0)), (lambda i: (0, 1, 0))
    return _pcall(
        body, name="attn_fwd", grid=(nb,),
        in_specs=[_row_spec(Q_DIM), kv(prev), kv(cur), kv(meta), kv(prev), kv(cur), kv(meta), _const_spec((1, N_Q_HEADS))],
        out_specs=_row_spec(Q_DIM), out_shape=jax.ShapeDtypeStruct((lp, Q_DIM), BF16),
        compiler_params=_params(("parallel",)),
    )(q, k_sh, k_sh, k_sh, v_sh, v_sh, v_sh, sinks)


def _attn_bwd(q, k_sh, v_sh, sinks, do):
    lp = q.shape[0]
    nb = lp // BLOCK
    kv = lambda f: pl.BlockSpec((N_KV_HEADS, BLOCK, HEAD_DIM), f)
    cl = lambda s: jnp.minimum(s, nb - 1)

    def body(q_ref, do_ref, kp_ref, kc_ref, km_ref, vp_ref, vc_ref, vm_ref, sink_ref,
             dq_ref, dk_ref, dv_ref, dkm_ref, dvm_ref, dsink_ref, carry_k, carry_v):
        step = pl.program_id(0)

        @pl.when(step == 0)
        def _():
            carry_k[...] = jnp.zeros_like(carry_k)
            carry_v[...] = jnp.zeros_like(carry_v)
            dkm_ref[...] = jnp.zeros_like(dkm_ref)
            dvm_ref[...] = jnp.zeros_like(dvm_ref)
            dsink_ref[...] = jnp.zeros_like(dsink_ref)

        @pl.when(step < nb)
        def _():
            bias8 = jnp.tile(_attn_bias(step), (GROUP, 1))
            lane = lax.broadcasted_iota(jnp.int32, (1, LANES), 1)
            dsink = jnp.zeros((1, LANES), F32)
            for h in range(N_KV_HEADS):
                k3 = jnp.concatenate([kp_ref[h], kc_ref[h], km_ref[h]], axis=0)
                v3 = jnp.concatenate([vp_ref[h], vc_ref[h], vm_ref[h]], axis=0)
                qs = _stack_heads(q_ref, h)
                dos = _stack_heads(do_ref, h)
                p, psink = _attn_probs(qs, k3, bias8, _sink_column(sink_ref, h))
                dp = lax.dot_general(dos, v3, (((1,), (1,)), ((), ())), preferred_element_type=F32)
                delta = jnp.sum(p * dp, axis=1, keepdims=True)
                ds = (p * (dp - delta)).astype(BF16)
                dsk = -psink * delta
                for g in range(GROUP):
                    val = jnp.sum(dsk[g * BLOCK:(g + 1) * BLOCK], axis=0, keepdims=True)
                    dsink = dsink + jnp.where(lane == h * GROUP + g, val, 0.0)
                dqs = jnp.dot(ds, k3, preferred_element_type=F32)
                for g in range(GROUP):
                    n = h * GROUP + g
                    dq_ref[:, n * HEAD_DIM:(n + 1) * HEAD_DIM] = dqs[g * BLOCK:(g + 1) * BLOCK]
                dk3 = lax.dot_general(ds, qs, (((0,), (0,)), ((), ())), preferred_element_type=F32)
                dv3 = lax.dot_general(p.astype(BF16), dos, (((0,), (0,)), ((), ())), preferred_element_type=F32)
                dk_ref[h] = carry_k[h] + dk3[0:BLOCK]
                dv_ref[h] = carry_v[h] + dv3[0:BLOCK]
                carry_k[h] = dk3[BLOCK:2 * BLOCK]
                carry_v[h] = dv3[BLOCK:2 * BLOCK]
                dkm_ref[h] += dk3[2 * BLOCK:3 * BLOCK]
                dvm_ref[h] += dv3[2 * BLOCK:3 * BLOCK]
            dsink_ref[...] += dsink

        @pl.when(step == nb)
        def _():
            dk_ref[...] = carry_k[...]
            dv_ref[...] = carry_v[...]

    prev, cur, meta = (lambda s: (0, cl(s), 0)), (lambda s: (0, cl(s) + 1, 0)), (lambda s: (0, 1, 0))
    lag = lambda s: (0, jnp.maximum(s - 1, 0), 0)
    head_shape = jax.ShapeDtypeStruct((N_KV_HEADS, lp, HEAD_DIM), F32)
    meta_shape = jax.ShapeDtypeStruct((N_KV_HEADS, BLOCK, HEAD_DIM), F32)
    return _pcall(
        body, name="attn_bwd", grid=(nb + 1,),
        in_specs=[pl.BlockSpec((BLOCK, Q_DIM), lambda s: (cl(s), 0)), pl.BlockSpec((BLOCK, Q_DIM), lambda s: (cl(s), 0)),
                  kv(prev), kv(cur), kv(meta), kv(prev), kv(cur), kv(meta), _const_spec((1, N_Q_HEADS))],
        out_specs=[pl.BlockSpec((BLOCK, Q_DIM), lambda s: (cl(s), 0)), kv(lag), kv(lag),
                   _const_spec((N_KV_HEADS, BLOCK, HEAD_DIM)), _const_spec((N_KV_HEADS, BLOCK, HEAD_DIM)), _const_spec((1, LANES))],
        out_shape=[jax.ShapeDtypeStruct((lp, Q_DIM), F32), head_shape, head_shape, meta_shape, meta_shape,
                   jax.ShapeDtypeStruct((1, LANES), F32)],
        scratch_shapes=[pltpu.VMEM((N_KV_HEADS, BLOCK, HEAD_DIM), F32), pltpu.VMEM((N_KV_HEADS, BLOCK, HEAD_DIM), F32)],
        compiler_params=_params(("arbitrary",)),
    )(q, do, k_sh, k_sh, k_sh, v_sh, v_sh, v_sh, sinks)


CONV_CHUNK = 256


SUBLANES = 8
SH_BASE = BLOCK - 4 * SUBLANES
SH_ROWS = BLOCK + 3 * SUBLANES
DSH_ROWS = SH_ROWS


def _shifted_windows(src, sh, base, rows):
    for b in range(1, SUBLANES):
        sh[b] = src[base + b:base + b + rows, :]


def _window(src, sh, base, start, cols):
    a, b = divmod(start - base, SUBLANES)
    if b == 0:
        return src[start:start + BLOCK, cols]
    return sh[b, SUBLANES * a:SUBLANES * a + BLOCK, cols]


def _glu_masked(a_ref, g_ref, base):
    rows = base + lax.broadcasted_iota(jnp.int32, (BLOCK, 1), 0)
    return jnp.where(rows >= PAD_ROWS, a_ref[...].astype(F32) * _sigmoid(g_ref[...].astype(F32)), 0.0)


def _conv_fwd(zc, conv_w, conv_b, ln_g, ln_b, after=None):
    lp = zc.shape[0]
    cd = zc.shape[1] // 2
    nb = lp // BLOCK
    chunk = min(CONV_CHUNK, cd)
    back = lambda col: (lambda i: (jnp.maximum(i - 1, 0), col))
    lo = BLOCK - (CONV_WIDTH - 1)

    def body(ap_ref, gp_ref, ac_ref, gc_ref, w_ref, b_ref, lg_ref, lb_ref, co_ref, c2_ref, ext, sh):
        i = pl.program_id(0)
        ext[0:BLOCK, :] = _glu_masked(ap_ref, gp_ref, (i - 1) * BLOCK)
        ext[BLOCK:2 * BLOCK, :] = _glu_masked(ac_ref, gc_ref, i * BLOCK)
        _shifted_windows(ext, sh, SH_BASE, SH_ROWS)
        for c0 in range(0, cd, chunk):
            cols = slice(c0, c0 + chunk)
            acc = jnp.zeros((BLOCK, chunk), F32)
            for k in range(CONV_WIDTH):
                acc = acc + _window(ext, sh, SH_BASE, lo + k, cols) * w_ref[k:k + 1, cols]
            co_ref[:, cols] = acc + b_ref[:, cols]
        x = co_ref[...]
        mu = jnp.mean(x, axis=-1, keepdims=True)
        xc = x - mu
        r = lax.rsqrt(jnp.mean(xc * xc, axis=-1, keepdims=True) + EPS)
        y = xc * r * lg_ref[...] + lb_ref[...]
        c2_ref[...] = (y * _sigmoid(y)).astype(BF16)

    return _pcall(
        body, after=after, name="conv_fwd", grid=(nb,),
        in_specs=[pl.BlockSpec((BLOCK, cd), back(0)), pl.BlockSpec((BLOCK, cd), back(1)), _row_spec(cd, 0), _row_spec(cd, 1),
                  _const_spec((CONV_ROWS, cd)), _const_spec((1, cd)), _const_spec((1, cd)), _const_spec((1, cd))],
        out_specs=[_row_spec(cd), _row_spec(cd)],
        out_shape=[jax.ShapeDtypeStruct((lp, cd), F32), jax.ShapeDtypeStruct((lp, cd), BF16)],
        scratch_shapes=[pltpu.VMEM((2 * BLOCK, cd), F32), pltpu.VMEM((SUBLANES, SH_ROWS, cd), F32)],
        compiler_params=_params(("arbitrary",)),
    )(zc, zc, zc, zc, conv_w, conv_b, ln_g, ln_b)


def _conv_bwd_norm(dc2, conv_out, ln_g, ln_b):
    lp, cd = conv_out.shape

    def body(d_ref, x_ref, lg_ref, lb_ref, dco_ref, dlg_ref, dlb_ref, dcb_ref):
        i = pl.program_id(0)
        x = x_ref[...]
        g = lg_ref[...]
        mu = jnp.mean(x, axis=-1, keepdims=True)
        xc = x - mu
        r = lax.rsqrt(jnp.mean(xc * xc, axis=-1, keepdims=True) + EPS)
        xhat = xc * r
        y = xhat * g + lb_ref[...]
        sg = _sigmoid(y)
        dy = d_ref[...] * (sg * (1.0 + y * (1.0 - sg)))
        dxhat = dy * g
        dx = r * (dxhat - jnp.mean(dxhat, axis=-1, keepdims=True) - xhat * jnp.mean(dxhat * xhat, axis=-1, keepdims=True))
        dco_ref[...] = dx

        @pl.when(i == 0)
        def _():
            dlg_ref[...] = jnp.zeros_like(dlg_ref)
            dlb_ref[...] = jnp.zeros_like(dlb_ref)
            dcb_ref[...] = jnp.zeros_like(dcb_ref)

        dlg_ref[...] += jnp.sum(dy * xhat, axis=0, keepdims=True)
        dlb_ref[...] += jnp.sum(dy, axis=0, keepdims=True)
        dcb_ref[...] += jnp.sum(dx, axis=0, keepdims=True)

    vec = jax.ShapeDtypeStruct((1, cd), F32)
    return _pcall(
        body, name="conv_bwd_norm", grid=(lp // BLOCK,),
        in_specs=[_row_spec(cd), _row_spec(cd), _const_spec((1, cd)), _const_spec((1, cd))],
        out_specs=[_row_spec(cd), _const_spec((1, cd)), _const_spec((1, cd)), _const_spec((1, cd))],
        out_shape=[jax.ShapeDtypeStruct((lp, cd), F32), vec, vec, vec],
        compiler_params=_params(("arbitrary",)),
    )(dc2, conv_out, ln_g, ln_b)


def _conv_bwd_taps(dco, zc, conv_w):
    lp, cd = dco.shape
    nb = lp // BLOCK
    chunk = min(CONV_CHUNK, cd)
    back = lambda col: (lambda i: (jnp.maximum(i - 1, 0), col))
    fwd = lambda i: (jnp.minimum(i + 1, nb - 1), 0)
    lo = BLOCK - (CONV_WIDTH - 1)

    def body(dc_ref, dn_ref, ap_ref, gp_ref, ac_ref, gc_ref, w_ref, dz_ref, sum_ref, dw_ref, ext, dext, dcb, sh, dsh):
        i = pl.program_id(0)
        ext[0:BLOCK, :] = _glu_masked(ap_ref, gp_ref, (i - 1) * BLOCK)
        ext[BLOCK:2 * BLOCK, :] = _glu_masked(ac_ref, gc_ref, i * BLOCK)
        dext[0:BLOCK, :] = dc_ref[...]
        dext[BLOCK:2 * BLOCK, :] = dn_ref[...] * (i < nb - 1).astype(F32)
        _shifted_windows(ext, sh, SH_BASE, SH_ROWS)
        _shifted_windows(dext, dsh, 0, DSH_ROWS)

        @pl.when(i == 0)
        def _():
            dw_ref[...] = jnp.zeros_like(dw_ref)
            sum_ref[...] = jnp.zeros_like(sum_ref)

        for c0 in range(0, cd, chunk):
            cols = slice(c0, c0 + chunk)
            dcur = dext[0:BLOCK, cols]
            acc = jnp.zeros((BLOCK, chunk), F32)
            for k in range(CONV_WIDTH):
                s = CONV_WIDTH - 1 - k
                acc = acc + _window(dext, dsh, 0, s, cols) * w_ref[k:k + 1, cols]
                dw_ref[k:k + 1, cols] += jnp.sum(dcur * _window(ext, sh, SH_BASE, lo + k, cols), axis=0, keepdims=True)
            dcb[:, cols] = acc
        rows = i * BLOCK + lax.broadcasted_iota(jnp.int32, (BLOCK, 1), 0)
        dc = jnp.where(rows >= PAD_ROWS, dcb[...], 0.0)
        a = ac_ref[...].astype(F32)
        sg = _sigmoid(gc_ref[...].astype(F32))
        da = dc * sg
        dg = dc * a * sg * (1.0 - sg)
        dz_ref[:, 0:cd] = da.astype(BF16)
        dz_ref[:, cd:2 * cd] = dg.astype(BF16)
        sum_ref[:, 0:cd] += jnp.sum(da, axis=0, keepdims=True)
        sum_ref[:, cd:2 * cd] += jnp.sum(dg, axis=0, keepdims=True)

    return _pcall(
        body, name="conv_bwd_taps", grid=(nb,),
        in_specs=[_row_spec(cd), pl.BlockSpec((BLOCK, cd), fwd),
                  pl.BlockSpec((BLOCK, cd), back(0)), pl.BlockSpec((BLOCK, cd), back(1)), _row_spec(cd, 0), _row_spec(cd, 1),
                  _const_spec((CONV_ROWS, cd))],
        out_specs=[_row_spec(2 * cd), _const_spec((1, 2 * cd)), _const_spec((CONV_ROWS, cd))],
        out_shape=[jax.ShapeDtypeStruct((lp, 2 * cd), BF16), jax.ShapeDtypeStruct((1, 2 * cd), F32),
                   jax.ShapeDtypeStruct((CONV_ROWS, cd), F32)],
        scratch_shapes=[pltpu.VMEM((2 * BLOCK, cd), F32), pltpu.VMEM((2 * BLOCK, cd), F32), pltpu.VMEM((BLOCK, cd), F32),
                        pltpu.VMEM((SUBLANES, SH_ROWS, cd), F32), pltpu.VMEM((SUBLANES, DSH_ROWS, cd), F32)],
        compiler_params=_params(("arbitrary",)),
    )(dco, dco, zc, zc, zc, zc, conv_w)


def _gate_fwd(a, b, zg, after=None):
    lp, d = a.shape

    def body(a_ref, b_ref, ga_ref, gb_ref, m_ref):
        ga, gb = ga_ref[...].astype(F32), gb_ref[...].astype(F32)
        m_ref[...] = (_sigmoid(ga) * a_ref[...].astype(F32) + _sigmoid(gb) * b_ref[...].astype(F32)).astype(BF16)

    return _pcall(
        body, after=after, name="gate_fwd", grid=(lp // BLOCK,),
        in_specs=[_row_spec(d), _row_spec(d), _row_spec(d, 0), _row_spec(d, 1)], out_specs=_row_spec(d),
        out_shape=jax.ShapeDtypeStruct((lp, d), BF16), compiler_params=_params(("parallel",)),
    )(a, b, zg, zg)


def _gate_bwd(dm, a, b, zg):
    lp, d = a.shape

    def body(dm_ref, a_ref, b_ref, ga_ref, gb_ref, da_ref, db_ref, dz_ref, sum_ref, dbias_ref):
        i = pl.program_id(0)
        dm_ = dm_ref[...].astype(F32)
        sa = _sigmoid(ga_ref[...].astype(F32))
        sb = _sigmoid(gb_ref[...].astype(F32))
        db = dm_ * sb
        dga = dm_ * a_ref[...].astype(F32) * sa * (1.0 - sa)
        dgb = dm_ * b_ref[...].astype(F32) * sb * (1.0 - sb)
        da_ref[...] = (dm_ * sa).astype(BF16)
        db_ref[...] = db.astype(BF16)
        dz_ref[:, 0:d] = dga.astype(BF16)
        dz_ref[:, d:2 * d] = dgb.astype(BF16)

        @pl.when(i == 0)
        def _():
            sum_ref[...] = jnp.zeros_like(sum_ref)
            dbias_ref[...] = jnp.zeros_like(dbias_ref)

        sum_ref[:, 0:d] += jnp.sum(dga, axis=0, keepdims=True)
        sum_ref[:, d:2 * d] += jnp.sum(dgb, axis=0, keepdims=True)
        dbias_ref[...] += jnp.sum(db, axis=0, keepdims=True)

    return _pcall(
        body, name="gate_bwd", grid=(lp // BLOCK,),
        in_specs=[_row_spec(d), _row_spec(d), _row_spec(d), _row_spec(d, 0), _row_spec(d, 1)],
        out_specs=[_row_spec(d), _row_spec(d), _row_spec(2 * d), _const_spec((1, 2 * d)), _const_spec((1, d))],
        out_shape=[jax.ShapeDtypeStruct((lp, d), BF16), jax.ShapeDtypeStruct((lp, d), BF16), jax.ShapeDtypeStruct((lp, 2 * d), BF16),
                   jax.ShapeDtypeStruct((1, 2 * d), F32), jax.ShapeDtypeStruct((1, d), F32)],
        compiler_params=_params(("arbitrary",)),
    )(dm, a, b, zg, zg)


def _swiglu_fwd(gu, after=None):
    lp = gu.shape[0]
    f = gu.shape[1] // 2

    def body(g_ref, u_ref, o_ref):
        g = g_ref[...].astype(F32)
        o_ref[...] = (g * _sigmoid(g) * u_ref[...].astype(F32)).astype(BF16)

    return _pcall(
        body, after=after, name="swiglu_fwd", grid=(lp // BLOCK,), in_specs=[_row_spec(f, 0), _row_spec(f, 1)], out_specs=_row_spec(f),
        out_shape=jax.ShapeDtypeStruct((lp, f), BF16), compiler_params=_params(("parallel",)),
    )(gu, gu)


def _swiglu_bwd(dact, gu):
    lp, f = dact.shape

    def body(d_ref, g_ref, u_ref, o_ref):
        g = g_ref[...].astype(F32)
        d = d_ref[...].astype(F32)
        sg = _sigmoid(g)
        o_ref[:, 0:f] = (d * u_ref[...].astype(F32) * (sg * (1.0 + g * (1.0 - sg)))).astype(BF16)
        o_ref[:, f:2 * f] = (d * g * sg).astype(BF16)

    return _pcall(
        body, name="swiglu_bwd", grid=(lp // BLOCK,), in_specs=[_row_spec(f), _row_spec(f, 0), _row_spec(f, 1)],
        out_specs=_row_spec(2 * f), out_shape=jax.ShapeDtypeStruct((lp, 2 * f), BF16), compiler_params=_params(("parallel",)),
    )(dact, gu, gu)


ANY = pl.BlockSpec(memory_space=pl.ANY)


def _all_gather_rows(x, name, after=None):
    r, c = x.shape

    def body(x_ref, out_ref, send_sems, recv_sems, local_sem):
        mx, my, mc = lax.axis_index("x"), lax.axis_index("y"), lax.axis_index("c")
        me, sibling = (mx, my, mc), (mx, my, 1 - mc)
        chips = [(1 - mx, my), (mx, 1 - my), (1 - mx, 1 - my)]

        def rows(px, py, pc):
            return out_ref.at[pl.ds((4 * px + 2 * py + pc) * r, r), :]

        def copy(k, block, to, src=None):
            return pltpu.make_async_remote_copy(
                src_ref=rows(*block) if src is None else src, dst_ref=rows(*block),
                send_sem=send_sems.at[k], recv_sem=recv_sems.at[k], device_id=to, device_id_type=MESH)

        mine = pltpu.make_async_copy(x_ref, rows(*me), local_sem)
        mine.start()
        first = [copy(0, me, sibling, src=x_ref)]
        first += [copy(1 + j, me, (*chip, mc), src=x_ref) for j, chip in enumerate(chips)]
        for cp in first:
            cp.start()
        passed = [copy(4 + j, (*chip, mc), sibling) for j, chip in enumerate(chips)]
        for j, chip in enumerate(chips):
            copy(1 + j, (*chip, mc), me).wait_recv()
            passed[j].start()
        copy(0, sibling, me).wait_recv()
        for j, chip in enumerate(chips):
            copy(4 + j, (*chip, 1 - mc), me).wait_recv()
        for cp in first + passed:
            cp.wait_send()
        mine.wait()

    return _pcall(
        body, after=after, name=name, in_specs=[ANY], out_specs=ANY, out_shape=jax.ShapeDtypeStruct((N_DEV * r, c), x.dtype),
        scratch_shapes=[pltpu.SemaphoreType.DMA((7,)), pltpu.SemaphoreType.DMA((7,)), pltpu.SemaphoreType.DMA(())],
    )(x)


HBM = pl.BlockSpec(memory_space=pltpu.HBM)
SEM = pl.BlockSpec(memory_space=pltpu.SEMAPHORE)
IN_FLIGHT = pltpu.CompilerParams(has_side_effects=pltpu.SideEffectType.DATAFLOW_SIDE_EFFECTING)
N_PEERS = 4


def _place_rows(shard, after, name):
    r, c = shard.shape
    tr = _pick(r, max(16, ELEMENTWISE_BLOCK_BYTES // (4 * c)), 16)
    steps = r // tr
    dev = (4 * lax.axis_index("x") + 2 * lax.axis_index("y") + lax.axis_index("c")).astype(jnp.int32).reshape(1)

    def body(dev_ref, x_ref, after_ref, o_ref):
        o_ref[...] = x_ref[...].astype(BF16)

    return _pcall(
        body, name=name,
        grid_spec=pltpu.PrefetchScalarGridSpec(
            num_scalar_prefetch=1, grid=(steps,),
            in_specs=[pl.BlockSpec((tr, c), lambda i, dev_ref: (i, 0)), pl.BlockSpec(memory_space=pl.ANY)],
            out_specs=pl.BlockSpec((tr, c), lambda i, dev_ref: (dev_ref[0] * steps + i, 0))),
        out_shape=jax.ShapeDtypeStruct((N_DEV * r, c), BF16), compiler_params=_params(("parallel",)),
    )(dev, shard, after)


def _gather_start(full, name):
    r = full.shape[0] // N_DEV

    def body(full_ref, send_sems, recv_sems, out_ref):
        mx, my, mc = lax.axis_index("x"), lax.axis_index("y"), lax.axis_index("c")
        mine = full_ref.at[pl.ds((4 * mx + 2 * my + mc) * r, r), :]
        for k, peer in enumerate([(mx, my, 1 - mc), (1 - mx, my, mc), (mx, 1 - my, mc), (1 - mx, 1 - my, mc)]):
            pltpu.make_async_remote_copy(
                src_ref=mine, dst_ref=mine, send_sem=send_sems.at[k], recv_sem=recv_sems.at[k],
                device_id=peer, device_id_type=MESH).start()

    return pl.pallas_call(
        body, name=name, in_specs=[HBM], out_specs=(SEM, SEM, HBM),
        out_shape=(pltpu.SemaphoreType.DMA((N_PEERS,)), pltpu.SemaphoreType.DMA((N_PEERS,)), pltpu.HBM(full.shape, full.dtype)),
        input_output_aliases={0: 2}, compiler_params=IN_FLIGHT,
    )(pltpu.with_memory_space_constraint(full, pltpu.HBM))


def _gather_wait(full, send_sem, recv_sem, after, name):
    r = full.shape[0] // N_DEV

    def body(full_ref, send_ref, recv_ref, after_ref, out_ref):
        mx, my, mc = lax.axis_index("x"), lax.axis_index("y"), lax.axis_index("c")
        block = full_ref.at[pl.ds(0, r), :]
        for k in range(N_PEERS):
            cp = pltpu.make_async_remote_copy(
                src_ref=block, dst_ref=block, send_sem=send_ref.at[k], recv_sem=recv_ref.at[k],
                device_id=(mx, my, mc), device_id_type=MESH)
            cp.wait_send()
            cp.wait_recv()

    return pl.pallas_call(
        body, name=name, in_specs=[HBM, SEM, SEM, pl.BlockSpec(memory_space=pl.ANY)], out_specs=HBM,
        out_shape=pltpu.HBM(full.shape, full.dtype), input_output_aliases={0: 0}, compiler_params=IN_FLIGHT,
    )(full, send_sem, recv_sem, after)


def _gather_forward_start(full, name):
    r = full.shape[0] // N_DEV

    def body(full_ref, send_sems, recv_sems, out_ref):
        mx, my, mc = lax.axis_index("x"), lax.axis_index("y"), lax.axis_index("c")
        for k, (px, py) in enumerate([(1 - mx, my), (mx, 1 - my), (1 - mx, 1 - my)]):
            rows = full_ref.at[pl.ds((4 * px + 2 * py + mc) * r, r), :]
            pltpu.make_async_remote_copy(
                src_ref=rows, dst_ref=rows, send_sem=send_sems.at[k], recv_sem=recv_sems.at[k],
                device_id=(mx, my, 1 - mc), device_id_type=MESH).start()

    return pl.pallas_call(
        body, name=name, in_specs=[HBM], out_specs=(SEM, SEM, HBM),
        out_shape=(pltpu.SemaphoreType.DMA((3,)), pltpu.SemaphoreType.DMA((3,)), pltpu.HBM(full.shape, full.dtype)),
        input_output_aliases={0: 2}, compiler_params=IN_FLIGHT,
    )(full)


def _gather_forward_wait(full, send_sem, recv_sem, after, name):
    r = full.shape[0] // N_DEV

    def body(full_ref, send_ref, recv_ref, after_ref, out_ref):
        mx, my, mc = lax.axis_index("x"), lax.axis_index("y"), lax.axis_index("c")
        block = full_ref.at[pl.ds(0, r), :]
        for k in range(3):
            cp = pltpu.make_async_remote_copy(
                src_ref=block, dst_ref=block, send_sem=send_ref.at[k], recv_sem=recv_ref.at[k],
                device_id=(mx, my, mc), device_id_type=MESH)
            cp.wait_send()
            cp.wait_recv()

    return pl.pallas_call(
        body, name=name, in_specs=[HBM, SEM, SEM, pl.BlockSpec(memory_space=pl.ANY)], out_specs=HBM,
        out_shape=pltpu.HBM(full.shape, full.dtype), input_output_aliases={0: 0}, compiler_params=IN_FLIGHT,
    )(full, send_sem, recv_sem, after)


def _pair_exchange_start(g, name):
    r = g.shape[0] // N_DEV
    c = g.shape[1]
    land = (len(CHIPS), r, c)

    def body(g_ref, land_ref, send_sems, recv_sems, g_out, land_out):
        mx, my, mc = lax.axis_index("x"), lax.axis_index("y"), lax.axis_index("c")
        for j, (px, py) in enumerate(CHIPS):
            pltpu.make_async_remote_copy(
                src_ref=g_ref.at[pl.ds((4 * px + 2 * py + 1 - mc) * r, r), :], dst_ref=land_ref.at[j],
                send_sem=send_sems.at[j], recv_sem=recv_sems.at[j], device_id=(mx, my, 1 - mc), device_id_type=MESH).start()

    return pl.pallas_call(
        body, name=name, in_specs=[HBM, HBM], out_specs=(SEM, SEM, HBM, HBM),
        out_shape=(pltpu.SemaphoreType.DMA((4,)), pltpu.SemaphoreType.DMA((4,)), pltpu.HBM(g.shape, g.dtype), pltpu.HBM(land, g.dtype)),
        input_output_aliases={0: 2, 1: 3}, compiler_params=IN_FLIGHT,
    )(pltpu.with_memory_space_constraint(g, pltpu.HBM), pltpu.with_memory_space_constraint(lax.empty(land, g.dtype), pltpu.HBM))


def _pair_exchange_wait(send_sem, recv_sem, g, land, after, name):
    def body(g_ref, land_ref, send_ref, recv_ref, after_ref, g_out, land_out):
        mx, my, mc = lax.axis_index("x"), lax.axis_index("y"), lax.axis_index("c")
        for j in range(len(CHIPS)):
            cp = pltpu.make_async_remote_copy(
                src_ref=land_ref.at[0], dst_ref=land_ref.at[0], send_sem=send_ref.at[j], recv_sem=recv_ref.at[j],
                device_id=(mx, my, mc), device_id_type=MESH)
            cp.wait_send()
            cp.wait_recv()

    return pl.pallas_call(
        body, name=name, in_specs=[HBM, HBM, SEM, SEM, pl.BlockSpec(memory_space=pl.ANY)], out_specs=(HBM, HBM),
        out_shape=(pltpu.HBM(g.shape, g.dtype), pltpu.HBM(land.shape, land.dtype)), input_output_aliases={0: 0, 1: 1},
        compiler_params=IN_FLIGHT,
    )(g, land, send_sem, recv_sem, after)


def _chip_exchange_start(ps, after, name):
    def body(ps_ref, rx_ref, after_ref, send_sems, recv_sems, ps_out, rx_out):
        mx, my, mc = lax.axis_index("x"), lax.axis_index("y"), lax.axis_index("c")
        chips = [(1 - mx, my), (mx, 1 - my), (1 - mx, 1 - my)]
        for k, (px, py) in enumerate(chips):
            pltpu.make_async_remote_copy(
                src_ref=ps_ref.at[2 * px + py], dst_ref=rx_ref.at[2 * mx + my], send_sem=send_sems.at[k], recv_sem=recv_sems.at[k],
                device_id=(px, py, mc), device_id_type=MESH).start()

    return pl.pallas_call(
        body, name=name, in_specs=[HBM, HBM, pl.BlockSpec(memory_space=pl.ANY)], out_specs=(SEM, SEM, HBM, HBM),
        out_shape=(pltpu.SemaphoreType.DMA((3,)), pltpu.SemaphoreType.DMA((3,)), pltpu.HBM(ps.shape, ps.dtype), pltpu.HBM(ps.shape, ps.dtype)),
        input_output_aliases={0: 2, 1: 3}, compiler_params=IN_FLIGHT,
    )(pltpu.with_memory_space_constraint(ps, pltpu.HBM), pltpu.with_memory_space_constraint(lax.empty(ps.shape, ps.dtype), pltpu.HBM), after)


def _chip_exchange_wait(send_sem, recv_sem, ps, rx, after, name):
    def body(ps_ref, rx_ref, send_ref, recv_ref, after_ref, ps_out, rx_out):
        mx, my, mc = lax.axis_index("x"), lax.axis_index("y"), lax.axis_index("c")
        for k in range(3):
            cp = pltpu.make_async_remote_copy(
                src_ref=ps_ref.at[0], dst_ref=rx_ref.at[0], send_sem=send_ref.at[k], recv_sem=recv_ref.at[k],
                device_id=(mx, my, mc), device_id_type=MESH)
            cp.wait_send()
            cp.wait_recv()

    return pl.pallas_call(
        body, name=name, in_specs=[HBM, HBM, SEM, SEM, pl.BlockSpec(memory_space=pl.ANY)], out_specs=(HBM, HBM),
        out_shape=(pltpu.HBM(ps.shape, ps.dtype), pltpu.HBM(rx.shape, rx.dtype)), input_output_aliases={0: 0, 1: 1},
        compiler_params=IN_FLIGHT,
    )(ps, rx, send_sem, recv_sem, after)


def _sum_chips(ps, rx, name):
    n, r, c = rx.shape
    tr = _pick(r, max(16, 4 * ELEMENTWISE_BLOCK_BYTES // (4 * n * c)), 16)
    chip =(2 * lax.axis_index("x") + lax.axis_index("y")).astype(jnp.int32).reshape(1)

    def body(chip_ref, own_ref, x_ref, o_ref):
        me = chip_ref[0]
        own = own_ref[0].astype(F32)
        acc = jnp.where(me == 0, own, x_ref[0].astype(F32))
        for j in range(1, n):
            acc = acc + jnp.where(me == j, own, x_ref[j].astype(F32))
        o_ref[...] = acc

    return _pcall(
        body, name=name,
        grid_spec=pltpu.PrefetchScalarGridSpec(
            num_scalar_prefetch=1, grid=(r // tr,),
            in_specs=[pl.BlockSpec((1, tr, c), lambda i, chip_ref: (chip_ref[0], i, 0)), pl.BlockSpec((n, tr, c), lambda i, chip_ref: (0, i, 0))],
            out_specs=pl.BlockSpec((tr, c), lambda i, chip_ref: (i, 0))),
        out_shape=jax.ShapeDtypeStruct((r, c), F32), compiler_params=_params(("parallel",)),
    )(chip, ps, rx)


def _pair_exchange(g, name):
    r = g.shape[0] // N_DEV
    c = g.shape[1]

    def body(g_ref, theirs_ref, send_sems, recv_sems):
        mx, my, mc = lax.axis_index("x"), lax.axis_index("y"), lax.axis_index("c")
        sibling = (mx, my, 1 - mc)
        copies = []
        for j, (px, py) in enumerate(CHIPS):
            give = g_ref.at[pl.ds((4 * px + 2 * py + 1 - mc) * r, r), :]
            rc = pltpu.make_async_remote_copy(
                src_ref=give, dst_ref=theirs_ref.at[j], send_sem=send_sems.at[j], recv_sem=recv_sems.at[j],
                device_id=sibling, device_id_type=MESH)
            rc.start()
            copies.append(rc)
        for cp in copies:
            cp.wait()

    return _pcall(
        body, name=name, in_specs=[ANY], out_specs=ANY, out_shape=jax.ShapeDtypeStruct((len(CHIPS), r, c), g.dtype),
        scratch_shapes=[pltpu.SemaphoreType.DMA((4,)), pltpu.SemaphoreType.DMA((4,))],
    )(g)


def _pair_sum(g, theirs, name):
    nch, r, c = theirs.shape
    tr = _pick(r, max(16, 3 * ELEMENTWISE_BLOCK_BYTES // (2 * c)), 16)
    core = lax.axis_index("c").astype(jnp.int32).reshape(1)

    def body(core_ref, a_ref, b_ref, o_ref):
        o_ref[...] = (a_ref[...].astype(F32) + b_ref[...].astype(F32)).astype(o_ref.dtype)

    spec = pl.BlockSpec((1, tr, c), lambda j, i, core_ref: (j, i, 0))
    own = pl.BlockSpec((1, tr, c), lambda j, i, core_ref: (2 * j + core_ref[0], i, 0))
    return _pcall(
        body, name=name,
        grid_spec=pltpu.PrefetchScalarGridSpec(num_scalar_prefetch=1, grid=(nch, r // tr), in_specs=[own, spec], out_specs=spec),
        out_shape=jax.ShapeDtypeStruct(theirs.shape, theirs.dtype), compiler_params=_params(("parallel", "parallel")),
    )(core, g.reshape(N_DEV, r, c), theirs)


def _sum_blocks(rx, name):
    n, r, c = rx.shape
    tr = _pick(r, max(8, ELEMENTWISE_BLOCK_BYTES // (4 * n * c)), 8)

    def body(x_ref, o_ref):
        acc = x_ref[0].astype(F32)
        for j in range(1, n):
            acc = acc + x_ref[j].astype(F32)
        o_ref[...] = acc

    return _pcall(
        body, name=name, grid=(r // tr,), in_specs=[pl.BlockSpec((n, tr, c), lambda i: (0, i, 0))],
        out_specs=pl.BlockSpec((tr, c), lambda i: (i, 0)), out_shape=jax.ShapeDtypeStruct((r, c), F32),
        compiler_params=_params(("parallel",)),
    )(rx)


def _adamw(w, g, m, v, name):
    r, c = w.shape
    tr = _pick(r, max(8, ELEMENTWISE_BLOCK_BYTES // (4 * c)), 8)
    c1 = 1.0 - ADAM_B1 ** ADAM_STEP
    c2 = 1.0 - ADAM_B2 ** ADAM_STEP

    def body(w_ref, g_ref, m_ref, v_ref, d_ref, nm_ref, nv_ref):
        gg = g_ref[...]
        nm = ADAM_B1 * m_ref[...] + (1.0 - ADAM_B1) * gg
        nv = ADAM_B2 * v_ref[...] + (1.0 - ADAM_B2) * (gg * gg)
        d_ref[...] = -ADAM_LR * ((nm / c1) / (jnp.sqrt(nv / c2) + ADAM_EPS) + ADAM_WD * w_ref[...])
        nm_ref[...] = nm
        nv_ref[...] = nv

    spec = pl.BlockSpec((tr, c), lambda i: (i, 0))
    shp = jax.ShapeDtypeStruct((r, c), F32)
    return _pcall(
        body, name=name, grid=(r // tr,), in_specs=[spec] * 4, out_specs=[spec] * 3, out_shape=[shp] * 3,
        compiler_params=_params(("parallel",)),
    )(w, g, m, v)


def _pack(parts):
    flat, layout, row = [], [], 0
    for p in parts:
        n = p.size
        rows = -(-n // LANES)
        flat.append(jnp.pad(p.reshape(-1).astype(F32), (0, rows * LANES - n)))
        layout.append((row, n, p.shape))
        row += rows
    total = -(-row // 8) * 8
    if total > row:
        flat.append(jnp.zeros(((total - row) * LANES,), F32))
    return jnp.concatenate(flat).reshape(total, LANES), layout


def _unpack(slab, layout):
    flat = slab.reshape(-1)
    return [flat[row * LANES:row * LANES + n].reshape(shape) for row, n, shape in layout]


def kernel(x, meta_tokens, mix_norm_g, w_in, b_in, attn_sinks, conv_w, conv_b, conv_ln_g, conv_ln_b, w_attn_o, w_conv_o, b_conv_o, w_out, ffn_norm_g, w_gate_up, w_down, final_norm_g, loss_target, m_meta_tokens, m_mix_norm_g, m_w_in, m_b_in, m_attn_sinks, m_conv_w, m_conv_b, m_conv_ln_g, m_conv_ln_b, m_w_attn_o, m_w_conv_o, m_b_conv_o, m_w_out, m_ffn_norm_g, m_w_gate_up, m_w_down, m_final_norm_g, v_meta_tokens, v_mix_norm_g, v_w_in, v_b_in, v_attn_sinks, v_conv_w, v_conv_b, v_conv_ln_g, v_conv_ln_b, v_w_attn_o, v_w_conv_o, v_b_conv_o, v_w_out, v_ffn_norm_g, v_w_gate_up, v_w_down, v_final_norm_g):
    xs = x[0]
    tgt = loss_target[0]
    s, d = xs.shape
    lp = s + BLOCK
    cd = conv_b.shape[1]
    ffn = w_down.shape[1] * N_DEV
    dev = 4 * lax.axis_index("x") + 2 * lax.axis_index("y") + lax.axis_index("c")
    cw_cols = conv_w.shape[3]
    meta_cols = meta_tokens.shape[1]

    small, small_layout = _pack([meta_tokens, jnp.pad(conv_w[0, :, 0, :], ((0, CONV_ROWS - CONV_WIDTH), (0, 0)))])
    small_flat = _all_gather_rows(small, "gather_small")
    small_all = small_flat.reshape(N_DEV, *small.shape)
    meta_parts, cw_parts = zip(*[_unpack(small_all[j], small_layout) for j in range(N_DEV)])
    meta_full = jnp.concatenate(meta_parts, axis=1)
    conv_w_full = jnp.concatenate(cw_parts, axis=1)
    g_send, g_recv, g_full = [], [], []
    tok = small_flat
    for shard, name in ((w_in[0].T, "w_in"), (w_attn_o[0].T, "w_attn_o"), (w_conv_o[0].T, "w_conv_o"), (w_out[0], "w_out"),
                        (w_gate_up[0].T, "w_gate_up"), (w_down[0], "w_down")):
        send_sem, recv_sem, tok = _gather_start(_place_rows(shard, tok, "place_" + name), "gather_start_" + name)
        g_send.append(send_sem)
        g_recv.append(recv_sem)
        g_full.append(tok)

    def arrived(w, after, name):
        full = _gather_wait(g_full[w], g_send[w], g_recv[w], after, "gather_wait_" + name)
        return _gather_forward_start(full, "gather_forward_start_" + name)

    def whole(passing, after, name):
        return _gather_forward_wait(passing[2], passing[0], passing[1], after, "gather_forward_wait_" + name)

    ctab, stab = _rope_tables(lp)
    mm = functools.partial(_matmul, tm=1056, tn=1024)

    passing = arrived(0, tok, "w_in")
    h0, u = _prep(xs, meta_full, mix_norm_g, after=passing[2])
    win_t = whole(passing, u, "w_in")
    bq, bkv, bc, bg = b_in[:, :Q_DIM], b_in[:, Q_DIM:Q_DIM + 2 * KV_DIM], b_in[:, Q_DIM + 2 * KV_DIM:Q_DIM + 2 * KV_DIM + 2 * cd], b_in[:, Q_DIM + 2 * KV_DIM + 2 * cd:]
    o_kv, o_c, o_g = Q_DIM, Q_DIM + 2 * KV_DIM, Q_DIM + 2 * KV_DIM + 2 * cd
    in_proj = functools.partial(_matmul, u, win_t, mode="nt", out_dtype=BF16, tm=2112, tn=512, tk=d)
    zq = in_proj(name="in_proj_q", bias=bq, b_row_off=0, b_rows=Q_DIM)
    zkv = in_proj(name="in_proj_kv", bias=bkv, b_row_off=o_kv, b_rows=2 * KV_DIM)
    zc = in_proj(name="in_proj_conv", bias=bc, b_row_off=o_c, b_rows=2 * cd)
    zg = in_proj(name="in_proj_gates", bias=bg, b_row_off=o_g, b_rows=2 * d)
    passing = arrived(1, zg, "w_attn_o")
    q_rot, k_sh, v_sh = _rope_fwd(zq, zkv, ctab, stab, after=passing[2])
    o = _attn_fwd(q_rot, k_sh, v_sh, attn_sinks)
    wao_t = whole(passing, o, "w_attn_o")
    br_a = mm(o, wao_t, mode="nt", name="attn_out_proj", out_dtype=BF16, tk=Q_DIM)
    passing = arrived(2, br_a, "w_conv_o")
    conv_out, c2 = _conv_fwd(zc, conv_w_full, conv_b, conv_ln_g, conv_ln_b, after=passing[2])
    wco_t = whole(passing, c2, "w_conv_o")
    br_b = mm(c2, wco_t, mode="nt", name="conv_out_proj", out_dtype=BF16, tk=cd, bias=b_conv_o)
    passing = arrived(3, br_b, "w_out")
    merged = _gate_fwd(br_a, br_b, zg, after=passing[2])
    wout = whole(passing, merged, "w_out")
    passing = arrived(4, wout, "w_gate_up")
    h1 = mm(merged, wout, mode="nn", name="mix_out_proj", out_dtype=F32, tn=512, tk=d, residual=h0, after=passing[2])
    u2 = _rmsnorm_fwd(h1, ffn_norm_g, "ffn_rmsnorm")
    wgu_t = whole(passing, u2, "w_gate_up")
    gu = _matmul(u2, wgu_t, mode="nt", name="ffn_gate_up", out_dtype=BF16, tm=2112, tn=512, tk=d)
    passing = arrived(5, gu, "w_down")
    act = _swiglu_fwd(gu, after=passing[2])
    wdown = whole(passing, act, "w_down")
    h2 = mm(act, wdown, mode="nn", name="ffn_down", out_dtype=F32, tn=512, tk=ffn // 2, residual=h1)
    dh2, dh2_b, loss_part, d_final_g = _final(h2, tgt, final_norm_g.reshape(1, d))

    wgrad = functools.partial(_matmul, mode="tn", out_dtype=BF16, tk=lp, tn=2048, b_inner=False)
    in_flight = {}

    def scatter_begin(g, name):
        return _pair_exchange_start(g, "rs_" + name + "_pair_start")

    def scatter_go_on(pair, after, name):
        g, theirs = _pair_exchange_wait(pair[0], pair[1], pair[2], pair[3], after, "rs_" + name + "_pair_wait")
        ps = _pair_sum(g, theirs, "rs_" + name + "_pair_sum")
        in_flight[name] = _chip_exchange_start(ps, theirs, "rs_" + name + "_chip_start")
        return in_flight[name][2]

    g_wdown = wgrad(act, dh2_b, name="ffn_down_dw", tm=256)
    pair = scatter_begin(g_wdown, "w_down")
    dact = _matmul(dh2_b, wdown, mode="nt", name="ffn_down_dx", out_dtype=BF16, tm=2112, tn=256, tk=d, after=pair[2])
    tok = scatter_go_on(pair, dact, "w_down")
    dgu = _swiglu_bwd(dact, gu)
    g_wgu_t = wgrad(dgu, u2, name="ffn_gate_up_dw", tm=512, after=tok)
    pair = scatter_begin(g_wgu_t, "w_gate_up")
    du2 = mm(dgu, wgu_t, mode="nn", name="ffn_gate_up_dx", out_dtype=F32, tn=512, tk=ffn // 2, after=pair[2])
    tok = scatter_go_on(pair, du2, "w_gate_up")
    dh1, dh1_b, d_ffn_g = _rmsnorm_bwd(du2, h1, ffn_norm_g, dh2, "ffn_rmsnorm_bwd")
    g_wout = wgrad(merged, dh1_b, name="mix_out_dw", tm=512, after=tok)
    pair = scatter_begin(g_wout, "w_out")
    dmerged = mm(dh1_b, wout, mode="nt", name="mix_out_dx", out_dtype=BF16, tk=d, after=pair[2])
    tok = scatter_go_on(pair, dmerged, "w_out")
    d_a, d_b, dz_g, sum_g, d_bco = _gate_bwd(dmerged, br_a, br_b, zg)
    g_wao_t = wgrad(d_a, o, name="attn_out_dw", tm=512, after=tok)
    pair = scatter_begin(g_wao_t, "w_attn_o")
    do = mm(d_a, wao_t, mode="nn", name="attn_out_dx", out_dtype=BF16, tk=d, after=pair[2])
    tok = scatter_go_on(pair, do, "w_attn_o")
    g_wco_t = wgrad(d_b, c2, name="conv_out_dw", tm=512, after=tok)
    pair = scatter_begin(g_wco_t, "w_conv_o")
    dc2 = mm(d_b, wco_t, mode="nn", name="conv_out_dx", out_dtype=F32, tk=d, after=pair[2])
    tok = scatter_go_on(pair, dc2, "w_conv_o")
    dq, dk, dv, dkm, dvm, d_sinks = _attn_bwd(q_rot, k_sh, v_sh, attn_sinks, do)
    dz_qkv, sum_qkv = _rope_bwd(dq, dk, dv, dkm, dvm, ctab, stab)
    dco, d_ln_g, d_ln_b, d_conv_b = _conv_bwd_norm(dc2, conv_out, conv_ln_g, conv_ln_b)
    dz_c, sum_c, d_conv_w = _conv_bwd_taps(dco, zc, conv_w_full)
    dz = jnp.concatenate([dz_qkv, dz_c, dz_g], axis=1)
    d_b_in = jnp.concatenate([sum_qkv, sum_c, sum_g], axis=1)
    in_dim = dz.shape[1]
    g_win_t = wgrad(dz, u, name="in_proj_dw", tm=512, after=tok)
    theirs = _pair_exchange(g_win_t, "rs_w_in_pair_exchange")
    in_flight["w_in"] = _chip_exchange_start(_pair_sum(g_win_t, theirs, "rs_w_in_pair_sum"), theirs, "rs_w_in_chip_start")
    du = mm(dz, win_t, mode="nn", name="in_proj_dx", out_dtype=F32, tk=in_dim // 4, after=in_flight["w_in"][2])
    grad_x, d_meta, d_mix_g = _rmsnorm_bwd_first(du, h0, mix_norm_g, dh1)

    weights = dict(meta_tokens=meta_tokens, mix_norm_g=mix_norm_g, w_in=w_in, b_in=b_in, attn_sinks=attn_sinks, conv_w=conv_w,
                   conv_b=conv_b, conv_ln_g=conv_ln_g, conv_ln_b=conv_ln_b, w_attn_o=w_attn_o, w_conv_o=w_conv_o, b_conv_o=b_conv_o,
                   w_out=w_out, ffn_norm_g=ffn_norm_g, w_gate_up=w_gate_up, w_down=w_down, final_norm_g=final_norm_g)
    m_in = dict(meta_tokens=m_meta_tokens, mix_norm_g=m_mix_norm_g, w_in=m_w_in, b_in=m_b_in, attn_sinks=m_attn_sinks, conv_w=m_conv_w,
                conv_b=m_conv_b, conv_ln_g=m_conv_ln_g, conv_ln_b=m_conv_ln_b, w_attn_o=m_w_attn_o, w_conv_o=m_w_conv_o,
                b_conv_o=m_b_conv_o, w_out=m_w_out, ffn_norm_g=m_ffn_norm_g, w_gate_up=m_w_gate_up, w_down=m_w_down,
                final_norm_g=m_final_norm_g)
    v_in = dict(meta_tokens=v_meta_tokens, mix_norm_g=v_mix_norm_g, w_in=v_w_in, b_in=v_b_in, attn_sinks=v_attn_sinks, conv_w=v_conv_w,
                conv_b=v_conv_b, conv_ln_g=v_conv_ln_g, conv_ln_b=v_conv_ln_b, w_attn_o=v_w_attn_o, w_conv_o=v_w_conv_o,
                b_conv_o=v_b_conv_o, w_out=v_w_out, ffn_norm_g=v_ffn_norm_g, w_gate_up=v_w_gate_up, w_down=v_w_down,
                final_norm_g=v_final_norm_g)
    names = list(weights)
    grads, delta, new_m, new_v = {}, {}, {}, {}
    transposed = ("w_in", "w_attn_o", "w_conv_o", "w_gate_up")
    tok = grad_x
    for n in ("w_down", "w_gate_up", "w_out", "w_attn_o", "w_conv_o", "w_in"):
        send_sem, recv_sem, ps, rx = in_flight[n]
        ps, rx = _chip_exchange_wait(send_sem, recv_sem, ps, rx, tok, "rs_" + n + "_chip_wait")
        g = _sum_chips(ps, rx, "rs_" + n + "_sum")
        g = g.T if n in transposed else g
        shape = weights[n].shape
        dl, nm, nv = _adamw(weights[n].reshape(g.shape), g, m_in[n].reshape(g.shape), v_in[n].reshape(g.shape), "adamw_" + n)
        grads[n], delta[n], new_m[n], new_v[n] = g.reshape(shape), dl.reshape(shape), nm.reshape(shape), nv.reshape(shape)
        tok = dl

    slab, slab_layout = _pack([loss_part[:, :1], d_mix_g, d_b_in, d_sinks[:, :N_Q_HEADS], d_conv_b, d_ln_g, d_ln_b, d_bco,
                               d_ffn_g, d_final_g, d_conv_w, d_meta])
    slab_all = _all_gather_rows(slab, "gather_small_grads", after=tok).reshape(N_DEV, *slab.shape)
    (loss, g_mix_g, g_b_in, g_sinks, g_conv_b, g_ln_g, g_ln_b, g_bco, g_ffn_g, g_final_g, g_conv_w_full, g_meta_full
     ) = _unpack(_sum_blocks(slab_all, "sum_small_grads"), slab_layout)
    g_conv_w = lax.dynamic_slice(g_conv_w_full, (0, dev * cw_cols), (CONV_WIDTH, cw_cols)).reshape(conv_w.shape)
    g_meta = lax.dynamic_slice(g_meta_full, (0, dev * meta_cols), (N_META, meta_cols))
    g_final_g = g_final_g.reshape(final_norm_g.shape)
    grads.update(meta_tokens=g_meta, mix_norm_g=g_mix_g, b_in=g_b_in, attn_sinks=g_sinks, conv_w=g_conv_w, conv_b=g_conv_b,
                 conv_ln_g=g_ln_g, conv_ln_b=g_ln_b, b_conv_o=g_bco, ffn_norm_g=g_ffn_g, final_norm_g=g_final_g)
    rest = [n for n in names if n not in delta]
    w_slab, rest_layout = _pack([weights[n] for n in rest])
    g_slab, _ = _pack([grads[n] for n in rest])
    m_slab, _ = _pack([m_in[n] for n in rest])
    v_slab, _ = _pack([v_in[n] for n in rest])
    dl, nm, nv = _adamw(w_slab, g_slab, m_slab, v_slab, "adamw_small")
    for n, a, b, c in zip(rest, _unpack(dl, rest_layout), _unpack(nm, rest_layout), _unpack(nv, rest_layout)):
        delta[n], new_m[n], new_v[n] = a, b, c

    return (loss.reshape(()), grad_x[None], *[grads[n] for n in names], *[delta[n] for n in names],
            *[new_m[n] for n in names], *[new_v[n] for n in names])
```

```python
import functools
import math

import jax
import jax.numpy as jnp
from jax import lax
from jax.experimental import pallas as pl
from jax.experimental.pallas import tpu as pltpu

F32 = jnp.float32
BF16 = jnp.bfloat16

N_DEV = 8
BLOCK = 128
N_META = 16
PAD_ROWS = BLOCK - N_META
HEAD_DIM = 64
N_Q_HEADS = 32
N_KV_HEADS = 4
GROUP = N_Q_HEADS // N_KV_HEADS
Q_DIM = N_Q_HEADS * HEAD_DIM
KV_DIM = N_KV_HEADS * HEAD_DIM
WINDOW = 128
CONV_WIDTH = 31
CONV_ROWS = 32
ROPE_THETA = 10000.0
EPS = 1e-6
ATTN_SCALE = HEAD_DIM ** -0.5
NEG = -1e30

ADAM_LR = 0.001
ADAM_B1 = 0.9
ADAM_B2 = 0.999
ADAM_EPS = 1e-08
ADAM_WD = 0.01
ADAM_STEP = 10

VMEM_LIMIT_BYTES = 56 * 1024 * 1024
LANES = 128
ELEMENTWISE_BLOCK_BYTES = 2 * 1024 * 1024
MESH = pl.DeviceIdType.MESH
CHIPS = ((0, 0), (0, 1), (1, 0), (1, 1))


def _pcall(body, after=None, **kw):
    if after is None:
        return pl.pallas_call(body, **kw)
    in_specs = list(kw.pop("in_specs"))
    n_in = len(in_specs)

    def ordered_body(*refs):
        return body(*refs[:n_in], *refs[n_in + 1:])

    call = pl.pallas_call(ordered_body, in_specs=in_specs + [pl.BlockSpec(memory_space=pl.ANY)], **kw)
    return lambda *args: call(*args, after)


def _params(semantics=None):
    if semantics is None:
        return pltpu.CompilerParams(vmem_limit_bytes=VMEM_LIMIT_BYTES)
    return pltpu.CompilerParams(dimension_semantics=semantics, vmem_limit_bytes=VMEM_LIMIT_BYTES)


def _pick(dim, pref, align):
    best = None
    t = align
    while t <= min(dim, pref):
        if dim % t == 0:
            best = t
        t += align
    return dim if best is None else best


def _sigmoid(x):
    return 1.0 / (1.0 + jnp.exp(-x))


def _matmul(a, b, *, mode, name, out_dtype, tm, tn, tk, bias=None, residual=None, b_inner=True,
            b_row_off=0, b_rows=None, after=None):
    if mode == "nn":
        m, k = a.shape
        n = b.shape[1]
    elif mode == "nt":
        m, k = a.shape
        n = b.shape[0] if b_rows is None else b_rows
    else:
        k, m = a.shape
        n = b.shape[1]
    tm = _pick(m, tm, 16)
    tn = _pick(math.gcd(n, b_row_off) if mode == "nt" and b_row_off else n, tn, LANES)
    tk = _pick(k, tk, LANES if mode != "tn" else 16)
    nm, nn, nk = m // tm, n // tn, k // tk
    if mode == "nt":
        assert b_row_off % tn == 0
    off = b_row_off // tn if mode == "nt" else 0

    if b_inner:
        grid = (nm, nn, nk)
        ij = lambda g0, g1: (g0, g1)
    else:
        grid = (nn, nm, nk)
        ij = lambda g0, g1: (g1, g0)

    if mode == "tn":
        a_spec = pl.BlockSpec((tk, tm), lambda g0, g1, kk: (kk, ij(g0, g1)[0]))
    else:
        a_spec = pl.BlockSpec((tm, tk), lambda g0, g1, kk: (ij(g0, g1)[0], kk))
    if mode == "nt":
        b_spec = pl.BlockSpec((tn, tk), lambda g0, g1, kk: (ij(g0, g1)[1] + off, kk))
    else:
        b_spec = pl.BlockSpec((tk, tn), lambda g0, g1, kk: (kk, ij(g0, g1)[1]))
    o_spec = pl.BlockSpec((tm, tn), lambda g0, g1, kk: ij(g0, g1))
    in_specs = [a_spec, b_spec]
    args = [a, b]
    if bias is not None:
        in_specs.append(pl.BlockSpec((1, tn), lambda g0, g1, kk: (0, ij(g0, g1)[1])))
        args.append(bias)
    if residual is not None:
        in_specs.append(o_spec)
        args.append(residual)
    dims = {"nn": (((1,), (0,)), ((), ())), "nt": (((1,), (1,)), ((), ())), "tn": (((0,), (0,)), ((), ()))}[mode]
    has_bias, has_res = bias is not None, residual is not None

    def body(*refs):
        a_ref, b_ref = refs[0], refs[1]
        pos = 2
        bias_ref = res_ref = None
        if has_bias:
            bias_ref = refs[pos]
            pos += 1
        if has_res:
            res_ref = refs[pos]
            pos += 1
        o_ref = refs[pos]
        acc_ref = refs[pos + 1] if nk > 1 else None

        def finish(acc):
            if has_bias:
                acc = acc + bias_ref[...]
            if has_res:
                acc = acc + res_ref[...]
            o_ref[...] = acc.astype(out_dtype)

        p = lax.dot_general(a_ref[...], b_ref[...], dims, preferred_element_type=F32)
        if nk == 1:
            finish(p)
        else:
            kk = pl.program_id(2)

            @pl.when(kk == 0)
            def _():
                acc_ref[...] = p

            @pl.when(kk > 0)
            def _():
                acc_ref[...] += p

            @pl.when(kk == nk - 1)
            def _():
                finish(acc_ref[...])

    return _pcall(
        body, after=after, name=name, grid=grid, in_specs=in_specs, out_specs=o_spec,
        out_shape=jax.ShapeDtypeStruct((m, n), out_dtype),
        scratch_shapes=[pltpu.VMEM((tm, tn), F32)] if nk > 1 else [],
        compiler_params=_params(("parallel", "parallel", "arbitrary")),
    )(*args)


def _row_spec(width, col=0):
    return pl.BlockSpec((BLOCK, width), lambda i: (i, col))


def _const_spec(shape):
    nd = len(shape)
    return pl.BlockSpec(shape, lambda i: (0,) * nd)


def _prep(x, meta_full, g, after=None):
    s, d = x.shape
    lp = s + BLOCK
    nb = lp // BLOCK

    def body(x_ref, meta_ref, g_ref, h_ref, u_ref):
        i = pl.program_id(0)

        @pl.when(i == 0)
        def _():
            h_ref[0:PAD_ROWS, :] = jnp.zeros((PAD_ROWS, d), F32)
            h_ref[PAD_ROWS:BLOCK, :] = meta_ref[...]

        @pl.when(i > 0)
        def _():
            h_ref[...] = x_ref[...]

        h = h_ref[...]
        r = lax.rsqrt(jnp.mean(h * h, axis=-1, keepdims=True) + EPS)
        u_ref[...] = (h * r * g_ref[...]).astype(BF16)

    return _pcall(
        body, after=after, name="prep_rmsnorm", grid=(nb,),
        in_specs=[pl.BlockSpec((BLOCK, d), lambda i: (jnp.maximum(i - 1, 0), 0)), _const_spec((N_META, d)), _const_spec((1, d))],
        out_specs=[_row_spec(d), _row_spec(d)],
        out_shape=[jax.ShapeDtypeStruct((lp, d), F32), jax.ShapeDtypeStruct((lp, d), BF16)],
        compiler_params=_params(("arbitrary",)),
    )(x, meta_full, g)


def _rmsnorm_fwd(h, g, name):
    lp, d = h.shape

    def body(h_ref, g_ref, u_ref):
        x = h_ref[...]
        r = lax.rsqrt(jnp.mean(x * x, axis=-1, keepdims=True) + EPS)
        u_ref[...] = (x * r * g_ref[...]).astype(BF16)

    return _pcall(
        body, name=name, grid=(lp // BLOCK,), in_specs=[_row_spec(d), _const_spec((1, d))], out_specs=_row_spec(d),
        out_shape=jax.ShapeDtypeStruct((lp, d), BF16), compiler_params=_params(("parallel",)),
    )(h, g)


def _rms_bwd_core(dy, x, g):
    r = lax.rsqrt(jnp.mean(x * x, axis=-1, keepdims=True) + EPS)
    xhat = x * r
    dxhat = dy * g
    dx = r * (dxhat - xhat * jnp.mean(dxhat * xhat, axis=-1, keepdims=True))
    return dx, jnp.sum(dy * xhat, axis=0, keepdims=True)


def _rmsnorm_bwd(dy, h, g, dres, name):
    lp, d = h.shape

    def body(dy_ref, h_ref, g_ref, dres_ref, dh_ref, dhb_ref, dg_ref):
        i = pl.program_id(0)
        dx, dg = _rms_bwd_core(dy_ref[...], h_ref[...], g_ref[...])
        dh = dres_ref[...] + dx
        dh_ref[...] = dh
        dhb_ref[...] = dh.astype(BF16)

        @pl.when(i == 0)
        def _():
            dg_ref[...] = jnp.zeros_like(dg_ref)

        dg_ref[...] += dg

    return _pcall(
        body, name=name, grid=(lp // BLOCK,),
        in_specs=[_row_spec(d), _row_spec(d), _const_spec((1, d)), _row_spec(d)],
        out_specs=[_row_spec(d), _row_spec(d), _const_spec((1, d))],
        out_shape=[jax.ShapeDtypeStruct((lp, d), F32), jax.ShapeDtypeStruct((lp, d), BF16), jax.ShapeDtypeStruct((1, d), F32)],
        compiler_params=_params(("arbitrary",)),
    )(dy, h, g, dres)


def _rmsnorm_bwd_first(dy, h, g, dres):
    lp, d = h.shape
    s = lp - BLOCK

    def body(dy_ref, h_ref, g_ref, dres_ref, gx_ref, dmeta_ref, dg_ref):
        i = pl.program_id(0)
        dx, dg = _rms_bwd_core(dy_ref[...], h_ref[...], g_ref[...])
        dh = dres_ref[...] + dx
        gx_ref[...] = dh

        @pl.when(i == 0)
        def _():
            dmeta_ref[...] = dh[PAD_ROWS:BLOCK, :]
            dg_ref[...] = jnp.zeros_like(dg_ref)

        dg_ref[...] += dg

    return _pcall(
        body, name="rmsnorm_bwd_first", grid=(lp // BLOCK,),
        in_specs=[_row_spec(d), _row_spec(d), _const_spec((1, d)), _row_spec(d)],
        out_specs=[pl.BlockSpec((BLOCK, d), lambda i: (jnp.maximum(i - 1, 0), 0)), _const_spec((N_META, d)), _const_spec((1, d))],
        out_shape=[jax.ShapeDtypeStruct((s, d), F32), jax.ShapeDtypeStruct((N_META, d), F32), jax.ShapeDtypeStruct((1, d), F32)],
        compiler_params=_params(("arbitrary",)),
    )(dy, h, g, dres)


def _final(h2, tgt, g):
    lp, d = h2.shape

    def body(h_ref, t_ref, g_ref, dh_ref, dhb_ref, loss_ref, dg_ref):
        i = pl.program_id(0)
        x = h_ref[...]
        gg = g_ref[...]
        r = lax.rsqrt(jnp.mean(x * x, axis=-1, keepdims=True) + EPS)
        xhat = x * r
        y = xhat * gg
        live = (i > 0).astype(F32)
        err = (y - t_ref[...]) * live
        dy = err * (1.0 / d)
        dxhat = dy * gg
        dh = r * (dxhat - xhat * jnp.mean(dxhat * xhat, axis=-1, keepdims=True))
        dh_ref[...] = dh
        dhb_ref[...] = dh.astype(BF16)

        @pl.when(i == 0)
        def _():
            loss_ref[...] = jnp.zeros_like(loss_ref)
            dg_ref[...] = jnp.zeros_like(dg_ref)

        row_loss = jnp.mean(err * err, axis=-1, keepdims=True)
        loss_ref[...] += 0.5 * jnp.sum(row_loss, axis=0, keepdims=True)
        dg_ref[...] += jnp.sum(dy * xhat, axis=0, keepdims=True)

    return _pcall(
        body, name="final_norm_loss", grid=(lp // BLOCK,),
        in_specs=[_row_spec(d), pl.BlockSpec((BLOCK, d), lambda i: (jnp.maximum(i - 1, 0), 0)), _const_spec((1, d))],
        out_specs=[_row_spec(d), _row_spec(d), _const_spec((1, LANES)), _const_spec((1, d))],
        out_shape=[jax.ShapeDtypeStruct((lp, d), F32), jax.ShapeDtypeStruct((lp, d), BF16),
                   jax.ShapeDtypeStruct((1, LANES), F32), jax.ShapeDtypeStruct((1, d), F32)],
        compiler_params=_params(("arbitrary",)),
    )(h2, tgt, g)


def _swap_halves(x):
    w = x.shape[1]
    lane = lax.broadcasted_iota(jnp.int32, x.shape, 1)
    first = (lane & (HEAD_DIM - 1)) < (HEAD_DIM // 2)
    return jnp.where(first, pltpu.roll(x, w - HEAD_DIM // 2, 1), pltpu.roll(x, HEAD_DIM // 2, 1))


def _rope_tables(lp):
    pos = jnp.maximum(jnp.arange(lp, dtype=jnp.int32) - PAD_ROWS, 0).astype(F32)
    inv_freq = ROPE_THETA ** (-jnp.arange(0, HEAD_DIM, 2, dtype=F32) / HEAD_DIM)
    ang = pos[:, None] * inv_freq[None, :]
    c, s = jnp.cos(ang), jnp.sin(ang)
    reps = LANES // HEAD_DIM
    return jnp.tile(jnp.concatenate([c, c], axis=1), (1, reps)), jnp.tile(jnp.concatenate([-s, s], axis=1), (1, reps))


def _rope_fwd(zq, zkv, ctab, stab, after=None):
    lp = zq.shape[0]
    nb = lp // BLOCK
    back = lambda s: (jnp.maximum(s - 1, 0), 0)

    def body(zq_ref, zkv_ref, c_ref, s_ref, q_ref, k_ref, v_ref):
        step = pl.program_id(0)
        c128, s128 = c_ref[...], s_ref[...]

        def rope(x):
            reps = x.shape[1] // LANES
            return x * jnp.tile(c128, (1, reps)) + _swap_halves(x) * jnp.tile(s128, (1, reps))

        q_ref[...] = (rope(zq_ref[...].astype(F32)) * ATTN_SCALE).astype(BF16)
        kv = zkv_ref[...].astype(F32)
        k = rope(kv[:, :KV_DIM])
        v = kv[:, KV_DIM:]

        @pl.when(step == 0)
        def _():
            k_ref[...] = jnp.zeros_like(k_ref)
            v_ref[...] = jnp.zeros_like(v_ref)

        @pl.when(step > 0)
        def _():
            for h in range(N_KV_HEADS):
                k_ref[h] = k[:, h * HEAD_DIM:(h + 1) * HEAD_DIM].astype(BF16)
                v_ref[h] = v[:, h * HEAD_DIM:(h + 1) * HEAD_DIM].astype(BF16)

    kv_spec = pl.BlockSpec((N_KV_HEADS, BLOCK, HEAD_DIM), lambda s: (0, s, 0))
    return _pcall(
        body, after=after, name="rope_fwd", grid=(nb + 1,),
        in_specs=[pl.BlockSpec((BLOCK, Q_DIM), back), pl.BlockSpec((BLOCK, 2 * KV_DIM), back),
                  pl.BlockSpec((BLOCK, LANES), back), pl.BlockSpec((BLOCK, LANES), back)],
        out_specs=[pl.BlockSpec((BLOCK, Q_DIM), back), kv_spec, kv_spec],
        out_shape=[jax.ShapeDtypeStruct((lp, Q_DIM), BF16),
                   jax.ShapeDtypeStruct((N_KV_HEADS, lp + BLOCK, HEAD_DIM), BF16),
                   jax.ShapeDtypeStruct((N_KV_HEADS, lp + BLOCK, HEAD_DIM), BF16)],
        compiler_params=_params(("arbitrary",)),
    )(zq, zkv, ctab, stab)


def _rope_bwd(dq, dk, dv, dkm, dvm, ctab, stab):
    lp = dq.shape[0]
    width = Q_DIM + 2 * KV_DIM
    head_spec = pl.BlockSpec((N_KV_HEADS, BLOCK, HEAD_DIM), lambda i: (0, i, 0))
    meta_spec = _const_spec((N_KV_HEADS, BLOCK, HEAD_DIM))

    def body(dq_ref, dk_ref, dv_ref, dkm_ref, dvm_ref, c_ref, s_ref, dz_ref, sum_ref, kbuf, vbuf):
        i = pl.program_id(0)
        c128, s128 = c_ref[...], s_ref[...]
        first = (i == 0).astype(F32)

        def rope_t(x):
            reps = x.shape[1] // LANES
            return x * jnp.tile(c128, (1, reps)) + _swap_halves(x * jnp.tile(s128, (1, reps)))

        for h in range(N_KV_HEADS):
            kbuf[:, h * HEAD_DIM:(h + 1) * HEAD_DIM] = dk_ref[h] + first * dkm_ref[h]
            vbuf[:, h * HEAD_DIM:(h + 1) * HEAD_DIM] = dv_ref[h] + first * dvm_ref[h]
        dzq = rope_t(dq_ref[...] * ATTN_SCALE)
        dzk = rope_t(kbuf[...])
        dzv = vbuf[...]
        dz_ref[:, 0:Q_DIM] = dzq.astype(BF16)
        dz_ref[:, Q_DIM:Q_DIM + KV_DIM] = dzk.astype(BF16)
        dz_ref[:, Q_DIM + KV_DIM:width] = dzv.astype(BF16)

        @pl.when(i == 0)
        def _():
            sum_ref[...] = jnp.zeros_like(sum_ref)

        sum_ref[:, 0:Q_DIM] += jnp.sum(dzq, axis=0, keepdims=True)
        sum_ref[:, Q_DIM:Q_DIM + KV_DIM] += jnp.sum(dzk, axis=0, keepdims=True)
        sum_ref[:, Q_DIM + KV_DIM:width] += jnp.sum(dzv, axis=0, keepdims=True)

    return _pcall(
        body, name="rope_bwd", grid=(lp // BLOCK,),
        in_specs=[_row_spec(Q_DIM), head_spec, head_spec, meta_spec, meta_spec, _row_spec(LANES), _row_spec(LANES)],
        out_specs=[_row_spec(width), _const_spec((1, width))],
        out_shape=[jax.ShapeDtypeStruct((lp, width), BF16), jax.ShapeDtypeStruct((1, width), F32)],
        scratch_shapes=[pltpu.VMEM((BLOCK, KV_DIM), F32), pltpu.VMEM((BLOCK, KV_DIM), F32)],
        compiler_params=_params(("arbitrary",)),
    )(dq, dk, dv, dkm, dvm, ctab, stab)


def _attn_bias(i):
    r = lax.broadcasted_iota(jnp.int32, (BLOCK, 3 * BLOCK), 0)
    c = lax.broadcasted_iota(jnp.int32, (BLOCK, 3 * BLOCK), 1)
    qp = i * BLOCK + r - PAD_ROWS
    kp = (i - 1) * BLOCK + c - PAD_ROWS
    band = (c < 2 * BLOCK) & (kp >= N_META) & (kp <= qp) & (qp - kp < WINDOW)
    mp = c - 2 * BLOCK - PAD_ROWS
    meta = (c >= 2 * BLOCK) & (mp >= 0) & (mp <= qp)
    return jnp.where(band | meta, 0.0, NEG).astype(F32)


HALF = BLOCK // 2
HALF_KEYS = 2 * BLOCK


def _half_keys(prev, own, meta, half):
    if half == 0:
        return jnp.concatenate([prev, own[0:HALF], meta[HALF:BLOCK]], axis=0)
    return jnp.concatenate([prev[HALF:BLOCK], own, meta[HALF:BLOCK]], axis=0)


def _half_bias(i, half):
    r = lax.broadcasted_iota(jnp.int32, (HALF, HALF_KEYS), 0) + half * HALF
    c = lax.broadcasted_iota(jnp.int32, (HALF, HALF_KEYS), 1)
    n_prev = BLOCK - half * HALF
    qp = i * BLOCK + r - PAD_ROWS
    kp = jnp.where(c < n_prev, (i - 1) * BLOCK + c + half * HALF, i * BLOCK + c - n_prev) - PAD_ROWS
    band = (c < HALF_KEYS - HALF) & (kp >= N_META) & (kp <= qp) & (qp - kp < WINDOW)
    mp = c - (HALF_KEYS - HALF) + HALF - PAD_ROWS
    meta = (c >= HALF_KEYS - HALF) & (mp >= 0) & (mp <= qp)
    return jnp.where(band | meta, 0.0, NEG).astype(F32)


def _half_rows(ref, heads, half):
    rows = slice(half * HALF, (half + 1) * HALF)
    return jnp.concatenate([ref[rows, n * HEAD_DIM:(n + 1) * HEAD_DIM] for n in heads], axis=0)


def _half_sinks(sink_ref, heads):
    return jnp.concatenate([jnp.broadcast_to(sink_ref[0:1, n:n + 1], (HALF, 1)) for n in heads], axis=0)


def _stack_heads(ref, h):
    return jnp.concatenate(
        [ref[:, (h * GROUP + g) * HEAD_DIM:(h * GROUP + g + 1) * HEAD_DIM] for g in range(GROUP)], axis=0)


def _attn_probs(qs, k3, bias8, sink):
    s = lax.dot_general(qs, k3, (((1,), (1,)), ((), ())), preferred_element_type=F32) + bias8
    m = jnp.maximum(jnp.max(s, axis=1, keepdims=True), sink)
    p = jnp.exp(s - m)
    ps = jnp.exp(sink - m)
    inv = 1.0 / (jnp.sum(p, axis=1, keepdims=True) + ps)
    return p * inv, ps * inv


def _sink_column(sink_ref, h):
    return jnp.concatenate(
        [jnp.broadcast_to(sink_ref[0:1, h * GROUP + g:h * GROUP + g + 1], (BLOCK, 1)) for g in range(GROUP)], axis=0)


def _attn_fwd(q, k_sh, v_sh, sinks):
    lp = q.shape[0]
    nb = lp // BLOCK
    kv = lambda f: pl.BlockSpec((N_KV_HEADS, BLOCK, HEAD_DIM), f)

    def body(q_ref, kp_ref, kc_ref, km_ref, vp_ref, vc_ref, vm_ref, sink_ref, o_ref):
        i = pl.program_id(0)
        for half in range(2):
            bias = jnp.tile(_half_bias(i, half), (GROUP, 1))
            rows = slice(half * HALF, (half + 1) * HALF)
            for h in range(N_KV_HEADS):
                heads = range(h * GROUP, (h + 1) * GROUP)
                keys = _half_keys(kp_ref[h], kc_ref[h], km_ref[h], half)
                vals = _half_keys(vp_ref[h], vc_ref[h], vm_ref[h], half)
                p, _ = _attn_probs(_half_rows(q_ref, heads, half), keys, bias, _half_sinks(sink_ref, heads))
                o = jnp.dot(p.astype(BF16), vals, preferred_element_type=F32)
                for j, n in enumerate(heads):
                    o_ref[rows, n * HEAD_DIM:(n + 1) * HEAD_DIM] = o[j * HALF:(j + 1) * HALF].astype(BF16)

    prev, cur, meta = (lambda i: (0, i, 0)), (lambda i: (0, i + 1, 0)), (lambda i: (0, 1, 0))
    return _pcall(
        body, name="attn_fwd", grid=(nb,),
        in_specs=[_row_spec(Q_DIM), kv(prev), kv(cur), kv(meta), kv(prev), kv(cur), kv(meta), _const_spec((1, N_Q_HEADS))],
        out_specs=_row_spec(Q_DIM), out_shape=jax.ShapeDtypeStruct((lp, Q_DIM), BF16),
        compiler_params=_params(("parallel",)),
    )(q, k_sh, k_sh, k_sh, v_sh, v_sh, v_sh, sinks)


def _attn_bwd(q, k_sh, v_sh, sinks, do):
    lp = q.shape[0]
    nb = lp // BLOCK
    kv = lambda f: pl.BlockSpec((N_KV_HEADS, BLOCK, HEAD_DIM), f)
    cl = lambda s: jnp.minimum(s, nb - 1)

    def body(q_ref, do_ref, kp_ref, kc_ref, km_ref, vp_ref, vc_ref, vm_ref, sink_ref,
             dq_ref, dk_ref, dv_ref, dkm_ref, dvm_ref, dsink_ref, carry_k, carry_v):
        step = pl.program_id(0)

        @pl.when(step == 0)
        def _():
            carry_k[...] = jnp.zeros_like(carry_k)
            carry_v[...] = jnp.zeros_like(carry_v)
            dkm_ref[...] = jnp.zeros_like(dkm_ref)
            dvm_ref[...] = jnp.zeros_like(dvm_ref)
            dsink_ref[...] = jnp.zeros_like(dsink_ref)

        @pl.when(step < nb)
        def _():
            bias8 = jnp.tile(_attn_bias(step), (GROUP, 1))
            lane = lax.broadcasted_iota(jnp.int32, (1, LANES), 1)
            dsink = jnp.zeros((1, LANES), F32)
            for h in range(N_KV_HEADS):
                k3 = jnp.concatenate([kp_ref[h], kc_ref[h], km_ref[h]], axis=0)
                v3 = jnp.concatenate([vp_ref[h], vc_ref[h], vm_ref[h]], axis=0)
                qs = _stack_heads(q_ref, h)
                dos = _stack_heads(do_ref, h)
                p, psink = _attn_probs(qs, k3, bias8, _sink_column(sink_ref, h))
                dp = lax.dot_general(dos, v3, (((1,), (1,)), ((), ())), preferred_element_type=F32)
                delta = jnp.sum(p * dp, axis=1, keepdims=True)
                ds = (p * (dp - delta)).astype(BF16)
                dsk = -psink * delta
                for g in range(GROUP):
                    val = jnp.sum(dsk[g * BLOCK:(g + 1) * BLOCK], axis=0, keepdims=True)
                    dsink = dsink + jnp.where(lane == h * GROUP + g, val, 0.0)
                dqs = jnp.dot(ds, k3, preferred_element_type=F32)
                for g in range(GROUP):
                    n = h * GROUP + g
                    dq_ref[:, n * HEAD_DIM:(n + 1) * HEAD_DIM] = dqs[g * BLOCK:(g + 1) * BLOCK]
                dk3 = lax.dot_general(ds, qs, (((0,), (0,)), ((), ())), preferred_element_type=F32)
                dv3 = lax.dot_general(p.astype(BF16), dos, (((0,), (0,)), ((), ())), preferred_element_type=F32)
                dk_ref[h] = carry_k[h] + dk3[0:BLOCK]
                dv_ref[h] = carry_v[h] + dv3[0:BLOCK]
                carry_k[h] = dk3[BLOCK:2 * BLOCK]
                carry_v[h] = dv3[BLOCK:2 * BLOCK]
                dkm_ref[h] += dk3[2 * BLOCK:3 * BLOCK]
                dvm_ref[h] += dv3[2 * BLOCK:3 * BLOCK]
            dsink_ref[...] += dsink

        @pl.when(step == nb)
        def _():
            dk_ref[...] = carry_k[...]
            dv_ref[...] = carry_v[...]

    prev, cur, meta = (lambda s: (0, cl(s), 0)), (lambda s: (0, cl(s) + 1, 0)), (lambda s: (0, 1, 0))
    lag = lambda s: (0, jnp.maximum(s - 1, 0), 0)
    head_shape = jax.ShapeDtypeStruct((N_KV_HEADS, lp, HEAD_DIM), F32)
    meta_shape = jax.ShapeDtypeStruct((N_KV_HEADS, BLOCK, HEAD_DIM), F32)
    return _pcall(
        body, name="attn_bwd", grid=(nb + 1,),
        in_specs=[pl.BlockSpec((BLOCK, Q_DIM), lambda s: (cl(s), 0)), pl.BlockSpec((BLOCK, Q_DIM), lambda s: (cl(s), 0)),
                  kv(prev), kv(cur), kv(meta), kv(prev), kv(cur), kv(meta), _const_spec((1, N_Q_HEADS))],
        out_specs=[pl.BlockSpec((BLOCK, Q_DIM), lambda s: (cl(s), 0)), kv(lag), kv(lag),
                   _const_spec((N_KV_HEADS, BLOCK, HEAD_DIM)), _const_spec((N_KV_HEADS, BLOCK, HEAD_DIM)), _const_spec((1, LANES))],
        out_shape=[jax.ShapeDtypeStruct((lp, Q_DIM), F32), head_shape, head_shape, meta_shape, meta_shape,
                   jax.ShapeDtypeStruct((1, LANES), F32)],
        scratch_shapes=[pltpu.VMEM((N_KV_HEADS, BLOCK, HEAD_DIM), F32), pltpu.VMEM((N_KV_HEADS, BLOCK, HEAD_DIM), F32)],
        compiler_params=_params(("arbitrary",)),
    )(q, do, k_sh, k_sh, k_sh, v_sh, v_sh, v_sh, sinks)


CONV_CHUNK = 256


SUBLANES = 8
SH_BASE = BLOCK - 4 * SUBLANES
SH_ROWS = BLOCK + 3 * SUBLANES
DSH_ROWS = SH_ROWS


def _shifted_windows(src, sh, base, rows):
    for b in range(1, SUBLANES):
        sh[b] = src[base + b:base + b + rows, :]


def _window(src, sh, base, start, cols):
    a, b = divmod(start - base, SUBLANES)
    if b == 0:
        return src[start:start + BLOCK, cols]
    return sh[b, SUBLANES * a:SUBLANES * a + BLOCK, cols]


def _glu_masked(a_ref, g_ref, base):
    rows = base + lax.broadcasted_iota(jnp.int32, (BLOCK, 1), 0)
    return jnp.where(rows >= PAD_ROWS, a_ref[...].astype(F32) * _sigmoid(g_ref[...].astype(F32)), 0.0)


def _conv_fwd(zc, conv_w, conv_b, ln_g, ln_b, after=None):
    lp = zc.shape[0]
    cd = zc.shape[1] // 2
    nb = lp // BLOCK
    chunk = min(CONV_CHUNK, cd)
    back = lambda col: (lambda i: (jnp.maximum(i - 1, 0), col))
    lo = BLOCK - (CONV_WIDTH - 1)

    def body(ap_ref, gp_ref, ac_ref, gc_ref, w_ref, b_ref, lg_ref, lb_ref, co_ref, c2_ref, ext, sh):
        i = pl.program_id(0)
        ext[0:BLOCK, :] = _glu_masked(ap_ref, gp_ref, (i - 1) * BLOCK)
        ext[BLOCK:2 * BLOCK, :] = _glu_masked(ac_ref, gc_ref, i * BLOCK)
        _shifted_windows(ext, sh, SH_BASE, SH_ROWS)
        for c0 in range(0, cd, chunk):
            cols = slice(c0, c0 + chunk)
            acc = jnp.zeros((BLOCK, chunk), F32)
            for k in range(CONV_WIDTH):
                acc = acc + _window(ext, sh, SH_BASE, lo + k, cols) * w_ref[k:k + 1, cols]
            co_ref[:, cols] = acc + b_ref[:, cols]
        x = co_ref[...]
        mu = jnp.mean(x, axis=-1, keepdims=True)
        xc = x - mu
        r = lax.rsqrt(jnp.mean(xc * xc, axis=-1, keepdims=True) + EPS)
        y = xc * r * lg_ref[...] + lb_ref[...]
        c2_ref[...] = (y * _sigmoid(y)).astype(BF16)

    return _pcall(
        body, after=after, name="conv_fwd", grid=(nb,),
        in_specs=[pl.BlockSpec((BLOCK, cd), back(0)), pl.BlockSpec((BLOCK, cd), back(1)), _row_spec(cd, 0), _row_spec(cd, 1),
                  _const_spec((CONV_ROWS, cd)), _const_spec((1, cd)), _const_spec((1, cd)), _const_spec((1, cd))],
        out_specs=[_row_spec(cd), _row_spec(cd)],
        out_shape=[jax.ShapeDtypeStruct((lp, cd), F32), jax.ShapeDtypeStruct((lp, cd), BF16)],
        scratch_shapes=[pltpu.VMEM((2 * BLOCK, cd), F32), pltpu.VMEM((SUBLANES, SH_ROWS, cd), F32)],
        compiler_params=_params(("arbitrary",)),
    )(zc, zc, zc, zc, conv_w, conv_b, ln_g, ln_b)


def _conv_bwd_norm(dc2, conv_out, ln_g, ln_b):
    lp, cd = conv_out.shape

    def body(d_ref, x_ref, lg_ref, lb_ref, dco_ref, dlg_ref, dlb_ref, dcb_ref):
        i = pl.program_id(0)
        x = x_ref[...]
        g = lg_ref[...]
        mu = jnp.mean(x, axis=-1, keepdims=True)
        xc = x - mu
        r = lax.rsqrt(jnp.mean(xc * xc, axis=-1, keepdims=True) + EPS)
        xhat = xc * r
        y = xhat * g + lb_ref[...]
        sg = _sigmoid(y)
        dy = d_ref[...] * (sg * (1.0 + y * (1.0 - sg)))
        dxhat = dy * g
        dx = r * (dxhat - jnp.mean(dxhat, axis=-1, keepdims=True) - xhat * jnp.mean(dxhat * xhat, axis=-1, keepdims=True))
        dco_ref[...] = dx

        @pl.when(i == 0)
        def _():
            dlg_ref[...] = jnp.zeros_like(dlg_ref)
            dlb_ref[...] = jnp.zeros_like(dlb_ref)
            dcb_ref[...] = jnp.zeros_like(dcb_ref)

        dlg_ref[...] += jnp.sum(dy * xhat, axis=0, keepdims=True)
        dlb_ref[...] += jnp.sum(dy, axis=0, keepdims=True)
        dcb_ref[...] += jnp.sum(dx, axis=0, keepdims=True)

    vec = jax.ShapeDtypeStruct((1, cd), F32)
    return _pcall(
        body, name="conv_bwd_norm", grid=(lp // BLOCK,),
        in_specs=[_row_spec(cd), _row_spec(cd), _const_spec((1, cd)), _const_spec((1, cd))],
        out_specs=[_row_spec(cd), _const_spec((1, cd)), _const_spec((1, cd)), _const_spec((1, cd))],
        out_shape=[jax.ShapeDtypeStruct((lp, cd), F32), vec, vec, vec],
        compiler_params=_params(("arbitrary",)),
    )(dc2, conv_out, ln_g, ln_b)


def _conv_bwd_taps(dco, zc, conv_w):
    lp, cd = dco.shape
    nb = lp // BLOCK
    chunk = min(CONV_CHUNK, cd)
    back = lambda col: (lambda i: (jnp.maximum(i - 1, 0), col))
    fwd = lambda i: (jnp.minimum(i + 1, nb - 1), 0)
    lo = BLOCK - (CONV_WIDTH - 1)

    def body(dc_ref, dn_ref, ap_ref, gp_ref, ac_ref, gc_ref, w_ref, dz_ref, sum_ref, dw_ref, ext, dext, dcb, sh, dsh):
        i = pl.program_id(0)
        ext[0:BLOCK, :] = _glu_masked(ap_ref, gp_ref, (i - 1) * BLOCK)
        ext[BLOCK:2 * BLOCK, :] = _glu_masked(ac_ref, gc_ref, i * BLOCK)
        dext[0:BLOCK, :] = dc_ref[...]
        dext[BLOCK:2 * BLOCK, :] = dn_ref[...] * (i < nb - 1).astype(F32)
        _shifted_windows(ext, sh, SH_BASE, SH_ROWS)
        _shifted_windows(dext, dsh, 0, DSH_ROWS)

        @pl.when(i == 0)
        def _():
            dw_ref[...] = jnp.zeros_like(dw_ref)
            sum_ref[...] = jnp.zeros_like(sum_ref)

        for c0 in range(0, cd, chunk):
            cols = slice(c0, c0 + chunk)
            dcur = dext[0:BLOCK, cols]
            acc = jnp.zeros((BLOCK, chunk), F32)
            for k in range(CONV_WIDTH):
                s = CONV_WIDTH - 1 - k
                acc = acc + _window(dext, dsh, 0, s, cols) * w_ref[k:k + 1, cols]
                dw_ref[k:k + 1, cols] += jnp.sum(dcur * _window(ext, sh, SH_BASE, lo + k, cols), axis=0, keepdims=True)
            dcb[:, cols] = acc
        rows = i * BLOCK + lax.broadcasted_iota(jnp.int32, (BLOCK, 1), 0)
        dc = jnp.where(rows >= PAD_ROWS, dcb[...], 0.0)
        a = ac_ref[...].astype(F32)
        sg = _sigmoid(gc_ref[...].astype(F32))
        da = dc * sg
        dg = dc * a * sg * (1.0 - sg)
        dz_ref[:, 0:cd] = da.astype(BF16)
        dz_ref[:, cd:2 * cd] = dg.astype(BF16)
        sum_ref[:, 0:cd] += jnp.sum(da, axis=0, keepdims=True)
        sum_ref[:, cd:2 * cd] += jnp.sum(dg, axis=0, keepdims=True)

    return _pcall(
        body, name="conv_bwd_taps", grid=(nb,),
        in_specs=[_row_spec(cd), pl.BlockSpec((BLOCK, cd), fwd),
                  pl.BlockSpec((BLOCK, cd), back(0)), pl.BlockSpec((BLOCK, cd), back(1)), _row_spec(cd, 0), _row_spec(cd, 1),
                  _const_spec((CONV_ROWS, cd))],
        out_specs=[_row_spec(2 * cd), _const_spec((1, 2 * cd)), _const_spec((CONV_ROWS, cd))],
        out_shape=[jax.ShapeDtypeStruct((lp, 2 * cd), BF16), jax.ShapeDtypeStruct((1, 2 * cd), F32),
                   jax.ShapeDtypeStruct((CONV_ROWS, cd), F32)],
        scratch_shapes=[pltpu.VMEM((2 * BLOCK, cd), F32), pltpu.VMEM((2 * BLOCK, cd), F32), pltpu.VMEM((BLOCK, cd), F32),
                        pltpu.VMEM((SUBLANES, SH_ROWS, cd), F32), pltpu.VMEM((SUBLANES, DSH_ROWS, cd), F32)],
        compiler_params=_params(("arbitrary",)),
    )(dco, dco, zc, zc, zc, zc, conv_w)


def _gate_fwd(a, b, zg, after=None):
    lp, d = a.shape

    def body(a_ref, b_ref, ga_ref, gb_ref, m_ref):
        ga, gb = ga_ref[...].astype(F32), gb_ref[...].astype(F32)
        m_ref[...] = (_sigmoid(ga) * a_ref[...].astype(F32) + _sigmoid(gb) * b_ref[...].astype(F32)).astype(BF16)

    return _pcall(
        body, after=after, name="gate_fwd", grid=(lp // BLOCK,),
        in_specs=[_row_spec(d), _row_spec(d), _row_spec(d, 0), _row_spec(d, 1)], out_specs=_row_spec(d),
        out_shape=jax.ShapeDtypeStruct((lp, d), BF16), compiler_params=_params(("parallel",)),
    )(a, b, zg, zg)


def _gate_bwd(dm, a, b, zg):
    lp, d = a.shape

    def body(dm_ref, a_ref, b_ref, ga_ref, gb_ref, da_ref, db_ref, dz_ref, sum_ref, dbias_ref):
        i = pl.program_id(0)
        dm_ = dm_ref[...].astype(F32)
        sa = _sigmoid(ga_ref[...].astype(F32))
        sb = _sigmoid(gb_ref[...].astype(F32))
        db = dm_ * sb
        dga = dm_ * a_ref[...].astype(F32) * sa * (1.0 - sa)
        dgb = dm_ * b_ref[...].astype(F32) * sb * (1.0 - sb)
        da_ref[...] = (dm_ * sa).astype(BF16)
        db_ref[...] = db.astype(BF16)
        dz_ref[:, 0:d] = dga.astype(BF16)
        dz_ref[:, d:2 * d] = dgb.astype(BF16)

        @pl.when(i == 0)
        def _():
            sum_ref[...] = jnp.zeros_like(sum_ref)
            dbias_ref[...] = jnp.zeros_like(dbias_ref)

        sum_ref[:, 0:d] += jnp.sum(dga, axis=0, keepdims=True)
        sum_ref[:, d:2 * d] += jnp.sum(dgb, axis=0, keepdims=True)
        dbias_ref[...] += jnp.sum(db, axis=0, keepdims=True)

    return _pcall(
        body, name="gate_bwd", grid=(lp // BLOCK,),
        in_specs=[_row_spec(d), _row_spec(d), _row_spec(d), _row_spec(d, 0), _row_spec(d, 1)],
        out_specs=[_row_spec(d), _row_spec(d), _row_spec(2 * d), _const_spec((1, 2 * d)), _const_spec((1, d))],
        out_shape=[jax.ShapeDtypeStruct((lp, d), BF16), jax.ShapeDtypeStruct((lp, d), BF16), jax.ShapeDtypeStruct((lp, 2 * d), BF16),
                   jax.ShapeDtypeStruct((1, 2 * d), F32), jax.ShapeDtypeStruct((1, d), F32)],
        compiler_params=_params(("arbitrary",)),
    )(dm, a, b, zg, zg)


def _swiglu_fwd(gu, after=None):
    lp = gu.shape[0]
    f = gu.shape[1] // 2

    def body(g_ref, u_ref, o_ref):
        g = g_ref[...].astype(F32)
        o_ref[...] = (g * _sigmoid(g) * u_ref[...].astype(F32)).astype(BF16)

    return _pcall(
        body, after=after, name="swiglu_fwd", grid=(lp // BLOCK,), in_specs=[_row_spec(f, 0), _row_spec(f, 1)], out_specs=_row_spec(f),
        out_shape=jax.ShapeDtypeStruct((lp, f), BF16), compiler_params=_params(("parallel",)),
    )(gu, gu)


def _swiglu_bwd(dact, gu):
    lp, f = dact.shape

    def body(d_ref, g_ref, u_ref, o_ref):
        g = g_ref[...].astype(F32)
        d = d_ref[...].astype(F32)
        sg = _sigmoid(g)
        o_ref[:, 0:f] = (d * u_ref[...].astype(F32) * (sg * (1.0 + g * (1.0 - sg)))).astype(BF16)
        o_ref[:, f:2 * f] = (d * g * sg).astype(BF16)

    return _pcall(
        body, name="swiglu_bwd", grid=(lp // BLOCK,), in_specs=[_row_spec(f), _row_spec(f, 0), _row_spec(f, 1)],
        out_specs=_row_spec(2 * f), out_shape=jax.ShapeDtypeStruct((lp, 2 * f), BF16), compiler_params=_params(("parallel",)),
    )(dact, gu, gu)


ANY = pl.BlockSpec(memory_space=pl.ANY)


def _all_gather_rows(x, name, after=None):
    r, c = x.shape

    def body(x_ref, out_ref, send_sems, recv_sems, local_sem):
        mx, my, mc = lax.axis_index("x"), lax.axis_index("y"), lax.axis_index("c")
        me, sibling = (mx, my, mc), (mx, my, 1 - mc)
        chips = [(1 - mx, my), (mx, 1 - my), (1 - mx, 1 - my)]

        def rows(px, py, pc):
            return out_ref.at[pl.ds((4 * px + 2 * py + pc) * r, r), :]

        def copy(k, block, to, src=None):
            return pltpu.make_async_remote_copy(
                src_ref=rows(*block) if src is None else src, dst_ref=rows(*block),
                send_sem=send_sems.at[k], recv_sem=recv_sems.at[k], device_id=to, device_id_type=MESH)

        mine = pltpu.make_async_copy(x_ref, rows(*me), local_sem)
        mine.start()
        first = [copy(0, me, sibling, src=x_ref)]
        first += [copy(1 + j, me, (*chip, mc), src=x_ref) for j, chip in enumerate(chips)]
        for cp in first:
            cp.start()
        passed = [copy(4 + j, (*chip, mc), sibling) for j, chip in enumerate(chips)]
        for j, chip in enumerate(chips):
            copy(1 + j, (*chip, mc), me).wait_recv()
            passed[j].start()
        copy(0, sibling, me).wait_recv()
        for j, chip in enumerate(chips):
            copy(4 + j, (*chip, 1 - mc), me).wait_recv()
        for cp in first + passed:
            cp.wait_send()
        mine.wait()

    return _pcall(
        body, after=after, name=name, in_specs=[ANY], out_specs=ANY, out_shape=jax.ShapeDtypeStruct((N_DEV * r, c), x.dtype),
        scratch_shapes=[pltpu.SemaphoreType.DMA((7,)), pltpu.SemaphoreType.DMA((7,)), pltpu.SemaphoreType.DMA(())],
    )(x)


HBM = pl.BlockSpec(memory_space=pltpu.HBM)
SEM = pl.BlockSpec(memory_space=pltpu.SEMAPHORE)
IN_FLIGHT = pltpu.CompilerParams(has_side_effects=pltpu.SideEffectType.DATAFLOW_SIDE_EFFECTING)


def _place_rows(shard, after, name):
    r, c = shard.shape
    tr = _pick(r, max(16, ELEMENTWISE_BLOCK_BYTES // (4 * c)), 16)
    steps = r // tr
    dev = (4 * lax.axis_index("x") + 2 * lax.axis_index("y") + lax.axis_index("c")).astype(jnp.int32).reshape(1)

    def body(dev_ref, x_ref, after_ref, o_ref):
        o_ref[...] = x_ref[...].astype(BF16)

    return _pcall(
        body, name=name,
        grid_spec=pltpu.PrefetchScalarGridSpec(
            num_scalar_prefetch=1, grid=(steps,),
            in_specs=[pl.BlockSpec((tr, c), lambda i, dev_ref: (i, 0)), pl.BlockSpec(memory_space=pl.ANY)],
            out_specs=pl.BlockSpec((tr, c), lambda i, dev_ref: (dev_ref[0] * steps + i, 0))),
        out_shape=jax.ShapeDtypeStruct((N_DEV * r, c), BF16), compiler_params=_params(("parallel",)),
    )(dev, shard, after)


def _rows_start(full, plan, name, after=None):
    r = full.shape[0] // N_DEV
    n = len(plan(0, 0, 0))

    ordered = after is not None

    def body(*refs):
        full_ref, (send_sems, recv_sems) = refs[0], refs[1 + ordered:3 + ordered]
        mx, my, mc = lax.axis_index("x"), lax.axis_index("y"), lax.axis_index("c")
        for k, ((bx, by, bc), target) in enumerate(plan(mx, my, mc)):
            rows = full_ref.at[pl.ds((4 * bx + 2 * by + bc) * r, r), :]
            pltpu.make_async_remote_copy(
                src_ref=rows, dst_ref=rows, send_sem=send_sems.at[k], recv_sem=recv_sems.at[k],
                device_id=target, device_id_type=MESH).start()

    return pl.pallas_call(
        body, name=name, in_specs=[HBM] + [pl.BlockSpec(memory_space=pl.ANY)] * ordered, out_specs=(SEM, SEM, HBM),
        out_shape=(pltpu.SemaphoreType.DMA((n,)), pltpu.SemaphoreType.DMA((n,)), pltpu.HBM(full.shape, full.dtype)),
        input_output_aliases={0: 2}, compiler_params=IN_FLIGHT,
    )(pltpu.with_memory_space_constraint(full, pltpu.HBM), *([after] if ordered else []))


def _rows_wait(started, after, name):
    send_sem, recv_sem, full = started
    r = full.shape[0] // N_DEV
    n = send_sem.shape[0]

    def body(full_ref, send_ref, recv_ref, after_ref, out_ref):
        mx, my, mc = lax.axis_index("x"), lax.axis_index("y"), lax.axis_index("c")
        block = full_ref.at[pl.ds(0, r), :]
        for k in range(n):
            cp = pltpu.make_async_remote_copy(
                src_ref=block, dst_ref=block, send_sem=send_ref.at[k], recv_sem=recv_ref.at[k],
                device_id=(mx, my, mc), device_id_type=MESH)
            cp.wait_send()
            cp.wait_recv()

    return pl.pallas_call(
        body, name=name, in_specs=[HBM, SEM, SEM, pl.BlockSpec(memory_space=pl.ANY)], out_specs=HBM,
        out_shape=pltpu.HBM(full.shape, full.dtype), input_output_aliases={0: 0}, compiler_params=IN_FLIGHT,
    )(full, send_sem, recv_sem, after)


def _plan_direct(mx, my, mc):
    me = (mx, my, mc)
    return [(me, (mx, my, 1 - mc)), (me, (1 - mx, my, mc)), (me, (mx, 1 - my, mc)), (me, (1 - mx, 1 - my, mc))]


def _plan_pass_on(mx, my, mc):
    sibling = (mx, my, 1 - mc)
    return [((1 - mx, my, mc), sibling), ((mx, 1 - my, mc), sibling), ((1 - mx, 1 - my, mc), sibling)]


def _plan_neighbours(mx, my, mc):
    me = (mx, my, mc)
    return [(me, (mx, my, 1 - mc)), (me, (1 - mx, my, mc)), (me, (mx, 1 - my, mc))]


def _plan_relay(mx, my, mc):
    sibling = (mx, my, 1 - mc)
    source = ((mx + 1 - mc) % 2, (my + mc) % 2, mc)
    target = ((mx + mc) % 2, (my + 1 - mc) % 2, mc)
    return [((1 - mx, my, mc), sibling), ((mx, 1 - my, mc), sibling), (source, target)]


def _plan_pass_on_diagonal(mx, my, mc):
    return [((1 - mx, 1 - my, mc), (mx, my, 1 - mc))]


def _pair_exchange_start(g, name):
    r = g.shape[0] // N_DEV
    c = g.shape[1]
    land = (len(CHIPS), r, c)

    def body(g_ref, land_ref, send_sems, recv_sems, g_out, land_out):
        mx, my, mc = lax.axis_index("x"), lax.axis_index("y"), lax.axis_index("c")
        for j, (px, py) in enumerate(CHIPS):
            pltpu.make_async_remote_copy(
                src_ref=g_ref.at[pl.ds((4 * px + 2 * py + 1 - mc) * r, r), :], dst_ref=land_ref.at[j],
                send_sem=send_sems.at[j], recv_sem=recv_sems.at[j], device_id=(mx, my, 1 - mc), device_id_type=MESH).start()

    return pl.pallas_call(
        body, name=name, in_specs=[HBM, HBM], out_specs=(SEM, SEM, HBM, HBM),
        out_shape=(pltpu.SemaphoreType.DMA((4,)), pltpu.SemaphoreType.DMA((4,)), pltpu.HBM(g.shape, g.dtype), pltpu.HBM(land, g.dtype)),
        input_output_aliases={0: 2, 1: 3}, compiler_params=IN_FLIGHT,
    )(pltpu.with_memory_space_constraint(g, pltpu.HBM), pltpu.with_memory_space_constraint(lax.empty(land, g.dtype), pltpu.HBM))


def _pair_exchange_wait(send_sem, recv_sem, g, land, after, name):
    def body(g_ref, land_ref, send_ref, recv_ref, after_ref, g_out, land_out):
        mx, my, mc = lax.axis_index("x"), lax.axis_index("y"), lax.axis_index("c")
        for j in range(len(CHIPS)):
            cp = pltpu.make_async_remote_copy(
                src_ref=land_ref.at[0], dst_ref=land_ref.at[0], send_sem=send_ref.at[j], recv_sem=recv_ref.at[j],
                device_id=(mx, my, mc), device_id_type=MESH)
            cp.wait_send()
            cp.wait_recv()

    return pl.pallas_call(
        body, name=name, in_specs=[HBM, HBM, SEM, SEM, pl.BlockSpec(memory_space=pl.ANY)], out_specs=(HBM, HBM),
        out_shape=(pltpu.HBM(g.shape, g.dtype), pltpu.HBM(land.shape, land.dtype)), input_output_aliases={0: 0, 1: 1},
        compiler_params=IN_FLIGHT,
    )(g, land, send_sem, recv_sem, after)


def _chip_exchange_start(ps, after, name):
    def body(ps_ref, rx_ref, after_ref, send_sems, recv_sems, ps_out, rx_out):
        mx, my, mc = lax.axis_index("x"), lax.axis_index("y"), lax.axis_index("c")
        chips = [(1 - mx, my), (mx, 1 - my), (1 - mx, 1 - my)]
        for k, (px, py) in enumerate(chips):
            pltpu.make_async_remote_copy(
                src_ref=ps_ref.at[2 * px + py], dst_ref=rx_ref.at[2 * mx + my], send_sem=send_sems.at[k], recv_sem=recv_sems.at[k],
                device_id=(px, py, mc), device_id_type=MESH).start()

    return pl.pallas_call(
        body, name=name, in_specs=[HBM, HBM, pl.BlockSpec(memory_space=pl.ANY)], out_specs=(SEM, SEM, HBM, HBM),
        out_shape=(pltpu.SemaphoreType.DMA((3,)), pltpu.SemaphoreType.DMA((3,)), pltpu.HBM(ps.shape, ps.dtype), pltpu.HBM(ps.shape, ps.dtype)),
        input_output_aliases={0: 2, 1: 3}, compiler_params=IN_FLIGHT,
    )(pltpu.with_memory_space_constraint(ps, pltpu.HBM), pltpu.with_memory_space_constraint(lax.empty(ps.shape, ps.dtype), pltpu.HBM), after)


def _chip_exchange_wait(send_sem, recv_sem, ps, rx, after, name):
    def body(ps_ref, rx_ref, send_ref, recv_ref, after_ref, ps_out, rx_out):
        mx, my, mc = lax.axis_index("x"), lax.axis_index("y"), lax.axis_index("c")
        for k in range(3):
            cp = pltpu.make_async_remote_copy(
                src_ref=ps_ref.at[0], dst_ref=rx_ref.at[0], send_sem=send_ref.at[k], recv_sem=recv_ref.at[k],
                device_id=(mx, my, mc), device_id_type=MESH)
            cp.wait_send()
            cp.wait_recv()

    return pl.pallas_call(
        body, name=name, in_specs=[HBM, HBM, SEM, SEM, pl.BlockSpec(memory_space=pl.ANY)], out_specs=(HBM, HBM),
        out_shape=(pltpu.HBM(ps.shape, ps.dtype), pltpu.HBM(rx.shape, rx.dtype)), input_output_aliases={0: 0, 1: 1},
        compiler_params=IN_FLIGHT,
    )(ps, rx, send_sem, recv_sem, after)


def _sum_chips(ps, rx, name):
    n, r, c = rx.shape
    tr = _pick(r, max(16, 4 * ELEMENTWISE_BLOCK_BYTES // (4 * n * c)), 16)
    chip =(2 * lax.axis_index("x") + lax.axis_index("y")).astype(jnp.int32).reshape(1)

    def body(chip_ref, own_ref, x_ref, o_ref):
        me = chip_ref[0]
        own = own_ref[0].astype(F32)
        acc = jnp.where(me == 0, own, x_ref[0].astype(F32))
        for j in range(1, n):
            acc = acc + jnp.where(me == j, own, x_ref[j].astype(F32))
        o_ref[...] = acc

    return _pcall(
        body, name=name,
        grid_spec=pltpu.PrefetchScalarGridSpec(
            num_scalar_prefetch=1, grid=(r // tr,),
            in_specs=[pl.BlockSpec((1, tr, c), lambda i, chip_ref: (chip_ref[0], i, 0)), pl.BlockSpec((n, tr, c), lambda i, chip_ref: (0, i, 0))],
            out_specs=pl.BlockSpec((tr, c), lambda i, chip_ref: (i, 0))),
        out_shape=jax.ShapeDtypeStruct((r, c), F32), compiler_params=_params(("parallel",)),
    )(chip, ps, rx)


def _pair_exchange(g, name):
    r = g.shape[0] // N_DEV
    c = g.shape[1]

    def body(g_ref, theirs_ref, send_sems, recv_sems):
        mx, my, mc = lax.axis_index("x"), lax.axis_index("y"), lax.axis_index("c")
        sibling = (mx, my, 1 - mc)
        copies = []
        for j, (px, py) in enumerate(CHIPS):
            give = g_ref.at[pl.ds((4 * px + 2 * py + 1 - mc) * r, r), :]
            rc = pltpu.make_async_remote_copy(
                src_ref=give, dst_ref=theirs_ref.at[j], send_sem=send_sems.at[j], recv_sem=recv_sems.at[j],
                device_id=sibling, device_id_type=MESH)
            rc.start()
            copies.append(rc)
        for cp in copies:
            cp.wait()

    return _pcall(
        body, name=name, in_specs=[ANY], out_specs=ANY, out_shape=jax.ShapeDtypeStruct((len(CHIPS), r, c), g.dtype),
        scratch_shapes=[pltpu.SemaphoreType.DMA((4,)), pltpu.SemaphoreType.DMA((4,))],
    )(g)


def _pair_sum(g, theirs, name):
    nch, r, c = theirs.shape
    tr = _pick(r, max(16, 3 * ELEMENTWISE_BLOCK_BYTES // (2 * c)), 16)
    core = lax.axis_index("c").astype(jnp.int32).reshape(1)

    def body(core_ref, a_ref, b_ref, o_ref):
        o_ref[...] = (a_ref[...].astype(F32) + b_ref[...].astype(F32)).astype(o_ref.dtype)

    spec = pl.BlockSpec((1, tr, c), lambda j, i, core_ref: (j, i, 0))
    own = pl.BlockSpec((1, tr, c), lambda j, i, core_ref: (2 * j + core_ref[0], i, 0))
    return _pcall(
        body, name=name,
        grid_spec=pltpu.PrefetchScalarGridSpec(num_scalar_prefetch=1, grid=(nch, r // tr), in_specs=[own, spec], out_specs=spec),
        out_shape=jax.ShapeDtypeStruct(theirs.shape, theirs.dtype), compiler_params=_params(("parallel", "parallel")),
    )(core, g.reshape(N_DEV, r, c), theirs)


def _sum_blocks(rx, name):
    n, r, c = rx.shape
    tr = _pick(r, max(8, ELEMENTWISE_BLOCK_BYTES // (4 * n * c)), 8)

    def body(x_ref, o_ref):
        acc = x_ref[0].astype(F32)
        for j in range(1, n):
            acc = acc + x_ref[j].astype(F32)
        o_ref[...] = acc

    return _pcall(
        body, name=name, grid=(r // tr,), in_specs=[pl.BlockSpec((n, tr, c), lambda i: (0, i, 0))],
        out_specs=pl.BlockSpec((tr, c), lambda i: (i, 0)), out_shape=jax.ShapeDtypeStruct((r, c), F32),
        compiler_params=_params(("parallel",)),
    )(rx)


def _adamw(w, g, m, v, name):
    r, c = w.shape
    tr = _pick(r, max(8, ELEMENTWISE_BLOCK_BYTES // (4 * c)), 8)
    c1 = 1.0 - ADAM_B1 ** ADAM_STEP
    c2 = 1.0 - ADAM_B2 ** ADAM_STEP

    def body(w_ref, g_ref, m_ref, v_ref, d_ref, nm_ref, nv_ref):
        gg = g_ref[...]
        nm = ADAM_B1 * m_ref[...] + (1.0 - ADAM_B1) * gg
        nv = ADAM_B2 * v_ref[...] + (1.0 - ADAM_B2) * (gg * gg)
        d_ref[...] = -ADAM_LR * ((nm / c1) / (jnp.sqrt(nv / c2) + ADAM_EPS) + ADAM_WD * w_ref[...])
        nm_ref[...] = nm
        nv_ref[...] = nv

    spec = pl.BlockSpec((tr, c), lambda i: (i, 0))
    shp = jax.ShapeDtypeStruct((r, c), F32)
    return _pcall(
        body, name=name, grid=(r // tr,), in_specs=[spec] * 4, out_specs=[spec] * 3, out_shape=[shp] * 3,
        compiler_params=_params(("parallel",)),
    )(w, g, m, v)


def _pack(parts):
    flat, layout, row = [], [], 0
    for p in parts:
        n = p.size
        rows = -(-n // LANES)
        flat.append(jnp.pad(p.reshape(-1).astype(F32), (0, rows * LANES - n)))
        layout.append((row, n, p.shape))
        row += rows
    total = -(-row // 8) * 8
    if total > row:
        flat.append(jnp.zeros(((total - row) * LANES,), F32))
    return jnp.concatenate(flat).reshape(total, LANES), layout


def _unpack(slab, layout):
    flat = slab.reshape(-1)
    return [flat[row * LANES:row * LANES + n].reshape(shape) for row, n, shape in layout]


def kernel(x, meta_tokens, mix_norm_g, w_in, b_in, attn_sinks, conv_w, conv_b, conv_ln_g, conv_ln_b, w_attn_o, w_conv_o, b_conv_o, w_out, ffn_norm_g, w_gate_up, w_down, final_norm_g, loss_target, m_meta_tokens, m_mix_norm_g, m_w_in, m_b_in, m_attn_sinks, m_conv_w, m_conv_b, m_conv_ln_g, m_conv_ln_b, m_w_attn_o, m_w_conv_o, m_b_conv_o, m_w_out, m_ffn_norm_g, m_w_gate_up, m_w_down, m_final_norm_g, v_meta_tokens, v_mix_norm_g, v_w_in, v_b_in, v_attn_sinks, v_conv_w, v_conv_b, v_conv_ln_g, v_conv_ln_b, v_w_attn_o, v_w_conv_o, v_b_conv_o, v_w_out, v_ffn_norm_g, v_w_gate_up, v_w_down, v_final_norm_g):
    xs = x[0]
    tgt = loss_target[0]
    s, d = xs.shape
    lp = s + BLOCK
    cd = conv_b.shape[1]
    ffn = w_down.shape[1] * N_DEV
    dev = 4 * lax.axis_index("x") + 2 * lax.axis_index("y") + lax.axis_index("c")
    cw_cols = conv_w.shape[3]
    meta_cols = meta_tokens.shape[1]

    small, small_layout = _pack([meta_tokens, jnp.pad(conv_w[0, :, 0, :], ((0, CONV_ROWS - CONV_WIDTH), (0, 0)))])
    small_flat = _all_gather_rows(small, "gather_small")
    small_all = small_flat.reshape(N_DEV, *small.shape)
    meta_parts, cw_parts = zip(*[_unpack(small_all[j], small_layout) for j in range(N_DEV)])
    meta_full = jnp.concatenate(meta_parts, axis=1)
    conv_w_full = jnp.concatenate(cw_parts, axis=1)
    shards = ((w_in[0].T, "w_in"), (w_attn_o[0].T, "w_attn_o"), (w_conv_o[0].T, "w_conv_o"), (w_out[0], "w_out"),
              (w_gate_up[0].T, "w_gate_up"), (w_down[0], "w_down"))
    first = _rows_start(_place_rows(shards[0][0], small_flat, "place_w_in"), _plan_neighbours, "gather_start_w_in")
    placed, tok = [], first[2]
    for shard, name in shards[1:]:
        tok = _place_rows(shard, tok, "place_" + name)
        placed.append(tok)
    relay = _rows_start(_rows_wait(first, tok, "gather_wait_w_in"), _plan_relay, "gather_relay_start_w_in")
    h0, u = _prep(xs, meta_full, mix_norm_g, after=relay[2])
    diagonal = _rows_start(_rows_wait(relay, u, "gather_relay_wait_w_in"), _plan_pass_on_diagonal, "gather_diagonal_start_w_in")
    started, tok = [None], diagonal[2]
    for full, (_, name) in zip(placed, shards[1:]):
        started.append(_rows_start(full, _plan_direct, "gather_start_" + name, after=tok))
        tok = started[-1][2]
    win_t = _rows_wait(diagonal, tok, "gather_diagonal_wait_w_in")

    def arrived(w, after, name):
        return _rows_start(_rows_wait(started[w], after, "gather_wait_" + name), _plan_pass_on, "gather_pass_on_start_" + name)

    def whole(passing, after, name):
        return _rows_wait(passing, after, "gather_pass_on_wait_" + name)

    ctab, stab = _rope_tables(lp)
    mm = functools.partial(_matmul, tm=1056, tn=1024)

    bq, bkv, bc, bg = b_in[:, :Q_DIM], b_in[:, Q_DIM:Q_DIM + 2 * KV_DIM], b_in[:, Q_DIM + 2 * KV_DIM:Q_DIM + 2 * KV_DIM + 2 * cd], b_in[:, Q_DIM + 2 * KV_DIM + 2 * cd:]
    o_kv, o_c, o_g = Q_DIM, Q_DIM + 2 * KV_DIM, Q_DIM + 2 * KV_DIM + 2 * cd
    in_proj = functools.partial(_matmul, u, win_t, mode="nt", out_dtype=BF16, tm=2112, tn=512, tk=d)
    zq = in_proj(name="in_proj_q", bias=bq, b_row_off=0, b_rows=Q_DIM)
    zkv = in_proj(name="in_proj_kv", bias=bkv, b_row_off=o_kv, b_rows=2 * KV_DIM)
    zc = in_proj(name="in_proj_conv", bias=bc, b_row_off=o_c, b_rows=2 * cd)
    zg = in_proj(name="in_proj_gates", bias=bg, b_row_off=o_g, b_rows=2 * d)
    passing = arrived(1, zg, "w_attn_o")
    q_rot, k_sh, v_sh = _rope_fwd(zq, zkv, ctab, stab, after=passing[2])
    o = _attn_fwd(q_rot, k_sh, v_sh, attn_sinks)
    wao_t = whole(passing, o, "w_attn_o")
    br_a = mm(o, wao_t, mode="nt", name="attn_out_proj", out_dtype=BF16, tk=Q_DIM)
    passing = arrived(2, br_a, "w_conv_o")
    conv_out, c2 = _conv_fwd(zc, conv_w_full, conv_b, conv_ln_g, conv_ln_b, after=passing[2])
    wco_t = whole(passing, c2, "w_conv_o")
    br_b = mm(c2, wco_t, mode="nt", name="conv_out_proj", out_dtype=BF16, tk=cd, bias=b_conv_o)
    passing = arrived(3, br_b, "w_out")
    merged = _gate_fwd(br_a, br_b, zg, after=passing[2])
    wout = whole(passing, merged, "w_out")
    passing = arrived(4, wout, "w_gate_up")
    h1 = mm(merged, wout, mode="nn", name="mix_out_proj", out_dtype=F32, tn=512, tk=d, residual=h0, after=passing[2])
    u2 = _rmsnorm_fwd(h1, ffn_norm_g, "ffn_rmsnorm")
    wgu_t = whole(passing, u2, "w_gate_up")
    gu = _matmul(u2, wgu_t, mode="nt", name="ffn_gate_up", out_dtype=BF16, tm=2112, tn=512, tk=d)
    passing = arrived(5, gu, "w_down")
    act = _swiglu_fwd(gu, after=passing[2])
    wdown = whole(passing, act, "w_down")
    h2 = mm(act, wdown, mode="nn", name="ffn_down", out_dtype=F32, tn=512, tk=ffn // 2, residual=h1)
    dh2, dh2_b, loss_part, d_final_g = _final(h2, tgt, final_norm_g.reshape(1, d))

    wgrad = functools.partial(_matmul, mode="tn", out_dtype=BF16, tk=lp, tn=2048, b_inner=False)
    in_flight = {}

    def scatter_begin(g, name):
        return _pair_exchange_start(g, "rs_" + name + "_pair_start")

    def scatter_go_on(pair, after, name):
        g, theirs = _pair_exchange_wait(pair[0], pair[1], pair[2], pair[3], after, "rs_" + name + "_pair_wait")
        ps = _pair_sum(g, theirs, "rs_" + name + "_pair_sum")
        in_flight[name] = _chip_exchange_start(ps, theirs, "rs_" + name + "_chip_start")
        return in_flight[name][2]

    g_wdown = wgrad(act, dh2_b, name="ffn_down_dw", tm=256)
    pair = scatter_begin(g_wdown, "w_down")
    dact = _matmul(dh2_b, wdown, mode="nt", name="ffn_down_dx", out_dtype=BF16, tm=2112, tn=256, tk=d, after=pair[2])
    tok = scatter_go_on(pair, dact, "w_down")
    dgu = _swiglu_bwd(dact, gu)
    g_wgu_t = wgrad(dgu, u2, name="ffn_gate_up_dw", tm=512, after=tok)
    pair = scatter_begin(g_wgu_t, "w_gate_up")
    du2 = mm(dgu, wgu_t, mode="nn", name="ffn_gate_up_dx", out_dtype=F32, tn=512, tk=ffn // 2, after=pair[2])
    tok = scatter_go_on(pair, du2, "w_gate_up")
    dh1, dh1_b, d_ffn_g = _rmsnorm_bwd(du2, h1, ffn_norm_g, dh2, "ffn_rmsnorm_bwd")
    g_wout = wgrad(merged, dh1_b, name="mix_out_dw", tm=512, after=tok)
    pair = scatter_begin(g_wout, "w_out")
    dmerged = mm(dh1_b, wout, mode="nt", name="mix_out_dx", out_dtype=BF16, tk=d, after=pair[2])
    tok = scatter_go_on(pair, dmerged, "w_out")
    d_a, d_b, dz_g, sum_g, d_bco = _gate_bwd(dmerged, br_a, br_b, zg)
    g_wao_t = wgrad(d_a, o, name="attn_out_dw", tm=512, after=tok)
    pair = scatter_begin(g_wao_t, "w_attn_o")
    do = mm(d_a, wao_t, mode="nn", name="attn_out_dx", out_dtype=BF16, tk=d, after=pair[2])
    tok = scatter_go_on(pair, do, "w_attn_o")
    g_wco_t = wgrad(d_b, c2, name="conv_out_dw", tm=512, after=tok)
    pair = scatter_begin(g_wco_t, "w_conv_o")
    dc2 = mm(d_b, wco_t, mode="nn", name="conv_out_dx", out_dtype=F32, tk=d, after=pair[2])
    tok = scatter_go_on(pair, dc2, "w_conv_o")
    dq, dk, dv, dkm, dvm, d_sinks = _attn_bwd(q_rot, k_sh, v_sh, attn_sinks, do)
    dz_qkv, sum_qkv = _rope_bwd(dq, dk, dv, dkm, dvm, ctab, stab)
    dco, d_ln_g, d_ln_b, d_conv_b = _conv_bwd_norm(dc2, conv_out, conv_ln_g, conv_ln_b)
    dz_c, sum_c, d_conv_w = _conv_bwd_taps(dco, zc, conv_w_full)
    dz = jnp.concatenate([dz_qkv, dz_c, dz_g], axis=1)
    d_b_in = jnp.concatenate([sum_qkv, sum_c, sum_g], axis=1)
    in_dim = dz.shape[1]
    g_win_t = wgrad(dz, u, name="in_proj_dw", tm=512, after=tok)
    theirs = _pair_exchange(g_win_t, "rs_w_in_pair_exchange")
    in_flight["w_in"] = _chip_exchange_start(_pair_sum(g_win_t, theirs, "rs_w_in_pair_sum"), theirs, "rs_w_in_chip_start")
    du = mm(dz, win_t, mode="nn", name="in_proj_dx", out_dtype=F32, tk=in_dim // 4, after=in_flight["w_in"][2])
    grad_x, d_meta, d_mix_g = _rmsnorm_bwd_first(du, h0, mix_norm_g, dh1)

    weights = dict(meta_tokens=meta_tokens, mix_norm_g=mix_norm_g, w_in=w_in, b_in=b_in, attn_sinks=attn_sinks, conv_w=conv_w,
                   conv_b=conv_b, conv_ln_g=conv_ln_g, conv_ln_b=conv_ln_b, w_attn_o=w_attn_o, w_conv_o=w_conv_o, b_conv_o=b_conv_o,
                   w_out=w_out, ffn_norm_g=ffn_norm_g, w_gate_up=w_gate_up, w_down=w_down, final_norm_g=final_norm_g)
    m_in = dict(meta_tokens=m_meta_tokens, mix_norm_g=m_mix_norm_g, w_in=m_w_in, b_in=m_b_in, attn_sinks=m_attn_sinks, conv_w=m_conv_w,
                conv_b=m_conv_b, conv_ln_g=m_conv_ln_g, conv_ln_b=m_conv_ln_b, w_attn_o=m_w_attn_o, w_conv_o=m_w_conv_o,
                b_conv_o=m_b_conv_o, w_out=m_w_out, ffn_norm_g=m_ffn_norm_g, w_gate_up=m_w_gate_up, w_down=m_w_down,
                final_norm_g=m_final_norm_g)
    v_in = dict(meta_tokens=v_meta_tokens, mix_norm_g=v_mix_norm_g, w_in=v_w_in, b_in=v_b_in, attn_sinks=v_attn_sinks, conv_w=v_conv_w,
                conv_b=v_conv_b, conv_ln_g=v_conv_ln_g, conv_ln_b=v_conv_ln_b, w_attn_o=v_w_attn_o, w_conv_o=v_w_conv_o,
                b_conv_o=v_b_conv_o, w_out=v_w_out, ffn_norm_g=v_ffn_norm_g, w_gate_up=v_w_gate_up, w_down=v_w_down,
                final_norm_g=v_final_norm_g)
    names = list(weights)
    grads, delta, new_m, new_v = {}, {}, {}, {}
    transposed = ("w_in", "w_attn_o", "w_conv_o", "w_gate_up")
    tok = grad_x
    for n in ("w_down", "w_gate_up", "w_out", "w_attn_o", "w_conv_o", "w_in"):
        send_sem, recv_sem, ps, rx = in_flight[n]
        ps, rx = _chip_exchange_wait(send_sem, recv_sem, ps, rx, tok, "rs_" + n + "_chip_wait")
        g = _sum_chips(ps, rx, "rs_" + n + "_sum")
        g = g.T if n in transposed else g
        shape = weights[n].shape
        dl, nm, nv = _adamw(weights[n].reshape(g.shape), g, m_in[n].reshape(g.shape), v_in[n].reshape(g.shape), "adamw_" + n)
        grads[n], delta[n], new_m[n], new_v[n] = g.reshape(shape), dl.reshape(shape), nm.reshape(shape), nv.reshape(shape)
        tok = dl

    slab, slab_layout = _pack([loss_part[:, :1], d_mix_g, d_b_in, d_sinks[:, :N_Q_HEADS], d_conv_b, d_ln_g, d_ln_b, d_bco,
                               d_ffn_g, d_final_g, d_conv_w, d_meta])
    slab_all = _all_gather_rows(slab, "gather_small_grads", after=tok).reshape(N_DEV, *slab.shape)
    (loss, g_mix_g, g_b_in, g_sinks, g_conv_b, g_ln_g, g_ln_b, g_bco, g_ffn_g, g_final_g, g_conv_w_full, g_meta_full
     ) = _unpack(_sum_blocks(slab_all, "sum_small_grads"), slab_layout)
    g_conv_w = lax.dynamic_slice(g_conv_w_full, (0, dev * cw_cols), (CONV_WIDTH, cw_cols)).reshape(conv_w.shape)
    g_meta = lax.dynamic_slice(g_meta_full, (0, dev * meta_cols), (N_META, meta_cols))
    g_final_g = g_final_g.reshape(final_norm_g.shape)
    grads.update(meta_tokens=g_meta, mix_norm_g=g_mix_g, b_in=g_b_in, attn_sinks=g_sinks, conv_w=g_conv_w, conv_b=g_conv_b,
                 conv_ln_g=g_ln_g, conv_ln_b=g_ln_b, b_conv_o=g_bco, ffn_norm_g=g_ffn_g, final_norm_g=g_final_g)
    rest = [n for n in names if n not in delta]
    w_slab, rest_layout = _pack([weights[n] for n in rest])
    g_slab, _ = _pack([grads[n] for n in rest])
    m_slab, _ = _pack([m_in[n] for n in rest])
    v_slab, _ = _pack([v_in[n] for n in rest])
    dl, nm, nv = _adamw(w_slab, g_slab, m_slab, v_slab, "adamw_small")
    for n, a, b, c in zip(rest, _unpack(dl, rest_layout), _unpack(nm, rest_layout), _unpack(nv, rest_layout)):
        delta[n], new_m[n], new_v[n] = a, b, c

    return (loss.reshape(()), grad_x[None], *[grads[n] for n in names], *[delta[n] for n in names],
            *[new_m[n] for n in names], *[new_v[n] for n in names])
```

```python
import functools
import math

import jax
import jax.numpy as jnp
from jax import lax
from jax.experimental import pallas as pl
from jax.experimental.pallas import tpu as pltpu

F32 = jnp.float32
BF16 = jnp.bfloat16

N_DEV = 8
BLOCK = 128
N_META = 16
PAD_ROWS = BLOCK - N_META
HEAD_DIM = 64
N_Q_HEADS = 32
N_KV_HEADS = 4
GROUP = N_Q_HEADS // N_KV_HEADS
Q_DIM = N_Q_HEADS * HEAD_DIM
KV_DIM = N_KV_HEADS * HEAD_DIM
WINDOW = 128
CONV_WIDTH = 31
CONV_ROWS = 32
ROPE_THETA = 10000.0
EPS = 1e-6
ATTN_SCALE = HEAD_DIM ** -0.5
NEG = -1e30

ADAM_LR = 0.001
ADAM_B1 = 0.9
ADAM_B2 = 0.999
ADAM_EPS = 1e-08
ADAM_WD = 0.01
ADAM_STEP = 10

VMEM_LIMIT_BYTES = 56 * 1024 * 1024
LANES = 128
ELEMENTWISE_BLOCK_BYTES = 2 * 1024 * 1024
MESH = pl.DeviceIdType.MESH
CHIPS = ((0, 0), (0, 1), (1, 0), (1, 1))


def _pcall(body, after=None, **kw):
    if after is None:
        return pl.pallas_call(body, **kw)
    in_specs = list(kw.pop("in_specs"))
    n_in = len(in_specs)

    def ordered_body(*refs):
        return body(*refs[:n_in], *refs[n_in + 1:])

    call = pl.pallas_call(ordered_body, in_specs=in_specs + [pl.BlockSpec(memory_space=pl.ANY)], **kw)
    return lambda *args: call(*args, after)


def _params(semantics=None):
    if semantics is None:
        return pltpu.CompilerParams(vmem_limit_bytes=VMEM_LIMIT_BYTES)
    return pltpu.CompilerParams(dimension_semantics=semantics, vmem_limit_bytes=VMEM_LIMIT_BYTES)


def _pick(dim, pref, align):
    best = None
    t = align
    while t <= min(dim, pref):
        if dim % t == 0:
            best = t
        t += align
    return dim if best is None else best


def _sigmoid(x):
    return 1.0 / (1.0 + jnp.exp(-x))


def _matmul(a, b, *, mode, name, out_dtype, tm, tn, tk, bias=None, residual=None, b_inner=True,
            b_row_off=0, b_rows=None, after=None):
    if mode == "nn":
        m, k = a.shape
        n = b.shape[1]
    elif mode == "nt":
        m, k = a.shape
        n = b.shape[0] if b_rows is None else b_rows
    else:
        k, m = a.shape
        n = b.shape[1]
    tm = _pick(m, tm, 16)
    tn = _pick(math.gcd(n, b_row_off) if mode == "nt" and b_row_off else n, tn, LANES)
    tk = _pick(k, tk, LANES if mode != "tn" else 16)
    nm, nn, nk = m // tm, n // tn, k // tk
    if mode == "nt":
        assert b_row_off % tn == 0
    off = b_row_off // tn if mode == "nt" else 0

    if b_inner:
        grid = (nm, nn, nk)
        ij = lambda g0, g1: (g0, g1)
    else:
        grid = (nn, nm, nk)
        ij = lambda g0, g1: (g1, g0)

    if mode == "tn":
        a_spec = pl.BlockSpec((tk, tm), lambda g0, g1, kk: (kk, ij(g0, g1)[0]))
    else:
        a_spec = pl.BlockSpec((tm, tk), lambda g0, g1, kk: (ij(g0, g1)[0], kk))
    if mode == "nt":
        b_spec = pl.BlockSpec((tn, tk), lambda g0, g1, kk: (ij(g0, g1)[1] + off, kk))
    else:
        b_spec = pl.BlockSpec((tk, tn), lambda g0, g1, kk: (kk, ij(g0, g1)[1]))
    o_spec = pl.BlockSpec((tm, tn), lambda g0, g1, kk: ij(g0, g1))
    in_specs = [a_spec, b_spec]
    args = [a, b]
    if bias is not None:
        in_specs.append(pl.BlockSpec((1, tn), lambda g0, g1, kk: (0, ij(g0, g1)[1])))
        args.append(bias)
    if residual is not None:
        in_specs.append(o_spec)
        args.append(residual)
    dims = {"nn": (((1,), (0,)), ((), ())), "nt": (((1,), (1,)), ((), ())), "tn": (((0,), (0,)), ((), ()))}[mode]
    has_bias, has_res = bias is not None, residual is not None

    def body(*refs):
        a_ref, b_ref = refs[0], refs[1]
        pos = 2
        bias_ref = res_ref = None
        if has_bias:
            bias_ref = refs[pos]
            pos += 1
        if has_res:
            res_ref = refs[pos]
            pos += 1
        o_ref = refs[pos]
        acc_ref = refs[pos + 1] if nk > 1 else None

        def finish(acc):
            if has_bias:
                acc = acc + bias_ref[...]
            if has_res:
                acc = acc + res_ref[...]
            o_ref[...] = acc.astype(out_dtype)

        p = lax.dot_general(a_ref[...], b_ref[...], dims, preferred_element_type=F32)
        if nk == 1:
            finish(p)
        else:
            kk = pl.program_id(2)

            @pl.when(kk == 0)
            def _():
                acc_ref[...] = p

            @pl.when(kk > 0)
            def _():
                acc_ref[...] += p

            @pl.when(kk == nk - 1)
            def _():
                finish(acc_ref[...])

    return _pcall(
        body, after=after, name=name, grid=grid, in_specs=in_specs, out_specs=o_spec,
        out_shape=jax.ShapeDtypeStruct((m, n), out_dtype),
        scratch_shapes=[pltpu.VMEM((tm, tn), F32)] if nk > 1 else [],
        compiler_params=_params(("parallel", "parallel", "arbitrary")),
    )(*args)


def _row_spec(width, col=0):
    return pl.BlockSpec((BLOCK, width), lambda i: (i, col))


def _const_spec(shape):
    nd = len(shape)
    return pl.BlockSpec(shape, lambda i: (0,) * nd)


def _prep(x, meta_full, g, after=None):
    s, d = x.shape
    lp = s + BLOCK
    nb = lp // BLOCK

    def body(x_ref, meta_ref, g_ref, h_ref, u_ref):
        i = pl.program_id(0)

        @pl.when(i == 0)
        def _():
            h_ref[0:PAD_ROWS, :] = jnp.zeros((PAD_ROWS, d), F32)
            h_ref[PAD_ROWS:BLOCK, :] = meta_ref[...]

        @pl.when(i > 0)
        def _():
            h_ref[...] = x_ref[...]

        h = h_ref[...]
        r = lax.rsqrt(jnp.mean(h * h, axis=-1, keepdims=True) + EPS)
        u_ref[...] = (h * r * g_ref[...]).astype(BF16)

    return _pcall(
        body, after=after, name="prep_rmsnorm", grid=(nb,),
        in_specs=[pl.BlockSpec((BLOCK, d), lambda i: (jnp.maximum(i - 1, 0), 0)), _const_spec((N_META, d)), _const_spec((1, d))],
        out_specs=[_row_spec(d), _row_spec(d)],
        out_shape=[jax.ShapeDtypeStruct((lp, d), F32), jax.ShapeDtypeStruct((lp, d), BF16)],
        compiler_params=_params(("arbitrary",)),
    )(x, meta_full, g)


def _rmsnorm_fwd(h, g, name):
    lp, d = h.shape

    def body(h_ref, g_ref, u_ref):
        x = h_ref[...]
        r = lax.rsqrt(jnp.mean(x * x, axis=-1, keepdims=True) + EPS)
        u_ref[...] = (x * r * g_ref[...]).astype(BF16)

    return _pcall(
        body, name=name, grid=(lp // BLOCK,), in_specs=[_row_spec(d), _const_spec((1, d))], out_specs=_row_spec(d),
        out_shape=jax.ShapeDtypeStruct((lp, d), BF16), compiler_params=_params(("parallel",)),
    )(h, g)


def _rms_bwd_core(dy, x, g):
    r = lax.rsqrt(jnp.mean(x * x, axis=-1, keepdims=True) + EPS)
    xhat = x * r
    dxhat = dy * g
    dx = r * (dxhat - xhat * jnp.mean(dxhat * xhat, axis=-1, keepdims=True))
    return dx, jnp.sum(dy * xhat, axis=0, keepdims=True)


def _rmsnorm_bwd(dy, h, g, dres, name):
    lp, d = h.shape

    def body(dy_ref, h_ref, g_ref, dres_ref, dh_ref, dhb_ref, dg_ref):
        i = pl.program_id(0)
        dx, dg = _rms_bwd_core(dy_ref[...], h_ref[...], g_ref[...])
        dh = dres_ref[...] + dx
        dh_ref[...] = dh
        dhb_ref[...] = dh.astype(BF16)

        @pl.when(i == 0)
        def _():
            dg_ref[...] = jnp.zeros_like(dg_ref)

        dg_ref[...] += dg

    return _pcall(
        body, name=name, grid=(lp // BLOCK,),
        in_specs=[_row_spec(d), _row_spec(d), _const_spec((1, d)), _row_spec(d)],
        out_specs=[_row_spec(d), _row_spec(d), _const_spec((1, d))],
        out_shape=[jax.ShapeDtypeStruct((lp, d), F32), jax.ShapeDtypeStruct((lp, d), BF16), jax.ShapeDtypeStruct((1, d), F32)],
        compiler_params=_params(("arbitrary",)),
    )(dy, h, g, dres)


def _rmsnorm_bwd_first(dy, h, g, dres):
    lp, d = h.shape
    s = lp - BLOCK

    def body(dy_ref, h_ref, g_ref, dres_ref, gx_ref, dmeta_ref, dg_ref):
        i = pl.program_id(0)
        dx, dg = _rms_bwd_core(dy_ref[...], h_ref[...], g_ref[...])
        dh = dres_ref[...] + dx
        gx_ref[...] = dh

        @pl.when(i == 0)
        def _():
            dmeta_ref[...] = dh[PAD_ROWS:BLOCK, :]
            dg_ref[...] = jnp.zeros_like(dg_ref)

        dg_ref[...] += dg

    return _pcall(
        body, name="rmsnorm_bwd_first", grid=(lp // BLOCK,),
        in_specs=[_row_spec(d), _row_spec(d), _const_spec((1, d)), _row_spec(d)],
        out_specs=[pl.BlockSpec((BLOCK, d), lambda i: (jnp.maximum(i - 1, 0), 0)), _const_spec((N_META, d)), _const_spec((1, d))],
        out_shape=[jax.ShapeDtypeStruct((s, d), F32), jax.ShapeDtypeStruct((N_META, d), F32), jax.ShapeDtypeStruct((1, d), F32)],
        compiler_params=_params(("arbitrary",)),
    )(dy, h, g, dres)


def _final(h2, tgt, g):
    lp, d = h2.shape

    def body(h_ref, t_ref, g_ref, dh_ref, dhb_ref, loss_ref, dg_ref):
        i = pl.program_id(0)
        x = h_ref[...]
        gg = g_ref[...]
        r = lax.rsqrt(jnp.mean(x * x, axis=-1, keepdims=True) + EPS)
        xhat = x * r
        y = xhat * gg
        live = (i > 0).astype(F32)
        err = (y - t_ref[...]) * live
        dy = err * (1.0 / d)
        dxhat = dy * gg
        dh = r * (dxhat - xhat * jnp.mean(dxhat * xhat, axis=-1, keepdims=True))
        dh_ref[...] = dh
        dhb_ref[...] = dh.astype(BF16)

        @pl.when(i == 0)
        def _():
            loss_ref[...] = jnp.zeros_like(loss_ref)
            dg_ref[...] = jnp.zeros_like(dg_ref)

        row_loss = jnp.mean(err * err, axis=-1, keepdims=True)
        loss_ref[...] += 0.5 * jnp.sum(row_loss, axis=0, keepdims=True)
        dg_ref[...] += jnp.sum(dy * xhat, axis=0, keepdims=True)

    return _pcall(
        body, name="final_norm_loss", grid=(lp // BLOCK,),
        in_specs=[_row_spec(d), pl.BlockSpec((BLOCK, d), lambda i: (jnp.maximum(i - 1, 0), 0)), _const_spec((1, d))],
        out_specs=[_row_spec(d), _row_spec(d), _const_spec((1, LANES)), _const_spec((1, d))],
        out_shape=[jax.ShapeDtypeStruct((lp, d), F32), jax.ShapeDtypeStruct((lp, d), BF16),
                   jax.ShapeDtypeStruct((1, LANES), F32), jax.ShapeDtypeStruct((1, d), F32)],
        compiler_params=_params(("arbitrary",)),
    )(h2, tgt, g)


def _swap_halves(x):
    w = x.shape[1]
    lane = lax.broadcasted_iota(jnp.int32, x.shape, 1)
    first = (lane & (HEAD_DIM - 1)) < (HEAD_DIM // 2)
    return jnp.where(first, pltpu.roll(x, w - HEAD_DIM // 2, 1), pltpu.roll(x, HEAD_DIM // 2, 1))


def _rope_tables(lp):
    pos = jnp.maximum(jnp.arange(lp, dtype=jnp.int32) - PAD_ROWS, 0).astype(F32)
    inv_freq = ROPE_THETA ** (-jnp.arange(0, HEAD_DIM, 2, dtype=F32) / HEAD_DIM)
    ang = pos[:, None] * inv_freq[None, :]
    c, s = jnp.cos(ang), jnp.sin(ang)
    reps = LANES // HEAD_DIM
    return jnp.tile(jnp.concatenate([c, c], axis=1), (1, reps)), jnp.tile(jnp.concatenate([-s, s], axis=1), (1, reps))


def _rope_fwd(zq, zkv, ctab, stab, after=None):
    lp = zq.shape[0]
    nb = lp // BLOCK
    back = lambda s: (jnp.maximum(s - 1, 0), 0)

    def body(zq_ref, zkv_ref, c_ref, s_ref, q_ref, k_ref, v_ref):
        step = pl.program_id(0)
        c128, s128 = c_ref[...], s_ref[...]

        def rope(x):
            reps = x.shape[1] // LANES
            return x * jnp.tile(c128, (1, reps)) + _swap_halves(x) * jnp.tile(s128, (1, reps))

        q_ref[...] = (rope(zq_ref[...].astype(F32)) * ATTN_SCALE).astype(BF16)
        kv = zkv_ref[...].astype(F32)
        k = rope(kv[:, :KV_DIM])
        v = kv[:, KV_DIM:]

        @pl.when(step == 0)
        def _():
            k_ref[...] = jnp.zeros_like(k_ref)
            v_ref[...] = jnp.zeros_like(v_ref)

        @pl.when(step > 0)
        def _():
            for h in range(N_KV_HEADS):
                k_ref[h] = k[:, h * HEAD_DIM:(h + 1) * HEAD_DIM].astype(BF16)
                v_ref[h] = v[:, h * HEAD_DIM:(h + 1) * HEAD_DIM].astype(BF16)

    kv_spec = pl.BlockSpec((N_KV_HEADS, BLOCK, HEAD_DIM), lambda s: (0, s, 0))
    return _pcall(
        body, after=after, name="rope_fwd", grid=(nb + 1,),
        in_specs=[pl.BlockSpec((BLOCK, Q_DIM), back), pl.BlockSpec((BLOCK, 2 * KV_DIM), back),
                  pl.BlockSpec((BLOCK, LANES), back), pl.BlockSpec((BLOCK, LANES), back)],
        out_specs=[pl.BlockSpec((BLOCK, Q_DIM), back), kv_spec, kv_spec],
        out_shape=[jax.ShapeDtypeStruct((lp, Q_DIM), BF16),
                   jax.ShapeDtypeStruct((N_KV_HEADS, lp + BLOCK, HEAD_DIM), BF16),
                   jax.ShapeDtypeStruct((N_KV_HEADS, lp + BLOCK, HEAD_DIM), BF16)],
        compiler_params=_params(("arbitrary",)),
    )(zq, zkv, ctab, stab)


def _rope_bwd(dq, dk, dv, dkm, dvm, ctab, stab):
    lp = dq.shape[0]
    width = Q_DIM + 2 * KV_DIM
    head_spec = pl.BlockSpec((N_KV_HEADS, BLOCK, HEAD_DIM), lambda i: (0, i, 0))
    meta_spec = _const_spec((N_KV_HEADS, BLOCK, HEAD_DIM))

    def body(dq_ref, dk_ref, dv_ref, dkm_ref, dvm_ref, c_ref, s_ref, dz_ref, sum_ref, kbuf, vbuf):
        i = pl.program_id(0)
        c128, s128 = c_ref[...], s_ref[...]
        first = (i == 0).astype(F32)

        def rope_t(x):
            reps = x.shape[1] // LANES
            return x * jnp.tile(c128, (1, reps)) + _swap_halves(x * jnp.tile(s128, (1, reps)))

        for h in range(N_KV_HEADS):
            kbuf[:, h * HEAD_DIM:(h + 1) * HEAD_DIM] = dk_ref[h] + first * dkm_ref[h]
            vbuf[:, h * HEAD_DIM:(h + 1) * HEAD_DIM] = dv_ref[h] + first * dvm_ref[h]
        dzq = rope_t(dq_ref[...] * ATTN_SCALE)
        dzk = rope_t(kbuf[...])
        dzv = vbuf[...]
        dz_ref[:, 0:Q_DIM] = dzq.astype(BF16)
        dz_ref[:, Q_DIM:Q_DIM + KV_DIM] = dzk.astype(BF16)
        dz_ref[:, Q_DIM + KV_DIM:width] = dzv.astype(BF16)

        @pl.when(i == 0)
        def _():
            sum_ref[...] = jnp.zeros_like(sum_ref)

        sum_ref[:, 0:Q_DIM] += jnp.sum(dzq, axis=0, keepdims=True)
        sum_ref[:, Q_DIM:Q_DIM + KV_DIM] += jnp.sum(dzk, axis=0, keepdims=True)
        sum_ref[:, Q_DIM + KV_DIM:width] += jnp.sum(dzv, axis=0, keepdims=True)

    return _pcall(
        body, name="rope_bwd", grid=(lp // BLOCK,),
        in_specs=[_row_spec(Q_DIM), head_spec, head_spec, meta_spec, meta_spec, _row_spec(LANES), _row_spec(LANES)],
        out_specs=[_row_spec(width), _const_spec((1, width))],
        out_shape=[jax.ShapeDtypeStruct((lp, width), BF16), jax.ShapeDtypeStruct((1, width), F32)],
        scratch_shapes=[pltpu.VMEM((BLOCK, KV_DIM), F32), pltpu.VMEM((BLOCK, KV_DIM), F32)],
        compiler_params=_params(("arbitrary",)),
    )(dq, dk, dv, dkm, dvm, ctab, stab)


def _attn_bias(i):
    r = lax.broadcasted_iota(jnp.int32, (BLOCK, 3 * BLOCK), 0)
    c = lax.broadcasted_iota(jnp.int32, (BLOCK, 3 * BLOCK), 1)
    qp = i * BLOCK + r - PAD_ROWS
    kp = (i - 1) * BLOCK + c - PAD_ROWS
    band = (c < 2 * BLOCK) & (kp >= N_META) & (kp <= qp) & (qp - kp < WINDOW)
    mp = c - 2 * BLOCK - PAD_ROWS
    meta = (c >= 2 * BLOCK) & (mp >= 0) & (mp <= qp)
    return jnp.where(band | meta, 0.0, NEG).astype(F32)


HALF = BLOCK // 2
HALF_KEYS = 2 * BLOCK


def _half_keys(prev, own, meta, half):
    if half == 0:
        return jnp.concatenate([prev, own[0:HALF], meta[HALF:BLOCK]], axis=0)
    return jnp.concatenate([prev[HALF:BLOCK], own, meta[HALF:BLOCK]], axis=0)


def _half_bias(i, half):
    r = lax.broadcasted_iota(jnp.int32, (HALF, HALF_KEYS), 0) + half * HALF
    c = lax.broadcasted_iota(jnp.int32, (HALF, HALF_KEYS), 1)
    n_prev = BLOCK - half * HALF
    qp = i * BLOCK + r - PAD_ROWS
    kp = jnp.where(c < n_prev, (i - 1) * BLOCK + c + half * HALF, i * BLOCK + c - n_prev) - PAD_ROWS
    band = (c < HALF_KEYS - HALF) & (kp >= N_META) & (kp <= qp) & (qp - kp < WINDOW)
    mp = c - (HALF_KEYS - HALF) + HALF - PAD_ROWS
    meta = (c >= HALF_KEYS - HALF) & (mp >= 0) & (mp <= qp)
    return jnp.where(band | meta, 0.0, NEG).astype(F32)


def _half_rows(ref, heads, half):
    rows = slice(half * HALF, (half + 1) * HALF)
    return jnp.concatenate([ref[rows, n * HEAD_DIM:(n + 1) * HEAD_DIM] for n in heads], axis=0)


def _half_sinks(sink_ref, heads):
    return jnp.concatenate([jnp.broadcast_to(sink_ref[0:1, n:n + 1], (HALF, 1)) for n in heads], axis=0)


def _stack_heads(ref, h):
    return jnp.concatenate(
        [ref[:, (h * GROUP + g) * HEAD_DIM:(h * GROUP + g + 1) * HEAD_DIM] for g in range(GROUP)], axis=0)


def _attn_probs(qs, k3, bias8, sink):
    s = lax.dot_general(qs, k3, (((1,), (1,)), ((), ())), preferred_element_type=F32) + bias8
    m = jnp.maximum(jnp.max(s, axis=1, keepdims=True), sink)
    p = jnp.exp(s - m)
    ps = jnp.exp(sink - m)
    inv = 1.0 / (jnp.sum(p, axis=1, keepdims=True) + ps)
    return p * inv, ps * inv


def _sink_column(sink_ref, h):
    return jnp.concatenate(
        [jnp.broadcast_to(sink_ref[0:1, h * GROUP + g:h * GROUP + g + 1], (BLOCK, 1)) for g in range(GROUP)], axis=0)


def _attn_fwd(q, k_sh, v_sh, sinks):
    lp = q.shape[0]
    nb = lp // BLOCK
    kv = lambda f: pl.BlockSpec((N_KV_HEADS, BLOCK, HEAD_DIM), f)

    def body(q_ref, kp_ref, kc_ref, km_ref, vp_ref, vc_ref, vm_ref, sink_ref, o_ref):
        i = pl.program_id(0)
        for half in range(2):
            bias = jnp.tile(_half_bias(i, half), (GROUP, 1))
            rows = slice(half * HALF, (half + 1) * HALF)
            for h in range(N_KV_HEADS):
                heads = range(h * GROUP, (h + 1) * GROUP)
                keys = _half_keys(kp_ref[h], kc_ref[h], km_ref[h], half)
                vals = _half_keys(vp_ref[h], vc_ref[h], vm_ref[h], half)
                p, _ = _attn_probs(_half_rows(q_ref, heads, half), keys, bias, _half_sinks(sink_ref, heads))
                o = jnp.dot(p.astype(BF16), vals, preferred_element_type=F32)
                for j, n in enumerate(heads):
                    o_ref[rows, n * HEAD_DIM:(n + 1) * HEAD_DIM] = o[j * HALF:(j + 1) * HALF].astype(BF16)

    prev, cur, meta = (lambda i: (0, i, 0)), (lambda i: (0, i + 1, 0)), (lambda i: (0, 1, 0))
    return _pcall(
        body, name="attn_fwd", grid=(nb,),
        in_specs=[_row_spec(Q_DIM), kv(prev), kv(cur), kv(meta), kv(prev), kv(cur), kv(meta), _const_spec((1, N_Q_HEADS))],
        out_specs=_row_spec(Q_DIM), out_shape=jax.ShapeDtypeStruct((lp, Q_DIM), BF16),
        compiler_params=_params(("parallel",)),
    )(q, k_sh, k_sh, k_sh, v_sh, v_sh, v_sh, sinks)


def _attn_bwd(q, k_sh, v_sh, sinks, do):
    lp = q.shape[0]
    nb = lp // BLOCK
    kv = lambda f: pl.BlockSpec((N_KV_HEADS, BLOCK, HEAD_DIM), f)
    cl = lambda s: jnp.minimum(s, nb - 1)

    def body(q_ref, do_ref, kp_ref, kc_ref, km_ref, vp_ref, vc_ref, vm_ref, sink_ref,
             dq_ref, dk_ref, dv_ref, dkm_ref, dvm_ref, dsink_ref, carry_k, carry_v):
        step = pl.program_id(0)

        @pl.when(step == 0)
        def _():
            carry_k[...] = jnp.zeros_like(carry_k)
            carry_v[...] = jnp.zeros_like(carry_v)
            dkm_ref[...] = jnp.zeros_like(dkm_ref)
            dvm_ref[...] = jnp.zeros_like(dvm_ref)
            dsink_ref[...] = jnp.zeros_like(dsink_ref)

        @pl.when(step < nb)
        def _():
            bias8 = jnp.tile(_attn_bias(step), (GROUP, 1))
            lane = lax.broadcasted_iota(jnp.int32, (1, LANES), 1)
            dsink = jnp.zeros((1, LANES), F32)
            for h in range(N_KV_HEADS):
                k3 = jnp.concatenate([kp_ref[h], kc_ref[h], km_ref[h]], axis=0)
                v3 = jnp.concatenate([vp_ref[h], vc_ref[h], vm_ref[h]], axis=0)
                qs = _stack_heads(q_ref, h)
                dos = _stack_heads(do_ref, h)
                p, psink = _attn_probs(qs, k3, bias8, _sink_column(sink_ref, h))
                dp = lax.dot_general(dos, v3, (((1,), (1,)), ((), ())), preferred_element_type=F32)
                delta = jnp.sum(p * dp, axis=1, keepdims=True)
                ds = (p * (dp - delta)).astype(BF16)
                dsk = -psink * delta
                for g in range(GROUP):
                    val = jnp.sum(dsk[g * BLOCK:(g + 1) * BLOCK], axis=0, keepdims=True)
                    dsink = dsink + jnp.where(lane == h * GROUP + g, val, 0.0)
                dqs = jnp.dot(ds, k3, preferred_element_type=F32)
                for g in range(GROUP):
                    n = h * GROUP + g
                    dq_ref[:, n * HEAD_DIM:(n + 1) * HEAD_DIM] = dqs[g * BLOCK:(g + 1) * BLOCK]
                dk3 = lax.dot_general(ds, qs, (((0,), (0,)), ((), ())), preferred_element_type=F32)
                dv3 = lax.dot_general(p.astype(BF16), dos, (((0,), (0,)), ((), ())), preferred_element_type=F32)
                dk_ref[h] = carry_k[h] + dk3[0:BLOCK]
                dv_ref[h] = carry_v[h] + dv3[0:BLOCK]
                carry_k[h] = dk3[BLOCK:2 * BLOCK]
                carry_v[h] = dv3[BLOCK:2 * BLOCK]
                dkm_ref[h] += dk3[2 * BLOCK:3 * BLOCK]
                dvm_ref[h] += dv3[2 * BLOCK:3 * BLOCK]
            dsink_ref[...] += dsink

        @pl.when(step == nb)
        def _():
            dk_ref[...] = carry_k[...]
            dv_ref[...] = carry_v[...]

    prev, cur, meta = (lambda s: (0, cl(s), 0)), (lambda s: (0, cl(s) + 1, 0)), (lambda s: (0, 1, 0))
    lag = lambda s: (0, jnp.maximum(s - 1, 0), 0)
    head_shape = jax.ShapeDtypeStruct((N_KV_HEADS, lp, HEAD_DIM), F32)
    meta_shape = jax.ShapeDtypeStruct((N_KV_HEADS, BLOCK, HEAD_DIM), F32)
    return _pcall(
        body, name="attn_bwd", grid=(nb + 1,),
        in_specs=[pl.BlockSpec((BLOCK, Q_DIM), lambda s: (cl(s), 0)), pl.BlockSpec((BLOCK, Q_DIM), lambda s: (cl(s), 0)),
                  kv(prev), kv(cur), kv(meta), kv(prev), kv(cur), kv(meta), _const_spec((1, N_Q_HEADS))],
        out_specs=[pl.BlockSpec((BLOCK, Q_DIM), lambda s: (cl(s), 0)), kv(lag), kv(lag),
                   _const_spec((N_KV_HEADS, BLOCK, HEAD_DIM)), _const_spec((N_KV_HEADS, BLOCK, HEAD_DIM)), _const_spec((1, LANES))],
        out_shape=[jax.ShapeDtypeStruct((lp, Q_DIM), F32), head_shape, head_shape, meta_shape, meta_shape,
                   jax.ShapeDtypeStruct((1, LANES), F32)],
        scratch_shapes=[pltpu.VMEM((N_KV_HEADS, BLOCK, HEAD_DIM), F32), pltpu.VMEM((N_KV_HEADS, BLOCK, HEAD_DIM), F32)],
        compiler_params=_params(("arbitrary",)),
    )(q, do, k_sh, k_sh, k_sh, v_sh, v_sh, v_sh, sinks)


CONV_CHUNK = 256


SUBLANES = 8
SH_BASE = BLOCK - 4 * SUBLANES
SH_ROWS = BLOCK + 3 * SUBLANES
DSH_ROWS = SH_ROWS


def _shifted_windows(src, sh, base, rows):
    for b in range(1, SUBLANES):
        sh[b] = src[base + b:base + b + rows, :]


def _window(src, sh, base, start, cols):
    a, b = divmod(start - base, SUBLANES)
    if b == 0:
        return src[start:start + BLOCK, cols]
    return sh[b, SUBLANES * a:SUBLANES * a + BLOCK, cols]


def _glu_masked(a_ref, g_ref, base):
    rows = base + lax.broadcasted_iota(jnp.int32, (BLOCK, 1), 0)
    return jnp.where(rows >= PAD_ROWS, a_ref[...].astype(F32) * _sigmoid(g_ref[...].astype(F32)), 0.0)


def _conv_fwd(zc, conv_w, conv_b, ln_g, ln_b, after=None):
    lp = zc.shape[0]
    cd = zc.shape[1] // 2
    nb = lp // BLOCK
    chunk = min(CONV_CHUNK, cd)
    back = lambda col: (lambda i: (jnp.maximum(i - 1, 0), col))
    lo = BLOCK - (CONV_WIDTH - 1)

    def body(ap_ref, gp_ref, ac_ref, gc_ref, w_ref, b_ref, lg_ref, lb_ref, co_ref, c2_ref, ext, sh):
        i = pl.program_id(0)
        ext[0:BLOCK, :] = _glu_masked(ap_ref, gp_ref, (i - 1) * BLOCK)
        ext[BLOCK:2 * BLOCK, :] = _glu_masked(ac_ref, gc_ref, i * BLOCK)
        _shifted_windows(ext, sh, SH_BASE, SH_ROWS)
        for c0 in range(0, cd, chunk):
            cols = slice(c0, c0 + chunk)
            acc = jnp.zeros((BLOCK, chunk), F32)
            for k in range(CONV_WIDTH):
                acc = acc + _window(ext, sh, SH_BASE, lo + k, cols) * w_ref[k:k + 1, cols]
            co_ref[:, cols] = acc + b_ref[:, cols]
        x = co_ref[...]
        mu = jnp.mean(x, axis=-1, keepdims=True)
        xc = x - mu
        r = lax.rsqrt(jnp.mean(xc * xc, axis=-1, keepdims=True) + EPS)
        y = xc * r * lg_ref[...] + lb_ref[...]
        c2_ref[...] = (y * _sigmoid(y)).astype(BF16)

    return _pcall(
        body, after=after, name="conv_fwd", grid=(nb,),
        in_specs=[pl.BlockSpec((BLOCK, cd), back(0)), pl.BlockSpec((BLOCK, cd), back(1)), _row_spec(cd, 0), _row_spec(cd, 1),
                  _const_spec((CONV_ROWS, cd)), _const_spec((1, cd)), _const_spec((1, cd)), _const_spec((1, cd))],
        out_specs=[_row_spec(cd), _row_spec(cd)],
        out_shape=[jax.ShapeDtypeStruct((lp, cd), F32), jax.ShapeDtypeStruct((lp, cd), BF16)],
        scratch_shapes=[pltpu.VMEM((2 * BLOCK, cd), F32), pltpu.VMEM((SUBLANES, SH_ROWS, cd), F32)],
        compiler_params=_params(("arbitrary",)),
    )(zc, zc, zc, zc, conv_w, conv_b, ln_g, ln_b)


def _conv_bwd_norm(dc2, conv_out, ln_g, ln_b):
    lp, cd = conv_out.shape

    def body(d_ref, x_ref, lg_ref, lb_ref, dco_ref, dlg_ref, dlb_ref, dcb_ref):
        i = pl.program_id(0)
        x = x_ref[...]
        g = lg_ref[...]
        mu = jnp.mean(x, axis=-1, keepdims=True)
        xc = x - mu
        r = lax.rsqrt(jnp.mean(xc * xc, axis=-1, keepdims=True) + EPS)
        xhat = xc * r
        y = xhat * g + lb_ref[...]
        sg = _sigmoid(y)
        dy = d_ref[...] * (sg * (1.0 + y * (1.0 - sg)))
        dxhat = dy * g
        dx = r * (dxhat - jnp.mean(dxhat, axis=-1, keepdims=True) - xhat * jnp.mean(dxhat * xhat, axis=-1, keepdims=True))
        dco_ref[...] = dx

        @pl.when(i == 0)
        def _():
            dlg_ref[...] = jnp.zeros_like(dlg_ref)
            dlb_ref[...] = jnp.zeros_like(dlb_ref)
            dcb_ref[...] = jnp.zeros_like(dcb_ref)

        dlg_ref[...] += jnp.sum(dy * xhat, axis=0, keepdims=True)
        dlb_ref[...] += jnp.sum(dy, axis=0, keepdims=True)
        dcb_ref[...] += jnp.sum(dx, axis=0, keepdims=True)

    vec = jax.ShapeDtypeStruct((1, cd), F32)
    return _pcall(
        body, name="conv_bwd_norm", grid=(lp // BLOCK,),
        in_specs=[_row_spec(cd), _row_spec(cd), _const_spec((1, cd)), _const_spec((1, cd))],
        out_specs=[_row_spec(cd), _const_spec((1, cd)), _const_spec((1, cd)), _const_spec((1, cd))],
        out_shape=[jax.ShapeDtypeStruct((lp, cd), F32), vec, vec, vec],
        compiler_params=_params(("arbitrary",)),
    )(dc2, conv_out, ln_g, ln_b)


def _conv_bwd_taps(dco, zc, conv_w):
    lp, cd = dco.shape
    nb = lp // BLOCK
    chunk = min(CONV_CHUNK, cd)
    back = lambda col: (lambda i: (jnp.maximum(i - 1, 0), col))
    fwd = lambda i: (jnp.minimum(i + 1, nb - 1), 0)
    lo = BLOCK - (CONV_WIDTH - 1)

    def body(dc_ref, dn_ref, ap_ref, gp_ref, ac_ref, gc_ref, w_ref, dz_ref, sum_ref, dw_ref, ext, dext, dcb, sh, dsh):
        i = pl.program_id(0)
        ext[0:BLOCK, :] = _glu_masked(ap_ref, gp_ref, (i - 1) * BLOCK)
        ext[BLOCK:2 * BLOCK, :] = _glu_masked(ac_ref, gc_ref, i * BLOCK)
        dext[0:BLOCK, :] = dc_ref[...]
        dext[BLOCK:2 * BLOCK, :] = dn_ref[...] * (i < nb - 1).astype(F32)
        _shifted_windows(ext, sh, SH_BASE, SH_ROWS)
        _shifted_windows(dext, dsh, 0, DSH_ROWS)

        @pl.when(i == 0)
        def _():
            dw_ref[...] = jnp.zeros_like(dw_ref)
            sum_ref[...] = jnp.zeros_like(sum_ref)

        for c0 in range(0, cd, chunk):
            cols = slice(c0, c0 + chunk)
            dcur = dext[0:BLOCK, cols]
            acc = jnp.zeros((BLOCK, chunk), F32)
            for k in range(CONV_WIDTH):
                s = CONV_WIDTH - 1 - k
                acc = acc + _window(dext, dsh, 0, s, cols) * w_ref[k:k + 1, cols]
                dw_ref[k:k + 1, cols] += jnp.sum(dcur * _window(ext, sh, SH_BASE, lo + k, cols), axis=0, keepdims=True)
            dcb[:, cols] = acc
        rows = i * BLOCK + lax.broadcasted_iota(jnp.int32, (BLOCK, 1), 0)
        dc = jnp.where(rows >= PAD_ROWS, dcb[...], 0.0)
        a = ac_ref[...].astype(F32)
        sg = _sigmoid(gc_ref[...].astype(F32))
        da = dc * sg
        dg = dc * a * sg * (1.0 - sg)
        dz_ref[:, 0:cd] = da.astype(BF16)
        dz_ref[:, cd:2 * cd] = dg.astype(BF16)
        sum_ref[:, 0:cd] += jnp.sum(da, axis=0, keepdims=True)
        sum_ref[:, cd:2 * cd] += jnp.sum(dg, axis=0, keepdims=True)

    return _pcall(
        body, name="conv_bwd_taps", grid=(nb,),
        in_specs=[_row_spec(cd), pl.BlockSpec((BLOCK, cd), fwd),
                  pl.BlockSpec((BLOCK, cd), back(0)), pl.BlockSpec((BLOCK, cd), back(1)), _row_spec(cd, 0), _row_spec(cd, 1),
                  _const_spec((CONV_ROWS, cd))],
        out_specs=[_row_spec(2 * cd), _const_spec((1, 2 * cd)), _const_spec((CONV_ROWS, cd))],
        out_shape=[jax.ShapeDtypeStruct((lp, 2 * cd), BF16), jax.ShapeDtypeStruct((1, 2 * cd), F32),
                   jax.ShapeDtypeStruct((CONV_ROWS, cd), F32)],
        scratch_shapes=[pltpu.VMEM((2 * BLOCK, cd), F32), pltpu.VMEM((2 * BLOCK, cd), F32), pltpu.VMEM((BLOCK, cd), F32),
                        pltpu.VMEM((SUBLANES, SH_ROWS, cd), F32), pltpu.VMEM((SUBLANES, DSH_ROWS, cd), F32)],
        compiler_params=_params(("arbitrary",)),
    )(dco, dco, zc, zc, zc, zc, conv_w)


def _gate_fwd(a, b, zg, after=None):
    lp, d = a.shape

    def body(a_ref, b_ref, ga_ref, gb_ref, m_ref):
        ga, gb = ga_ref[...].astype(F32), gb_ref[...].astype(F32)
        m_ref[...] = (_sigmoid(ga) * a_ref[...].astype(F32) + _sigmoid(gb) * b_ref[...].astype(F32)).astype(BF16)

    return _pcall(
        body, after=after, name="gate_fwd", grid=(lp // BLOCK,),
        in_specs=[_row_spec(d), _row_spec(d), _row_spec(d, 0), _row_spec(d, 1)], out_specs=_row_spec(d),
        out_shape=jax.ShapeDtypeStruct((lp, d), BF16), compiler_params=_params(("parallel",)),
    )(a, b, zg, zg)


def _gate_bwd(dm, a, b, zg):
    lp, d = a.shape

    def body(dm_ref, a_ref, b_ref, ga_ref, gb_ref, da_ref, db_ref, dz_ref, sum_ref, dbias_ref):
        i = pl.program_id(0)
        dm_ = dm_ref[...].astype(F32)
        sa = _sigmoid(ga_ref[...].astype(F32))
        sb = _sigmoid(gb_ref[...].astype(F32))
        db = dm_ * sb
        dga = dm_ * a_ref[...].astype(F32) * sa * (1.0 - sa)
        dgb = dm_ * b_ref[...].astype(F32) * sb * (1.0 - sb)
        da_ref[...] = (dm_ * sa).astype(BF16)
        db_ref[...] = db.astype(BF16)
        dz_ref[:, 0:d] = dga.astype(BF16)
        dz_ref[:, d:2 * d] = dgb.astype(BF16)

        @pl.when(i == 0)
        def _():
            sum_ref[...] = jnp.zeros_like(sum_ref)
            dbias_ref[...] = jnp.zeros_like(dbias_ref)

        sum_ref[:, 0:d] += jnp.sum(dga, axis=0, keepdims=True)
        sum_ref[:, d:2 * d] += jnp.sum(dgb, axis=0, keepdims=True)
        dbias_ref[...] += jnp.sum(db, axis=0, keepdims=True)

    return _pcall(
        body, name="gate_bwd", grid=(lp // BLOCK,),
        in_specs=[_row_spec(d), _row_spec(d), _row_spec(d), _row_spec(d, 0), _row_spec(d, 1)],
        out_specs=[_row_spec(d), _row_spec(d), _row_spec(2 * d), _const_spec((1, 2 * d)), _const_spec((1, d))],
        out_shape=[jax.ShapeDtypeStruct((lp, d), BF16), jax.ShapeDtypeStruct((lp, d), BF16), jax.ShapeDtypeStruct((lp, 2 * d), BF16),
                   jax.ShapeDtypeStruct((1, 2 * d), F32), jax.ShapeDtypeStruct((1, d), F32)],
        compiler_params=_params(("arbitrary",)),
    )(dm, a, b, zg, zg)


def _swiglu_fwd(gu, after=None):
    lp = gu.shape[0]
    f = gu.shape[1] // 2

    def body(g_ref, u_ref, o_ref):
        g = g_ref[...].astype(F32)
        o_ref[...] = (g * _sigmoid(g) * u_ref[...].astype(F32)).astype(BF16)

    return _pcall(
        body, after=after, name="swiglu_fwd", grid=(lp // BLOCK,), in_specs=[_row_spec(f, 0), _row_spec(f, 1)], out_specs=_row_spec(f),
        out_shape=jax.ShapeDtypeStruct((lp, f), BF16), compiler_params=_params(("parallel",)),
    )(gu, gu)


def _swiglu_bwd(dact, gu):
    lp, f = dact.shape

    def body(d_ref, g_ref, u_ref, o_ref):
        g = g_ref[...].astype(F32)
        d = d_ref[...].astype(F32)
        sg = _sigmoid(g)
        o_ref[:, 0:f] = (d * u_ref[...].astype(F32) * (sg * (1.0 + g * (1.0 - sg)))).astype(BF16)
        o_ref[:, f:2 * f] = (d * g * sg).astype(BF16)

    return _pcall(
        body, name="swiglu_bwd", grid=(lp // BLOCK,), in_specs=[_row_spec(f), _row_spec(f, 0), _row_spec(f, 1)],
        out_specs=_row_spec(2 * f), out_shape=jax.ShapeDtypeStruct((lp, 2 * f), BF16), compiler_params=_params(("parallel",)),
    )(dact, gu, gu)


ANY = pl.BlockSpec(memory_space=pl.ANY)


def _all_gather_rows(x, name, after=None):
    r, c = x.shape

    def body(x_ref, out_ref, send_sems, recv_sems, local_sem):
        mx, my, mc = lax.axis_index("x"), lax.axis_index("y"), lax.axis_index("c")
        me, sibling = (mx, my, mc), (mx, my, 1 - mc)
        chips = [(1 - mx, my), (mx, 1 - my), (1 - mx, 1 - my)]

        def rows(px, py, pc):
            return out_ref.at[pl.ds((4 * px + 2 * py + pc) * r, r), :]

        def copy(k, block, to, src=None):
            return pltpu.make_async_remote_copy(
                src_ref=rows(*block) if src is None else src, dst_ref=rows(*block),
                send_sem=send_sems.at[k], recv_sem=recv_sems.at[k], device_id=to, device_id_type=MESH)

        mine = pltpu.make_async_copy(x_ref, rows(*me), local_sem)
        mine.start()
        first = [copy(0, me, sibling, src=x_ref)]
        first += [copy(1 + j, me, (*chip, mc), src=x_ref) for j, chip in enumerate(chips)]
        for cp in first:
            cp.start()
        passed = [copy(4 + j, (*chip, mc), sibling) for j, chip in enumerate(chips)]
        for j, chip in enumerate(chips):
            copy(1 + j, (*chip, mc), me).wait_recv()
            passed[j].start()
        copy(0, sibling, me).wait_recv()
        for j, chip in enumerate(chips):
            copy(4 + j, (*chip, 1 - mc), me).wait_recv()
        for cp in first + passed:
            cp.wait_send()
        mine.wait()

    return _pcall(
        body, after=after, name=name, in_specs=[ANY], out_specs=ANY, out_shape=jax.ShapeDtypeStruct((N_DEV * r, c), x.dtype),
        scratch_shapes=[pltpu.SemaphoreType.DMA((7,)), pltpu.SemaphoreType.DMA((7,)), pltpu.SemaphoreType.DMA(())],
    )(x)


HBM = pl.BlockSpec(memory_space=pltpu.HBM)
SEM = pl.BlockSpec(memory_space=pltpu.SEMAPHORE)
IN_FLIGHT = pltpu.CompilerParams(has_side_effects=pltpu.SideEffectType.DATAFLOW_SIDE_EFFECTING)


def _place_rows(shard, after, name):
    r, c = shard.shape
    tr = _pick(r, max(16, ELEMENTWISE_BLOCK_BYTES // (4 * c)), 16)
    steps = r // tr
    dev = (4 * lax.axis_index("x") + 2 * lax.axis_index("y") + lax.axis_index("c")).astype(jnp.int32).reshape(1)

    def body(dev_ref, x_ref, after_ref, o_ref):
        o_ref[...] = x_ref[...].astype(BF16)

    return _pcall(
        body, name=name,
        grid_spec=pltpu.PrefetchScalarGridSpec(
            num_scalar_prefetch=1, grid=(steps,),
            in_specs=[pl.BlockSpec((tr, c), lambda i, dev_ref: (i, 0)), pl.BlockSpec(memory_space=pl.ANY)],
            out_specs=pl.BlockSpec((tr, c), lambda i, dev_ref: (dev_ref[0] * steps + i, 0))),
        out_shape=jax.ShapeDtypeStruct((N_DEV * r, c), BF16), compiler_params=_params(("parallel",)),
    )(dev, shard, after)


def _rows_start(full, plan, name, after=None):
    r = full.shape[0] // N_DEV
    n = len(plan(0, 0, 0))

    ordered = after is not None

    def body(*refs):
        full_ref, (send_sems, recv_sems) = refs[0], refs[1 + ordered:3 + ordered]
        mx, my, mc = lax.axis_index("x"), lax.axis_index("y"), lax.axis_index("c")
        for k, ((bx, by, bc), target) in enumerate(plan(mx, my, mc)):
            rows = full_ref.at[pl.ds((4 * bx + 2 * by + bc) * r, r), :]
            pltpu.make_async_remote_copy(
                src_ref=rows, dst_ref=rows, send_sem=send_sems.at[k], recv_sem=recv_sems.at[k],
                device_id=target, device_id_type=MESH).start()

    return pl.pallas_call(
        body, name=name, in_specs=[HBM] + [pl.BlockSpec(memory_space=pl.ANY)] * ordered, out_specs=(SEM, SEM, HBM),
        out_shape=(pltpu.SemaphoreType.DMA((n,)), pltpu.SemaphoreType.DMA((n,)), pltpu.HBM(full.shape, full.dtype)),
        input_output_aliases={0: 2}, compiler_params=IN_FLIGHT,
    )(pltpu.with_memory_space_constraint(full, pltpu.HBM), *([after] if ordered else []))


def _rows_wait(started, after, name):
    send_sem, recv_sem, full = started
    r = full.shape[0] // N_DEV
    n = send_sem.shape[0]

    def body(full_ref, send_ref, recv_ref, after_ref, out_ref):
        mx, my, mc = lax.axis_index("x"), lax.axis_index("y"), lax.axis_index("c")
        block = full_ref.at[pl.ds(0, r), :]
        for k in range(n):
            cp = pltpu.make_async_remote_copy(
                src_ref=block, dst_ref=block, send_sem=send_ref.at[k], recv_sem=recv_ref.at[k],
                device_id=(mx, my, mc), device_id_type=MESH)
            cp.wait_send()
            cp.wait_recv()

    return pl.pallas_call(
        body, name=name, in_specs=[HBM, SEM, SEM, pl.BlockSpec(memory_space=pl.ANY)], out_specs=HBM,
        out_shape=pltpu.HBM(full.shape, full.dtype), input_output_aliases={0: 0}, compiler_params=IN_FLIGHT,
    )(full, send_sem, recv_sem, after)


def _plan_direct(mx, my, mc):
    me = (mx, my, mc)
    return [(me, (mx, my, 1 - mc)), (me, (1 - mx, my, mc)), (me, (mx, 1 - my, mc)), (me, (1 - mx, 1 - my, mc))]


def _plan_pass_on(mx, my, mc):
    sibling = (mx, my, 1 - mc)
    return [((1 - mx, my, mc), sibling), ((mx, 1 - my, mc), sibling), ((1 - mx, 1 - my, mc), sibling)]


def _plan_neighbours(mx, my, mc):
    me = (mx, my, mc)
    return [(me, (mx, my, 1 - mc)), (me, (1 - mx, my, mc)), (me, (mx, 1 - my, mc))]


def _plan_relay(mx, my, mc):
    sibling = (mx, my, 1 - mc)
    source = ((mx + 1 - mc) % 2, (my + mc) % 2, mc)
    target = ((mx + mc) % 2, (my + 1 - mc) % 2, mc)
    return [((1 - mx, my, mc), sibling), ((mx, 1 - my, mc), sibling), (source, target)]


def _plan_pass_on_diagonal(mx, my, mc):
    return [((1 - mx, 1 - my, mc), (mx, my, 1 - mc))]


def _pair_exchange_start(g, name):
    r = g.shape[0] // N_DEV
    c = g.shape[1]
    land = (len(CHIPS), r, c)

    def body(g_ref, land_ref, send_sems, recv_sems, g_out, land_out):
        mx, my, mc = lax.axis_index("x"), lax.axis_index("y"), lax.axis_index("c")
        for j, (px, py) in enumerate(CHIPS):
            pltpu.make_async_remote_copy(
                src_ref=g_ref.at[pl.ds((4 * px + 2 * py + 1 - mc) * r, r), :], dst_ref=land_ref.at[j],
                send_sem=send_sems.at[j], recv_sem=recv_sems.at[j], device_id=(mx, my, 1 - mc), device_id_type=MESH).start()

    return pl.pallas_call(
        body, name=name, in_specs=[HBM, HBM], out_specs=(SEM, SEM, HBM, HBM),
        out_shape=(pltpu.SemaphoreType.DMA((4,)), pltpu.SemaphoreType.DMA((4,)), pltpu.HBM(g.shape, g.dtype), pltpu.HBM(land, g.dtype)),
        input_output_aliases={0: 2, 1: 3}, compiler_params=IN_FLIGHT,
    )(pltpu.with_memory_space_constraint(g, pltpu.HBM), pltpu.with_memory_space_constraint(lax.empty(land, g.dtype), pltpu.HBM))


def _pair_exchange_wait(send_sem, recv_sem, g, land, after, name):
    def body(g_ref, land_ref, send_ref, recv_ref, after_ref, g_out, land_out):
        mx, my, mc = lax.axis_index("x"), lax.axis_index("y"), lax.axis_index("c")
        for j in range(len(CHIPS)):
            cp = pltpu.make_async_remote_copy(
                src_ref=land_ref.at[0], dst_ref=land_ref.at[0], send_sem=send_ref.at[j], recv_sem=recv_ref.at[j],
                device_id=(mx, my, mc), device_id_type=MESH)
            cp.wait_send()
            cp.wait_recv()

    return pl.pallas_call(
        body, name=name, in_specs=[HBM, HBM, SEM, SEM, pl.BlockSpec(memory_space=pl.ANY)], out_specs=(HBM, HBM),
        out_shape=(pltpu.HBM(g.shape, g.dtype), pltpu.HBM(land.shape, land.dtype)), input_output_aliases={0: 0, 1: 1},
        compiler_params=IN_FLIGHT,
    )(g, land, send_sem, recv_sem, after)


def _chip_exchange_start(ps, after, name):
    def body(ps_ref, rx_ref, after_ref, send_sems, recv_sems, ps_out, rx_out):
        mx, my, mc = lax.axis_index("x"), lax.axis_index("y"), lax.axis_index("c")
        chips = [(1 - mx, my), (mx, 1 - my), (1 - mx, 1 - my)]
        for k, (px, py) in enumerate(chips):
            pltpu.make_async_remote_copy(
                src_ref=ps_ref.at[2 * px + py], dst_ref=rx_ref.at[2 * mx + my], send_sem=send_sems.at[k], recv_sem=recv_sems.at[k],
                device_id=(px, py, mc), device_id_type=MESH).start()

    return pl.pallas_call(
        body, name=name, in_specs=[HBM, HBM, pl.BlockSpec(memory_space=pl.ANY)], out_specs=(SEM, SEM, HBM, HBM),
        out_shape=(pltpu.SemaphoreType.DMA((3,)), pltpu.SemaphoreType.DMA((3,)), pltpu.HBM(ps.shape, ps.dtype), pltpu.HBM(ps.shape, ps.dtype)),
        input_output_aliases={0: 2, 1: 3}, compiler_params=IN_FLIGHT,
    )(pltpu.with_memory_space_constraint(ps, pltpu.HBM), pltpu.with_memory_space_constraint(lax.empty(ps.shape, ps.dtype), pltpu.HBM), after)


def _chip_exchange_wait(send_sem, recv_sem, ps, rx, after, name):
    def body(ps_ref, rx_ref, send_ref, recv_ref, after_ref, ps_out, rx_out):
        mx, my, mc = lax.axis_index("x"), lax.axis_index("y"), lax.axis_index("c")
        for k in range(3):
            cp = pltpu.make_async_remote_copy(
                src_ref=ps_ref.at[0], dst_ref=rx_ref.at[0], send_sem=send_ref.at[k], recv_sem=recv_ref.at[k],
                device_id=(mx, my, mc), device_id_type=MESH)
            cp.wait_send()
            cp.wait_recv()

    return pl.pallas_call(
        body, name=name, in_specs=[HBM, HBM, SEM, SEM, pl.BlockSpec(memory_space=pl.ANY)], out_specs=(HBM, HBM),
        out_shape=(pltpu.HBM(ps.shape, ps.dtype), pltpu.HBM(rx.shape, rx.dtype)), input_output_aliases={0: 0, 1: 1},
        compiler_params=IN_FLIGHT,
    )(ps, rx, send_sem, recv_sem, after)


def _sum_chips(ps, rx, name):
    n, r, c = rx.shape
    tr = _pick(r, max(16, 4 * ELEMENTWISE_BLOCK_BYTES // (4 * n * c)), 16)
    chip =(2 * lax.axis_index("x") + lax.axis_index("y")).astype(jnp.int32).reshape(1)

    def body(chip_ref, own_ref, x_ref, o_ref):
        me = chip_ref[0]
        own = own_ref[0].astype(F32)
        acc = jnp.where(me == 0, own, x_ref[0].astype(F32))
        for j in range(1, n):
            acc = acc + jnp.where(me == j, own, x_ref[j].astype(F32))
        o_ref[...] = acc

    return _pcall(
        body, name=name,
        grid_spec=pltpu.PrefetchScalarGridSpec(
            num_scalar_prefetch=1, grid=(r // tr,),
            in_specs=[pl.BlockSpec((1, tr, c), lambda i, chip_ref: (chip_ref[0], i, 0)), pl.BlockSpec((n, tr, c), lambda i, chip_ref: (0, i, 0))],
            out_specs=pl.BlockSpec((tr, c), lambda i, chip_ref: (i, 0))),
        out_shape=jax.ShapeDtypeStruct((r, c), F32), compiler_params=_params(("parallel",)),
    )(chip, ps, rx)


def _pair_exchange(g, name):
    r = g.shape[0] // N_DEV
    c = g.shape[1]

    def body(g_ref, theirs_ref, send_sems, recv_sems):
        mx, my, mc = lax.axis_index("x"), lax.axis_index("y"), lax.axis_index("c")
        sibling = (mx, my, 1 - mc)
        copies = []
        for j, (px, py) in enumerate(CHIPS):
            give = g_ref.at[pl.ds((4 * px + 2 * py + 1 - mc) * r, r), :]
            rc = pltpu.make_async_remote_copy(
                src_ref=give, dst_ref=theirs_ref.at[j], send_sem=send_sems.at[j], recv_sem=recv_sems.at[j],
                device_id=sibling, device_id_type=MESH)
            rc.start()
            copies.append(rc)
        for cp in copies:
            cp.wait()

    return _pcall(
        body, name=name, in_specs=[ANY], out_specs=ANY, out_shape=jax.ShapeDtypeStruct((len(CHIPS), r, c), g.dtype),
        scratch_shapes=[pltpu.SemaphoreType.DMA((4,)), pltpu.SemaphoreType.DMA((4,))],
    )(g)


def _pair_sum(g, theirs, name):
    nch, r, c = theirs.shape
    tr = _pick(r, max(16, 3 * ELEMENTWISE_BLOCK_BYTES // (2 * c)), 16)
    core = lax.axis_index("c").astype(jnp.int32).reshape(1)

    def body(core_ref, a_ref, b_ref, o_ref):
        o_ref[...] = (a_ref[...].astype(F32) + b_ref[...].astype(F32)).astype(o_ref.dtype)

    spec = pl.BlockSpec((1, tr, c), lambda j, i, core_ref: (j, i, 0))
    own = pl.BlockSpec((1, tr, c), lambda j, i, core_ref: (2 * j + core_ref[0], i, 0))
    return _pcall(
        body, name=name,
        grid_spec=pltpu.PrefetchScalarGridSpec(num_scalar_prefetch=1, grid=(nch, r // tr), in_specs=[own, spec], out_specs=spec),
        out_shape=jax.ShapeDtypeStruct(theirs.shape, theirs.dtype), compiler_params=_params(("parallel", "parallel")),
    )(core, g.reshape(N_DEV, r, c), theirs)


def _sum_blocks(rx, name):
    n, r, c = rx.shape
    tr = _pick(r, max(8, ELEMENTWISE_BLOCK_BYTES // (4 * n * c)), 8)

    def body(x_ref, o_ref):
        acc = x_ref[0].astype(F32)
        for j in range(1, n):
            acc = acc + x_ref[j].astype(F32)
        o_ref[...] = acc

    return _pcall(
        body, name=name, grid=(r // tr,), in_specs=[pl.BlockSpec((n, tr, c), lambda i: (0, i, 0))],
        out_specs=pl.BlockSpec((tr, c), lambda i: (i, 0)), out_shape=jax.ShapeDtypeStruct((r, c), F32),
        compiler_params=_params(("parallel",)),
    )(rx)


def _adamw(w, g, m, v, name):
    r, c = w.shape
    tr = _pick(r, max(8, ELEMENTWISE_BLOCK_BYTES // (4 * c)), 8)
    c1 = 1.0 - ADAM_B1 ** ADAM_STEP
    c2 = 1.0 - ADAM_B2 ** ADAM_STEP

    def body(w_ref, g_ref, m_ref, v_ref, d_ref, nm_ref, nv_ref):
        gg = g_ref[...]
        nm = ADAM_B1 * m_ref[...] + (1.0 - ADAM_B1) * gg
        nv = ADAM_B2 * v_ref[...] + (1.0 - ADAM_B2) * (gg * gg)
        d_ref[...] = -ADAM_LR * ((nm / c1) / (jnp.sqrt(nv / c2) + ADAM_EPS) + ADAM_WD * w_ref[...])
        nm_ref[...] = nm
        nv_ref[...] = nv

    spec = pl.BlockSpec((tr, c), lambda i: (i, 0))
    shp = jax.ShapeDtypeStruct((r, c), F32)
    return _pcall(
        body, name=name, grid=(r // tr,), in_specs=[spec] * 4, out_specs=[spec] * 3, out_shape=[shp] * 3,
        compiler_params=_params(("parallel",)),
    )(w, g, m, v)


def _pack(parts):
    flat, layout, row = [], [], 0
    for p in parts:
        n = p.size
        rows = -(-n // LANES)
        flat.append(jnp.pad(p.reshape(-1).astype(F32), (0, rows * LANES - n)))
        layout.append((row, n, p.shape))
        row += rows
    total = -(-row // 8) * 8
    if total > row:
        flat.append(jnp.zeros(((total - row) * LANES,), F32))
    return jnp.concatenate(flat).reshape(total, LANES), layout


def _unpack(slab, layout):
    flat = slab.reshape(-1)
    return [flat[row * LANES:row * LANES + n].reshape(shape) for row, n, shape in layout]


def kernel(x, meta_tokens, mix_norm_g, w_in, b_in, attn_sinks, conv_w, conv_b, conv_ln_g, conv_ln_b, w_attn_o, w_conv_o, b_conv_o, w_out, ffn_norm_g, w_gate_up, w_down, final_norm_g, loss_target, m_meta_tokens, m_mix_norm_g, m_w_in, m_b_in, m_attn_sinks, m_conv_w, m_conv_b, m_conv_ln_g, m_conv_ln_b, m_w_attn_o, m_w_conv_o, m_b_conv_o, m_w_out, m_ffn_norm_g, m_w_gate_up, m_w_down, m_final_norm_g, v_meta_tokens, v_mix_norm_g, v_w_in, v_b_in, v_attn_sinks, v_conv_w, v_conv_b, v_conv_ln_g, v_conv_ln_b, v_w_attn_o, v_w_conv_o, v_b_conv_o, v_w_out, v_ffn_norm_g, v_w_gate_up, v_w_down, v_final_norm_g):
    xs = x[0]
    tgt = loss_target[0]
    s, d = xs.shape
    lp = s + BLOCK
    cd = conv_b.shape[1]
    ffn = w_down.shape[1] * N_DEV
    dev = 4 * lax.axis_index("x") + 2 * lax.axis_index("y") + lax.axis_index("c")
    cw_cols = conv_w.shape[3]
    meta_cols = meta_tokens.shape[1]

    small, small_layout = _pack([meta_tokens, jnp.pad(conv_w[0, :, 0, :], ((0, CONV_ROWS - CONV_WIDTH), (0, 0)))])
    small_flat = _all_gather_rows(small, "gather_small")
    small_all = small_flat.reshape(N_DEV, *small.shape)
    meta_parts, cw_parts = zip(*[_unpack(small_all[j], small_layout) for j in range(N_DEV)])
    meta_full = jnp.concatenate(meta_parts, axis=1)
    conv_w_full = jnp.concatenate(cw_parts, axis=1)
    shards = ((w_in[0].T, "w_in"), (w_attn_o[0].T, "w_attn_o"), (w_conv_o[0].T, "w_conv_o"), (w_out[0], "w_out"),
              (w_gate_up[0].T, "w_gate_up"), (w_down[0], "w_down"))
    first = _rows_start(_place_rows(shards[0][0], small_flat, "place_w_in"), _plan_neighbours, "gather_start_w_in")
    placed, tok = [], first[2]
    for shard, name in shards[1:]:
        tok = _place_rows(shard, tok, "place_" + name)
        placed.append(tok)
    relay = _rows_start(_rows_wait(first, tok, "gather_wait_w_in"), _plan_relay, "gather_relay_start_w_in")
    h0, u = _prep(xs, meta_full, mix_norm_g, after=relay[2])
    diagonal = _rows_start(_rows_wait(relay, u, "gather_relay_wait_w_in"), _plan_pass_on_diagonal, "gather_diagonal_start_w_in")
    started, tok = [None], diagonal[2]
    for full, (_, name) in zip(placed, shards[1:]):
        started.append(_rows_start(full, _plan_direct, "gather_start_" + name, after=tok))
        tok = started[-1][2]
    win_t = _rows_wait(diagonal, tok, "gather_diagonal_wait_w_in")

    def arrived(w, after, name):
        return _rows_start(_rows_wait(started[w], after, "gather_wait_" + name), _plan_pass_on, "gather_pass_on_start_" + name)

    def whole(passing, after, name):
        return _rows_wait(passing, after, "gather_pass_on_wait_" + name)

    ctab, stab = _rope_tables(lp)
    mm = functools.partial(_matmul, tm=1056, tn=1024)

    bq, bkv, bc, bg = b_in[:, :Q_DIM], b_in[:, Q_DIM:Q_DIM + 2 * KV_DIM], b_in[:, Q_DIM + 2 * KV_DIM:Q_DIM + 2 * KV_DIM + 2 * cd], b_in[:, Q_DIM + 2 * KV_DIM + 2 * cd:]
    o_kv, o_c, o_g = Q_DIM, Q_DIM + 2 * KV_DIM, Q_DIM + 2 * KV_DIM + 2 * cd
    in_proj = functools.partial(_matmul, u, win_t, mode="nt", out_dtype=BF16, tm=2112, tn=512, tk=d)
    zq = in_proj(name="in_proj_q", bias=bq, b_row_off=0, b_rows=Q_DIM)
    zkv = in_proj(name="in_proj_kv", bias=bkv, b_row_off=o_kv, b_rows=2 * KV_DIM)
    zc = in_proj(name="in_proj_conv", bias=bc, b_row_off=o_c, b_rows=2 * cd)
    zg = in_proj(name="in_proj_gates", bias=bg, b_row_off=o_g, b_rows=2 * d)
    passing = arrived(1, zg, "w_attn_o")
    q_rot, k_sh, v_sh = _rope_fwd(zq, zkv, ctab, stab, after=passing[2])
    o = _attn_fwd(q_rot, k_sh, v_sh, attn_sinks)
    wao_t = whole(passing, o, "w_attn_o")
    br_a = mm(o, wao_t, mode="nt", name="attn_out_proj", out_dtype=BF16, tk=Q_DIM)
    passing = arrived(2, br_a, "w_conv_o")
    conv_out, c2 = _conv_fwd(zc, conv_w_full, conv_b, conv_ln_g, conv_ln_b, after=passing[2])
    wco_t = whole(passing, c2, "w_conv_o")
    br_b = mm(c2, wco_t, mode="nt", name="conv_out_proj", out_dtype=BF16, tk=cd, bias=b_conv_o)
    passing = arrived(3, br_b, "w_out")
    merged = _gate_fwd(br_a, br_b, zg, after=passing[2])
    wout = whole(passing, merged, "w_out")
    passing = arrived(4, wout, "w_gate_up")
    h1 = mm(merged, wout, mode="nn", name="mix_out_proj", out_dtype=F32, tn=512, tk=d, residual=h0, after=passing[2])
    u2 = _rmsnorm_fwd(h1, ffn_norm_g, "ffn_rmsnorm")
    wgu_t = whole(passing, u2, "w_gate_up")
    gu = _matmul(u2, wgu_t, mode="nt", name="ffn_gate_up", out_dtype=BF16, tm=2112, tn=512, tk=d)
    passing = arrived(5, gu, "w_down")
    act = _swiglu_fwd(gu, after=passing[2])
    wdown = whole(passing, act, "w_down")
    h2 = mm(act, wdown, mode="nn", name="ffn_down", out_dtype=F32, tn=512, tk=ffn // 2, residual=h1)
    dh2, dh2_b, loss_part, d_final_g = _final(h2, tgt, final_norm_g.reshape(1, d))

    wgrad = functools.partial(_matmul, mode="tn", out_dtype=BF16, tk=lp, tn=2048, b_inner=False)
    in_flight = {}

    def scatter_begin(g, name):
        return _pair_exchange_start(g, "rs_" + name + "_pair_start")

    def scatter_go_on(pair, after, name):
        g, theirs = _pair_exchange_wait(pair[0], pair[1], pair[2], pair[3], after, "rs_" + name + "_pair_wait")
        ps = _pair_sum(g, theirs, "rs_" + name + "_pair_sum")
        in_flight[name] = _chip_exchange_start(ps, theirs, "rs_" + name + "_chip_start")
        return in_flight[name][2]

    g_wdown = wgrad(act, dh2_b, name="ffn_down_dw", tm=256)
    pair = scatter_begin(g_wdown, "w_down")
    dact = _matmul(dh2_b, wdown, mode="nt", name="ffn_down_dx", out_dtype=BF16, tm=2112, tn=256, tk=d, after=pair[2])
    tok = scatter_go_on(pair, dact, "w_down")
    dgu = _swiglu_bwd(dact, gu)
    g_wgu_t = wgrad(dgu, u2, name="ffn_gate_up_dw", tm=512, after=tok)
    pair = scatter_begin(g_wgu_t, "w_gate_up")
    du2 = mm(dgu, wgu_t, mode="nn", name="ffn_gate_up_dx", out_dtype=F32, tn=512, tk=ffn // 2, after=pair[2])
    tok = scatter_go_on(pair, du2, "w_gate_up")
    dh1, dh1_b, d_ffn_g = _rmsnorm_bwd(du2, h1, ffn_norm_g, dh2, "ffn_rmsnorm_bwd")
    g_wout = wgrad(merged, dh1_b, name="mix_out_dw", tm=512, after=tok)
    pair = scatter_begin(g_wout, "w_out")
    dmerged = mm(dh1_b, wout, mode="nt", name="mix_out_dx", out_dtype=BF16, tk=d, after=pair[2])
    tok = scatter_go_on(pair, dmerged, "w_out")
    d_a, d_b, dz_g, sum_g, d_bco = _gate_bwd(dmerged, br_a, br_b, zg)
    g_wao_t = wgrad(d_a, o, name="attn_out_dw", tm=512, after=tok)
    pair = scatter_begin(g_wao_t, "w_attn_o")
    do = mm(d_a, wao_t, mode="nn", name="attn_out_dx", out_dtype=BF16, tk=d, after=pair[2])
    tok = scatter_go_on(pair, do, "w_attn_o")
    g_wco_t = wgrad(d_b, c2, name="conv_out_dw", tm=512, after=tok)
    pair = scatter_begin(g_wco_t, "w_conv_o")
    dc2 = mm(d_b, wco_t, mode="nn", name="conv_out_dx", out_dtype=F32, tk=d, after=pair[2])
    tok = scatter_go_on(pair, dc2, "w_conv_o")
    dq, dk, dv, dkm, dvm, d_sinks = _attn_bwd(q_rot, k_sh, v_sh, attn_sinks, do)
    dz_qkv, sum_qkv = _rope_bwd(dq, dk, dv, dkm, dvm, ctab, stab)
    dco, d_ln_g, d_ln_b, d_conv_b = _conv_bwd_norm(dc2, conv_out, conv_ln_g, conv_ln_b)
    dz_c, sum_c, d_conv_w = _conv_bwd_taps(dco, zc, conv_w_full)
    dz = jnp.concatenate([dz_qkv, dz_c, dz_g], axis=1)
    d_b_in = jnp.concatenate([sum_qkv, sum_c, sum_g], axis=1)
    in_dim = dz.shape[1]
    g_win_t = wgrad(dz, u, name="in_proj_dw", tm=512, after=tok)
    theirs = _pair_exchange(g_win_t, "rs_w_in_pair_exchange")
    in_flight["w_in"] = _chip_exchange_start(_pair_sum(g_win_t, theirs, "rs_w_in_pair_sum"), theirs, "rs_w_in_chip_start")
    du = mm(dz, win_t, mode="nn", name="in_proj_dx", out_dtype=F32, tk=in_dim // 4, after=in_flight["w_in"][2])
    grad_x, d_meta, d_mix_g = _rmsnorm_bwd_first(du, h0, mix_norm_g, dh1)

    weights = dict(meta_tokens=meta_tokens, mix_norm_g=mix_norm_g, w_in=w_in, b_in=b_in, attn_sinks=attn_sinks, conv_w=conv_w,
                   conv_b=conv_b, conv_ln_g=conv_ln_g, conv_ln_b=conv_ln_b, w_attn_o=w_attn_o, w_conv_o=w_conv_o, b_conv_o=b_conv_o,
                   w_out=w_out, ffn_norm_g=ffn_norm_g, w_gate_up=w_gate_up, w_down=w_down, final_norm_g=final_norm_g)
    m_in = dict(meta_tokens=m_meta_tokens, mix_norm_g=m_mix_norm_g, w_in=m_w_in, b_in=m_b_in, attn_sinks=m_attn_sinks, conv_w=m_conv_w,
                conv_b=m_conv_b, conv_ln_g=m_conv_ln_g, conv_ln_b=m_conv_ln_b, w_attn_o=m_w_attn_o, w_conv_o=m_w_conv_o,
                b_conv_o=m_b_conv_o, w_out=m_w_out, ffn_norm_g=m_ffn_norm_g, w_gate_up=m_w_gate_up, w_down=m_w_down,
                final_norm_g=m_final_norm_g)
    v_in = dict(meta_tokens=v_meta_tokens, mix_norm_g=v_mix_norm_g, w_in=v_w_in, b_in=v_b_in, attn_sinks=v_attn_sinks, conv_w=v_conv_w,
                conv_b=v_conv_b, conv_ln_g=v_conv_ln_g, conv_ln_b=v_conv_ln_b, w_attn_o=v_w_attn_o, w_conv_o=v_w_conv_o,
                b_conv_o=v_b_conv_o, w_out=v_w_out, ffn_norm_g=v_ffn_norm_g, w_gate_up=v_w_gate_up, w_down=v_w_down,
                final_norm_g=v_final_norm_g)
    names = list(weights)
    grads, delta, new_m, new_v = {}, {}, {}, {}
    tok = grad_x
    for n in ("w_down", "w_gate_up", "w_out", "w_attn_o", "w_conv_o", "w_in"):
        send_sem, recv_sem, ps, rx = in_flight[n]
        ps, rx = _chip_exchange_wait(send_sem, recv_sem, ps, rx, tok, "rs_" + n + "_chip_wait")
        g = _sum_chips(ps, rx, "rs_" + n + "_sum")
        if n in ("w_attn_o", "w_conv_o"):
            g = g.T
        oriented = (lambda a: a[0].T) if n in ("w_in", "w_gate_up") else (lambda a: a[0])
        back = (lambda a: a.T[None]) if n in ("w_in", "w_gate_up") else (lambda a: a[None])
        dl, nm, nv = _adamw(oriented(weights[n]), g, oriented(m_in[n]), oriented(v_in[n]), "adamw_" + n)
        grads[n], delta[n], new_m[n], new_v[n] = back(g), back(dl), back(nm), back(nv)
        tok = dl

    slab, slab_layout = _pack([loss_part[:, :1], d_mix_g, d_b_in, d_sinks[:, :N_Q_HEADS], d_conv_b, d_ln_g, d_ln_b, d_bco,
                               d_ffn_g, d_final_g, d_conv_w, d_meta])
    slab_all = _all_gather_rows(slab, "gather_small_grads", after=tok).reshape(N_DEV, *slab.shape)
    (loss, g_mix_g, g_b_in, g_sinks, g_conv_b, g_ln_g, g_ln_b, g_bco, g_ffn_g, g_final_g, g_conv_w_full, g_meta_full
     ) = _unpack(_sum_blocks(slab_all, "sum_small_grads"), slab_layout)
    g_conv_w = lax.dynamic_slice(g_conv_w_full, (0, dev * cw_cols), (CONV_WIDTH, cw_cols)).reshape(conv_w.shape)
    g_meta = lax.dynamic_slice(g_meta_full, (0, dev * meta_cols), (N_META, meta_cols))
    g_final_g = g_final_g.reshape(final_norm_g.shape)
    grads.update(meta_tokens=g_meta, mix_norm_g=g_mix_g, b_in=g_b_in, attn_sinks=g_sinks, conv_w=g_conv_w, conv_b=g_conv_b,
                 conv_ln_g=g_ln_g, conv_ln_b=g_ln_b, b_conv_o=g_bco, ffn_norm_g=g_ffn_g, final_norm_g=g_final_g)
    rest = [n for n in names if n not in delta]
    w_slab, rest_layout = _pack([weights[n] for n in rest])
    g_slab, _ = _pack([grads[n] for n in rest])
    m_slab, _ = _pack([m_in[n] for n in rest])
    v_slab, _ = _pack([v_in[n] for n in rest])
    dl, nm, nv = _adamw(w_slab, g_slab, m_slab, v_slab, "adamw_small")
    for n, a, b, c in zip(rest, _unpack(dl, rest_layout), _unpack(nm, rest_layout), _unpack(nv, rest_layout)):
        delta[n], new_m[n], new_v[n] = a, b, c

    return (loss.reshape(()), grad_x[None], *[grads[n] for n in names], *[delta[n] for n in names],
            *[new_m[n] for n in names], *[new_v[n] for n in names])
```

```python
import functools
import math

import jax
import jax.numpy as jnp
from jax import lax
from jax.experimental import pallas as pl
from jax.experimental.pallas import tpu as pltpu

F32 = jnp.float32
BF16 = jnp.bfloat16

N_DEV = 8
BLOCK = 128
N_META = 16
PAD_ROWS = BLOCK - N_META
HEAD_DIM = 64
N_Q_HEADS = 32
N_KV_HEADS = 4
GROUP = N_Q_HEADS // N_KV_HEADS
Q_DIM = N_Q_HEADS * HEAD_DIM
KV_DIM = N_KV_HEADS * HEAD_DIM
WINDOW = 128
CONV_WIDTH = 31
CONV_ROWS = 32
ROPE_THETA = 10000.0
EPS = 1e-6
ATTN_SCALE = HEAD_DIM ** -0.5
NEG = -1e30

ADAM_LR = 0.001
ADAM_B1 = 0.9
ADAM_B2 = 0.999
ADAM_EPS = 1e-08
ADAM_WD = 0.01
ADAM_STEP = 10

VMEM_LIMIT_BYTES = 56 * 1024 * 1024
LANES = 128
ELEMENTWISE_BLOCK_BYTES = 2 * 1024 * 1024
MESH = pl.DeviceIdType.MESH
CHIPS = ((0, 0), (0, 1), (1, 0), (1, 1))


def _pcall(body, after=None, **kw):
    if after is None:
        return pl.pallas_call(body, **kw)
    in_specs = list(kw.pop("in_specs"))
    n_in = len(in_specs)

    def ordered_body(*refs):
        return body(*refs[:n_in], *refs[n_in + 1:])

    call = pl.pallas_call(ordered_body, in_specs=in_specs + [pl.BlockSpec(memory_space=pl.ANY)], **kw)
    return lambda *args: call(*args, after)


def _params(semantics=None):
    if semantics is None:
        return pltpu.CompilerParams(vmem_limit_bytes=VMEM_LIMIT_BYTES)
    return pltpu.CompilerParams(dimension_semantics=semantics, vmem_limit_bytes=VMEM_LIMIT_BYTES)


def _pick(dim, pref, align):
    best = None
    t = align
    while t <= min(dim, pref):
        if dim % t == 0:
            best = t
        t += align
    return dim if best is None else best


def _sigmoid(x):
    return 1.0 / (1.0 + jnp.exp(-x))


def _matmul(a, b, *, mode, name, out_dtype, tm, tn, tk, bias=None, residual=None, b_inner=True,
            b_row_off=0, b_rows=None, after=None):
    if mode == "nn":
        m, k = a.shape
        n = b.shape[1]
    elif mode == "nt":
        m, k = a.shape
        n = b.shape[0] if b_rows is None else b_rows
    else:
        k, m = a.shape
        n = b.shape[1]
    tm = _pick(m, tm, 16)
    tn = _pick(math.gcd(n, b_row_off) if mode == "nt" and b_row_off else n, tn, LANES)
    tk = _pick(k, tk, LANES if mode != "tn" else 16)
    nm, nn, nk = m // tm, n // tn, k // tk
    if mode == "nt":
        assert b_row_off % tn == 0
    off = b_row_off // tn if mode == "nt" else 0

    if b_inner:
        grid = (nm, nn, nk)
        ij = lambda g0, g1: (g0, g1)
    else:
        grid = (nn, nm, nk)
        ij = lambda g0, g1: (g1, g0)

    if mode == "tn":
        a_spec = pl.BlockSpec((tk, tm), lambda g0, g1, kk: (kk, ij(g0, g1)[0]))
    else:
        a_spec = pl.BlockSpec((tm, tk), lambda g0, g1, kk: (ij(g0, g1)[0], kk))
    if mode == "nt":
        b_spec = pl.BlockSpec((tn, tk), lambda g0, g1, kk: (ij(g0, g1)[1] + off, kk))
    else:
        b_spec = pl.BlockSpec((tk, tn), lambda g0, g1, kk: (kk, ij(g0, g1)[1]))
    o_spec = pl.BlockSpec((tm, tn), lambda g0, g1, kk: ij(g0, g1))
    in_specs = [a_spec, b_spec]
    args = [a, b]
    if bias is not None:
        in_specs.append(pl.BlockSpec((1, tn), lambda g0, g1, kk: (0, ij(g0, g1)[1])))
        args.append(bias)
    if residual is not None:
        in_specs.append(o_spec)
        args.append(residual)
    dims = {"nn": (((1,), (0,)), ((), ())), "nt": (((1,), (1,)), ((), ())), "tn": (((0,), (0,)), ((), ()))}[mode]
    has_bias, has_res = bias is not None, residual is not None

    def body(*refs):
        a_ref, b_ref = refs[0], refs[1]
        pos = 2
        bias_ref = res_ref = None
        if has_bias:
            bias_ref = refs[pos]
            pos += 1
        if has_res:
            res_ref = refs[pos]
            pos += 1
        o_ref = refs[pos]
        acc_ref = refs[pos + 1] if nk > 1 else None

        def finish(acc):
            if has_bias:
                acc = acc + bias_ref[...]
            if has_res:
                acc = acc + res_ref[...]
            o_ref[...] = acc.astype(out_dtype)

        p = lax.dot_general(a_ref[...], b_ref[...], dims, preferred_element_type=F32)
        if nk == 1:
            finish(p)
        else:
            kk = pl.program_id(2)

            @pl.when(kk == 0)
            def _():
                acc_ref[...] = p

            @pl.when(kk > 0)
            def _():
                acc_ref[...] += p

            @pl.when(kk == nk - 1)
            def _():
                finish(acc_ref[...])

    return _pcall(
        body, after=after, name=name, grid=grid, in_specs=in_specs, out_specs=o_spec,
        out_shape=jax.ShapeDtypeStruct((m, n), out_dtype),
        scratch_shapes=[pltpu.VMEM((tm, tn), F32)] if nk > 1 else [],
        compiler_params=_params(("parallel", "parallel", "arbitrary")),
    )(*args)


def _row_spec(width, col=0):
    return pl.BlockSpec((BLOCK, width), lambda i: (i, col))


def _const_spec(shape):
    nd = len(shape)
    return pl.BlockSpec(shape, lambda i: (0,) * nd)


def _prep(x, meta_full, g, after=None):
    s, d = x.shape
    lp = s + BLOCK
    nb = lp // BLOCK

    def body(x_ref, meta_ref, g_ref, h_ref, u_ref):
        i = pl.program_id(0)

        @pl.when(i == 0)
        def _():
            h_ref[0:PAD_ROWS, :] = jnp.zeros((PAD_ROWS, d), F32)
            h_ref[PAD_ROWS:BLOCK, :] = meta_ref[...]

        @pl.when(i > 0)
        def _():
            h_ref[...] = x_ref[...]

        h = h_ref[...]
        r = lax.rsqrt(jnp.mean(h * h, axis=-1, keepdims=True) + EPS)
        u_ref[...] = (h * r * g_ref[...]).astype(BF16)

    return _pcall(
        body, after=after, name="prep_rmsnorm", grid=(nb,),
        in_specs=[pl.BlockSpec((BLOCK, d), lambda i: (jnp.maximum(i - 1, 0), 0)), _const_spec((N_META, d)), _const_spec((1, d))],
        out_specs=[_row_spec(d), _row_spec(d)],
        out_shape=[jax.ShapeDtypeStruct((lp, d), F32), jax.ShapeDtypeStruct((lp, d), BF16)],
        compiler_params=_params(("arbitrary",)),
    )(x, meta_full, g)


def _rmsnorm_fwd(h, g, name):
    lp, d = h.shape

    def body(h_ref, g_ref, u_ref):
        x = h_ref[...]
        r = lax.rsqrt(jnp.mean(x * x, axis=-1, keepdims=True) + EPS)
        u_ref[...] = (x * r * g_ref[...]).astype(BF16)

    return _pcall(
        body, name=name, grid=(lp // BLOCK,), in_specs=[_row_spec(d), _const_spec((1, d))], out_specs=_row_spec(d),
        out_shape=jax.ShapeDtypeStruct((lp, d), BF16), compiler_params=_params(("parallel",)),
    )(h, g)


def _rms_bwd_core(dy, x, g):
    r = lax.rsqrt(jnp.mean(x * x, axis=-1, keepdims=True) + EPS)
    xhat = x * r
    dxhat = dy * g
    dx = r * (dxhat - xhat * jnp.mean(dxhat * xhat, axis=-1, keepdims=True))
    return dx, jnp.sum(dy * xhat, axis=0, keepdims=True)


def _rmsnorm_bwd(dy, h, g, dres, name):
    lp, d = h.shape

    def body(dy_ref, h_ref, g_ref, dres_ref, dh_ref, dhb_ref, dg_ref):
        i = pl.program_id(0)
        dx, dg = _rms_bwd_core(dy_ref[...], h_ref[...], g_ref[...])
        dh = dres_ref[...] + dx
        dh_ref[...] = dh
        dhb_ref[...] = dh.astype(BF16)

        @pl.when(i == 0)
        def _():
            dg_ref[...] = jnp.zeros_like(dg_ref)

        dg_ref[...] += dg

    return _pcall(
        body, name=name, grid=(lp // BLOCK,),
        in_specs=[_row_spec(d), _row_spec(d), _const_spec((1, d)), _row_spec(d)],
        out_specs=[_row_spec(d), _row_spec(d), _const_spec((1, d))],
        out_shape=[jax.ShapeDtypeStruct((lp, d), F32), jax.ShapeDtypeStruct((lp, d), BF16), jax.ShapeDtypeStruct((1, d), F32)],
        compiler_params=_params(("arbitrary",)),
    )(dy, h, g, dres)


def _rmsnorm_bwd_first(dy, h, g, dres):
    lp, d = h.shape
    s = lp - BLOCK

    def body(dy_ref, h_ref, g_ref, dres_ref, gx_ref, dmeta_ref, dg_ref):
        i = pl.program_id(0)
        dx, dg = _rms_bwd_core(dy_ref[...], h_ref[...], g_ref[...])
        dh = dres_ref[...] + dx
        gx_ref[...] = dh

        @pl.when(i == 0)
        def _():
            dmeta_ref[...] = dh[PAD_ROWS:BLOCK, :]
            dg_ref[...] = jnp.zeros_like(dg_ref)

        dg_ref[...] += dg

    return _pcall(
        body, name="rmsnorm_bwd_first", grid=(lp // BLOCK,),
        in_specs=[_row_spec(d), _row_spec(d), _const_spec((1, d)), _row_spec(d)],
        out_specs=[pl.BlockSpec((BLOCK, d), lambda i: (jnp.maximum(i - 1, 0), 0)), _const_spec((N_META, d)), _const_spec((1, d))],
        out_shape=[jax.ShapeDtypeStruct((s, d), F32), jax.ShapeDtypeStruct((N_META, d), F32), jax.ShapeDtypeStruct((1, d), F32)],
        compiler_params=_params(("arbitrary",)),
    )(dy, h, g, dres)


def _final(h2, tgt, g):
    lp, d = h2.shape

    def body(h_ref, t_ref, g_ref, dh_ref, dhb_ref, loss_ref, dg_ref):
        i = pl.program_id(0)
        x = h_ref[...]
        gg = g_ref[...]
        r = lax.rsqrt(jnp.mean(x * x, axis=-1, keepdims=True) + EPS)
        xhat = x * r
        y = xhat * gg
        live = (i > 0).astype(F32)
        err = (y - t_ref[...]) * live
        dy = err * (1.0 / d)
        dxhat = dy * gg
        dh = r * (dxhat - xhat * jnp.mean(dxhat * xhat, axis=-1, keepdims=True))
        dh_ref[...] = dh
        dhb_ref[...] = dh.astype(BF16)

        @pl.when(i == 0)
        def _():
            loss_ref[...] = jnp.zeros_like(loss_ref)
            dg_ref[...] = jnp.zeros_like(dg_ref)

        row_loss = jnp.mean(err * err, axis=-1, keepdims=True)
        loss_ref[...] += 0.5 * jnp.sum(row_loss, axis=0, keepdims=True)
        dg_ref[...] += jnp.sum(dy * xhat, axis=0, keepdims=True)

    return _pcall(
        body, name="final_norm_loss", grid=(lp // BLOCK,),
        in_specs=[_row_spec(d), pl.BlockSpec((BLOCK, d), lambda i: (jnp.maximum(i - 1, 0), 0)), _const_spec((1, d))],
        out_specs=[_row_spec(d), _row_spec(d), _const_spec((1, LANES)), _const_spec((1, d))],
        out_shape=[jax.ShapeDtypeStruct((lp, d), F32), jax.ShapeDtypeStruct((lp, d), BF16),
                   jax.ShapeDtypeStruct((1, LANES), F32), jax.ShapeDtypeStruct((1, d), F32)],
        compiler_params=_params(("arbitrary",)),
    )(h2, tgt, g)


def _swap_halves(x):
    w = x.shape[1]
    lane = lax.broadcasted_iota(jnp.int32, x.shape, 1)
    first = (lane & (HEAD_DIM - 1)) < (HEAD_DIM // 2)
    return jnp.where(first, pltpu.roll(x, w - HEAD_DIM // 2, 1), pltpu.roll(x, HEAD_DIM // 2, 1))


def _rope_tables(lp):
    pos = jnp.maximum(jnp.arange(lp, dtype=jnp.int32) - PAD_ROWS, 0).astype(F32)
    inv_freq = ROPE_THETA ** (-jnp.arange(0, HEAD_DIM, 2, dtype=F32) / HEAD_DIM)
    ang = pos[:, None] * inv_freq[None, :]
    c, s = jnp.cos(ang), jnp.sin(ang)
    reps = LANES // HEAD_DIM
    return jnp.tile(jnp.concatenate([c, c], axis=1), (1, reps)), jnp.tile(jnp.concatenate([-s, s], axis=1), (1, reps))


def _rope_fwd(zq, zkv, ctab, stab, after=None):
    lp = zq.shape[0]
    nb = lp // BLOCK
    back = lambda s: (jnp.maximum(s - 1, 0), 0)

    def body(zq_ref, zkv_ref, c_ref, s_ref, q_ref, k_ref, v_ref):
        step = pl.program_id(0)
        c128, s128 = c_ref[...], s_ref[...]

        def rope(x):
            reps = x.shape[1] // LANES
            return x * jnp.tile(c128, (1, reps)) + _swap_halves(x) * jnp.tile(s128, (1, reps))

        q_ref[...] = (rope(zq_ref[...].astype(F32)) * ATTN_SCALE).astype(BF16)
        kv = zkv_ref[...].astype(F32)
        k = rope(kv[:, :KV_DIM])
        v = kv[:, KV_DIM:]

        @pl.when(step == 0)
        def _():
            k_ref[...] = jnp.zeros_like(k_ref)
            v_ref[...] = jnp.zeros_like(v_ref)

        @pl.when(step > 0)
        def _():
            for h in range(N_KV_HEADS):
                k_ref[h] = k[:, h * HEAD_DIM:(h + 1) * HEAD_DIM].astype(BF16)
                v_ref[h] = v[:, h * HEAD_DIM:(h + 1) * HEAD_DIM].astype(BF16)

    kv_spec = pl.BlockSpec((N_KV_HEADS, BLOCK, HEAD_DIM), lambda s: (0, s, 0))
    return _pcall(
        body, after=after, name="rope_fwd", grid=(nb + 1,),
        in_specs=[pl.BlockSpec((BLOCK, Q_DIM), back), pl.BlockSpec((BLOCK, 2 * KV_DIM), back),
                  pl.BlockSpec((BLOCK, LANES), back), pl.BlockSpec((BLOCK, LANES), back)],
        out_specs=[pl.BlockSpec((BLOCK, Q_DIM), back), kv_spec, kv_spec],
        out_shape=[jax.ShapeDtypeStruct((lp, Q_DIM), BF16),
                   jax.ShapeDtypeStruct((N_KV_HEADS, lp + BLOCK, HEAD_DIM), BF16),
                   jax.ShapeDtypeStruct((N_KV_HEADS, lp + BLOCK, HEAD_DIM), BF16)],
        compiler_params=_params(("arbitrary",)),
    )(zq, zkv, ctab, stab)


def _rope_bwd(dq, dk, dv, dkm, dvm, ctab, stab):
    lp = dq.shape[0]
    width = Q_DIM + 2 * KV_DIM
    head_spec = pl.BlockSpec((N_KV_HEADS, BLOCK, HEAD_DIM), lambda i: (0, i, 0))
    meta_spec = _const_spec((N_KV_HEADS, BLOCK, HEAD_DIM))

    def body(dq_ref, dk_ref, dv_ref, dkm_ref, dvm_ref, c_ref, s_ref, dz_ref, sum_ref, kbuf, vbuf):
        i = pl.program_id(0)
        c128, s128 = c_ref[...], s_ref[...]
        first = (i == 0).astype(F32)

        def rope_t(x):
            reps = x.shape[1] // LANES
            return x * jnp.tile(c128, (1, reps)) + _swap_halves(x * jnp.tile(s128, (1, reps)))

        for h in range(N_KV_HEADS):
            kbuf[:, h * HEAD_DIM:(h + 1) * HEAD_DIM] = dk_ref[h] + first * dkm_ref[h]
            vbuf[:, h * HEAD_DIM:(h + 1) * HEAD_DIM] = dv_ref[h] + first * dvm_ref[h]
        dzq = rope_t(dq_ref[...] * ATTN_SCALE)
        dzk = rope_t(kbuf[...])
        dzv = vbuf[...]
        dz_ref[:, 0:Q_DIM] = dzq.astype(BF16)
        dz_ref[:, Q_DIM:Q_DIM + KV_DIM] = dzk.astype(BF16)
        dz_ref[:, Q_DIM + KV_DIM:width] = dzv.astype(BF16)

        @pl.when(i == 0)
        def _():
            sum_ref[...] = jnp.zeros_like(sum_ref)

        sum_ref[:, 0:Q_DIM] += jnp.sum(dzq, axis=0, keepdims=True)
        sum_ref[:, Q_DIM:Q_DIM + KV_DIM] += jnp.sum(dzk, axis=0, keepdims=True)
        sum_ref[:, Q_DIM + KV_DIM:width] += jnp.sum(dzv, axis=0, keepdims=True)

    return _pcall(
        body, name="rope_bwd", grid=(lp // BLOCK,),
        in_specs=[_row_spec(Q_DIM), head_spec, head_spec, meta_spec, meta_spec, _row_spec(LANES), _row_spec(LANES)],
        out_specs=[_row_spec(width), _const_spec((1, width))],
        out_shape=[jax.ShapeDtypeStruct((lp, width), BF16), jax.ShapeDtypeStruct((1, width), F32)],
        scratch_shapes=[pltpu.VMEM((BLOCK, KV_DIM), F32), pltpu.VMEM((BLOCK, KV_DIM), F32)],
        compiler_params=_params(("arbitrary",)),
    )(dq, dk, dv, dkm, dvm, ctab, stab)


def _attn_bias(i):
    r = lax.broadcasted_iota(jnp.int32, (BLOCK, 3 * BLOCK), 0)
    c = lax.broadcasted_iota(jnp.int32, (BLOCK, 3 * BLOCK), 1)
    qp = i * BLOCK + r - PAD_ROWS
    kp = (i - 1) * BLOCK + c - PAD_ROWS
    band = (c < 2 * BLOCK) & (kp >= N_META) & (kp <= qp) & (qp - kp < WINDOW)
    mp = c - 2 * BLOCK - PAD_ROWS
    meta = (c >= 2 * BLOCK) & (mp >= 0) & (mp <= qp)
    return jnp.where(band | meta, 0.0, NEG).astype(F32)


HALF = BLOCK // 2
HALF_KEYS = 2 * BLOCK


def _half_keys(prev, own, meta, half):
    if half == 0:
        return jnp.concatenate([prev, own[0:HALF], meta[HALF:BLOCK]], axis=0)
    return jnp.concatenate([prev[HALF:BLOCK], own, meta[HALF:BLOCK]], axis=0)


def _half_bias(i, half):
    r = lax.broadcasted_iota(jnp.int32, (HALF, HALF_KEYS), 0) + half * HALF
    c = lax.broadcasted_iota(jnp.int32, (HALF, HALF_KEYS), 1)
    n_prev = BLOCK - half * HALF
    qp = i * BLOCK + r - PAD_ROWS
    kp = jnp.where(c < n_prev, (i - 1) * BLOCK + c + half * HALF, i * BLOCK + c - n_prev) - PAD_ROWS
    band = (c < HALF_KEYS - HALF) & (kp >= N_META) & (kp <= qp) & (qp - kp < WINDOW)
    mp = c - (HALF_KEYS - HALF) + HALF - PAD_ROWS
    meta = (c >= HALF_KEYS - HALF) & (mp >= 0) & (mp <= qp)
    return jnp.where(band | meta, 0.0, NEG).astype(F32)


def _half_rows(ref, heads, half):
    rows = slice(half * HALF, (half + 1) * HALF)
    return jnp.concatenate([ref[rows, n * HEAD_DIM:(n + 1) * HEAD_DIM] for n in heads], axis=0)


def _half_sinks(sink_ref, heads):
    return jnp.concatenate([jnp.broadcast_to(sink_ref[0:1, n:n + 1], (HALF, 1)) for n in heads], axis=0)


def _stack_heads(ref, h):
    return jnp.concatenate(
        [ref[:, (h * GROUP + g) * HEAD_DIM:(h * GROUP + g + 1) * HEAD_DIM] for g in range(GROUP)], axis=0)


def _attn_probs(qs, k3, bias8, sink):
    s = lax.dot_general(qs, k3, (((1,), (1,)), ((), ())), preferred_element_type=F32) + bias8
    m = jnp.maximum(jnp.max(s, axis=1, keepdims=True), sink)
    p = jnp.exp(s - m)
    ps = jnp.exp(sink - m)
    inv = 1.0 / (jnp.sum(p, axis=1, keepdims=True) + ps)
    return p * inv, ps * inv


def _sink_column(sink_ref, h):
    return jnp.concatenate(
        [jnp.broadcast_to(sink_ref[0:1, h * GROUP + g:h * GROUP + g + 1], (BLOCK, 1)) for g in range(GROUP)], axis=0)


def _attn_fwd(q, k_sh, v_sh, sinks):
    lp = q.shape[0]
    nb = lp // BLOCK
    kv = lambda f: pl.BlockSpec((N_KV_HEADS, BLOCK, HEAD_DIM), f)

    def body(q_ref, kp_ref, kc_ref, km_ref, vp_ref, vc_ref, vm_ref, sink_ref, o_ref):
        i = pl.program_id(0)
        for half in range(2):
            bias = jnp.tile(_half_bias(i, half), (GROUP, 1))
            rows = slice(half * HALF, (half + 1) * HALF)
            for h in range(N_KV_HEADS):
                heads = range(h * GROUP, (h + 1) * GROUP)
                keys = _half_keys(kp_ref[h], kc_ref[h], km_ref[h], half)
                vals = _half_keys(vp_ref[h], vc_ref[h], vm_ref[h], half)
                p, _ = _attn_probs(_half_rows(q_ref, heads, half), keys, bias, _half_sinks(sink_ref, heads))
                o = jnp.dot(p.astype(BF16), vals, preferred_element_type=F32)
                for j, n in enumerate(heads):
                    o_ref[rows, n * HEAD_DIM:(n + 1) * HEAD_DIM] = o[j * HALF:(j + 1) * HALF].astype(BF16)

    prev, cur, meta = (lambda i: (0, i, 0)), (lambda i: (0, i + 1, 0)), (lambda i: (0, 1, 0))
    return _pcall(
        body, name="attn_fwd", grid=(nb,),
        in_specs=[_row_spec(Q_DIM), kv(prev), kv(cur), kv(meta), kv(prev), kv(cur), kv(meta), _const_spec((1, N_Q_HEADS))],
        out_specs=_row_spec(Q_DIM), out_shape=jax.ShapeDtypeStruct((lp, Q_DIM), BF16),
        compiler_params=_params(("parallel",)),
    )(q, k_sh, k_sh, k_sh, v_sh, v_sh, v_sh, sinks)


def _attn_bwd(q, k_sh, v_sh, sinks, do):
    lp = q.shape[0]
    nb = lp // BLOCK
    kv = lambda f: pl.BlockSpec((N_KV_HEADS, BLOCK, HEAD_DIM), f)
    cl = lambda s: jnp.minimum(s, nb - 1)

    def body(q_ref, do_ref, kp_ref, kc_ref, km_ref, vp_ref, vc_ref, vm_ref, sink_ref,
             dq_ref, dk_ref, dv_ref, dkm_ref, dvm_ref, dsink_ref, carry_k, carry_v):
        step = pl.program_id(0)

        @pl.when(step == 0)
        def _():
            carry_k[...] = jnp.zeros_like(carry_k)
            carry_v[...] = jnp.zeros_like(carry_v)
            dkm_ref[...] = jnp.zeros_like(dkm_ref)
            dvm_ref[...] = jnp.zeros_like(dvm_ref)
            dsink_ref[...] = jnp.zeros_like(dsink_ref)

        @pl.when(step < nb)
        def _():
            bias8 = jnp.tile(_attn_bias(step), (GROUP, 1))
            lane = lax.broadcasted_iota(jnp.int32, (1, LANES), 1)
            dsink = jnp.zeros((1, LANES), F32)
            for h in range(N_KV_HEADS):
                k3 = jnp.concatenate([kp_ref[h], kc_ref[h], km_ref[h]], axis=0)
                v3 = jnp.concatenate([vp_ref[h], vc_ref[h], vm_ref[h]], axis=0)
                qs = _stack_heads(q_ref, h)
                dos = _stack_heads(do_ref, h)
                p, psink = _attn_probs(qs, k3, bias8, _sink_column(sink_ref, h))
                dp = lax.dot_general(dos, v3, (((1,), (1,)), ((), ())), preferred_element_type=F32)
                delta = jnp.sum(p * dp, axis=1, keepdims=True)
                ds = (p * (dp - delta)).astype(BF16)
                dsk = -psink * delta
                for g in range(GROUP):
                    val = jnp.sum(dsk[g * BLOCK:(g + 1) * BLOCK], axis=0, keepdims=True)
                    dsink = dsink + jnp.where(lane == h * GROUP + g, val, 0.0)
                dqs = jnp.dot(ds, k3, preferred_element_type=F32)
                for g in range(GROUP):
                    n = h * GROUP + g
                    dq_ref[:, n * HEAD_DIM:(n + 1) * HEAD_DIM] = dqs[g * BLOCK:(g + 1) * BLOCK]
                dk3 = lax.dot_general(ds, qs, (((0,), (0,)), ((), ())), preferred_element_type=F32)
                dv3 = lax.dot_general(p.astype(BF16), dos, (((0,), (0,)), ((), ())), preferred_element_type=F32)
                dk_ref[h] = carry_k[h] + dk3[0:BLOCK]
                dv_ref[h] = carry_v[h] + dv3[0:BLOCK]
                carry_k[h] = dk3[BLOCK:2 * BLOCK]
                carry_v[h] = dv3[BLOCK:2 * BLOCK]
                dkm_ref[h] += dk3[2 * BLOCK:3 * BLOCK]
                dvm_ref[h] += dv3[2 * BLOCK:3 * BLOCK]
            dsink_ref[...] += dsink

        @pl.when(step == nb)
        def _():
            dk_ref[...] = carry_k[...]
            dv_ref[...] = carry_v[...]

    prev, cur, meta = (lambda s: (0, cl(s), 0)), (lambda s: (0, cl(s) + 1, 0)), (lambda s: (0, 1, 0))
    lag = lambda s: (0, jnp.maximum(s - 1, 0), 0)
    head_shape = jax.ShapeDtypeStruct((N_KV_HEADS, lp, HEAD_DIM), F32)
    meta_shape = jax.ShapeDtypeStruct((N_KV_HEADS, BLOCK, HEAD_DIM), F32)
    return _pcall(
        body, name="attn_bwd", grid=(nb + 1,),
        in_specs=[pl.BlockSpec((BLOCK, Q_DIM), lambda s: (cl(s), 0)), pl.BlockSpec((BLOCK, Q_DIM), lambda s: (cl(s), 0)),
                  kv(prev), kv(cur), kv(meta), kv(prev), kv(cur), kv(meta), _const_spec((1, N_Q_HEADS))],
        out_specs=[pl.BlockSpec((BLOCK, Q_DIM), lambda s: (cl(s), 0)), kv(lag), kv(lag),
                   _const_spec((N_KV_HEADS, BLOCK, HEAD_DIM)), _const_spec((N_KV_HEADS, BLOCK, HEAD_DIM)), _const_spec((1, LANES))],
        out_shape=[jax.ShapeDtypeStruct((lp, Q_DIM), F32), head_shape, head_shape, meta_shape, meta_shape,
                   jax.ShapeDtypeStruct((1, LANES), F32)],
        scratch_shapes=[pltpu.VMEM((N_KV_HEADS, BLOCK, HEAD_DIM), F32), pltpu.VMEM((N_KV_HEADS, BLOCK, HEAD_DIM), F32)],
        compiler_params=_params(("arbitrary",)),
    )(q, do, k_sh, k_sh, k_sh, v_sh, v_sh, v_sh, sinks)


CONV_CHUNK = 256


SUBLANES = 8
SH_BASE = BLOCK - 4 * SUBLANES
SH_ROWS = BLOCK + 3 * SUBLANES
DSH_ROWS = SH_ROWS


def _shifted_windows(src, sh, base, rows):
    for b in range(1, SUBLANES):
        sh[b] = src[base + b:base + b + rows, :]


def _window(src, sh, base, start, cols):
    a, b = divmod(start - base, SUBLANES)
    if b == 0:
        return src[start:start + BLOCK, cols]
    return sh[b, SUBLANES * a:SUBLANES * a + BLOCK, cols]


def _glu_masked(a_ref, g_ref, base):
    rows = base + lax.broadcasted_iota(jnp.int32, (BLOCK, 1), 0)
    return jnp.where(rows >= PAD_ROWS, a_ref[...].astype(F32) * _sigmoid(g_ref[...].astype(F32)), 0.0)


def _conv_fwd(zc, conv_w, conv_b, ln_g, ln_b, after=None):
    lp = zc.shape[0]
    cd = zc.shape[1] // 2
    nb = lp // BLOCK
    chunk = min(CONV_CHUNK, cd)
    back = lambda col: (lambda i: (jnp.maximum(i - 1, 0), col))
    lo = BLOCK - (CONV_WIDTH - 1)

    def body(ap_ref, gp_ref, ac_ref, gc_ref, w_ref, b_ref, lg_ref, lb_ref, co_ref, c2_ref, ext, sh):
        i = pl.program_id(0)
        ext[0:BLOCK, :] = _glu_masked(ap_ref, gp_ref, (i - 1) * BLOCK)
        ext[BLOCK:2 * BLOCK, :] = _glu_masked(ac_ref, gc_ref, i * BLOCK)
        _shifted_windows(ext, sh, SH_BASE, SH_ROWS)
        for c0 in range(0, cd, chunk):
            cols = slice(c0, c0 + chunk)
            acc = jnp.zeros((BLOCK, chunk), F32)
            for k in range(CONV_WIDTH):
                acc = acc + _window(ext, sh, SH_BASE, lo + k, cols) * w_ref[k:k + 1, cols]
            co_ref[:, cols] = acc + b_ref[:, cols]
        x = co_ref[...]
        mu = jnp.mean(x, axis=-1, keepdims=True)
        xc = x - mu
        r = lax.rsqrt(jnp.mean(xc * xc, axis=-1, keepdims=True) + EPS)
        y = xc * r * lg_ref[...] + lb_ref[...]
        c2_ref[...] = (y * _sigmoid(y)).astype(BF16)

    return _pcall(
        body, after=after, name="conv_fwd", grid=(nb,),
        in_specs=[pl.BlockSpec((BLOCK, cd), back(0)), pl.BlockSpec((BLOCK, cd), back(1)), _row_spec(cd, 0), _row_spec(cd, 1),
                  _const_spec((CONV_ROWS, cd)), _const_spec((1, cd)), _const_spec((1, cd)), _const_spec((1, cd))],
        out_specs=[_row_spec(cd), _row_spec(cd)],
        out_shape=[jax.ShapeDtypeStruct((lp, cd), F32), jax.ShapeDtypeStruct((lp, cd), BF16)],
        scratch_shapes=[pltpu.VMEM((2 * BLOCK, cd), F32), pltpu.VMEM((SUBLANES, SH_ROWS, cd), F32)],
        compiler_params=_params(("arbitrary",)),
    )(zc, zc, zc, zc, conv_w, conv_b, ln_g, ln_b)


def _conv_bwd_norm(dc2, conv_out, ln_g, ln_b):
    lp, cd = conv_out.shape

    def body(d_ref, x_ref, lg_ref, lb_ref, dco_ref, dlg_ref, dlb_ref, dcb_ref):
        i = pl.program_id(0)
        x = x_ref[...]
        g = lg_ref[...]
        mu = jnp.mean(x, axis=-1, keepdims=True)
        xc = x - mu
        r = lax.rsqrt(jnp.mean(xc * xc, axis=-1, keepdims=True) + EPS)
        xhat = xc * r
        y = xhat * g + lb_ref[...]
        sg = _sigmoid(y)
        dy = d_ref[...] * (sg * (1.0 + y * (1.0 - sg)))
        dxhat = dy * g
        dx = r * (dxhat - jnp.mean(dxhat, axis=-1, keepdims=True) - xhat * jnp.mean(dxhat * xhat, axis=-1, keepdims=True))
        dco_ref[...] = dx

        @pl.when(i == 0)
        def _():
            dlg_ref[...] = jnp.zeros_like(dlg_ref)
            dlb_ref[...] = jnp.zeros_like(dlb_ref)
            dcb_ref[...] = jnp.zeros_like(dcb_ref)

        dlg_ref[...] += jnp.sum(dy * xhat, axis=0, keepdims=True)
        dlb_ref[...] += jnp.sum(dy, axis=0, keepdims=True)
        dcb_ref[...] += jnp.sum(dx, axis=0, keepdims=True)

    vec = jax.ShapeDtypeStruct((1, cd), F32)
    return _pcall(
        body, name="conv_bwd_norm", grid=(lp // BLOCK,),
        in_specs=[_row_spec(cd), _row_spec(cd), _const_spec((1, cd)), _const_spec((1, cd))],
        out_specs=[_row_spec(cd), _const_spec((1, cd)), _const_spec((1, cd)), _const_spec((1, cd))],
        out_shape=[jax.ShapeDtypeStruct((lp, cd), F32), vec, vec, vec],
        compiler_params=_params(("arbitrary",)),
    )(dc2, conv_out, ln_g, ln_b)


def _conv_bwd_taps(dco, zc, conv_w):
    lp, cd = dco.shape
    nb = lp // BLOCK
    chunk = min(CONV_CHUNK, cd)
    back = lambda col: (lambda i: (jnp.maximum(i - 1, 0), col))
    fwd = lambda i: (jnp.minimum(i + 1, nb - 1), 0)
    lo = BLOCK - (CONV_WIDTH - 1)

    def body(dc_ref, dn_ref, ap_ref, gp_ref, ac_ref, gc_ref, w_ref, dz_ref, sum_ref, dw_ref, ext, dext, dcb, sh, dsh):
        i = pl.program_id(0)
        ext[0:BLOCK, :] = _glu_masked(ap_ref, gp_ref, (i - 1) * BLOCK)
        ext[BLOCK:2 * BLOCK, :] = _glu_masked(ac_ref, gc_ref, i * BLOCK)
        dext[0:BLOCK, :] = dc_ref[...]
        dext[BLOCK:2 * BLOCK, :] = dn_ref[...] * (i < nb - 1).astype(F32)
        _shifted_windows(ext, sh, SH_BASE, SH_ROWS)
        _shifted_windows(dext, dsh, 0, DSH_ROWS)

        @pl.when(i == 0)
        def _():
            dw_ref[...] = jnp.zeros_like(dw_ref)
            sum_ref[...] = jnp.zeros_like(sum_ref)

        for c0 in range(0, cd, chunk):
            cols = slice(c0, c0 + chunk)
            dcur = dext[0:BLOCK, cols]
            acc = jnp.zeros((BLOCK, chunk), F32)
            for k in range(CONV_WIDTH):
                s = CONV_WIDTH - 1 - k
                acc = acc + _window(dext, dsh, 0, s, cols) * w_ref[k:k + 1, cols]
                dw_ref[k:k + 1, cols] += jnp.sum(dcur * _window(ext, sh, SH_BASE, lo + k, cols), axis=0, keepdims=True)
            dcb[:, cols] = acc
        rows = i * BLOCK + lax.broadcasted_iota(jnp.int32, (BLOCK, 1), 0)
        dc = jnp.where(rows >= PAD_ROWS, dcb[...], 0.0)
        a = ac_ref[...].astype(F32)
        sg = _sigmoid(gc_ref[...].astype(F32))
        da = dc * sg
        dg = dc * a * sg * (1.0 - sg)
        dz_ref[:, 0:cd] = da.astype(BF16)
        dz_ref[:, cd:2 * cd] = dg.astype(BF16)
        sum_ref[:, 0:cd] += jnp.sum(da, axis=0, keepdims=True)
        sum_ref[:, cd:2 * cd] += jnp.sum(dg, axis=0, keepdims=True)

    return _pcall(
        body, name="conv_bwd_taps", grid=(nb,),
        in_specs=[_row_spec(cd), pl.BlockSpec((BLOCK, cd), fwd),
                  pl.BlockSpec((BLOCK, cd), back(0)), pl.BlockSpec((BLOCK, cd), back(1)), _row_spec(cd, 0), _row_spec(cd, 1),
                  _const_spec((CONV_ROWS, cd))],
        out_specs=[_row_spec(2 * cd), _const_spec((1, 2 * cd)), _const_spec((CONV_ROWS, cd))],
        out_shape=[jax.ShapeDtypeStruct((lp, 2 * cd), BF16), jax.ShapeDtypeStruct((1, 2 * cd), F32),
                   jax.ShapeDtypeStruct((CONV_ROWS, cd), F32)],
        scratch_shapes=[pltpu.VMEM((2 * BLOCK, cd), F32), pltpu.VMEM((2 * BLOCK, cd), F32), pltpu.VMEM((BLOCK, cd), F32),
                        pltpu.VMEM((SUBLANES, SH_ROWS, cd), F32), pltpu.VMEM((SUBLANES, DSH_ROWS, cd), F32)],
        compiler_params=_params(("arbitrary",)),
    )(dco, dco, zc, zc, zc, zc, conv_w)


def _gate_fwd(a, b, zg, after=None):
    lp, d = a.shape

    def body(a_ref, b_ref, ga_ref, gb_ref, m_ref):
        ga, gb = ga_ref[...].astype(F32), gb_ref[...].astype(F32)
        m_ref[...] = (_sigmoid(ga) * a_ref[...].astype(F32) + _sigmoid(gb) * b_ref[...].astype(F32)).astype(BF16)

    return _pcall(
        body, after=after, name="gate_fwd", grid=(lp // BLOCK,),
        in_specs=[_row_spec(d), _row_spec(d), _row_spec(d, 0), _row_spec(d, 1)], out_specs=_row_spec(d),
        out_shape=jax.ShapeDtypeStruct((lp, d), BF16), compiler_params=_params(("parallel",)),
    )(a, b, zg, zg)


def _gate_bwd(dm, a, b, zg):
    lp, d = a.shape

    def body(dm_ref, a_ref, b_ref, ga_ref, gb_ref, da_ref, db_ref, dz_ref, sum_ref, dbias_ref):
        i = pl.program_id(0)
        dm_ = dm_ref[...].astype(F32)
        sa = _sigmoid(ga_ref[...].astype(F32))
        sb = _sigmoid(gb_ref[...].astype(F32))
        db = dm_ * sb
        dga = dm_ * a_ref[...].astype(F32) * sa * (1.0 - sa)
        dgb = dm_ * b_ref[...].astype(F32) * sb * (1.0 - sb)
        da_ref[...] = (dm_ * sa).astype(BF16)
        db_ref[...] = db.astype(BF16)
        dz_ref[:, 0:d] = dga.astype(BF16)
        dz_ref[:, d:2 * d] = dgb.astype(BF16)

        @pl.when(i == 0)
        def _():
            sum_ref[...] = jnp.zeros_like(sum_ref)
            dbias_ref[...] = jnp.zeros_like(dbias_ref)

        sum_ref[:, 0:d] += jnp.sum(dga, axis=0, keepdims=True)
        sum_ref[:, d:2 * d] += jnp.sum(dgb, axis=0, keepdims=True)
        dbias_ref[...] += jnp.sum(db, axis=0, keepdims=True)

    return _pcall(
        body, name="gate_bwd", grid=(lp // BLOCK,),
        in_specs=[_row_spec(d), _row_spec(d), _row_spec(d), _row_spec(d, 0), _row_spec(d, 1)],
        out_specs=[_row_spec(d), _row_spec(d), _row_spec(2 * d), _const_spec((1, 2 * d)), _const_spec((1, d))],
        out_shape=[jax.ShapeDtypeStruct((lp, d), BF16), jax.ShapeDtypeStruct((lp, d), BF16), jax.ShapeDtypeStruct((lp, 2 * d), BF16),
                   jax.ShapeDtypeStruct((1, 2 * d), F32), jax.ShapeDtypeStruct((1, d), F32)],
        compiler_params=_params(("arbitrary",)),
    )(dm, a, b, zg, zg)


def _gate_up_swiglu(u2, wgu_t, *, tm, tn, after=None):
    m, k = u2.shape
    f = wgu_t.shape[0] // 2
    tm, tn = _pick(m, tm, 16), _pick(f, tn, LANES)
    nj = f // tn
    dims = (((1,), (1,)), ((), ()))

    def body(a_ref, wg_ref, wu_ref, g_ref, u_ref, act_ref):
        a = a_ref[...]
        g = lax.dot_general(a, wg_ref[...], dims, preferred_element_type=F32).astype(BF16)
        up = lax.dot_general(a, wu_ref[...], dims, preferred_element_type=F32).astype(BF16)
        g_ref[...] = g
        u_ref[...] = up
        gf = g.astype(F32)
        act_ref[...] = (gf * _sigmoid(gf) * up.astype(F32)).astype(BF16)

    out = pl.BlockSpec((tm, tn), lambda i, j: (i, j))
    shape = jax.ShapeDtypeStruct((m, f), BF16)
    return _pcall(
        body, after=after, name="ffn_gate_up_swiglu", grid=(m // tm, nj),
        in_specs=[pl.BlockSpec((tm, k), lambda i, j: (i, 0)), pl.BlockSpec((tn, k), lambda i, j: (j, 0)),
                  pl.BlockSpec((tn, k), lambda i, j: (j + nj, 0))],
        out_specs=[out, out, out], out_shape=[shape, shape, shape], compiler_params=_params(("parallel", "arbitrary")),
    )(u2, wgu_t, wgu_t)


def _swiglu_bwd(dact, g, up):
    lp, f = dact.shape

    def body(d_ref, g_ref, u_ref, o_ref):
        g = g_ref[...].astype(F32)
        d = d_ref[...].astype(F32)
        sg = _sigmoid(g)
        o_ref[:, 0:f] = (d * u_ref[...].astype(F32) * (sg * (1.0 + g * (1.0 - sg)))).astype(BF16)
        o_ref[:, f:2 * f] = (d * g * sg).astype(BF16)

    return _pcall(
        body, name="swiglu_bwd", grid=(lp // BLOCK,), in_specs=[_row_spec(f), _row_spec(f), _row_spec(f)],
        out_specs=_row_spec(2 * f), out_shape=jax.ShapeDtypeStruct((lp, 2 * f), BF16), compiler_params=_params(("parallel",)),
    )(dact, g, up)


ANY = pl.BlockSpec(memory_space=pl.ANY)


def _all_gather_rows(x, name, after=None):
    r, c = x.shape

    def body(x_ref, out_ref, send_sems, recv_sems, local_sem):
        mx, my, mc = lax.axis_index("x"), lax.axis_index("y"), lax.axis_index("c")
        me, sibling = (mx, my, mc), (mx, my, 1 - mc)
        chips = [(1 - mx, my), (mx, 1 - my), (1 - mx, 1 - my)]

        def rows(px, py, pc):
            return out_ref.at[pl.ds((4 * px + 2 * py + pc) * r, r), :]

        def copy(k, block, to, src=None):
            return pltpu.make_async_remote_copy(
                src_ref=rows(*block) if src is None else src, dst_ref=rows(*block),
                send_sem=send_sems.at[k], recv_sem=recv_sems.at[k], device_id=to, device_id_type=MESH)

        mine = pltpu.make_async_copy(x_ref, rows(*me), local_sem)
        mine.start()
        first = [copy(0, me, sibling, src=x_ref)]
        first += [copy(1 + j, me, (*chip, mc), src=x_ref) for j, chip in enumerate(chips)]
        for cp in first:
            cp.start()
        passed = [copy(4 + j, (*chip, mc), sibling) for j, chip in enumerate(chips)]
        for j, chip in enumerate(chips):
            copy(1 + j, (*chip, mc), me).wait_recv()
            passed[j].start()
        copy(0, sibling, me).wait_recv()
        for j, chip in enumerate(chips):
            copy(4 + j, (*chip, 1 - mc), me).wait_recv()
        for cp in first + passed:
            cp.wait_send()
        mine.wait()

    return _pcall(
        body, after=after, name=name, in_specs=[ANY], out_specs=ANY, out_shape=jax.ShapeDtypeStruct((N_DEV * r, c), x.dtype),
        scratch_shapes=[pltpu.SemaphoreType.DMA((7,)), pltpu.SemaphoreType.DMA((7,)), pltpu.SemaphoreType.DMA(())],
    )(x)


HBM = pl.BlockSpec(memory_space=pltpu.HBM)
SEM = pl.BlockSpec(memory_space=pltpu.SEMAPHORE)
IN_FLIGHT = pltpu.CompilerParams(has_side_effects=pltpu.SideEffectType.DATAFLOW_SIDE_EFFECTING)


def _place_rows(shard, after, name):
    r, c = shard.shape
    tr = _pick(r, max(16, ELEMENTWISE_BLOCK_BYTES // (4 * c)), 16)
    steps = r // tr
    dev = (4 * lax.axis_index("x") + 2 * lax.axis_index("y") + lax.axis_index("c")).astype(jnp.int32).reshape(1)

    def body(dev_ref, x_ref, after_ref, o_ref):
        o_ref[...] = x_ref[...].astype(BF16)

    return _pcall(
        body, name=name,
        grid_spec=pltpu.PrefetchScalarGridSpec(
            num_scalar_prefetch=1, grid=(steps,),
            in_specs=[pl.BlockSpec((tr, c), lambda i, dev_ref: (i, 0)), pl.BlockSpec(memory_space=pl.ANY)],
            out_specs=pl.BlockSpec((tr, c), lambda i, dev_ref: (dev_ref[0] * steps + i, 0))),
        out_shape=jax.ShapeDtypeStruct((N_DEV * r, c), BF16), compiler_params=_params(("parallel",)),
    )(dev, shard, after)


def _rows_start(full, plan, name, after=None):
    r = full.shape[0] // N_DEV
    n = len(plan(0, 0, 0))

    ordered = after is not None

    def body(*refs):
        full_ref, (send_sems, recv_sems) = refs[0], refs[1 + ordered:3 + ordered]
        mx, my, mc = lax.axis_index("x"), lax.axis_index("y"), lax.axis_index("c")
        for k, ((bx, by, bc), target) in enumerate(plan(mx, my, mc)):
            rows = full_ref.at[pl.ds((4 * bx + 2 * by + bc) * r, r), :]
            pltpu.make_async_remote_copy(
                src_ref=rows, dst_ref=rows, send_sem=send_sems.at[k], recv_sem=recv_sems.at[k],
                device_id=target, device_id_type=MESH).start()

    return pl.pallas_call(
        body, name=name, in_specs=[HBM] + [pl.BlockSpec(memory_space=pl.ANY)] * ordered, out_specs=(SEM, SEM, HBM),
        out_shape=(pltpu.SemaphoreType.DMA((n,)), pltpu.SemaphoreType.DMA((n,)), pltpu.HBM(full.shape, full.dtype)),
        input_output_aliases={0: 2}, compiler_params=IN_FLIGHT,
    )(pltpu.with_memory_space_constraint(full, pltpu.HBM), *([after] if ordered else []))


def _rows_wait(started, after, name):
    send_sem, recv_sem, full = started
    r = full.shape[0] // N_DEV
    n = send_sem.shape[0]

    def body(full_ref, send_ref, recv_ref, after_ref, out_ref):
        mx, my, mc = lax.axis_index("x"), lax.axis_index("y"), lax.axis_index("c")
        block = full_ref.at[pl.ds(0, r), :]
        for k in range(n):
            cp = pltpu.make_async_remote_copy(
                src_ref=block, dst_ref=block, send_sem=send_ref.at[k], recv_sem=recv_ref.at[k],
                device_id=(mx, my, mc), device_id_type=MESH)
            cp.wait_send()
            cp.wait_recv()

    return pl.pallas_call(
        body, name=name, in_specs=[HBM, SEM, SEM, pl.BlockSpec(memory_space=pl.ANY)], out_specs=HBM,
        out_shape=pltpu.HBM(full.shape, full.dtype), input_output_aliases={0: 0}, compiler_params=IN_FLIGHT,
    )(full, send_sem, recv_sem, after)


def _plan_direct(mx, my, mc):
    me = (mx, my, mc)
    return [(me, (mx, my, 1 - mc)), (me, (1 - mx, my, mc)), (me, (mx, 1 - my, mc)), (me, (1 - mx, 1 - my, mc))]


def _plan_pass_on(mx, my, mc):
    sibling = (mx, my, 1 - mc)
    return [((1 - mx, my, mc), sibling), ((mx, 1 - my, mc), sibling), ((1 - mx, 1 - my, mc), sibling)]


def _plan_neighbours(mx, my, mc):
    me = (mx, my, mc)
    return [(me, (mx, my, 1 - mc)), (me, (1 - mx, my, mc)), (me, (mx, 1 - my, mc))]


def _plan_relay(mx, my, mc):
    sibling = (mx, my, 1 - mc)
    source = ((mx + 1 - mc) % 2, (my + mc) % 2, mc)
    target = ((mx + mc) % 2, (my + 1 - mc) % 2, mc)
    return [((1 - mx, my, mc), sibling), ((mx, 1 - my, mc), sibling), (source, target)]


def _plan_pass_on_diagonal(mx, my, mc):
    return [((1 - mx, 1 - my, mc), (mx, my, 1 - mc))]


def _pair_exchange_start(g, name):
    r = g.shape[0] // N_DEV
    c = g.shape[1]
    land = (len(CHIPS), r, c)

    def body(g_ref, land_ref, send_sems, recv_sems, g_out, land_out):
        mx, my, mc = lax.axis_index("x"), lax.axis_index("y"), lax.axis_index("c")
        for j, (px, py) in enumerate(CHIPS):
            pltpu.make_async_remote_copy(
                src_ref=g_ref.at[pl.ds((4 * px + 2 * py + 1 - mc) * r, r), :], dst_ref=land_ref.at[j],
                send_sem=send_sems.at[j], recv_sem=recv_sems.at[j], device_id=(mx, my, 1 - mc), device_id_type=MESH).start()

    return pl.pallas_call(
        body, name=name, in_specs=[HBM, HBM], out_specs=(SEM, SEM, HBM, HBM),
        out_shape=(pltpu.SemaphoreType.DMA((4,)), pltpu.SemaphoreType.DMA((4,)), pltpu.HBM(g.shape, g.dtype), pltpu.HBM(land, g.dtype)),
        input_output_aliases={0: 2, 1: 3}, compiler_params=IN_FLIGHT,
    )(pltpu.with_memory_space_constraint(g, pltpu.HBM), pltpu.with_memory_space_constraint(lax.empty(land, g.dtype), pltpu.HBM))


def _pair_exchange_wait(send_sem, recv_sem, g, land, after, name):
    def body(g_ref, land_ref, send_ref, recv_ref, after_ref, g_out, land_out):
        mx, my, mc = lax.axis_index("x"), lax.axis_index("y"), lax.axis_index("c")
        for j in range(len(CHIPS)):
            cp = pltpu.make_async_remote_copy(
                src_ref=land_ref.at[0], dst_ref=land_ref.at[0], send_sem=send_ref.at[j], recv_sem=recv_ref.at[j],
                device_id=(mx, my, mc), device_id_type=MESH)
            cp.wait_send()
            cp.wait_recv()

    return pl.pallas_call(
        body, name=name, in_specs=[HBM, HBM, SEM, SEM, pl.BlockSpec(memory_space=pl.ANY)], out_specs=(HBM, HBM),
        out_shape=(pltpu.HBM(g.shape, g.dtype), pltpu.HBM(land.shape, land.dtype)), input_output_aliases={0: 0, 1: 1},
        compiler_params=IN_FLIGHT,
    )(g, land, send_sem, recv_sem, after)


def _chip_exchange_start(ps, after, name):
    def body(ps_ref, rx_ref, after_ref, send_sems, recv_sems, ps_out, rx_out):
        mx, my, mc = lax.axis_index("x"), lax.axis_index("y"), lax.axis_index("c")
        chips = [(1 - mx, my), (mx, 1 - my), (1 - mx, 1 - my)]
        for k, (px, py) in enumerate(chips):
            pltpu.make_async_remote_copy(
                src_ref=ps_ref.at[2 * px + py], dst_ref=rx_ref.at[2 * mx + my], send_sem=send_sems.at[k], recv_sem=recv_sems.at[k],
                device_id=(px, py, mc), device_id_type=MESH).start()

    return pl.pallas_call(
        body, name=name, in_specs=[HBM, HBM, pl.BlockSpec(memory_space=pl.ANY)], out_specs=(SEM, SEM, HBM, HBM),
        out_shape=(pltpu.SemaphoreType.DMA((3,)), pltpu.SemaphoreType.DMA((3,)), pltpu.HBM(ps.shape, ps.dtype), pltpu.HBM(ps.shape, ps.dtype)),
        input_output_aliases={0: 2, 1: 3}, compiler_params=IN_FLIGHT,
    )(pltpu.with_memory_space_constraint(ps, pltpu.HBM), pltpu.with_memory_space_constraint(lax.empty(ps.shape, ps.dtype), pltpu.HBM), after)


def _chip_exchange_wait(send_sem, recv_sem, ps, rx, after, name):
    def body(ps_ref, rx_ref, send_ref, recv_ref, after_ref, ps_out, rx_out):
        mx, my, mc = lax.axis_index("x"), lax.axis_index("y"), lax.axis_index("c")
        for k in range(3):
            cp = pltpu.make_async_remote_copy(
                src_ref=ps_ref.at[0], dst_ref=rx_ref.at[0], send_sem=send_ref.at[k], recv_sem=recv_ref.at[k],
                device_id=(mx, my, mc), device_id_type=MESH)
            cp.wait_send()
            cp.wait_recv()

    return pl.pallas_call(
        body, name=name, in_specs=[HBM, HBM, SEM, SEM, pl.BlockSpec(memory_space=pl.ANY)], out_specs=(HBM, HBM),
        out_shape=(pltpu.HBM(ps.shape, ps.dtype), pltpu.HBM(rx.shape, rx.dtype)), input_output_aliases={0: 0, 1: 1},
        compiler_params=IN_FLIGHT,
    )(ps, rx, send_sem, recv_sem, after)


def _sum_chips(ps, rx, name):
    n, r, c = rx.shape
    tr = _pick(r, max(16, 4 * ELEMENTWISE_BLOCK_BYTES // (4 * n * c)), 16)
    chip =(2 * lax.axis_index("x") + lax.axis_index("y")).astype(jnp.int32).reshape(1)

    def body(chip_ref, own_ref, x_ref, o_ref):
        me = chip_ref[0]
        own = own_ref[0].astype(F32)
        acc = jnp.where(me == 0, own, x_ref[0].astype(F32))
        for j in range(1, n):
            acc = acc + jnp.where(me == j, own, x_ref[j].astype(F32))
        o_ref[...] = acc

    return _pcall(
        body, name=name,
        grid_spec=pltpu.PrefetchScalarGridSpec(
            num_scalar_prefetch=1, grid=(r // tr,),
            in_specs=[pl.BlockSpec((1, tr, c), lambda i, chip_ref: (chip_ref[0], i, 0)), pl.BlockSpec((n, tr, c), lambda i, chip_ref: (0, i, 0))],
            out_specs=pl.BlockSpec((tr, c), lambda i, chip_ref: (i, 0))),
        out_shape=jax.ShapeDtypeStruct((r, c), F32), compiler_params=_params(("parallel",)),
    )(chip, ps, rx)


def _pair_exchange(g, name):
    r = g.shape[0] // N_DEV
    c = g.shape[1]

    def body(g_ref, theirs_ref, send_sems, recv_sems):
        mx, my, mc = lax.axis_index("x"), lax.axis_index("y"), lax.axis_index("c")
        sibling = (mx, my, 1 - mc)
        copies = []
        for j, (px, py) in enumerate(CHIPS):
            give = g_ref.at[pl.ds((4 * px + 2 * py + 1 - mc) * r, r), :]
            rc = pltpu.make_async_remote_copy(
                src_ref=give, dst_ref=theirs_ref.at[j], send_sem=send_sems.at[j], recv_sem=recv_sems.at[j],
                device_id=sibling, device_id_type=MESH)
            rc.start()
            copies.append(rc)
        for cp in copies:
            cp.wait()

    return _pcall(
        body, name=name, in_specs=[ANY], out_specs=ANY, out_shape=jax.ShapeDtypeStruct((len(CHIPS), r, c), g.dtype),
        scratch_shapes=[pltpu.SemaphoreType.DMA((4,)), pltpu.SemaphoreType.DMA((4,))],
    )(g)


def _pair_sum(g, theirs, name):
    nch, r, c = theirs.shape
    tr = _pick(r, max(16, 3 * ELEMENTWISE_BLOCK_BYTES // (2 * c)), 16)
    core = lax.axis_index("c").astype(jnp.int32).reshape(1)

    def body(core_ref, a_ref, b_ref, o_ref):
        o_ref[...] = (a_ref[...].astype(F32) + b_ref[...].astype(F32)).astype(o_ref.dtype)

    spec = pl.BlockSpec((1, tr, c), lambda j, i, core_ref: (j, i, 0))
    own = pl.BlockSpec((1, tr, c), lambda j, i, core_ref: (2 * j + core_ref[0], i, 0))
    return _pcall(
        body, name=name,
        grid_spec=pltpu.PrefetchScalarGridSpec(num_scalar_prefetch=1, grid=(nch, r // tr), in_specs=[own, spec], out_specs=spec),
        out_shape=jax.ShapeDtypeStruct(theirs.shape, theirs.dtype), compiler_params=_params(("parallel", "parallel")),
    )(core, g.reshape(N_DEV, r, c), theirs)


def _sum_blocks(rx, name):
    n, r, c = rx.shape
    tr = _pick(r, max(8, ELEMENTWISE_BLOCK_BYTES // (4 * n * c)), 8)

    def body(x_ref, o_ref):
        acc = x_ref[0].astype(F32)
        for j in range(1, n):
            acc = acc + x_ref[j].astype(F32)
        o_ref[...] = acc

    return _pcall(
        body, name=name, grid=(r // tr,), in_specs=[pl.BlockSpec((n, tr, c), lambda i: (0, i, 0))],
        out_specs=pl.BlockSpec((tr, c), lambda i: (i, 0)), out_shape=jax.ShapeDtypeStruct((r, c), F32),
        compiler_params=_params(("parallel",)),
    )(rx)


def _adamw(w, g, m, v, name):
    r, c = w.shape
    tr = _pick(r, max(8, ELEMENTWISE_BLOCK_BYTES // (4 * c)), 8)
    c1 = 1.0 - ADAM_B1 ** ADAM_STEP
    c2 = 1.0 - ADAM_B2 ** ADAM_STEP

    def body(w_ref, g_ref, m_ref, v_ref, d_ref, nm_ref, nv_ref):
        gg = g_ref[...]
        nm = ADAM_B1 * m_ref[...] + (1.0 - ADAM_B1) * gg
        nv = ADAM_B2 * v_ref[...] + (1.0 - ADAM_B2) * (gg * gg)
        d_ref[...] = -ADAM_LR * ((nm / c1) / (jnp.sqrt(nv / c2) + ADAM_EPS) + ADAM_WD * w_ref[...])
        nm_ref[...] = nm
        nv_ref[...] = nv

    spec = pl.BlockSpec((tr, c), lambda i: (i, 0))
    shp = jax.ShapeDtypeStruct((r, c), F32)
    return _pcall(
        body, name=name, grid=(r // tr,), in_specs=[spec] * 4, out_specs=[spec] * 3, out_shape=[shp] * 3,
        compiler_params=_params(("parallel",)),
    )(w, g, m, v)


def _pack(parts):
    flat, layout, row = [], [], 0
    for p in parts:
        n = p.size
        rows = -(-n // LANES)
        flat.append(jnp.pad(p.reshape(-1).astype(F32), (0, rows * LANES - n)))
        layout.append((row, n, p.shape))
        row += rows
    total = -(-row // 8) * 8
    if total > row:
        flat.append(jnp.zeros(((total - row) * LANES,), F32))
    return jnp.concatenate(flat).reshape(total, LANES), layout


def _unpack(slab, layout):
    flat = slab.reshape(-1)
    return [flat[row * LANES:row * LANES + n].reshape(shape) for row, n, shape in layout]


def kernel(x, meta_tokens, mix_norm_g, w_in, b_in, attn_sinks, conv_w, conv_b, conv_ln_g, conv_ln_b, w_attn_o, w_conv_o, b_conv_o, w_out, ffn_norm_g, w_gate_up, w_down, final_norm_g, loss_target, m_meta_tokens, m_mix_norm_g, m_w_in, m_b_in, m_attn_sinks, m_conv_w, m_conv_b, m_conv_ln_g, m_conv_ln_b, m_w_attn_o, m_w_conv_o, m_b_conv_o, m_w_out, m_ffn_norm_g, m_w_gate_up, m_w_down, m_final_norm_g, v_meta_tokens, v_mix_norm_g, v_w_in, v_b_in, v_attn_sinks, v_conv_w, v_conv_b, v_conv_ln_g, v_conv_ln_b, v_w_attn_o, v_w_conv_o, v_b_conv_o, v_w_out, v_ffn_norm_g, v_w_gate_up, v_w_down, v_final_norm_g):
    xs = x[0]
    tgt = loss_target[0]
    s, d = xs.shape
    lp = s + BLOCK
    cd = conv_b.shape[1]
    ffn = w_down.shape[1] * N_DEV
    dev = 4 * lax.axis_index("x") + 2 * lax.axis_index("y") + lax.axis_index("c")
    cw_cols = conv_w.shape[3]
    meta_cols = meta_tokens.shape[1]

    small, small_layout = _pack([meta_tokens, jnp.pad(conv_w[0, :, 0, :], ((0, CONV_ROWS - CONV_WIDTH), (0, 0)))])
    small_flat = _all_gather_rows(small, "gather_small")
    small_all = small_flat.reshape(N_DEV, *small.shape)
    meta_parts, cw_parts = zip(*[_unpack(small_all[j], small_layout) for j in range(N_DEV)])
    meta_full = jnp.concatenate(meta_parts, axis=1)
    conv_w_full = jnp.concatenate(cw_parts, axis=1)
    shards = ((w_in[0].T, "w_in"), (w_attn_o[0].T, "w_attn_o"), (w_conv_o[0].T, "w_conv_o"), (w_out[0], "w_out"),
              (w_gate_up[0].T, "w_gate_up"), (w_down[0], "w_down"))
    first = _rows_start(_place_rows(shards[0][0], small_flat, "place_w_in"), _plan_neighbours, "gather_start_w_in")
    placed, tok = [], first[2]
    for shard, name in shards[1:]:
        tok = _place_rows(shard, tok, "place_" + name)
        placed.append(tok)
    relay = _rows_start(_rows_wait(first, tok, "gather_wait_w_in"), _plan_relay, "gather_relay_start_w_in")
    h0, u = _prep(xs, meta_full, mix_norm_g, after=relay[2])
    diagonal = _rows_start(_rows_wait(relay, u, "gather_relay_wait_w_in"), _plan_pass_on_diagonal, "gather_diagonal_start_w_in")
    started, tok = [None], diagonal[2]
    for full, (_, name) in zip(placed, shards[1:]):
        started.append(_rows_start(full, _plan_direct, "gather_start_" + name, after=tok))
        tok = started[-1][2]
    win_t = _rows_wait(diagonal, tok, "gather_diagonal_wait_w_in")

    def arrived(w, after, name):
        return _rows_start(_rows_wait(started[w], after, "gather_wait_" + name), _plan_pass_on, "gather_pass_on_start_" + name)

    def whole(passing, after, name):
        return _rows_wait(passing, after, "gather_pass_on_wait_" + name)

    ctab, stab = _rope_tables(lp)
    mm = functools.partial(_matmul, tm=1056, tn=1024)

    bq, bkv, bc, bg = b_in[:, :Q_DIM], b_in[:, Q_DIM:Q_DIM + 2 * KV_DIM], b_in[:, Q_DIM + 2 * KV_DIM:Q_DIM + 2 * KV_DIM + 2 * cd], b_in[:, Q_DIM + 2 * KV_DIM + 2 * cd:]
    o_kv, o_c, o_g = Q_DIM, Q_DIM + 2 * KV_DIM, Q_DIM + 2 * KV_DIM + 2 * cd
    in_proj = functools.partial(_matmul, u, win_t, mode="nt", out_dtype=BF16, tm=2112, tn=512, tk=d)
    zq = in_proj(name="in_proj_q", bias=bq, b_row_off=0, b_rows=Q_DIM)
    zkv = in_proj(name="in_proj_kv", bias=bkv, b_row_off=o_kv, b_rows=2 * KV_DIM)
    zc = in_proj(name="in_proj_conv", bias=bc, b_row_off=o_c, b_rows=2 * cd)
    zg = in_proj(name="in_proj_gates", bias=bg, b_row_off=o_g, b_rows=2 * d)
    passing = arrived(1, zg, "w_attn_o")
    q_rot, k_sh, v_sh = _rope_fwd(zq, zkv, ctab, stab, after=passing[2])
    o = _attn_fwd(q_rot, k_sh, v_sh, attn_sinks)
    wao_t = whole(passing, o, "w_attn_o")
    br_a = mm(o, wao_t, mode="nt", name="attn_out_proj", out_dtype=BF16, tk=Q_DIM)
    passing = arrived(2, br_a, "w_conv_o")
    conv_out, c2 = _conv_fwd(zc, conv_w_full, conv_b, conv_ln_g, conv_ln_b, after=passing[2])
    wco_t = whole(passing, c2, "w_conv_o")
    br_b = mm(c2, wco_t, mode="nt", name="conv_out_proj", out_dtype=BF16, tk=cd, bias=b_conv_o)
    passing = arrived(3, br_b, "w_out")
    merged = _gate_fwd(br_a, br_b, zg, after=passing[2])
    wout = whole(passing, merged, "w_out")
    passing = arrived(4, wout, "w_gate_up")
    h1 = mm(merged, wout, mode="nn", name="mix_out_proj", out_dtype=F32, tn=512, tk=d, residual=h0, after=passing[2])
    u2 = _rmsnorm_fwd(h1, ffn_norm_g, "ffn_rmsnorm")
    wgu_t = whole(passing, u2, "w_gate_up")
    passing = arrived(5, wgu_t, "w_down")
    gu_g, gu_u, act = _gate_up_swiglu(u2, wgu_t, tm=2112, tn=256, after=passing[2])
    wdown = whole(passing, act, "w_down")
    h2 = mm(act, wdown, mode="nn", name="ffn_down", out_dtype=F32, tn=512, tk=ffn // 2, residual=h1)
    dh2, dh2_b, loss_part, d_final_g = _final(h2, tgt, final_norm_g.reshape(1, d))

    wgrad = functools.partial(_matmul, mode="tn", out_dtype=BF16, tk=lp, tn=2048, b_inner=False)
    in_flight = {}

    def scatter_begin(g, name):
        return _pair_exchange_start(g, "rs_" + name + "_pair_start")

    def scatter_go_on(pair, after, name):
        g, theirs = _pair_exchange_wait(pair[0], pair[1], pair[2], pair[3], after, "rs_" + name + "_pair_wait")
        ps = _pair_sum(g, theirs, "rs_" + name + "_pair_sum")
        in_flight[name] = _chip_exchange_start(ps, theirs, "rs_" + name + "_chip_start")
        return in_flight[name][2]

    g_wdown = wgrad(act, dh2_b, name="ffn_down_dw", tm=256)
    pair = scatter_begin(g_wdown, "w_down")
    dact = _matmul(dh2_b, wdown, mode="nt", name="ffn_down_dx", out_dtype=BF16, tm=2112, tn=256, tk=d, after=pair[2])
    tok = scatter_go_on(pair, dact, "w_down")
    dgu = _swiglu_bwd(dact, gu_g, gu_u)
    g_wgu_t = wgrad(dgu, u2, name="ffn_gate_up_dw", tm=512, after=tok)
    pair = scatter_begin(g_wgu_t, "w_gate_up")
    du2 = mm(dgu, wgu_t, mode="nn", name="ffn_gate_up_dx", out_dtype=F32, tn=512, tk=ffn // 2, after=pair[2])
    tok = scatter_go_on(pair, du2, "w_gate_up")
    dh1, dh1_b, d_ffn_g = _rmsnorm_bwd(du2, h1, ffn_norm_g, dh2, "ffn_rmsnorm_bwd")
    g_wout = wgrad(merged, dh1_b, name="mix_out_dw", tm=512, after=tok)
    pair = scatter_begin(g_wout, "w_out")
    dmerged = mm(dh1_b, wout, mode="nt", name="mix_out_dx", out_dtype=BF16, tk=d, after=pair[2])
    tok = scatter_go_on(pair, dmerged, "w_out")
    d_a, d_b, dz_g, sum_g, d_bco = _gate_bwd(dmerged, br_a, br_b, zg)
    g_wao_t = wgrad(d_a, o, name="attn_out_dw", tm=512, after=tok)
    pair = scatter_begin(g_wao_t, "w_attn_o")
    do = mm(d_a, wao_t, mode="nn", name="attn_out_dx", out_dtype=BF16, tk=d, after=pair[2])
    tok = scatter_go_on(pair, do, "w_attn_o")
    g_wco_t = wgrad(d_b, c2, name="conv_out_dw", tm=512, after=tok)
    pair = scatter_begin(g_wco_t, "w_conv_o")
    dc2 = mm(d_b, wco_t, mode="nn", name="conv_out_dx", out_dtype=F32, tk=d, after=pair[2])
    tok = scatter_go_on(pair, dc2, "w_conv_o")
    dq, dk, dv, dkm, dvm, d_sinks = _attn_bwd(q_rot, k_sh, v_sh, attn_sinks, do)
    dz_qkv, sum_qkv = _rope_bwd(dq, dk, dv, dkm, dvm, ctab, stab)
    dco, d_ln_g, d_ln_b, d_conv_b = _conv_bwd_norm(dc2, conv_out, conv_ln_g, conv_ln_b)
    dz_c, sum_c, d_conv_w = _conv_bwd_taps(dco, zc, conv_w_full)
    dz = jnp.concatenate([dz_qkv, dz_c, dz_g], axis=1)
    d_b_in = jnp.concatenate([sum_qkv, sum_c, sum_g], axis=1)
    in_dim = dz.shape[1]
    g_win_t = wgrad(dz, u, name="in_proj_dw", tm=512, after=tok)
    theirs = _pair_exchange(g_win_t, "rs_w_in_pair_exchange")
    in_flight["w_in"] = _chip_exchange_start(_pair_sum(g_win_t, theirs, "rs_w_in_pair_sum"), theirs, "rs_w_in_chip_start")
    du = mm(dz, win_t, mode="nn", name="in_proj_dx", out_dtype=F32, tk=in_dim // 4, after=in_flight["w_in"][2])
    grad_x, d_meta, d_mix_g = _rmsnorm_bwd_first(du, h0, mix_norm_g, dh1)

    weights = dict(meta_tokens=meta_tokens, mix_norm_g=mix_norm_g, w_in=w_in, b_in=b_in, attn_sinks=attn_sinks, conv_w=conv_w,
                   conv_b=conv_b, conv_ln_g=conv_ln_g, conv_ln_b=conv_ln_b, w_attn_o=w_attn_o, w_conv_o=w_conv_o, b_conv_o=b_conv_o,
                   w_out=w_out, ffn_norm_g=ffn_norm_g, w_gate_up=w_gate_up, w_down=w_down, final_norm_g=final_norm_g)
    m_in = dict(meta_tokens=m_meta_tokens, mix_norm_g=m_mix_norm_g, w_in=m_w_in, b_in=m_b_in, attn_sinks=m_attn_sinks, conv_w=m_conv_w,
                conv_b=m_conv_b, conv_ln_g=m_conv_ln_g, conv_ln_b=m_conv_ln_b, w_attn_o=m_w_attn_o, w_conv_o=m_w_conv_o,
                b_conv_o=m_b_conv_o, w_out=m_w_out, ffn_norm_g=m_ffn_norm_g, w_gate_up=m_w_gate_up, w_down=m_w_down,
                final_norm_g=m_final_norm_g)
    v_in = dict(meta_tokens=v_meta_tokens, mix_norm_g=v_mix_norm_g, w_in=v_w_in, b_in=v_b_in, attn_sinks=v_attn_sinks, conv_w=v_conv_w,
                conv_b=v_conv_b, conv_ln_g=v_conv_ln_g, conv_ln_b=v_conv_ln_b, w_attn_o=v_w_attn_o, w_conv_o=v_w_conv_o,
                b_conv_o=v_b_conv_o, w_out=v_w_out, ffn_norm_g=v_ffn_norm_g, w_gate_up=v_w_gate_up, w_down=v_w_down,
                final_norm_g=v_final_norm_g)
    names = list(weights)
    grads, delta, new_m, new_v = {}, {}, {}, {}
    tok = grad_x
    for n in ("w_down", "w_gate_up", "w_out", "w_attn_o", "w_conv_o", "w_in"):
        send_sem, recv_sem, ps, rx = in_flight[n]
        ps, rx = _chip_exchange_wait(send_sem, recv_sem, ps, rx, tok, "rs_" + n + "_chip_wait")
        g = _sum_chips(ps, rx, "rs_" + n + "_sum")
        if n in ("w_attn_o", "w_conv_o"):
            g = g.T
        oriented = (lambda a: a[0].T) if n in ("w_in", "w_gate_up") else (lambda a: a[0])
        back = (lambda a: a.T[None]) if n in ("w_in", "w_gate_up") else (lambda a: a[None])
        dl, nm, nv = _adamw(oriented(weights[n]), g, oriented(m_in[n]), oriented(v_in[n]), "adamw_" + n)
        grads[n], delta[n], new_m[n], new_v[n] = back(g), back(dl), back(nm), back(nv)
        tok = dl

    slab, slab_layout = _pack([loss_part[:, :1], d_mix_g, d_b_in, d_sinks[:, :N_Q_HEADS], d_conv_b, d_ln_g, d_ln_b, d_bco,
                               d_ffn_g, d_final_g, d_conv_w, d_meta])
    slab_all = _all_gather_rows(slab, "gather_small_grads", after=tok).reshape(N_DEV, *slab.shape)
    (loss, g_mix_g, g_b_in, g_sinks, g_conv_b, g_ln_g, g_ln_b, g_bco, g_ffn_g, g_final_g, g_conv_w_full, g_meta_full
     ) = _unpack(_sum_blocks(slab_all, "sum_small_grads"), slab_layout)
    g_conv_w = lax.dynamic_slice(g_conv_w_full, (0, dev * cw_cols), (CONV_WIDTH, cw_cols)).reshape(conv_w.shape)
    g_meta = lax.dynamic_slice(g_meta_full, (0, dev * meta_cols), (N_META, meta_cols))
    g_final_g = g_final_g.reshape(final_norm_g.shape)
    grads.update(meta_tokens=g_meta, mix_norm_g=g_mix_g, b_in=g_b_in, attn_sinks=g_sinks, conv_w=g_conv_w, conv_b=g_conv_b,
                 conv_ln_g=g_ln_g, conv_ln_b=g_ln_b, b_conv_o=g_bco, ffn_norm_g=g_ffn_g, final_norm_g=g_final_g)
    rest = [n for n in names if n not in delta]
    w_slab, rest_layout = _pack([weights[n] for n in rest])
    g_slab, _ = _pack([grads[n] for n in rest])
    m_slab, _ = _pack([m_in[n] for n in rest])
    v_slab, _ = _pack([v_in[n] for n in rest])
    dl, nm, nv = _adamw(w_slab, g_slab, m_slab, v_slab, "adamw_small")
    for n, a, b, c in zip(rest, _unpack(dl, rest_layout), _unpack(nm, rest_layout), _unpack(nv, rest_layout)):
        delta[n], new_m[n], new_v[n] = a, b, c

    return (loss.reshape(()), grad_x[None], *[grads[n] for n in names], *[delta[n] for n in names],
            *[new_m[n] for n in names], *[new_v[n] for n in names])
```

```python
import functools
import math

import jax
import jax.numpy as jnp
from jax import lax
from jax.experimental import pallas as pl
from jax.experimental.pallas import tpu as pltpu

F32 = jnp.float32
BF16 = jnp.bfloat16

N_DEV = 8
BLOCK = 128
N_META = 16
PAD_ROWS = BLOCK - N_META
HEAD_DIM = 64
N_Q_HEADS = 32
N_KV_HEADS = 4
GROUP = N_Q_HEADS // N_KV_HEADS
Q_DIM = N_Q_HEADS * HEAD_DIM
KV_DIM = N_KV_HEADS * HEAD_DIM
WINDOW = 128
CONV_WIDTH = 31
CONV_ROWS = 32
ROPE_THETA = 10000.0
EPS = 1e-6
ATTN_SCALE = HEAD_DIM ** -0.5
NEG = -1e30

ADAM_LR = 0.001
ADAM_B1 = 0.9
ADAM_B2 = 0.999
ADAM_EPS = 1e-08
ADAM_WD = 0.01
ADAM_STEP = 10

VMEM_LIMIT_BYTES = 56 * 1024 * 1024
LANES = 128
ELEMENTWISE_BLOCK_BYTES = 2 * 1024 * 1024
MESH = pl.DeviceIdType.MESH
CHIPS = ((0, 0), (0, 1), (1, 0), (1, 1))


def _pcall(body, after=None, **kw):
    if after is None:
        return pl.pallas_call(body, **kw)
    in_specs = list(kw.pop("in_specs"))
    n_in = len(in_specs)

    def ordered_body(*refs):
        return body(*refs[:n_in], *refs[n_in + 1:])

    call = pl.pallas_call(ordered_body, in_specs=in_specs + [pl.BlockSpec(memory_space=pl.ANY)], **kw)
    return lambda *args: call(*args, after)


def _params(semantics=None):
    if semantics is None:
        return pltpu.CompilerParams(vmem_limit_bytes=VMEM_LIMIT_BYTES)
    return pltpu.CompilerParams(dimension_semantics=semantics, vmem_limit_bytes=VMEM_LIMIT_BYTES)


def _pick(dim, pref, align):
    best = None
    t = align
    while t <= min(dim, pref):
        if dim % t == 0:
            best = t
        t += align
    return dim if best is None else best


def _sigmoid(x):
    return 1.0 / (1.0 + jnp.exp(-x))


def _matmul(a, b, *, mode, name, out_dtype, tm, tn, tk, bias=None, residual=None, b_inner=True,
            b_row_off=0, b_rows=None, after=None):
    if mode == "nn":
        m, k = a.shape
        n = b.shape[1]
    elif mode == "nt":
        m, k = a.shape
        n = b.shape[0] if b_rows is None else b_rows
    else:
        k, m = a.shape
        n = b.shape[1]
    tm = _pick(m, tm, 16)
    tn = _pick(math.gcd(n, b_row_off) if mode == "nt" and b_row_off else n, tn, LANES)
    tk = _pick(k, tk, LANES if mode != "tn" else 16)
    nm, nn, nk = m // tm, n // tn, k // tk
    if mode == "nt":
        assert b_row_off % tn == 0
    off = b_row_off // tn if mode == "nt" else 0

    if b_inner:
        grid = (nm, nn, nk)
        ij = lambda g0, g1: (g0, g1)
    else:
        grid = (nn, nm, nk)
        ij = lambda g0, g1: (g1, g0)

    if mode == "tn":
        a_spec = pl.BlockSpec((tk, tm), lambda g0, g1, kk: (kk, ij(g0, g1)[0]))
    else:
        a_spec = pl.BlockSpec((tm, tk), lambda g0, g1, kk: (ij(g0, g1)[0], kk))
    if mode == "nt":
        b_spec = pl.BlockSpec((tn, tk), lambda g0, g1, kk: (ij(g0, g1)[1] + off, kk))
    else:
        b_spec = pl.BlockSpec((tk, tn), lambda g0, g1, kk: (kk, ij(g0, g1)[1]))
    o_spec = pl.BlockSpec((tm, tn), lambda g0, g1, kk: ij(g0, g1))
    in_specs = [a_spec, b_spec]
    args = [a, b]
    if bias is not None:
        in_specs.append(pl.BlockSpec((1, tn), lambda g0, g1, kk: (0, ij(g0, g1)[1])))
        args.append(bias)
    if residual is not None:
        in_specs.append(o_spec)
        args.append(residual)
    dims = {"nn": (((1,), (0,)), ((), ())), "nt": (((1,), (1,)), ((), ())), "tn": (((0,), (0,)), ((), ()))}[mode]
    has_bias, has_res = bias is not None, residual is not None

    def body(*refs):
        a_ref, b_ref = refs[0], refs[1]
        pos = 2
        bias_ref = res_ref = None
        if has_bias:
            bias_ref = refs[pos]
            pos += 1
        if has_res:
            res_ref = refs[pos]
            pos += 1
        o_ref = refs[pos]
        acc_ref = refs[pos + 1] if nk > 1 else None

        def finish(acc):
            if has_bias:
                acc = acc + bias_ref[...]
            if has_res:
                acc = acc + res_ref[...]
            o_ref[...] = acc.astype(out_dtype)

        p = lax.dot_general(a_ref[...], b_ref[...], dims, preferred_element_type=F32)
        if nk == 1:
            finish(p)
        else:
            kk = pl.program_id(2)

            @pl.when(kk == 0)
            def _():
                acc_ref[...] = p

            @pl.when(kk > 0)
            def _():
                acc_ref[...] += p

            @pl.when(kk == nk - 1)
            def _():
                finish(acc_ref[...])

    return _pcall(
        body, after=after, name=name, grid=grid, in_specs=in_specs, out_specs=o_spec,
        out_shape=jax.ShapeDtypeStruct((m, n), out_dtype),
        scratch_shapes=[pltpu.VMEM((tm, tn), F32)] if nk > 1 else [],
        compiler_params=_params(("parallel", "parallel", "arbitrary")),
    )(*args)


def _row_spec(width, col=0):
    return pl.BlockSpec((BLOCK, width), lambda i: (i, col))


def _const_spec(shape):
    nd = len(shape)
    return pl.BlockSpec(shape, lambda i: (0,) * nd)


def _prep(x, meta_full, g, after=None):
    s, d = x.shape
    lp = s + BLOCK
    nb = lp // BLOCK

    def body(x_ref, meta_ref, g_ref, h_ref, u_ref):
        i = pl.program_id(0)

        @pl.when(i == 0)
        def _():
            h_ref[0:PAD_ROWS, :] = jnp.zeros((PAD_ROWS, d), F32)
            h_ref[PAD_ROWS:BLOCK, :] = meta_ref[...]

        @pl.when(i > 0)
        def _():
            h_ref[...] = x_ref[...]

        h = h_ref[...]
        r = lax.rsqrt(jnp.mean(h * h, axis=-1, keepdims=True) + EPS)
        u_ref[...] = (h * r * g_ref[...]).astype(BF16)

    return _pcall(
        body, after=after, name="prep_rmsnorm", grid=(nb,),
        in_specs=[pl.BlockSpec((BLOCK, d), lambda i: (jnp.maximum(i - 1, 0), 0)), _const_spec((N_META, d)), _const_spec((1, d))],
        out_specs=[_row_spec(d), _row_spec(d)],
        out_shape=[jax.ShapeDtypeStruct((lp, d), F32), jax.ShapeDtypeStruct((lp, d), BF16)],
        compiler_params=_params(("arbitrary",)),
    )(x, meta_full, g)


def _rmsnorm_fwd(h, g, name):
    lp, d = h.shape

    def body(h_ref, g_ref, u_ref):
        x = h_ref[...]
        r = lax.rsqrt(jnp.mean(x * x, axis=-1, keepdims=True) + EPS)
        u_ref[...] = (x * r * g_ref[...]).astype(BF16)

    return _pcall(
        body, name=name, grid=(lp // BLOCK,), in_specs=[_row_spec(d), _const_spec((1, d))], out_specs=_row_spec(d),
        out_shape=jax.ShapeDtypeStruct((lp, d), BF16), compiler_params=_params(("parallel",)),
    )(h, g)


def _rms_bwd_core(dy, x, g):
    r = lax.rsqrt(jnp.mean(x * x, axis=-1, keepdims=True) + EPS)
    xhat = x * r
    dxhat = dy * g
    dx = r * (dxhat - xhat * jnp.mean(dxhat * xhat, axis=-1, keepdims=True))
    return dx, jnp.sum(dy * xhat, axis=0, keepdims=True)


def _rmsnorm_bwd(dy, h, g, dres, name):
    lp, d = h.shape

    def body(dy_ref, h_ref, g_ref, dres_ref, dh_ref, dhb_ref, dg_ref):
        i = pl.program_id(0)
        dx, dg = _rms_bwd_core(dy_ref[...], h_ref[...], g_ref[...])
        dh = dres_ref[...] + dx
        dh_ref[...] = dh
        dhb_ref[...] = dh.astype(BF16)

        @pl.when(i == 0)
        def _():
            dg_ref[...] = jnp.zeros_like(dg_ref)

        dg_ref[...] += dg

    return _pcall(
        body, name=name, grid=(lp // BLOCK,),
        in_specs=[_row_spec(d), _row_spec(d), _const_spec((1, d)), _row_spec(d)],
        out_specs=[_row_spec(d), _row_spec(d), _const_spec((1, d))],
        out_shape=[jax.ShapeDtypeStruct((lp, d), F32), jax.ShapeDtypeStruct((lp, d), BF16), jax.ShapeDtypeStruct((1, d), F32)],
        compiler_params=_params(("arbitrary",)),
    )(dy, h, g, dres)


def _rmsnorm_bwd_first(dy, h, g, dres):
    lp, d = h.shape
    s = lp - BLOCK

    def body(dy_ref, h_ref, g_ref, dres_ref, gx_ref, dmeta_ref, dg_ref):
        i = pl.program_id(0)
        dx, dg = _rms_bwd_core(dy_ref[...], h_ref[...], g_ref[...])
        dh = dres_ref[...] + dx
        gx_ref[...] = dh

        @pl.when(i == 0)
        def _():
            dmeta_ref[...] = dh[PAD_ROWS:BLOCK, :]
            dg_ref[...] = jnp.zeros_like(dg_ref)

        dg_ref[...] += dg

    return _pcall(
        body, name="rmsnorm_bwd_first", grid=(lp // BLOCK,),
        in_specs=[_row_spec(d), _row_spec(d), _const_spec((1, d)), _row_spec(d)],
        out_specs=[pl.BlockSpec((BLOCK, d), lambda i: (jnp.maximum(i - 1, 0), 0)), _const_spec((N_META, d)), _const_spec((1, d))],
        out_shape=[jax.ShapeDtypeStruct((s, d), F32), jax.ShapeDtypeStruct((N_META, d), F32), jax.ShapeDtypeStruct((1, d), F32)],
        compiler_params=_params(("arbitrary",)),
    )(dy, h, g, dres)


def _final(h2, tgt, g):
    lp, d = h2.shape

    def body(h_ref, t_ref, g_ref, dh_ref, dhb_ref, loss_ref, dg_ref):
        i = pl.program_id(0)
        x = h_ref[...]
        gg = g_ref[...]
        r = lax.rsqrt(jnp.mean(x * x, axis=-1, keepdims=True) + EPS)
        xhat = x * r
        y = xhat * gg
        live = (i > 0).astype(F32)
        err = (y - t_ref[...]) * live
        dy = err * (1.0 / d)
        dxhat = dy * gg
        dh = r * (dxhat - xhat * jnp.mean(dxhat * xhat, axis=-1, keepdims=True))
        dh_ref[...] = dh
        dhb_ref[...] = dh.astype(BF16)

        @pl.when(i == 0)
        def _():
            loss_ref[...] = jnp.zeros_like(loss_ref)
            dg_ref[...] = jnp.zeros_like(dg_ref)

        row_loss = jnp.mean(err * err, axis=-1, keepdims=True)
        loss_ref[...] += 0.5 * jnp.sum(row_loss, axis=0, keepdims=True)
        dg_ref[...] += jnp.sum(dy * xhat, axis=0, keepdims=True)

    return _pcall(
        body, name="final_norm_loss", grid=(lp // BLOCK,),
        in_specs=[_row_spec(d), pl.BlockSpec((BLOCK, d), lambda i: (jnp.maximum(i - 1, 0), 0)), _const_spec((1, d))],
        out_specs=[_row_spec(d), _row_spec(d), _const_spec((1, LANES)), _const_spec((1, d))],
        out_shape=[jax.ShapeDtypeStruct((lp, d), F32), jax.ShapeDtypeStruct((lp, d), BF16),
                   jax.ShapeDtypeStruct((1, LANES), F32), jax.ShapeDtypeStruct((1, d), F32)],
        compiler_params=_params(("arbitrary",)),
    )(h2, tgt, g)


def _swap_halves(x):
    w = x.shape[1]
    lane = lax.broadcasted_iota(jnp.int32, x.shape, 1)
    first = (lane & (HEAD_DIM - 1)) < (HEAD_DIM // 2)
    return jnp.where(first, pltpu.roll(x, w - HEAD_DIM // 2, 1), pltpu.roll(x, HEAD_DIM // 2, 1))


def _rope_tables(lp):
    pos = jnp.maximum(jnp.arange(lp, dtype=jnp.int32) - PAD_ROWS, 0).astype(F32)
    inv_freq = ROPE_THETA ** (-jnp.arange(0, HEAD_DIM, 2, dtype=F32) / HEAD_DIM)
    ang = pos[:, None] * inv_freq[None, :]
    c, s = jnp.cos(ang), jnp.sin(ang)
    reps = LANES // HEAD_DIM
    return jnp.tile(jnp.concatenate([c, c], axis=1), (1, reps)), jnp.tile(jnp.concatenate([-s, s], axis=1), (1, reps))


def _rope_fwd(zq, zkv, ctab, stab, after=None):
    lp = zq.shape[0]
    nb = lp // BLOCK
    back = lambda s: (jnp.maximum(s - 1, 0), 0)

    def body(zq_ref, zkv_ref, c_ref, s_ref, q_ref, k_ref, v_ref):
        step = pl.program_id(0)
        c128, s128 = c_ref[...], s_ref[...]

        def rope(x):
            reps = x.shape[1] // LANES
            return x * jnp.tile(c128, (1, reps)) + _swap_halves(x) * jnp.tile(s128, (1, reps))

        q_ref[...] = (rope(zq_ref[...].astype(F32)) * ATTN_SCALE).astype(BF16)
        kv = zkv_ref[...].astype(F32)
        k = rope(kv[:, :KV_DIM])
        v = kv[:, KV_DIM:]

        @pl.when(step == 0)
        def _():
            k_ref[...] = jnp.zeros_like(k_ref)
            v_ref[...] = jnp.zeros_like(v_ref)

        @pl.when(step > 0)
        def _():
            for h in range(N_KV_HEADS):
                k_ref[h] = k[:, h * HEAD_DIM:(h + 1) * HEAD_DIM].astype(BF16)
                v_ref[h] = v[:, h * HEAD_DIM:(h + 1) * HEAD_DIM].astype(BF16)

    kv_spec = pl.BlockSpec((N_KV_HEADS, BLOCK, HEAD_DIM), lambda s: (0, s, 0))
    return _pcall(
        body, after=after, name="rope_fwd", grid=(nb + 1,),
        in_specs=[pl.BlockSpec((BLOCK, Q_DIM), back), pl.BlockSpec((BLOCK, 2 * KV_DIM), back),
                  pl.BlockSpec((BLOCK, LANES), back), pl.BlockSpec((BLOCK, LANES), back)],
        out_specs=[pl.BlockSpec((BLOCK, Q_DIM), back), kv_spec, kv_spec],
        out_shape=[jax.ShapeDtypeStruct((lp, Q_DIM), BF16),
                   jax.ShapeDtypeStruct((N_KV_HEADS, lp + BLOCK, HEAD_DIM), BF16),
                   jax.ShapeDtypeStruct((N_KV_HEADS, lp + BLOCK, HEAD_DIM), BF16)],
        compiler_params=_params(("arbitrary",)),
    )(zq, zkv, ctab, stab)


def _rope_bwd(dq, dk, dv, dkm, dvm, ctab, stab):
    lp = dq.shape[0]
    width = Q_DIM + 2 * KV_DIM
    head_spec = pl.BlockSpec((N_KV_HEADS, BLOCK, HEAD_DIM), lambda i: (0, i, 0))
    meta_spec = _const_spec((N_KV_HEADS, BLOCK, HEAD_DIM))

    def body(dq_ref, dk_ref, dv_ref, dkm_ref, dvm_ref, c_ref, s_ref, dz_ref, sum_ref, kbuf, vbuf):
        i = pl.program_id(0)
        c128, s128 = c_ref[...], s_ref[...]
        first = (i == 0).astype(F32)

        def rope_t(x):
            reps = x.shape[1] // LANES
            return x * jnp.tile(c128, (1, reps)) + _swap_halves(x * jnp.tile(s128, (1, reps)))

        for h in range(N_KV_HEADS):
            kbuf[:, h * HEAD_DIM:(h + 1) * HEAD_DIM] = dk_ref[h] + first * dkm_ref[h]
            vbuf[:, h * HEAD_DIM:(h + 1) * HEAD_DIM] = dv_ref[h] + first * dvm_ref[h]
        dzq = rope_t(dq_ref[...] * ATTN_SCALE)
        dzk = rope_t(kbuf[...])
        dzv = vbuf[...]
        dz_ref[:, 0:Q_DIM] = dzq.astype(BF16)
        dz_ref[:, Q_DIM:Q_DIM + KV_DIM] = dzk.astype(BF16)
        dz_ref[:, Q_DIM + KV_DIM:width] = dzv.astype(BF16)

        @pl.when(i == 0)
        def _():
            sum_ref[...] = jnp.zeros_like(sum_ref)

        sum_ref[:, 0:Q_DIM] += jnp.sum(dzq, axis=0, keepdims=True)
        sum_ref[:, Q_DIM:Q_DIM + KV_DIM] += jnp.sum(dzk, axis=0, keepdims=True)
        sum_ref[:, Q_DIM + KV_DIM:width] += jnp.sum(dzv, axis=0, keepdims=True)

    return _pcall(
        body, name="rope_bwd", grid=(lp // BLOCK,),
        in_specs=[_row_spec(Q_DIM), head_spec, head_spec, meta_spec, meta_spec, _row_spec(LANES), _row_spec(LANES)],
        out_specs=[_row_spec(width), _const_spec((1, width))],
        out_shape=[jax.ShapeDtypeStruct((lp, width), BF16), jax.ShapeDtypeStruct((1, width), F32)],
        scratch_shapes=[pltpu.VMEM((BLOCK, KV_DIM), F32), pltpu.VMEM((BLOCK, KV_DIM), F32)],
        compiler_params=_params(("arbitrary",)),
    )(dq, dk, dv, dkm, dvm, ctab, stab)


def _attn_bias(i):
    r = lax.broadcasted_iota(jnp.int32, (BLOCK, 3 * BLOCK), 0)
    c = lax.broadcasted_iota(jnp.int32, (BLOCK, 3 * BLOCK), 1)
    qp = i * BLOCK + r - PAD_ROWS
    kp = (i - 1) * BLOCK + c - PAD_ROWS
    band = (c < 2 * BLOCK) & (kp >= N_META) & (kp <= qp) & (qp - kp < WINDOW)
    mp = c - 2 * BLOCK - PAD_ROWS
    meta = (c >= 2 * BLOCK) & (mp >= 0) & (mp <= qp)
    return jnp.where(band | meta, 0.0, NEG).astype(F32)


HALF = BLOCK // 2
HALF_KEYS = 2 * BLOCK


def _half_keys(prev, own, meta, half):
    if half == 0:
        return jnp.concatenate([prev, own[0:HALF], meta[HALF:BLOCK]], axis=0)
    return jnp.concatenate([prev[HALF:BLOCK], own, meta[HALF:BLOCK]], axis=0)


def _half_bias(i, half):
    r = lax.broadcasted_iota(jnp.int32, (HALF, HALF_KEYS), 0) + half * HALF
    c = lax.broadcasted_iota(jnp.int32, (HALF, HALF_KEYS), 1)
    n_prev = BLOCK - half * HALF
    qp = i * BLOCK + r - PAD_ROWS
    kp = jnp.where(c < n_prev, (i - 1) * BLOCK + c + half * HALF, i * BLOCK + c - n_prev) - PAD_ROWS
    band = (c < HALF_KEYS - HALF) & (kp >= N_META) & (kp <= qp) & (qp - kp < WINDOW)
    mp = c - (HALF_KEYS - HALF) + HALF - PAD_ROWS
    meta = (c >= HALF_KEYS - HALF) & (mp >= 0) & (mp <= qp)
    return jnp.where(band | meta, 0.0, NEG).astype(F32)


def _half_rows(ref, heads, half):
    rows = slice(half * HALF, (half + 1) * HALF)
    return jnp.concatenate([ref[rows, n * HEAD_DIM:(n + 1) * HEAD_DIM] for n in heads], axis=0)


def _half_sinks(sink_ref, heads):
    return jnp.concatenate([jnp.broadcast_to(sink_ref[0:1, n:n + 1], (HALF, 1)) for n in heads], axis=0)


def _stack_heads(ref, h):
    return jnp.concatenate(
        [ref[:, (h * GROUP + g) * HEAD_DIM:(h * GROUP + g + 1) * HEAD_DIM] for g in range(GROUP)], axis=0)


def _attn_probs(qs, k3, bias8, sink):
    s = lax.dot_general(qs, k3, (((1,), (1,)), ((), ())), preferred_element_type=F32) + bias8
    m = jnp.maximum(jnp.max(s, axis=1, keepdims=True), sink)
    p = jnp.exp(s - m)
    ps = jnp.exp(sink - m)
    inv = 1.0 / (jnp.sum(p, axis=1, keepdims=True) + ps)
    return p * inv, ps * inv


def _sink_column(sink_ref, h):
    return jnp.concatenate(
        [jnp.broadcast_to(sink_ref[0:1, h * GROUP + g:h * GROUP + g + 1], (BLOCK, 1)) for g in range(GROUP)], axis=0)


def _attn_fwd(q, k_sh, v_sh, sinks):
    lp = q.shape[0]
    nb = lp // BLOCK
    kv = lambda f: pl.BlockSpec((N_KV_HEADS, BLOCK, HEAD_DIM), f)

    def body(q_ref, kp_ref, kc_ref, km_ref, vp_ref, vc_ref, vm_ref, sink_ref, o_ref):
        i = pl.program_id(0)
        for half in range(2):
            bias = jnp.tile(_half_bias(i, half), (GROUP, 1))
            rows = slice(half * HALF, (half + 1) * HALF)
            for h in range(N_KV_HEADS):
                heads = range(h * GROUP, (h + 1) * GROUP)
                keys = _half_keys(kp_ref[h], kc_ref[h], km_ref[h], half)
                vals = _half_keys(vp_ref[h], vc_ref[h], vm_ref[h], half)
                p, _ = _attn_probs(_half_rows(q_ref, heads, half), keys, bias, _half_sinks(sink_ref, heads))
                o = jnp.dot(p.astype(BF16), vals, preferred_element_type=F32)
                for j, n in enumerate(heads):
                    o_ref[rows, n * HEAD_DIM:(n + 1) * HEAD_DIM] = o[j * HALF:(j + 1) * HALF].astype(BF16)

    prev, cur, meta = (lambda i: (0, i, 0)), (lambda i: (0, i + 1, 0)), (lambda i: (0, 1, 0))
    return _pcall(
        body, name="attn_fwd", grid=(nb,),
        in_specs=[_row_spec(Q_DIM), kv(prev), kv(cur), kv(meta), kv(prev), kv(cur), kv(meta), _const_spec((1, N_Q_HEADS))],
        out_specs=_row_spec(Q_DIM), out_shape=jax.ShapeDtypeStruct((lp, Q_DIM), BF16),
        compiler_params=_params(("parallel",)),
    )(q, k_sh, k_sh, k_sh, v_sh, v_sh, v_sh, sinks)


def _attn_bwd(q, k_sh, v_sh, sinks, do):
    lp = q.shape[0]
    nb = lp // BLOCK
    kv = lambda f: pl.BlockSpec((N_KV_HEADS, BLOCK, HEAD_DIM), f)
    cl = lambda s: jnp.minimum(s, nb - 1)

    def body(q_ref, do_ref, kp_ref, kc_ref, km_ref, vp_ref, vc_ref, vm_ref, sink_ref,
             dq_ref, dk_ref, dv_ref, dkm_ref, dvm_ref, dsink_ref, carry_k, carry_v):
        step = pl.program_id(0)

        @pl.when(step == 0)
        def _():
            carry_k[...] = jnp.zeros_like(carry_k)
            carry_v[...] = jnp.zeros_like(carry_v)
            dkm_ref[...] = jnp.zeros_like(dkm_ref)
            dvm_ref[...] = jnp.zeros_like(dvm_ref)
            dsink_ref[...] = jnp.zeros_like(dsink_ref)

        @pl.when(step < nb)
        def _():
            bias8 = jnp.tile(_attn_bias(step), (GROUP, 1))
            lane = lax.broadcasted_iota(jnp.int32, (1, LANES), 1)
            dsink = jnp.zeros((1, LANES), F32)
            for h in range(N_KV_HEADS):
                k3 = jnp.concatenate([kp_ref[h], kc_ref[h], km_ref[h]], axis=0)
                v3 = jnp.concatenate([vp_ref[h], vc_ref[h], vm_ref[h]], axis=0)
                qs = _stack_heads(q_ref, h)
                dos = _stack_heads(do_ref, h)
                p, psink = _attn_probs(qs, k3, bias8, _sink_column(sink_ref, h))
                dp = lax.dot_general(dos, v3, (((1,), (1,)), ((), ())), preferred_element_type=F32)
                delta = jnp.sum(p * dp, axis=1, keepdims=True)
                ds = (p * (dp - delta)).astype(BF16)
                dsk = -psink * delta
                for g in range(GROUP):
                    val = jnp.sum(dsk[g * BLOCK:(g + 1) * BLOCK], axis=0, keepdims=True)
                    dsink = dsink + jnp.where(lane == h * GROUP + g, val, 0.0)
                dqs = jnp.dot(ds, k3, preferred_element_type=F32)
                for g in range(GROUP):
                    n = h * GROUP + g
                    dq_ref[:, n * HEAD_DIM:(n + 1) * HEAD_DIM] = dqs[g * BLOCK:(g + 1) * BLOCK]
                dk3 = lax.dot_general(ds, qs, (((0,), (0,)), ((), ())), preferred_element_type=F32)
                dv3 = lax.dot_general(p.astype(BF16), dos, (((0,), (0,)), ((), ())), preferred_element_type=F32)
                dk_ref[h] = carry_k[h] + dk3[0:BLOCK]
                dv_ref[h] = carry_v[h] + dv3[0:BLOCK]
                carry_k[h] = dk3[BLOCK:2 * BLOCK]
                carry_v[h] = dv3[BLOCK:2 * BLOCK]
                dkm_ref[h] += dk3[2 * BLOCK:3 * BLOCK]
                dvm_ref[h] += dv3[2 * BLOCK:3 * BLOCK]
            dsink_ref[...] += dsink

        @pl.when(step == nb)
        def _():
            dk_ref[...] = carry_k[...]
            dv_ref[...] = carry_v[...]

    prev, cur, meta = (lambda s: (0, cl(s), 0)), (lambda s: (0, cl(s) + 1, 0)), (lambda s: (0, 1, 0))
    lag = lambda s: (0, jnp.maximum(s - 1, 0), 0)
    head_shape = jax.ShapeDtypeStruct((N_KV_HEADS, lp, HEAD_DIM), F32)
    meta_shape = jax.ShapeDtypeStruct((N_KV_HEADS, BLOCK, HEAD_DIM), F32)
    return _pcall(
        body, name="attn_bwd", grid=(nb + 1,),
        in_specs=[pl.BlockSpec((BLOCK, Q_DIM), lambda s: (cl(s), 0)), pl.BlockSpec((BLOCK, Q_DIM), lambda s: (cl(s), 0)),
                  kv(prev), kv(cur), kv(meta), kv(prev), kv(cur), kv(meta), _const_spec((1, N_Q_HEADS))],
        out_specs=[pl.BlockSpec((BLOCK, Q_DIM), lambda s: (cl(s), 0)), kv(lag), kv(lag),
                   _const_spec((N_KV_HEADS, BLOCK, HEAD_DIM)), _const_spec((N_KV_HEADS, BLOCK, HEAD_DIM)), _const_spec((1, LANES))],
        out_shape=[jax.ShapeDtypeStruct((lp, Q_DIM), F32), head_shape, head_shape, meta_shape, meta_shape,
                   jax.ShapeDtypeStruct((1, LANES), F32)],
        scratch_shapes=[pltpu.VMEM((N_KV_HEADS, BLOCK, HEAD_DIM), F32), pltpu.VMEM((N_KV_HEADS, BLOCK, HEAD_DIM), F32)],
        compiler_params=_params(("arbitrary",)),
    )(q, do, k_sh, k_sh, k_sh, v_sh, v_sh, v_sh, sinks)


CONV_CHUNK = 256


SUBLANES = 8
SH_BASE = BLOCK - 4 * SUBLANES
SH_ROWS = BLOCK + 3 * SUBLANES
DSH_ROWS = SH_ROWS


def _shifted_windows(src, sh, base, rows):
    for b in range(1, SUBLANES):
        sh[b] = src[base + b:base + b + rows, :]


def _window(src, sh, base, start, cols):
    a, b = divmod(start - base, SUBLANES)
    if b == 0:
        return src[start:start + BLOCK, cols]
    return sh[b, SUBLANES * a:SUBLANES * a + BLOCK, cols]


def _glu_masked(a_ref, g_ref, base):
    rows = base + lax.broadcasted_iota(jnp.int32, (BLOCK, 1), 0)
    return jnp.where(rows >= PAD_ROWS, a_ref[...].astype(F32) * _sigmoid(g_ref[...].astype(F32)), 0.0)


def _conv_fwd(zc, conv_w, conv_b, ln_g, ln_b, after=None):
    lp = zc.shape[0]
    cd = zc.shape[1] // 2
    nb = lp // BLOCK
    chunk = min(CONV_CHUNK, cd)
    back = lambda col: (lambda i: (jnp.maximum(i - 1, 0), col))
    lo = BLOCK - (CONV_WIDTH - 1)

    def body(ap_ref, gp_ref, ac_ref, gc_ref, w_ref, b_ref, lg_ref, lb_ref, co_ref, c2_ref, ext, sh):
        i = pl.program_id(0)
        ext[0:BLOCK, :] = _glu_masked(ap_ref, gp_ref, (i - 1) * BLOCK)
        ext[BLOCK:2 * BLOCK, :] = _glu_masked(ac_ref, gc_ref, i * BLOCK)
        _shifted_windows(ext, sh, SH_BASE, SH_ROWS)
        for c0 in range(0, cd, chunk):
            cols = slice(c0, c0 + chunk)
            acc = jnp.zeros((BLOCK, chunk), F32)
            for k in range(CONV_WIDTH):
                acc = acc + _window(ext, sh, SH_BASE, lo + k, cols) * w_ref[k:k + 1, cols]
            co_ref[:, cols] = acc + b_ref[:, cols]
        x = co_ref[...]
        mu = jnp.mean(x, axis=-1, keepdims=True)
        xc = x - mu
        r = lax.rsqrt(jnp.mean(xc * xc, axis=-1, keepdims=True) + EPS)
        y = xc * r * lg_ref[...] + lb_ref[...]
        c2_ref[...] = (y * _sigmoid(y)).astype(BF16)

    return _pcall(
        body, after=after, name="conv_fwd", grid=(nb,),
        in_specs=[pl.BlockSpec((BLOCK, cd), back(0)), pl.BlockSpec((BLOCK, cd), back(1)), _row_spec(cd, 0), _row_spec(cd, 1),
                  _const_spec((CONV_ROWS, cd)), _const_spec((1, cd)), _const_spec((1, cd)), _const_spec((1, cd))],
        out_specs=[_row_spec(cd), _row_spec(cd)],
        out_shape=[jax.ShapeDtypeStruct((lp, cd), F32), jax.ShapeDtypeStruct((lp, cd), BF16)],
        scratch_shapes=[pltpu.VMEM((2 * BLOCK, cd), F32), pltpu.VMEM((SUBLANES, SH_ROWS, cd), F32)],
        compiler_params=_params(("arbitrary",)),
    )(zc, zc, zc, zc, conv_w, conv_b, ln_g, ln_b)


def _conv_bwd_norm(dc2, conv_out, ln_g, ln_b):
    lp, cd = conv_out.shape

    def body(d_ref, x_ref, lg_ref, lb_ref, dco_ref, dlg_ref, dlb_ref, dcb_ref):
        i = pl.program_id(0)
        x = x_ref[...]
        g = lg_ref[...]
        mu = jnp.mean(x, axis=-1, keepdims=True)
        xc = x - mu
        r = lax.rsqrt(jnp.mean(xc * xc, axis=-1, keepdims=True) + EPS)
        xhat = xc * r
        y = xhat * g + lb_ref[...]
        sg = _sigmoid(y)
        dy = d_ref[...] * (sg * (1.0 + y * (1.0 - sg)))
        dxhat = dy * g
        dx = r * (dxhat - jnp.mean(dxhat, axis=-1, keepdims=True) - xhat * jnp.mean(dxhat * xhat, axis=-1, keepdims=True))
        dco_ref[...] = dx

        @pl.when(i == 0)
        def _():
            dlg_ref[...] = jnp.zeros_like(dlg_ref)
            dlb_ref[...] = jnp.zeros_like(dlb_ref)
            dcb_ref[...] = jnp.zeros_like(dcb_ref)

        dlg_ref[...] += jnp.sum(dy * xhat, axis=0, keepdims=True)
        dlb_ref[...] += jnp.sum(dy, axis=0, keepdims=True)
        dcb_ref[...] += jnp.sum(dx, axis=0, keepdims=True)

    vec = jax.ShapeDtypeStruct((1, cd), F32)
    return _pcall(
        body, name="conv_bwd_norm", grid=(lp // BLOCK,),
        in_specs=[_row_spec(cd), _row_spec(cd), _const_spec((1, cd)), _const_spec((1, cd))],
        out_specs=[_row_spec(cd), _const_spec((1, cd)), _const_spec((1, cd)), _const_spec((1, cd))],
        out_shape=[jax.ShapeDtypeStruct((lp, cd), F32), vec, vec, vec],
        compiler_params=_params(("arbitrary",)),
    )(dc2, conv_out, ln_g, ln_b)


def _conv_bwd_taps(dco, zc, conv_w):
    lp, cd = dco.shape
    nb = lp // BLOCK
    chunk = min(CONV_CHUNK, cd)
    back = lambda col: (lambda i: (jnp.maximum(i - 1, 0), col))
    fwd = lambda i: (jnp.minimum(i + 1, nb - 1), 0)
    lo = BLOCK - (CONV_WIDTH - 1)

    def body(dc_ref, dn_ref, ap_ref, gp_ref, ac_ref, gc_ref, w_ref, dz_ref, sum_ref, dw_ref, ext, dext, dcb, sh, dsh):
        i = pl.program_id(0)
        ext[0:BLOCK, :] = _glu_masked(ap_ref, gp_ref, (i - 1) * BLOCK)
        ext[BLOCK:2 * BLOCK, :] = _glu_masked(ac_ref, gc_ref, i * BLOCK)
        dext[0:BLOCK, :] = dc_ref[...]
        dext[BLOCK:2 * BLOCK, :] = dn_ref[...] * (i < nb - 1).astype(F32)
        _shifted_windows(ext, sh, SH_BASE, SH_ROWS)
        _shifted_windows(dext, dsh, 0, DSH_ROWS)

        @pl.when(i == 0)
        def _():
            dw_ref[...] = jnp.zeros_like(dw_ref)
            sum_ref[...] = jnp.zeros_like(sum_ref)

        for c0 in range(0, cd, chunk):
            cols = slice(c0, c0 + chunk)
            dcur = dext[0:BLOCK, cols]
            acc = jnp.zeros((BLOCK, chunk), F32)
            for k in range(CONV_WIDTH):
                s = CONV_WIDTH - 1 - k
                acc = acc + _window(dext, dsh, 0, s, cols) * w_ref[k:k + 1, cols]
                dw_ref[k:k + 1, cols] += jnp.sum(dcur * _window(ext, sh, SH_BASE, lo + k, cols), axis=0, keepdims=True)
            dcb[:, cols] = acc
        rows = i * BLOCK + lax.broadcasted_iota(jnp.int32, (BLOCK, 1), 0)
        dc = jnp.where(rows >= PAD_ROWS, dcb[...], 0.0)
        a = ac_ref[...].astype(F32)
        sg = _sigmoid(gc_ref[...].astype(F32))
        da = dc * sg
        dg = dc * a * sg * (1.0 - sg)
        dz_ref[:, 0:cd] = da.astype(BF16)
        dz_ref[:, cd:2 * cd] = dg.astype(BF16)
        sum_ref[:, 0:cd] += jnp.sum(da, axis=0, keepdims=True)
        sum_ref[:, cd:2 * cd] += jnp.sum(dg, axis=0, keepdims=True)

    return _pcall(
        body, name="conv_bwd_taps", grid=(nb,),
        in_specs=[_row_spec(cd), pl.BlockSpec((BLOCK, cd), fwd),
                  pl.BlockSpec((BLOCK, cd), back(0)), pl.BlockSpec((BLOCK, cd), back(1)), _row_spec(cd, 0), _row_spec(cd, 1),
                  _const_spec((CONV_ROWS, cd))],
        out_specs=[_row_spec(2 * cd), _const_spec((1, 2 * cd)), _const_spec((CONV_ROWS, cd))],
        out_shape=[jax.ShapeDtypeStruct((lp, 2 * cd), BF16), jax.ShapeDtypeStruct((1, 2 * cd), F32),
                   jax.ShapeDtypeStruct((CONV_ROWS, cd), F32)],
        scratch_shapes=[pltpu.VMEM((2 * BLOCK, cd), F32), pltpu.VMEM((2 * BLOCK, cd), F32), pltpu.VMEM((BLOCK, cd), F32),
                        pltpu.VMEM((SUBLANES, SH_ROWS, cd), F32), pltpu.VMEM((SUBLANES, DSH_ROWS, cd), F32)],
        compiler_params=_params(("arbitrary",)),
    )(dco, dco, zc, zc, zc, zc, conv_w)


def _gate_fwd(a, b, zg, after=None):
    lp, d = a.shape

    def body(a_ref, b_ref, ga_ref, gb_ref, m_ref):
        ga, gb = ga_ref[...].astype(F32), gb_ref[...].astype(F32)
        m_ref[...] = (_sigmoid(ga) * a_ref[...].astype(F32) + _sigmoid(gb) * b_ref[...].astype(F32)).astype(BF16)

    return _pcall(
        body, after=after, name="gate_fwd", grid=(lp // BLOCK,),
        in_specs=[_row_spec(d), _row_spec(d), _row_spec(d, 0), _row_spec(d, 1)], out_specs=_row_spec(d),
        out_shape=jax.ShapeDtypeStruct((lp, d), BF16), compiler_params=_params(("parallel",)),
    )(a, b, zg, zg)


def _gate_bwd(dm, a, b, zg):
    lp, d = a.shape

    def body(dm_ref, a_ref, b_ref, ga_ref, gb_ref, da_ref, db_ref, dz_ref, sum_ref, dbias_ref):
        i = pl.program_id(0)
        dm_ = dm_ref[...].astype(F32)
        sa = _sigmoid(ga_ref[...].astype(F32))
        sb = _sigmoid(gb_ref[...].astype(F32))
        db = dm_ * sb
        dga = dm_ * a_ref[...].astype(F32) * sa * (1.0 - sa)
        dgb = dm_ * b_ref[...].astype(F32) * sb * (1.0 - sb)
        da_ref[...] = (dm_ * sa).astype(BF16)
        db_ref[...] = db.astype(BF16)
        dz_ref[:, 0:d] = dga.astype(BF16)
        dz_ref[:, d:2 * d] = dgb.astype(BF16)

        @pl.when(i == 0)
        def _():
            sum_ref[...] = jnp.zeros_like(sum_ref)
            dbias_ref[...] = jnp.zeros_like(dbias_ref)

        sum_ref[:, 0:d] += jnp.sum(dga, axis=0, keepdims=True)
        sum_ref[:, d:2 * d] += jnp.sum(dgb, axis=0, keepdims=True)
        dbias_ref[...] += jnp.sum(db, axis=0, keepdims=True)

    return _pcall(
        body, name="gate_bwd", grid=(lp // BLOCK,),
        in_specs=[_row_spec(d), _row_spec(d), _row_spec(d), _row_spec(d, 0), _row_spec(d, 1)],
        out_specs=[_row_spec(d), _row_spec(d), _row_spec(2 * d), _const_spec((1, 2 * d)), _const_spec((1, d))],
        out_shape=[jax.ShapeDtypeStruct((lp, d), BF16), jax.ShapeDtypeStruct((lp, d), BF16), jax.ShapeDtypeStruct((lp, 2 * d), BF16),
                   jax.ShapeDtypeStruct((1, 2 * d), F32), jax.ShapeDtypeStruct((1, d), F32)],
        compiler_params=_params(("arbitrary",)),
    )(dm, a, b, zg, zg)


def _gate_up_swiglu(u2, wgu_t, *, row_block, tm, tn, name, filled=None, after=None):
    m, k = u2.shape
    f = wgu_t.shape[0] // 2
    tn = _pick(f, tn, LANES)
    nj = f // tn
    dims = (((1,), (1,)), ((), ()))
    n_filled = 0 if filled is None else 3

    def body(*refs):
        a_ref, wg_ref, wu_ref = refs[:3]
        g_ref, u_ref, act_ref = refs[3 + n_filled:]
        a = a_ref[...]
        g = lax.dot_general(a, wg_ref[...], dims, preferred_element_type=F32).astype(BF16)
        up = lax.dot_general(a, wu_ref[...], dims, preferred_element_type=F32).astype(BF16)
        g_ref[...] = g
        u_ref[...] = up
        gf = g.astype(F32)
        act_ref[...] = (gf * _sigmoid(gf) * up.astype(F32)).astype(BF16)

    out = pl.BlockSpec((tm, tn), lambda j: (row_block, j))
    shape = jax.ShapeDtypeStruct((m, f), BF16)
    return _pcall(
        body, after=after, name=name, grid=(nj,),
        in_specs=[pl.BlockSpec((tm, k), lambda j: (row_block, 0)), pl.BlockSpec((tn, k), lambda j: (j, 0)),
                  pl.BlockSpec((tn, k), lambda j: (j + nj, 0))] + [pl.BlockSpec(memory_space=pl.ANY)] * n_filled,
        out_specs=[out, out, out], out_shape=[shape, shape, shape],
        input_output_aliases={3 + i: i for i in range(n_filled)}, compiler_params=_params(("arbitrary",)),
    )(u2, wgu_t, wgu_t, *(filled or ()))


def _swiglu_bwd(dact, g, up):
    lp, f = dact.shape

    def body(d_ref, g_ref, u_ref, o_ref):
        g = g_ref[...].astype(F32)
        d = d_ref[...].astype(F32)
        sg = _sigmoid(g)
        o_ref[:, 0:f] = (d * u_ref[...].astype(F32) * (sg * (1.0 + g * (1.0 - sg)))).astype(BF16)
        o_ref[:, f:2 * f] = (d * g * sg).astype(BF16)

    return _pcall(
        body, name="swiglu_bwd", grid=(lp // BLOCK,), in_specs=[_row_spec(f), _row_spec(f), _row_spec(f)],
        out_specs=_row_spec(2 * f), out_shape=jax.ShapeDtypeStruct((lp, 2 * f), BF16), compiler_params=_params(("parallel",)),
    )(dact, g, up)


ANY = pl.BlockSpec(memory_space=pl.ANY)


def _all_gather_rows(x, name, after=None):
    r, c = x.shape

    def body(x_ref, out_ref, send_sems, recv_sems, local_sem):
        mx, my, mc = lax.axis_index("x"), lax.axis_index("y"), lax.axis_index("c")
        me, sibling = (mx, my, mc), (mx, my, 1 - mc)
        chips = [(1 - mx, my), (mx, 1 - my), (1 - mx, 1 - my)]

        def rows(px, py, pc):
            return out_ref.at[pl.ds((4 * px + 2 * py + pc) * r, r), :]

        def copy(k, block, to, src=None):
            return pltpu.make_async_remote_copy(
                src_ref=rows(*block) if src is None else src, dst_ref=rows(*block),
                send_sem=send_sems.at[k], recv_sem=recv_sems.at[k], device_id=to, device_id_type=MESH)

        mine = pltpu.make_async_copy(x_ref, rows(*me), local_sem)
        mine.start()
        first = [copy(0, me, sibling, src=x_ref)]
        first += [copy(1 + j, me, (*chip, mc), src=x_ref) for j, chip in enumerate(chips)]
        for cp in first:
            cp.start()
        passed = [copy(4 + j, (*chip, mc), sibling) for j, chip in enumerate(chips)]
        for j, chip in enumerate(chips):
            copy(1 + j, (*chip, mc), me).wait_recv()
            passed[j].start()
        copy(0, sibling, me).wait_recv()
        for j, chip in enumerate(chips):
            copy(4 + j, (*chip, 1 - mc), me).wait_recv()
        for cp in first + passed:
            cp.wait_send()
        mine.wait()

    return _pcall(
        body, after=after, name=name, in_specs=[ANY], out_specs=ANY, out_shape=jax.ShapeDtypeStruct((N_DEV * r, c), x.dtype),
        scratch_shapes=[pltpu.SemaphoreType.DMA((7,)), pltpu.SemaphoreType.DMA((7,)), pltpu.SemaphoreType.DMA(())],
    )(x)


HBM = pl.BlockSpec(memory_space=pltpu.HBM)
SEM = pl.BlockSpec(memory_space=pltpu.SEMAPHORE)
IN_FLIGHT = pltpu.CompilerParams(has_side_effects=pltpu.SideEffectType.DATAFLOW_SIDE_EFFECTING)


def _place_rows(shard, after, name):
    r, c = shard.shape
    tr = _pick(r, max(16, ELEMENTWISE_BLOCK_BYTES // (4 * c)), 16)
    steps = r // tr
    dev = (4 * lax.axis_index("x") + 2 * lax.axis_index("y") + lax.axis_index("c")).astype(jnp.int32).reshape(1)

    def body(dev_ref, x_ref, after_ref, o_ref):
        o_ref[...] = x_ref[...].astype(BF16)

    return _pcall(
        body, name=name,
        grid_spec=pltpu.PrefetchScalarGridSpec(
            num_scalar_prefetch=1, grid=(steps,),
            in_specs=[pl.BlockSpec((tr, c), lambda i, dev_ref: (i, 0)), pl.BlockSpec(memory_space=pl.ANY)],
            out_specs=pl.BlockSpec((tr, c), lambda i, dev_ref: (dev_ref[0] * steps + i, 0))),
        out_shape=jax.ShapeDtypeStruct((N_DEV * r, c), BF16), compiler_params=_params(("parallel",)),
    )(dev, shard, after)


def _rows_start(full, plan, name, after=None):
    r = full.shape[0] // N_DEV
    n = len(plan(0, 0, 0))

    ordered = after is not None

    def body(*refs):
        full_ref, (send_sems, recv_sems) = refs[0], refs[1 + ordered:3 + ordered]
        mx, my, mc = lax.axis_index("x"), lax.axis_index("y"), lax.axis_index("c")
        for k, ((bx, by, bc), target) in enumerate(plan(mx, my, mc)):
            rows = full_ref.at[pl.ds((4 * bx + 2 * by + bc) * r, r), :]
            pltpu.make_async_remote_copy(
                src_ref=rows, dst_ref=rows, send_sem=send_sems.at[k], recv_sem=recv_sems.at[k],
                device_id=target, device_id_type=MESH).start()

    return pl.pallas_call(
        body, name=name, in_specs=[HBM] + [pl.BlockSpec(memory_space=pl.ANY)] * ordered, out_specs=(SEM, SEM, HBM),
        out_shape=(pltpu.SemaphoreType.DMA((n,)), pltpu.SemaphoreType.DMA((n,)), pltpu.HBM(full.shape, full.dtype)),
        input_output_aliases={0: 2}, compiler_params=IN_FLIGHT,
    )(pltpu.with_memory_space_constraint(full, pltpu.HBM), *([after] if ordered else []))


def _rows_wait(started, after, name):
    send_sem, recv_sem, full = started
    r = full.shape[0] // N_DEV
    n = send_sem.shape[0]

    def body(full_ref, send_ref, recv_ref, after_ref, out_ref):
        mx, my, mc = lax.axis_index("x"), lax.axis_index("y"), lax.axis_index("c")
        block = full_ref.at[pl.ds(0, r), :]
        for k in range(n):
            cp = pltpu.make_async_remote_copy(
                src_ref=block, dst_ref=block, send_sem=send_ref.at[k], recv_sem=recv_ref.at[k],
                device_id=(mx, my, mc), device_id_type=MESH)
            cp.wait_send()
            cp.wait_recv()

    return pl.pallas_call(
        body, name=name, in_specs=[HBM, SEM, SEM, pl.BlockSpec(memory_space=pl.ANY)], out_specs=HBM,
        out_shape=pltpu.HBM(full.shape, full.dtype), input_output_aliases={0: 0}, compiler_params=IN_FLIGHT,
    )(full, send_sem, recv_sem, after)


def _plan_direct(mx, my, mc):
    me = (mx, my, mc)
    return [(me, (mx, my, 1 - mc)), (me, (1 - mx, my, mc)), (me, (mx, 1 - my, mc)), (me, (1 - mx, 1 - my, mc))]


def _plan_pass_on(mx, my, mc):
    sibling = (mx, my, 1 - mc)
    return [((1 - mx, my, mc), sibling), ((mx, 1 - my, mc), sibling), ((1 - mx, 1 - my, mc), sibling)]


def _plan_neighbours(mx, my, mc):
    me = (mx, my, mc)
    return [(me, (mx, my, 1 - mc)), (me, (1 - mx, my, mc)), (me, (mx, 1 - my, mc))]


def _plan_relay(mx, my, mc):
    sibling = (mx, my, 1 - mc)
    source = ((mx + 1 - mc) % 2, (my + mc) % 2, mc)
    target = ((mx + mc) % 2, (my + 1 - mc) % 2, mc)
    return [((1 - mx, my, mc), sibling), ((mx, 1 - my, mc), sibling), (source, target)]


def _plan_pass_on_diagonal(mx, my, mc):
    return [((1 - mx, 1 - my, mc), (mx, my, 1 - mc))]


def _pair_exchange_start(g, name):
    r = g.shape[0] // N_DEV
    c = g.shape[1]
    land = (len(CHIPS), r, c)

    def body(g_ref, land_ref, send_sems, recv_sems, g_out, land_out):
        mx, my, mc = lax.axis_index("x"), lax.axis_index("y"), lax.axis_index("c")
        for j, (px, py) in enumerate(CHIPS):
            pltpu.make_async_remote_copy(
                src_ref=g_ref.at[pl.ds((4 * px + 2 * py + 1 - mc) * r, r), :], dst_ref=land_ref.at[j],
                send_sem=send_sems.at[j], recv_sem=recv_sems.at[j], device_id=(mx, my, 1 - mc), device_id_type=MESH).start()

    return pl.pallas_call(
        body, name=name, in_specs=[HBM, HBM], out_specs=(SEM, SEM, HBM, HBM),
        out_shape=(pltpu.SemaphoreType.DMA((4,)), pltpu.SemaphoreType.DMA((4,)), pltpu.HBM(g.shape, g.dtype), pltpu.HBM(land, g.dtype)),
        input_output_aliases={0: 2, 1: 3}, compiler_params=IN_FLIGHT,
    )(pltpu.with_memory_space_constraint(g, pltpu.HBM), pltpu.with_memory_space_constraint(lax.empty(land, g.dtype), pltpu.HBM))


def _pair_exchange_wait(send_sem, recv_sem, g, land, after, name):
    def body(g_ref, land_ref, send_ref, recv_ref, after_ref, g_out, land_out):
        mx, my, mc = lax.axis_index("x"), lax.axis_index("y"), lax.axis_index("c")
        for j in range(len(CHIPS)):
            cp = pltpu.make_async_remote_copy(
                src_ref=land_ref.at[0], dst_ref=land_ref.at[0], send_sem=send_ref.at[j], recv_sem=recv_ref.at[j],
                device_id=(mx, my, mc), device_id_type=MESH)
            cp.wait_send()
            cp.wait_recv()

    return pl.pallas_call(
        body, name=name, in_specs=[HBM, HBM, SEM, SEM, pl.BlockSpec(memory_space=pl.ANY)], out_specs=(HBM, HBM),
        out_shape=(pltpu.HBM(g.shape, g.dtype), pltpu.HBM(land.shape, land.dtype)), input_output_aliases={0: 0, 1: 1},
        compiler_params=IN_FLIGHT,
    )(g, land, send_sem, recv_sem, after)


def _chip_exchange_start(ps, after, name):
    def body(ps_ref, rx_ref, after_ref, send_sems, recv_sems, ps_out, rx_out):
        mx, my, mc = lax.axis_index("x"), lax.axis_index("y"), lax.axis_index("c")
        chips = [(1 - mx, my), (mx, 1 - my), (1 - mx, 1 - my)]
        for k, (px, py) in enumerate(chips):
            pltpu.make_async_remote_copy(
                src_ref=ps_ref.at[2 * px + py], dst_ref=rx_ref.at[2 * mx + my], send_sem=send_sems.at[k], recv_sem=recv_sems.at[k],
                device_id=(px, py, mc), device_id_type=MESH).start()

    return pl.pallas_call(
        body, name=name, in_specs=[HBM, HBM, pl.BlockSpec(memory_space=pl.ANY)], out_specs=(SEM, SEM, HBM, HBM),
        out_shape=(pltpu.SemaphoreType.DMA((3,)), pltpu.SemaphoreType.DMA((3,)), pltpu.HBM(ps.shape, ps.dtype), pltpu.HBM(ps.shape, ps.dtype)),
        input_output_aliases={0: 2, 1: 3}, compiler_params=IN_FLIGHT,
    )(pltpu.with_memory_space_constraint(ps, pltpu.HBM), pltpu.with_memory_space_constraint(lax.empty(ps.shape, ps.dtype), pltpu.HBM), after)


def _chip_exchange_wait(send_sem, recv_sem, ps, rx, after, name):
    def body(ps_ref, rx_ref, send_ref, recv_ref, after_ref, ps_out, rx_out):
        mx, my, mc = lax.axis_index("x"), lax.axis_index("y"), lax.axis_index("c")
        for k in range(3):
            cp = pltpu.make_async_remote_copy(
                src_ref=ps_ref.at[0], dst_ref=rx_ref.at[0], send_sem=send_ref.at[k], recv_sem=recv_ref.at[k],
                device_id=(mx, my, mc), device_id_type=MESH)
            cp.wait_send()
            cp.wait_recv()

    return pl.pallas_call(
        body, name=name, in_specs=[HBM, HBM, SEM, SEM, pl.BlockSpec(memory_space=pl.ANY)], out_specs=(HBM, HBM),
        out_shape=(pltpu.HBM(ps.shape, ps.dtype), pltpu.HBM(rx.shape, rx.dtype)), input_output_aliases={0: 0, 1: 1},
        compiler_params=IN_FLIGHT,
    )(ps, rx, send_sem, recv_sem, after)


def _sum_chips(ps, rx, name):
    n, r, c = rx.shape
    tr = _pick(r, max(16, 4 * ELEMENTWISE_BLOCK_BYTES // (4 * n * c)), 16)
    chip =(2 * lax.axis_index("x") + lax.axis_index("y")).astype(jnp.int32).reshape(1)

    def body(chip_ref, own_ref, x_ref, o_ref):
        me = chip_ref[0]
        own = own_ref[0].astype(F32)
        acc = jnp.where(me == 0, own, x_ref[0].astype(F32))
        for j in range(1, n):
            acc = acc + jnp.where(me == j, own, x_ref[j].astype(F32))
        o_ref[...] = acc

    return _pcall(
        body, name=name,
        grid_spec=pltpu.PrefetchScalarGridSpec(
            num_scalar_prefetch=1, grid=(r // tr,),
            in_specs=[pl.BlockSpec((1, tr, c), lambda i, chip_ref: (chip_ref[0], i, 0)), pl.BlockSpec((n, tr, c), lambda i, chip_ref: (0, i, 0))],
            out_specs=pl.BlockSpec((tr, c), lambda i, chip_ref: (i, 0))),
        out_shape=jax.ShapeDtypeStruct((r, c), F32), compiler_params=_params(("parallel",)),
    )(chip, ps, rx)


def _pair_exchange(g, name):
    r = g.shape[0] // N_DEV
    c = g.shape[1]

    def body(g_ref, theirs_ref, send_sems, recv_sems):
        mx, my, mc = lax.axis_index("x"), lax.axis_index("y"), lax.axis_index("c")
        sibling = (mx, my, 1 - mc)
        copies = []
        for j, (px, py) in enumerate(CHIPS):
            give = g_ref.at[pl.ds((4 * px + 2 * py + 1 - mc) * r, r), :]
            rc = pltpu.make_async_remote_copy(
                src_ref=give, dst_ref=theirs_ref.at[j], send_sem=send_sems.at[j], recv_sem=recv_sems.at[j],
                device_id=sibling, device_id_type=MESH)
            rc.start()
            copies.append(rc)
        for cp in copies:
            cp.wait()

    return _pcall(
        body, name=name, in_specs=[ANY], out_specs=ANY, out_shape=jax.ShapeDtypeStruct((len(CHIPS), r, c), g.dtype),
        scratch_shapes=[pltpu.SemaphoreType.DMA((4,)), pltpu.SemaphoreType.DMA((4,))],
    )(g)


def _pair_sum(g, theirs, name):
    nch, r, c = theirs.shape
    tr = _pick(r, max(16, 3 * ELEMENTWISE_BLOCK_BYTES // (2 * c)), 16)
    core = lax.axis_index("c").astype(jnp.int32).reshape(1)

    def body(core_ref, a_ref, b_ref, o_ref):
        o_ref[...] = (a_ref[...].astype(F32) + b_ref[...].astype(F32)).astype(o_ref.dtype)

    spec = pl.BlockSpec((1, tr, c), lambda j, i, core_ref: (j, i, 0))
    own = pl.BlockSpec((1, tr, c), lambda j, i, core_ref: (2 * j + core_ref[0], i, 0))
    return _pcall(
        body, name=name,
        grid_spec=pltpu.PrefetchScalarGridSpec(num_scalar_prefetch=1, grid=(nch, r // tr), in_specs=[own, spec], out_specs=spec),
        out_shape=jax.ShapeDtypeStruct(theirs.shape, theirs.dtype), compiler_params=_params(("parallel", "parallel")),
    )(core, g.reshape(N_DEV, r, c), theirs)


def _sum_blocks(rx, name):
    n, r, c = rx.shape
    tr = _pick(r, max(8, ELEMENTWISE_BLOCK_BYTES // (4 * n * c)), 8)

    def body(x_ref, o_ref):
        acc = x_ref[0].astype(F32)
        for j in range(1, n):
            acc = acc + x_ref[j].astype(F32)
        o_ref[...] = acc

    return _pcall(
        body, name=name, grid=(r // tr,), in_specs=[pl.BlockSpec((n, tr, c), lambda i: (0, i, 0))],
        out_specs=pl.BlockSpec((tr, c), lambda i: (i, 0)), out_shape=jax.ShapeDtypeStruct((r, c), F32),
        compiler_params=_params(("parallel",)),
    )(rx)


def _adamw(w, g, m, v, name):
    r, c = w.shape
    tr = _pick(r, max(8, ELEMENTWISE_BLOCK_BYTES // (4 * c)), 8)
    c1 = 1.0 - ADAM_B1 ** ADAM_STEP
    c2 = 1.0 - ADAM_B2 ** ADAM_STEP

    def body(w_ref, g_ref, m_ref, v_ref, d_ref, nm_ref, nv_ref):
        gg = g_ref[...]
        nm = ADAM_B1 * m_ref[...] + (1.0 - ADAM_B1) * gg
        nv = ADAM_B2 * v_ref[...] + (1.0 - ADAM_B2) * (gg * gg)
        d_ref[...] = -ADAM_LR * ((nm / c1) / (jnp.sqrt(nv / c2) + ADAM_EPS) + ADAM_WD * w_ref[...])
        nm_ref[...] = nm
        nv_ref[...] = nv

    spec = pl.BlockSpec((tr, c), lambda i: (i, 0))
    shp = jax.ShapeDtypeStruct((r, c), F32)
    return _pcall(
        body, name=name, grid=(r // tr,), in_specs=[spec] * 4, out_specs=[spec] * 3, out_shape=[shp] * 3,
        compiler_params=_params(("parallel",)),
    )(w, g, m, v)


def _pack(parts):
    flat, layout, row = [], [], 0
    for p in parts:
        n = p.size
        rows = -(-n // LANES)
        flat.append(jnp.pad(p.reshape(-1).astype(F32), (0, rows * LANES - n)))
        layout.append((row, n, p.shape))
        row += rows
    total = -(-row // 8) * 8
    if total > row:
        flat.append(jnp.zeros(((total - row) * LANES,), F32))
    return jnp.concatenate(flat).reshape(total, LANES), layout


def _unpack(slab, layout):
    flat = slab.reshape(-1)
    return [flat[row * LANES:row * LANES + n].reshape(shape) for row, n, shape in layout]


def kernel(x, meta_tokens, mix_norm_g, w_in, b_in, attn_sinks, conv_w, conv_b, conv_ln_g, conv_ln_b, w_attn_o, w_conv_o, b_conv_o, w_out, ffn_norm_g, w_gate_up, w_down, final_norm_g, loss_target, m_meta_tokens, m_mix_norm_g, m_w_in, m_b_in, m_attn_sinks, m_conv_w, m_conv_b, m_conv_ln_g, m_conv_ln_b, m_w_attn_o, m_w_conv_o, m_b_conv_o, m_w_out, m_ffn_norm_g, m_w_gate_up, m_w_down, m_final_norm_g, v_meta_tokens, v_mix_norm_g, v_w_in, v_b_in, v_attn_sinks, v_conv_w, v_conv_b, v_conv_ln_g, v_conv_ln_b, v_w_attn_o, v_w_conv_o, v_b_conv_o, v_w_out, v_ffn_norm_g, v_w_gate_up, v_w_down, v_final_norm_g):
    xs = x[0]
    tgt = loss_target[0]
    s, d = xs.shape
    lp = s + BLOCK
    cd = conv_b.shape[1]
    ffn = w_down.shape[1] * N_DEV
    dev = 4 * lax.axis_index("x") + 2 * lax.axis_index("y") + lax.axis_index("c")
    cw_cols = conv_w.shape[3]
    meta_cols = meta_tokens.shape[1]

    small, small_layout = _pack([meta_tokens, jnp.pad(conv_w[0, :, 0, :], ((0, CONV_ROWS - CONV_WIDTH), (0, 0)))])
    small_flat = _all_gather_rows(small, "gather_small")
    small_all = small_flat.reshape(N_DEV, *small.shape)
    meta_parts, cw_parts = zip(*[_unpack(small_all[j], small_layout) for j in range(N_DEV)])
    meta_full = jnp.concatenate(meta_parts, axis=1)
    conv_w_full = jnp.concatenate(cw_parts, axis=1)
    shards = ((w_in[0].T, "w_in"), (w_attn_o[0].T, "w_attn_o"), (w_conv_o[0].T, "w_conv_o"), (w_out[0], "w_out"),
              (w_gate_up[0].T, "w_gate_up"), (w_down[0], "w_down"))
    first = _rows_start(_place_rows(shards[0][0], small_flat, "place_w_in"), _plan_neighbours, "gather_start_w_in")
    placed, tok = [], first[2]
    for shard, name in shards[1:]:
        tok = _place_rows(shard, tok, "place_" + name)
        placed.append(tok)
    relay = _rows_start(_rows_wait(first, tok, "gather_wait_w_in"), _plan_relay, "gather_relay_start_w_in")
    h0, u = _prep(xs, meta_full, mix_norm_g, after=relay[2])
    diagonal = _rows_start(_rows_wait(relay, u, "gather_relay_wait_w_in"), _plan_pass_on_diagonal, "gather_diagonal_start_w_in")
    started, tok = [None], diagonal[2]
    for full, (_, name) in zip(placed, shards[1:]):
        started.append(_rows_start(full, _plan_direct, "gather_start_" + name, after=tok))
        tok = started[-1][2]
    win_t = _rows_wait(diagonal, tok, "gather_diagonal_wait_w_in")

    def arrived(w, after, name):
        return _rows_start(_rows_wait(started[w], after, "gather_wait_" + name), _plan_pass_on, "gather_pass_on_start_" + name)

    def whole(passing, after, name):
        return _rows_wait(passing, after, "gather_pass_on_wait_" + name)

    ctab, stab = _rope_tables(lp)
    mm = functools.partial(_matmul, tm=1056, tn=1024)

    bq, bkv, bc, bg = b_in[:, :Q_DIM], b_in[:, Q_DIM:Q_DIM + 2 * KV_DIM], b_in[:, Q_DIM + 2 * KV_DIM:Q_DIM + 2 * KV_DIM + 2 * cd], b_in[:, Q_DIM + 2 * KV_DIM + 2 * cd:]
    o_kv, o_c, o_g = Q_DIM, Q_DIM + 2 * KV_DIM, Q_DIM + 2 * KV_DIM + 2 * cd
    in_proj = functools.partial(_matmul, u, win_t, mode="nt", out_dtype=BF16, tm=2112, tn=512, tk=d)
    zq = in_proj(name="in_proj_q", bias=bq, b_row_off=0, b_rows=Q_DIM)
    zkv = in_proj(name="in_proj_kv", bias=bkv, b_row_off=o_kv, b_rows=2 * KV_DIM)
    zc = in_proj(name="in_proj_conv", bias=bc, b_row_off=o_c, b_rows=2 * cd)
    zg = in_proj(name="in_proj_gates", bias=bg, b_row_off=o_g, b_rows=2 * d)
    passing = arrived(1, zg, "w_attn_o")
    q_rot, k_sh, v_sh = _rope_fwd(zq, zkv, ctab, stab, after=passing[2])
    o = _attn_fwd(q_rot, k_sh, v_sh, attn_sinks)
    wao_t = whole(passing, o, "w_attn_o")
    br_a = mm(o, wao_t, mode="nt", name="attn_out_proj", out_dtype=BF16, tk=Q_DIM)
    passing = arrived(2, br_a, "w_conv_o")
    conv_out, c2 = _conv_fwd(zc, conv_w_full, conv_b, conv_ln_g, conv_ln_b, after=passing[2])
    wco_t = whole(passing, c2, "w_conv_o")
    br_b = mm(c2, wco_t, mode="nt", name="conv_out_proj", out_dtype=BF16, tk=cd, bias=b_conv_o)
    passing = arrived(3, br_b, "w_out")
    merged = _gate_fwd(br_a, br_b, zg, after=passing[2])
    wout = whole(passing, merged, "w_out")
    passing = arrived(4, wout, "w_gate_up")
    h1 = mm(merged, wout, mode="nn", name="mix_out_proj", out_dtype=F32, tn=512, tk=d, residual=h0, after=passing[2])
    u2 = _rmsnorm_fwd(h1, ffn_norm_g, "ffn_rmsnorm")
    wgu_t = whole(passing, u2, "w_gate_up")
    half_rows = _pick(lp, lp // 2, 16)
    ffn_in = _gate_up_swiglu(u2, wgu_t, row_block=0, tm=half_rows, tn=256, name="ffn_gate_up_swiglu_0")
    passing = arrived(5, ffn_in[2], "w_down")
    for rb in range(1, lp // half_rows):
        ffn_in = _gate_up_swiglu(u2, wgu_t, row_block=rb, tm=half_rows, tn=256, name="ffn_gate_up_swiglu_%d" % rb,
                                 filled=ffn_in, after=passing[2])
    gu_g, gu_u, act = ffn_in
    wdown = whole(passing, act, "w_down")
    h2 = mm(act, wdown, mode="nn", name="ffn_down", out_dtype=F32, tn=512, tk=ffn // 2, residual=h1)
    dh2, dh2_b, loss_part, d_final_g = _final(h2, tgt, final_norm_g.reshape(1, d))

    wgrad = functools.partial(_matmul, mode="tn", out_dtype=BF16, tk=lp, tn=2048, b_inner=False)
    in_flight = {}

    def scatter_begin(g, name):
        return _pair_exchange_start(g, "rs_" + name + "_pair_start")

    def scatter_go_on(pair, after, name):
        g, theirs = _pair_exchange_wait(pair[0], pair[1], pair[2], pair[3], after, "rs_" + name + "_pair_wait")
        ps = _pair_sum(g, theirs, "rs_" + name + "_pair_sum")
        in_flight[name] = _chip_exchange_start(ps, theirs, "rs_" + name + "_chip_start")
        return in_flight[name][2]

    g_wdown = wgrad(act, dh2_b, name="ffn_down_dw", tm=256)
    pair = scatter_begin(g_wdown, "w_down")
    dact = _matmul(dh2_b, wdown, mode="nt", name="ffn_down_dx", out_dtype=BF16, tm=2112, tn=256, tk=d, after=pair[2])
    tok = scatter_go_on(pair, dact, "w_down")
    dgu = _swiglu_bwd(dact, gu_g, gu_u)
    g_wgu_t = wgrad(dgu, u2, name="ffn_gate_up_dw", tm=512, after=tok)
    pair = scatter_begin(g_wgu_t, "w_gate_up")
    du2 = mm(dgu, wgu_t, mode="nn", name="ffn_gate_up_dx", out_dtype=F32, tn=512, tk=ffn // 2, after=pair[2])
    tok = scatter_go_on(pair, du2, "w_gate_up")
    dh1, dh1_b, d_ffn_g = _rmsnorm_bwd(du2, h1, ffn_norm_g, dh2, "ffn_rmsnorm_bwd")
    g_wout = wgrad(merged, dh1_b, name="mix_out_dw", tm=512, after=tok)
    pair = scatter_begin(g_wout, "w_out")
    dmerged = mm(dh1_b, wout, mode="nt", name="mix_out_dx", out_dtype=BF16, tk=d, after=pair[2])
    tok = scatter_go_on(pair, dmerged, "w_out")
    d_a, d_b, dz_g, sum_g, d_bco = _gate_bwd(dmerged, br_a, br_b, zg)
    g_wao_t = wgrad(d_a, o, name="attn_out_dw", tm=512, after=tok)
    pair = scatter_begin(g_wao_t, "w_attn_o")
    do = mm(d_a, wao_t, mode="nn", name="attn_out_dx", out_dtype=BF16, tk=d, after=pair[2])
    tok = scatter_go_on(pair, do, "w_attn_o")
    g_wco_t = wgrad(d_b, c2, name="conv_out_dw", tm=512, after=tok)
    pair = scatter_begin(g_wco_t, "w_conv_o")
    dc2 = mm(d_b, wco_t, mode="nn", name="conv_out_dx", out_dtype=F32, tk=d, after=pair[2])
    tok = scatter_go_on(pair, dc2, "w_conv_o")
    dq, dk, dv, dkm, dvm, d_sinks = _attn_bwd(q_rot, k_sh, v_sh, attn_sinks, do)
    dz_qkv, sum_qkv = _rope_bwd(dq, dk, dv, dkm, dvm, ctab, stab)
    dco, d_ln_g, d_ln_b, d_conv_b = _conv_bwd_norm(dc2, conv_out, conv_ln_g, conv_ln_b)
    dz_c, sum_c, d_conv_w = _conv_bwd_taps(dco, zc, conv_w_full)
    dz = jnp.concatenate([dz_qkv, dz_c, dz_g], axis=1)
    d_b_in = jnp.concatenate([sum_qkv, sum_c, sum_g], axis=1)
    in_dim = dz.shape[1]
    g_win_t = wgrad(dz, u, name="in_proj_dw", tm=512, after=tok)
    theirs = _pair_exchange(g_win_t, "rs_w_in_pair_exchange")
    in_flight["w_in"] = _chip_exchange_start(_pair_sum(g_win_t, theirs, "rs_w_in_pair_sum"), theirs, "rs_w_in_chip_start")
    du = mm(dz, win_t, mode="nn", name="in_proj_dx", out_dtype=F32, tk=in_dim // 4, after=in_flight["w_in"][2])
    grad_x, d_meta, d_mix_g = _rmsnorm_bwd_first(du, h0, mix_norm_g, dh1)

    weights = dict(meta_tokens=meta_tokens, mix_norm_g=mix_norm_g, w_in=w_in, b_in=b_in, attn_sinks=attn_sinks, conv_w=conv_w,
                   conv_b=conv_b, conv_ln_g=conv_ln_g, conv_ln_b=conv_ln_b, w_attn_o=w_attn_o, w_conv_o=w_conv_o, b_conv_o=b_conv_o,
                   w_out=w_out, ffn_norm_g=ffn_norm_g, w_gate_up=w_gate_up, w_down=w_down, final_norm_g=final_norm_g)
    m_in = dict(meta_tokens=m_meta_tokens, mix_norm_g=m_mix_norm_g, w_in=m_w_in, b_in=m_b_in, attn_sinks=m_attn_sinks, conv_w=m_conv_w,
                conv_b=m_conv_b, conv_ln_g=m_conv_ln_g, conv_ln_b=m_conv_ln_b, w_attn_o=m_w_attn_o, w_conv_o=m_w_conv_o,
                b_conv_o=m_b_conv_o, w_out=m_w_out, ffn_norm_g=m_ffn_norm_g, w_gate_up=m_w_gate_up, w_down=m_w_down,
                final_norm_g=m_final_norm_g)
    v_in = dict(meta_tokens=v_meta_tokens, mix_norm_g=v_mix_norm_g, w_in=v_w_in, b_in=v_b_in, attn_sinks=v_attn_sinks, conv_w=v_conv_w,
                conv_b=v_conv_b, conv_ln_g=v_conv_ln_g, conv_ln_b=v_conv_ln_b, w_attn_o=v_w_attn_o, w_conv_o=v_w_conv_o,
                b_conv_o=v_b_conv_o, w_out=v_w_out, ffn_norm_g=v_ffn_norm_g, w_gate_up=v_w_gate_up, w_down=v_w_down,
                final_norm_g=v_final_norm_g)
    names = list(weights)
    grads, delta, new_m, new_v = {}, {}, {}, {}
    tok = grad_x
    for n in ("w_down", "w_gate_up", "w_out", "w_attn_o", "w_conv_o", "w_in"):
        send_sem, recv_sem, ps, rx = in_flight[n]
        ps, rx = _chip_exchange_wait(send_sem, recv_sem, ps, rx, tok, "rs_" + n + "_chip_wait")
        g = _sum_chips(ps, rx, "rs_" + n + "_sum")
        if n in ("w_attn_o", "w_conv_o"):
            g = g.T
        oriented = (lambda a: a[0].T) if n in ("w_in", "w_gate_up") else (lambda a: a[0])
        back = (lambda a: a.T[None]) if n in ("w_in", "w_gate_up") else (lambda a: a[None])
        dl, nm, nv = _adamw(oriented(weights[n]), g, oriented(m_in[n]), oriented(v_in[n]), "adamw_" + n)
        grads[n], delta[n], new_m[n], new_v[n] = back(g), back(dl), back(nm), back(nv)
        tok = dl

    slab, slab_layout = _pack([loss_part[:, :1], d_mix_g, d_b_in, d_sinks[:, :N_Q_HEADS], d_conv_b, d_ln_g, d_ln_b, d_bco,
                               d_ffn_g, d_final_g, d_conv_w, d_meta])
    slab_all = _all_gather_rows(slab, "gather_small_grads", after=tok).reshape(N_DEV, *slab.shape)
    (loss, g_mix_g, g_b_in, g_sinks, g_conv_b, g_ln_g, g_ln_b, g_bco, g_ffn_g, g_final_g, g_conv_w_full, g_meta_full
     ) = _unpack(_sum_blocks(slab_all, "sum_small_grads"), slab_layout)
    g_conv_w = lax.dynamic_slice(g_conv_w_full, (0, dev * cw_cols), (CONV_WIDTH, cw_cols)).reshape(conv_w.shape)
    g_meta = lax.dynamic_slice(g_meta_full, (0, dev * meta_cols), (N_META, meta_cols))
    g_final_g = g_final_g.reshape(final_norm_g.shape)
    grads.update(meta_tokens=g_meta, mix_norm_g=g_mix_g, b_in=g_b_in, attn_sinks=g_sinks, conv_w=g_conv_w, conv_b=g_conv_b,
                 conv_ln_g=g_ln_g, conv_ln_b=g_ln_b, b_conv_o=g_bco, ffn_norm_g=g_ffn_g, final_norm_g=g_final_g)
    rest = [n for n in names if n not in delta]
    w_slab, rest_layout = _pack([weights[n] for n in rest])
    g_slab, _ = _pack([grads[n] for n in rest])
    m_slab, _ = _pack([m_in[n] for n in rest])
    v_slab, _ = _pack([v_in[n] for n in rest])
    dl, nm, nv = _adamw(w_slab, g_slab, m_slab, v_slab, "adamw_small")
    for n, a, b, c in zip(rest, _unpack(dl, rest_layout), _unpack(nm, rest_layout), _unpack(nv, rest_layout)):
        delta[n], new_m[n], new_v[n] = a, b, c

    return (loss.reshape(()), grad_x[None], *[grads[n] for n in names], *[delta[n] for n in names],
            *[new_m[n] for n in names], *[new_v[n] for n in names])
```

```python
import functools
import math

import jax
import jax.numpy as jnp
from jax import lax
from jax.experimental import pallas as pl
from jax.experimental.pallas import tpu as pltpu

F32 = jnp.float32
BF16 = jnp.bfloat16

N_DEV = 8
BLOCK = 128
N_META = 16
PAD_ROWS = BLOCK - N_META
HEAD_DIM = 64
N_Q_HEADS = 32
N_KV_HEADS = 4
GROUP = N_Q_HEADS // N_KV_HEADS
Q_DIM = N_Q_HEADS * HEAD_DIM
KV_DIM = N_KV_HEADS * HEAD_DIM
WINDOW = 128
CONV_WIDTH = 31
CONV_ROWS = 32
ROPE_THETA = 10000.0
EPS = 1e-6
ATTN_SCALE = HEAD_DIM ** -0.5
NEG = -1e30

ADAM_LR = 0.001
ADAM_B1 = 0.9
ADAM_B2 = 0.999
ADAM_EPS = 1e-08
ADAM_WD = 0.01
ADAM_STEP = 10

VMEM_LIMIT_BYTES = 56 * 1024 * 1024
LANES = 128
ELEMENTWISE_BLOCK_BYTES = 2 * 1024 * 1024
MESH = pl.DeviceIdType.MESH
CHIPS = ((0, 0), (0, 1), (1, 0), (1, 1))


def _pcall(body, after=None, **kw):
    if after is None:
        return pl.pallas_call(body, **kw)
    in_specs = list(kw.pop("in_specs"))
    n_in = len(in_specs)

    def ordered_body(*refs):
        return body(*refs[:n_in], *refs[n_in + 1:])

    call = pl.pallas_call(ordered_body, in_specs=in_specs + [pl.BlockSpec(memory_space=pl.ANY)], **kw)
    return lambda *args: call(*args, after)


def _params(semantics=None):
    if semantics is None:
        return pltpu.CompilerParams(vmem_limit_bytes=VMEM_LIMIT_BYTES)
    return pltpu.CompilerParams(dimension_semantics=semantics, vmem_limit_bytes=VMEM_LIMIT_BYTES)


def _pick(dim, pref, align):
    best = None
    t = align
    while t <= min(dim, pref):
        if dim % t == 0:
            best = t
        t += align
    return dim if best is None else best


def _sigmoid(x):
    return 1.0 / (1.0 + jnp.exp(-x))


def _matmul(a, b, *, mode, name, out_dtype, tm, tn, tk, bias=None, residual=None, b_inner=True,
            b_row_off=0, b_rows=None, after=None):
    if mode == "nn":
        m, k = a.shape
        n = b.shape[1]
    elif mode == "nt":
        m, k = a.shape
        n = b.shape[0] if b_rows is None else b_rows
    else:
        k, m = a.shape
        n = b.shape[1]
    tm = _pick(m, tm, 16)
    tn = _pick(math.gcd(n, b_row_off) if mode == "nt" and b_row_off else n, tn, LANES)
    tk = _pick(k, tk, LANES if mode != "tn" else 16)
    nm, nn, nk = m // tm, n // tn, k // tk
    if mode == "nt":
        assert b_row_off % tn == 0
    off = b_row_off // tn if mode == "nt" else 0

    if b_inner:
        grid = (nm, nn, nk)
        ij = lambda g0, g1: (g0, g1)
    else:
        grid = (nn, nm, nk)
        ij = lambda g0, g1: (g1, g0)

    if mode == "tn":
        a_spec = pl.BlockSpec((tk, tm), lambda g0, g1, kk: (kk, ij(g0, g1)[0]))
    else:
        a_spec = pl.BlockSpec((tm, tk), lambda g0, g1, kk: (ij(g0, g1)[0], kk))
    if mode == "nt":
        b_spec = pl.BlockSpec((tn, tk), lambda g0, g1, kk: (ij(g0, g1)[1] + off, kk))
    else:
        b_spec = pl.BlockSpec((tk, tn), lambda g0, g1, kk: (kk, ij(g0, g1)[1]))
    o_spec = pl.BlockSpec((tm, tn), lambda g0, g1, kk: ij(g0, g1))
    in_specs = [a_spec, b_spec]
    args = [a, b]
    if bias is not None:
        in_specs.append(pl.BlockSpec((1, tn), lambda g0, g1, kk: (0, ij(g0, g1)[1])))
        args.append(bias)
    if residual is not None:
        in_specs.append(o_spec)
        args.append(residual)
    dims = {"nn": (((1,), (0,)), ((), ())), "nt": (((1,), (1,)), ((), ())), "tn": (((0,), (0,)), ((), ()))}[mode]
    has_bias, has_res = bias is not None, residual is not None

    def body(*refs):
        a_ref, b_ref = refs[0], refs[1]
        pos = 2
        bias_ref = res_ref = None
        if has_bias:
            bias_ref = refs[pos]
            pos += 1
        if has_res:
            res_ref = refs[pos]
            pos += 1
        o_ref = refs[pos]
        acc_ref = refs[pos + 1] if nk > 1 else None

        def finish(acc):
            if has_bias:
                acc = acc + bias_ref[...]
            if has_res:
                acc = acc + res_ref[...]
            o_ref[...] = acc.astype(out_dtype)

        p = lax.dot_general(a_ref[...], b_ref[...], dims, preferred_element_type=F32)
        if nk == 1:
            finish(p)
        else:
            kk = pl.program_id(2)

            @pl.when(kk == 0)
            def _():
                acc_ref[...] = p

            @pl.when(kk > 0)
            def _():
                acc_ref[...] += p

            @pl.when(kk == nk - 1)
            def _():
                finish(acc_ref[...])

    return _pcall(
        body, after=after, name=name, grid=grid, in_specs=in_specs, out_specs=o_spec,
        out_shape=jax.ShapeDtypeStruct((m, n), out_dtype),
        scratch_shapes=[pltpu.VMEM((tm, tn), F32)] if nk > 1 else [],
        compiler_params=_params(("parallel", "parallel", "arbitrary")),
    )(*args)


def _row_spec(width, col=0):
    return pl.BlockSpec((BLOCK, width), lambda i: (i, col))


def _const_spec(shape):
    nd = len(shape)
    return pl.BlockSpec(shape, lambda i: (0,) * nd)


def _prep(x, meta_full, g, after=None):
    s, d = x.shape
    lp = s + BLOCK
    nb = lp // BLOCK

    def body(x_ref, meta_ref, g_ref, h_ref, u_ref):
        i = pl.program_id(0)

        @pl.when(i == 0)
        def _():
            h_ref[0:PAD_ROWS, :] = jnp.zeros((PAD_ROWS, d), F32)
            h_ref[PAD_ROWS:BLOCK, :] = meta_ref[...]

        @pl.when(i > 0)
        def _():
            h_ref[...] = x_ref[...]

        h = h_ref[...]
        r = lax.rsqrt(jnp.mean(h * h, axis=-1, keepdims=True) + EPS)
        u_ref[...] = (h * r * g_ref[...]).astype(BF16)

    return _pcall(
        body, after=after, name="prep_rmsnorm", grid=(nb,),
        in_specs=[pl.BlockSpec((BLOCK, d), lambda i: (jnp.maximum(i - 1, 0), 0)), _const_spec((N_META, d)), _const_spec((1, d))],
        out_specs=[_row_spec(d), _row_spec(d)],
        out_shape=[jax.ShapeDtypeStruct((lp, d), F32), jax.ShapeDtypeStruct((lp, d), BF16)],
        compiler_params=_params(("arbitrary",)),
    )(x, meta_full, g)


def _rmsnorm_fwd(h, g, name):
    lp, d = h.shape

    def body(h_ref, g_ref, u_ref):
        x = h_ref[...]
        r = lax.rsqrt(jnp.mean(x * x, axis=-1, keepdims=True) + EPS)
        u_ref[...] = (x * r * g_ref[...]).astype(BF16)

    return _pcall(
        body, name=name, grid=(lp // BLOCK,), in_specs=[_row_spec(d), _const_spec((1, d))], out_specs=_row_spec(d),
        out_shape=jax.ShapeDtypeStruct((lp, d), BF16), compiler_params=_params(("parallel",)),
    )(h, g)


def _rms_bwd_core(dy, x, g):
    r = lax.rsqrt(jnp.mean(x * x, axis=-1, keepdims=True) + EPS)
    xhat = x * r
    dxhat = dy * g
    dx = r * (dxhat - xhat * jnp.mean(dxhat * xhat, axis=-1, keepdims=True))
    return dx, jnp.sum(dy * xhat, axis=0, keepdims=True)


def _rmsnorm_bwd(dy, h, g, dres, name):
    lp, d = h.shape

    def body(dy_ref, h_ref, g_ref, dres_ref, dh_ref, dhb_ref, dg_ref):
        i = pl.program_id(0)
        dx, dg = _rms_bwd_core(dy_ref[...], h_ref[...], g_ref[...])
        dh = dres_ref[...] + dx
        dh_ref[...] = dh
        dhb_ref[...] = dh.astype(BF16)

        @pl.when(i == 0)
        def _():
            dg_ref[...] = jnp.zeros_like(dg_ref)

        dg_ref[...] += dg

    return _pcall(
        body, name=name, grid=(lp // BLOCK,),
        in_specs=[_row_spec(d), _row_spec(d), _const_spec((1, d)), _row_spec(d)],
        out_specs=[_row_spec(d), _row_spec(d), _const_spec((1, d))],
        out_shape=[jax.ShapeDtypeStruct((lp, d), F32), jax.ShapeDtypeStruct((lp, d), BF16), jax.ShapeDtypeStruct((1, d), F32)],
        compiler_params=_params(("arbitrary",)),
    )(dy, h, g, dres)


def _rmsnorm_bwd_first(dy, h, g, dres):
    lp, d = h.shape
    s = lp - BLOCK

    def body(dy_ref, h_ref, g_ref, dres_ref, gx_ref, dmeta_ref, dg_ref):
        i = pl.program_id(0)
        dx, dg = _rms_bwd_core(dy_ref[...], h_ref[...], g_ref[...])
        dh = dres_ref[...] + dx
        gx_ref[...] = dh

        @pl.when(i == 0)
        def _():
            dmeta_ref[...] = dh[PAD_ROWS:BLOCK, :]
            dg_ref[...] = jnp.zeros_like(dg_ref)

        dg_ref[...] += dg

    return _pcall(
        body, name="rmsnorm_bwd_first", grid=(lp // BLOCK,),
        in_specs=[_row_spec(d), _row_spec(d), _const_spec((1, d)), _row_spec(d)],
        out_specs=[pl.BlockSpec((BLOCK, d), lambda i: (jnp.maximum(i - 1, 0), 0)), _const_spec((N_META, d)), _const_spec((1, d))],
        out_shape=[jax.ShapeDtypeStruct((s, d), F32), jax.ShapeDtypeStruct((N_META, d), F32), jax.ShapeDtypeStruct((1, d), F32)],
        compiler_params=_params(("arbitrary",)),
    )(dy, h, g, dres)


def _final(h2, tgt, g):
    lp, d = h2.shape

    def body(h_ref, t_ref, g_ref, dh_ref, dhb_ref, loss_ref, dg_ref):
        i = pl.program_id(0)
        x = h_ref[...]
        gg = g_ref[...]
        r = lax.rsqrt(jnp.mean(x * x, axis=-1, keepdims=True) + EPS)
        xhat = x * r
        y = xhat * gg
        live = (i > 0).astype(F32)
        err = (y - t_ref[...]) * live
        dy = err * (1.0 / d)
        dxhat = dy * gg
        dh = r * (dxhat - xhat * jnp.mean(dxhat * xhat, axis=-1, keepdims=True))
        dh_ref[...] = dh
        dhb_ref[...] = dh.astype(BF16)

        @pl.when(i == 0)
        def _():
            loss_ref[...] = jnp.zeros_like(loss_ref)
            dg_ref[...] = jnp.zeros_like(dg_ref)

        row_loss = jnp.mean(err * err, axis=-1, keepdims=True)
        loss_ref[...] += 0.5 * jnp.sum(row_loss, axis=0, keepdims=True)
        dg_ref[...] += jnp.sum(dy * xhat, axis=0, keepdims=True)

    return _pcall(
        body, name="final_norm_loss", grid=(lp // BLOCK,),
        in_specs=[_row_spec(d), pl.BlockSpec((BLOCK, d), lambda i: (jnp.maximum(i - 1, 0), 0)), _const_spec((1, d))],
        out_specs=[_row_spec(d), _row_spec(d), _const_spec((1, LANES)), _const_spec((1, d))],
        out_shape=[jax.ShapeDtypeStruct((lp, d), F32), jax.ShapeDtypeStruct((lp, d), BF16),
                   jax.ShapeDtypeStruct((1, LANES), F32), jax.ShapeDtypeStruct((1, d), F32)],
        compiler_params=_params(("arbitrary",)),
    )(h2, tgt, g)


def _swap_halves(x):
    w = x.shape[1]
    lane = lax.broadcasted_iota(jnp.int32, x.shape, 1)
    first = (lane & (HEAD_DIM - 1)) < (HEAD_DIM // 2)
    return jnp.where(first, pltpu.roll(x, w - HEAD_DIM // 2, 1), pltpu.roll(x, HEAD_DIM // 2, 1))


def _rope_tables(lp):
    pos = jnp.maximum(jnp.arange(lp, dtype=jnp.int32) - PAD_ROWS, 0).astype(F32)
    inv_freq = ROPE_THETA ** (-jnp.arange(0, HEAD_DIM, 2, dtype=F32) / HEAD_DIM)
    ang = pos[:, None] * inv_freq[None, :]
    c, s = jnp.cos(ang), jnp.sin(ang)
    reps = LANES // HEAD_DIM
    return jnp.tile(jnp.concatenate([c, c], axis=1), (1, reps)), jnp.tile(jnp.concatenate([-s, s], axis=1), (1, reps))


def _rope_fwd(zq, zkv, ctab, stab, after=None):
    lp = zq.shape[0]
    nb = lp // BLOCK
    back = lambda s: (jnp.maximum(s - 1, 0), 0)

    def body(zq_ref, zkv_ref, c_ref, s_ref, q_ref, k_ref, v_ref):
        step = pl.program_id(0)
        c128, s128 = c_ref[...], s_ref[...]

        def rope(x):
            reps = x.shape[1] // LANES
            return x * jnp.tile(c128, (1, reps)) + _swap_halves(x) * jnp.tile(s128, (1, reps))

        q_ref[...] = (rope(zq_ref[...].astype(F32)) * ATTN_SCALE).astype(BF16)
        kv = zkv_ref[...].astype(F32)
        k = rope(kv[:, :KV_DIM])
        v = kv[:, KV_DIM:]

        @pl.when(step == 0)
        def _():
            k_ref[...] = jnp.zeros_like(k_ref)
            v_ref[...] = jnp.zeros_like(v_ref)

        @pl.when(step > 0)
        def _():
            for h in range(N_KV_HEADS):
                k_ref[h] = k[:, h * HEAD_DIM:(h + 1) * HEAD_DIM].astype(BF16)
                v_ref[h] = v[:, h * HEAD_DIM:(h + 1) * HEAD_DIM].astype(BF16)

    kv_spec = pl.BlockSpec((N_KV_HEADS, BLOCK, HEAD_DIM), lambda s: (0, s, 0))
    return _pcall(
        body, after=after, name="rope_fwd", grid=(nb + 1,),
        in_specs=[pl.BlockSpec((BLOCK, Q_DIM), back), pl.BlockSpec((BLOCK, 2 * KV_DIM), back),
                  pl.BlockSpec((BLOCK, LANES), back), pl.BlockSpec((BLOCK, LANES), back)],
        out_specs=[pl.BlockSpec((BLOCK, Q_DIM), back), kv_spec, kv_spec],
        out_shape=[jax.ShapeDtypeStruct((lp, Q_DIM), BF16),
                   jax.ShapeDtypeStruct((N_KV_HEADS, lp + BLOCK, HEAD_DIM), BF16),
                   jax.ShapeDtypeStruct((N_KV_HEADS, lp + BLOCK, HEAD_DIM), BF16)],
        compiler_params=_params(("arbitrary",)),
    )(zq, zkv, ctab, stab)


def _rope_bwd(dq, dk, dv, dkm, dvm, ctab, stab):
    lp = dq.shape[0]
    width = Q_DIM + 2 * KV_DIM
    head_spec = pl.BlockSpec((N_KV_HEADS, BLOCK, HEAD_DIM), lambda i: (0, i, 0))
    meta_spec = _const_spec((N_KV_HEADS, BLOCK, HEAD_DIM))

    def body(dq_ref, dk_ref, dv_ref, dkm_ref, dvm_ref, c_ref, s_ref, dz_ref, sum_ref, kbuf, vbuf):
        i = pl.program_id(0)
        c128, s128 = c_ref[...], s_ref[...]
        first = (i == 0).astype(F32)

        def rope_t(x):
            reps = x.shape[1] // LANES
            return x * jnp.tile(c128, (1, reps)) + _swap_halves(x * jnp.tile(s128, (1, reps)))

        for h in range(N_KV_HEADS):
            kbuf[:, h * HEAD_DIM:(h + 1) * HEAD_DIM] = dk_ref[h] + first * dkm_ref[h]
            vbuf[:, h * HEAD_DIM:(h + 1) * HEAD_DIM] = dv_ref[h] + first * dvm_ref[h]
        dzq = rope_t(dq_ref[...] * ATTN_SCALE)
        dzk = rope_t(kbuf[...])
        dzv = vbuf[...]
        dz_ref[:, 0:Q_DIM] = dzq.astype(BF16)
        dz_ref[:, Q_DIM:Q_DIM + KV_DIM] = dzk.astype(BF16)
        dz_ref[:, Q_DIM + KV_DIM:width] = dzv.astype(BF16)

        @pl.when(i == 0)
        def _():
            sum_ref[...] = jnp.zeros_like(sum_ref)

        sum_ref[:, 0:Q_DIM] += jnp.sum(dzq, axis=0, keepdims=True)
        sum_ref[:, Q_DIM:Q_DIM + KV_DIM] += jnp.sum(dzk, axis=0, keepdims=True)
        sum_ref[:, Q_DIM + KV_DIM:width] += jnp.sum(dzv, axis=0, keepdims=True)

    return _pcall(
        body, name="rope_bwd", grid=(lp // BLOCK,),
        in_specs=[_row_spec(Q_DIM), head_spec, head_spec, meta_spec, meta_spec, _row_spec(LANES), _row_spec(LANES)],
        out_specs=[_row_spec(width), _const_spec((1, width))],
        out_shape=[jax.ShapeDtypeStruct((lp, width), BF16), jax.ShapeDtypeStruct((1, width), F32)],
        scratch_shapes=[pltpu.VMEM((BLOCK, KV_DIM), F32), pltpu.VMEM((BLOCK, KV_DIM), F32)],
        compiler_params=_params(("arbitrary",)),
    )(dq, dk, dv, dkm, dvm, ctab, stab)


def _attn_bias(i):
    r = lax.broadcasted_iota(jnp.int32, (BLOCK, 3 * BLOCK), 0)
    c = lax.broadcasted_iota(jnp.int32, (BLOCK, 3 * BLOCK), 1)
    qp = i * BLOCK + r - PAD_ROWS
    kp = (i - 1) * BLOCK + c - PAD_ROWS
    band = (c < 2 * BLOCK) & (kp >= N_META) & (kp <= qp) & (qp - kp < WINDOW)
    mp = c - 2 * BLOCK - PAD_ROWS
    meta = (c >= 2 * BLOCK) & (mp >= 0) & (mp <= qp)
    return jnp.where(band | meta, 0.0, NEG).astype(F32)


HALF = BLOCK // 2
HALF_KEYS = 2 * BLOCK


def _half_keys(prev, own, meta, half):
    if half == 0:
        return jnp.concatenate([prev, own[0:HALF], meta[HALF:BLOCK]], axis=0)
    return jnp.concatenate([prev[HALF:BLOCK], own, meta[HALF:BLOCK]], axis=0)


def _half_bias(i, half):
    r = lax.broadcasted_iota(jnp.int32, (HALF, HALF_KEYS), 0) + half * HALF
    c = lax.broadcasted_iota(jnp.int32, (HALF, HALF_KEYS), 1)
    n_prev = BLOCK - half * HALF
    qp = i * BLOCK + r - PAD_ROWS
    kp = jnp.where(c < n_prev, (i - 1) * BLOCK + c + half * HALF, i * BLOCK + c - n_prev) - PAD_ROWS
    band = (c < HALF_KEYS - HALF) & (kp >= N_META) & (kp <= qp) & (qp - kp < WINDOW)
    mp = c - (HALF_KEYS - HALF) + HALF - PAD_ROWS
    meta = (c >= HALF_KEYS - HALF) & (mp >= 0) & (mp <= qp)
    return jnp.where(band | meta, 0.0, NEG).astype(F32)


def _half_rows(ref, heads, half):
    rows = slice(half * HALF, (half + 1) * HALF)
    return jnp.concatenate([ref[rows, n * HEAD_DIM:(n + 1) * HEAD_DIM] for n in heads], axis=0)


def _half_sinks(sink_ref, heads):
    return jnp.concatenate([jnp.broadcast_to(sink_ref[0:1, n:n + 1], (HALF, 1)) for n in heads], axis=0)


def _stack_heads(ref, h):
    return jnp.concatenate(
        [ref[:, (h * GROUP + g) * HEAD_DIM:(h * GROUP + g + 1) * HEAD_DIM] for g in range(GROUP)], axis=0)


def _attn_probs(qs, k3, bias8, sink):
    s = lax.dot_general(qs, k3, (((1,), (1,)), ((), ())), preferred_element_type=F32) + bias8
    m = jnp.maximum(jnp.max(s, axis=1, keepdims=True), sink)
    p = jnp.exp(s - m)
    ps = jnp.exp(sink - m)
    inv = 1.0 / (jnp.sum(p, axis=1, keepdims=True) + ps)
    return p * inv, ps * inv


def _sink_column(sink_ref, h):
    return jnp.concatenate(
        [jnp.broadcast_to(sink_ref[0:1, h * GROUP + g:h * GROUP + g + 1], (BLOCK, 1)) for g in range(GROUP)], axis=0)


def _attn_fwd(q, k_sh, v_sh, sinks):
    lp = q.shape[0]
    nb = lp // BLOCK
    kv = lambda f: pl.BlockSpec((N_KV_HEADS, BLOCK, HEAD_DIM), f)

    def body(q_ref, kp_ref, kc_ref, km_ref, vp_ref, vc_ref, vm_ref, sink_ref, o_ref):
        i = pl.program_id(0)
        for half in range(2):
            bias = jnp.tile(_half_bias(i, half), (GROUP, 1))
            rows = slice(half * HALF, (half + 1) * HALF)
            for h in range(N_KV_HEADS):
                heads = range(h * GROUP, (h + 1) * GROUP)
                keys = _half_keys(kp_ref[h], kc_ref[h], km_ref[h], half)
                vals = _half_keys(vp_ref[h], vc_ref[h], vm_ref[h], half)
                p, _ = _attn_probs(_half_rows(q_ref, heads, half), keys, bias, _half_sinks(sink_ref, heads))
                o = jnp.dot(p.astype(BF16), vals, preferred_element_type=F32)
                for j, n in enumerate(heads):
                    o_ref[rows, n * HEAD_DIM:(n + 1) * HEAD_DIM] = o[j * HALF:(j + 1) * HALF].astype(BF16)

    prev, cur, meta = (lambda i: (0, i, 0)), (lambda i: (0, i + 1, 0)), (lambda i: (0, 1, 0))
    return _pcall(
        body, name="attn_fwd", grid=(nb,),
        in_specs=[_row_spec(Q_DIM), kv(prev), kv(cur), kv(meta), kv(prev), kv(cur), kv(meta), _const_spec((1, N_Q_HEADS))],
        out_specs=_row_spec(Q_DIM), out_shape=jax.ShapeDtypeStruct((lp, Q_DIM), BF16),
        compiler_params=_params(("parallel",)),
    )(q, k_sh, k_sh, k_sh, v_sh, v_sh, v_sh, sinks)


def _attn_bwd(q, k_sh, v_sh, sinks, do):
    lp = q.shape[0]
    nb = lp // BLOCK
    kv = lambda f: pl.BlockSpec((N_KV_HEADS, BLOCK, HEAD_DIM), f)
    cl = lambda s: jnp.minimum(s, nb - 1)

    def body(q_ref, do_ref, kp_ref, kc_ref, km_ref, vp_ref, vc_ref, vm_ref, sink_ref,
             dq_ref, dk_ref, dv_ref, dkm_ref, dvm_ref, dsink_ref, carry_k, carry_v):
        step = pl.program_id(0)

        @pl.when(step == 0)
        def _():
            carry_k[...] = jnp.zeros_like(carry_k)
            carry_v[...] = jnp.zeros_like(carry_v)
            dkm_ref[...] = jnp.zeros_like(dkm_ref)
            dvm_ref[...] = jnp.zeros_like(dvm_ref)
            dsink_ref[...] = jnp.zeros_like(dsink_ref)

        @pl.when(step < nb)
        def _():
            bias8 = jnp.tile(_attn_bias(step), (GROUP, 1))
            lane = lax.broadcasted_iota(jnp.int32, (1, LANES), 1)
            dsink = jnp.zeros((1, LANES), F32)
            for h in range(N_KV_HEADS):
                k3 = jnp.concatenate([kp_ref[h], kc_ref[h], km_ref[h]], axis=0)
                v3 = jnp.concatenate([vp_ref[h], vc_ref[h], vm_ref[h]], axis=0)
                qs = _stack_heads(q_ref, h)
                dos = _stack_heads(do_ref, h)
                p, psink = _attn_probs(qs, k3, bias8, _sink_column(sink_ref, h))
                dp = lax.dot_general(dos, v3, (((1,), (1,)), ((), ())), preferred_element_type=F32)
                delta = jnp.sum(p * dp, axis=1, keepdims=True)
                ds = (p * (dp - delta)).astype(BF16)
                dsk = -psink * delta
                for g in range(GROUP):
                    val = jnp.sum(dsk[g * BLOCK:(g + 1) * BLOCK], axis=0, keepdims=True)
                    dsink = dsink + jnp.where(lane == h * GROUP + g, val, 0.0)
                dqs = jnp.dot(ds, k3, preferred_element_type=F32)
                for g in range(GROUP):
                    n = h * GROUP + g
                    dq_ref[:, n * HEAD_DIM:(n + 1) * HEAD_DIM] = dqs[g * BLOCK:(g + 1) * BLOCK]
                dk3 = lax.dot_general(ds, qs, (((0,), (0,)), ((), ())), preferred_element_type=F32)
                dv3 = lax.dot_general(p.astype(BF16), dos, (((0,), (0,)), ((), ())), preferred_element_type=F32)
                dk_ref[h] = carry_k[h] + dk3[0:BLOCK]
                dv_ref[h] = carry_v[h] + dv3[0:BLOCK]
                carry_k[h] = dk3[BLOCK:2 * BLOCK]
                carry_v[h] = dv3[BLOCK:2 * BLOCK]
                dkm_ref[h] += dk3[2 * BLOCK:3 * BLOCK]
                dvm_ref[h] += dv3[2 * BLOCK:3 * BLOCK]
            dsink_ref[...] += dsink

        @pl.when(step == nb)
        def _():
            dk_ref[...] = carry_k[...]
            dv_ref[...] = carry_v[...]

    prev, cur, meta = (lambda s: (0, cl(s), 0)), (lambda s: (0, cl(s) + 1, 0)), (lambda s: (0, 1, 0))
    lag = lambda s: (0, jnp.maximum(s - 1, 0), 0)
    head_shape = jax.ShapeDtypeStruct((N_KV_HEADS, lp, HEAD_DIM), F32)
    meta_shape = jax.ShapeDtypeStruct((N_KV_HEADS, BLOCK, HEAD_DIM), F32)
    return _pcall(
        body, name="attn_bwd", grid=(nb + 1,),
        in_specs=[pl.BlockSpec((BLOCK, Q_DIM), lambda s: (cl(s), 0)), pl.BlockSpec((BLOCK, Q_DIM), lambda s: (cl(s), 0)),
                  kv(prev), kv(cur), kv(meta), kv(prev), kv(cur), kv(meta), _const_spec((1, N_Q_HEADS))],
        out_specs=[pl.BlockSpec((BLOCK, Q_DIM), lambda s: (cl(s), 0)), kv(lag), kv(lag),
                   _const_spec((N_KV_HEADS, BLOCK, HEAD_DIM)), _const_spec((N_KV_HEADS, BLOCK, HEAD_DIM)), _const_spec((1, LANES))],
        out_shape=[jax.ShapeDtypeStruct((lp, Q_DIM), F32), head_shape, head_shape, meta_shape, meta_shape,
                   jax.ShapeDtypeStruct((1, LANES), F32)],
        scratch_shapes=[pltpu.VMEM((N_KV_HEADS, BLOCK, HEAD_DIM), F32), pltpu.VMEM((N_KV_HEADS, BLOCK, HEAD_DIM), F32)],
        compiler_params=_params(("arbitrary",)),
    )(q, do, k_sh, k_sh, k_sh, v_sh, v_sh, v_sh, sinks)


CONV_CHUNK = 256


SUBLANES = 8
SH_BASE = BLOCK - 4 * SUBLANES
SH_ROWS = BLOCK + 3 * SUBLANES
DSH_ROWS = SH_ROWS


def _shifted_windows(src, sh, base, rows):
    for b in range(1, SUBLANES):
        sh[b] = src[base + b:base + b + rows, :]


def _window(src, sh, base, start, cols):
    a, b = divmod(start - base, SUBLANES)
    if b == 0:
        return src[start:start + BLOCK, cols]
    return sh[b, SUBLANES * a:SUBLANES * a + BLOCK, cols]


def _glu_masked(a_ref, g_ref, base):
    rows = base + lax.broadcasted_iota(jnp.int32, (BLOCK, 1), 0)
    return jnp.where(rows >= PAD_ROWS, a_ref[...].astype(F32) * _sigmoid(g_ref[...].astype(F32)), 0.0)


def _conv_fwd(zc, conv_w, conv_b, ln_g, ln_b, after=None):
    lp = zc.shape[0]
    cd = zc.shape[1] // 2
    nb = lp // BLOCK
    chunk = min(CONV_CHUNK, cd)
    back = lambda col: (lambda i: (jnp.maximum(i - 1, 0), col))
    lo = BLOCK - (CONV_WIDTH - 1)

    def body(ap_ref, gp_ref, ac_ref, gc_ref, w_ref, b_ref, lg_ref, lb_ref, co_ref, c2_ref, ext, sh):
        i = pl.program_id(0)
        ext[0:BLOCK, :] = _glu_masked(ap_ref, gp_ref, (i - 1) * BLOCK)
        ext[BLOCK:2 * BLOCK, :] = _glu_masked(ac_ref, gc_ref, i * BLOCK)
        _shifted_windows(ext, sh, SH_BASE, SH_ROWS)
        for c0 in range(0, cd, chunk):
            cols = slice(c0, c0 + chunk)
            acc = jnp.zeros((BLOCK, chunk), F32)
            for k in range(CONV_WIDTH):
                acc = acc + _window(ext, sh, SH_BASE, lo + k, cols) * w_ref[k:k + 1, cols]
            co_ref[:, cols] = acc + b_ref[:, cols]
        x = co_ref[...]
        mu = jnp.mean(x, axis=-1, keepdims=True)
        xc = x - mu
        r = lax.rsqrt(jnp.mean(xc * xc, axis=-1, keepdims=True) + EPS)
        y = xc * r * lg_ref[...] + lb_ref[...]
        c2_ref[...] = (y * _sigmoid(y)).astype(BF16)

    return _pcall(
        body, after=after, name="conv_fwd", grid=(nb,),
        in_specs=[pl.BlockSpec((BLOCK, cd), back(0)), pl.BlockSpec((BLOCK, cd), back(1)), _row_spec(cd, 0), _row_spec(cd, 1),
                  _const_spec((CONV_ROWS, cd)), _const_spec((1, cd)), _const_spec((1, cd)), _const_spec((1, cd))],
        out_specs=[_row_spec(cd), _row_spec(cd)],
        out_shape=[jax.ShapeDtypeStruct((lp, cd), F32), jax.ShapeDtypeStruct((lp, cd), BF16)],
        scratch_shapes=[pltpu.VMEM((2 * BLOCK, cd), F32), pltpu.VMEM((SUBLANES, SH_ROWS, cd), F32)],
        compiler_params=_params(("arbitrary",)),
    )(zc, zc, zc, zc, conv_w, conv_b, ln_g, ln_b)


def _conv_bwd_norm(dc2, conv_out, ln_g, ln_b):
    lp, cd = conv_out.shape

    def body(d_ref, x_ref, lg_ref, lb_ref, dco_ref, dlg_ref, dlb_ref, dcb_ref):
        i = pl.program_id(0)
        x = x_ref[...]
        g = lg_ref[...]
        mu = jnp.mean(x, axis=-1, keepdims=True)
        xc = x - mu
        r = lax.rsqrt(jnp.mean(xc * xc, axis=-1, keepdims=True) + EPS)
        xhat = xc * r
        y = xhat * g + lb_ref[...]
        sg = _sigmoid(y)
        dy = d_ref[...] * (sg * (1.0 + y * (1.0 - sg)))
        dxhat = dy * g
        dx = r * (dxhat - jnp.mean(dxhat, axis=-1, keepdims=True) - xhat * jnp.mean(dxhat * xhat, axis=-1, keepdims=True))
        dco_ref[...] = dx

        @pl.when(i == 0)
        def _():
            dlg_ref[...] = jnp.zeros_like(dlg_ref)
            dlb_ref[...] = jnp.zeros_like(dlb_ref)
            dcb_ref[...] = jnp.zeros_like(dcb_ref)

        dlg_ref[...] += jnp.sum(dy * xhat, axis=0, keepdims=True)
        dlb_ref[...] += jnp.sum(dy, axis=0, keepdims=True)
        dcb_ref[...] += jnp.sum(dx, axis=0, keepdims=True)

    vec = jax.ShapeDtypeStruct((1, cd), F32)
    return _pcall(
        body, name="conv_bwd_norm", grid=(lp // BLOCK,),
        in_specs=[_row_spec(cd), _row_spec(cd), _const_spec((1, cd)), _const_spec((1, cd))],
        out_specs=[_row_spec(cd), _const_spec((1, cd)), _const_spec((1, cd)), _const_spec((1, cd))],
        out_shape=[jax.ShapeDtypeStruct((lp, cd), F32), vec, vec, vec],
        compiler_params=_params(("arbitrary",)),
    )(dc2, conv_out, ln_g, ln_b)


def _conv_bwd_taps(dco, zc, conv_w):
    lp, cd = dco.shape
    nb = lp // BLOCK
    chunk = min(CONV_CHUNK, cd)
    back = lambda col: (lambda i: (jnp.maximum(i - 1, 0), col))
    fwd = lambda i: (jnp.minimum(i + 1, nb - 1), 0)
    lo = BLOCK - (CONV_WIDTH - 1)

    def body(dc_ref, dn_ref, ap_ref, gp_ref, ac_ref, gc_ref, w_ref, dz_ref, sum_ref, dw_ref, ext, dext, dcb, sh, dsh):
        i = pl.program_id(0)
        ext[0:BLOCK, :] = _glu_masked(ap_ref, gp_ref, (i - 1) * BLOCK)
        ext[BLOCK:2 * BLOCK, :] = _glu_masked(ac_ref, gc_ref, i * BLOCK)
        dext[0:BLOCK, :] = dc_ref[...]
        dext[BLOCK:2 * BLOCK, :] = dn_ref[...] * (i < nb - 1).astype(F32)
        _shifted_windows(ext, sh, SH_BASE, SH_ROWS)
        _shifted_windows(dext, dsh, 0, DSH_ROWS)

        @pl.when(i == 0)
        def _():
            dw_ref[...] = jnp.zeros_like(dw_ref)
            sum_ref[...] = jnp.zeros_like(sum_ref)

        for c0 in range(0, cd, chunk):
            cols = slice(c0, c0 + chunk)
            dcur = dext[0:BLOCK, cols]
            acc = jnp.zeros((BLOCK, chunk), F32)
            for k in range(CONV_WIDTH):
                s = CONV_WIDTH - 1 - k
                acc = acc + _window(dext, dsh, 0, s, cols) * w_ref[k:k + 1, cols]
                dw_ref[k:k + 1, cols] += jnp.sum(dcur * _window(ext, sh, SH_BASE, lo + k, cols), axis=0, keepdims=True)
            dcb[:, cols] = acc
        rows = i * BLOCK + lax.broadcasted_iota(jnp.int32, (BLOCK, 1), 0)
        dc = jnp.where(rows >= PAD_ROWS, dcb[...], 0.0)
        a = ac_ref[...].astype(F32)
        sg = _sigmoid(gc_ref[...].astype(F32))
        da = dc * sg
        dg = dc * a * sg * (1.0 - sg)
        dz_ref[:, 0:cd] = da.astype(BF16)
        dz_ref[:, cd:2 * cd] = dg.astype(BF16)
        sum_ref[:, 0:cd] += jnp.sum(da, axis=0, keepdims=True)
        sum_ref[:, cd:2 * cd] += jnp.sum(dg, axis=0, keepdims=True)

    return _pcall(
        body, name="conv_bwd_taps", grid=(nb,),
        in_specs=[_row_spec(cd), pl.BlockSpec((BLOCK, cd), fwd),
                  pl.BlockSpec((BLOCK, cd), back(0)), pl.BlockSpec((BLOCK, cd), back(1)), _row_spec(cd, 0), _row_spec(cd, 1),
                  _const_spec((CONV_ROWS, cd))],
        out_specs=[_row_spec(2 * cd), _const_spec((1, 2 * cd)), _const_spec((CONV_ROWS, cd))],
        out_shape=[jax.ShapeDtypeStruct((lp, 2 * cd), BF16), jax.ShapeDtypeStruct((1, 2 * cd), F32),
                   jax.ShapeDtypeStruct((CONV_ROWS, cd), F32)],
        scratch_shapes=[pltpu.VMEM((2 * BLOCK, cd), F32), pltpu.VMEM((2 * BLOCK, cd), F32), pltpu.VMEM((BLOCK, cd), F32),
                        pltpu.VMEM((SUBLANES, SH_ROWS, cd), F32), pltpu.VMEM((SUBLANES, DSH_ROWS, cd), F32)],
        compiler_params=_params(("arbitrary",)),
    )(dco, dco, zc, zc, zc, zc, conv_w)


def _conv_out_gate(c2, wco_t, bias, br_a, zg, *, tm, tn):
    m, k = c2.shape
    d = wco_t.shape[0]
    tm, tn = _pick(m, tm, 16), _pick(d, tn, LANES)
    nj = d // tn

    def body(a_ref, w_ref, bias_ref, bra_ref, ga_ref, gb_ref, brb_ref, m_ref):
        acc = lax.dot_general(a_ref[...], w_ref[...], (((1,), (1,)), ((), ())), preferred_element_type=F32)
        brb = (acc + bias_ref[...]).astype(BF16)
        brb_ref[...] = brb
        ga, gb = ga_ref[...].astype(F32), gb_ref[...].astype(F32)
        m_ref[...] = (_sigmoid(ga) * bra_ref[...].astype(F32) + _sigmoid(gb) * brb.astype(F32)).astype(BF16)

    tile = pl.BlockSpec((tm, tn), lambda i, j: (i, j))
    shape = jax.ShapeDtypeStruct((m, d), BF16)
    return _pcall(
        body, name="conv_out_proj_gate", grid=(m // tm, nj),
        in_specs=[pl.BlockSpec((tm, k), lambda i, j: (i, 0)), pl.BlockSpec((tn, k), lambda i, j: (j, 0)),
                  pl.BlockSpec((1, tn), lambda i, j: (0, j)), tile, tile, pl.BlockSpec((tm, tn), lambda i, j: (i, j + nj))],
        out_specs=[tile, tile], out_shape=[shape, shape], compiler_params=_params(("parallel", "arbitrary")),
    )(c2, wco_t, bias, br_a, zg, zg)


def _gate_bwd(dm, a, b, zg):
    lp, d = a.shape

    def body(dm_ref, a_ref, b_ref, ga_ref, gb_ref, da_ref, db_ref, dz_ref, sum_ref, dbias_ref):
        i = pl.program_id(0)
        dm_ = dm_ref[...].astype(F32)
        sa = _sigmoid(ga_ref[...].astype(F32))
        sb = _sigmoid(gb_ref[...].astype(F32))
        db = dm_ * sb
        dga = dm_ * a_ref[...].astype(F32) * sa * (1.0 - sa)
        dgb = dm_ * b_ref[...].astype(F32) * sb * (1.0 - sb)
        da_ref[...] = (dm_ * sa).astype(BF16)
        db_ref[...] = db.astype(BF16)
        dz_ref[:, 0:d] = dga.astype(BF16)
        dz_ref[:, d:2 * d] = dgb.astype(BF16)

        @pl.when(i == 0)
        def _():
            sum_ref[...] = jnp.zeros_like(sum_ref)
            dbias_ref[...] = jnp.zeros_like(dbias_ref)

        sum_ref[:, 0:d] += jnp.sum(dga, axis=0, keepdims=True)
        sum_ref[:, d:2 * d] += jnp.sum(dgb, axis=0, keepdims=True)
        dbias_ref[...] += jnp.sum(db, axis=0, keepdims=True)

    return _pcall(
        body, name="gate_bwd", grid=(lp // BLOCK,),
        in_specs=[_row_spec(d), _row_spec(d), _row_spec(d), _row_spec(d, 0), _row_spec(d, 1)],
        out_specs=[_row_spec(d), _row_spec(d), _row_spec(2 * d), _const_spec((1, 2 * d)), _const_spec((1, d))],
        out_shape=[jax.ShapeDtypeStruct((lp, d), BF16), jax.ShapeDtypeStruct((lp, d), BF16), jax.ShapeDtypeStruct((lp, 2 * d), BF16),
                   jax.ShapeDtypeStruct((1, 2 * d), F32), jax.ShapeDtypeStruct((1, d), F32)],
        compiler_params=_params(("arbitrary",)),
    )(dm, a, b, zg, zg)


def _gate_up_swiglu(u2, wgu_t, *, row_block, tm, tn, name, filled=None, after=None):
    m, k = u2.shape
    f = wgu_t.shape[0] // 2
    tn = _pick(f, tn, LANES)
    nj = f // tn
    dims = (((1,), (1,)), ((), ()))
    n_filled = 0 if filled is None else 3

    def body(*refs):
        a_ref, wg_ref, wu_ref = refs[:3]
        g_ref, u_ref, act_ref = refs[3 + n_filled:]
        a = a_ref[...]
        g = lax.dot_general(a, wg_ref[...], dims, preferred_element_type=F32).astype(BF16)
        up = lax.dot_general(a, wu_ref[...], dims, preferred_element_type=F32).astype(BF16)
        g_ref[...] = g
        u_ref[...] = up
        gf = g.astype(F32)
        act_ref[...] = (gf * _sigmoid(gf) * up.astype(F32)).astype(BF16)

    out = pl.BlockSpec((tm, tn), lambda j: (row_block, j))
    shape = jax.ShapeDtypeStruct((m, f), BF16)
    return _pcall(
        body, after=after, name=name, grid=(nj,),
        in_specs=[pl.BlockSpec((tm, k), lambda j: (row_block, 0)), pl.BlockSpec((tn, k), lambda j: (j, 0)),
                  pl.BlockSpec((tn, k), lambda j: (j + nj, 0))] + [pl.BlockSpec(memory_space=pl.ANY)] * n_filled,
        out_specs=[out, out, out], out_shape=[shape, shape, shape],
        input_output_aliases={3 + i: i for i in range(n_filled)}, compiler_params=_params(("arbitrary",)),
    )(u2, wgu_t, wgu_t, *(filled or ()))


def _swiglu_bwd(dact, g, up):
    lp, f = dact.shape

    def body(d_ref, g_ref, u_ref, o_ref):
        g = g_ref[...].astype(F32)
        d = d_ref[...].astype(F32)
        sg = _sigmoid(g)
        o_ref[:, 0:f] = (d * u_ref[...].astype(F32) * (sg * (1.0 + g * (1.0 - sg)))).astype(BF16)
        o_ref[:, f:2 * f] = (d * g * sg).astype(BF16)

    return _pcall(
        body, name="swiglu_bwd", grid=(lp // BLOCK,), in_specs=[_row_spec(f), _row_spec(f), _row_spec(f)],
        out_specs=_row_spec(2 * f), out_shape=jax.ShapeDtypeStruct((lp, 2 * f), BF16), compiler_params=_params(("parallel",)),
    )(dact, g, up)


ANY = pl.BlockSpec(memory_space=pl.ANY)


def _all_gather_rows(x, name, after=None):
    r, c = x.shape

    def body(x_ref, out_ref, send_sems, recv_sems, local_sem):
        mx, my, mc = lax.axis_index("x"), lax.axis_index("y"), lax.axis_index("c")
        me, sibling = (mx, my, mc), (mx, my, 1 - mc)
        chips = [(1 - mx, my), (mx, 1 - my), (1 - mx, 1 - my)]

        def rows(px, py, pc):
            return out_ref.at[pl.ds((4 * px + 2 * py + pc) * r, r), :]

        def copy(k, block, to, src=None):
            return pltpu.make_async_remote_copy(
                src_ref=rows(*block) if src is None else src, dst_ref=rows(*block),
                send_sem=send_sems.at[k], recv_sem=recv_sems.at[k], device_id=to, device_id_type=MESH)

        mine = pltpu.make_async_copy(x_ref, rows(*me), local_sem)
        mine.start()
        first = [copy(0, me, sibling, src=x_ref)]
        first += [copy(1 + j, me, (*chip, mc), src=x_ref) for j, chip in enumerate(chips)]
        for cp in first:
            cp.start()
        passed = [copy(4 + j, (*chip, mc), sibling) for j, chip in enumerate(chips)]
        for j, chip in enumerate(chips):
            copy(1 + j, (*chip, mc), me).wait_recv()
            passed[j].start()
        copy(0, sibling, me).wait_recv()
        for j, chip in enumerate(chips):
            copy(4 + j, (*chip, 1 - mc), me).wait_recv()
        for cp in first + passed:
            cp.wait_send()
        mine.wait()

    return _pcall(
        body, after=after, name=name, in_specs=[ANY], out_specs=ANY, out_shape=jax.ShapeDtypeStruct((N_DEV * r, c), x.dtype),
        scratch_shapes=[pltpu.SemaphoreType.DMA((7,)), pltpu.SemaphoreType.DMA((7,)), pltpu.SemaphoreType.DMA(())],
    )(x)


HBM = pl.BlockSpec(memory_space=pltpu.HBM)
SEM = pl.BlockSpec(memory_space=pltpu.SEMAPHORE)
IN_FLIGHT = pltpu.CompilerParams(has_side_effects=pltpu.SideEffectType.DATAFLOW_SIDE_EFFECTING)


def _place_rows(shard, after, name):
    r, c = shard.shape
    tr = _pick(r, max(16, ELEMENTWISE_BLOCK_BYTES // (4 * c)), 16)
    steps = r // tr
    dev = (4 * lax.axis_index("x") + 2 * lax.axis_index("y") + lax.axis_index("c")).astype(jnp.int32).reshape(1)

    def body(dev_ref, x_ref, after_ref, o_ref):
        o_ref[...] = x_ref[...].astype(BF16)

    return _pcall(
        body, name=name,
        grid_spec=pltpu.PrefetchScalarGridSpec(
            num_scalar_prefetch=1, grid=(steps,),
            in_specs=[pl.BlockSpec((tr, c), lambda i, dev_ref: (i, 0)), pl.BlockSpec(memory_space=pl.ANY)],
            out_specs=pl.BlockSpec((tr, c), lambda i, dev_ref: (dev_ref[0] * steps + i, 0))),
        out_shape=jax.ShapeDtypeStruct((N_DEV * r, c), BF16), compiler_params=_params(("parallel",)),
    )(dev, shard, after)


def _rows_start(full, plan, name, after=None):
    r = full.shape[0] // N_DEV
    n = len(plan(0, 0, 0))

    ordered = after is not None

    def body(*refs):
        full_ref, (send_sems, recv_sems) = refs[0], refs[1 + ordered:3 + ordered]
        mx, my, mc = lax.axis_index("x"), lax.axis_index("y"), lax.axis_index("c")
        for k, ((bx, by, bc), target) in enumerate(plan(mx, my, mc)):
            rows = full_ref.at[pl.ds((4 * bx + 2 * by + bc) * r, r), :]
            pltpu.make_async_remote_copy(
                src_ref=rows, dst_ref=rows, send_sem=send_sems.at[k], recv_sem=recv_sems.at[k],
                device_id=target, device_id_type=MESH).start()

    return pl.pallas_call(
        body, name=name, in_specs=[HBM] + [pl.BlockSpec(memory_space=pl.ANY)] * ordered, out_specs=(SEM, SEM, HBM),
        out_shape=(pltpu.SemaphoreType.DMA((n,)), pltpu.SemaphoreType.DMA((n,)), pltpu.HBM(full.shape, full.dtype)),
        input_output_aliases={0: 2}, compiler_params=IN_FLIGHT,
    )(pltpu.with_memory_space_constraint(full, pltpu.HBM), *([after] if ordered else []))


def _rows_wait(started, after, name):
    send_sem, recv_sem, full = started
    r = full.shape[0] // N_DEV
    n = send_sem.shape[0]

    def body(full_ref, send_ref, recv_ref, after_ref, out_ref):
        mx, my, mc = lax.axis_index("x"), lax.axis_index("y"), lax.axis_index("c")
        block = full_ref.at[pl.ds(0, r), :]
        for k in range(n):
            cp = pltpu.make_async_remote_copy(
                src_ref=block, dst_ref=block, send_sem=send_ref.at[k], recv_sem=recv_ref.at[k],
                device_id=(mx, my, mc), device_id_type=MESH)
            cp.wait_send()
            cp.wait_recv()

    return pl.pallas_call(
        body, name=name, in_specs=[HBM, SEM, SEM, pl.BlockSpec(memory_space=pl.ANY)], out_specs=HBM,
        out_shape=pltpu.HBM(full.shape, full.dtype), input_output_aliases={0: 0}, compiler_params=IN_FLIGHT,
    )(full, send_sem, recv_sem, after)


def _plan_direct(mx, my, mc):
    me = (mx, my, mc)
    return [(me, (mx, my, 1 - mc)), (me, (1 - mx, my, mc)), (me, (mx, 1 - my, mc)), (me, (1 - mx, 1 - my, mc))]


def _plan_pass_on(mx, my, mc):
    sibling = (mx, my, 1 - mc)
    return [((1 - mx, my, mc), sibling), ((mx, 1 - my, mc), sibling), ((1 - mx, 1 - my, mc), sibling)]


def _plan_neighbours(mx, my, mc):
    me = (mx, my, mc)
    return [(me, (mx, my, 1 - mc)), (me, (1 - mx, my, mc)), (me, (mx, 1 - my, mc))]


def _plan_relay(mx, my, mc):
    sibling = (mx, my, 1 - mc)
    source = ((mx + 1 - mc) % 2, (my + mc) % 2, mc)
    target = ((mx + mc) % 2, (my + 1 - mc) % 2, mc)
    return [((1 - mx, my, mc), sibling), ((mx, 1 - my, mc), sibling), (source, target)]


def _plan_pass_on_diagonal(mx, my, mc):
    return [((1 - mx, 1 - my, mc), (mx, my, 1 - mc))]


def _pair_exchange_start(g, name):
    r = g.shape[0] // N_DEV
    c = g.shape[1]
    land = (len(CHIPS), r, c)

    def body(g_ref, land_ref, send_sems, recv_sems, g_out, land_out):
        mx, my, mc = lax.axis_index("x"), lax.axis_index("y"), lax.axis_index("c")
        for j, (px, py) in enumerate(CHIPS):
            pltpu.make_async_remote_copy(
                src_ref=g_ref.at[pl.ds((4 * px + 2 * py + 1 - mc) * r, r), :], dst_ref=land_ref.at[j],
                send_sem=send_sems.at[j], recv_sem=recv_sems.at[j], device_id=(mx, my, 1 - mc), device_id_type=MESH).start()

    return pl.pallas_call(
        body, name=name, in_specs=[HBM, HBM], out_specs=(SEM, SEM, HBM, HBM),
        out_shape=(pltpu.SemaphoreType.DMA((4,)), pltpu.SemaphoreType.DMA((4,)), pltpu.HBM(g.shape, g.dtype), pltpu.HBM(land, g.dtype)),
        input_output_aliases={0: 2, 1: 3}, compiler_params=IN_FLIGHT,
    )(pltpu.with_memory_space_constraint(g, pltpu.HBM), pltpu.with_memory_space_constraint(lax.empty(land, g.dtype), pltpu.HBM))


def _pair_exchange_wait(send_sem, recv_sem, g, land, after, name):
    def body(g_ref, land_ref, send_ref, recv_ref, after_ref, g_out, land_out):
        mx, my, mc = lax.axis_index("x"), lax.axis_index("y"), lax.axis_index("c")
        for j in range(len(CHIPS)):
            cp = pltpu.make_async_remote_copy(
                src_ref=land_ref.at[0], dst_ref=land_ref.at[0], send_sem=send_ref.at[j], recv_sem=recv_ref.at[j],
                device_id=(mx, my, mc), device_id_type=MESH)
            cp.wait_send()
            cp.wait_recv()

    return pl.pallas_call(
        body, name=name, in_specs=[HBM, HBM, SEM, SEM, pl.BlockSpec(memory_space=pl.ANY)], out_specs=(HBM, HBM),
        out_shape=(pltpu.HBM(g.shape, g.dtype), pltpu.HBM(land.shape, land.dtype)), input_output_aliases={0: 0, 1: 1},
        compiler_params=IN_FLIGHT,
    )(g, land, send_sem, recv_sem, after)


def _chip_exchange_start(ps, after, name):
    def body(ps_ref, rx_ref, after_ref, send_sems, recv_sems, ps_out, rx_out):
        mx, my, mc = lax.axis_index("x"), lax.axis_index("y"), lax.axis_index("c")
        chips = [(1 - mx, my), (mx, 1 - my), (1 - mx, 1 - my)]
        for k, (px, py) in enumerate(chips):
            pltpu.make_async_remote_copy(
                src_ref=ps_ref.at[2 * px + py], dst_ref=rx_ref.at[2 * mx + my], send_sem=send_sems.at[k], recv_sem=recv_sems.at[k],
                device_id=(px, py, mc), device_id_type=MESH).start()

    return pl.pallas_call(
        body, name=name, in_specs=[HBM, HBM, pl.BlockSpec(memory_space=pl.ANY)], out_specs=(SEM, SEM, HBM, HBM),
        out_shape=(pltpu.SemaphoreType.DMA((3,)), pltpu.SemaphoreType.DMA((3,)), pltpu.HBM(ps.shape, ps.dtype), pltpu.HBM(ps.shape, ps.dtype)),
        input_output_aliases={0: 2, 1: 3}, compiler_params=IN_FLIGHT,
    )(pltpu.with_memory_space_constraint(ps, pltpu.HBM), pltpu.with_memory_space_constraint(lax.empty(ps.shape, ps.dtype), pltpu.HBM), after)


def _chip_exchange_wait(send_sem, recv_sem, ps, rx, after, name):
    def body(ps_ref, rx_ref, send_ref, recv_ref, after_ref, ps_out, rx_out):
        mx, my, mc = lax.axis_index("x"), lax.axis_index("y"), lax.axis_index("c")
        for k in range(3):
            cp = pltpu.make_async_remote_copy(
                src_ref=ps_ref.at[0], dst_ref=rx_ref.at[0], send_sem=send_ref.at[k], recv_sem=recv_ref.at[k],
                device_id=(mx, my, mc), device_id_type=MESH)
            cp.wait_send()
            cp.wait_recv()

    return pl.pallas_call(
        body, name=name, in_specs=[HBM, HBM, SEM, SEM, pl.BlockSpec(memory_space=pl.ANY)], out_specs=(HBM, HBM),
        out_shape=(pltpu.HBM(ps.shape, ps.dtype), pltpu.HBM(rx.shape, rx.dtype)), input_output_aliases={0: 0, 1: 1},
        compiler_params=IN_FLIGHT,
    )(ps, rx, send_sem, recv_sem, after)


def _sum_chips(ps, rx, name):
    n, r, c = rx.shape
    tr = _pick(r, max(16, 4 * ELEMENTWISE_BLOCK_BYTES // (4 * n * c)), 16)
    chip =(2 * lax.axis_index("x") + lax.axis_index("y")).astype(jnp.int32).reshape(1)

    def body(chip_ref, own_ref, x_ref, o_ref):
        me = chip_ref[0]
        own = own_ref[0].astype(F32)
        acc = jnp.where(me == 0, own, x_ref[0].astype(F32))
        for j in range(1, n):
            acc = acc + jnp.where(me == j, own, x_ref[j].astype(F32))
        o_ref[...] = acc

    return _pcall(
        body, name=name,
        grid_spec=pltpu.PrefetchScalarGridSpec(
            num_scalar_prefetch=1, grid=(r // tr,),
            in_specs=[pl.BlockSpec((1, tr, c), lambda i, chip_ref: (chip_ref[0], i, 0)), pl.BlockSpec((n, tr, c), lambda i, chip_ref: (0, i, 0))],
            out_specs=pl.BlockSpec((tr, c), lambda i, chip_ref: (i, 0))),
        out_shape=jax.ShapeDtypeStruct((r, c), F32), compiler_params=_params(("parallel",)),
    )(chip, ps, rx)


def _pair_exchange(g, name):
    r = g.shape[0] // N_DEV
    c = g.shape[1]

    def body(g_ref, theirs_ref, send_sems, recv_sems):
        mx, my, mc = lax.axis_index("x"), lax.axis_index("y"), lax.axis_index("c")
        sibling = (mx, my, 1 - mc)
        copies = []
        for j, (px, py) in enumerate(CHIPS):
            give = g_ref.at[pl.ds((4 * px + 2 * py + 1 - mc) * r, r), :]
            rc = pltpu.make_async_remote_copy(
                src_ref=give, dst_ref=theirs_ref.at[j], send_sem=send_sems.at[j], recv_sem=recv_sems.at[j],
                device_id=sibling, device_id_type=MESH)
            rc.start()
            copies.append(rc)
        for cp in copies:
            cp.wait()

    return _pcall(
        body, name=name, in_specs=[ANY], out_specs=ANY, out_shape=jax.ShapeDtypeStruct((len(CHIPS), r, c), g.dtype),
        scratch_shapes=[pltpu.SemaphoreType.DMA((4,)), pltpu.SemaphoreType.DMA((4,))],
    )(g)


def _pair_sum(g, theirs, name):
    nch, r, c = theirs.shape
    tr = _pick(r, max(16, 3 * ELEMENTWISE_BLOCK_BYTES // (2 * c)), 16)
    core = lax.axis_index("c").astype(jnp.int32).reshape(1)

    def body(core_ref, a_ref, b_ref, o_ref):
        o_ref[...] = (a_ref[...].astype(F32) + b_ref[...].astype(F32)).astype(o_ref.dtype)

    spec = pl.BlockSpec((1, tr, c), lambda j, i, core_ref: (j, i, 0))
    own = pl.BlockSpec((1, tr, c), lambda j, i, core_ref: (2 * j + core_ref[0], i, 0))
    return _pcall(
        body, name=name,
        grid_spec=pltpu.PrefetchScalarGridSpec(num_scalar_prefetch=1, grid=(nch, r // tr), in_specs=[own, spec], out_specs=spec),
        out_shape=jax.ShapeDtypeStruct(theirs.shape, theirs.dtype), compiler_params=_params(("parallel", "parallel")),
    )(core, g.reshape(N_DEV, r, c), theirs)


def _sum_blocks(rx, name):
    n, r, c = rx.shape
    tr = _pick(r, max(8, ELEMENTWISE_BLOCK_BYTES // (4 * n * c)), 8)

    def body(x_ref, o_ref):
        acc = x_ref[0].astype(F32)
        for j in range(1, n):
            acc = acc + x_ref[j].astype(F32)
        o_ref[...] = acc

    return _pcall(
        body, name=name, grid=(r // tr,), in_specs=[pl.BlockSpec((n, tr, c), lambda i: (0, i, 0))],
        out_specs=pl.BlockSpec((tr, c), lambda i: (i, 0)), out_shape=jax.ShapeDtypeStruct((r, c), F32),
        compiler_params=_params(("parallel",)),
    )(rx)


def _adamw(w, g, m, v, name):
    r, c = w.shape
    tr = _pick(r, max(8, ELEMENTWISE_BLOCK_BYTES // (4 * c)), 8)

    def body(w_ref, g_ref, m_ref, v_ref, d_ref, nm_ref, nv_ref):
        d_ref[...], nm_ref[...], nv_ref[...] = _adam_update(w_ref[...], g_ref[...], m_ref[...], v_ref[...])

    spec = pl.BlockSpec((tr, c), lambda i: (i, 0))
    shp = jax.ShapeDtypeStruct((r, c), F32)
    return _pcall(
        body, name=name, grid=(r // tr,), in_specs=[spec] * 4, out_specs=[spec] * 3, out_shape=[shp] * 3,
        compiler_params=_params(("parallel",)),
    )(w, g, m, v)


def _adam_update(w, g, m, v):
    nm = ADAM_B1 * m + (1.0 - ADAM_B1) * g
    nv = ADAM_B2 * v + (1.0 - ADAM_B2) * (g * g)
    delta = -ADAM_LR * ((nm / (1.0 - ADAM_B1 ** ADAM_STEP)) / (jnp.sqrt(nv / (1.0 - ADAM_B2 ** ADAM_STEP)) + ADAM_EPS) + ADAM_WD * w)
    return delta, nm, nv


def _sum_chips_adamw(ps, rx, w, m, v, name):
    n, r, c = rx.shape
    tr = _pick(r, max(16, 4 * ELEMENTWISE_BLOCK_BYTES // (4 * n * c)), 16)
    chip = (2 * lax.axis_index("x") + lax.axis_index("y")).astype(jnp.int32).reshape(1)

    def body(chip_ref, own_ref, x_ref, w_ref, m_ref, v_ref, g_ref, d_ref, nm_ref, nv_ref):
        me = chip_ref[0]
        own = own_ref[0].astype(F32)
        g = jnp.where(me == 0, own, x_ref[0].astype(F32))
        for j in range(1, n):
            g = g + jnp.where(me == j, own, x_ref[j].astype(F32))
        g_ref[...] = g
        d_ref[...], nm_ref[...], nv_ref[...] = _adam_update(w_ref[...], g, m_ref[...], v_ref[...])

    rows = pl.BlockSpec((tr, c), lambda i, chip_ref: (i, 0))
    shp = jax.ShapeDtypeStruct((r, c), F32)
    return _pcall(
        body, name=name,
        grid_spec=pltpu.PrefetchScalarGridSpec(
            num_scalar_prefetch=1, grid=(r // tr,),
            in_specs=[pl.BlockSpec((1, tr, c), lambda i, chip_ref: (chip_ref[0], i, 0)),
                      pl.BlockSpec((n, tr, c), lambda i, chip_ref: (0, i, 0)), rows, rows, rows],
            out_specs=[rows] * 4),
        out_shape=[shp] * 4, compiler_params=_params(("parallel",)),
    )(chip, ps, rx, w, m, v)


def _pack(parts):
    flat, layout, row = [], [], 0
    for p in parts:
        n = p.size
        rows = -(-n // LANES)
        flat.append(jnp.pad(p.reshape(-1).astype(F32), (0, rows * LANES - n)))
        layout.append((row, n, p.shape))
        row += rows
    total = -(-row // 8) * 8
    if total > row:
        flat.append(jnp.zeros(((total - row) * LANES,), F32))
    return jnp.concatenate(flat).reshape(total, LANES), layout


def _unpack(slab, layout):
    flat = slab.reshape(-1)
    return [flat[row * LANES:row * LANES + n].reshape(shape) for row, n, shape in layout]


def kernel(x, meta_tokens, mix_norm_g, w_in, b_in, attn_sinks, conv_w, conv_b, conv_ln_g, conv_ln_b, w_attn_o, w_conv_o, b_conv_o, w_out, ffn_norm_g, w_gate_up, w_down, final_norm_g, loss_target, m_meta_tokens, m_mix_norm_g, m_w_in, m_b_in, m_attn_sinks, m_conv_w, m_conv_b, m_conv_ln_g, m_conv_ln_b, m_w_attn_o, m_w_conv_o, m_b_conv_o, m_w_out, m_ffn_norm_g, m_w_gate_up, m_w_down, m_final_norm_g, v_meta_tokens, v_mix_norm_g, v_w_in, v_b_in, v_attn_sinks, v_conv_w, v_conv_b, v_conv_ln_g, v_conv_ln_b, v_w_attn_o, v_w_conv_o, v_b_conv_o, v_w_out, v_ffn_norm_g, v_w_gate_up, v_w_down, v_final_norm_g):
    xs = x[0]
    tgt = loss_target[0]
    s, d = xs.shape
    lp = s + BLOCK
    cd = conv_b.shape[1]
    ffn = w_down.shape[1] * N_DEV
    dev = 4 * lax.axis_index("x") + 2 * lax.axis_index("y") + lax.axis_index("c")
    cw_cols = conv_w.shape[3]
    meta_cols = meta_tokens.shape[1]

    small, small_layout = _pack([meta_tokens, jnp.pad(conv_w[0, :, 0, :], ((0, CONV_ROWS - CONV_WIDTH), (0, 0)))])
    small_flat = _all_gather_rows(small, "gather_small")
    small_all = small_flat.reshape(N_DEV, *small.shape)
    meta_parts, cw_parts = zip(*[_unpack(small_all[j], small_layout) for j in range(N_DEV)])
    meta_full = jnp.concatenate(meta_parts, axis=1)
    conv_w_full = jnp.concatenate(cw_parts, axis=1)
    shards = ((w_in[0].T, "w_in"), (w_attn_o[0].T, "w_attn_o"), (w_conv_o[0].T, "w_conv_o"), (w_out[0], "w_out"),
              (w_gate_up[0].T, "w_gate_up"), (w_down[0], "w_down"))
    first = _rows_start(_place_rows(shards[0][0], small_flat, "place_w_in"), _plan_neighbours, "gather_start_w_in")
    placed, tok = [], first[2]
    for shard, name in shards[1:]:
        tok = _place_rows(shard, tok, "place_" + name)
        placed.append(tok)
    relay = _rows_start(_rows_wait(first, tok, "gather_wait_w_in"), _plan_relay, "gather_relay_start_w_in")
    h0, u = _prep(xs, meta_full, mix_norm_g, after=relay[2])
    diagonal = _rows_start(_rows_wait(relay, u, "gather_relay_wait_w_in"), _plan_pass_on_diagonal, "gather_diagonal_start_w_in")
    started, tok = [None], diagonal[2]
    for full, (_, name) in zip(placed, shards[1:]):
        started.append(_rows_start(full, _plan_direct, "gather_start_" + name, after=tok))
        tok = started[-1][2]
    win_t = _rows_wait(diagonal, tok, "gather_diagonal_wait_w_in")

    def arrived(w, after, name):
        return _rows_start(_rows_wait(started[w], after, "gather_wait_" + name), _plan_pass_on, "gather_pass_on_start_" + name)

    def whole(passing, after, name):
        return _rows_wait(passing, after, "gather_pass_on_wait_" + name)

    ctab, stab = _rope_tables(lp)
    mm = functools.partial(_matmul, tm=1056, tn=1024)

    bq, bkv, bc, bg = b_in[:, :Q_DIM], b_in[:, Q_DIM:Q_DIM + 2 * KV_DIM], b_in[:, Q_DIM + 2 * KV_DIM:Q_DIM + 2 * KV_DIM + 2 * cd], b_in[:, Q_DIM + 2 * KV_DIM + 2 * cd:]
    o_kv, o_c, o_g = Q_DIM, Q_DIM + 2 * KV_DIM, Q_DIM + 2 * KV_DIM + 2 * cd
    in_proj = functools.partial(_matmul, u, win_t, mode="nt", out_dtype=BF16, tm=2112, tn=512, tk=d)
    zq = in_proj(name="in_proj_q", bias=bq, b_row_off=0, b_rows=Q_DIM)
    zkv = in_proj(name="in_proj_kv", bias=bkv, b_row_off=o_kv, b_rows=2 * KV_DIM)
    zc = in_proj(name="in_proj_conv", bias=bc, b_row_off=o_c, b_rows=2 * cd)
    zg = in_proj(name="in_proj_gates", bias=bg, b_row_off=o_g, b_rows=2 * d)
    passing = arrived(1, zg, "w_attn_o")
    q_rot, k_sh, v_sh = _rope_fwd(zq, zkv, ctab, stab, after=passing[2])
    o = _attn_fwd(q_rot, k_sh, v_sh, attn_sinks)
    wao_t = whole(passing, o, "w_attn_o")
    br_a = mm(o, wao_t, mode="nt", name="attn_out_proj", out_dtype=BF16, tk=Q_DIM)
    passing = arrived(2, br_a, "w_conv_o")
    passing_out = arrived(3, passing[2], "w_out")
    conv_out, c2 = _conv_fwd(zc, conv_w_full, conv_b, conv_ln_g, conv_ln_b, after=passing_out[2])
    wco_t = whole(passing, c2, "w_conv_o")
    br_b, merged = _conv_out_gate(c2, wco_t, b_conv_o, br_a, zg, tm=1056, tn=1024)
    wout = whole(passing_out, merged, "w_out")
    passing = arrived(4, wout, "w_gate_up")
    h1 = mm(merged, wout, mode="nn", name="mix_out_proj", out_dtype=F32, tn=512, tk=d, residual=h0, after=passing[2])
    u2 = _rmsnorm_fwd(h1, ffn_norm_g, "ffn_rmsnorm")
    wgu_t = whole(passing, u2, "w_gate_up")
    half_rows = _pick(lp, lp // 2, 16)
    ffn_in = _gate_up_swiglu(u2, wgu_t, row_block=0, tm=half_rows, tn=256, name="ffn_gate_up_swiglu_0")
    passing = arrived(5, ffn_in[2], "w_down")
    for rb in range(1, lp // half_rows):
        ffn_in = _gate_up_swiglu(u2, wgu_t, row_block=rb, tm=half_rows, tn=256, name="ffn_gate_up_swiglu_%d" % rb,
                                 filled=ffn_in, after=passing[2])
    gu_g, gu_u, act = ffn_in
    wdown = whole(passing, act, "w_down")
    h2 = mm(act, wdown, mode="nn", name="ffn_down", out_dtype=F32, tn=512, tk=ffn // 2, residual=h1)
    dh2, dh2_b, loss_part, d_final_g = _final(h2, tgt, final_norm_g.reshape(1, d))

    wgrad = functools.partial(_matmul, mode="tn", out_dtype=BF16, tk=lp, tn=2048, b_inner=False)
    in_flight = {}

    def scatter_begin(g, name):
        return _pair_exchange_start(g, "rs_" + name + "_pair_start")

    def scatter_go_on(pair, after, name):
        g, theirs = _pair_exchange_wait(pair[0], pair[1], pair[2], pair[3], after, "rs_" + name + "_pair_wait")
        ps = _pair_sum(g, theirs, "rs_" + name + "_pair_sum")
        in_flight[name] = _chip_exchange_start(ps, theirs, "rs_" + name + "_chip_start")
        return in_flight[name][2]

    g_wdown = wgrad(act, dh2_b, name="ffn_down_dw", tm=256)
    pair = scatter_begin(g_wdown, "w_down")
    dact = _matmul(dh2_b, wdown, mode="nt", name="ffn_down_dx", out_dtype=BF16, tm=2112, tn=256, tk=d, after=pair[2])
    tok = scatter_go_on(pair, dact, "w_down")
    dgu = _swiglu_bwd(dact, gu_g, gu_u)
    g_wgu_t = wgrad(dgu, u2, name="ffn_gate_up_dw", tm=512, after=tok)
    pair = scatter_begin(g_wgu_t, "w_gate_up")
    du2 = mm(dgu, wgu_t, mode="nn", name="ffn_gate_up_dx", out_dtype=F32, tn=512, tk=ffn // 2, after=pair[2])
    tok = scatter_go_on(pair, du2, "w_gate_up")
    dh1, dh1_b, d_ffn_g = _rmsnorm_bwd(du2, h1, ffn_norm_g, dh2, "ffn_rmsnorm_bwd")
    g_wout = wgrad(merged, dh1_b, name="mix_out_dw", tm=512, after=tok)
    pair = scatter_begin(g_wout, "w_out")
    dmerged = mm(dh1_b, wout, mode="nt", name="mix_out_dx", out_dtype=BF16, tk=d, after=pair[2])
    tok = scatter_go_on(pair, dmerged, "w_out")
    d_a, d_b, dz_g, sum_g, d_bco = _gate_bwd(dmerged, br_a, br_b, zg)
    g_wao_t = wgrad(d_a, o, name="attn_out_dw", tm=512, after=tok)
    pair = scatter_begin(g_wao_t, "w_attn_o")
    do = mm(d_a, wao_t, mode="nn", name="attn_out_dx", out_dtype=BF16, tk=d, after=pair[2])
    tok = scatter_go_on(pair, do, "w_attn_o")
    g_wco_t = wgrad(d_b, c2, name="conv_out_dw", tm=512, after=tok)
    pair = scatter_begin(g_wco_t, "w_conv_o")
    dc2 = mm(d_b, wco_t, mode="nn", name="conv_out_dx", out_dtype=F32, tk=d, after=pair[2])
    tok = scatter_go_on(pair, dc2, "w_conv_o")
    dq, dk, dv, dkm, dvm, d_sinks = _attn_bwd(q_rot, k_sh, v_sh, attn_sinks, do)
    dz_qkv, sum_qkv = _rope_bwd(dq, dk, dv, dkm, dvm, ctab, stab)
    dco, d_ln_g, d_ln_b, d_conv_b = _conv_bwd_norm(dc2, conv_out, conv_ln_g, conv_ln_b)
    dz_c, sum_c, d_conv_w = _conv_bwd_taps(dco, zc, conv_w_full)
    dz = jnp.concatenate([dz_qkv, dz_c, dz_g], axis=1)
    d_b_in = jnp.concatenate([sum_qkv, sum_c, sum_g], axis=1)
    in_dim = dz.shape[1]
    g_win_t = wgrad(dz, u, name="in_proj_dw", tm=512, after=tok)
    theirs = _pair_exchange(g_win_t, "rs_w_in_pair_exchange")
    in_flight["w_in"] = _chip_exchange_start(_pair_sum(g_win_t, theirs, "rs_w_in_pair_sum"), theirs, "rs_w_in_chip_start")
    du = mm(dz, win_t, mode="nn", name="in_proj_dx", out_dtype=F32, tk=in_dim // 4, after=in_flight["w_in"][2])
    grad_x, d_meta, d_mix_g = _rmsnorm_bwd_first(du, h0, mix_norm_g, dh1)

    weights = dict(meta_tokens=meta_tokens, mix_norm_g=mix_norm_g, w_in=w_in, b_in=b_in, attn_sinks=attn_sinks, conv_w=conv_w,
                   conv_b=conv_b, conv_ln_g=conv_ln_g, conv_ln_b=conv_ln_b, w_attn_o=w_attn_o, w_conv_o=w_conv_o, b_conv_o=b_conv_o,
                   w_out=w_out, ffn_norm_g=ffn_norm_g, w_gate_up=w_gate_up, w_down=w_down, final_norm_g=final_norm_g)
    m_in = dict(meta_tokens=m_meta_tokens, mix_norm_g=m_mix_norm_g, w_in=m_w_in, b_in=m_b_in, attn_sinks=m_attn_sinks, conv_w=m_conv_w,
                conv_b=m_conv_b, conv_ln_g=m_conv_ln_g, conv_ln_b=m_conv_ln_b, w_attn_o=m_w_attn_o, w_conv_o=m_w_conv_o,
                b_conv_o=m_b_conv_o, w_out=m_w_out, ffn_norm_g=m_ffn_norm_g, w_gate_up=m_w_gate_up, w_down=m_w_down,
                final_norm_g=m_final_norm_g)
    v_in = dict(meta_tokens=v_meta_tokens, mix_norm_g=v_mix_norm_g, w_in=v_w_in, b_in=v_b_in, attn_sinks=v_attn_sinks, conv_w=v_conv_w,
                conv_b=v_conv_b, conv_ln_g=v_conv_ln_g, conv_ln_b=v_conv_ln_b, w_attn_o=v_w_attn_o, w_conv_o=v_w_conv_o,
                b_conv_o=v_b_conv_o, w_out=v_w_out, ffn_norm_g=v_ffn_norm_g, w_gate_up=v_w_gate_up, w_down=v_w_down,
                final_norm_g=v_final_norm_g)
    names = list(weights)
    grads, delta, new_m, new_v = {}, {}, {}, {}
    tok = grad_x
    for n in ("w_down", "w_gate_up", "w_out", "w_attn_o", "w_conv_o", "w_in"):
        send_sem, recv_sem, ps, rx = in_flight[n]
        ps, rx = _chip_exchange_wait(send_sem, recv_sem, ps, rx, tok, "rs_" + n + "_chip_wait")
        oriented = (lambda a: a[0].T) if n in ("w_in", "w_gate_up") else (lambda a: a[0])
        back = (lambda a: a.T[None]) if n in ("w_in", "w_gate_up") else (lambda a: a[None])
        if n in ("w_attn_o", "w_conv_o"):
            g = _sum_chips(ps, rx, "rs_" + n + "_sum").T
            dl, nm, nv = _adamw(oriented(weights[n]), g, oriented(m_in[n]), oriented(v_in[n]), "adamw_" + n)
        else:
            g, dl, nm, nv = _sum_chips_adamw(ps, rx, oriented(weights[n]), oriented(m_in[n]), oriented(v_in[n]), "rs_" + n + "_sum_adamw")
        grads[n], delta[n], new_m[n], new_v[n] = back(g), back(dl), back(nm), back(nv)
        tok = dl

    slab, slab_layout = _pack([loss_part[:, :1], d_mix_g, d_b_in, d_sinks[:, :N_Q_HEADS], d_conv_b, d_ln_g, d_ln_b, d_bco,
                               d_ffn_g, d_final_g, d_conv_w, d_meta])
    slab_all = _all_gather_rows(slab, "gather_small_grads", after=tok).reshape(N_DEV, *slab.shape)
    (loss, g_mix_g, g_b_in, g_sinks, g_conv_b, g_ln_g, g_ln_b, g_bco, g_ffn_g, g_final_g, g_conv_w_full, g_meta_full
     ) = _unpack(_sum_blocks(slab_all, "sum_small_grads"), slab_layout)
    g_conv_w = lax.dynamic_slice(g_conv_w_full, (0, dev * cw_cols), (CONV_WIDTH, cw_cols)).reshape(conv_w.shape)
    g_meta = lax.dynamic_slice(g_meta_full, (0, dev * meta_cols), (N_META, meta_cols))
    g_final_g = g_final_g.reshape(final_norm_g.shape)
    grads.update(meta_tokens=g_meta, mix_norm_g=g_mix_g, b_in=g_b_in, attn_sinks=g_sinks, conv_w=g_conv_w, conv_b=g_conv_b,
                 conv_ln_g=g_ln_g, conv_ln_b=g_ln_b, b_conv_o=g_bco, ffn_norm_g=g_ffn_g, final_norm_g=g_final_g)
    rest = [n for n in names if n not in delta]
    w_slab, rest_layout = _pack([weights[n] for n in rest])
    g_slab, _ = _pack([grads[n] for n in rest])
    m_slab, _ = _pack([m_in[n] for n in rest])
    v_slab, _ = _pack([v_in[n] for n in rest])
    dl, nm, nv = _adamw(w_slab, g_slab, m_slab, v_slab, "adamw_small")
    for n, a, b, c in zip(rest, _unpack(dl, rest_layout), _unpack(nm, rest_layout), _unpack(nv, rest_layout)):
        delta[n], new_m[n], new_v[n] = a, b, c

    return (loss.reshape(()), grad_x[None], *[grads[n] for n in names], *[delta[n] for n in names],
            *[new_m[n] for n in names], *[new_v[n] for n in names])
```

```python
import functools
import math

import jax
import jax.numpy as jnp
from jax import lax
from jax.experimental import pallas as pl
from jax.experimental.pallas import tpu as pltpu

F32 = jnp.float32
BF16 = jnp.bfloat16

N_DEV = 8
BLOCK = 128
N_META = 16
PAD_ROWS = BLOCK - N_META
HEAD_DIM = 64
N_Q_HEADS = 32
N_KV_HEADS = 4
GROUP = N_Q_HEADS // N_KV_HEADS
Q_DIM = N_Q_HEADS * HEAD_DIM
KV_DIM = N_KV_HEADS * HEAD_DIM
WINDOW = 128
CONV_WIDTH = 31
CONV_ROWS = 32
ROPE_THETA = 10000.0
EPS = 1e-6
ATTN_SCALE = HEAD_DIM ** -0.5
NEG = -1e30

ADAM_LR = 0.001
ADAM_B1 = 0.9
ADAM_B2 = 0.999
ADAM_EPS = 1e-08
ADAM_WD = 0.01
ADAM_STEP = 10

VMEM_LIMIT_BYTES = 56 * 1024 * 1024
LANES = 128
ELEMENTWISE_BLOCK_BYTES = 2 * 1024 * 1024
MESH = pl.DeviceIdType.MESH
CHIPS = ((0, 0), (0, 1), (1, 0), (1, 1))


def _pcall(body, after=None, **kw):
    if after is None:
        return pl.pallas_call(body, **kw)
    in_specs = list(kw.pop("in_specs"))
    n_in = len(in_specs)

    def ordered_body(*refs):
        return body(*refs[:n_in], *refs[n_in + 1:])

    call = pl.pallas_call(ordered_body, in_specs=in_specs + [pl.BlockSpec(memory_space=pl.ANY)], **kw)
    return lambda *args: call(*args, after)


def _params(semantics=None):
    if semantics is None:
        return pltpu.CompilerParams(vmem_limit_bytes=VMEM_LIMIT_BYTES)
    return pltpu.CompilerParams(dimension_semantics=semantics, vmem_limit_bytes=VMEM_LIMIT_BYTES)


def _pick(dim, pref, align):
    best = None
    t = align
    while t <= min(dim, pref):
        if dim % t == 0:
            best = t
        t += align
    return dim if best is None else best


def _sigmoid(x):
    return 1.0 / (1.0 + jnp.exp(-x))


def _matmul(a, b, *, mode, name, out_dtype, tm, tn, tk, bias=None, residual=None, b_inner=True,
            b_row_off=0, b_rows=None, after=None):
    halves = a.ndim == 3
    a_rows, a_cols = (a.shape[1], 2 * a.shape[2]) if halves else a.shape
    if mode == "nn":
        m, k = a_rows, a_cols
        n = b.shape[1]
    elif mode == "nt":
        m, k = a_rows, a_cols
        n = b.shape[0] if b_rows is None else b_rows
    else:
        k, m = a_rows, a_cols
        n = b.shape[1]
    tm = _pick(m // 2 if halves and mode == "tn" else m, tm, 16)
    tn = _pick(math.gcd(n, b_row_off) if mode == "nt" and b_row_off else n, tn, LANES)
    tk = _pick(k // 2 if halves and mode == "nn" else k, tk, LANES if mode != "tn" else 16)
    nm, nn, nk = m // tm, n // tn, k // tk
    if mode == "nt":
        assert b_row_off % tn == 0
    off = b_row_off // tn if mode == "nt" else 0

    if b_inner:
        grid = (nm, nn, nk)
        ij = lambda g0, g1: (g0, g1)
    else:
        grid = (nn, nm, nk)
        ij = lambda g0, g1: (g1, g0)

    if halves and mode == "tn":
        a_spec = pl.BlockSpec((None, tk, tm), lambda g0, g1, kk: (ij(g0, g1)[0] // (nm // 2), kk, ij(g0, g1)[0] % (nm // 2)))
    elif halves:
        assert mode == "nn"
        a_spec = pl.BlockSpec((None, tm, tk), lambda g0, g1, kk: (kk // (nk // 2), ij(g0, g1)[0], kk % (nk // 2)))
    elif mode == "tn":
        a_spec = pl.BlockSpec((tk, tm), lambda g0, g1, kk: (kk, ij(g0, g1)[0]))
    else:
        a_spec = pl.BlockSpec((tm, tk), lambda g0, g1, kk: (ij(g0, g1)[0], kk))
    if mode == "nt":
        b_spec = pl.BlockSpec((tn, tk), lambda g0, g1, kk: (ij(g0, g1)[1] + off, kk))
    else:
        b_spec = pl.BlockSpec((tk, tn), lambda g0, g1, kk: (kk, ij(g0, g1)[1]))
    o_spec = pl.BlockSpec((tm, tn), lambda g0, g1, kk: ij(g0, g1))
    in_specs = [a_spec, b_spec]
    args = [a, b]
    if bias is not None:
        in_specs.append(pl.BlockSpec((1, tn), lambda g0, g1, kk: (0, ij(g0, g1)[1])))
        args.append(bias)
    if residual is not None:
        in_specs.append(o_spec)
        args.append(residual)
    dims = {"nn": (((1,), (0,)), ((), ())), "nt": (((1,), (1,)), ((), ())), "tn": (((0,), (0,)), ((), ()))}[mode]
    has_bias, has_res = bias is not None, residual is not None

    def body(*refs):
        a_ref, b_ref = refs[0], refs[1]
        pos = 2
        bias_ref = res_ref = None
        if has_bias:
            bias_ref = refs[pos]
            pos += 1
        if has_res:
            res_ref = refs[pos]
            pos += 1
        o_ref = refs[pos]
        acc_ref = refs[pos + 1] if nk > 1 else None

        def finish(acc):
            if has_bias:
                acc = acc + bias_ref[...]
            if has_res:
                acc = acc + res_ref[...]
            o_ref[...] = acc.astype(out_dtype)

        p = lax.dot_general(a_ref[...], b_ref[...], dims, preferred_element_type=F32)
        if nk == 1:
            finish(p)
        else:
            kk = pl.program_id(2)

            @pl.when(kk == 0)
            def _():
                acc_ref[...] = p

            @pl.when(kk > 0)
            def _():
                acc_ref[...] += p

            @pl.when(kk == nk - 1)
            def _():
                finish(acc_ref[...])

    return _pcall(
        body, after=after, name=name, grid=grid, in_specs=in_specs, out_specs=o_spec,
        out_shape=jax.ShapeDtypeStruct((m, n), out_dtype),
        scratch_shapes=[pltpu.VMEM((tm, tn), F32)] if nk > 1 else [],
        compiler_params=_params(("parallel", "parallel", "arbitrary")),
    )(*args)


def _row_spec(width, col=0):
    return pl.BlockSpec((BLOCK, width), lambda i: (i, col))


def _const_spec(shape):
    nd = len(shape)
    return pl.BlockSpec(shape, lambda i: (0,) * nd)


def _prep(x, meta_full, g, after=None):
    s, d = x.shape
    lp = s + BLOCK
    nb = lp // BLOCK

    def body(x_ref, meta_ref, g_ref, h_ref, u_ref):
        i = pl.program_id(0)

        @pl.when(i == 0)
        def _():
            h_ref[0:PAD_ROWS, :] = jnp.zeros((PAD_ROWS, d), F32)
            h_ref[PAD_ROWS:BLOCK, :] = meta_ref[...]

        @pl.when(i > 0)
        def _():
            h_ref[...] = x_ref[...]

        h = h_ref[...]
        r = lax.rsqrt(jnp.mean(h * h, axis=-1, keepdims=True) + EPS)
        u_ref[...] = (h * r * g_ref[...]).astype(BF16)

    return _pcall(
        body, after=after, name="prep_rmsnorm", grid=(nb,),
        in_specs=[pl.BlockSpec((BLOCK, d), lambda i: (jnp.maximum(i - 1, 0), 0)), _const_spec((N_META, d)), _const_spec((1, d))],
        out_specs=[_row_spec(d), _row_spec(d)],
        out_shape=[jax.ShapeDtypeStruct((lp, d), F32), jax.ShapeDtypeStruct((lp, d), BF16)],
        compiler_params=_params(("arbitrary",)),
    )(x, meta_full, g)


def _rmsnorm_fwd(h, g, name):
    lp, d = h.shape

    def body(h_ref, g_ref, u_ref):
        x = h_ref[...]
        r = lax.rsqrt(jnp.mean(x * x, axis=-1, keepdims=True) + EPS)
        u_ref[...] = (x * r * g_ref[...]).astype(BF16)

    return _pcall(
        body, name=name, grid=(lp // BLOCK,), in_specs=[_row_spec(d), _const_spec((1, d))], out_specs=_row_spec(d),
        out_shape=jax.ShapeDtypeStruct((lp, d), BF16), compiler_params=_params(("parallel",)),
    )(h, g)


def _rms_bwd_core(dy, x, g):
    r = lax.rsqrt(jnp.mean(x * x, axis=-1, keepdims=True) + EPS)
    xhat = x * r
    dxhat = dy * g
    dx = r * (dxhat - xhat * jnp.mean(dxhat * xhat, axis=-1, keepdims=True))
    return dx, jnp.sum(dy * xhat, axis=0, keepdims=True)


def _rmsnorm_bwd(dy, h, g, dres, name):
    lp, d = h.shape

    def body(dy_ref, h_ref, g_ref, dres_ref, dh_ref, dhb_ref, dg_ref):
        i = pl.program_id(0)
        dx, dg = _rms_bwd_core(dy_ref[...], h_ref[...], g_ref[...])
        dh = dres_ref[...] + dx
        dh_ref[...] = dh
        dhb_ref[...] = dh.astype(BF16)

        @pl.when(i == 0)
        def _():
            dg_ref[...] = jnp.zeros_like(dg_ref)

        dg_ref[...] += dg

    return _pcall(
        body, name=name, grid=(lp // BLOCK,),
        in_specs=[_row_spec(d), _row_spec(d), _const_spec((1, d)), _row_spec(d)],
        out_specs=[_row_spec(d), _row_spec(d), _const_spec((1, d))],
        out_shape=[jax.ShapeDtypeStruct((lp, d), F32), jax.ShapeDtypeStruct((lp, d), BF16), jax.ShapeDtypeStruct((1, d), F32)],
        compiler_params=_params(("arbitrary",)),
    )(dy, h, g, dres)


def _rmsnorm_bwd_first(dy, h, g, dres):
    lp, d = h.shape
    s = lp - BLOCK

    def body(dy_ref, h_ref, g_ref, dres_ref, gx_ref, dmeta_ref, dg_ref):
        i = pl.program_id(0)
        dx, dg = _rms_bwd_core(dy_ref[...], h_ref[...], g_ref[...])
        dh = dres_ref[...] + dx
        gx_ref[...] = dh

        @pl.when(i == 0)
        def _():
            dmeta_ref[...] = dh[PAD_ROWS:BLOCK, :]
            dg_ref[...] = jnp.zeros_like(dg_ref)

        dg_ref[...] += dg

    return _pcall(
        body, name="rmsnorm_bwd_first", grid=(lp // BLOCK,),
        in_specs=[_row_spec(d), _row_spec(d), _const_spec((1, d)), _row_spec(d)],
        out_specs=[pl.BlockSpec((BLOCK, d), lambda i: (jnp.maximum(i - 1, 0), 0)), _const_spec((N_META, d)), _const_spec((1, d))],
        out_shape=[jax.ShapeDtypeStruct((s, d), F32), jax.ShapeDtypeStruct((N_META, d), F32), jax.ShapeDtypeStruct((1, d), F32)],
        compiler_params=_params(("arbitrary",)),
    )(dy, h, g, dres)


def _final(h2, tgt, g):
    lp, d = h2.shape

    def body(h_ref, t_ref, g_ref, dh_ref, dhb_ref, loss_ref, dg_ref):
        i = pl.program_id(0)
        x = h_ref[...]
        gg = g_ref[...]
        r = lax.rsqrt(jnp.mean(x * x, axis=-1, keepdims=True) + EPS)
        xhat = x * r
        y = xhat * gg
        live = (i > 0).astype(F32)
        err = (y - t_ref[...]) * live
        dy = err * (1.0 / d)
        dxhat = dy * gg
        dh = r * (dxhat - xhat * jnp.mean(dxhat * xhat, axis=-1, keepdims=True))
        dh_ref[...] = dh
        dhb_ref[...] = dh.astype(BF16)

        @pl.when(i == 0)
        def _():
            loss_ref[...] = jnp.zeros_like(loss_ref)
            dg_ref[...] = jnp.zeros_like(dg_ref)

        row_loss = jnp.mean(err * err, axis=-1, keepdims=True)
        loss_ref[...] += 0.5 * jnp.sum(row_loss, axis=0, keepdims=True)
        dg_ref[...] += jnp.sum(dy * xhat, axis=0, keepdims=True)

    return _pcall(
        body, name="final_norm_loss", grid=(lp // BLOCK,),
        in_specs=[_row_spec(d), pl.BlockSpec((BLOCK, d), lambda i: (jnp.maximum(i - 1, 0), 0)), _const_spec((1, d))],
        out_specs=[_row_spec(d), _row_spec(d), _const_spec((1, LANES)), _const_spec((1, d))],
        out_shape=[jax.ShapeDtypeStruct((lp, d), F32), jax.ShapeDtypeStruct((lp, d), BF16),
                   jax.ShapeDtypeStruct((1, LANES), F32), jax.ShapeDtypeStruct((1, d), F32)],
        compiler_params=_params(("arbitrary",)),
    )(h2, tgt, g)


def _swap_halves(x):
    w = x.shape[1]
    lane = lax.broadcasted_iota(jnp.int32, x.shape, 1)
    first = (lane & (HEAD_DIM - 1)) < (HEAD_DIM // 2)
    return jnp.where(first, pltpu.roll(x, w - HEAD_DIM // 2, 1), pltpu.roll(x, HEAD_DIM // 2, 1))


def _rope_tables(lp):
    pos = jnp.maximum(jnp.arange(lp, dtype=jnp.int32) - PAD_ROWS, 0).astype(F32)
    inv_freq = ROPE_THETA ** (-jnp.arange(0, HEAD_DIM, 2, dtype=F32) / HEAD_DIM)
    ang = pos[:, None] * inv_freq[None, :]
    c, s = jnp.cos(ang), jnp.sin(ang)
    reps = LANES // HEAD_DIM
    return jnp.tile(jnp.concatenate([c, c], axis=1), (1, reps)), jnp.tile(jnp.concatenate([-s, s], axis=1), (1, reps))


def _rope_fwd(zq, zkv, ctab, stab, after=None):
    lp = zq.shape[0]
    nb = lp // BLOCK
    back = lambda s: (jnp.maximum(s - 1, 0), 0)

    def body(zq_ref, zkv_ref, c_ref, s_ref, q_ref, k_ref, v_ref):
        step = pl.program_id(0)
        c128, s128 = c_ref[...], s_ref[...]

        def rope(x):
            reps = x.shape[1] // LANES
            return x * jnp.tile(c128, (1, reps)) + _swap_halves(x) * jnp.tile(s128, (1, reps))

        q_ref[...] = (rope(zq_ref[...].astype(F32)) * ATTN_SCALE).astype(BF16)
        kv = zkv_ref[...].astype(F32)
        k = rope(kv[:, :KV_DIM])
        v = kv[:, KV_DIM:]

        @pl.when(step == 0)
        def _():
            k_ref[...] = jnp.zeros_like(k_ref)
            v_ref[...] = jnp.zeros_like(v_ref)

        @pl.when(step > 0)
        def _():
            for h in range(N_KV_HEADS):
                k_ref[h] = k[:, h * HEAD_DIM:(h + 1) * HEAD_DIM].astype(BF16)
                v_ref[h] = v[:, h * HEAD_DIM:(h + 1) * HEAD_DIM].astype(BF16)

    kv_spec = pl.BlockSpec((N_KV_HEADS, BLOCK, HEAD_DIM), lambda s: (0, s, 0))
    return _pcall(
        body, after=after, name="rope_fwd", grid=(nb + 1,),
        in_specs=[pl.BlockSpec((BLOCK, Q_DIM), back), pl.BlockSpec((BLOCK, 2 * KV_DIM), back),
                  pl.BlockSpec((BLOCK, LANES), back), pl.BlockSpec((BLOCK, LANES), back)],
        out_specs=[pl.BlockSpec((BLOCK, Q_DIM), back), kv_spec, kv_spec],
        out_shape=[jax.ShapeDtypeStruct((lp, Q_DIM), BF16),
                   jax.ShapeDtypeStruct((N_KV_HEADS, lp + BLOCK, HEAD_DIM), BF16),
                   jax.ShapeDtypeStruct((N_KV_HEADS, lp + BLOCK, HEAD_DIM), BF16)],
        compiler_params=_params(("arbitrary",)),
    )(zq, zkv, ctab, stab)


def _rope_bwd(dq, dk, dv, dkm, dvm, ctab, stab):
    lp = dq.shape[0]
    width = Q_DIM + 2 * KV_DIM
    head_spec = pl.BlockSpec((N_KV_HEADS, BLOCK, HEAD_DIM), lambda i: (0, i, 0))
    meta_spec = _const_spec((N_KV_HEADS, BLOCK, HEAD_DIM))

    def body(dq_ref, dk_ref, dv_ref, dkm_ref, dvm_ref, c_ref, s_ref, dz_ref, sum_ref, kbuf, vbuf):
        i = pl.program_id(0)
        c128, s128 = c_ref[...], s_ref[...]
        first = (i == 0).astype(F32)

        def rope_t(x):
            reps = x.shape[1] // LANES
            return x * jnp.tile(c128, (1, reps)) + _swap_halves(x * jnp.tile(s128, (1, reps)))

        for h in range(N_KV_HEADS):
            kbuf[:, h * HEAD_DIM:(h + 1) * HEAD_DIM] = dk_ref[h] + first * dkm_ref[h]
            vbuf[:, h * HEAD_DIM:(h + 1) * HEAD_DIM] = dv_ref[h] + first * dvm_ref[h]
        dzq = rope_t(dq_ref[...] * ATTN_SCALE)
        dzk = rope_t(kbuf[...])
        dzv = vbuf[...]
        dz_ref[:, 0:Q_DIM] = dzq.astype(BF16)
        dz_ref[:, Q_DIM:Q_DIM + KV_DIM] = dzk.astype(BF16)
        dz_ref[:, Q_DIM + KV_DIM:width] = dzv.astype(BF16)

        @pl.when(i == 0)
        def _():
            sum_ref[...] = jnp.zeros_like(sum_ref)

        sum_ref[:, 0:Q_DIM] += jnp.sum(dzq, axis=0, keepdims=True)
        sum_ref[:, Q_DIM:Q_DIM + KV_DIM] += jnp.sum(dzk, axis=0, keepdims=True)
        sum_ref[:, Q_DIM + KV_DIM:width] += jnp.sum(dzv, axis=0, keepdims=True)

    return _pcall(
        body, name="rope_bwd", grid=(lp // BLOCK,),
        in_specs=[_row_spec(Q_DIM), head_spec, head_spec, meta_spec, meta_spec, _row_spec(LANES), _row_spec(LANES)],
        out_specs=[_row_spec(width), _const_spec((1, width))],
        out_shape=[jax.ShapeDtypeStruct((lp, width), BF16), jax.ShapeDtypeStruct((1, width), F32)],
        scratch_shapes=[pltpu.VMEM((BLOCK, KV_DIM), F32), pltpu.VMEM((BLOCK, KV_DIM), F32)],
        compiler_params=_params(("arbitrary",)),
    )(dq, dk, dv, dkm, dvm, ctab, stab)


def _attn_bias(i):
    r = lax.broadcasted_iota(jnp.int32, (BLOCK, 3 * BLOCK), 0)
    c = lax.broadcasted_iota(jnp.int32, (BLOCK, 3 * BLOCK), 1)
    qp = i * BLOCK + r - PAD_ROWS
    kp = (i - 1) * BLOCK + c - PAD_ROWS
    band = (c < 2 * BLOCK) & (kp >= N_META) & (kp <= qp) & (qp - kp < WINDOW)
    mp = c - 2 * BLOCK - PAD_ROWS
    meta = (c >= 2 * BLOCK) & (mp >= 0) & (mp <= qp)
    return jnp.where(band | meta, 0.0, NEG).astype(F32)


HALF = BLOCK // 2
HALF_KEYS = 2 * BLOCK


def _half_keys(prev, own, meta, half):
    if half == 0:
        return jnp.concatenate([prev, own[0:HALF], meta[HALF:BLOCK]], axis=0)
    return jnp.concatenate([prev[HALF:BLOCK], own, meta[HALF:BLOCK]], axis=0)


def _half_bias(i, half):
    r = lax.broadcasted_iota(jnp.int32, (HALF, HALF_KEYS), 0) + half * HALF
    c = lax.broadcasted_iota(jnp.int32, (HALF, HALF_KEYS), 1)
    n_prev = BLOCK - half * HALF
    qp = i * BLOCK + r - PAD_ROWS
    kp = jnp.where(c < n_prev, (i - 1) * BLOCK + c + half * HALF, i * BLOCK + c - n_prev) - PAD_ROWS
    band = (c < HALF_KEYS - HALF) & (kp >= N_META) & (kp <= qp) & (qp - kp < WINDOW)
    mp = c - (HALF_KEYS - HALF) + HALF - PAD_ROWS
    meta = (c >= HALF_KEYS - HALF) & (mp >= 0) & (mp <= qp)
    return jnp.where(band | meta, 0.0, NEG).astype(F32)


def _half_rows(ref, heads, half):
    rows = slice(half * HALF, (half + 1) * HALF)
    return jnp.concatenate([ref[rows, n * HEAD_DIM:(n + 1) * HEAD_DIM] for n in heads], axis=0)


def _half_sinks(sink_ref, heads):
    return jnp.concatenate([jnp.broadcast_to(sink_ref[0:1, n:n + 1], (HALF, 1)) for n in heads], axis=0)


def _stack_heads(ref, h):
    return jnp.concatenate(
        [ref[:, (h * GROUP + g) * HEAD_DIM:(h * GROUP + g + 1) * HEAD_DIM] for g in range(GROUP)], axis=0)


def _attn_probs(qs, k3, bias8, sink):
    s = lax.dot_general(qs, k3, (((1,), (1,)), ((), ())), preferred_element_type=F32) + bias8
    m = jnp.maximum(jnp.max(s, axis=1, keepdims=True), sink)
    p = jnp.exp(s - m)
    ps = jnp.exp(sink - m)
    inv = 1.0 / (jnp.sum(p, axis=1, keepdims=True) + ps)
    return p * inv, ps * inv


def _sink_column(sink_ref, h):
    return jnp.concatenate(
        [jnp.broadcast_to(sink_ref[0:1, h * GROUP + g:h * GROUP + g + 1], (BLOCK, 1)) for g in range(GROUP)], axis=0)


def _attn_fwd(q, k_sh, v_sh, sinks):
    lp = q.shape[0]
    nb = lp // BLOCK
    kv = lambda f: pl.BlockSpec((N_KV_HEADS, BLOCK, HEAD_DIM), f)

    def body(q_ref, kp_ref, kc_ref, km_ref, vp_ref, vc_ref, vm_ref, sink_ref, o_ref):
        i = pl.program_id(0)
        for half in range(2):
            bias = jnp.tile(_half_bias(i, half), (GROUP, 1))
            rows = slice(half * HALF, (half + 1) * HALF)
            for h in range(N_KV_HEADS):
                heads = range(h * GROUP, (h + 1) * GROUP)
                keys = _half_keys(kp_ref[h], kc_ref[h], km_ref[h], half)
                vals = _half_keys(vp_ref[h], vc_ref[h], vm_ref[h], half)
                p, _ = _attn_probs(_half_rows(q_ref, heads, half), keys, bias, _half_sinks(sink_ref, heads))
                o = jnp.dot(p.astype(BF16), vals, preferred_element_type=F32)
                for j, n in enumerate(heads):
                    o_ref[rows, n * HEAD_DIM:(n + 1) * HEAD_DIM] = o[j * HALF:(j + 1) * HALF].astype(BF16)

    prev, cur, meta = (lambda i: (0, i, 0)), (lambda i: (0, i + 1, 0)), (lambda i: (0, 1, 0))
    return _pcall(
        body, name="attn_fwd", grid=(nb,),
        in_specs=[_row_spec(Q_DIM), kv(prev), kv(cur), kv(meta), kv(prev), kv(cur), kv(meta), _const_spec((1, N_Q_HEADS))],
        out_specs=_row_spec(Q_DIM), out_shape=jax.ShapeDtypeStruct((lp, Q_DIM), BF16),
        compiler_params=_params(("parallel",)),
    )(q, k_sh, k_sh, k_sh, v_sh, v_sh, v_sh, sinks)


def _attn_bwd(q, k_sh, v_sh, sinks, do):
    lp = q.shape[0]
    nb = lp // BLOCK
    kv = lambda f: pl.BlockSpec((N_KV_HEADS, BLOCK, HEAD_DIM), f)
    cl = lambda s: jnp.minimum(s, nb - 1)

    def body(q_ref, do_ref, kp_ref, kc_ref, km_ref, vp_ref, vc_ref, vm_ref, sink_ref,
             dq_ref, dk_ref, dv_ref, dkm_ref, dvm_ref, dsink_ref, carry_k, carry_v):
        step = pl.program_id(0)

        @pl.when(step == 0)
        def _():
            carry_k[...] = jnp.zeros_like(carry_k)
            carry_v[...] = jnp.zeros_like(carry_v)
            dkm_ref[...] = jnp.zeros_like(dkm_ref)
            dvm_ref[...] = jnp.zeros_like(dvm_ref)
            dsink_ref[...] = jnp.zeros_like(dsink_ref)

        @pl.when(step < nb)
        def _():
            bias8 = jnp.tile(_attn_bias(step), (GROUP, 1))
            lane = lax.broadcasted_iota(jnp.int32, (1, LANES), 1)
            dsink = jnp.zeros((1, LANES), F32)
            for h in range(N_KV_HEADS):
                k3 = jnp.concatenate([kp_ref[h], kc_ref[h], km_ref[h]], axis=0)
                v3 = jnp.concatenate([vp_ref[h], vc_ref[h], vm_ref[h]], axis=0)
                qs = _stack_heads(q_ref, h)
                dos = _stack_heads(do_ref, h)
                p, psink = _attn_probs(qs, k3, bias8, _sink_column(sink_ref, h))
                dp = lax.dot_general(dos, v3, (((1,), (1,)), ((), ())), preferred_element_type=F32)
                delta = jnp.sum(p * dp, axis=1, keepdims=True)
                ds = (p * (dp - delta)).astype(BF16)
                dsk = -psink * delta
                for g in range(GROUP):
                    val = jnp.sum(dsk[g * BLOCK:(g + 1) * BLOCK], axis=0, keepdims=True)
                    dsink = dsink + jnp.where(lane == h * GROUP + g, val, 0.0)
                dqs = jnp.dot(ds, k3, preferred_element_type=F32)
                for g in range(GROUP):
                    n = h * GROUP + g
                    dq_ref[:, n * HEAD_DIM:(n + 1) * HEAD_DIM] = dqs[g * BLOCK:(g + 1) * BLOCK]
                dk3 = lax.dot_general(ds, qs, (((0,), (0,)), ((), ())), preferred_element_type=F32)
                dv3 = lax.dot_general(p.astype(BF16), dos, (((0,), (0,)), ((), ())), preferred_element_type=F32)
                dk_ref[h] = carry_k[h] + dk3[0:BLOCK]
                dv_ref[h] = carry_v[h] + dv3[0:BLOCK]
                carry_k[h] = dk3[BLOCK:2 * BLOCK]
                carry_v[h] = dv3[BLOCK:2 * BLOCK]
                dkm_ref[h] += dk3[2 * BLOCK:3 * BLOCK]
                dvm_ref[h] += dv3[2 * BLOCK:3 * BLOCK]
            dsink_ref[...] += dsink

        @pl.when(step == nb)
        def _():
            dk_ref[...] = carry_k[...]
            dv_ref[...] = carry_v[...]

    prev, cur, meta = (lambda s: (0, cl(s), 0)), (lambda s: (0, cl(s) + 1, 0)), (lambda s: (0, 1, 0))
    lag = lambda s: (0, jnp.maximum(s - 1, 0), 0)
    head_shape = jax.ShapeDtypeStruct((N_KV_HEADS, lp, HEAD_DIM), F32)
    meta_shape = jax.ShapeDtypeStruct((N_KV_HEADS, BLOCK, HEAD_DIM), F32)
    return _pcall(
        body, name="attn_bwd", grid=(nb + 1,),
        in_specs=[pl.BlockSpec((BLOCK, Q_DIM), lambda s: (cl(s), 0)), pl.BlockSpec((BLOCK, Q_DIM), lambda s: (cl(s), 0)),
                  kv(prev), kv(cur), kv(meta), kv(prev), kv(cur), kv(meta), _const_spec((1, N_Q_HEADS))],
        out_specs=[pl.BlockSpec((BLOCK, Q_DIM), lambda s: (cl(s), 0)), kv(lag), kv(lag),
                   _const_spec((N_KV_HEADS, BLOCK, HEAD_DIM)), _const_spec((N_KV_HEADS, BLOCK, HEAD_DIM)), _const_spec((1, LANES))],
        out_shape=[jax.ShapeDtypeStruct((lp, Q_DIM), F32), head_shape, head_shape, meta_shape, meta_shape,
                   jax.ShapeDtypeStruct((1, LANES), F32)],
        scratch_shapes=[pltpu.VMEM((N_KV_HEADS, BLOCK, HEAD_DIM), F32), pltpu.VMEM((N_KV_HEADS, BLOCK, HEAD_DIM), F32)],
        compiler_params=_params(("arbitrary",)),
    )(q, do, k_sh, k_sh, k_sh, v_sh, v_sh, v_sh, sinks)


CONV_CHUNK = 256


SUBLANES = 8
SH_BASE = BLOCK - 4 * SUBLANES
SH_ROWS = BLOCK + 3 * SUBLANES
DSH_ROWS = SH_ROWS


def _shifted_windows(src, sh, base, rows):
    for b in range(1, SUBLANES):
        sh[b] = src[base + b:base + b + rows, :]


def _window(src, sh, base, start, cols):
    a, b = divmod(start - base, SUBLANES)
    if b == 0:
        return src[start:start + BLOCK, cols]
    return sh[b, SUBLANES * a:SUBLANES * a + BLOCK, cols]


def _glu_masked(a_ref, g_ref, base):
    rows = base + lax.broadcasted_iota(jnp.int32, (BLOCK, 1), 0)
    return jnp.where(rows >= PAD_ROWS, a_ref[...].astype(F32) * _sigmoid(g_ref[...].astype(F32)), 0.0)


def _conv_fwd(zc, conv_w, conv_b, ln_g, ln_b, after=None):
    lp = zc.shape[0]
    cd = zc.shape[1] // 2
    nb = lp // BLOCK
    chunk = min(CONV_CHUNK, cd)
    back = lambda col: (lambda i: (jnp.maximum(i - 1, 0), col))
    lo = BLOCK - (CONV_WIDTH - 1)

    def body(ap_ref, gp_ref, ac_ref, gc_ref, w_ref, b_ref, lg_ref, lb_ref, co_ref, c2_ref, ext, sh):
        i = pl.program_id(0)
        ext[0:BLOCK, :] = _glu_masked(ap_ref, gp_ref, (i - 1) * BLOCK)
        ext[BLOCK:2 * BLOCK, :] = _glu_masked(ac_ref, gc_ref, i * BLOCK)
        _shifted_windows(ext, sh, SH_BASE, SH_ROWS)
        for c0 in range(0, cd, chunk):
            cols = slice(c0, c0 + chunk)
            acc = jnp.zeros((BLOCK, chunk), F32)
            for k in range(CONV_WIDTH):
                acc = acc + _window(ext, sh, SH_BASE, lo + k, cols) * w_ref[k:k + 1, cols]
            co_ref[:, cols] = acc + b_ref[:, cols]
        x = co_ref[...]
        mu = jnp.mean(x, axis=-1, keepdims=True)
        xc = x - mu
        r = lax.rsqrt(jnp.mean(xc * xc, axis=-1, keepdims=True) + EPS)
        y = xc * r * lg_ref[...] + lb_ref[...]
        c2_ref[...] = (y * _sigmoid(y)).astype(BF16)

    return _pcall(
        body, after=after, name="conv_fwd", grid=(nb,),
        in_specs=[pl.BlockSpec((BLOCK, cd), back(0)), pl.BlockSpec((BLOCK, cd), back(1)), _row_spec(cd, 0), _row_spec(cd, 1),
                  _const_spec((CONV_ROWS, cd)), _const_spec((1, cd)), _const_spec((1, cd)), _const_spec((1, cd))],
        out_specs=[_row_spec(cd), _row_spec(cd)],
        out_shape=[jax.ShapeDtypeStruct((lp, cd), F32), jax.ShapeDtypeStruct((lp, cd), BF16)],
        scratch_shapes=[pltpu.VMEM((2 * BLOCK, cd), F32), pltpu.VMEM((SUBLANES, SH_ROWS, cd), F32)],
        compiler_params=_params(("arbitrary",)),
    )(zc, zc, zc, zc, conv_w, conv_b, ln_g, ln_b)


def _conv_bwd_norm(dc2, conv_out, ln_g, ln_b):
    lp, cd = conv_out.shape

    def body(d_ref, x_ref, lg_ref, lb_ref, dco_ref, dlg_ref, dlb_ref, dcb_ref):
        i = pl.program_id(0)
        x = x_ref[...]
        g = lg_ref[...]
        mu = jnp.mean(x, axis=-1, keepdims=True)
        xc = x - mu
        r = lax.rsqrt(jnp.mean(xc * xc, axis=-1, keepdims=True) + EPS)
        xhat = xc * r
        y = xhat * g + lb_ref[...]
        sg = _sigmoid(y)
        dy = d_ref[...] * (sg * (1.0 + y * (1.0 - sg)))
        dxhat = dy * g
        dx = r * (dxhat - jnp.mean(dxhat, axis=-1, keepdims=True) - xhat * jnp.mean(dxhat * xhat, axis=-1, keepdims=True))
        dco_ref[...] = dx

        @pl.when(i == 0)
        def _():
            dlg_ref[...] = jnp.zeros_like(dlg_ref)
            dlb_ref[...] = jnp.zeros_like(dlb_ref)
            dcb_ref[...] = jnp.zeros_like(dcb_ref)

        dlg_ref[...] += jnp.sum(dy * xhat, axis=0, keepdims=True)
        dlb_ref[...] += jnp.sum(dy, axis=0, keepdims=True)
        dcb_ref[...] += jnp.sum(dx, axis=0, keepdims=True)

    vec = jax.ShapeDtypeStruct((1, cd), F32)
    return _pcall(
        body, name="conv_bwd_norm", grid=(lp // BLOCK,),
        in_specs=[_row_spec(cd), _row_spec(cd), _const_spec((1, cd)), _const_spec((1, cd))],
        out_specs=[_row_spec(cd), _const_spec((1, cd)), _const_spec((1, cd)), _const_spec((1, cd))],
        out_shape=[jax.ShapeDtypeStruct((lp, cd), F32), vec, vec, vec],
        compiler_params=_params(("arbitrary",)),
    )(dc2, conv_out, ln_g, ln_b)


def _conv_bwd_taps(dco, zc, conv_w):
    lp, cd = dco.shape
    nb = lp // BLOCK
    chunk = min(CONV_CHUNK, cd)
    back = lambda col: (lambda i: (jnp.maximum(i - 1, 0), col))
    fwd = lambda i: (jnp.minimum(i + 1, nb - 1), 0)
    lo = BLOCK - (CONV_WIDTH - 1)

    def body(dc_ref, dn_ref, ap_ref, gp_ref, ac_ref, gc_ref, w_ref, dz_ref, sum_ref, dw_ref, ext, dext, dcb, sh, dsh):
        i = pl.program_id(0)
        ext[0:BLOCK, :] = _glu_masked(ap_ref, gp_ref, (i - 1) * BLOCK)
        ext[BLOCK:2 * BLOCK, :] = _glu_masked(ac_ref, gc_ref, i * BLOCK)
        dext[0:BLOCK, :] = dc_ref[...]
        dext[BLOCK:2 * BLOCK, :] = dn_ref[...] * (i < nb - 1).astype(F32)
        _shifted_windows(ext, sh, SH_BASE, SH_ROWS)
        _shifted_windows(dext, dsh, 0, DSH_ROWS)

        @pl.when(i == 0)
        def _():
            dw_ref[...] = jnp.zeros_like(dw_ref)
            sum_ref[...] = jnp.zeros_like(sum_ref)

        for c0 in range(0, cd, chunk):
            cols = slice(c0, c0 + chunk)
            dcur = dext[0:BLOCK, cols]
            acc = jnp.zeros((BLOCK, chunk), F32)
            for k in range(CONV_WIDTH):
                s = CONV_WIDTH - 1 - k
                acc = acc + _window(dext, dsh, 0, s, cols) * w_ref[k:k + 1, cols]
                dw_ref[k:k + 1, cols] += jnp.sum(dcur * _window(ext, sh, SH_BASE, lo + k, cols), axis=0, keepdims=True)
            dcb[:, cols] = acc
        rows = i * BLOCK + lax.broadcasted_iota(jnp.int32, (BLOCK, 1), 0)
        dc = jnp.where(rows >= PAD_ROWS, dcb[...], 0.0)
        a = ac_ref[...].astype(F32)
        sg = _sigmoid(gc_ref[...].astype(F32))
        da = dc * sg
        dg = dc * a * sg * (1.0 - sg)
        dz_ref[:, 0:cd] = da.astype(BF16)
        dz_ref[:, cd:2 * cd] = dg.astype(BF16)
        sum_ref[:, 0:cd] += jnp.sum(da, axis=0, keepdims=True)
        sum_ref[:, cd:2 * cd] += jnp.sum(dg, axis=0, keepdims=True)

    return _pcall(
        body, name="conv_bwd_taps", grid=(nb,),
        in_specs=[_row_spec(cd), pl.BlockSpec((BLOCK, cd), fwd),
                  pl.BlockSpec((BLOCK, cd), back(0)), pl.BlockSpec((BLOCK, cd), back(1)), _row_spec(cd, 0), _row_spec(cd, 1),
                  _const_spec((CONV_ROWS, cd))],
        out_specs=[_row_spec(2 * cd), _const_spec((1, 2 * cd)), _const_spec((CONV_ROWS, cd))],
        out_shape=[jax.ShapeDtypeStruct((lp, 2 * cd), BF16), jax.ShapeDtypeStruct((1, 2 * cd), F32),
                   jax.ShapeDtypeStruct((CONV_ROWS, cd), F32)],
        scratch_shapes=[pltpu.VMEM((2 * BLOCK, cd), F32), pltpu.VMEM((2 * BLOCK, cd), F32), pltpu.VMEM((BLOCK, cd), F32),
                        pltpu.VMEM((SUBLANES, SH_ROWS, cd), F32), pltpu.VMEM((SUBLANES, DSH_ROWS, cd), F32)],
        compiler_params=_params(("arbitrary",)),
    )(dco, dco, zc, zc, zc, zc, conv_w)


def _conv_out_gate(c2, wco_t, bias, br_a, zg, *, tm, tn):
    m, k = c2.shape
    d = wco_t.shape[0]
    tm, tn = _pick(m, tm, 16), _pick(d, tn, LANES)
    nj = d // tn

    def body(a_ref, w_ref, bias_ref, bra_ref, ga_ref, gb_ref, brb_ref, m_ref):
        acc = lax.dot_general(a_ref[...], w_ref[...], (((1,), (1,)), ((), ())), preferred_element_type=F32)
        brb = (acc + bias_ref[...]).astype(BF16)
        brb_ref[...] = brb
        ga, gb = ga_ref[...].astype(F32), gb_ref[...].astype(F32)
        m_ref[...] = (_sigmoid(ga) * bra_ref[...].astype(F32) + _sigmoid(gb) * brb.astype(F32)).astype(BF16)

    tile = pl.BlockSpec((tm, tn), lambda i, j: (i, j))
    shape = jax.ShapeDtypeStruct((m, d), BF16)
    return _pcall(
        body, name="conv_out_proj_gate", grid=(m // tm, nj),
        in_specs=[pl.BlockSpec((tm, k), lambda i, j: (i, 0)), pl.BlockSpec((tn, k), lambda i, j: (j, 0)),
                  pl.BlockSpec((1, tn), lambda i, j: (0, j)), tile, tile, pl.BlockSpec((tm, tn), lambda i, j: (i, j + nj))],
        out_specs=[tile, tile], out_shape=[shape, shape], compiler_params=_params(("parallel", "arbitrary")),
    )(c2, wco_t, bias, br_a, zg, zg)


def _gate_bwd(dm, a, b, zg):
    lp, d = a.shape

    def body(dm_ref, a_ref, b_ref, ga_ref, gb_ref, da_ref, db_ref, dz_ref, sum_ref, dbias_ref):
        i = pl.program_id(0)
        dm_ = dm_ref[...].astype(F32)
        sa = _sigmoid(ga_ref[...].astype(F32))
        sb = _sigmoid(gb_ref[...].astype(F32))
        db = dm_ * sb
        dga = dm_ * a_ref[...].astype(F32) * sa * (1.0 - sa)
        dgb = dm_ * b_ref[...].astype(F32) * sb * (1.0 - sb)
        da_ref[...] = (dm_ * sa).astype(BF16)
        db_ref[...] = db.astype(BF16)
        dz_ref[:, 0:d] = dga.astype(BF16)
        dz_ref[:, d:2 * d] = dgb.astype(BF16)

        @pl.when(i == 0)
        def _():
            sum_ref[...] = jnp.zeros_like(sum_ref)
            dbias_ref[...] = jnp.zeros_like(dbias_ref)

        sum_ref[:, 0:d] += jnp.sum(dga, axis=0, keepdims=True)
        sum_ref[:, d:2 * d] += jnp.sum(dgb, axis=0, keepdims=True)
        dbias_ref[...] += jnp.sum(db, axis=0, keepdims=True)

    return _pcall(
        body, name="gate_bwd", grid=(lp // BLOCK,),
        in_specs=[_row_spec(d), _row_spec(d), _row_spec(d), _row_spec(d, 0), _row_spec(d, 1)],
        out_specs=[_row_spec(d), _row_spec(d), _row_spec(2 * d), _const_spec((1, 2 * d)), _const_spec((1, d))],
        out_shape=[jax.ShapeDtypeStruct((lp, d), BF16), jax.ShapeDtypeStruct((lp, d), BF16), jax.ShapeDtypeStruct((lp, 2 * d), BF16),
                   jax.ShapeDtypeStruct((1, 2 * d), F32), jax.ShapeDtypeStruct((1, d), F32)],
        compiler_params=_params(("arbitrary",)),
    )(dm, a, b, zg, zg)


def _gate_up_swiglu(u2, wgu_t, *, row_block, tm, tn, name, filled=None, after=None):
    m, k = u2.shape
    f = wgu_t.shape[0] // 2
    tn = _pick(f, tn, LANES)
    nj = f // tn
    dims = (((1,), (1,)), ((), ()))
    n_filled = 0 if filled is None else 3

    def body(*refs):
        a_ref, wg_ref, wu_ref = refs[:3]
        g_ref, u_ref, act_ref = refs[3 + n_filled:]
        a = a_ref[...]
        g = lax.dot_general(a, wg_ref[...], dims, preferred_element_type=F32).astype(BF16)
        up = lax.dot_general(a, wu_ref[...], dims, preferred_element_type=F32).astype(BF16)
        g_ref[...] = g
        u_ref[...] = up
        gf = g.astype(F32)
        act_ref[...] = (gf * _sigmoid(gf) * up.astype(F32)).astype(BF16)

    out = pl.BlockSpec((tm, tn), lambda j: (row_block, j))
    shape = jax.ShapeDtypeStruct((m, f), BF16)
    return _pcall(
        body, after=after, name=name, grid=(nj,),
        in_specs=[pl.BlockSpec((tm, k), lambda j: (row_block, 0)), pl.BlockSpec((tn, k), lambda j: (j, 0)),
                  pl.BlockSpec((tn, k), lambda j: (j + nj, 0))] + [pl.BlockSpec(memory_space=pl.ANY)] * n_filled,
        out_specs=[out, out, out], out_shape=[shape, shape, shape],
        input_output_aliases={3 + i: i for i in range(n_filled)}, compiler_params=_params(("arbitrary",)),
    )(u2, wgu_t, wgu_t, *(filled or ()))


def _down_dx_swiglu_bwd(dh, wdown, g, up, *, tm, tn, after=None):
    m, k = dh.shape
    f = wdown.shape[0]
    tm, tn = _pick(m, tm, 16), _pick(f, tn, LANES)

    def body(a_ref, w_ref, g_ref, u_ref, o_ref):
        acc = lax.dot_general(a_ref[...], w_ref[...], (((1,), (1,)), ((), ())), preferred_element_type=F32)
        d = acc.astype(BF16).astype(F32)
        gf = g_ref[...].astype(F32)
        sg = _sigmoid(gf)
        o_ref[0] = (d * u_ref[...].astype(F32) * (sg * (1.0 + gf * (1.0 - sg)))).astype(BF16)
        o_ref[1] = (d * gf * sg).astype(BF16)

    tile = pl.BlockSpec((tm, tn), lambda i, j: (i, j))
    return _pcall(
        body, after=after, name="ffn_down_dx_swiglu_bwd", grid=(m // tm, f // tn),
        in_specs=[pl.BlockSpec((tm, k), lambda i, j: (i, 0)), pl.BlockSpec((tn, k), lambda i, j: (j, 0)), tile, tile],
        out_specs=pl.BlockSpec((2, tm, tn), lambda i, j: (0, i, j)), out_shape=jax.ShapeDtypeStruct((2, m, f), BF16),
        compiler_params=_params(("parallel", "arbitrary")),
    )(dh, wdown, g, up)


ANY = pl.BlockSpec(memory_space=pl.ANY)


def _all_gather_rows(x, name, after=None):
    r, c = x.shape

    def body(x_ref, out_ref, send_sems, recv_sems, local_sem):
        mx, my, mc = lax.axis_index("x"), lax.axis_index("y"), lax.axis_index("c")
        me, sibling = (mx, my, mc), (mx, my, 1 - mc)
        chips = [(1 - mx, my), (mx, 1 - my), (1 - mx, 1 - my)]

        def rows(px, py, pc):
            return out_ref.at[pl.ds((4 * px + 2 * py + pc) * r, r), :]

        def copy(k, block, to, src=None):
            return pltpu.make_async_remote_copy(
                src_ref=rows(*block) if src is None else src, dst_ref=rows(*block),
                send_sem=send_sems.at[k], recv_sem=recv_sems.at[k], device_id=to, device_id_type=MESH)

        mine = pltpu.make_async_copy(x_ref, rows(*me), local_sem)
        mine.start()
        first = [copy(0, me, sibling, src=x_ref)]
        first += [copy(1 + j, me, (*chip, mc), src=x_ref) for j, chip in enumerate(chips)]
        for cp in first:
            cp.start()
        passed = [copy(4 + j, (*chip, mc), sibling) for j, chip in enumerate(chips)]
        for j, chip in enumerate(chips):
            copy(1 + j, (*chip, mc), me).wait_recv()
            passed[j].start()
        copy(0, sibling, me).wait_recv()
        for j, chip in enumerate(chips):
            copy(4 + j, (*chip, 1 - mc), me).wait_recv()
        for cp in first + passed:
            cp.wait_send()
        mine.wait()

    return _pcall(
        body, after=after, name=name, in_specs=[ANY], out_specs=ANY, out_shape=jax.ShapeDtypeStruct((N_DEV * r, c), x.dtype),
        scratch_shapes=[pltpu.SemaphoreType.DMA((7,)), pltpu.SemaphoreType.DMA((7,)), pltpu.SemaphoreType.DMA(())],
    )(x)


HBM = pl.BlockSpec(memory_space=pltpu.HBM)
SEM = pl.BlockSpec(memory_space=pltpu.SEMAPHORE)
IN_FLIGHT = pltpu.CompilerParams(has_side_effects=pltpu.SideEffectType.DATAFLOW_SIDE_EFFECTING)


def _place_rows(shard, after, name):
    r, c = shard.shape
    tr = _pick(r, max(16, ELEMENTWISE_BLOCK_BYTES // (4 * c)), 16)
    steps = r // tr
    dev = (4 * lax.axis_index("x") + 2 * lax.axis_index("y") + lax.axis_index("c")).astype(jnp.int32).reshape(1)

    def body(dev_ref, x_ref, after_ref, o_ref):
        o_ref[...] = x_ref[...].astype(BF16)

    return _pcall(
        body, name=name,
        grid_spec=pltpu.PrefetchScalarGridSpec(
            num_scalar_prefetch=1, grid=(steps,),
            in_specs=[pl.BlockSpec((tr, c), lambda i, dev_ref: (i, 0)), pl.BlockSpec(memory_space=pl.ANY)],
            out_specs=pl.BlockSpec((tr, c), lambda i, dev_ref: (dev_ref[0] * steps + i, 0))),
        out_shape=jax.ShapeDtypeStruct((N_DEV * r, c), BF16), compiler_params=_params(("parallel",)),
    )(dev, shard, after)


def _rows_start(full, plan, name, after=None):
    r = full.shape[0] // N_DEV
    n = len(plan(0, 0, 0))

    ordered = after is not None

    def body(*refs):
        full_ref, (send_sems, recv_sems) = refs[0], refs[1 + ordered:3 + ordered]
        mx, my, mc = lax.axis_index("x"), lax.axis_index("y"), lax.axis_index("c")
        for k, ((bx, by, bc), target) in enumerate(plan(mx, my, mc)):
            rows = full_ref.at[pl.ds((4 * bx + 2 * by + bc) * r, r), :]
            pltpu.make_async_remote_copy(
                src_ref=rows, dst_ref=rows, send_sem=send_sems.at[k], recv_sem=recv_sems.at[k],
                device_id=target, device_id_type=MESH).start()

    return pl.pallas_call(
        body, name=name, in_specs=[HBM] + [pl.BlockSpec(memory_space=pl.ANY)] * ordered, out_specs=(SEM, SEM, HBM),
        out_shape=(pltpu.SemaphoreType.DMA((n,)), pltpu.SemaphoreType.DMA((n,)), pltpu.HBM(full.shape, full.dtype)),
        input_output_aliases={0: 2}, compiler_params=IN_FLIGHT,
    )(pltpu.with_memory_space_constraint(full, pltpu.HBM), *([after] if ordered else []))


def _rows_wait(started, after, name):
    send_sem, recv_sem, full = started
    r = full.shape[0] // N_DEV
    n = send_sem.shape[0]

    def body(full_ref, send_ref, recv_ref, after_ref, out_ref):
        mx, my, mc = lax.axis_index("x"), lax.axis_index("y"), lax.axis_index("c")
        block = full_ref.at[pl.ds(0, r), :]
        for k in range(n):
            cp = pltpu.make_async_remote_copy(
                src_ref=block, dst_ref=block, send_sem=send_ref.at[k], recv_sem=recv_ref.at[k],
                device_id=(mx, my, mc), device_id_type=MESH)
            cp.wait_send()
            cp.wait_recv()

    return pl.pallas_call(
        body, name=name, in_specs=[HBM, SEM, SEM, pl.BlockSpec(memory_space=pl.ANY)], out_specs=HBM,
        out_shape=pltpu.HBM(full.shape, full.dtype), input_output_aliases={0: 0}, compiler_params=IN_FLIGHT,
    )(full, send_sem, recv_sem, after)


def _plan_direct(mx, my, mc):
    me = (mx, my, mc)
    return [(me, (mx, my, 1 - mc)), (me, (1 - mx, my, mc)), (me, (mx, 1 - my, mc)), (me, (1 - mx, 1 - my, mc))]


def _plan_pass_on(mx, my, mc):
    sibling = (mx, my, 1 - mc)
    return [((1 - mx, my, mc), sibling), ((mx, 1 - my, mc), sibling), ((1 - mx, 1 - my, mc), sibling)]


def _plan_neighbours(mx, my, mc):
    me = (mx, my, mc)
    return [(me, (mx, my, 1 - mc)), (me, (1 - mx, my, mc)), (me, (mx, 1 - my, mc))]


def _plan_relay(mx, my, mc):
    sibling = (mx, my, 1 - mc)
    source = ((mx + 1 - mc) % 2, (my + mc) % 2, mc)
    target = ((mx + mc) % 2, (my + 1 - mc) % 2, mc)
    return [((1 - mx, my, mc), sibling), ((mx, 1 - my, mc), sibling), (source, target)]


def _plan_pass_on_diagonal(mx, my, mc):
    return [((1 - mx, 1 - my, mc), (mx, my, 1 - mc))]


def _pair_exchange_start(g, name):
    r = g.shape[0] // N_DEV
    c = g.shape[1]
    land = (len(CHIPS), r, c)

    def body(g_ref, land_ref, send_sems, recv_sems, g_out, land_out):
        mx, my, mc = lax.axis_index("x"), lax.axis_index("y"), lax.axis_index("c")
        for j, (px, py) in enumerate(CHIPS):
            pltpu.make_async_remote_copy(
                src_ref=g_ref.at[pl.ds((4 * px + 2 * py + 1 - mc) * r, r), :], dst_ref=land_ref.at[j],
                send_sem=send_sems.at[j], recv_sem=recv_sems.at[j], device_id=(mx, my, 1 - mc), device_id_type=MESH).start()

    return pl.pallas_call(
        body, name=name, in_specs=[HBM, HBM], out_specs=(SEM, SEM, HBM, HBM),
        out_shape=(pltpu.SemaphoreType.DMA((4,)), pltpu.SemaphoreType.DMA((4,)), pltpu.HBM(g.shape, g.dtype), pltpu.HBM(land, g.dtype)),
        input_output_aliases={0: 2, 1: 3}, compiler_params=IN_FLIGHT,
    )(pltpu.with_memory_space_constraint(g, pltpu.HBM), pltpu.with_memory_space_constraint(lax.empty(land, g.dtype), pltpu.HBM))


def _pair_exchange_wait(send_sem, recv_sem, g, land, after, name):
    def body(g_ref, land_ref, send_ref, recv_ref, after_ref, g_out, land_out):
        mx, my, mc = lax.axis_index("x"), lax.axis_index("y"), lax.axis_index("c")
        for j in range(len(CHIPS)):
            cp = pltpu.make_async_remote_copy(
                src_ref=land_ref.at[0], dst_ref=land_ref.at[0], send_sem=send_ref.at[j], recv_sem=recv_ref.at[j],
                device_id=(mx, my, mc), device_id_type=MESH)
            cp.wait_send()
            cp.wait_recv()

    return pl.pallas_call(
        body, name=name, in_specs=[HBM, HBM, SEM, SEM, pl.BlockSpec(memory_space=pl.ANY)], out_specs=(HBM, HBM),
        out_shape=(pltpu.HBM(g.shape, g.dtype), pltpu.HBM(land.shape, land.dtype)), input_output_aliases={0: 0, 1: 1},
        compiler_params=IN_FLIGHT,
    )(g, land, send_sem, recv_sem, after)


def _chip_exchange_start(ps, after, name):
    def body(ps_ref, rx_ref, after_ref, send_sems, recv_sems, ps_out, rx_out):
        mx, my, mc = lax.axis_index("x"), lax.axis_index("y"), lax.axis_index("c")
        chips = [(1 - mx, my), (mx, 1 - my), (1 - mx, 1 - my)]
        for k, (px, py) in enumerate(chips):
            pltpu.make_async_remote_copy(
                src_ref=ps_ref.at[2 * px + py], dst_ref=rx_ref.at[2 * mx + my], send_sem=send_sems.at[k], recv_sem=recv_sems.at[k],
                device_id=(px, py, mc), device_id_type=MESH).start()

    return pl.pallas_call(
        body, name=name, in_specs=[HBM, HBM, pl.BlockSpec(memory_space=pl.ANY)], out_specs=(SEM, SEM, HBM, HBM),
        out_shape=(pltpu.SemaphoreType.DMA((3,)), pltpu.SemaphoreType.DMA((3,)), pltpu.HBM(ps.shape, ps.dtype), pltpu.HBM(ps.shape, ps.dtype)),
        input_output_aliases={0: 2, 1: 3}, compiler_params=IN_FLIGHT,
    )(pltpu.with_memory_space_constraint(ps, pltpu.HBM), pltpu.with_memory_space_constraint(lax.empty(ps.shape, ps.dtype), pltpu.HBM), after)


def _chip_exchange_wait(send_sem, recv_sem, ps, rx, after, name):
    def body(ps_ref, rx_ref, send_ref, recv_ref, after_ref, ps_out, rx_out):
        mx, my, mc = lax.axis_index("x"), lax.axis_index("y"), lax.axis_index("c")
        for k in range(3):
            cp = pltpu.make_async_remote_copy(
                src_ref=ps_ref.at[0], dst_ref=rx_ref.at[0], send_sem=send_ref.at[k], recv_sem=recv_ref.at[k],
                device_id=(mx, my, mc), device_id_type=MESH)
            cp.wait_send()
            cp.wait_recv()

    return pl.pallas_call(
        body, name=name, in_specs=[HBM, HBM, SEM, SEM, pl.BlockSpec(memory_space=pl.ANY)], out_specs=(HBM, HBM),
        out_shape=(pltpu.HBM(ps.shape, ps.dtype), pltpu.HBM(rx.shape, rx.dtype)), input_output_aliases={0: 0, 1: 1},
        compiler_params=IN_FLIGHT,
    )(ps, rx, send_sem, recv_sem, after)


def _sum_chips(ps, rx, name):
    n, r, c = rx.shape
    tr = _pick(r, max(16, 4 * ELEMENTWISE_BLOCK_BYTES // (4 * n * c)), 16)
    chip =(2 * lax.axis_index("x") + lax.axis_index("y")).astype(jnp.int32).reshape(1)

    def body(chip_ref, own_ref, x_ref, o_ref):
        me = chip_ref[0]
        own = own_ref[0].astype(F32)
        acc = jnp.where(me == 0, own, x_ref[0].astype(F32))
        for j in range(1, n):
            acc = acc + jnp.where(me == j, own, x_ref[j].astype(F32))
        o_ref[...] = acc

    return _pcall(
        body, name=name,
        grid_spec=pltpu.PrefetchScalarGridSpec(
            num_scalar_prefetch=1, grid=(r // tr,),
            in_specs=[pl.BlockSpec((1, tr, c), lambda i, chip_ref: (chip_ref[0], i, 0)), pl.BlockSpec((n, tr, c), lambda i, chip_ref: (0, i, 0))],
            out_specs=pl.BlockSpec((tr, c), lambda i, chip_ref: (i, 0))),
        out_shape=jax.ShapeDtypeStruct((r, c), F32), compiler_params=_params(("parallel",)),
    )(chip, ps, rx)


def _pair_exchange(g, name):
    r = g.shape[0] // N_DEV
    c = g.shape[1]

    def body(g_ref, theirs_ref, send_sems, recv_sems):
        mx, my, mc = lax.axis_index("x"), lax.axis_index("y"), lax.axis_index("c")
        sibling = (mx, my, 1 - mc)
        copies = []
        for j, (px, py) in enumerate(CHIPS):
            give = g_ref.at[pl.ds((4 * px + 2 * py + 1 - mc) * r, r), :]
            rc = pltpu.make_async_remote_copy(
                src_ref=give, dst_ref=theirs_ref.at[j], send_sem=send_sems.at[j], recv_sem=recv_sems.at[j],
                device_id=sibling, device_id_type=MESH)
            rc.start()
            copies.append(rc)
        for cp in copies:
            cp.wait()

    return _pcall(
        body, name=name, in_specs=[ANY], out_specs=ANY, out_shape=jax.ShapeDtypeStruct((len(CHIPS), r, c), g.dtype),
        scratch_shapes=[pltpu.SemaphoreType.DMA((4,)), pltpu.SemaphoreType.DMA((4,))],
    )(g)


def _pair_sum(g, theirs, name):
    nch, r, c = theirs.shape
    tr = _pick(r, max(16, 3 * ELEMENTWISE_BLOCK_BYTES // (2 * c)), 16)
    core = lax.axis_index("c").astype(jnp.int32).reshape(1)

    def body(core_ref, a_ref, b_ref, o_ref):
        o_ref[...] = (a_ref[...].astype(F32) + b_ref[...].astype(F32)).astype(o_ref.dtype)

    spec = pl.BlockSpec((1, tr, c), lambda j, i, core_ref: (j, i, 0))
    own = pl.BlockSpec((1, tr, c), lambda j, i, core_ref: (2 * j + core_ref[0], i, 0))
    return _pcall(
        body, name=name,
        grid_spec=pltpu.PrefetchScalarGridSpec(num_scalar_prefetch=1, grid=(nch, r // tr), in_specs=[own, spec], out_specs=spec),
        out_shape=jax.ShapeDtypeStruct(theirs.shape, theirs.dtype), compiler_params=_params(("parallel", "parallel")),
    )(core, g.reshape(N_DEV, r, c), theirs)


def _sum_blocks(rx, name):
    n, r, c = rx.shape
    tr = _pick(r, max(8, ELEMENTWISE_BLOCK_BYTES // (4 * n * c)), 8)

    def body(x_ref, o_ref):
        acc = x_ref[0].astype(F32)
        for j in range(1, n):
            acc = acc + x_ref[j].astype(F32)
        o_ref[...] = acc

    return _pcall(
        body, name=name, grid=(r // tr,), in_specs=[pl.BlockSpec((n, tr, c), lambda i: (0, i, 0))],
        out_specs=pl.BlockSpec((tr, c), lambda i: (i, 0)), out_shape=jax.ShapeDtypeStruct((r, c), F32),
        compiler_params=_params(("parallel",)),
    )(rx)


def _adamw(w, g, m, v, name):
    r, c = w.shape
    tr = _pick(r, max(8, ELEMENTWISE_BLOCK_BYTES // (4 * c)), 8)

    def body(w_ref, g_ref, m_ref, v_ref, d_ref, nm_ref, nv_ref):
        d_ref[...], nm_ref[...], nv_ref[...] = _adam_update(w_ref[...], g_ref[...], m_ref[...], v_ref[...])

    spec = pl.BlockSpec((tr, c), lambda i: (i, 0))
    shp = jax.ShapeDtypeStruct((r, c), F32)
    return _pcall(
        body, name=name, grid=(r // tr,), in_specs=[spec] * 4, out_specs=[spec] * 3, out_shape=[shp] * 3,
        compiler_params=_params(("parallel",)),
    )(w, g, m, v)


def _adam_update(w, g, m, v):
    nm = ADAM_B1 * m + (1.0 - ADAM_B1) * g
    nv = ADAM_B2 * v + (1.0 - ADAM_B2) * (g * g)
    delta = -ADAM_LR * ((nm / (1.0 - ADAM_B1 ** ADAM_STEP)) / (jnp.sqrt(nv / (1.0 - ADAM_B2 ** ADAM_STEP)) + ADAM_EPS) + ADAM_WD * w)
    return delta, nm, nv


def _sum_chips_adamw(ps, rx, w, m, v, name):
    n, r, c = rx.shape
    tr = _pick(r, max(16, 4 * ELEMENTWISE_BLOCK_BYTES // (4 * n * c)), 16)
    chip = (2 * lax.axis_index("x") + lax.axis_index("y")).astype(jnp.int32).reshape(1)

    def body(chip_ref, own_ref, x_ref, w_ref, m_ref, v_ref, g_ref, d_ref, nm_ref, nv_ref):
        me = chip_ref[0]
        own = own_ref[0].astype(F32)
        g = jnp.where(me == 0, own, x_ref[0].astype(F32))
        for j in range(1, n):
            g = g + jnp.where(me == j, own, x_ref[j].astype(F32))
        g_ref[...] = g
        d_ref[...], nm_ref[...], nv_ref[...] = _adam_update(w_ref[...], g, m_ref[...], v_ref[...])

    rows = pl.BlockSpec((tr, c), lambda i, chip_ref: (i, 0))
    shp = jax.ShapeDtypeStruct((r, c), F32)
    return _pcall(
        body, name=name,
        grid_spec=pltpu.PrefetchScalarGridSpec(
            num_scalar_prefetch=1, grid=(r // tr,),
            in_specs=[pl.BlockSpec((1, tr, c), lambda i, chip_ref: (chip_ref[0], i, 0)),
                      pl.BlockSpec((n, tr, c), lambda i, chip_ref: (0, i, 0)), rows, rows, rows],
            out_specs=[rows] * 4),
        out_shape=[shp] * 4, compiler_params=_params(("parallel",)),
    )(chip, ps, rx, w, m, v)


def _pack(parts):
    flat, layout, row = [], [], 0
    for p in parts:
        n = p.size
        rows = -(-n // LANES)
        flat.append(jnp.pad(p.reshape(-1).astype(F32), (0, rows * LANES - n)))
        layout.append((row, n, p.shape))
        row += rows
    total = -(-row // 8) * 8
    if total > row:
        flat.append(jnp.zeros(((total - row) * LANES,), F32))
    return jnp.concatenate(flat).reshape(total, LANES), layout


def _unpack(slab, layout):
    flat = slab.reshape(-1)
    return [flat[row * LANES:row * LANES + n].reshape(shape) for row, n, shape in layout]


def kernel(x, meta_tokens, mix_norm_g, w_in, b_in, attn_sinks, conv_w, conv_b, conv_ln_g, conv_ln_b, w_attn_o, w_conv_o, b_conv_o, w_out, ffn_norm_g, w_gate_up, w_down, final_norm_g, loss_target, m_meta_tokens, m_mix_norm_g, m_w_in, m_b_in, m_attn_sinks, m_conv_w, m_conv_b, m_conv_ln_g, m_conv_ln_b, m_w_attn_o, m_w_conv_o, m_b_conv_o, m_w_out, m_ffn_norm_g, m_w_gate_up, m_w_down, m_final_norm_g, v_meta_tokens, v_mix_norm_g, v_w_in, v_b_in, v_attn_sinks, v_conv_w, v_conv_b, v_conv_ln_g, v_conv_ln_b, v_w_attn_o, v_w_conv_o, v_b_conv_o, v_w_out, v_ffn_norm_g, v_w_gate_up, v_w_down, v_final_norm_g):
    xs = x[0]
    tgt = loss_target[0]
    s, d = xs.shape
    lp = s + BLOCK
    cd = conv_b.shape[1]
    ffn = w_down.shape[1] * N_DEV
    dev = 4 * lax.axis_index("x") + 2 * lax.axis_index("y") + lax.axis_index("c")
    cw_cols = conv_w.shape[3]
    meta_cols = meta_tokens.shape[1]

    small, small_layout = _pack([meta_tokens, jnp.pad(conv_w[0, :, 0, :], ((0, CONV_ROWS - CONV_WIDTH), (0, 0)))])
    small_flat = _all_gather_rows(small, "gather_small")
    small_all = small_flat.reshape(N_DEV, *small.shape)
    meta_parts, cw_parts = zip(*[_unpack(small_all[j], small_layout) for j in range(N_DEV)])
    meta_full = jnp.concatenate(meta_parts, axis=1)
    conv_w_full = jnp.concatenate(cw_parts, axis=1)
    shards = ((w_in[0].T, "w_in"), (w_attn_o[0].T, "w_attn_o"), (w_conv_o[0].T, "w_conv_o"), (w_out[0], "w_out"),
              (w_gate_up[0].T, "w_gate_up"), (w_down[0], "w_down"))
    first = _rows_start(_place_rows(shards[0][0], small_flat, "place_w_in"), _plan_neighbours, "gather_start_w_in")
    placed, tok = [], first[2]
    for shard, name in shards[1:]:
        tok = _place_rows(shard, tok, "place_" + name)
        placed.append(tok)
    relay = _rows_start(_rows_wait(first, tok, "gather_wait_w_in"), _plan_relay, "gather_relay_start_w_in")
    h0, u = _prep(xs, meta_full, mix_norm_g, after=relay[2])
    diagonal = _rows_start(_rows_wait(relay, u, "gather_relay_wait_w_in"), _plan_pass_on_diagonal, "gather_diagonal_start_w_in")
    started, tok = [None], diagonal[2]
    for full, (_, name) in zip(placed, shards[1:]):
        started.append(_rows_start(full, _plan_direct, "gather_start_" + name, after=tok))
        tok = started[-1][2]
    win_t = _rows_wait(diagonal, tok, "gather_diagonal_wait_w_in")

    def arrived(w, after, name):
        return _rows_start(_rows_wait(started[w], after, "gather_wait_" + name), _plan_pass_on, "gather_pass_on_start_" + name)

    def whole(passing, after, name):
        return _rows_wait(passing, after, "gather_pass_on_wait_" + name)

    ctab, stab = _rope_tables(lp)
    mm = functools.partial(_matmul, tm=1056, tn=1024)

    bq, bkv, bc, bg = b_in[:, :Q_DIM], b_in[:, Q_DIM:Q_DIM + 2 * KV_DIM], b_in[:, Q_DIM + 2 * KV_DIM:Q_DIM + 2 * KV_DIM + 2 * cd], b_in[:, Q_DIM + 2 * KV_DIM + 2 * cd:]
    o_kv, o_c, o_g = Q_DIM, Q_DIM + 2 * KV_DIM, Q_DIM + 2 * KV_DIM + 2 * cd
    in_proj = functools.partial(_matmul, u, win_t, mode="nt", out_dtype=BF16, tm=2112, tn=512, tk=d)
    zq = in_proj(name="in_proj_q", bias=bq, b_row_off=0, b_rows=Q_DIM)
    zkv = in_proj(name="in_proj_kv", bias=bkv, b_row_off=o_kv, b_rows=2 * KV_DIM)
    zc = in_proj(name="in_proj_conv", bias=bc, b_row_off=o_c, b_rows=2 * cd)
    zg = in_proj(name="in_proj_gates", bias=bg, b_row_off=o_g, b_rows=2 * d)
    passing = arrived(1, zg, "w_attn_o")
    q_rot, k_sh, v_sh = _rope_fwd(zq, zkv, ctab, stab, after=passing[2])
    o = _attn_fwd(q_rot, k_sh, v_sh, attn_sinks)
    wao_t = whole(passing, o, "w_attn_o")
    br_a = mm(o, wao_t, mode="nt", name="attn_out_proj", out_dtype=BF16, tk=Q_DIM)
    passing = arrived(2, br_a, "w_conv_o")
    passing_out = arrived(3, passing[2], "w_out")
    conv_out, c2 = _conv_fwd(zc, conv_w_full, conv_b, conv_ln_g, conv_ln_b, after=passing_out[2])
    wco_t = whole(passing, c2, "w_conv_o")
    br_b, merged = _conv_out_gate(c2, wco_t, b_conv_o, br_a, zg, tm=1056, tn=1024)
    wout = whole(passing_out, merged, "w_out")
    passing = arrived(4, wout, "w_gate_up")
    h1 = mm(merged, wout, mode="nn", name="mix_out_proj", out_dtype=F32, tn=512, tk=d, residual=h0, after=passing[2])
    u2 = _rmsnorm_fwd(h1, ffn_norm_g, "ffn_rmsnorm")
    wgu_t = whole(passing, u2, "w_gate_up")
    half_rows = _pick(lp, lp // 2, 16)
    ffn_in = _gate_up_swiglu(u2, wgu_t, row_block=0, tm=half_rows, tn=256, name="ffn_gate_up_swiglu_0")
    passing = arrived(5, ffn_in[2], "w_down")
    for rb in range(1, lp // half_rows):
        ffn_in = _gate_up_swiglu(u2, wgu_t, row_block=rb, tm=half_rows, tn=256, name="ffn_gate_up_swiglu_%d" % rb,
                                 filled=ffn_in, after=passing[2])
    gu_g, gu_u, act = ffn_in
    wdown = whole(passing, act, "w_down")
    h2 = mm(act, wdown, mode="nn", name="ffn_down", out_dtype=F32, tn=512, tk=ffn // 2, residual=h1)
    dh2, dh2_b, loss_part, d_final_g = _final(h2, tgt, final_norm_g.reshape(1, d))

    wgrad = functools.partial(_matmul, mode="tn", out_dtype=BF16, tk=lp, tn=2048, b_inner=False)
    in_flight = {}

    def scatter_begin(g, name):
        return _pair_exchange_start(g, "rs_" + name + "_pair_start")

    def scatter_go_on(pair, after, name):
        g, theirs = _pair_exchange_wait(pair[0], pair[1], pair[2], pair[3], after, "rs_" + name + "_pair_wait")
        ps = _pair_sum(g, theirs, "rs_" + name + "_pair_sum")
        in_flight[name] = _chip_exchange_start(ps, theirs, "rs_" + name + "_chip_start")
        return in_flight[name][2]

    g_wdown = wgrad(act, dh2_b, name="ffn_down_dw", tm=256)
    pair = scatter_begin(g_wdown, "w_down")
    dgu = _down_dx_swiglu_bwd(dh2_b, wdown, gu_g, gu_u, tm=1056, tn=256, after=pair[2])
    tok = scatter_go_on(pair, dgu, "w_down")
    g_wgu_t = wgrad(dgu, u2, name="ffn_gate_up_dw", tm=256, after=tok)
    pair = scatter_begin(g_wgu_t, "w_gate_up")
    du2 = mm(dgu, wgu_t, mode="nn", name="ffn_gate_up_dx", out_dtype=F32, tn=512, tk=ffn // 2, after=pair[2])
    tok = scatter_go_on(pair, du2, "w_gate_up")
    dh1, dh1_b, d_ffn_g = _rmsnorm_bwd(du2, h1, ffn_norm_g, dh2, "ffn_rmsnorm_bwd")
    g_wout = wgrad(merged, dh1_b, name="mix_out_dw", tm=512, after=tok)
    pair = scatter_begin(g_wout, "w_out")
    dmerged = mm(dh1_b, wout, mode="nt", name="mix_out_dx", out_dtype=BF16, tk=d, after=pair[2])
    tok = scatter_go_on(pair, dmerged, "w_out")
    d_a, d_b, dz_g, sum_g, d_bco = _gate_bwd(dmerged, br_a, br_b, zg)
    g_wao_t = wgrad(d_a, o, name="attn_out_dw", tm=512, after=tok)
    pair = scatter_begin(g_wao_t, "w_attn_o")
    do = mm(d_a, wao_t, mode="nn", name="attn_out_dx", out_dtype=BF16, tk=d, after=pair[2])
    tok = scatter_go_on(pair, do, "w_attn_o")
    g_wco_t = wgrad(d_b, c2, name="conv_out_dw", tm=512, after=tok)
    pair = scatter_begin(g_wco_t, "w_conv_o")
    dc2 = mm(d_b, wco_t, mode="nn", name="conv_out_dx", out_dtype=F32, tk=d, after=pair[2])
    tok = scatter_go_on(pair, dc2, "w_conv_o")
    dq, dk, dv, dkm, dvm, d_sinks = _attn_bwd(q_rot, k_sh, v_sh, attn_sinks, do)
    dz_qkv, sum_qkv = _rope_bwd(dq, dk, dv, dkm, dvm, ctab, stab)
    dco, d_ln_g, d_ln_b, d_conv_b = _conv_bwd_norm(dc2, conv_out, conv_ln_g, conv_ln_b)
    dz_c, sum_c, d_conv_w = _conv_bwd_taps(dco, zc, conv_w_full)
    dz = jnp.concatenate([dz_qkv, dz_c, dz_g], axis=1)
    d_b_in = jnp.concatenate([sum_qkv, sum_c, sum_g], axis=1)
    in_dim = dz.shape[1]
    g_win_t = wgrad(dz, u, name="in_proj_dw", tm=512, after=tok)
    theirs = _pair_exchange(g_win_t, "rs_w_in_pair_exchange")
    in_flight["w_in"] = _chip_exchange_start(_pair_sum(g_win_t, theirs, "rs_w_in_pair_sum"), theirs, "rs_w_in_chip_start")
    du = mm(dz, win_t, mode="nn", name="in_proj_dx", out_dtype=F32, tk=in_dim // 4, after=in_flight["w_in"][2])
    grad_x, d_meta, d_mix_g = _rmsnorm_bwd_first(du, h0, mix_norm_g, dh1)

    weights = dict(meta_tokens=meta_tokens, mix_norm_g=mix_norm_g, w_in=w_in, b_in=b_in, attn_sinks=attn_sinks, conv_w=conv_w,
                   conv_b=conv_b, conv_ln_g=conv_ln_g, conv_ln_b=conv_ln_b, w_attn_o=w_attn_o, w_conv_o=w_conv_o, b_conv_o=b_conv_o,
                   w_out=w_out, ffn_norm_g=ffn_norm_g, w_gate_up=w_gate_up, w_down=w_down, final_norm_g=final_norm_g)
    m_in = dict(meta_tokens=m_meta_tokens, mix_norm_g=m_mix_norm_g, w_in=m_w_in, b_in=m_b_in, attn_sinks=m_attn_sinks, conv_w=m_conv_w,
                conv_b=m_conv_b, conv_ln_g=m_conv_ln_g, conv_ln_b=m_conv_ln_b, w_attn_o=m_w_attn_o, w_conv_o=m_w_conv_o,
                b_conv_o=m_b_conv_o, w_out=m_w_out, ffn_norm_g=m_ffn_norm_g, w_gate_up=m_w_gate_up, w_down=m_w_down,
                final_norm_g=m_final_norm_g)
    v_in = dict(meta_tokens=v_meta_tokens, mix_norm_g=v_mix_norm_g, w_in=v_w_in, b_in=v_b_in, attn_sinks=v_attn_sinks, conv_w=v_conv_w,
                conv_b=v_conv_b, conv_ln_g=v_conv_ln_g, conv_ln_b=v_conv_ln_b, w_attn_o=v_w_attn_o, w_conv_o=v_w_conv_o,
                b_conv_o=v_b_conv_o, w_out=v_w_out, ffn_norm_g=v_ffn_norm_g, w_gate_up=v_w_gate_up, w_down=v_w_down,
                final_norm_g=v_final_norm_g)
    names = list(weights)
    grads, delta, new_m, new_v = {}, {}, {}, {}
    tok = grad_x
    for n in ("w_down", "w_gate_up", "w_out", "w_attn_o", "w_conv_o", "w_in"):
        send_sem, recv_sem, ps, rx = in_flight[n]
        ps, rx = _chip_exchange_wait(send_sem, recv_sem, ps, rx, tok, "rs_" + n + "_chip_wait")
        oriented = (lambda a: a[0].T) if n in ("w_in", "w_gate_up") else (lambda a: a[0])
        back = (lambda a: a.T[None]) if n in ("w_in", "w_gate_up") else (lambda a: a[None])
        if n in ("w_attn_o", "w_conv_o"):
            g = _sum_chips(ps, rx, "rs_" + n + "_sum").T
            dl, nm, nv = _adamw(oriented(weights[n]), g, oriented(m_in[n]), oriented(v_in[n]), "adamw_" + n)
        else:
            g, dl, nm, nv = _sum_chips_adamw(ps, rx, oriented(weights[n]), oriented(m_in[n]), oriented(v_in[n]), "rs_" + n + "_sum_adamw")
        grads[n], delta[n], new_m[n], new_v[n] = back(g), back(dl), back(nm), back(nv)
        tok = dl

    slab, slab_layout = _pack([loss_part[:, :1], d_mix_g, d_b_in, d_sinks[:, :N_Q_HEADS], d_conv_b, d_ln_g, d_ln_b, d_bco,
                               d_ffn_g, d_final_g, d_conv_w, d_meta])
    slab_all = _all_gather_rows(slab, "gather_small_grads", after=tok).reshape(N_DEV, *slab.shape)
    (loss, g_mix_g, g_b_in, g_sinks, g_conv_b, g_ln_g, g_ln_b, g_bco, g_ffn_g, g_final_g, g_conv_w_full, g_meta_full
     ) = _unpack(_sum_blocks(slab_all, "sum_small_grads"), slab_layout)
    g_conv_w = lax.dynamic_slice(g_conv_w_full, (0, dev * cw_cols), (CONV_WIDTH, cw_cols)).reshape(conv_w.shape)
    g_meta = lax.dynamic_slice(g_meta_full, (0, dev * meta_cols), (N_META, meta_cols))
    g_final_g = g_final_g.reshape(final_norm_g.shape)
    grads.update(meta_tokens=g_meta, mix_norm_g=g_mix_g, b_in=g_b_in, attn_sinks=g_sinks, conv_w=g_conv_w, conv_b=g_conv_b,
                 conv_ln_g=g_ln_g, conv_ln_b=g_ln_b, b_conv_o=g_bco, ffn_norm_g=g_ffn_g, final_norm_g=g_final_g)
    rest = [n for n in names if n not in delta]
    w_slab, rest_layout = _pack([weights[n] for n in rest])
    g_slab, _ = _pack([grads[n] for n in rest])
    m_slab, _ = _pack([m_in[n] for n in rest])
    v_slab, _ = _pack([v_in[n] for n in rest])
    dl, nm, nv = _adamw(w_slab, g_slab, m_slab, v_slab, "adamw_small")
    for n, a, b, c in zip(rest, _unpack(dl, rest_layout), _unpack(nm, rest_layout), _unpack(nv, rest_layout)):
        delta[n], new_m[n], new_v[n] = a, b, c

    return (loss.reshape(()), grad_x[None], *[grads[n] for n in names], *[delta[n] for n in names],
            *[new_m[n] for n in names], *[new_v[n] for n in names])
```

```python
import functools
import math

import jax
import jax.numpy as jnp
from jax import lax
from jax.experimental import pallas as pl
from jax.experimental.pallas import tpu as pltpu

F32 = jnp.float32
BF16 = jnp.bfloat16

N_DEV = 8
BLOCK = 128
N_META = 16
PAD_ROWS = BLOCK - N_META
HEAD_DIM = 64
N_Q_HEADS = 32
N_KV_HEADS = 4
GROUP = N_Q_HEADS // N_KV_HEADS
Q_DIM = N_Q_HEADS * HEAD_DIM
KV_DIM = N_KV_HEADS * HEAD_DIM
WINDOW = 128
CONV_WIDTH = 31
CONV_ROWS = 32
ROPE_THETA = 10000.0
EPS = 1e-6
ATTN_SCALE = HEAD_DIM ** -0.5
NEG = -1e30

ADAM_LR = 0.001
ADAM_B1 = 0.9
ADAM_B2 = 0.999
ADAM_EPS = 1e-08
ADAM_WD = 0.01
ADAM_STEP = 10

VMEM_LIMIT_BYTES = 56 * 1024 * 1024
LANES = 128
ELEMENTWISE_BLOCK_BYTES = 2 * 1024 * 1024
MESH = pl.DeviceIdType.MESH
CHIPS = ((0, 0), (0, 1), (1, 0), (1, 1))


def _pcall(body, after=None, **kw):
    if after is None:
        return pl.pallas_call(body, **kw)
    in_specs = list(kw.pop("in_specs"))
    n_in = len(in_specs)

    def ordered_body(*refs):
        return body(*refs[:n_in], *refs[n_in + 1:])

    call = pl.pallas_call(ordered_body, in_specs=in_specs + [pl.BlockSpec(memory_space=pl.ANY)], **kw)
    return lambda *args: call(*args, after)


def _params(semantics=None):
    if semantics is None:
        return pltpu.CompilerParams(vmem_limit_bytes=VMEM_LIMIT_BYTES)
    return pltpu.CompilerParams(dimension_semantics=semantics, vmem_limit_bytes=VMEM_LIMIT_BYTES)


def _pick(dim, pref, align):
    best = None
    t = align
    while t <= min(dim, pref):
        if dim % t == 0:
            best = t
        t += align
    return dim if best is None else best


def _sigmoid(x):
    return 1.0 / (1.0 + jnp.exp(-x))


def _matmul(a, b, *, mode, name, out_dtype, tm, tn, tk, bias=None, residual=None, b_inner=True,
            b_row_off=0, b_rows=None, after=None):
    halves = a.ndim == 3
    a_rows, a_cols = (a.shape[1], 2 * a.shape[2]) if halves else a.shape
    if mode == "nn":
        m, k = a_rows, a_cols
        n = b.shape[1]
    elif mode == "nt":
        m, k = a_rows, a_cols
        n = b.shape[0] if b_rows is None else b_rows
    else:
        k, m = a_rows, a_cols
        n = b.shape[1]
    tm = _pick(m // 2 if halves and mode == "tn" else m, tm, 16)
    tn = _pick(math.gcd(n, b_row_off) if mode == "nt" and b_row_off else n, tn, LANES)
    tk = _pick(k // 2 if halves and mode == "nn" else k, tk, LANES if mode != "tn" else 16)
    nm, nn, nk = m // tm, n // tn, k // tk
    if mode == "nt":
        assert b_row_off % tn == 0
    off = b_row_off // tn if mode == "nt" else 0

    if b_inner:
        grid = (nm, nn, nk)
        ij = lambda g0, g1: (g0, g1)
    else:
        grid = (nn, nm, nk)
        ij = lambda g0, g1: (g1, g0)

    if halves and mode == "tn":
        a_spec = pl.BlockSpec((None, tk, tm), lambda g0, g1, kk: (ij(g0, g1)[0] // (nm // 2), kk, ij(g0, g1)[0] % (nm // 2)))
    elif halves:
        assert mode == "nn"
        a_spec = pl.BlockSpec((None, tm, tk), lambda g0, g1, kk: (kk // (nk // 2), ij(g0, g1)[0], kk % (nk // 2)))
    elif mode == "tn":
        a_spec = pl.BlockSpec((tk, tm), lambda g0, g1, kk: (kk, ij(g0, g1)[0]))
    else:
        a_spec = pl.BlockSpec((tm, tk), lambda g0, g1, kk: (ij(g0, g1)[0], kk))
    if mode == "nt":
        b_spec = pl.BlockSpec((tn, tk), lambda g0, g1, kk: (ij(g0, g1)[1] + off, kk))
    else:
        b_spec = pl.BlockSpec((tk, tn), lambda g0, g1, kk: (kk, ij(g0, g1)[1]))
    o_spec = pl.BlockSpec((tm, tn), lambda g0, g1, kk: ij(g0, g1))
    in_specs = [a_spec, b_spec]
    args = [a, b]
    if bias is not None:
        in_specs.append(pl.BlockSpec((1, tn), lambda g0, g1, kk: (0, ij(g0, g1)[1])))
        args.append(bias)
    if residual is not None:
        in_specs.append(o_spec)
        args.append(residual)
    dims = {"nn": (((1,), (0,)), ((), ())), "nt": (((1,), (1,)), ((), ())), "tn": (((0,), (0,)), ((), ()))}[mode]
    has_bias, has_res = bias is not None, residual is not None

    def body(*refs):
        a_ref, b_ref = refs[0], refs[1]
        pos = 2
        bias_ref = res_ref = None
        if has_bias:
            bias_ref = refs[pos]
            pos += 1
        if has_res:
            res_ref = refs[pos]
            pos += 1
        o_ref = refs[pos]
        acc_ref = refs[pos + 1] if nk > 1 else None

        def finish(acc):
            if has_bias:
                acc = acc + bias_ref[...]
            if has_res:
                acc = acc + res_ref[...]
            o_ref[...] = acc.astype(out_dtype)

        p = lax.dot_general(a_ref[...], b_ref[...], dims, preferred_element_type=F32)
        if nk == 1:
            finish(p)
        else:
            kk = pl.program_id(2)

            @pl.when(kk == 0)
            def _():
                acc_ref[...] = p

            @pl.when(kk > 0)
            def _():
                acc_ref[...] += p

            @pl.when(kk == nk - 1)
            def _():
                finish(acc_ref[...])

    return _pcall(
        body, after=after, name=name, grid=grid, in_specs=in_specs, out_specs=o_spec,
        out_shape=jax.ShapeDtypeStruct((m, n), out_dtype),
        scratch_shapes=[pltpu.VMEM((tm, tn), F32)] if nk > 1 else [],
        compiler_params=_params(("parallel", "parallel", "arbitrary")),
    )(*args)


def _row_spec(width, col=0):
    return pl.BlockSpec((BLOCK, width), lambda i: (i, col))


def _const_spec(shape):
    nd = len(shape)
    return pl.BlockSpec(shape, lambda i: (0,) * nd)


def _prep(x, meta_full, g, after=None):
    s, d = x.shape
    lp = s + BLOCK
    nb = lp // BLOCK

    def body(x_ref, meta_ref, g_ref, h_ref, u_ref):
        i = pl.program_id(0)

        @pl.when(i == 0)
        def _():
            h_ref[0:PAD_ROWS, :] = jnp.zeros((PAD_ROWS, d), F32)
            h_ref[PAD_ROWS:BLOCK, :] = meta_ref[...]

        @pl.when(i > 0)
        def _():
            h_ref[...] = x_ref[...]

        h = h_ref[...]
        r = lax.rsqrt(jnp.mean(h * h, axis=-1, keepdims=True) + EPS)
        u_ref[...] = (h * r * g_ref[...]).astype(BF16)

    return _pcall(
        body, after=after, name="prep_rmsnorm", grid=(nb,),
        in_specs=[pl.BlockSpec((BLOCK, d), lambda i: (jnp.maximum(i - 1, 0), 0)), _const_spec((N_META, d)), _const_spec((1, d))],
        out_specs=[_row_spec(d), _row_spec(d)],
        out_shape=[jax.ShapeDtypeStruct((lp, d), F32), jax.ShapeDtypeStruct((lp, d), BF16)],
        compiler_params=_params(("arbitrary",)),
    )(x, meta_full, g)


def _rmsnorm_fwd(h, g, name):
    lp, d = h.shape

    def body(h_ref, g_ref, u_ref):
        x = h_ref[...]
        r = lax.rsqrt(jnp.mean(x * x, axis=-1, keepdims=True) + EPS)
        u_ref[...] = (x * r * g_ref[...]).astype(BF16)

    return _pcall(
        body, name=name, grid=(lp // BLOCK,), in_specs=[_row_spec(d), _const_spec((1, d))], out_specs=_row_spec(d),
        out_shape=jax.ShapeDtypeStruct((lp, d), BF16), compiler_params=_params(("parallel",)),
    )(h, g)


def _rms_bwd_core(dy, x, g):
    r = lax.rsqrt(jnp.mean(x * x, axis=-1, keepdims=True) + EPS)
    xhat = x * r
    dxhat = dy * g
    dx = r * (dxhat - xhat * jnp.mean(dxhat * xhat, axis=-1, keepdims=True))
    return dx, jnp.sum(dy * xhat, axis=0, keepdims=True)


def _rmsnorm_bwd(dy, h, g, dres, name):
    lp, d = h.shape

    def body(dy_ref, h_ref, g_ref, dres_ref, dh_ref, dhb_ref, dg_ref):
        i = pl.program_id(0)
        dx, dg = _rms_bwd_core(dy_ref[...], h_ref[...], g_ref[...])
        dh = dres_ref[...] + dx
        dh_ref[...] = dh
        dhb_ref[...] = dh.astype(BF16)

        @pl.when(i == 0)
        def _():
            dg_ref[...] = jnp.zeros_like(dg_ref)

        dg_ref[...] += dg

    return _pcall(
        body, name=name, grid=(lp // BLOCK,),
        in_specs=[_row_spec(d), _row_spec(d), _const_spec((1, d)), _row_spec(d)],
        out_specs=[_row_spec(d), _row_spec(d), _const_spec((1, d))],
        out_shape=[jax.ShapeDtypeStruct((lp, d), F32), jax.ShapeDtypeStruct((lp, d), BF16), jax.ShapeDtypeStruct((1, d), F32)],
        compiler_params=_params(("arbitrary",)),
    )(dy, h, g, dres)


def _rmsnorm_bwd_first(dy, h, g, dres):
    lp, d = h.shape
    s = lp - BLOCK

    def body(dy_ref, h_ref, g_ref, dres_ref, gx_ref, dmeta_ref, dg_ref):
        i = pl.program_id(0)
        dx, dg = _rms_bwd_core(dy_ref[...], h_ref[...], g_ref[...])
        dh = dres_ref[...] + dx
        gx_ref[...] = dh

        @pl.when(i == 0)
        def _():
            dmeta_ref[...] = dh[PAD_ROWS:BLOCK, :]
            dg_ref[...] = jnp.zeros_like(dg_ref)

        dg_ref[...] += dg

    return _pcall(
        body, name="rmsnorm_bwd_first", grid=(lp // BLOCK,),
        in_specs=[_row_spec(d), _row_spec(d), _const_spec((1, d)), _row_spec(d)],
        out_specs=[pl.BlockSpec((BLOCK, d), lambda i: (jnp.maximum(i - 1, 0), 0)), _const_spec((N_META, d)), _const_spec((1, d))],
        out_shape=[jax.ShapeDtypeStruct((s, d), F32), jax.ShapeDtypeStruct((N_META, d), F32), jax.ShapeDtypeStruct((1, d), F32)],
        compiler_params=_params(("arbitrary",)),
    )(dy, h, g, dres)


def _final(h2, tgt, g):
    lp, d = h2.shape

    def body(h_ref, t_ref, g_ref, dh_ref, dhb_ref, loss_ref, dg_ref):
        i = pl.program_id(0)
        x = h_ref[...]
        gg = g_ref[...]
        r = lax.rsqrt(jnp.mean(x * x, axis=-1, keepdims=True) + EPS)
        xhat = x * r
        y = xhat * gg
        live = (i > 0).astype(F32)
        err = (y - t_ref[...]) * live
        dy = err * (1.0 / d)
        dxhat = dy * gg
        dh = r * (dxhat - xhat * jnp.mean(dxhat * xhat, axis=-1, keepdims=True))
        dh_ref[...] = dh
        dhb_ref[...] = dh.astype(BF16)

        @pl.when(i == 0)
        def _():
            loss_ref[...] = jnp.zeros_like(loss_ref)
            dg_ref[...] = jnp.zeros_like(dg_ref)

        row_loss = jnp.mean(err * err, axis=-1, keepdims=True)
        loss_ref[...] += 0.5 * jnp.sum(row_loss, axis=0, keepdims=True)
        dg_ref[...] += jnp.sum(dy * xhat, axis=0, keepdims=True)

    return _pcall(
        body, name="final_norm_loss", grid=(lp // BLOCK,),
        in_specs=[_row_spec(d), pl.BlockSpec((BLOCK, d), lambda i: (jnp.maximum(i - 1, 0), 0)), _const_spec((1, d))],
        out_specs=[_row_spec(d), _row_spec(d), _const_spec((1, LANES)), _const_spec((1, d))],
        out_shape=[jax.ShapeDtypeStruct((lp, d), F32), jax.ShapeDtypeStruct((lp, d), BF16),
                   jax.ShapeDtypeStruct((1, LANES), F32), jax.ShapeDtypeStruct((1, d), F32)],
        compiler_params=_params(("arbitrary",)),
    )(h2, tgt, g)


def _swap_halves(x):
    w = x.shape[1]
    lane = lax.broadcasted_iota(jnp.int32, x.shape, 1)
    first = (lane & (HEAD_DIM - 1)) < (HEAD_DIM // 2)
    return jnp.where(first, pltpu.roll(x, w - HEAD_DIM // 2, 1), pltpu.roll(x, HEAD_DIM // 2, 1))


def _rope_tables(lp):
    pos = jnp.maximum(jnp.arange(lp, dtype=jnp.int32) - PAD_ROWS, 0).astype(F32)
    inv_freq = ROPE_THETA ** (-jnp.arange(0, HEAD_DIM, 2, dtype=F32) / HEAD_DIM)
    ang = pos[:, None] * inv_freq[None, :]
    c, s = jnp.cos(ang), jnp.sin(ang)
    reps = LANES // HEAD_DIM
    return jnp.tile(jnp.concatenate([c, c], axis=1), (1, reps)), jnp.tile(jnp.concatenate([-s, s], axis=1), (1, reps))


def _rope_fwd(zq, zkv, ctab, stab, after=None):
    lp = zq.shape[0]
    nb = lp // BLOCK
    back = lambda s: (jnp.maximum(s - 1, 0), 0)

    def body(zq_ref, zkv_ref, c_ref, s_ref, q_ref, k_ref, v_ref):
        step = pl.program_id(0)
        c128, s128 = c_ref[...], s_ref[...]

        def rope(x):
            reps = x.shape[1] // LANES
            return x * jnp.tile(c128, (1, reps)) + _swap_halves(x) * jnp.tile(s128, (1, reps))

        q_ref[...] = (rope(zq_ref[...].astype(F32)) * ATTN_SCALE).astype(BF16)
        kv = zkv_ref[...].astype(F32)
        k = rope(kv[:, :KV_DIM])
        v = kv[:, KV_DIM:]

        @pl.when(step == 0)
        def _():
            k_ref[...] = jnp.zeros_like(k_ref)
            v_ref[...] = jnp.zeros_like(v_ref)

        @pl.when(step > 0)
        def _():
            for h in range(N_KV_HEADS):
                k_ref[h] = k[:, h * HEAD_DIM:(h + 1) * HEAD_DIM].astype(BF16)
                v_ref[h] = v[:, h * HEAD_DIM:(h + 1) * HEAD_DIM].astype(BF16)

    kv_spec = pl.BlockSpec((N_KV_HEADS, BLOCK, HEAD_DIM), lambda s: (0, s, 0))
    return _pcall(
        body, after=after, name="rope_fwd", grid=(nb + 1,),
        in_specs=[pl.BlockSpec((BLOCK, Q_DIM), back), pl.BlockSpec((BLOCK, 2 * KV_DIM), back),
                  pl.BlockSpec((BLOCK, LANES), back), pl.BlockSpec((BLOCK, LANES), back)],
        out_specs=[pl.BlockSpec((BLOCK, Q_DIM), back), kv_spec, kv_spec],
        out_shape=[jax.ShapeDtypeStruct((lp, Q_DIM), BF16),
                   jax.ShapeDtypeStruct((N_KV_HEADS, lp + BLOCK, HEAD_DIM), BF16),
                   jax.ShapeDtypeStruct((N_KV_HEADS, lp + BLOCK, HEAD_DIM), BF16)],
        compiler_params=_params(("arbitrary",)),
    )(zq, zkv, ctab, stab)


def _rope_bwd(dq, dk, dv, dkm, dvm, ctab, stab):
    lp = dq.shape[0]
    width = Q_DIM + 2 * KV_DIM
    head_spec = pl.BlockSpec((N_KV_HEADS, BLOCK, HEAD_DIM), lambda i: (0, i, 0))
    meta_spec = _const_spec((N_KV_HEADS, BLOCK, HEAD_DIM))

    def body(dq_ref, dk_ref, dv_ref, dkm_ref, dvm_ref, c_ref, s_ref, dz_ref, sum_ref, kbuf, vbuf):
        i = pl.program_id(0)
        c128, s128 = c_ref[...], s_ref[...]
        first = (i == 0).astype(F32)

        def rope_t(x):
            reps = x.shape[1] // LANES
            return x * jnp.tile(c128, (1, reps)) + _swap_halves(x * jnp.tile(s128, (1, reps)))

        for h in range(N_KV_HEADS):
            kbuf[:, h * HEAD_DIM:(h + 1) * HEAD_DIM] = dk_ref[h] + first * dkm_ref[h]
            vbuf[:, h * HEAD_DIM:(h + 1) * HEAD_DIM] = dv_ref[h] + first * dvm_ref[h]
        dzq = rope_t(dq_ref[...] * ATTN_SCALE)
        dzk = rope_t(kbuf[...])
        dzv = vbuf[...]
        dz_ref[:, 0:Q_DIM] = dzq.astype(BF16)
        dz_ref[:, Q_DIM:Q_DIM + KV_DIM] = dzk.astype(BF16)
        dz_ref[:, Q_DIM + KV_DIM:width] = dzv.astype(BF16)

        @pl.when(i == 0)
        def _():
            sum_ref[...] = jnp.zeros_like(sum_ref)

        sum_ref[:, 0:Q_DIM] += jnp.sum(dzq, axis=0, keepdims=True)
        sum_ref[:, Q_DIM:Q_DIM + KV_DIM] += jnp.sum(dzk, axis=0, keepdims=True)
        sum_ref[:, Q_DIM + KV_DIM:width] += jnp.sum(dzv, axis=0, keepdims=True)

    return _pcall(
        body, name="rope_bwd", grid=(lp // BLOCK,),
        in_specs=[_row_spec(Q_DIM), head_spec, head_spec, meta_spec, meta_spec, _row_spec(LANES), _row_spec(LANES)],
        out_specs=[_row_spec(width), _const_spec((1, width))],
        out_shape=[jax.ShapeDtypeStruct((lp, width), BF16), jax.ShapeDtypeStruct((1, width), F32)],
        scratch_shapes=[pltpu.VMEM((BLOCK, KV_DIM), F32), pltpu.VMEM((BLOCK, KV_DIM), F32)],
        compiler_params=_params(("arbitrary",)),
    )(dq, dk, dv, dkm, dvm, ctab, stab)


def _attn_bias(i):
    r = lax.broadcasted_iota(jnp.int32, (BLOCK, 3 * BLOCK), 0)
    c = lax.broadcasted_iota(jnp.int32, (BLOCK, 3 * BLOCK), 1)
    qp = i * BLOCK + r - PAD_ROWS
    kp = (i - 1) * BLOCK + c - PAD_ROWS
    band = (c < 2 * BLOCK) & (kp >= N_META) & (kp <= qp) & (qp - kp < WINDOW)
    mp = c - 2 * BLOCK - PAD_ROWS
    meta = (c >= 2 * BLOCK) & (mp >= 0) & (mp <= qp)
    return jnp.where(band | meta, 0.0, NEG).astype(F32)


HALF = BLOCK // 2
HALF_KEYS = 2 * BLOCK


def _half_keys(prev, own, meta, half):
    if half == 0:
        return jnp.concatenate([prev, own[0:HALF], meta[HALF:BLOCK]], axis=0)
    return jnp.concatenate([prev[HALF:BLOCK], own, meta[HALF:BLOCK]], axis=0)


def _half_bias(i, half):
    r = lax.broadcasted_iota(jnp.int32, (HALF, HALF_KEYS), 0) + half * HALF
    c = lax.broadcasted_iota(jnp.int32, (HALF, HALF_KEYS), 1)
    n_prev = BLOCK - half * HALF
    qp = i * BLOCK + r - PAD_ROWS
    kp = jnp.where(c < n_prev, (i - 1) * BLOCK + c + half * HALF, i * BLOCK + c - n_prev) - PAD_ROWS
    band = (c < HALF_KEYS - HALF) & (kp >= N_META) & (kp <= qp) & (qp - kp < WINDOW)
    mp = c - (HALF_KEYS - HALF) + HALF - PAD_ROWS
    meta = (c >= HALF_KEYS - HALF) & (mp >= 0) & (mp <= qp)
    return jnp.where(band | meta, 0.0, NEG).astype(F32)


def _half_rows(ref, heads, half):
    rows = slice(half * HALF, (half + 1) * HALF)
    return jnp.concatenate([ref[rows, n * HEAD_DIM:(n + 1) * HEAD_DIM] for n in heads], axis=0)


def _half_sinks(sink_ref, heads):
    return jnp.concatenate([jnp.broadcast_to(sink_ref[0:1, n:n + 1], (HALF, 1)) for n in heads], axis=0)


def _stack_heads(ref, h):
    return jnp.concatenate(
        [ref[:, (h * GROUP + g) * HEAD_DIM:(h * GROUP + g + 1) * HEAD_DIM] for g in range(GROUP)], axis=0)


def _attn_probs(qs, k3, bias8, sink):
    s = lax.dot_general(qs, k3, (((1,), (1,)), ((), ())), preferred_element_type=F32) + bias8
    m = jnp.maximum(jnp.max(s, axis=1, keepdims=True), sink)
    p = jnp.exp(s - m)
    ps = jnp.exp(sink - m)
    inv = 1.0 / (jnp.sum(p, axis=1, keepdims=True) + ps)
    return p * inv, ps * inv


def _sink_column(sink_ref, h):
    return jnp.concatenate(
        [jnp.broadcast_to(sink_ref[0:1, h * GROUP + g:h * GROUP + g + 1], (BLOCK, 1)) for g in range(GROUP)], axis=0)


def _attn_fwd(q, k_sh, v_sh, sinks):
    lp = q.shape[0]
    nb = lp // BLOCK
    kv = lambda f: pl.BlockSpec((N_KV_HEADS, BLOCK, HEAD_DIM), f)

    def body(q_ref, kp_ref, kc_ref, km_ref, vp_ref, vc_ref, vm_ref, sink_ref, o_ref):
        i = pl.program_id(0)
        for half in range(2):
            bias = jnp.tile(_half_bias(i, half), (GROUP, 1))
            rows = slice(half * HALF, (half + 1) * HALF)
            for h in range(N_KV_HEADS):
                heads = range(h * GROUP, (h + 1) * GROUP)
                keys = _half_keys(kp_ref[h], kc_ref[h], km_ref[h], half)
                vals = _half_keys(vp_ref[h], vc_ref[h], vm_ref[h], half)
                p, _ = _attn_probs(_half_rows(q_ref, heads, half), keys, bias, _half_sinks(sink_ref, heads))
                o = jnp.dot(p.astype(BF16), vals, preferred_element_type=F32)
                for j, n in enumerate(heads):
                    o_ref[rows, n * HEAD_DIM:(n + 1) * HEAD_DIM] = o[j * HALF:(j + 1) * HALF].astype(BF16)

    prev, cur, meta = (lambda i: (0, i, 0)), (lambda i: (0, i + 1, 0)), (lambda i: (0, 1, 0))
    return _pcall(
        body, name="attn_fwd", grid=(nb,),
        in_specs=[_row_spec(Q_DIM), kv(prev), kv(cur), kv(meta), kv(prev), kv(cur), kv(meta), _const_spec((1, N_Q_HEADS))],
        out_specs=_row_spec(Q_DIM), out_shape=jax.ShapeDtypeStruct((lp, Q_DIM), BF16),
        compiler_params=_params(("parallel",)),
    )(q, k_sh, k_sh, k_sh, v_sh, v_sh, v_sh, sinks)


def _attn_bwd(q, k_sh, v_sh, sinks, do):
    lp = q.shape[0]
    nb = lp // BLOCK
    kv = lambda f: pl.BlockSpec((N_KV_HEADS, BLOCK, HEAD_DIM), f)
    cl = lambda s: jnp.minimum(s, nb - 1)

    def body(q_ref, do_ref, kp_ref, kc_ref, km_ref, vp_ref, vc_ref, vm_ref, sink_ref,
             dq_ref, dk_ref, dv_ref, dkm_ref, dvm_ref, dsink_ref, carry_k, carry_v):
        step = pl.program_id(0)

        @pl.when(step == 0)
        def _():
            carry_k[...] = jnp.zeros_like(carry_k)
            carry_v[...] = jnp.zeros_like(carry_v)
            dkm_ref[...] = jnp.zeros_like(dkm_ref)
            dvm_ref[...] = jnp.zeros_like(dvm_ref)
            dsink_ref[...] = jnp.zeros_like(dsink_ref)

        @pl.when(step < nb)
        def _():
            bias8 = jnp.tile(_attn_bias(step), (GROUP, 1))
            lane = lax.broadcasted_iota(jnp.int32, (1, LANES), 1)
            dsink = jnp.zeros((1, LANES), F32)
            for h in range(N_KV_HEADS):
                k3 = jnp.concatenate([kp_ref[h], kc_ref[h], km_ref[h]], axis=0)
                v3 = jnp.concatenate([vp_ref[h], vc_ref[h], vm_ref[h]], axis=0)
                qs = _stack_heads(q_ref, h)
                dos = _stack_heads(do_ref, h)
                p, psink = _attn_probs(qs, k3, bias8, _sink_column(sink_ref, h))
                dp = lax.dot_general(dos, v3, (((1,), (1,)), ((), ())), preferred_element_type=F32)
                delta = jnp.sum(p * dp, axis=1, keepdims=True)
                ds = (p * (dp - delta)).astype(BF16)
                dsk = -psink * delta
                for g in range(GROUP):
                    val = jnp.sum(dsk[g * BLOCK:(g + 1) * BLOCK], axis=0, keepdims=True)
                    dsink = dsink + jnp.where(lane == h * GROUP + g, val, 0.0)
                dqs = jnp.dot(ds, k3, preferred_element_type=F32)
                for g in range(GROUP):
                    n = h * GROUP + g
                    dq_ref[:, n * HEAD_DIM:(n + 1) * HEAD_DIM] = dqs[g * BLOCK:(g + 1) * BLOCK]
                dk3 = lax.dot_general(ds, qs, (((0,), (0,)), ((), ())), preferred_element_type=F32)
                dv3 = lax.dot_general(p.astype(BF16), dos, (((0,), (0,)), ((), ())), preferred_element_type=F32)
                dk_ref[h] = carry_k[h] + dk3[0:BLOCK]
                dv_ref[h] = carry_v[h] + dv3[0:BLOCK]
                carry_k[h] = dk3[BLOCK:2 * BLOCK]
                carry_v[h] = dv3[BLOCK:2 * BLOCK]
                dkm_ref[h] += dk3[2 * BLOCK:3 * BLOCK]
                dvm_ref[h] += dv3[2 * BLOCK:3 * BLOCK]
            dsink_ref[...] += dsink

        @pl.when(step == nb)
        def _():
            dk_ref[...] = carry_k[...]
            dv_ref[...] = carry_v[...]

    prev, cur, meta = (lambda s: (0, cl(s), 0)), (lambda s: (0, cl(s) + 1, 0)), (lambda s: (0, 1, 0))
    lag = lambda s: (0, jnp.maximum(s - 1, 0), 0)
    head_shape = jax.ShapeDtypeStruct((N_KV_HEADS, lp, HEAD_DIM), F32)
    meta_shape = jax.ShapeDtypeStruct((N_KV_HEADS, BLOCK, HEAD_DIM), F32)
    return _pcall(
        body, name="attn_bwd", grid=(nb + 1,),
        in_specs=[pl.BlockSpec((BLOCK, Q_DIM), lambda s: (cl(s), 0)), pl.BlockSpec((BLOCK, Q_DIM), lambda s: (cl(s), 0)),
                  kv(prev), kv(cur), kv(meta), kv(prev), kv(cur), kv(meta), _const_spec((1, N_Q_HEADS))],
        out_specs=[pl.BlockSpec((BLOCK, Q_DIM), lambda s: (cl(s), 0)), kv(lag), kv(lag),
                   _const_spec((N_KV_HEADS, BLOCK, HEAD_DIM)), _const_spec((N_KV_HEADS, BLOCK, HEAD_DIM)), _const_spec((1, LANES))],
        out_shape=[jax.ShapeDtypeStruct((lp, Q_DIM), F32), head_shape, head_shape, meta_shape, meta_shape,
                   jax.ShapeDtypeStruct((1, LANES), F32)],
        scratch_shapes=[pltpu.VMEM((N_KV_HEADS, BLOCK, HEAD_DIM), F32), pltpu.VMEM((N_KV_HEADS, BLOCK, HEAD_DIM), F32)],
        compiler_params=_params(("arbitrary",)),
    )(q, do, k_sh, k_sh, k_sh, v_sh, v_sh, v_sh, sinks)


CONV_CHUNK = 256


SUBLANES = 8
SH_BASE = BLOCK - 4 * SUBLANES
SH_ROWS = BLOCK + 3 * SUBLANES
DSH_ROWS = SH_ROWS


def _shifted_windows(src, sh, base, rows):
    for b in range(1, SUBLANES):
        sh[b] = src[base + b:base + b + rows, :]


def _window(src, sh, base, start, cols):
    a, b = divmod(start - base, SUBLANES)
    if b == 0:
        return src[start:start + BLOCK, cols]
    return sh[b, SUBLANES * a:SUBLANES * a + BLOCK, cols]


def _glu_masked(a_ref, g_ref, base):
    rows = base + lax.broadcasted_iota(jnp.int32, (BLOCK, 1), 0)
    return jnp.where(rows >= PAD_ROWS, a_ref[...].astype(F32) * _sigmoid(g_ref[...].astype(F32)), 0.0)


def _conv_fwd(zc, conv_w, conv_b, ln_g, ln_b, after=None):
    lp = zc.shape[0]
    cd = zc.shape[1] // 2
    nb = lp // BLOCK
    chunk = min(CONV_CHUNK, cd)
    back = lambda col: (lambda i: (jnp.maximum(i - 1, 0), col))
    lo = BLOCK - (CONV_WIDTH - 1)

    def body(ap_ref, gp_ref, ac_ref, gc_ref, w_ref, b_ref, lg_ref, lb_ref, co_ref, c2_ref, ext, sh):
        i = pl.program_id(0)
        ext[0:BLOCK, :] = _glu_masked(ap_ref, gp_ref, (i - 1) * BLOCK)
        ext[BLOCK:2 * BLOCK, :] = _glu_masked(ac_ref, gc_ref, i * BLOCK)
        _shifted_windows(ext, sh, SH_BASE, SH_ROWS)
        for c0 in range(0, cd, chunk):
            cols = slice(c0, c0 + chunk)
            acc = jnp.zeros((BLOCK, chunk), F32)
            for k in range(CONV_WIDTH):
                acc = acc + _window(ext, sh, SH_BASE, lo + k, cols) * w_ref[k:k + 1, cols]
            co_ref[:, cols] = acc + b_ref[:, cols]
        x = co_ref[...]
        mu = jnp.mean(x, axis=-1, keepdims=True)
        xc = x - mu
        r = lax.rsqrt(jnp.mean(xc * xc, axis=-1, keepdims=True) + EPS)
        y = xc * r * lg_ref[...] + lb_ref[...]
        c2_ref[...] = (y * _sigmoid(y)).astype(BF16)

    return _pcall(
        body, after=after, name="conv_fwd", grid=(nb,),
        in_specs=[pl.BlockSpec((BLOCK, cd), back(0)), pl.BlockSpec((BLOCK, cd), back(1)), _row_spec(cd, 0), _row_spec(cd, 1),
                  _const_spec((CONV_ROWS, cd)), _const_spec((1, cd)), _const_spec((1, cd)), _const_spec((1, cd))],
        out_specs=[_row_spec(cd), _row_spec(cd)],
        out_shape=[jax.ShapeDtypeStruct((lp, cd), F32), jax.ShapeDtypeStruct((lp, cd), BF16)],
        scratch_shapes=[pltpu.VMEM((2 * BLOCK, cd), F32), pltpu.VMEM((SUBLANES, SH_ROWS, cd), F32)],
        compiler_params=_params(("arbitrary",)),
    )(zc, zc, zc, zc, conv_w, conv_b, ln_g, ln_b)


def _conv_bwd_norm(dc2, conv_out, ln_g, ln_b):
    lp, cd = conv_out.shape

    def body(d_ref, x_ref, lg_ref, lb_ref, dco_ref, dlg_ref, dlb_ref, dcb_ref):
        i = pl.program_id(0)
        x = x_ref[...]
        g = lg_ref[...]
        mu = jnp.mean(x, axis=-1, keepdims=True)
        xc = x - mu
        r = lax.rsqrt(jnp.mean(xc * xc, axis=-1, keepdims=True) + EPS)
        xhat = xc * r
        y = xhat * g + lb_ref[...]
        sg = _sigmoid(y)
        dy = d_ref[...] * (sg * (1.0 + y * (1.0 - sg)))
        dxhat = dy * g
        dx = r * (dxhat - jnp.mean(dxhat, axis=-1, keepdims=True) - xhat * jnp.mean(dxhat * xhat, axis=-1, keepdims=True))
        dco_ref[...] = dx

        @pl.when(i == 0)
        def _():
            dlg_ref[...] = jnp.zeros_like(dlg_ref)
            dlb_ref[...] = jnp.zeros_like(dlb_ref)
            dcb_ref[...] = jnp.zeros_like(dcb_ref)

        dlg_ref[...] += jnp.sum(dy * xhat, axis=0, keepdims=True)
        dlb_ref[...] += jnp.sum(dy, axis=0, keepdims=True)
        dcb_ref[...] += jnp.sum(dx, axis=0, keepdims=True)

    vec = jax.ShapeDtypeStruct((1, cd), F32)
    return _pcall(
        body, name="conv_bwd_norm", grid=(lp // BLOCK,),
        in_specs=[_row_spec(cd), _row_spec(cd), _const_spec((1, cd)), _const_spec((1, cd))],
        out_specs=[_row_spec(cd), _const_spec((1, cd)), _const_spec((1, cd)), _const_spec((1, cd))],
        out_shape=[jax.ShapeDtypeStruct((lp, cd), F32), vec, vec, vec],
        compiler_params=_params(("arbitrary",)),
    )(dc2, conv_out, ln_g, ln_b)


def _conv_bwd_taps(dco, zc, conv_w):
    lp, cd = dco.shape
    nb = lp // BLOCK
    chunk = min(CONV_CHUNK, cd)
    back = lambda col: (lambda i: (jnp.maximum(i - 1, 0), col))
    fwd = lambda i: (jnp.minimum(i + 1, nb - 1), 0)
    lo = BLOCK - (CONV_WIDTH - 1)

    def body(dc_ref, dn_ref, ap_ref, gp_ref, ac_ref, gc_ref, w_ref, dz_ref, sum_ref, dw_ref, ext, dext, dcb, sh, dsh):
        i = pl.program_id(0)
        ext[0:BLOCK, :] = _glu_masked(ap_ref, gp_ref, (i - 1) * BLOCK)
        ext[BLOCK:2 * BLOCK, :] = _glu_masked(ac_ref, gc_ref, i * BLOCK)
        dext[0:BLOCK, :] = dc_ref[...]
        dext[BLOCK:2 * BLOCK, :] = dn_ref[...] * (i < nb - 1).astype(F32)
        _shifted_windows(ext, sh, SH_BASE, SH_ROWS)
        _shifted_windows(dext, dsh, 0, DSH_ROWS)

        @pl.when(i == 0)
        def _():
            dw_ref[...] = jnp.zeros_like(dw_ref)
            sum_ref[...] = jnp.zeros_like(sum_ref)

        for c0 in range(0, cd, chunk):
            cols = slice(c0, c0 + chunk)
            dcur = dext[0:BLOCK, cols]
            acc = jnp.zeros((BLOCK, chunk), F32)
            for k in range(CONV_WIDTH):
                s = CONV_WIDTH - 1 - k
                acc = acc + _window(dext, dsh, 0, s, cols) * w_ref[k:k + 1, cols]
                dw_ref[k:k + 1, cols] += jnp.sum(dcur * _window(ext, sh, SH_BASE, lo + k, cols), axis=0, keepdims=True)
            dcb[:, cols] = acc
        rows = i * BLOCK + lax.broadcasted_iota(jnp.int32, (BLOCK, 1), 0)
        dc = jnp.where(rows >= PAD_ROWS, dcb[...], 0.0)
        a = ac_ref[...].astype(F32)
        sg = _sigmoid(gc_ref[...].astype(F32))
        da = dc * sg
        dg = dc * a * sg * (1.0 - sg)
        dz_ref[:, 0:cd] = da.astype(BF16)
        dz_ref[:, cd:2 * cd] = dg.astype(BF16)
        sum_ref[:, 0:cd] += jnp.sum(da, axis=0, keepdims=True)
        sum_ref[:, cd:2 * cd] += jnp.sum(dg, axis=0, keepdims=True)

    return _pcall(
        body, name="conv_bwd_taps", grid=(nb,),
        in_specs=[_row_spec(cd), pl.BlockSpec((BLOCK, cd), fwd),
                  pl.BlockSpec((BLOCK, cd), back(0)), pl.BlockSpec((BLOCK, cd), back(1)), _row_spec(cd, 0), _row_spec(cd, 1),
                  _const_spec((CONV_ROWS, cd))],
        out_specs=[_row_spec(2 * cd), _const_spec((1, 2 * cd)), _const_spec((CONV_ROWS, cd))],
        out_shape=[jax.ShapeDtypeStruct((lp, 2 * cd), BF16), jax.ShapeDtypeStruct((1, 2 * cd), F32),
                   jax.ShapeDtypeStruct((CONV_ROWS, cd), F32)],
        scratch_shapes=[pltpu.VMEM((2 * BLOCK, cd), F32), pltpu.VMEM((2 * BLOCK, cd), F32), pltpu.VMEM((BLOCK, cd), F32),
                        pltpu.VMEM((SUBLANES, SH_ROWS, cd), F32), pltpu.VMEM((SUBLANES, DSH_ROWS, cd), F32)],
        compiler_params=_params(("arbitrary",)),
    )(dco, dco, zc, zc, zc, zc, conv_w)


def _conv_out_gate(c2, wco_t, bias, br_a, zg, *, tm, tn):
    m, k = c2.shape
    d = wco_t.shape[0]
    tm, tn = _pick(m, tm, 16), _pick(d, tn, LANES)
    nj = d // tn

    def body(a_ref, w_ref, bias_ref, bra_ref, ga_ref, gb_ref, brb_ref, m_ref):
        acc = lax.dot_general(a_ref[...], w_ref[...], (((1,), (1,)), ((), ())), preferred_element_type=F32)
        brb = (acc + bias_ref[...]).astype(BF16)
        brb_ref[...] = brb
        ga, gb = ga_ref[...].astype(F32), gb_ref[...].astype(F32)
        m_ref[...] = (_sigmoid(ga) * bra_ref[...].astype(F32) + _sigmoid(gb) * brb.astype(F32)).astype(BF16)

    tile = pl.BlockSpec((tm, tn), lambda i, j: (i, j))
    shape = jax.ShapeDtypeStruct((m, d), BF16)
    return _pcall(
        body, name="conv_out_proj_gate", grid=(m // tm, nj),
        in_specs=[pl.BlockSpec((tm, k), lambda i, j: (i, 0)), pl.BlockSpec((tn, k), lambda i, j: (j, 0)),
                  pl.BlockSpec((1, tn), lambda i, j: (0, j)), tile, tile, pl.BlockSpec((tm, tn), lambda i, j: (i, j + nj))],
        out_specs=[tile, tile], out_shape=[shape, shape], compiler_params=_params(("parallel", "arbitrary")),
    )(c2, wco_t, bias, br_a, zg, zg)


def _gate_bwd(dm, a, b, zg):
    lp, d = a.shape

    def body(dm_ref, a_ref, b_ref, ga_ref, gb_ref, da_ref, db_ref, dz_ref, sum_ref, dbias_ref):
        i = pl.program_id(0)
        dm_ = dm_ref[...].astype(F32)
        sa = _sigmoid(ga_ref[...].astype(F32))
        sb = _sigmoid(gb_ref[...].astype(F32))
        db = dm_ * sb
        dga = dm_ * a_ref[...].astype(F32) * sa * (1.0 - sa)
        dgb = dm_ * b_ref[...].astype(F32) * sb * (1.0 - sb)
        da_ref[...] = (dm_ * sa).astype(BF16)
        db_ref[...] = db.astype(BF16)
        dz_ref[:, 0:d] = dga.astype(BF16)
        dz_ref[:, d:2 * d] = dgb.astype(BF16)

        @pl.when(i == 0)
        def _():
            sum_ref[...] = jnp.zeros_like(sum_ref)
            dbias_ref[...] = jnp.zeros_like(dbias_ref)

        sum_ref[:, 0:d] += jnp.sum(dga, axis=0, keepdims=True)
        sum_ref[:, d:2 * d] += jnp.sum(dgb, axis=0, keepdims=True)
        dbias_ref[...] += jnp.sum(db, axis=0, keepdims=True)

    return _pcall(
        body, name="gate_bwd", grid=(lp // BLOCK,),
        in_specs=[_row_spec(d), _row_spec(d), _row_spec(d), _row_spec(d, 0), _row_spec(d, 1)],
        out_specs=[_row_spec(d), _row_spec(d), _row_spec(2 * d), _const_spec((1, 2 * d)), _const_spec((1, d))],
        out_shape=[jax.ShapeDtypeStruct((lp, d), BF16), jax.ShapeDtypeStruct((lp, d), BF16), jax.ShapeDtypeStruct((lp, 2 * d), BF16),
                   jax.ShapeDtypeStruct((1, 2 * d), F32), jax.ShapeDtypeStruct((1, d), F32)],
        compiler_params=_params(("arbitrary",)),
    )(dm, a, b, zg, zg)


def _gate_up_swiglu(u2, wgu_t, *, row_block, tm, tn, name, filled=None, after=None):
    m, k = u2.shape
    f = wgu_t.shape[0] // 2
    tn = _pick(f, tn, LANES)
    nj = f // tn
    dims = (((1,), (1,)), ((), ()))
    n_filled = 0 if filled is None else 3

    def body(*refs):
        a_ref, wg_ref, wu_ref = refs[:3]
        g_ref, u_ref, act_ref = refs[3 + n_filled:]
        a = a_ref[...]
        g = lax.dot_general(a, wg_ref[...], dims, preferred_element_type=F32).astype(BF16)
        up = lax.dot_general(a, wu_ref[...], dims, preferred_element_type=F32).astype(BF16)
        g_ref[...] = g
        u_ref[...] = up
        gf = g.astype(F32)
        act_ref[...] = (gf * _sigmoid(gf) * up.astype(F32)).astype(BF16)

    out = pl.BlockSpec((tm, tn), lambda j: (row_block, j))
    shape = jax.ShapeDtypeStruct((m, f), BF16)
    return _pcall(
        body, after=after, name=name, grid=(nj,),
        in_specs=[pl.BlockSpec((tm, k), lambda j: (row_block, 0)), pl.BlockSpec((tn, k), lambda j: (j, 0)),
                  pl.BlockSpec((tn, k), lambda j: (j + nj, 0))] + [pl.BlockSpec(memory_space=pl.ANY)] * n_filled,
        out_specs=[out, out, out], out_shape=[shape, shape, shape],
        input_output_aliases={3 + i: i for i in range(n_filled)}, compiler_params=_params(("arbitrary",)),
    )(u2, wgu_t, wgu_t, *(filled or ()))


def _down_dx_swiglu_bwd(dh, wdown, g, up, *, tm, tn, after=None):
    m, k = dh.shape
    f = wdown.shape[0]
    tm, tn = _pick(m, tm, 16), _pick(f, tn, LANES)

    def body(a_ref, w_ref, g_ref, u_ref, o_ref):
        acc = lax.dot_general(a_ref[...], w_ref[...], (((1,), (1,)), ((), ())), preferred_element_type=F32)
        d = acc.astype(BF16).astype(F32)
        gf = g_ref[...].astype(F32)
        sg = _sigmoid(gf)
        o_ref[0] = (d * u_ref[...].astype(F32) * (sg * (1.0 + gf * (1.0 - sg)))).astype(BF16)
        o_ref[1] = (d * gf * sg).astype(BF16)

    tile = pl.BlockSpec((tm, tn), lambda i, j: (i, j))
    return _pcall(
        body, after=after, name="ffn_down_dx_swiglu_bwd", grid=(m // tm, f // tn),
        in_specs=[pl.BlockSpec((tm, k), lambda i, j: (i, 0)), pl.BlockSpec((tn, k), lambda i, j: (j, 0)), tile, tile],
        out_specs=pl.BlockSpec((2, tm, tn), lambda i, j: (0, i, j)), out_shape=jax.ShapeDtypeStruct((2, m, f), BF16),
        compiler_params=_params(("parallel", "arbitrary")),
    )(dh, wdown, g, up)


ANY = pl.BlockSpec(memory_space=pl.ANY)


def _all_gather_rows(x, name, after=None):
    r, c = x.shape

    def body(x_ref, out_ref, send_sems, recv_sems, local_sem):
        mx, my, mc = lax.axis_index("x"), lax.axis_index("y"), lax.axis_index("c")
        me, sibling = (mx, my, mc), (mx, my, 1 - mc)
        chips = [(1 - mx, my), (mx, 1 - my), (1 - mx, 1 - my)]

        def rows(px, py, pc):
            return out_ref.at[pl.ds((4 * px + 2 * py + pc) * r, r), :]

        def copy(k, block, to, src=None):
            return pltpu.make_async_remote_copy(
                src_ref=rows(*block) if src is None else src, dst_ref=rows(*block),
                send_sem=send_sems.at[k], recv_sem=recv_sems.at[k], device_id=to, device_id_type=MESH)

        mine = pltpu.make_async_copy(x_ref, rows(*me), local_sem)
        mine.start()
        first = [copy(0, me, sibling, src=x_ref)]
        first += [copy(1 + j, me, (*chip, mc), src=x_ref) for j, chip in enumerate(chips)]
        for cp in first:
            cp.start()
        passed = [copy(4 + j, (*chip, mc), sibling) for j, chip in enumerate(chips)]
        for j, chip in enumerate(chips):
            copy(1 + j, (*chip, mc), me).wait_recv()
            passed[j].start()
        copy(0, sibling, me).wait_recv()
        for j, chip in enumerate(chips):
            copy(4 + j, (*chip, 1 - mc), me).wait_recv()
        for cp in first + passed:
            cp.wait_send()
        mine.wait()

    return _pcall(
        body, after=after, name=name, in_specs=[ANY], out_specs=ANY, out_shape=jax.ShapeDtypeStruct((N_DEV * r, c), x.dtype),
        scratch_shapes=[pltpu.SemaphoreType.DMA((7,)), pltpu.SemaphoreType.DMA((7,)), pltpu.SemaphoreType.DMA(())],
    )(x)


HBM = pl.BlockSpec(memory_space=pltpu.HBM)
SEM = pl.BlockSpec(memory_space=pltpu.SEMAPHORE)
IN_FLIGHT = pltpu.CompilerParams(has_side_effects=pltpu.SideEffectType.DATAFLOW_SIDE_EFFECTING)


def _place_rows(shard, after, name):
    r, c = shard.shape
    tr = _pick(r, max(16, ELEMENTWISE_BLOCK_BYTES // (4 * c)), 16)
    steps = r // tr
    dev = (4 * lax.axis_index("x") + 2 * lax.axis_index("y") + lax.axis_index("c")).astype(jnp.int32).reshape(1)

    def body(dev_ref, x_ref, after_ref, o_ref):
        o_ref[...] = x_ref[...].astype(BF16)

    return _pcall(
        body, name=name,
        grid_spec=pltpu.PrefetchScalarGridSpec(
            num_scalar_prefetch=1, grid=(steps,),
            in_specs=[pl.BlockSpec((tr, c), lambda i, dev_ref: (i, 0)), pl.BlockSpec(memory_space=pl.ANY)],
            out_specs=pl.BlockSpec((tr, c), lambda i, dev_ref: (dev_ref[0] * steps + i, 0))),
        out_shape=jax.ShapeDtypeStruct((N_DEV * r, c), BF16), compiler_params=_params(("parallel",)),
    )(dev, shard, after)


def _rows_start(full, plan, name, after=None):
    r = full.shape[0] // N_DEV
    n = len(plan(0, 0, 0))

    ordered = after is not None

    def body(*refs):
        full_ref, (send_sems, recv_sems) = refs[0], refs[1 + ordered:3 + ordered]
        mx, my, mc = lax.axis_index("x"), lax.axis_index("y"), lax.axis_index("c")
        for k, ((bx, by, bc), target) in enumerate(plan(mx, my, mc)):
            rows = full_ref.at[pl.ds((4 * bx + 2 * by + bc) * r, r), :]
            pltpu.make_async_remote_copy(
                src_ref=rows, dst_ref=rows, send_sem=send_sems.at[k], recv_sem=recv_sems.at[k],
                device_id=target, device_id_type=MESH).start()

    return pl.pallas_call(
        body, name=name, in_specs=[HBM] + [pl.BlockSpec(memory_space=pl.ANY)] * ordered, out_specs=(SEM, SEM, HBM),
        out_shape=(pltpu.SemaphoreType.DMA((n,)), pltpu.SemaphoreType.DMA((n,)), pltpu.HBM(full.shape, full.dtype)),
        input_output_aliases={0: 2}, compiler_params=IN_FLIGHT,
    )(pltpu.with_memory_space_constraint(full, pltpu.HBM), *([after] if ordered else []))


def _rows_wait(started, after, name):
    send_sem, recv_sem, full = started
    r = full.shape[0] // N_DEV
    n = send_sem.shape[0]

    def body(full_ref, send_ref, recv_ref, after_ref, out_ref):
        mx, my, mc = lax.axis_index("x"), lax.axis_index("y"), lax.axis_index("c")
        block = full_ref.at[pl.ds(0, r), :]
        for k in range(n):
            cp = pltpu.make_async_remote_copy(
                src_ref=block, dst_ref=block, send_sem=send_ref.at[k], recv_sem=recv_ref.at[k],
                device_id=(mx, my, mc), device_id_type=MESH)
            cp.wait_send()
            cp.wait_recv()

    return pl.pallas_call(
        body, name=name, in_specs=[HBM, SEM, SEM, pl.BlockSpec(memory_space=pl.ANY)], out_specs=HBM,
        out_shape=pltpu.HBM(full.shape, full.dtype), input_output_aliases={0: 0}, compiler_params=IN_FLIGHT,
    )(full, send_sem, recv_sem, after)


def _plan_direct(mx, my, mc):
    me = (mx, my, mc)
    return [(me, (mx, my, 1 - mc)), (me, (1 - mx, my, mc)), (me, (mx, 1 - my, mc)), (me, (1 - mx, 1 - my, mc))]


def _plan_pass_on(mx, my, mc):
    sibling = (mx, my, 1 - mc)
    return [((1 - mx, my, mc), sibling), ((mx, 1 - my, mc), sibling), ((1 - mx, 1 - my, mc), sibling)]


def _plan_neighbours(mx, my, mc):
    me = (mx, my, mc)
    return [(me, (mx, my, 1 - mc)), (me, (1 - mx, my, mc)), (me, (mx, 1 - my, mc))]


def _plan_relay(mx, my, mc):
    sibling = (mx, my, 1 - mc)
    source = ((mx + 1 - mc) % 2, (my + mc) % 2, mc)
    target = ((mx + mc) % 2, (my + 1 - mc) % 2, mc)
    return [((1 - mx, my, mc), sibling), ((mx, 1 - my, mc), sibling), (source, target)]


def _plan_pass_on_diagonal(mx, my, mc):
    return [((1 - mx, 1 - my, mc), (mx, my, 1 - mc))]


def _pair_exchange_start(g, name):
    r = g.shape[0] // N_DEV
    c = g.shape[1]
    land = (len(CHIPS), r, c)

    def body(g_ref, land_ref, send_sems, recv_sems, g_out, land_out):
        mx, my, mc = lax.axis_index("x"), lax.axis_index("y"), lax.axis_index("c")
        for j, (px, py) in enumerate(CHIPS):
            pltpu.make_async_remote_copy(
                src_ref=g_ref.at[pl.ds((4 * px + 2 * py + 1 - mc) * r, r), :], dst_ref=land_ref.at[j],
                send_sem=send_sems.at[j], recv_sem=recv_sems.at[j], device_id=(mx, my, 1 - mc), device_id_type=MESH).start()

    return pl.pallas_call(
        body, name=name, in_specs=[HBM, HBM], out_specs=(SEM, SEM, HBM, HBM),
        out_shape=(pltpu.SemaphoreType.DMA((4,)), pltpu.SemaphoreType.DMA((4,)), pltpu.HBM(g.shape, g.dtype), pltpu.HBM(land, g.dtype)),
        input_output_aliases={0: 2, 1: 3}, compiler_params=IN_FLIGHT,
    )(pltpu.with_memory_space_constraint(g, pltpu.HBM), pltpu.with_memory_space_constraint(lax.empty(land, g.dtype), pltpu.HBM))


def _pair_exchange_wait(send_sem, recv_sem, g, land, after, name):
    def body(g_ref, land_ref, send_ref, recv_ref, after_ref, g_out, land_out):
        mx, my, mc = lax.axis_index("x"), lax.axis_index("y"), lax.axis_index("c")
        for j in range(len(CHIPS)):
            cp = pltpu.make_async_remote_copy(
                src_ref=land_ref.at[0], dst_ref=land_ref.at[0], send_sem=send_ref.at[j], recv_sem=recv_ref.at[j],
                device_id=(mx, my, mc), device_id_type=MESH)
            cp.wait_send()
            cp.wait_recv()

    return pl.pallas_call(
        body, name=name, in_specs=[HBM, HBM, SEM, SEM, pl.BlockSpec(memory_space=pl.ANY)], out_specs=(HBM, HBM),
        out_shape=(pltpu.HBM(g.shape, g.dtype), pltpu.HBM(land.shape, land.dtype)), input_output_aliases={0: 0, 1: 1},
        compiler_params=IN_FLIGHT,
    )(g, land, send_sem, recv_sem, after)


def _chip_exchange_start(ps, after, name):
    def body(ps_ref, rx_ref, after_ref, send_sems, recv_sems, ps_out, rx_out):
        mx, my, mc = lax.axis_index("x"), lax.axis_index("y"), lax.axis_index("c")
        chips = [(1 - mx, my), (mx, 1 - my), (1 - mx, 1 - my)]
        for k, (px, py) in enumerate(chips):
            pltpu.make_async_remote_copy(
                src_ref=ps_ref.at[2 * px + py], dst_ref=rx_ref.at[2 * mx + my], send_sem=send_sems.at[k], recv_sem=recv_sems.at[k],
                device_id=(px, py, mc), device_id_type=MESH).start()

    return pl.pallas_call(
        body, name=name, in_specs=[HBM, HBM, pl.BlockSpec(memory_space=pl.ANY)], out_specs=(SEM, SEM, HBM, HBM),
        out_shape=(pltpu.SemaphoreType.DMA((3,)), pltpu.SemaphoreType.DMA((3,)), pltpu.HBM(ps.shape, ps.dtype), pltpu.HBM(ps.shape, ps.dtype)),
        input_output_aliases={0: 2, 1: 3}, compiler_params=IN_FLIGHT,
    )(pltpu.with_memory_space_constraint(ps, pltpu.HBM), pltpu.with_memory_space_constraint(lax.empty(ps.shape, ps.dtype), pltpu.HBM), after)


def _chip_exchange_wait(send_sem, recv_sem, ps, rx, after, name):
    def body(ps_ref, rx_ref, send_ref, recv_ref, after_ref, ps_out, rx_out):
        mx, my, mc = lax.axis_index("x"), lax.axis_index("y"), lax.axis_index("c")
        for k in range(3):
            cp = pltpu.make_async_remote_copy(
                src_ref=ps_ref.at[0], dst_ref=rx_ref.at[0], send_sem=send_ref.at[k], recv_sem=recv_ref.at[k],
                device_id=(mx, my, mc), device_id_type=MESH)
            cp.wait_send()
            cp.wait_recv()

    return pl.pallas_call(
        body, name=name, in_specs=[HBM, HBM, SEM, SEM, pl.BlockSpec(memory_space=pl.ANY)], out_specs=(HBM, HBM),
        out_shape=(pltpu.HBM(ps.shape, ps.dtype), pltpu.HBM(rx.shape, rx.dtype)), input_output_aliases={0: 0, 1: 1},
        compiler_params=IN_FLIGHT,
    )(ps, rx, send_sem, recv_sem, after)


def _sum_chips(ps, rx, name):
    n, r, c = rx.shape
    tr = _pick(r, max(16, 4 * ELEMENTWISE_BLOCK_BYTES // (4 * n * c)), 16)
    chip =(2 * lax.axis_index("x") + lax.axis_index("y")).astype(jnp.int32).reshape(1)

    def body(chip_ref, own_ref, x_ref, o_ref):
        me = chip_ref[0]
        own = own_ref[0].astype(F32)
        acc = jnp.where(me == 0, own, x_ref[0].astype(F32))
        for j in range(1, n):
            acc = acc + jnp.where(me == j, own, x_ref[j].astype(F32))
        o_ref[...] = acc

    return _pcall(
        body, name=name,
        grid_spec=pltpu.PrefetchScalarGridSpec(
            num_scalar_prefetch=1, grid=(r // tr,),
            in_specs=[pl.BlockSpec((1, tr, c), lambda i, chip_ref: (chip_ref[0], i, 0)), pl.BlockSpec((n, tr, c), lambda i, chip_ref: (0, i, 0))],
            out_specs=pl.BlockSpec((tr, c), lambda i, chip_ref: (i, 0))),
        out_shape=jax.ShapeDtypeStruct((r, c), F32), compiler_params=_params(("parallel",)),
    )(chip, ps, rx)


def _pair_exchange(g, name):
    r = g.shape[0] // N_DEV
    c = g.shape[1]

    def body(g_ref, theirs_ref, send_sems, recv_sems):
        mx, my, mc = lax.axis_index("x"), lax.axis_index("y"), lax.axis_index("c")
        sibling = (mx, my, 1 - mc)
        copies = []
        for j, (px, py) in enumerate(CHIPS):
            give = g_ref.at[pl.ds((4 * px + 2 * py + 1 - mc) * r, r), :]
            rc = pltpu.make_async_remote_copy(
                src_ref=give, dst_ref=theirs_ref.at[j], send_sem=send_sems.at[j], recv_sem=recv_sems.at[j],
                device_id=sibling, device_id_type=MESH)
            rc.start()
            copies.append(rc)
        for cp in copies:
            cp.wait()

    return _pcall(
        body, name=name, in_specs=[ANY], out_specs=ANY, out_shape=jax.ShapeDtypeStruct((len(CHIPS), r, c), g.dtype),
        scratch_shapes=[pltpu.SemaphoreType.DMA((4,)), pltpu.SemaphoreType.DMA((4,))],
    )(g)


def _pair_sum(g, theirs, name):
    nch, r, c = theirs.shape
    tr = _pick(r, max(16, 3 * ELEMENTWISE_BLOCK_BYTES // (2 * c)), 16)
    core = lax.axis_index("c").astype(jnp.int32).reshape(1)

    def body(core_ref, a_ref, b_ref, o_ref):
        o_ref[...] = (a_ref[...].astype(F32) + b_ref[...].astype(F32)).astype(o_ref.dtype)

    spec = pl.BlockSpec((1, tr, c), lambda j, i, core_ref: (j, i, 0))
    own = pl.BlockSpec((1, tr, c), lambda j, i, core_ref: (2 * j + core_ref[0], i, 0))
    return _pcall(
        body, name=name,
        grid_spec=pltpu.PrefetchScalarGridSpec(num_scalar_prefetch=1, grid=(nch, r // tr), in_specs=[own, spec], out_specs=spec),
        out_shape=jax.ShapeDtypeStruct(theirs.shape, theirs.dtype), compiler_params=_params(("parallel", "parallel")),
    )(core, g.reshape(N_DEV, r, c), theirs)


def _sum_blocks(rx, name):
    n, r, c = rx.shape
    tr = _pick(r, max(8, ELEMENTWISE_BLOCK_BYTES // (4 * n * c)), 8)

    def body(x_ref, o_ref):
        acc = x_ref[0].astype(F32)
        for j in range(1, n):
            acc = acc + x_ref[j].astype(F32)
        o_ref[...] = acc

    return _pcall(
        body, name=name, grid=(r // tr,), in_specs=[pl.BlockSpec((n, tr, c), lambda i: (0, i, 0))],
        out_specs=pl.BlockSpec((tr, c), lambda i: (i, 0)), out_shape=jax.ShapeDtypeStruct((r, c), F32),
        compiler_params=_params(("parallel",)),
    )(rx)


def _adamw(w, g, m, v, name):
    r, c = w.shape
    tr = _pick(r, max(8, ELEMENTWISE_BLOCK_BYTES // (4 * c)), 8)

    def body(w_ref, g_ref, m_ref, v_ref, d_ref, nm_ref, nv_ref):
        d_ref[...], nm_ref[...], nv_ref[...] = _adam_update(w_ref[...], g_ref[...], m_ref[...], v_ref[...])

    spec = pl.BlockSpec((tr, c), lambda i: (i, 0))
    shp = jax.ShapeDtypeStruct((r, c), F32)
    return _pcall(
        body, name=name, grid=(r // tr,), in_specs=[spec] * 4, out_specs=[spec] * 3, out_shape=[shp] * 3,
        compiler_params=_params(("parallel",)),
    )(w, g, m, v)


def _adam_update(w, g, m, v):
    nm = ADAM_B1 * m + (1.0 - ADAM_B1) * g
    nv = ADAM_B2 * v + (1.0 - ADAM_B2) * (g * g)
    delta = -ADAM_LR * ((nm / (1.0 - ADAM_B1 ** ADAM_STEP)) / (jnp.sqrt(nv / (1.0 - ADAM_B2 ** ADAM_STEP)) + ADAM_EPS) + ADAM_WD * w)
    return delta, nm, nv


def _sum_chips_adamw(ps, rx, w, m, v, name):
    n, r, c = rx.shape
    tr = _pick(r, max(16, 4 * ELEMENTWISE_BLOCK_BYTES // (4 * n * c)), 16)
    chip = (2 * lax.axis_index("x") + lax.axis_index("y")).astype(jnp.int32).reshape(1)

    def body(chip_ref, own_ref, x_ref, w_ref, m_ref, v_ref, g_ref, d_ref, nm_ref, nv_ref):
        me = chip_ref[0]
        own = own_ref[0].astype(F32)
        g = jnp.where(me == 0, own, x_ref[0].astype(F32))
        for j in range(1, n):
            g = g + jnp.where(me == j, own, x_ref[j].astype(F32))
        g_ref[...] = g
        d_ref[...], nm_ref[...], nv_ref[...] = _adam_update(w_ref[...], g, m_ref[...], v_ref[...])

    rows = pl.BlockSpec((tr, c), lambda i, chip_ref: (i, 0))
    shp = jax.ShapeDtypeStruct((r, c), F32)
    return _pcall(
        body, name=name,
        grid_spec=pltpu.PrefetchScalarGridSpec(
            num_scalar_prefetch=1, grid=(r // tr,),
            in_specs=[pl.BlockSpec((1, tr, c), lambda i, chip_ref: (chip_ref[0], i, 0)),
                      pl.BlockSpec((n, tr, c), lambda i, chip_ref: (0, i, 0)), rows, rows, rows],
            out_specs=[rows] * 4),
        out_shape=[shp] * 4, compiler_params=_params(("parallel",)),
    )(chip, ps, rx, w, m, v)


def _pack(parts):
    flat, layout, row = [], [], 0
    for p in parts:
        n = p.size
        rows = -(-n // LANES)
        flat.append(jnp.pad(p.reshape(-1).astype(F32), (0, rows * LANES - n)))
        layout.append((row, n, p.shape))
        row += rows
    total = -(-row // 8) * 8
    if total > row:
        flat.append(jnp.zeros(((total - row) * LANES,), F32))
    return jnp.concatenate(flat).reshape(total, LANES), layout


def _unpack(slab, layout):
    flat = slab.reshape(-1)
    return [flat[row * LANES:row * LANES + n].reshape(shape) for row, n, shape in layout]


def kernel(x, meta_tokens, mix_norm_g, w_in, b_in, attn_sinks, conv_w, conv_b, conv_ln_g, conv_ln_b, w_attn_o, w_conv_o, b_conv_o, w_out, ffn_norm_g, w_gate_up, w_down, final_norm_g, loss_target, m_meta_tokens, m_mix_norm_g, m_w_in, m_b_in, m_attn_sinks, m_conv_w, m_conv_b, m_conv_ln_g, m_conv_ln_b, m_w_attn_o, m_w_conv_o, m_b_conv_o, m_w_out, m_ffn_norm_g, m_w_gate_up, m_w_down, m_final_norm_g, v_meta_tokens, v_mix_norm_g, v_w_in, v_b_in, v_attn_sinks, v_conv_w, v_conv_b, v_conv_ln_g, v_conv_ln_b, v_w_attn_o, v_w_conv_o, v_b_conv_o, v_w_out, v_ffn_norm_g, v_w_gate_up, v_w_down, v_final_norm_g):
    xs = x[0]
    tgt = loss_target[0]
    s, d = xs.shape
    lp = s + BLOCK
    cd = conv_b.shape[1]
    ffn = w_down.shape[1] * N_DEV
    dev = 4 * lax.axis_index("x") + 2 * lax.axis_index("y") + lax.axis_index("c")
    cw_cols = conv_w.shape[3]
    meta_cols = meta_tokens.shape[1]

    small, small_layout = _pack([meta_tokens, jnp.pad(conv_w[0, :, 0, :], ((0, CONV_ROWS - CONV_WIDTH), (0, 0)))])
    small_flat = _all_gather_rows(small, "gather_small")
    small_all = small_flat.reshape(N_DEV, *small.shape)
    meta_parts, cw_parts = zip(*[_unpack(small_all[j], small_layout) for j in range(N_DEV)])
    meta_full = jnp.concatenate(meta_parts, axis=1)
    conv_w_full = jnp.concatenate(cw_parts, axis=1)
    shards = ((w_in[0].T, "w_in"), (w_attn_o[0].T, "w_attn_o"), (w_conv_o[0].T, "w_conv_o"), (w_out[0], "w_out"),
              (w_gate_up[0].T, "w_gate_up"), (w_down[0], "w_down"))
    first = _rows_start(_place_rows(shards[0][0], small_flat, "place_w_in"), _plan_neighbours, "gather_start_w_in")
    placed, tok = [], first[2]
    for shard, name in shards[1:]:
        tok = _place_rows(shard, tok, "place_" + name)
        placed.append(tok)
    relay = _rows_start(_rows_wait(first, tok, "gather_wait_w_in"), _plan_relay, "gather_relay_start_w_in")
    h0, u = _prep(xs, meta_full, mix_norm_g, after=relay[2])
    diagonal = _rows_start(_rows_wait(relay, u, "gather_relay_wait_w_in"), _plan_pass_on_diagonal, "gather_diagonal_start_w_in")
    started, tok = [None], diagonal[2]
    for full, (_, name) in zip(placed, shards[1:]):
        started.append(_rows_start(full, _plan_direct, "gather_start_" + name, after=tok))
        tok = started[-1][2]
    win_t = _rows_wait(diagonal, tok, "gather_diagonal_wait_w_in")

    def arrived(w, after, name):
        return _rows_start(_rows_wait(started[w], after, "gather_wait_" + name), _plan_pass_on, "gather_pass_on_start_" + name)

    def whole(passing, after, name):
        return _rows_wait(passing, after, "gather_pass_on_wait_" + name)

    ctab, stab = _rope_tables(lp)
    mm = functools.partial(_matmul, tm=1056, tn=1024)

    bq, bkv, bc, bg = b_in[:, :Q_DIM], b_in[:, Q_DIM:Q_DIM + 2 * KV_DIM], b_in[:, Q_DIM + 2 * KV_DIM:Q_DIM + 2 * KV_DIM + 2 * cd], b_in[:, Q_DIM + 2 * KV_DIM + 2 * cd:]
    o_kv, o_c, o_g = Q_DIM, Q_DIM + 2 * KV_DIM, Q_DIM + 2 * KV_DIM + 2 * cd
    in_proj = functools.partial(_matmul, u, win_t, mode="nt", out_dtype=BF16, tm=2112, tn=512, tk=d)
    zq = in_proj(name="in_proj_q", bias=bq, b_row_off=0, b_rows=Q_DIM)
    zkv = in_proj(name="in_proj_kv", bias=bkv, b_row_off=o_kv, b_rows=2 * KV_DIM)
    zc = in_proj(name="in_proj_conv", bias=bc, b_row_off=o_c, b_rows=2 * cd)
    zg = in_proj(name="in_proj_gates", bias=bg, b_row_off=o_g, b_rows=2 * d)
    passing = arrived(1, zg, "w_attn_o")
    q_rot, k_sh, v_sh = _rope_fwd(zq, zkv, ctab, stab, after=passing[2])
    o = _attn_fwd(q_rot, k_sh, v_sh, attn_sinks)
    wao_t = whole(passing, o, "w_attn_o")
    br_a = mm(o, wao_t, mode="nt", name="attn_out_proj", out_dtype=BF16, tk=Q_DIM)
    passing = arrived(2, br_a, "w_conv_o")
    passing_out = arrived(3, passing[2], "w_out")
    conv_out, c2 = _conv_fwd(zc, conv_w_full, conv_b, conv_ln_g, conv_ln_b, after=passing_out[2])
    wco_t = whole(passing, c2, "w_conv_o")
    br_b, merged = _conv_out_gate(c2, wco_t, b_conv_o, br_a, zg, tm=1056, tn=1024)
    wout = whole(passing_out, merged, "w_out")
    passing = arrived(4, wout, "w_gate_up")
    h1 = mm(merged, wout, mode="nn", name="mix_out_proj", out_dtype=F32, tn=512, tk=d, residual=h0, after=passing[2])
    u2 = _rmsnorm_fwd(h1, ffn_norm_g, "ffn_rmsnorm")
    wgu_t = whole(passing, u2, "w_gate_up")
    half_rows = _pick(lp, lp // 2, 16)
    ffn_in = _gate_up_swiglu(u2, wgu_t, row_block=0, tm=half_rows, tn=256, name="ffn_gate_up_swiglu_0")
    passing = arrived(5, ffn_in[2], "w_down")
    for rb in range(1, lp // half_rows):
        ffn_in = _gate_up_swiglu(u2, wgu_t, row_block=rb, tm=half_rows, tn=256, name="ffn_gate_up_swiglu_%d" % rb,
                                 filled=ffn_in, after=passing[2])
    gu_g, gu_u, act = ffn_in
    wdown = whole(passing, act, "w_down")
    h2 = mm(act, wdown, mode="nn", name="ffn_down", out_dtype=F32, tm=528, tn=512, tk=ffn, residual=h1)
    dh2, dh2_b, loss_part, d_final_g = _final(h2, tgt, final_norm_g.reshape(1, d))

    wgrad = functools.partial(_matmul, mode="tn", out_dtype=BF16, tk=lp, tn=2048, b_inner=False)
    in_flight = {}

    def scatter_begin(g, name):
        return _pair_exchange_start(g, "rs_" + name + "_pair_start")

    def scatter_go_on(pair, after, name):
        g, theirs = _pair_exchange_wait(pair[0], pair[1], pair[2], pair[3], after, "rs_" + name + "_pair_wait")
        ps = _pair_sum(g, theirs, "rs_" + name + "_pair_sum")
        in_flight[name] = _chip_exchange_start(ps, theirs, "rs_" + name + "_chip_start")
        return in_flight[name][2]

    g_wdown = wgrad(act, dh2_b, name="ffn_down_dw", tm=256)
    pair = scatter_begin(g_wdown, "w_down")
    dgu = _down_dx_swiglu_bwd(dh2_b, wdown, gu_g, gu_u, tm=1056, tn=256, after=pair[2])
    tok = scatter_go_on(pair, dgu, "w_down")
    g_wgu_t = wgrad(dgu, u2, name="ffn_gate_up_dw", tm=256, after=tok)
    pair = scatter_begin(g_wgu_t, "w_gate_up")
    du2 = mm(dgu, wgu_t, mode="nn", name="ffn_gate_up_dx", out_dtype=F32, tn=512, tk=ffn // 2, after=pair[2])
    tok = scatter_go_on(pair, du2, "w_gate_up")
    dh1, dh1_b, d_ffn_g = _rmsnorm_bwd(du2, h1, ffn_norm_g, dh2, "ffn_rmsnorm_bwd")
    g_wout = wgrad(merged, dh1_b, name="mix_out_dw", tm=512, after=tok)
    pair = scatter_begin(g_wout, "w_out")
    dmerged = mm(dh1_b, wout, mode="nt", name="mix_out_dx", out_dtype=BF16, tm=2112, tn=512, tk=d, after=pair[2])
    tok = scatter_go_on(pair, dmerged, "w_out")
    d_a, d_b, dz_g, sum_g, d_bco = _gate_bwd(dmerged, br_a, br_b, zg)
    g_wao_t = wgrad(d_a, o, name="attn_out_dw", tm=512, after=tok)
    pair = scatter_begin(g_wao_t, "w_attn_o")
    do = mm(d_a, wao_t, mode="nn", name="attn_out_dx", out_dtype=BF16, tk=d, after=pair[2])
    tok = scatter_go_on(pair, do, "w_attn_o")
    g_wco_t = wgrad(d_b, c2, name="conv_out_dw", tm=512, after=tok)
    pair = scatter_begin(g_wco_t, "w_conv_o")
    dc2 = mm(d_b, wco_t, mode="nn", name="conv_out_dx", out_dtype=F32, tk=d, after=pair[2])
    tok = scatter_go_on(pair, dc2, "w_conv_o")
    dq, dk, dv, dkm, dvm, d_sinks = _attn_bwd(q_rot, k_sh, v_sh, attn_sinks, do)
    dz_qkv, sum_qkv = _rope_bwd(dq, dk, dv, dkm, dvm, ctab, stab)
    dco, d_ln_g, d_ln_b, d_conv_b = _conv_bwd_norm(dc2, conv_out, conv_ln_g, conv_ln_b)
    dz_c, sum_c, d_conv_w = _conv_bwd_taps(dco, zc, conv_w_full)
    dz = jnp.concatenate([dz_qkv, dz_c, dz_g], axis=1)
    d_b_in = jnp.concatenate([sum_qkv, sum_c, sum_g], axis=1)
    in_dim = dz.shape[1]
    g_win_t = wgrad(dz, u, name="in_proj_dw", tm=512, after=tok)
    theirs = _pair_exchange(g_win_t, "rs_w_in_pair_exchange")
    in_flight["w_in"] = _chip_exchange_start(_pair_sum(g_win_t, theirs, "rs_w_in_pair_sum"), theirs, "rs_w_in_chip_start")
    du = mm(dz, win_t, mode="nn", name="in_proj_dx", out_dtype=F32, tk=in_dim // 4, after=in_flight["w_in"][2])
    grad_x, d_meta, d_mix_g = _rmsnorm_bwd_first(du, h0, mix_norm_g, dh1)

    weights = dict(meta_tokens=meta_tokens, mix_norm_g=mix_norm_g, w_in=w_in, b_in=b_in, attn_sinks=attn_sinks, conv_w=conv_w,
                   conv_b=conv_b, conv_ln_g=conv_ln_g, conv_ln_b=conv_ln_b, w_attn_o=w_attn_o, w_conv_o=w_conv_o, b_conv_o=b_conv_o,
                   w_out=w_out, ffn_norm_g=ffn_norm_g, w_gate_up=w_gate_up, w_down=w_down, final_norm_g=final_norm_g)
    m_in = dict(meta_tokens=m_meta_tokens, mix_norm_g=m_mix_norm_g, w_in=m_w_in, b_in=m_b_in, attn_sinks=m_attn_sinks, conv_w=m_conv_w,
                conv_b=m_conv_b, conv_ln_g=m_conv_ln_g, conv_ln_b=m_conv_ln_b, w_attn_o=m_w_attn_o, w_conv_o=m_w_conv_o,
                b_conv_o=m_b_conv_o, w_out=m_w_out, ffn_norm_g=m_ffn_norm_g, w_gate_up=m_w_gate_up, w_down=m_w_down,
                final_norm_g=m_final_norm_g)
    v_in = dict(meta_tokens=v_meta_tokens, mix_norm_g=v_mix_norm_g, w_in=v_w_in, b_in=v_b_in, attn_sinks=v_attn_sinks, conv_w=v_conv_w,
                conv_b=v_conv_b, conv_ln_g=v_conv_ln_g, conv_ln_b=v_conv_ln_b, w_attn_o=v_w_attn_o, w_conv_o=v_w_conv_o,
                b_conv_o=v_b_conv_o, w_out=v_w_out, ffn_norm_g=v_ffn_norm_g, w_gate_up=v_w_gate_up, w_down=v_w_down,
                final_norm_g=v_final_norm_g)
    names = list(weights)
    grads, delta, new_m, new_v = {}, {}, {}, {}
    tok = grad_x
    for n in ("w_down", "w_gate_up", "w_out", "w_attn_o", "w_conv_o", "w_in"):
        send_sem, recv_sem, ps, rx = in_flight[n]
        ps, rx = _chip_exchange_wait(send_sem, recv_sem, ps, rx, tok, "rs_" + n + "_chip_wait")
        oriented = (lambda a: a[0].T) if n in ("w_in", "w_gate_up") else (lambda a: a[0])
        back = (lambda a: a.T[None]) if n in ("w_in", "w_gate_up") else (lambda a: a[None])
        if n in ("w_attn_o", "w_conv_o"):
            g = _sum_chips(ps, rx, "rs_" + n + "_sum").T
            dl, nm, nv = _adamw(oriented(weights[n]), g, oriented(m_in[n]), oriented(v_in[n]), "adamw_" + n)
        else:
            g, dl, nm, nv = _sum_chips_adamw(ps, rx, oriented(weights[n]), oriented(m_in[n]), oriented(v_in[n]), "rs_" + n + "_sum_adamw")
        grads[n], delta[n], new_m[n], new_v[n] = back(g), back(dl), back(nm), back(nv)
        tok = dl

    slab, slab_layout = _pack([loss_part[:, :1], d_mix_g, d_b_in, d_sinks[:, :N_Q_HEADS], d_conv_b, d_ln_g, d_ln_b, d_bco,
                               d_ffn_g, d_final_g, d_conv_w, d_meta])
    slab_all = _all_gather_rows(slab, "gather_small_grads", after=tok).reshape(N_DEV, *slab.shape)
    (loss, g_mix_g, g_b_in, g_sinks, g_conv_b, g_ln_g, g_ln_b, g_bco, g_ffn_g, g_final_g, g_conv_w_full, g_meta_full
     ) = _unpack(_sum_blocks(slab_all, "sum_small_grads"), slab_layout)
    g_conv_w = lax.dynamic_slice(g_conv_w_full, (0, dev * cw_cols), (CONV_WIDTH, cw_cols)).reshape(conv_w.shape)
    g_meta = lax.dynamic_slice(g_meta_full, (0, dev * meta_cols), (N_META, meta_cols))
    g_final_g = g_final_g.reshape(final_norm_g.shape)
    grads.update(meta_tokens=g_meta, mix_norm_g=g_mix_g, b_in=g_b_in, attn_sinks=g_sinks, conv_w=g_conv_w, conv_b=g_conv_b,
                 conv_ln_g=g_ln_g, conv_ln_b=g_ln_b, b_conv_o=g_bco, ffn_norm_g=g_ffn_g, final_norm_g=g_final_g)
    rest = [n for n in names if n not in delta]
    w_slab, rest_layout = _pack([weights[n] for n in rest])
    g_slab, _ = _pack([grads[n] for n in rest])
    m_slab, _ = _pack([m_in[n] for n in rest])
    v_slab, _ = _pack([v_in[n] for n in rest])
    dl, nm, nv = _adamw(w_slab, g_slab, m_slab, v_slab, "adamw_small")
    for n, a, b, c in zip(rest, _unpack(dl, rest_layout), _unpack(nm, rest_layout), _unpack(nv, rest_layout)):
        delta[n], new_m[n], new_v[n] = a, b, c

    return (loss.reshape(()), grad_x[None], *[grads[n] for n in names], *[delta[n] for n in names],
            *[new_m[n] for n in names], *[new_v[n] for n in names])
```

```python
import functools
import math

import jax
import jax.numpy as jnp
from jax import lax
from jax.experimental import pallas as pl
from jax.experimental.pallas import tpu as pltpu

F32 = jnp.float32
BF16 = jnp.bfloat16

N_DEV = 8
BLOCK = 128
N_META = 16
PAD_ROWS = BLOCK - N_META
HEAD_DIM = 64
N_Q_HEADS = 32
N_KV_HEADS = 4
GROUP = N_Q_HEADS // N_KV_HEADS
Q_DIM = N_Q_HEADS * HEAD_DIM
KV_DIM = N_KV_HEADS * HEAD_DIM
WINDOW = 128
CONV_WIDTH = 31
CONV_ROWS = 32
ROPE_THETA = 10000.0
EPS = 1e-6
ATTN_SCALE = HEAD_DIM ** -0.5
NEG = -1e30

ADAM_LR = 0.001
ADAM_B1 = 0.9
ADAM_B2 = 0.999
ADAM_EPS = 1e-08
ADAM_WD = 0.01
ADAM_STEP = 10

VMEM_LIMIT_BYTES = 56 * 1024 * 1024
LANES = 128
ELEMENTWISE_BLOCK_BYTES = 2 * 1024 * 1024
MESH = pl.DeviceIdType.MESH
CHIPS = ((0, 0), (0, 1), (1, 0), (1, 1))


def _pcall(body, after=None, **kw):
    if after is None:
        return pl.pallas_call(body, **kw)
    in_specs = list(kw.pop("in_specs"))
    n_in = len(in_specs)

    def ordered_body(*refs):
        return body(*refs[:n_in], *refs[n_in + 1:])

    call = pl.pallas_call(ordered_body, in_specs=in_specs + [pl.BlockSpec(memory_space=pl.ANY)], **kw)
    return lambda *args: call(*args, after)


def _params(semantics=None):
    if semantics is None:
        return pltpu.CompilerParams(vmem_limit_bytes=VMEM_LIMIT_BYTES)
    return pltpu.CompilerParams(dimension_semantics=semantics, vmem_limit_bytes=VMEM_LIMIT_BYTES)


def _pick(dim, pref, align):
    best = None
    t = align
    while t <= min(dim, pref):
        if dim % t == 0:
            best = t
        t += align
    return dim if best is None else best


def _sigmoid(x):
    return 1.0 / (1.0 + jnp.exp(-x))


def _matmul(a, b, *, mode, name, out_dtype, tm, tn, tk, bias=None, residual=None, b_inner=True,
            b_row_off=0, b_rows=None, after=None):
    halves = a.ndim == 3
    a_rows, a_cols = (a.shape[1], 2 * a.shape[2]) if halves else a.shape
    if mode == "nn":
        m, k = a_rows, a_cols
        n = b.shape[1]
    elif mode == "nt":
        m, k = a_rows, a_cols
        n = b.shape[0] if b_rows is None else b_rows
    else:
        k, m = a_rows, a_cols
        n = b.shape[1]
    tm = _pick(m // 2 if halves and mode == "tn" else m, tm, 16)
    tn = _pick(math.gcd(n, b_row_off) if mode == "nt" and b_row_off else n, tn, LANES)
    tk = _pick(k // 2 if halves and mode == "nn" else k, tk, LANES if mode != "tn" else 16)
    nm, nn, nk = m // tm, n // tn, k // tk
    if mode == "nt":
        assert b_row_off % tn == 0
    off = b_row_off // tn if mode == "nt" else 0

    if b_inner:
        grid = (nm, nn, nk)
        ij = lambda g0, g1: (g0, g1)
    else:
        grid = (nn, nm, nk)
        ij = lambda g0, g1: (g1, g0)

    if halves and mode == "tn":
        a_spec = pl.BlockSpec((None, tk, tm), lambda g0, g1, kk: (ij(g0, g1)[0] // (nm // 2), kk, ij(g0, g1)[0] % (nm // 2)))
    elif halves:
        assert mode == "nn"
        a_spec = pl.BlockSpec((None, tm, tk), lambda g0, g1, kk: (kk // (nk // 2), ij(g0, g1)[0], kk % (nk // 2)))
    elif mode == "tn":
        a_spec = pl.BlockSpec((tk, tm), lambda g0, g1, kk: (kk, ij(g0, g1)[0]))
    else:
        a_spec = pl.BlockSpec((tm, tk), lambda g0, g1, kk: (ij(g0, g1)[0], kk))
    if mode == "nt":
        b_spec = pl.BlockSpec((tn, tk), lambda g0, g1, kk: (ij(g0, g1)[1] + off, kk))
    else:
        b_spec = pl.BlockSpec((tk, tn), lambda g0, g1, kk: (kk, ij(g0, g1)[1]))
    o_spec = pl.BlockSpec((tm, tn), lambda g0, g1, kk: ij(g0, g1))
    in_specs = [a_spec, b_spec]
    args = [a, b]
    if bias is not None:
        in_specs.append(pl.BlockSpec((1, tn), lambda g0, g1, kk: (0, ij(g0, g1)[1])))
        args.append(bias)
    if residual is not None:
        in_specs.append(o_spec)
        args.append(residual)
    dims = {"nn": (((1,), (0,)), ((), ())), "nt": (((1,), (1,)), ((), ())), "tn": (((0,), (0,)), ((), ()))}[mode]
    has_bias, has_res = bias is not None, residual is not None

    def body(*refs):
        a_ref, b_ref = refs[0], refs[1]
        pos = 2
        bias_ref = res_ref = None
        if has_bias:
            bias_ref = refs[pos]
            pos += 1
        if has_res:
            res_ref = refs[pos]
            pos += 1
        o_ref = refs[pos]
        acc_ref = refs[pos + 1] if nk > 1 else None

        def finish(acc):
            if has_bias:
                acc = acc + bias_ref[...]
            if has_res:
                acc = acc + res_ref[...]
            o_ref[...] = acc.astype(out_dtype)

        p = lax.dot_general(a_ref[...], b_ref[...], dims, preferred_element_type=F32)
        if nk == 1:
            finish(p)
        else:
            kk = pl.program_id(2)

            @pl.when(kk == 0)
            def _():
                acc_ref[...] = p

            @pl.when(kk > 0)
            def _():
                acc_ref[...] += p

            @pl.when(kk == nk - 1)
            def _():
                finish(acc_ref[...])

    return _pcall(
        body, after=after, name=name, grid=grid, in_specs=in_specs, out_specs=o_spec,
        out_shape=jax.ShapeDtypeStruct((m, n), out_dtype),
        scratch_shapes=[pltpu.VMEM((tm, tn), F32)] if nk > 1 else [],
        compiler_params=_params(("parallel", "parallel", "arbitrary")),
    )(*args)


def _row_spec(width, col=0):
    return pl.BlockSpec((BLOCK, width), lambda i: (i, col))


def _const_spec(shape):
    nd = len(shape)
    return pl.BlockSpec(shape, lambda i: (0,) * nd)


def _prep(x, meta_full, g, after=None):
    s, d = x.shape
    lp = s + BLOCK
    nb = lp // BLOCK

    def body(x_ref, meta_ref, g_ref, h_ref, u_ref):
        i = pl.program_id(0)

        @pl.when(i == 0)
        def _():
            h_ref[0:PAD_ROWS, :] = jnp.zeros((PAD_ROWS, d), F32)
            h_ref[PAD_ROWS:BLOCK, :] = meta_ref[...]

        @pl.when(i > 0)
        def _():
            h_ref[...] = x_ref[...]

        h = h_ref[...]
        r = lax.rsqrt(jnp.mean(h * h, axis=-1, keepdims=True) + EPS)
        u_ref[...] = (h * r * g_ref[...]).astype(BF16)

    return _pcall(
        body, after=after, name="prep_rmsnorm", grid=(nb,),
        in_specs=[pl.BlockSpec((BLOCK, d), lambda i: (jnp.maximum(i - 1, 0), 0)), _const_spec((N_META, d)), _const_spec((1, d))],
        out_specs=[_row_spec(d), _row_spec(d)],
        out_shape=[jax.ShapeDtypeStruct((lp, d), F32), jax.ShapeDtypeStruct((lp, d), BF16)],
        compiler_params=_params(("arbitrary",)),
    )(x, meta_full, g)


def _rmsnorm_fwd(h, g, name):
    lp, d = h.shape

    def body(h_ref, g_ref, u_ref):
        x = h_ref[...]
        r = lax.rsqrt(jnp.mean(x * x, axis=-1, keepdims=True) + EPS)
        u_ref[...] = (x * r * g_ref[...]).astype(BF16)

    return _pcall(
        body, name=name, grid=(lp // BLOCK,), in_specs=[_row_spec(d), _const_spec((1, d))], out_specs=_row_spec(d),
        out_shape=jax.ShapeDtypeStruct((lp, d), BF16), compiler_params=_params(("parallel",)),
    )(h, g)


def _rms_bwd_core(dy, x, g):
    r = lax.rsqrt(jnp.mean(x * x, axis=-1, keepdims=True) + EPS)
    xhat = x * r
    dxhat = dy * g
    dx = r * (dxhat - xhat * jnp.mean(dxhat * xhat, axis=-1, keepdims=True))
    return dx, jnp.sum(dy * xhat, axis=0, keepdims=True)


def _rmsnorm_bwd(dy, h, g, dres, name):
    lp, d = h.shape

    def body(dy_ref, h_ref, g_ref, dres_ref, dh_ref, dhb_ref, dg_ref):
        i = pl.program_id(0)
        dx, dg = _rms_bwd_core(dy_ref[...], h_ref[...], g_ref[...])
        dh = dres_ref[...] + dx
        dh_ref[...] = dh
        dhb_ref[...] = dh.astype(BF16)

        @pl.when(i == 0)
        def _():
            dg_ref[...] = jnp.zeros_like(dg_ref)

        dg_ref[...] += dg

    return _pcall(
        body, name=name, grid=(lp // BLOCK,),
        in_specs=[_row_spec(d), _row_spec(d), _const_spec((1, d)), _row_spec(d)],
        out_specs=[_row_spec(d), _row_spec(d), _const_spec((1, d))],
        out_shape=[jax.ShapeDtypeStruct((lp, d), F32), jax.ShapeDtypeStruct((lp, d), BF16), jax.ShapeDtypeStruct((1, d), F32)],
        compiler_params=_params(("arbitrary",)),
    )(dy, h, g, dres)


def _rmsnorm_bwd_first(dy, h, g, dres):
    lp, d = h.shape
    s = lp - BLOCK

    def body(dy_ref, h_ref, g_ref, dres_ref, gx_ref, dmeta_ref, dg_ref):
        i = pl.program_id(0)
        dx, dg = _rms_bwd_core(dy_ref[...], h_ref[...], g_ref[...])
        dh = dres_ref[...] + dx
        gx_ref[...] = dh

        @pl.when(i == 0)
        def _():
            dmeta_ref[...] = dh[PAD_ROWS:BLOCK, :]
            dg_ref[...] = jnp.zeros_like(dg_ref)

        dg_ref[...] += dg

    return _pcall(
        body, name="rmsnorm_bwd_first", grid=(lp // BLOCK,),
        in_specs=[_row_spec(d), _row_spec(d), _const_spec((1, d)), _row_spec(d)],
        out_specs=[pl.BlockSpec((BLOCK, d), lambda i: (jnp.maximum(i - 1, 0), 0)), _const_spec((N_META, d)), _const_spec((1, d))],
        out_shape=[jax.ShapeDtypeStruct((s, d), F32), jax.ShapeDtypeStruct((N_META, d), F32), jax.ShapeDtypeStruct((1, d), F32)],
        compiler_params=_params(("arbitrary",)),
    )(dy, h, g, dres)


def _final(h2, tgt, g):
    lp, d = h2.shape

    def body(h_ref, t_ref, g_ref, dh_ref, dhb_ref, loss_ref, dg_ref):
        i = pl.program_id(0)
        x = h_ref[...]
        gg = g_ref[...]
        r = lax.rsqrt(jnp.mean(x * x, axis=-1, keepdims=True) + EPS)
        xhat = x * r
        y = xhat * gg
        live = (i > 0).astype(F32)
        err = (y - t_ref[...]) * live
        dy = err * (1.0 / d)
        dxhat = dy * gg
        dh = r * (dxhat - xhat * jnp.mean(dxhat * xhat, axis=-1, keepdims=True))
        dh_ref[...] = dh
        dhb_ref[...] = dh.astype(BF16)

        @pl.when(i == 0)
        def _():
            loss_ref[...] = jnp.zeros_like(loss_ref)
            dg_ref[...] = jnp.zeros_like(dg_ref)

        row_loss = jnp.mean(err * err, axis=-1, keepdims=True)
        loss_ref[...] += 0.5 * jnp.sum(row_loss, axis=0, keepdims=True)
        dg_ref[...] += jnp.sum(dy * xhat, axis=0, keepdims=True)

    return _pcall(
        body, name="final_norm_loss", grid=(lp // BLOCK,),
        in_specs=[_row_spec(d), pl.BlockSpec((BLOCK, d), lambda i: (jnp.maximum(i - 1, 0), 0)), _const_spec((1, d))],
        out_specs=[_row_spec(d), _row_spec(d), _const_spec((1, LANES)), _const_spec((1, d))],
        out_shape=[jax.ShapeDtypeStruct((lp, d), F32), jax.ShapeDtypeStruct((lp, d), BF16),
                   jax.ShapeDtypeStruct((1, LANES), F32), jax.ShapeDtypeStruct((1, d), F32)],
        compiler_params=_params(("arbitrary",)),
    )(h2, tgt, g)


def _swap_halves(x):
    w = x.shape[1]
    lane = lax.broadcasted_iota(jnp.int32, x.shape, 1)
    first = (lane & (HEAD_DIM - 1)) < (HEAD_DIM // 2)
    return jnp.where(first, pltpu.roll(x, w - HEAD_DIM // 2, 1), pltpu.roll(x, HEAD_DIM // 2, 1))


def _rope_tables(lp):
    pos = jnp.maximum(jnp.arange(lp, dtype=jnp.int32) - PAD_ROWS, 0).astype(F32)
    inv_freq = ROPE_THETA ** (-jnp.arange(0, HEAD_DIM, 2, dtype=F32) / HEAD_DIM)
    ang = pos[:, None] * inv_freq[None, :]
    c, s = jnp.cos(ang), jnp.sin(ang)
    reps = LANES // HEAD_DIM
    return jnp.tile(jnp.concatenate([c, c], axis=1), (1, reps)), jnp.tile(jnp.concatenate([-s, s], axis=1), (1, reps))


def _rope_fwd(zq, zkv, ctab, stab, after=None):
    lp = zq.shape[0]
    nb = lp // BLOCK
    back = lambda s: (jnp.maximum(s - 1, 0), 0)

    def body(zq_ref, zkv_ref, c_ref, s_ref, q_ref, k_ref, v_ref):
        step = pl.program_id(0)
        c128, s128 = c_ref[...], s_ref[...]

        def rope(x):
            reps = x.shape[1] // LANES
            return x * jnp.tile(c128, (1, reps)) + _swap_halves(x) * jnp.tile(s128, (1, reps))

        q_ref[...] = (rope(zq_ref[...].astype(F32)) * ATTN_SCALE).astype(BF16)
        kv = zkv_ref[...].astype(F32)
        k = rope(kv[:, :KV_DIM])
        v = kv[:, KV_DIM:]

        @pl.when(step == 0)
        def _():
            k_ref[...] = jnp.zeros_like(k_ref)
            v_ref[...] = jnp.zeros_like(v_ref)

        @pl.when(step > 0)
        def _():
            for h in range(N_KV_HEADS):
                k_ref[h] = k[:, h * HEAD_DIM:(h + 1) * HEAD_DIM].astype(BF16)
                v_ref[h] = v[:, h * HEAD_DIM:(h + 1) * HEAD_DIM].astype(BF16)

    kv_spec = pl.BlockSpec((N_KV_HEADS, BLOCK, HEAD_DIM), lambda s: (0, s, 0))
    return _pcall(
        body, after=after, name="rope_fwd", grid=(nb + 1,),
        in_specs=[pl.BlockSpec((BLOCK, Q_DIM), back), pl.BlockSpec((BLOCK, 2 * KV_DIM), back),
                  pl.BlockSpec((BLOCK, LANES), back), pl.BlockSpec((BLOCK, LANES), back)],
        out_specs=[pl.BlockSpec((BLOCK, Q_DIM), back), kv_spec, kv_spec],
        out_shape=[jax.ShapeDtypeStruct((lp, Q_DIM), BF16),
                   jax.ShapeDtypeStruct((N_KV_HEADS, lp + BLOCK, HEAD_DIM), BF16),
                   jax.ShapeDtypeStruct((N_KV_HEADS, lp + BLOCK, HEAD_DIM), BF16)],
        compiler_params=_params(("arbitrary",)),
    )(zq, zkv, ctab, stab)


def _rope_bwd(dq, dk, dv, dkm, dvm, ctab, stab):
    lp = dq.shape[0]
    width = Q_DIM + 2 * KV_DIM
    head_spec = pl.BlockSpec((N_KV_HEADS, BLOCK, HEAD_DIM), lambda i: (0, i, 0))
    meta_spec = _const_spec((N_KV_HEADS, BLOCK, HEAD_DIM))

    def body(dq_ref, dk_ref, dv_ref, dkm_ref, dvm_ref, c_ref, s_ref, dz_ref, sum_ref, kbuf, vbuf):
        i = pl.program_id(0)
        c128, s128 = c_ref[...], s_ref[...]
        first = (i == 0).astype(F32)

        def rope_t(x):
            reps = x.shape[1] // LANES
            return x * jnp.tile(c128, (1, reps)) + _swap_halves(x * jnp.tile(s128, (1, reps)))

        for h in range(N_KV_HEADS):
            kbuf[:, h * HEAD_DIM:(h + 1) * HEAD_DIM] = dk_ref[h] + first * dkm_ref[h]
            vbuf[:, h * HEAD_DIM:(h + 1) * HEAD_DIM] = dv_ref[h] + first * dvm_ref[h]
        dzq = rope_t(dq_ref[...] * ATTN_SCALE)
        dzk = rope_t(kbuf[...])
        dzv = vbuf[...]
        dz_ref[:, 0:Q_DIM] = dzq.astype(BF16)
        dz_ref[:, Q_DIM:Q_DIM + KV_DIM] = dzk.astype(BF16)
        dz_ref[:, Q_DIM + KV_DIM:width] = dzv.astype(BF16)

        @pl.when(i == 0)
        def _():
            sum_ref[...] = jnp.zeros_like(sum_ref)

        sum_ref[:, 0:Q_DIM] += jnp.sum(dzq, axis=0, keepdims=True)
        sum_ref[:, Q_DIM:Q_DIM + KV_DIM] += jnp.sum(dzk, axis=0, keepdims=True)
        sum_ref[:, Q_DIM + KV_DIM:width] += jnp.sum(dzv, axis=0, keepdims=True)

    return _pcall(
        body, name="rope_bwd", grid=(lp // BLOCK,),
        in_specs=[_row_spec(Q_DIM), head_spec, head_spec, meta_spec, meta_spec, _row_spec(LANES), _row_spec(LANES)],
        out_specs=[_row_spec(width), _const_spec((1, width))],
        out_shape=[jax.ShapeDtypeStruct((lp, width), BF16), jax.ShapeDtypeStruct((1, width), F32)],
        scratch_shapes=[pltpu.VMEM((BLOCK, KV_DIM), F32), pltpu.VMEM((BLOCK, KV_DIM), F32)],
        compiler_params=_params(("arbitrary",)),
    )(dq, dk, dv, dkm, dvm, ctab, stab)


def _attn_bias(i):
    r = lax.broadcasted_iota(jnp.int32, (BLOCK, 3 * BLOCK), 0)
    c = lax.broadcasted_iota(jnp.int32, (BLOCK, 3 * BLOCK), 1)
    qp = i * BLOCK + r - PAD_ROWS
    kp = (i - 1) * BLOCK + c - PAD_ROWS
    band = (c < 2 * BLOCK) & (kp >= N_META) & (kp <= qp) & (qp - kp < WINDOW)
    mp = c - 2 * BLOCK - PAD_ROWS
    meta = (c >= 2 * BLOCK) & (mp >= 0) & (mp <= qp)
    return jnp.where(band | meta, 0.0, NEG).astype(F32)


HALF = BLOCK // 2
HALF_KEYS = 2 * BLOCK


def _half_keys(prev, own, meta, half):
    if half == 0:
        return jnp.concatenate([prev, own[0:HALF], meta[HALF:BLOCK]], axis=0)
    return jnp.concatenate([prev[HALF:BLOCK], own, meta[HALF:BLOCK]], axis=0)


def _half_bias(i, half):
    r = lax.broadcasted_iota(jnp.int32, (HALF, HALF_KEYS), 0) + half * HALF
    c = lax.broadcasted_iota(jnp.int32, (HALF, HALF_KEYS), 1)
    n_prev = BLOCK - half * HALF
    qp = i * BLOCK + r - PAD_ROWS
    kp = jnp.where(c < n_prev, (i - 1) * BLOCK + c + half * HALF, i * BLOCK + c - n_prev) - PAD_ROWS
    band = (c < HALF_KEYS - HALF) & (kp >= N_META) & (kp <= qp) & (qp - kp < WINDOW)
    mp = c - (HALF_KEYS - HALF) + HALF - PAD_ROWS
    meta = (c >= HALF_KEYS - HALF) & (mp >= 0) & (mp <= qp)
    return jnp.where(band | meta, 0.0, NEG).astype(F32)


def _half_rows(ref, heads, half):
    rows = slice(half * HALF, (half + 1) * HALF)
    return jnp.concatenate([ref[rows, n * HEAD_DIM:(n + 1) * HEAD_DIM] for n in heads], axis=0)


def _half_sinks(sink_ref, heads):
    return jnp.concatenate([jnp.broadcast_to(sink_ref[0:1, n:n + 1], (HALF, 1)) for n in heads], axis=0)


def _stack_heads(ref, h):
    return jnp.concatenate(
        [ref[:, (h * GROUP + g) * HEAD_DIM:(h * GROUP + g + 1) * HEAD_DIM] for g in range(GROUP)], axis=0)


def _attn_probs(qs, k3, bias8, sink):
    s = lax.dot_general(qs, k3, (((1,), (1,)), ((), ())), preferred_element_type=F32) + bias8
    m = jnp.maximum(jnp.max(s, axis=1, keepdims=True), sink)
    p = jnp.exp(s - m)
    ps = jnp.exp(sink - m)
    inv = 1.0 / (jnp.sum(p, axis=1, keepdims=True) + ps)
    return p * inv, ps * inv


def _sink_column(sink_ref, h):
    return jnp.concatenate(
        [jnp.broadcast_to(sink_ref[0:1, h * GROUP + g:h * GROUP + g + 1], (BLOCK, 1)) for g in range(GROUP)], axis=0)


def _attn_fwd(q, k_sh, v_sh, sinks):
    lp = q.shape[0]
    nb = lp // BLOCK
    kv = lambda f: pl.BlockSpec((N_KV_HEADS, BLOCK, HEAD_DIM), f)

    def body(q_ref, kp_ref, kc_ref, km_ref, vp_ref, vc_ref, vm_ref, sink_ref, o_ref):
        i = pl.program_id(0)
        for half in range(2):
            bias = jnp.tile(_half_bias(i, half), (GROUP, 1))
            rows = slice(half * HALF, (half + 1) * HALF)
            for h in range(N_KV_HEADS):
                heads = range(h * GROUP, (h + 1) * GROUP)
                keys = _half_keys(kp_ref[h], kc_ref[h], km_ref[h], half)
                vals = _half_keys(vp_ref[h], vc_ref[h], vm_ref[h], half)
                p, _ = _attn_probs(_half_rows(q_ref, heads, half), keys, bias, _half_sinks(sink_ref, heads))
                o = jnp.dot(p.astype(BF16), vals, preferred_element_type=F32)
                for j, n in enumerate(heads):
                    o_ref[rows, n * HEAD_DIM:(n + 1) * HEAD_DIM] = o[j * HALF:(j + 1) * HALF].astype(BF16)

    prev, cur, meta = (lambda i: (0, i, 0)), (lambda i: (0, i + 1, 0)), (lambda i: (0, 1, 0))
    return _pcall(
        body, name="attn_fwd", grid=(nb,),
        in_specs=[_row_spec(Q_DIM), kv(prev), kv(cur), kv(meta), kv(prev), kv(cur), kv(meta), _const_spec((1, N_Q_HEADS))],
        out_specs=_row_spec(Q_DIM), out_shape=jax.ShapeDtypeStruct((lp, Q_DIM), BF16),
        compiler_params=_params(("parallel",)),
    )(q, k_sh, k_sh, k_sh, v_sh, v_sh, v_sh, sinks)


def _attn_bwd(q, k_sh, v_sh, sinks, do):
    lp = q.shape[0]
    nb = lp // BLOCK
    kv = lambda f: pl.BlockSpec((N_KV_HEADS, BLOCK, HEAD_DIM), f)
    cl = lambda s: jnp.minimum(s, nb - 1)

    def body(q_ref, do_ref, kp_ref, kc_ref, km_ref, vp_ref, vc_ref, vm_ref, sink_ref,
             dq_ref, dk_ref, dv_ref, dkm_ref, dvm_ref, dsink_ref, carry_k, carry_v):
        step = pl.program_id(0)

        @pl.when(step == 0)
        def _():
            carry_k[...] = jnp.zeros_like(carry_k)
            carry_v[...] = jnp.zeros_like(carry_v)
            dkm_ref[...] = jnp.zeros_like(dkm_ref)
            dvm_ref[...] = jnp.zeros_like(dvm_ref)
            dsink_ref[...] = jnp.zeros_like(dsink_ref)

        @pl.when(step < nb)
        def _():
            bias8 = jnp.tile(_attn_bias(step), (GROUP, 1))
            lane = lax.broadcasted_iota(jnp.int32, (1, LANES), 1)
            dsink = jnp.zeros((1, LANES), F32)
            for h in range(N_KV_HEADS):
                k3 = jnp.concatenate([kp_ref[h], kc_ref[h], km_ref[h]], axis=0)
                v3 = jnp.concatenate([vp_ref[h], vc_ref[h], vm_ref[h]], axis=0)
                qs = _stack_heads(q_ref, h)
                dos = _stack_heads(do_ref, h)
                p, psink = _attn_probs(qs, k3, bias8, _sink_column(sink_ref, h))
                dp = lax.dot_general(dos, v3, (((1,), (1,)), ((), ())), preferred_element_type=F32)
                delta = jnp.sum(p * dp, axis=1, keepdims=True)
                ds = (p * (dp - delta)).astype(BF16)
                dsk = -psink * delta
                for g in range(GROUP):
                    val = jnp.sum(dsk[g * BLOCK:(g + 1) * BLOCK], axis=0, keepdims=True)
                    dsink = dsink + jnp.where(lane == h * GROUP + g, val, 0.0)
                dqs = jnp.dot(ds, k3, preferred_element_type=F32)
                for g in range(GROUP):
                    n = h * GROUP + g
                    dq_ref[:, n * HEAD_DIM:(n + 1) * HEAD_DIM] = dqs[g * BLOCK:(g + 1) * BLOCK]
                dk3 = lax.dot_general(ds, qs, (((0,), (0,)), ((), ())), preferred_element_type=F32)
                dv3 = lax.dot_general(p.astype(BF16), dos, (((0,), (0,)), ((), ())), preferred_element_type=F32)
                dk_ref[h] = carry_k[h] + dk3[0:BLOCK]
                dv_ref[h] = carry_v[h] + dv3[0:BLOCK]
                carry_k[h] = dk3[BLOCK:2 * BLOCK]
                carry_v[h] = dv3[BLOCK:2 * BLOCK]
                dkm_ref[h] += dk3[2 * BLOCK:3 * BLOCK]
                dvm_ref[h] += dv3[2 * BLOCK:3 * BLOCK]
            dsink_ref[...] += dsink

        @pl.when(step == nb)
        def _():
            dk_ref[...] = carry_k[...]
            dv_ref[...] = carry_v[...]

    prev, cur, meta = (lambda s: (0, cl(s), 0)), (lambda s: (0, cl(s) + 1, 0)), (lambda s: (0, 1, 0))
    lag = lambda s: (0, jnp.maximum(s - 1, 0), 0)
    head_shape = jax.ShapeDtypeStruct((N_KV_HEADS, lp, HEAD_DIM), F32)
    meta_shape = jax.ShapeDtypeStruct((N_KV_HEADS, BLOCK, HEAD_DIM), F32)
    return _pcall(
        body, name="attn_bwd", grid=(nb + 1,),
        in_specs=[pl.BlockSpec((BLOCK, Q_DIM), lambda s: (cl(s), 0)), pl.BlockSpec((BLOCK, Q_DIM), lambda s: (cl(s), 0)),
                  kv(prev), kv(cur), kv(meta), kv(prev), kv(cur), kv(meta), _const_spec((1, N_Q_HEADS))],
        out_specs=[pl.BlockSpec((BLOCK, Q_DIM), lambda s: (cl(s), 0)), kv(lag), kv(lag),
                   _const_spec((N_KV_HEADS, BLOCK, HEAD_DIM)), _const_spec((N_KV_HEADS, BLOCK, HEAD_DIM)), _const_spec((1, LANES))],
        out_shape=[jax.ShapeDtypeStruct((lp, Q_DIM), F32), head_shape, head_shape, meta_shape, meta_shape,
                   jax.ShapeDtypeStruct((1, LANES), F32)],
        scratch_shapes=[pltpu.VMEM((N_KV_HEADS, BLOCK, HEAD_DIM), F32), pltpu.VMEM((N_KV_HEADS, BLOCK, HEAD_DIM), F32)],
        compiler_params=_params(("arbitrary",)),
    )(q, do, k_sh, k_sh, k_sh, v_sh, v_sh, v_sh, sinks)


CONV_CHUNK = 256


SUBLANES = 8
SH_BASE = BLOCK - 4 * SUBLANES
SH_ROWS = BLOCK + 3 * SUBLANES
DSH_ROWS = SH_ROWS


def _shifted_windows(src, sh, base, rows):
    for b in range(1, SUBLANES):
        sh[b] = src[base + b:base + b + rows, :]


def _window(src, sh, base, start, cols):
    a, b = divmod(start - base, SUBLANES)
    if b == 0:
        return src[start:start + BLOCK, cols]
    return sh[b, SUBLANES * a:SUBLANES * a + BLOCK, cols]


def _glu_masked(a_ref, g_ref, base):
    rows = base + lax.broadcasted_iota(jnp.int32, (BLOCK, 1), 0)
    return jnp.where(rows >= PAD_ROWS, a_ref[...].astype(F32) * _sigmoid(g_ref[...].astype(F32)), 0.0)


def _conv_fwd(zc, conv_w, conv_b, ln_g, ln_b, after=None):
    lp = zc.shape[0]
    cd = zc.shape[1] // 2
    nb = lp // BLOCK
    chunk = min(CONV_CHUNK, cd)
    back = lambda col: (lambda i: (jnp.maximum(i - 1, 0), col))
    lo = BLOCK - (CONV_WIDTH - 1)

    def body(ap_ref, gp_ref, ac_ref, gc_ref, w_ref, b_ref, lg_ref, lb_ref, co_ref, c2_ref, ext, sh):
        i = pl.program_id(0)
        ext[0:BLOCK, :] = _glu_masked(ap_ref, gp_ref, (i - 1) * BLOCK)
        ext[BLOCK:2 * BLOCK, :] = _glu_masked(ac_ref, gc_ref, i * BLOCK)
        _shifted_windows(ext, sh, SH_BASE, SH_ROWS)
        for c0 in range(0, cd, chunk):
            cols = slice(c0, c0 + chunk)
            acc = jnp.zeros((BLOCK, chunk), F32)
            for k in range(CONV_WIDTH):
                acc = acc + _window(ext, sh, SH_BASE, lo + k, cols) * w_ref[k:k + 1, cols]
            co_ref[:, cols] = acc + b_ref[:, cols]
        x = co_ref[...]
        mu = jnp.mean(x, axis=-1, keepdims=True)
        xc = x - mu
        r = lax.rsqrt(jnp.mean(xc * xc, axis=-1, keepdims=True) + EPS)
        y = xc * r * lg_ref[...] + lb_ref[...]
        c2_ref[...] = (y * _sigmoid(y)).astype(BF16)

    return _pcall(
        body, after=after, name="conv_fwd", grid=(nb,),
        in_specs=[pl.BlockSpec((BLOCK, cd), back(0)), pl.BlockSpec((BLOCK, cd), back(1)), _row_spec(cd, 0), _row_spec(cd, 1),
                  _const_spec((CONV_ROWS, cd)), _const_spec((1, cd)), _const_spec((1, cd)), _const_spec((1, cd))],
        out_specs=[_row_spec(cd), _row_spec(cd)],
        out_shape=[jax.ShapeDtypeStruct((lp, cd), F32), jax.ShapeDtypeStruct((lp, cd), BF16)],
        scratch_shapes=[pltpu.VMEM((2 * BLOCK, cd), F32), pltpu.VMEM((SUBLANES, SH_ROWS, cd), F32)],
        compiler_params=_params(("arbitrary",)),
    )(zc, zc, zc, zc, conv_w, conv_b, ln_g, ln_b)


def _conv_bwd_norm(dc2, conv_out, ln_g, ln_b):
    lp, cd = conv_out.shape

    def body(d_ref, x_ref, lg_ref, lb_ref, dco_ref, dlg_ref, dlb_ref, dcb_ref):
        i = pl.program_id(0)
        x = x_ref[...]
        g = lg_ref[...]
        mu = jnp.mean(x, axis=-1, keepdims=True)
        xc = x - mu
        r = lax.rsqrt(jnp.mean(xc * xc, axis=-1, keepdims=True) + EPS)
        xhat = xc * r
        y = xhat * g + lb_ref[...]
        sg = _sigmoid(y)
        dy = d_ref[...] * (sg * (1.0 + y * (1.0 - sg)))
        dxhat = dy * g
        dx = r * (dxhat - jnp.mean(dxhat, axis=-1, keepdims=True) - xhat * jnp.mean(dxhat * xhat, axis=-1, keepdims=True))
        dco_ref[...] = dx

        @pl.when(i == 0)
        def _():
            dlg_ref[...] = jnp.zeros_like(dlg_ref)
            dlb_ref[...] = jnp.zeros_like(dlb_ref)
            dcb_ref[...] = jnp.zeros_like(dcb_ref)

        dlg_ref[...] += jnp.sum(dy * xhat, axis=0, keepdims=True)
        dlb_ref[...] += jnp.sum(dy, axis=0, keepdims=True)
        dcb_ref[...] += jnp.sum(dx, axis=0, keepdims=True)

    vec = jax.ShapeDtypeStruct((1, cd), F32)
    return _pcall(
        body, name="conv_bwd_norm", grid=(lp // BLOCK,),
        in_specs=[_row_spec(cd), _row_spec(cd), _const_spec((1, cd)), _const_spec((1, cd))],
        out_specs=[_row_spec(cd), _const_spec((1, cd)), _const_spec((1, cd)), _const_spec((1, cd))],
        out_shape=[jax.ShapeDtypeStruct((lp, cd), F32), vec, vec, vec],
        compiler_params=_params(("arbitrary",)),
    )(dc2, conv_out, ln_g, ln_b)


def _conv_bwd_taps(dco, zc, conv_w):
    lp, cd = dco.shape
    nb = lp // BLOCK
    chunk = min(CONV_CHUNK, cd)
    back = lambda col: (lambda i: (jnp.maximum(i - 1, 0), col))
    fwd = lambda i: (jnp.minimum(i + 1, nb - 1), 0)
    lo = BLOCK - (CONV_WIDTH - 1)

    def body(dc_ref, dn_ref, ap_ref, gp_ref, ac_ref, gc_ref, w_ref, dz_ref, sum_ref, dw_ref, ext, dext, dcb, sh, dsh):
        i = pl.program_id(0)
        ext[0:BLOCK, :] = _glu_masked(ap_ref, gp_ref, (i - 1) * BLOCK)
        ext[BLOCK:2 * BLOCK, :] = _glu_masked(ac_ref, gc_ref, i * BLOCK)
        dext[0:BLOCK, :] = dc_ref[...]
        dext[BLOCK:2 * BLOCK, :] = dn_ref[...] * (i < nb - 1).astype(F32)
        _shifted_windows(ext, sh, SH_BASE, SH_ROWS)
        _shifted_windows(dext, dsh, 0, DSH_ROWS)

        @pl.when(i == 0)
        def _():
            dw_ref[...] = jnp.zeros_like(dw_ref)
            sum_ref[...] = jnp.zeros_like(sum_ref)

        for c0 in range(0, cd, chunk):
            cols = slice(c0, c0 + chunk)
            dcur = dext[0:BLOCK, cols]
            acc = jnp.zeros((BLOCK, chunk), F32)
            for k in range(CONV_WIDTH):
                s = CONV_WIDTH - 1 - k
                acc = acc + _window(dext, dsh, 0, s, cols) * w_ref[k:k + 1, cols]
                dw_ref[k:k + 1, cols] += jnp.sum(dcur * _window(ext, sh, SH_BASE, lo + k, cols), axis=0, keepdims=True)
            dcb[:, cols] = acc
        rows = i * BLOCK + lax.broadcasted_iota(jnp.int32, (BLOCK, 1), 0)
        dc = jnp.where(rows >= PAD_ROWS, dcb[...], 0.0)
        a = ac_ref[...].astype(F32)
        sg = _sigmoid(gc_ref[...].astype(F32))
        da = dc * sg
        dg = dc * a * sg * (1.0 - sg)
        dz_ref[:, 0:cd] = da.astype(BF16)
        dz_ref[:, cd:2 * cd] = dg.astype(BF16)
        sum_ref[:, 0:cd] += jnp.sum(da, axis=0, keepdims=True)
        sum_ref[:, cd:2 * cd] += jnp.sum(dg, axis=0, keepdims=True)

    return _pcall(
        body, name="conv_bwd_taps", grid=(nb,),
        in_specs=[_row_spec(cd), pl.BlockSpec((BLOCK, cd), fwd),
                  pl.BlockSpec((BLOCK, cd), back(0)), pl.BlockSpec((BLOCK, cd), back(1)), _row_spec(cd, 0), _row_spec(cd, 1),
                  _const_spec((CONV_ROWS, cd))],
        out_specs=[_row_spec(2 * cd), _const_spec((1, 2 * cd)), _const_spec((CONV_ROWS, cd))],
        out_shape=[jax.ShapeDtypeStruct((lp, 2 * cd), BF16), jax.ShapeDtypeStruct((1, 2 * cd), F32),
                   jax.ShapeDtypeStruct((CONV_ROWS, cd), F32)],
        scratch_shapes=[pltpu.VMEM((2 * BLOCK, cd), F32), pltpu.VMEM((2 * BLOCK, cd), F32), pltpu.VMEM((BLOCK, cd), F32),
                        pltpu.VMEM((SUBLANES, SH_ROWS, cd), F32), pltpu.VMEM((SUBLANES, DSH_ROWS, cd), F32)],
        compiler_params=_params(("arbitrary",)),
    )(dco, dco, zc, zc, zc, zc, conv_w)


def _conv_out_gate(c2, wco_t, bias, br_a, zg, *, tm, tn):
    m, k = c2.shape
    d = wco_t.shape[0]
    tm, tn = _pick(m, tm, 16), _pick(d, tn, LANES)
    nj = d // tn

    def body(a_ref, w_ref, bias_ref, bra_ref, ga_ref, gb_ref, brb_ref, m_ref):
        acc = lax.dot_general(a_ref[...], w_ref[...], (((1,), (1,)), ((), ())), preferred_element_type=F32)
        brb = (acc + bias_ref[...]).astype(BF16)
        brb_ref[...] = brb
        ga, gb = ga_ref[...].astype(F32), gb_ref[...].astype(F32)
        m_ref[...] = (_sigmoid(ga) * bra_ref[...].astype(F32) + _sigmoid(gb) * brb.astype(F32)).astype(BF16)

    tile = pl.BlockSpec((tm, tn), lambda i, j: (i, j))
    shape = jax.ShapeDtypeStruct((m, d), BF16)
    return _pcall(
        body, name="conv_out_proj_gate", grid=(m // tm, nj),
        in_specs=[pl.BlockSpec((tm, k), lambda i, j: (i, 0)), pl.BlockSpec((tn, k), lambda i, j: (j, 0)),
                  pl.BlockSpec((1, tn), lambda i, j: (0, j)), tile, tile, pl.BlockSpec((tm, tn), lambda i, j: (i, j + nj))],
        out_specs=[tile, tile], out_shape=[shape, shape], compiler_params=_params(("parallel", "arbitrary")),
    )(c2, wco_t, bias, br_a, zg, zg)


def _gate_bwd(dm, a, b, zg):
    lp, d = a.shape

    def body(dm_ref, a_ref, b_ref, ga_ref, gb_ref, da_ref, db_ref, dz_ref, sum_ref, dbias_ref):
        i = pl.program_id(0)
        dm_ = dm_ref[...].astype(F32)
        sa = _sigmoid(ga_ref[...].astype(F32))
        sb = _sigmoid(gb_ref[...].astype(F32))
        db = dm_ * sb
        dga = dm_ * a_ref[...].astype(F32) * sa * (1.0 - sa)
        dgb = dm_ * b_ref[...].astype(F32) * sb * (1.0 - sb)
        da_ref[...] = (dm_ * sa).astype(BF16)
        db_ref[...] = db.astype(BF16)
        dz_ref[:, 0:d] = dga.astype(BF16)
        dz_ref[:, d:2 * d] = dgb.astype(BF16)

        @pl.when(i == 0)
        def _():
            sum_ref[...] = jnp.zeros_like(sum_ref)
            dbias_ref[...] = jnp.zeros_like(dbias_ref)

        sum_ref[:, 0:d] += jnp.sum(dga, axis=0, keepdims=True)
        sum_ref[:, d:2 * d] += jnp.sum(dgb, axis=0, keepdims=True)
        dbias_ref[...] += jnp.sum(db, axis=0, keepdims=True)

    return _pcall(
        body, name="gate_bwd", grid=(lp // BLOCK,),
        in_specs=[_row_spec(d), _row_spec(d), _row_spec(d), _row_spec(d, 0), _row_spec(d, 1)],
        out_specs=[_row_spec(d), _row_spec(d), _row_spec(2 * d), _const_spec((1, 2 * d)), _const_spec((1, d))],
        out_shape=[jax.ShapeDtypeStruct((lp, d), BF16), jax.ShapeDtypeStruct((lp, d), BF16), jax.ShapeDtypeStruct((lp, 2 * d), BF16),
                   jax.ShapeDtypeStruct((1, 2 * d), F32), jax.ShapeDtypeStruct((1, d), F32)],
        compiler_params=_params(("arbitrary",)),
    )(dm, a, b, zg, zg)


def _gate_up_swiglu(u2, wgu_t, *, row_block, tm, tn, name, filled=None, after=None):
    m, k = u2.shape
    f = wgu_t.shape[0] // 2
    tn = _pick(f, tn, LANES)
    nj = f // tn
    dims = (((1,), (1,)), ((), ()))
    n_filled = 0 if filled is None else 3

    def body(*refs):
        a_ref, wg_ref, wu_ref = refs[:3]
        g_ref, u_ref, act_ref = refs[3 + n_filled:]
        a = a_ref[...]
        g = lax.dot_general(a, wg_ref[...], dims, preferred_element_type=F32).astype(BF16)
        up = lax.dot_general(a, wu_ref[...], dims, preferred_element_type=F32).astype(BF16)
        g_ref[...] = g
        u_ref[...] = up
        gf = g.astype(F32)
        act_ref[...] = (gf * _sigmoid(gf) * up.astype(F32)).astype(BF16)

    out = pl.BlockSpec((tm, tn), lambda j: (row_block, j))
    shape = jax.ShapeDtypeStruct((m, f), BF16)
    return _pcall(
        body, after=after, name=name, grid=(nj,),
        in_specs=[pl.BlockSpec((tm, k), lambda j: (row_block, 0)), pl.BlockSpec((tn, k), lambda j: (j, 0)),
                  pl.BlockSpec((tn, k), lambda j: (j + nj, 0))] + [pl.BlockSpec(memory_space=pl.ANY)] * n_filled,
        out_specs=[out, out, out], out_shape=[shape, shape, shape],
        input_output_aliases={3 + i: i for i in range(n_filled)}, compiler_params=_params(("arbitrary",)),
    )(u2, wgu_t, wgu_t, *(filled or ()))


def _down_dx_swiglu_bwd(dh, wdown, g, up, *, tm, tn, after=None):
    m, k = dh.shape
    f = wdown.shape[0]
    tm, tn = _pick(m, tm, 16), _pick(f, tn, LANES)

    def body(a_ref, w_ref, g_ref, u_ref, o_ref):
        acc = lax.dot_general(a_ref[...], w_ref[...], (((1,), (1,)), ((), ())), preferred_element_type=F32)
        d = acc.astype(BF16).astype(F32)
        gf = g_ref[...].astype(F32)
        sg = _sigmoid(gf)
        o_ref[0] = (d * u_ref[...].astype(F32) * (sg * (1.0 + gf * (1.0 - sg)))).astype(BF16)
        o_ref[1] = (d * gf * sg).astype(BF16)

    tile = pl.BlockSpec((tm, tn), lambda i, j: (i, j))
    return _pcall(
        body, after=after, name="ffn_down_dx_swiglu_bwd", grid=(m // tm, f // tn),
        in_specs=[pl.BlockSpec((tm, k), lambda i, j: (i, 0)), pl.BlockSpec((tn, k), lambda i, j: (j, 0)), tile, tile],
        out_specs=pl.BlockSpec((2, tm, tn), lambda i, j: (0, i, j)), out_shape=jax.ShapeDtypeStruct((2, m, f), BF16),
        compiler_params=_params(("parallel", "arbitrary")),
    )(dh, wdown, g, up)


ANY = pl.BlockSpec(memory_space=pl.ANY)


def _all_gather_rows(x, name, after=None):
    r, c = x.shape

    def body(x_ref, out_ref, send_sems, recv_sems, local_sem):
        mx, my, mc = lax.axis_index("x"), lax.axis_index("y"), lax.axis_index("c")
        me, sibling = (mx, my, mc), (mx, my, 1 - mc)
        chips = [(1 - mx, my), (mx, 1 - my), (1 - mx, 1 - my)]

        def rows(px, py, pc):
            return out_ref.at[pl.ds((4 * px + 2 * py + pc) * r, r), :]

        def copy(k, block, to, src=None):
            return pltpu.make_async_remote_copy(
                src_ref=rows(*block) if src is None else src, dst_ref=rows(*block),
                send_sem=send_sems.at[k], recv_sem=recv_sems.at[k], device_id=to, device_id_type=MESH)

        mine = pltpu.make_async_copy(x_ref, rows(*me), local_sem)
        mine.start()
        first = [copy(0, me, sibling, src=x_ref)]
        first += [copy(1 + j, me, (*chip, mc), src=x_ref) for j, chip in enumerate(chips)]
        for cp in first:
            cp.start()
        passed = [copy(4 + j, (*chip, mc), sibling) for j, chip in enumerate(chips)]
        for j, chip in enumerate(chips):
            copy(1 + j, (*chip, mc), me).wait_recv()
            passed[j].start()
        copy(0, sibling, me).wait_recv()
        for j, chip in enumerate(chips):
            copy(4 + j, (*chip, 1 - mc), me).wait_recv()
        for cp in first + passed:
            cp.wait_send()
        mine.wait()

    return _pcall(
        body, after=after, name=name, in_specs=[ANY], out_specs=ANY, out_shape=jax.ShapeDtypeStruct((N_DEV * r, c), x.dtype),
        scratch_shapes=[pltpu.SemaphoreType.DMA((7,)), pltpu.SemaphoreType.DMA((7,)), pltpu.SemaphoreType.DMA(())],
    )(x)


HBM = pl.BlockSpec(memory_space=pltpu.HBM)
SEM = pl.BlockSpec(memory_space=pltpu.SEMAPHORE)
IN_FLIGHT = pltpu.CompilerParams(has_side_effects=pltpu.SideEffectType.DATAFLOW_SIDE_EFFECTING)


def _place_rows(shard, after, name):
    r, c = shard.shape
    tr = _pick(r, max(16, ELEMENTWISE_BLOCK_BYTES // (4 * c)), 16)
    steps = r // tr
    dev = (4 * lax.axis_index("x") + 2 * lax.axis_index("y") + lax.axis_index("c")).astype(jnp.int32).reshape(1)

    def body(dev_ref, x_ref, after_ref, o_ref):
        o_ref[...] = x_ref[...].astype(BF16)

    return _pcall(
        body, name=name,
        grid_spec=pltpu.PrefetchScalarGridSpec(
            num_scalar_prefetch=1, grid=(steps,),
            in_specs=[pl.BlockSpec((tr, c), lambda i, dev_ref: (i, 0)), pl.BlockSpec(memory_space=pl.ANY)],
            out_specs=pl.BlockSpec((tr, c), lambda i, dev_ref: (dev_ref[0] * steps + i, 0))),
        out_shape=jax.ShapeDtypeStruct((N_DEV * r, c), BF16), compiler_params=_params(("parallel",)),
    )(dev, shard, after)


def _rows_start(full, plan, name, after=None):
    r = full.shape[0] // N_DEV
    n = len(plan(0, 0, 0))

    ordered = after is not None

    def body(*refs):
        full_ref, (send_sems, recv_sems) = refs[0], refs[1 + ordered:3 + ordered]
        mx, my, mc = lax.axis_index("x"), lax.axis_index("y"), lax.axis_index("c")
        for k, ((bx, by, bc), target) in enumerate(plan(mx, my, mc)):
            rows = full_ref.at[pl.ds((4 * bx + 2 * by + bc) * r, r), :]
            pltpu.make_async_remote_copy(
                src_ref=rows, dst_ref=rows, send_sem=send_sems.at[k], recv_sem=recv_sems.at[k],
                device_id=target, device_id_type=MESH).start()

    return pl.pallas_call(
        body, name=name, in_specs=[HBM] + [pl.BlockSpec(memory_space=pl.ANY)] * ordered, out_specs=(SEM, SEM, HBM),
        out_shape=(pltpu.SemaphoreType.DMA((n,)), pltpu.SemaphoreType.DMA((n,)), pltpu.HBM(full.shape, full.dtype)),
        input_output_aliases={0: 2}, compiler_params=IN_FLIGHT,
    )(pltpu.with_memory_space_constraint(full, pltpu.HBM), *([after] if ordered else []))


def _rows_wait(started, after, name):
    send_sem, recv_sem, full = started
    r = full.shape[0] // N_DEV
    n = send_sem.shape[0]

    def body(full_ref, send_ref, recv_ref, after_ref, out_ref):
        mx, my, mc = lax.axis_index("x"), lax.axis_index("y"), lax.axis_index("c")
        block = full_ref.at[pl.ds(0, r), :]
        for k in range(n):
            cp = pltpu.make_async_remote_copy(
                src_ref=block, dst_ref=block, send_sem=send_ref.at[k], recv_sem=recv_ref.at[k],
                device_id=(mx, my, mc), device_id_type=MESH)
            cp.wait_send()
            cp.wait_recv()

    return pl.pallas_call(
        body, name=name, in_specs=[HBM, SEM, SEM, pl.BlockSpec(memory_space=pl.ANY)], out_specs=HBM,
        out_shape=pltpu.HBM(full.shape, full.dtype), input_output_aliases={0: 0}, compiler_params=IN_FLIGHT,
    )(full, send_sem, recv_sem, after)


def _plan_direct(mx, my, mc):
    me = (mx, my, mc)
    return [(me, (mx, my, 1 - mc)), (me, (1 - mx, my, mc)), (me, (mx, 1 - my, mc)), (me, (1 - mx, 1 - my, mc))]


def _plan_pass_on(mx, my, mc):
    sibling = (mx, my, 1 - mc)
    return [((1 - mx, my, mc), sibling), ((mx, 1 - my, mc), sibling), ((1 - mx, 1 - my, mc), sibling)]


def _plan_neighbours(mx, my, mc):
    me = (mx, my, mc)
    return [(me, (mx, my, 1 - mc)), (me, (1 - mx, my, mc)), (me, (mx, 1 - my, mc))]


def _plan_relay(mx, my, mc):
    sibling = (mx, my, 1 - mc)
    source = ((mx + 1 - mc) % 2, (my + mc) % 2, mc)
    target = ((mx + mc) % 2, (my + 1 - mc) % 2, mc)
    return [((1 - mx, my, mc), sibling), ((mx, 1 - my, mc), sibling), (source, target)]


def _plan_pass_on_diagonal(mx, my, mc):
    return [((1 - mx, 1 - my, mc), (mx, my, 1 - mc))]


def _pair_exchange_start(g, name):
    r = g.shape[0] // N_DEV
    c = g.shape[1]
    land = (len(CHIPS), r, c)

    def body(g_ref, land_ref, send_sems, recv_sems, g_out, land_out):
        mx, my, mc = lax.axis_index("x"), lax.axis_index("y"), lax.axis_index("c")
        for j, (px, py) in enumerate(CHIPS):
            pltpu.make_async_remote_copy(
                src_ref=g_ref.at[pl.ds((4 * px + 2 * py + 1 - mc) * r, r), :], dst_ref=land_ref.at[j],
                send_sem=send_sems.at[j], recv_sem=recv_sems.at[j], device_id=(mx, my, 1 - mc), device_id_type=MESH).start()

    return pl.pallas_call(
        body, name=name, in_specs=[HBM, HBM], out_specs=(SEM, SEM, HBM, HBM),
        out_shape=(pltpu.SemaphoreType.DMA((4,)), pltpu.SemaphoreType.DMA((4,)), pltpu.HBM(g.shape, g.dtype), pltpu.HBM(land, g.dtype)),
        input_output_aliases={0: 2, 1: 3}, compiler_params=IN_FLIGHT,
    )(pltpu.with_memory_space_constraint(g, pltpu.HBM), pltpu.with_memory_space_constraint(lax.empty(land, g.dtype), pltpu.HBM))


def _pair_exchange_wait(send_sem, recv_sem, g, land, after, name):
    def body(g_ref, land_ref, send_ref, recv_ref, after_ref, g_out, land_out):
        mx, my, mc = lax.axis_index("x"), lax.axis_index("y"), lax.axis_index("c")
        for j in range(len(CHIPS)):
            cp = pltpu.make_async_remote_copy(
                src_ref=land_ref.at[0], dst_ref=land_ref.at[0], send_sem=send_ref.at[j], recv_sem=recv_ref.at[j],
                device_id=(mx, my, mc), device_id_type=MESH)
            cp.wait_send()
            cp.wait_recv()

    return pl.pallas_call(
        body, name=name, in_specs=[HBM, HBM, SEM, SEM, pl.BlockSpec(memory_space=pl.ANY)], out_specs=(HBM, HBM),
        out_shape=(pltpu.HBM(g.shape, g.dtype), pltpu.HBM(land.shape, land.dtype)), input_output_aliases={0: 0, 1: 1},
        compiler_params=IN_FLIGHT,
    )(g, land, send_sem, recv_sem, after)


def _chip_exchange_start(ps, after, name):
    def body(ps_ref, rx_ref, after_ref, send_sems, recv_sems, ps_out, rx_out):
        mx, my, mc = lax.axis_index("x"), lax.axis_index("y"), lax.axis_index("c")
        chips = [(1 - mx, my), (mx, 1 - my), (1 - mx, 1 - my)]
        for k, (px, py) in enumerate(chips):
            pltpu.make_async_remote_copy(
                src_ref=ps_ref.at[2 * px + py], dst_ref=rx_ref.at[2 * mx + my], send_sem=send_sems.at[k], recv_sem=recv_sems.at[k],
                device_id=(px, py, mc), device_id_type=MESH).start()

    return pl.pallas_call(
        body, name=name, in_specs=[HBM, HBM, pl.BlockSpec(memory_space=pl.ANY)], out_specs=(SEM, SEM, HBM, HBM),
        out_shape=(pltpu.SemaphoreType.DMA((3,)), pltpu.SemaphoreType.DMA((3,)), pltpu.HBM(ps.shape, ps.dtype), pltpu.HBM(ps.shape, ps.dtype)),
        input_output_aliases={0: 2, 1: 3}, compiler_params=IN_FLIGHT,
    )(pltpu.with_memory_space_constraint(ps, pltpu.HBM), pltpu.with_memory_space_constraint(lax.empty(ps.shape, ps.dtype), pltpu.HBM), after)


def _chip_exchange_wait(send_sem, recv_sem, ps, rx, after, name):
    def body(ps_ref, rx_ref, send_ref, recv_ref, after_ref, ps_out, rx_out):
        mx, my, mc = lax.axis_index("x"), lax.axis_index("y"), lax.axis_index("c")
        for k in range(3):
            cp = pltpu.make_async_remote_copy(
                src_ref=ps_ref.at[0], dst_ref=rx_ref.at[0], send_sem=send_ref.at[k], recv_sem=recv_ref.at[k],
                device_id=(mx, my, mc), device_id_type=MESH)
            cp.wait_send()
            cp.wait_recv()

    return pl.pallas_call(
        body, name=name, in_specs=[HBM, HBM, SEM, SEM, pl.BlockSpec(memory_space=pl.ANY)], out_specs=(HBM, HBM),
        out_shape=(pltpu.HBM(ps.shape, ps.dtype), pltpu.HBM(rx.shape, rx.dtype)), input_output_aliases={0: 0, 1: 1},
        compiler_params=IN_FLIGHT,
    )(ps, rx, send_sem, recv_sem, after)


def _sum_chips(ps, rx, name):
    n, r, c = rx.shape
    tr = _pick(r, max(16, 4 * ELEMENTWISE_BLOCK_BYTES // (4 * n * c)), 16)
    chip =(2 * lax.axis_index("x") + lax.axis_index("y")).astype(jnp.int32).reshape(1)

    def body(chip_ref, own_ref, x_ref, o_ref):
        me = chip_ref[0]
        own = own_ref[0].astype(F32)
        acc = jnp.where(me == 0, own, x_ref[0].astype(F32))
        for j in range(1, n):
            acc = acc + jnp.where(me == j, own, x_ref[j].astype(F32))
        o_ref[...] = acc

    return _pcall(
        body, name=name,
        grid_spec=pltpu.PrefetchScalarGridSpec(
            num_scalar_prefetch=1, grid=(r // tr,),
            in_specs=[pl.BlockSpec((1, tr, c), lambda i, chip_ref: (chip_ref[0], i, 0)), pl.BlockSpec((n, tr, c), lambda i, chip_ref: (0, i, 0))],
            out_specs=pl.BlockSpec((tr, c), lambda i, chip_ref: (i, 0))),
        out_shape=jax.ShapeDtypeStruct((r, c), F32), compiler_params=_params(("parallel",)),
    )(chip, ps, rx)


def _pair_exchange(g, name):
    r = g.shape[0] // N_DEV
    c = g.shape[1]

    def body(g_ref, theirs_ref, send_sems, recv_sems):
        mx, my, mc = lax.axis_index("x"), lax.axis_index("y"), lax.axis_index("c")
        sibling = (mx, my, 1 - mc)
        copies = []
        for j, (px, py) in enumerate(CHIPS):
            give = g_ref.at[pl.ds((4 * px + 2 * py + 1 - mc) * r, r), :]
            rc = pltpu.make_async_remote_copy(
                src_ref=give, dst_ref=theirs_ref.at[j], send_sem=send_sems.at[j], recv_sem=recv_sems.at[j],
                device_id=sibling, device_id_type=MESH)
            rc.start()
            copies.append(rc)
        for cp in copies:
            cp.wait()

    return _pcall(
        body, name=name, in_specs=[ANY], out_specs=ANY, out_shape=jax.ShapeDtypeStruct((len(CHIPS), r, c), g.dtype),
        scratch_shapes=[pltpu.SemaphoreType.DMA((4,)), pltpu.SemaphoreType.DMA((4,))],
    )(g)


def _pair_sum(g, theirs, name):
    nch, r, c = theirs.shape
    tr = _pick(r, max(16, 3 * ELEMENTWISE_BLOCK_BYTES // (2 * c)), 16)
    core = lax.axis_index("c").astype(jnp.int32).reshape(1)

    def body(core_ref, a_ref, b_ref, o_ref):
        o_ref[...] = (a_ref[...].astype(F32) + b_ref[...].astype(F32)).astype(o_ref.dtype)

    spec = pl.BlockSpec((1, tr, c), lambda j, i, core_ref: (j, i, 0))
    own = pl.BlockSpec((1, tr, c), lambda j, i, core_ref: (2 * j + core_ref[0], i, 0))
    return _pcall(
        body, name=name,
        grid_spec=pltpu.PrefetchScalarGridSpec(num_scalar_prefetch=1, grid=(nch, r // tr), in_specs=[own, spec], out_specs=spec),
        out_shape=jax.ShapeDtypeStruct(theirs.shape, theirs.dtype), compiler_params=_params(("parallel", "parallel")),
    )(core, g.reshape(N_DEV, r, c), theirs)


def _sum_blocks(rx, name):
    n, r, c = rx.shape
    tr = _pick(r, max(8, ELEMENTWISE_BLOCK_BYTES // (4 * n * c)), 8)

    def body(x_ref, o_ref):
        acc = x_ref[0].astype(F32)
        for j in range(1, n):
            acc = acc + x_ref[j].astype(F32)
        o_ref[...] = acc

    return _pcall(
        body, name=name, grid=(r // tr,), in_specs=[pl.BlockSpec((n, tr, c), lambda i: (0, i, 0))],
        out_specs=pl.BlockSpec((tr, c), lambda i: (i, 0)), out_shape=jax.ShapeDtypeStruct((r, c), F32),
        compiler_params=_params(("parallel",)),
    )(rx)


def _adamw(w, g, m, v, name):
    r, c = w.shape
    tr = _pick(r, max(8, ELEMENTWISE_BLOCK_BYTES // (4 * c)), 8)

    def body(w_ref, g_ref, m_ref, v_ref, d_ref, nm_ref, nv_ref):
        d_ref[...], nm_ref[...], nv_ref[...] = _adam_update(w_ref[...], g_ref[...], m_ref[...], v_ref[...])

    spec = pl.BlockSpec((tr, c), lambda i: (i, 0))
    shp = jax.ShapeDtypeStruct((r, c), F32)
    return _pcall(
        body, name=name, grid=(r // tr,), in_specs=[spec] * 4, out_specs=[spec] * 3, out_shape=[shp] * 3,
        compiler_params=_params(("parallel",)),
    )(w, g, m, v)


def _adam_update(w, g, m, v):
    nm = ADAM_B1 * m + (1.0 - ADAM_B1) * g
    nv = ADAM_B2 * v + (1.0 - ADAM_B2) * (g * g)
    delta = -ADAM_LR * ((nm / (1.0 - ADAM_B1 ** ADAM_STEP)) / (jnp.sqrt(nv / (1.0 - ADAM_B2 ** ADAM_STEP)) + ADAM_EPS) + ADAM_WD * w)
    return delta, nm, nv


def _sum_chips_adamw(ps, rx, w, m, v, name):
    n, r, c = rx.shape
    tr = _pick(r, max(16, 4 * ELEMENTWISE_BLOCK_BYTES // (4 * n * c)), 16)
    chip = (2 * lax.axis_index("x") + lax.axis_index("y")).astype(jnp.int32).reshape(1)

    def body(chip_ref, own_ref, x_ref, w_ref, m_ref, v_ref, g_ref, d_ref, nm_ref, nv_ref):
        me = chip_ref[0]
        own = own_ref[0].astype(F32)
        g = jnp.where(me == 0, own, x_ref[0].astype(F32))
        for j in range(1, n):
            g = g + jnp.where(me == j, own, x_ref[j].astype(F32))
        g_ref[...] = g
        d_ref[...], nm_ref[...], nv_ref[...] = _adam_update(w_ref[...], g, m_ref[...], v_ref[...])

    rows = pl.BlockSpec((tr, c), lambda i, chip_ref: (i, 0))
    shp = jax.ShapeDtypeStruct((r, c), F32)
    return _pcall(
        body, name=name,
        grid_spec=pltpu.PrefetchScalarGridSpec(
            num_scalar_prefetch=1, grid=(r // tr,),
            in_specs=[pl.BlockSpec((1, tr, c), lambda i, chip_ref: (chip_ref[0], i, 0)),
                      pl.BlockSpec((n, tr, c), lambda i, chip_ref: (0, i, 0)), rows, rows, rows],
            out_specs=[rows] * 4),
        out_shape=[shp] * 4, compiler_params=_params(("parallel",)),
    )(chip, ps, rx, w, m, v)


def _pack(parts):
    flat, layout, row = [], [], 0
    for p in parts:
        n = p.size
        rows = -(-n // LANES)
        flat.append(jnp.pad(p.reshape(-1).astype(F32), (0, rows * LANES - n)))
        layout.append((row, n, p.shape))
        row += rows
    total = -(-row // 8) * 8
    if total > row:
        flat.append(jnp.zeros(((total - row) * LANES,), F32))
    return jnp.concatenate(flat).reshape(total, LANES), layout


def _unpack(slab, layout):
    flat = slab.reshape(-1)
    return [flat[row * LANES:row * LANES + n].reshape(shape) for row, n, shape in layout]


def kernel(x, meta_tokens, mix_norm_g, w_in, b_in, attn_sinks, conv_w, conv_b, conv_ln_g, conv_ln_b, w_attn_o, w_conv_o, b_conv_o, w_out, ffn_norm_g, w_gate_up, w_down, final_norm_g, loss_target, m_meta_tokens, m_mix_norm_g, m_w_in, m_b_in, m_attn_sinks, m_conv_w, m_conv_b, m_conv_ln_g, m_conv_ln_b, m_w_attn_o, m_w_conv_o, m_b_conv_o, m_w_out, m_ffn_norm_g, m_w_gate_up, m_w_down, m_final_norm_g, v_meta_tokens, v_mix_norm_g, v_w_in, v_b_in, v_attn_sinks, v_conv_w, v_conv_b, v_conv_ln_g, v_conv_ln_b, v_w_attn_o, v_w_conv_o, v_b_conv_o, v_w_out, v_ffn_norm_g, v_w_gate_up, v_w_down, v_final_norm_g):
    xs = x[0]
    tgt = loss_target[0]
    s, d = xs.shape
    lp = s + BLOCK
    cd = conv_b.shape[1]
    ffn = w_down.shape[1] * N_DEV
    dev = 4 * lax.axis_index("x") + 2 * lax.axis_index("y") + lax.axis_index("c")
    cw_cols = conv_w.shape[3]
    meta_cols = meta_tokens.shape[1]

    small, small_layout = _pack([meta_tokens, jnp.pad(conv_w[0, :, 0, :], ((0, CONV_ROWS - CONV_WIDTH), (0, 0)))])
    small_flat = _all_gather_rows(small, "gather_small")
    small_all = small_flat.reshape(N_DEV, *small.shape)
    meta_parts, cw_parts = zip(*[_unpack(small_all[j], small_layout) for j in range(N_DEV)])
    meta_full = jnp.concatenate(meta_parts, axis=1)
    conv_w_full = jnp.concatenate(cw_parts, axis=1)
    shards = ((w_in[0].T, "w_in"), (w_attn_o[0].T, "w_attn_o"), (w_conv_o[0].T, "w_conv_o"), (w_out[0], "w_out"),
              (w_gate_up[0].T, "w_gate_up"), (w_down[0], "w_down"))
    first = _rows_start(_place_rows(shards[0][0], small_flat, "place_w_in"), _plan_neighbours, "gather_start_w_in")
    placed, tok = [], first[2]
    for shard, name in shards[1:]:
        tok = _place_rows(shard, tok, "place_" + name)
        placed.append(tok)
    relay = _rows_start(_rows_wait(first, tok, "gather_wait_w_in"), _plan_relay, "gather_relay_start_w_in")
    h0, u = _prep(xs, meta_full, mix_norm_g, after=relay[2])
    diagonal = _rows_start(_rows_wait(relay, u, "gather_relay_wait_w_in"), _plan_pass_on_diagonal, "gather_diagonal_start_w_in")
    started, tok = [None], diagonal[2]
    for full, (_, name) in zip(placed, shards[1:]):
        started.append(_rows_start(full, _plan_direct, "gather_start_" + name, after=tok))
        tok = started[-1][2]
    win_t = _rows_wait(diagonal, tok, "gather_diagonal_wait_w_in")

    def arrived(w, after, name):
        return _rows_start(_rows_wait(started[w], after, "gather_wait_" + name), _plan_pass_on, "gather_pass_on_start_" + name)

    def whole(passing, after, name):
        return _rows_wait(passing, after, "gather_pass_on_wait_" + name)

    ctab, stab = _rope_tables(lp)
    mm = functools.partial(_matmul, tm=1056, tn=1024)

    bq, bkv, bc, bg = b_in[:, :Q_DIM], b_in[:, Q_DIM:Q_DIM + 2 * KV_DIM], b_in[:, Q_DIM + 2 * KV_DIM:Q_DIM + 2 * KV_DIM + 2 * cd], b_in[:, Q_DIM + 2 * KV_DIM + 2 * cd:]
    o_kv, o_c, o_g = Q_DIM, Q_DIM + 2 * KV_DIM, Q_DIM + 2 * KV_DIM + 2 * cd
    in_proj = functools.partial(_matmul, u, win_t, mode="nt", out_dtype=BF16, tm=2112, tn=512, tk=d)
    zq = in_proj(name="in_proj_q", bias=bq, b_row_off=0, b_rows=Q_DIM)
    zkv = in_proj(name="in_proj_kv", bias=bkv, b_row_off=o_kv, b_rows=2 * KV_DIM)
    zc = in_proj(name="in_proj_conv", bias=bc, b_row_off=o_c, b_rows=2 * cd)
    zg = in_proj(name="in_proj_gates", bias=bg, b_row_off=o_g, b_rows=2 * d)
    passing = arrived(1, zg, "w_attn_o")
    q_rot, k_sh, v_sh = _rope_fwd(zq, zkv, ctab, stab, after=passing[2])
    o = _attn_fwd(q_rot, k_sh, v_sh, attn_sinks)
    wao_t = whole(passing, o, "w_attn_o")
    br_a = mm(o, wao_t, mode="nt", name="attn_out_proj", out_dtype=BF16, tk=Q_DIM)
    passing = arrived(2, br_a, "w_conv_o")
    passing_out = arrived(3, passing[2], "w_out")
    conv_out, c2 = _conv_fwd(zc, conv_w_full, conv_b, conv_ln_g, conv_ln_b, after=passing_out[2])
    wco_t = whole(passing, c2, "w_conv_o")
    br_b, merged = _conv_out_gate(c2, wco_t, b_conv_o, br_a, zg, tm=1056, tn=1024)
    wout = whole(passing_out, merged, "w_out")
    passing = arrived(4, wout, "w_gate_up")
    h1 = mm(merged, wout, mode="nn", name="mix_out_proj", out_dtype=F32, tn=512, tk=d, residual=h0, after=passing[2])
    u2 = _rmsnorm_fwd(h1, ffn_norm_g, "ffn_rmsnorm")
    wgu_t = whole(passing, u2, "w_gate_up")
    half_rows = _pick(lp, lp // 2, 16)
    ffn_in = _gate_up_swiglu(u2, wgu_t, row_block=0, tm=half_rows, tn=256, name="ffn_gate_up_swiglu_0")
    passing = arrived(5, ffn_in[2], "w_down")
    for rb in range(1, lp // half_rows):
        ffn_in = _gate_up_swiglu(u2, wgu_t, row_block=rb, tm=half_rows, tn=256, name="ffn_gate_up_swiglu_%d" % rb,
                                 filled=ffn_in, after=passing[2])
    gu_g, gu_u, act = ffn_in
    wdown = whole(passing, act, "w_down")
    h2 = mm(act, wdown, mode="nn", name="ffn_down", out_dtype=F32, tm=528, tn=512, tk=ffn, residual=h1)
    dh2, dh2_b, loss_part, d_final_g = _final(h2, tgt, final_norm_g.reshape(1, d))

    wgrad = functools.partial(_matmul, mode="tn", out_dtype=BF16, tk=lp, tn=2048, b_inner=False)
    in_flight = {}

    def scatter_begin(g, name):
        return _pair_exchange_start(g, "rs_" + name + "_pair_start")

    def scatter_go_on(pair, after, name):
        g, theirs = _pair_exchange_wait(pair[0], pair[1], pair[2], pair[3], after, "rs_" + name + "_pair_wait")
        ps = _pair_sum(g, theirs, "rs_" + name + "_pair_sum")
        in_flight[name] = _chip_exchange_start(ps, theirs, "rs_" + name + "_chip_start")
        return in_flight[name][2]

    g_wdown = wgrad(act, dh2_b, name="ffn_down_dw", tm=256)
    pair = scatter_begin(g_wdown, "w_down")
    dgu = _down_dx_swiglu_bwd(dh2_b, wdown, gu_g, gu_u, tm=1056, tn=256, after=pair[2])
    tok = scatter_go_on(pair, dgu, "w_down")
    g_wgu_t = wgrad(dgu, u2, name="ffn_gate_up_dw", tm=256, after=tok)
    pair = scatter_begin(g_wgu_t, "w_gate_up")
    du2 = mm(dgu, wgu_t, mode="nn", name="ffn_gate_up_dx", out_dtype=F32, tn=512, tk=ffn // 2, after=pair[2])
    tok = scatter_go_on(pair, du2, "w_gate_up")
    dh1, dh1_b, d_ffn_g = _rmsnorm_bwd(du2, h1, ffn_norm_g, dh2, "ffn_rmsnorm_bwd")
    g_wout = wgrad(merged, dh1_b, name="mix_out_dw", tm=512, after=tok)
    pair = scatter_begin(g_wout, "w_out")
    dmerged = mm(dh1_b, wout, mode="nt", name="mix_out_dx", out_dtype=BF16, tm=2112, tn=512, tk=d, after=pair[2])
    tok = scatter_go_on(pair, dmerged, "w_out")
    d_a, d_b, dz_g, sum_g, d_bco = _gate_bwd(dmerged, br_a, br_b, zg)
    g_wao_t = wgrad(d_a, o, name="attn_out_dw", tm=512, after=tok)
    pair = scatter_begin(g_wao_t, "w_attn_o")
    do = mm(d_a, wao_t, mode="nn", name="attn_out_dx", out_dtype=BF16, tk=d, after=pair[2])
    tok = scatter_go_on(pair, do, "w_attn_o")
    g_wco_t = wgrad(d_b, c2, name="conv_out_dw", tm=512, after=tok)
    pair = scatter_begin(g_wco_t, "w_conv_o")
    dc2 = mm(d_b, wco_t, mode="nn", name="conv_out_dx", out_dtype=F32, tk=d, after=pair[2])
    tok = scatter_go_on(pair, dc2, "w_conv_o")
    dq, dk, dv, dkm, dvm, d_sinks = _attn_bwd(q_rot, k_sh, v_sh, attn_sinks, do)
    dz_qkv, sum_qkv = _rope_bwd(dq, dk, dv, dkm, dvm, ctab, stab)
    dco, d_ln_g, d_ln_b, d_conv_b = _conv_bwd_norm(dc2, conv_out, conv_ln_g, conv_ln_b)
    dz_c, sum_c, d_conv_w = _conv_bwd_taps(dco, zc, conv_w_full)
    dz = jnp.concatenate([dz_qkv, dz_c, dz_g], axis=1)
    d_b_in = jnp.concatenate([sum_qkv, sum_c, sum_g], axis=1)
    in_dim = dz.shape[1]
    g_win_t = wgrad(dz, u, name="in_proj_dw", tm=512, after=tok)
    pair = scatter_begin(g_win_t, "w_in")
    du = mm(dz, win_t, mode="nn", name="in_proj_dx", out_dtype=F32, tk=in_dim // 4, after=pair[2])
    scatter_go_on(pair, du, "w_in")
    grad_x, d_meta, d_mix_g = _rmsnorm_bwd_first(du, h0, mix_norm_g, dh1)

    weights = dict(meta_tokens=meta_tokens, mix_norm_g=mix_norm_g, w_in=w_in, b_in=b_in, attn_sinks=attn_sinks, conv_w=conv_w,
                   conv_b=conv_b, conv_ln_g=conv_ln_g, conv_ln_b=conv_ln_b, w_attn_o=w_attn_o, w_conv_o=w_conv_o, b_conv_o=b_conv_o,
                   w_out=w_out, ffn_norm_g=ffn_norm_g, w_gate_up=w_gate_up, w_down=w_down, final_norm_g=final_norm_g)
    m_in = dict(meta_tokens=m_meta_tokens, mix_norm_g=m_mix_norm_g, w_in=m_w_in, b_in=m_b_in, attn_sinks=m_attn_sinks, conv_w=m_conv_w,
                conv_b=m_conv_b, conv_ln_g=m_conv_ln_g, conv_ln_b=m_conv_ln_b, w_attn_o=m_w_attn_o, w_conv_o=m_w_conv_o,
                b_conv_o=m_b_conv_o, w_out=m_w_out, ffn_norm_g=m_ffn_norm_g, w_gate_up=m_w_gate_up, w_down=m_w_down,
                final_norm_g=m_final_norm_g)
    v_in = dict(meta_tokens=v_meta_tokens, mix_norm_g=v_mix_norm_g, w_in=v_w_in, b_in=v_b_in, attn_sinks=v_attn_sinks, conv_w=v_conv_w,
                conv_b=v_conv_b, conv_ln_g=v_conv_ln_g, conv_ln_b=v_conv_ln_b, w_attn_o=v_w_attn_o, w_conv_o=v_w_conv_o,
                b_conv_o=v_b_conv_o, w_out=v_w_out, ffn_norm_g=v_ffn_norm_g, w_gate_up=v_w_gate_up, w_down=v_w_down,
                final_norm_g=v_final_norm_g)
    names = list(weights)
    grads, delta, new_m, new_v = {}, {}, {}, {}
    tok = in_flight["w_in"][2]
    for n in ("w_down", "w_gate_up", "w_out", "w_attn_o", "w_conv_o", "w_in"):
        send_sem, recv_sem, ps, rx = in_flight[n]
        ps, rx = _chip_exchange_wait(send_sem, recv_sem, ps, rx, tok, "rs_" + n + "_chip_wait")
        oriented = (lambda a: a[0].T) if n in ("w_in", "w_gate_up") else (lambda a: a[0])
        back = (lambda a: a.T[None]) if n in ("w_in", "w_gate_up") else (lambda a: a[None])
        if n in ("w_attn_o", "w_conv_o"):
            g = _sum_chips(ps, rx, "rs_" + n + "_sum").T
            dl, nm, nv = _adamw(oriented(weights[n]), g, oriented(m_in[n]), oriented(v_in[n]), "adamw_" + n)
        else:
            g, dl, nm, nv = _sum_chips_adamw(ps, rx, oriented(weights[n]), oriented(m_in[n]), oriented(v_in[n]), "rs_" + n + "_sum_adamw")
        grads[n], delta[n], new_m[n], new_v[n] = back(g), back(dl), back(nm), back(nv)
        tok = dl

    slab, slab_layout = _pack([loss_part[:, :1], d_mix_g, d_b_in, d_sinks[:, :N_Q_HEADS], d_conv_b, d_ln_g, d_ln_b, d_bco,
                               d_ffn_g, d_final_g, d_conv_w, d_meta])
    slab_all = _all_gather_rows(slab, "gather_small_grads", after=tok).reshape(N_DEV, *slab.shape)
    (loss, g_mix_g, g_b_in, g_sinks, g_conv_b, g_ln_g, g_ln_b, g_bco, g_ffn_g, g_final_g, g_conv_w_full, g_meta_full
     ) = _unpack(_sum_blocks(slab_all, "sum_small_grads"), slab_layout)
    g_conv_w = lax.dynamic_slice(g_conv_w_full, (0, dev * cw_cols), (CONV_WIDTH, cw_cols)).reshape(conv_w.shape)
    g_meta = lax.dynamic_slice(g_meta_full, (0, dev * meta_cols), (N_META, meta_cols))
    g_final_g = g_final_g.reshape(final_norm_g.shape)
    grads.update(meta_tokens=g_meta, mix_norm_g=g_mix_g, b_in=g_b_in, attn_sinks=g_sinks, conv_w=g_conv_w, conv_b=g_conv_b,
                 conv_ln_g=g_ln_g, conv_ln_b=g_ln_b, b_conv_o=g_bco, ffn_norm_g=g_ffn_g, final_norm_g=g_final_g)
    rest = [n for n in names if n not in delta]
    w_slab, rest_layout = _pack([weights[n] for n in rest])
    g_slab, _ = _pack([grads[n] for n in rest])
    m_slab, _ = _pack([m_in[n] for n in rest])
    v_slab, _ = _pack([v_in[n] for n in rest])
    dl, nm, nv = _adamw(w_slab, g_slab, m_slab, v_slab, "adamw_small")
    for n, a, b, c in zip(rest, _unpack(dl, rest_layout), _unpack(nm, rest_layout), _unpack(nv, rest_layout)):
        delta[n], new_m[n], new_v[n] = a, b, c

    return (loss.reshape(()), grad_x[None], *[grads[n] for n in names], *[delta[n] for n in names],
            *[new_m[n] for n in names], *[new_v[n] for n in names])
```
